```python
import math
import jax, jax.numpy as jnp
from jax import lax
import numpy as np

D_MODEL = 1024
BATCH = 8
SEQ = 8192
DEPTH = 1

GRID_W = 64
CTX_LEN = 256

D_INNER = 2 * D_MODEL
HEAD_DIM = 64
N_HEADS = D_INNER // HEAD_DIM
N_GROUPS = 8
HPG = N_HEADS // N_GROUPS
D_STATE = 128
SSM_CONV = 4
SSM_PAD = (2, 1)
CHUNK = 128

D_CONF = D_MODEL
CONF_KERNEL = 31
CONF_PAD = (CONF_KERNEL // 2, CONF_KERNEL // 2)

EPS = 1e-6

GN = N_GROUPS * D_STATE
X_END = D_INNER
B_END = X_END + GN
C_END = B_END + GN
DT_END = C_END + 2 * N_HEADS
Z_END = DT_END + D_INNER
GLU_END = Z_END + 2 * D_CONF
CG_END = GLU_END + D_CONF
IN_COLS = CG_END + 2 * D_MODEL

kernel_name = 'hybrid_ssd_conformer_prefix_block'


def rms_norm(x, w):
    xf = x.astype(jnp.float32)
    y = xf * lax.rsqrt(jnp.mean(xf * xf, axis=-1, keepdims=True) + EPS)
    return y.astype(x.dtype) * w


def group_rms_norm(x, w):
    shp = x.shape
    xg = x.reshape(*shp[:-1], N_GROUPS, shp[-1] // N_GROUPS).astype(jnp.float32)
    y = xg * lax.rsqrt(jnp.mean(xg * xg, axis=-1, keepdims=True) + EPS)
    return y.reshape(shp).astype(x.dtype) * w


def layer_norm(x, w, b):
    xf = x.astype(jnp.float32)
    mu = jnp.mean(xf, axis=-1, keepdims=True)
    var = jnp.mean(jnp.square(xf - mu), axis=-1, keepdims=True)
    return ((xf - mu) * lax.rsqrt(var + EPS)).astype(x.dtype) * w + b


def modulate(h, shift, scale):
    return h * (1 + scale) + shift


def depthwise_conv(u, w, b, pad):
    out = lax.conv_general_dilated(u, w[:, None, :], (1,), [pad],
                                   dimension_numbers=('NWC', 'WIO', 'NWC'),
                                   feature_group_count=u.shape[-1])
    return out + b


def rev(t):
    return jnp.flip(t, axis=1)


def to_chunks(t):
    return t.reshape(t.shape[0], t.shape[1] // CHUNK, CHUNK, *t.shape[2:])


def ssm_dt(dt_raw, dt_bias):
    b, L = dt_raw.shape[:2]
    return jax.nn.softplus(dt_raw.astype(jnp.float32).reshape(b, L, 2, N_GROUPS, HPG)
                           + dt_bias.astype(jnp.float32).reshape(2, N_GROUPS, HPG))


def ssm_decay(a_log):
    return -jnp.exp(a_log.astype(jnp.float32)).reshape(2, N_GROUPS, HPG)


def chunk_states(xs, dt, a, bm, h0):
    la = jnp.cumsum(dt * a, axis=2)
    w_end = jnp.exp(la[:, :, -1:] - la) * dt
    contrib = jnp.einsum('bcsgn,bcsgrp->bcgrpn', bm, xs * w_end[..., None])
    chunk_decay = jnp.exp(la[:, :, -1])

    def step(h, inp):
        s, d = inp
        return h * d[..., None, None] + s, h

    h_last, h_prev = lax.scan(step, h0, (jnp.moveaxis(contrib, 1, 0), jnp.moveaxis(chunk_decay, 1, 0)))
    return la, jnp.moveaxis(h_prev, 0, 1), h_last


def ssd_scan(xs, dt, a, bm, cm, h0):
    xs_c, dt_c, bm_c, cm_c = [to_chunks(t.astype(jnp.float32)) for t in (xs, dt, bm, cm)]
    la, h_prev, h_last = chunk_states(xs_c, dt_c, a, bm_c, h0)
    idx = jnp.arange(CHUNK)
    order = (idx[:, None] >= idx[None, :])[None, None, :, :, None, None]
    seg = la[:, :, :, None] - la[:, :, None, :]
    decay = jnp.exp(jnp.where(order, seg, -jnp.inf))
    scores = jnp.einsum('bclgn,bcsgn->bclsg', cm_c, bm_c)
    mix = scores[..., None] * decay * dt_c[:, :, None]
    y_diag = jnp.einsum('bclsgr,bcsgrp->bclgrp', mix, xs_c)
    y_off = jnp.einsum('bclgn,bcgrpn->bclgrp', cm_c, h_prev) * jnp.exp(la)[..., None]
    return (y_diag + y_off).reshape(xs.shape), h_last


def ssd_final_state(xs, dt, a, bm, h0):
    xs_c, dt_c, bm_c = [to_chunks(t.astype(jnp.float32)) for t in (xs, dt, bm)]
    _, _, h_last = chunk_states(xs_c, dt_c, a, bm_c, h0)
    return h_last


def context_states(h_ctx, w_in, ssm_conv_w, ssm_conv_b, dt_bias, a_log, h0):
    b, L, _ = h_ctx.shape
    xb = jax.nn.silu(depthwise_conv(h_ctx @ w_in[:, :B_END], ssm_conv_w[:, :B_END], ssm_conv_b[:B_END], SSM_PAD))
    xs = xb[..., :X_END].reshape(b, L, N_GROUPS, HPG, HEAD_DIM)
    bm = xb[..., X_END:B_END].reshape(b, L, N_GROUPS, D_STATE)
    dt = ssm_dt(h_ctx @ w_in[:, C_END:DT_END], dt_bias)
    a = ssm_decay(a_log)
    h_f = ssd_final_state(xs, dt[:, :, 0], a[0], bm, h0)
    h_b = ssd_final_state(rev(xs), rev(dt[:, :, 1]), a[1], rev(bm), h0)
    return h_f, h_b


def mixer(h, p, h0_f, h0_b, rows):
    b, L, _ = h.shape
    proj = h @ p['w_in']
    xbc = jax.nn.silu(depthwise_conv(proj[..., :C_END], p['ssm_conv_w'], p['ssm_conv_b'], SSM_PAD))
    xs = xbc[..., :X_END].reshape(b, L, N_GROUPS, HPG, HEAD_DIM)
    bm = xbc[..., X_END:B_END].reshape(b, L, N_GROUPS, D_STATE)
    cm = xbc[..., B_END:C_END].reshape(b, L, N_GROUPS, D_STATE)
    dt = ssm_dt(proj[..., C_END:DT_END], p['dt_bias'])
    a = ssm_decay(p['a_log'])
    y_f, h_f = ssd_scan(xs, dt[:, :, 0], a[0], bm, cm, h0_f)
    y_b, h_b = ssd_scan(rev(xs), rev(dt[:, :, 1]), a[1], rev(bm), rev(cm), h0_b)
    y = (y_f + rev(y_b)).astype(h.dtype) + p['d_skip'].reshape(N_GROUPS, HPG, 1) * xs
    y = y.reshape(b, L, D_INNER) * jax.nn.silu(proj[..., DT_END:Z_END])
    branch_ssm = group_rms_norm(y, p['ssm_norm_w']) @ p['w_out_ssm']
    glu = proj[..., Z_END:GLU_END]
    u = glu[..., :D_CONF] * jax.nn.sigmoid(glu[..., D_CONF:])
    if rows is not None:
        u = u.reshape(b * rows, GRID_W, D_CONF)
    u = depthwise_conv(u, p['conf_conv_w'], p['conf_conv_b'], CONF_PAD).reshape(b, L, D_CONF)
    u = jax.nn.silu(layer_norm(u, p['conf_ln_w'], p['conf_ln_b'])) * jax.nn.silu(proj[..., GLU_END:CG_END])
    branch_conf = u @ p['w_out_conf']
    g = jax.nn.sigmoid(proj[..., CG_END:])
    merged = g[..., :D_MODEL] * branch_ssm + g[..., D_MODEL:] * branch_conf
    return merged @ p['w_out'], h_f, h_b


def _fwd_setup_inputs(seed: int = 0) -> dict:
    key = jax.random.key(seed)
    ks = jax.random.split(key, 24)
    f32 = jnp.float32

    def nrm(k, shape, s):
        return jax.random.normal(k, shape, f32) * s

    dt0 = jnp.exp(jax.random.uniform(ks[10], (DEPTH, 2, N_HEADS), f32, math.log(1e-3), math.log(1e-1)))
    return {
        'x': nrm(ks[0], (BATCH, SEQ, D_MODEL), 1.0),
        'c': nrm(ks[1], (BATCH, D_MODEL), 1.0),
        'ctx': nrm(ks[2], (BATCH, CTX_LEN, D_MODEL), 1.0),
        'c_ctx': nrm(ks[3], (D_MODEL,), 1.0),
        'w_mod': nrm(ks[4], (DEPTH, D_MODEL, 3 * D_MODEL), D_MODEL ** -0.5),
        'b_mod': nrm(ks[5], (DEPTH, 3 * D_MODEL), 0.01),
        'norm_w': 1.0 + nrm(ks[6], (DEPTH, D_MODEL), 0.01),
        'w_in': nrm(ks[7], (DEPTH, D_MODEL, IN_COLS), D_MODEL ** -0.5),
        'ssm_conv_w': nrm(ks[8], (DEPTH, SSM_CONV, C_END), SSM_CONV ** -0.5),
        'ssm_conv_b': nrm(ks[9], (DEPTH, C_END), 0.01),
        'dt_bias': dt0 + jnp.log(-jnp.expm1(-dt0)),
        'a_log': jnp.log(jax.random.uniform(ks[11], (DEPTH, 2, N_HEADS), f32, 1.0, 16.0)),
        'd_skip': 1.0 + nrm(ks[12], (DEPTH, N_HEADS), 0.01),
        'ssm_norm_w': 1.0 + nrm(ks[13], (DEPTH, D_INNER), 0.01),
        'w_out_ssm': nrm(ks[14], (DEPTH, D_INNER, D_MODEL), D_INNER ** -0.5),
        'conf_conv_w': nrm(ks[15], (DEPTH, CONF_KERNEL, D_CONF), CONF_KERNEL ** -0.5),
        'conf_conv_b': nrm(ks[16], (DEPTH, D_CONF), 0.01),
        'conf_ln_w': 1.0 + nrm(ks[17], (DEPTH, D_CONF), 0.01),
        'conf_ln_b': nrm(ks[18], (DEPTH, D_CONF), 0.01),
        'w_out_conf': nrm(ks[19], (DEPTH, D_CONF, D_MODEL), D_CONF ** -0.5),
        'w_out': nrm(ks[20], (DEPTH, D_MODEL, D_MODEL), D_MODEL ** -0.5),
        'final_norm_w': 1.0 + nrm(ks[21], (D_MODEL,), 0.01),
    }


def _fwd_reference(x, c, ctx, c_ctx, w_mod, b_mod, norm_w, w_in, ssm_conv_w, ssm_conv_b, dt_bias, a_log,
              d_skip, ssm_norm_w, w_out_ssm, conf_conv_w, conf_conv_b, conf_ln_w, conf_ln_b,
              w_out_conf, w_out, final_norm_w):
    rows = x.shape[1] // GRID_W
    h0 = jnp.zeros((ctx.shape[0], N_GROUPS, HPG, HEAD_DIM, D_STATE), jnp.float32)
    for i in range(DEPTH):
        p = {'w_in': w_in[i], 'ssm_conv_w': ssm_conv_w[i], 'ssm_conv_b': ssm_conv_b[i],
             'dt_bias': dt_bias[i], 'a_log': a_log[i], 'd_skip': d_skip[i], 'ssm_norm_w': ssm_norm_w[i],
             'w_out_ssm': w_out_ssm[i], 'conf_conv_w': conf_conv_w[i], 'conf_conv_b': conf_conv_b[i],
             'conf_ln_w': conf_ln_w[i], 'conf_ln_b': conf_ln_b[i], 'w_out_conf': w_out_conf[i],
             'w_out': w_out[i]}
        mod_x = jax.nn.silu(c) @ w_mod[i] + b_mod[i]
        mod_c = jax.nn.silu(c_ctx) @ w_mod[i] + b_mod[i]
        shift_x, scale_x, gate_x = jnp.split(mod_x[:, None, :], 3, axis=-1)
        shift_c, scale_c, gate_c = jnp.split(mod_c, 3)
        h_ctx = modulate(rms_norm(ctx, norm_w[i]), shift_c, scale_c)
        if i < DEPTH - 1:
            ctx_out, h_f, h_b = mixer(h_ctx, p, h0, h0, None)
        else:
            h_f, h_b = context_states(h_ctx, p['w_in'], p['ssm_conv_w'], p['ssm_conv_b'],
                                      p['dt_bias'], p['a_log'], h0)
        h = modulate(rms_norm(x, norm_w[i]), shift_x, scale_x)
        out, _, _ = mixer(h, p, h_f, h_b, rows)
        x = x + gate_x * out
        if i < DEPTH - 1:
            ctx = ctx + gate_c * ctx_out
    return rms_norm(x, final_norm_w)


import jax as _jax
import jax.numpy as _jnp

TWIN_FORMAT = 'train_step'
FWD_PARAMS = ['x', 'c', 'ctx', 'c_ctx', 'w_mod', 'b_mod', 'norm_w', 'w_in', 'ssm_conv_w', 'ssm_conv_b', 'dt_bias', 'a_log', 'd_skip', 'ssm_norm_w', 'w_out_ssm', 'conf_conv_w', 'conf_conv_b', 'conf_ln_w', 'conf_ln_b', 'w_out_conf', 'w_out', 'final_norm_w']
TWIN_WEIGHTS = ['c_ctx', 'w_mod', 'b_mod', 'norm_w', 'w_in', 'ssm_conv_w', 'ssm_conv_b', 'dt_bias', 'a_log', 'd_skip', 'ssm_norm_w', 'w_out_ssm', 'conf_conv_w', 'conf_conv_b', 'conf_ln_w', 'conf_ln_b', 'w_out_conf', 'w_out', 'final_norm_w']
TWIN_DIFF_INPUT = 'x'
TWIN_INPUTS = ['x', 'c', 'ctx', 'c_ctx', 'w_mod', 'b_mod', 'norm_w', 'w_in', 'ssm_conv_w', 'ssm_conv_b', 'dt_bias', 'a_log', 'd_skip', 'ssm_norm_w', 'w_out_ssm', 'conf_conv_w', 'conf_conv_b', 'conf_ln_w', 'conf_ln_b', 'w_out_conf', 'w_out', 'final_norm_w', 'loss_target', 'm_c_ctx', 'm_w_mod', 'm_b_mod', 'm_norm_w', 'm_w_in', 'm_ssm_conv_w', 'm_ssm_conv_b', 'm_dt_bias', 'm_a_log', 'm_d_skip', 'm_ssm_norm_w', 'm_w_out_ssm', 'm_conf_conv_w', 'm_conf_conv_b', 'm_conf_ln_w', 'm_conf_ln_b', 'm_w_out_conf', 'm_w_out', 'm_final_norm_w', 'v_c_ctx', 'v_w_mod', 'v_b_mod', 'v_norm_w', 'v_w_in', 'v_ssm_conv_w', 'v_ssm_conv_b', 'v_dt_bias', 'v_a_log', 'v_d_skip', 'v_ssm_norm_w', 'v_w_out_ssm', 'v_conf_conv_w', 'v_conf_conv_b', 'v_conf_ln_w', 'v_conf_ln_b', 'v_w_out_conf', 'v_w_out', 'v_final_norm_w']
TWIN_OUTPUTS = ['loss', 'grad_x', 'grad_c_ctx', 'grad_w_mod', 'grad_b_mod', 'grad_norm_w', 'grad_w_in', 'grad_ssm_conv_w', 'grad_ssm_conv_b', 'grad_dt_bias', 'grad_a_log', 'grad_d_skip', 'grad_ssm_norm_w', 'grad_w_out_ssm', 'grad_conf_conv_w', 'grad_conf_conv_b', 'grad_conf_ln_w', 'grad_conf_ln_b', 'grad_w_out_conf', 'grad_w_out', 'grad_final_norm_w', 'delta_c_ctx', 'delta_w_mod', 'delta_b_mod', 'delta_norm_w', 'delta_w_in', 'delta_ssm_conv_w', 'delta_ssm_conv_b', 'delta_dt_bias', 'delta_a_log', 'delta_d_skip', 'delta_ssm_norm_w', 'delta_w_out_ssm', 'delta_conf_conv_w', 'delta_conf_conv_b', 'delta_conf_ln_w', 'delta_conf_ln_b', 'delta_w_out_conf', 'delta_w_out', 'delta_final_norm_w', 'new_m_c_ctx', 'new_m_w_mod', 'new_m_b_mod', 'new_m_norm_w', 'new_m_w_in', 'new_m_ssm_conv_w', 'new_m_ssm_conv_b', 'new_m_dt_bias', 'new_m_a_log', 'new_m_d_skip', 'new_m_ssm_norm_w', 'new_m_w_out_ssm', 'new_m_conf_conv_w', 'new_m_conf_conv_b', 'new_m_conf_ln_w', 'new_m_conf_ln_b', 'new_m_w_out_conf', 'new_m_w_out', 'new_m_final_norm_w', 'new_v_c_ctx', 'new_v_w_mod', 'new_v_b_mod', 'new_v_norm_w', 'new_v_w_in', 'new_v_ssm_conv_w', 'new_v_ssm_conv_b', 'new_v_dt_bias', 'new_v_a_log', 'new_v_d_skip', 'new_v_ssm_norm_w', 'new_v_w_out_ssm', 'new_v_conf_conv_w', 'new_v_conf_conv_b', 'new_v_conf_ln_w', 'new_v_conf_ln_b', 'new_v_w_out_conf', 'new_v_w_out', 'new_v_final_norm_w']
TWIN_LEAF_KINDS = {'loss': 'loss', 'grad_x': 'grad_x', 'grad_c_ctx': 'grad_w', 'grad_w_mod': 'grad_w', 'grad_b_mod': 'grad_w', 'grad_norm_w': 'grad_w', 'grad_w_in': 'grad_w', 'grad_ssm_conv_w': 'grad_w', 'grad_ssm_conv_b': 'grad_w', 'grad_dt_bias': 'grad_w', 'grad_a_log': 'grad_w', 'grad_d_skip': 'grad_w', 'grad_ssm_norm_w': 'grad_w', 'grad_w_out_ssm': 'grad_w', 'grad_conf_conv_w': 'grad_w', 'grad_conf_conv_b': 'grad_w', 'grad_conf_ln_w': 'grad_w', 'grad_conf_ln_b': 'grad_w', 'grad_w_out_conf': 'grad_w', 'grad_w_out': 'grad_w', 'grad_final_norm_w': 'grad_w', 'delta_c_ctx': 'delta_w', 'delta_w_mod': 'delta_w', 'delta_b_mod': 'delta_w', 'delta_norm_w': 'delta_w', 'delta_w_in': 'delta_w', 'delta_ssm_conv_w': 'delta_w', 'delta_ssm_conv_b': 'delta_w', 'delta_dt_bias': 'delta_w', 'delta_a_log': 'delta_w', 'delta_d_skip': 'delta_w', 'delta_ssm_norm_w': 'delta_w', 'delta_w_out_ssm': 'delta_w', 'delta_conf_conv_w': 'delta_w', 'delta_conf_conv_b': 'delta_w', 'delta_conf_ln_w': 'delta_w', 'delta_conf_ln_b': 'delta_w', 'delta_w_out_conf': 'delta_w', 'delta_w_out': 'delta_w', 'delta_final_norm_w': 'delta_w', 'new_m_c_ctx': 'new_m', 'new_m_w_mod': 'new_m', 'new_m_b_mod': 'new_m', 'new_m_norm_w': 'new_m', 'new_m_w_in': 'new_m', 'new_m_ssm_conv_w': 'new_m', 'new_m_ssm_conv_b': 'new_m', 'new_m_dt_bias': 'new_m', 'new_m_a_log': 'new_m', 'new_m_d_skip': 'new_m', 'new_m_ssm_norm_w': 'new_m', 'new_m_w_out_ssm': 'new_m', 'new_m_conf_conv_w': 'new_m', 'new_m_conf_conv_b': 'new_m', 'new_m_conf_ln_w': 'new_m', 'new_m_conf_ln_b': 'new_m', 'new_m_w_out_conf': 'new_m', 'new_m_w_out': 'new_m', 'new_m_final_norm_w': 'new_m', 'new_v_c_ctx': 'new_v', 'new_v_w_mod': 'new_v', 'new_v_b_mod': 'new_v', 'new_v_norm_w': 'new_v', 'new_v_w_in': 'new_v', 'new_v_ssm_conv_w': 'new_v', 'new_v_ssm_conv_b': 'new_v', 'new_v_dt_bias': 'new_v', 'new_v_a_log': 'new_v', 'new_v_d_skip': 'new_v', 'new_v_ssm_norm_w': 'new_v', 'new_v_w_out_ssm': 'new_v', 'new_v_conf_conv_w': 'new_v', 'new_v_conf_conv_b': 'new_v', 'new_v_conf_ln_w': 'new_v', 'new_v_conf_ln_b': 'new_v', 'new_v_w_out_conf': 'new_v', 'new_v_w_out': 'new_v', 'new_v_final_norm_w': 'new_v'}


def _forward(args):
    return _fwd_reference(*[args[k] for k in FWD_PARAMS])


def _output_shape():
    def fwd():
        inp = _fwd_setup_inputs(0)
        return _fwd_reference(*[inp[k] for k in FWD_PARAMS])
    out = _jax.eval_shape(fwd)
    return out.shape, out.dtype

N_MICROBATCH = 1
ADAM_LR = 0.001
ADAM_B1 = 0.9
ADAM_B2 = 0.999
ADAM_EPS = 1e-08
ADAM_WD = 0.01
ADAM_STEP = 10
PER_EXAMPLE_BATCH_AXIS = {'x': 0, 'c': 0, 'ctx': 0, 'loss_target': 0}
SHARED_INPUTS = []
_WEIGHT_DTYPES = {'c_ctx': _jnp.float32, 'w_mod': _jnp.float32, 'b_mod': _jnp.float32, 'norm_w': _jnp.float32, 'w_in': _jnp.float32, 'ssm_conv_w': _jnp.float32, 'ssm_conv_b': _jnp.float32, 'dt_bias': _jnp.float32, 'a_log': _jnp.float32, 'd_skip': _jnp.float32, 'ssm_norm_w': _jnp.float32, 'w_out_ssm': _jnp.float32, 'conf_conv_w': _jnp.float32, 'conf_conv_b': _jnp.float32, 'conf_ln_w': _jnp.float32, 'conf_ln_b': _jnp.float32, 'w_out_conf': _jnp.float32, 'w_out': _jnp.float32, 'final_norm_w': _jnp.float32}
MOMENT_SCALE = {'c_ctx': 1.276687e-02, 'w_mod': 8.056746e-02, 'b_mod': 1.350982e-01, 'norm_w': 1.232741e-01, 'w_in': 4.290441e-02, 'ssm_conv_w': 4.721872e-02, 'ssm_conv_b': 4.647768e-02, 'dt_bias': 1.041627e-01, 'a_log': 2.480180e-01, 'd_skip': 2.194659e-01, 'ssm_norm_w': 6.509175e-02, 'w_out_ssm': 7.984895e-02, 'conf_conv_w': 4.099794e-02, 'conf_conv_b': 6.086420e-02, 'conf_ln_w': 4.468915e-02, 'conf_ln_b': 4.080860e-02, 'w_out_conf': 3.979671e-02, 'w_out': 8.963504e-02, 'final_norm_w': 6.410683e+01}


def _to_microbatches(a, axis):
    t = _jnp.moveaxis(a, axis, 0)
    t = t.reshape((N_MICROBATCH, t.shape[0] // N_MICROBATCH) + t.shape[1:])
    return _jnp.moveaxis(t, 1, axis + 1)


def setup_inputs(seed: int = 0) -> dict:
    inp = _fwd_setup_inputs(seed)
    key = _jax.random.fold_in(_jax.random.key(seed), 7919)
    shape, _ = _output_shape()
    out = dict(inp)
    out["loss_target"] = _jax.random.normal(_jax.random.fold_in(key, 0), shape, _jnp.float32)
    for i, name in enumerate(TWIN_WEIGHTS):
        w = inp[name].astype(_jnp.float32)
        if MOMENT_SCALE is None:
            s = _jnp.sqrt(_jnp.mean(_jnp.square(w)) + 1e-30)
        else:
            s = MOMENT_SCALE[name]
        km, kv = _jax.random.split(_jax.random.fold_in(key, i + 1))
        out[name] = w
        out["m_" + name] = s * _jax.random.normal(km, w.shape, _jnp.float32)
        out["v_" + name] = (s * s) * _jax.random.uniform(kv, w.shape, _jnp.float32, 0.5, 1.5)
    if N_MICROBATCH > 1:
        for name, axis in PER_EXAMPLE_BATCH_AXIS.items():
            out[name] = _to_microbatches(out[name], axis)
    return {'x': out['x'], 'c': out['c'], 'ctx': out['ctx'], 'c_ctx': out['c_ctx'], 'w_mod': out['w_mod'], 'b_mod': out['b_mod'], 'norm_w': out['norm_w'], 'w_in': out['w_in'], 'ssm_conv_w': out['ssm_conv_w'], 'ssm_conv_b': out['ssm_conv_b'], 'dt_bias': out['dt_bias'], 'a_log': out['a_log'], 'd_skip': out['d_skip'], 'ssm_norm_w': out['ssm_norm_w'], 'w_out_ssm': out['w_out_ssm'], 'conf_conv_w': out['conf_conv_w'], 'conf_conv_b': out['conf_conv_b'], 'conf_ln_w': out['conf_ln_w'], 'conf_ln_b': out['conf_ln_b'], 'w_out_conf': out['w_out_conf'], 'w_out': out['w_out'], 'final_norm_w': out['final_norm_w'], 'loss_target': out['loss_target'], 'm_c_ctx': out['m_c_ctx'], 'm_w_mod': out['m_w_mod'], 'm_b_mod': out['m_b_mod'], 'm_norm_w': out['m_norm_w'], 'm_w_in': out['m_w_in'], 'm_ssm_conv_w': out['m_ssm_conv_w'], 'm_ssm_conv_b': out['m_ssm_conv_b'], 'm_dt_bias': out['m_dt_bias'], 'm_a_log': out['m_a_log'], 'm_d_skip': out['m_d_skip'], 'm_ssm_norm_w': out['m_ssm_norm_w'], 'm_w_out_ssm': out['m_w_out_ssm'], 'm_conf_conv_w': out['m_conf_conv_w'], 'm_conf_conv_b': out['m_conf_conv_b'], 'm_conf_ln_w': out['m_conf_ln_w'], 'm_conf_ln_b': out['m_conf_ln_b'], 'm_w_out_conf': out['m_w_out_conf'], 'm_w_out': out['m_w_out'], 'm_final_norm_w': out['m_final_norm_w'], 'v_c_ctx': out['v_c_ctx'], 'v_w_mod': out['v_w_mod'], 'v_b_mod': out['v_b_mod'], 'v_norm_w': out['v_norm_w'], 'v_w_in': out['v_w_in'], 'v_ssm_conv_w': out['v_ssm_conv_w'], 'v_ssm_conv_b': out['v_ssm_conv_b'], 'v_dt_bias': out['v_dt_bias'], 'v_a_log': out['v_a_log'], 'v_d_skip': out['v_d_skip'], 'v_ssm_norm_w': out['v_ssm_norm_w'], 'v_w_out_ssm': out['v_w_out_ssm'], 'v_conf_conv_w': out['v_conf_conv_w'], 'v_conf_conv_b': out['v_conf_conv_b'], 'v_conf_ln_w': out['v_conf_ln_w'], 'v_conf_ln_b': out['v_conf_ln_b'], 'v_w_out_conf': out['v_w_out_conf'], 'v_w_out': out['v_w_out'], 'v_final_norm_w': out['v_final_norm_w']}


def _loss(weights, diff, rest, loss_target):
    with _jax.named_scope("forward"):
        args = {**rest, TWIN_DIFF_INPUT: diff, **{k: w.astype(_WEIGHT_DTYPES[k]) for k, w in weights.items()}}
        y = _forward(args)
    with _jax.named_scope("loss_head"):
        err = _jnp.square(y.astype(_jnp.float32) - loss_target)
        return 0.5 * _jnp.sum(_jnp.mean(err, axis=-1)) if err.ndim else 0.5 * err


def _adamw(w, g, m, v):
    m = ADAM_B1 * m + (1.0 - ADAM_B1) * g
    v = ADAM_B2 * v + (1.0 - ADAM_B2) * _jnp.square(g)
    m_hat = m / (1.0 - ADAM_B1 ** ADAM_STEP)
    v_hat = v / (1.0 - ADAM_B2 ** ADAM_STEP)
    delta = -ADAM_LR * (m_hat / (_jnp.sqrt(v_hat) + ADAM_EPS) + ADAM_WD * w)
    return delta, m, v


def reference(x, c, ctx, c_ctx, w_mod, b_mod, norm_w, w_in, ssm_conv_w, ssm_conv_b, dt_bias, a_log, d_skip, ssm_norm_w, w_out_ssm, conf_conv_w, conf_conv_b, conf_ln_w, conf_ln_b, w_out_conf, w_out, final_norm_w, loss_target, m_c_ctx, m_w_mod, m_b_mod, m_norm_w, m_w_in, m_ssm_conv_w, m_ssm_conv_b, m_dt_bias, m_a_log, m_d_skip, m_ssm_norm_w, m_w_out_ssm, m_conf_conv_w, m_conf_conv_b, m_conf_ln_w, m_conf_ln_b, m_w_out_conf, m_w_out, m_final_norm_w, v_c_ctx, v_w_mod, v_b_mod, v_norm_w, v_w_in, v_ssm_conv_w, v_ssm_conv_b, v_dt_bias, v_a_log, v_d_skip, v_ssm_norm_w, v_w_out_ssm, v_conf_conv_w, v_conf_conv_b, v_conf_ln_w, v_conf_ln_b, v_w_out_conf, v_w_out, v_final_norm_w):
    given = dict(x=x, c=c, ctx=ctx, c_ctx=c_ctx, w_mod=w_mod, b_mod=b_mod, norm_w=norm_w, w_in=w_in, ssm_conv_w=ssm_conv_w, ssm_conv_b=ssm_conv_b, dt_bias=dt_bias, a_log=a_log, d_skip=d_skip, ssm_norm_w=ssm_norm_w, w_out_ssm=w_out_ssm, conf_conv_w=conf_conv_w, conf_conv_b=conf_conv_b, conf_ln_w=conf_ln_w, conf_ln_b=conf_ln_b, w_out_conf=w_out_conf, w_out=w_out, final_norm_w=final_norm_w, loss_target=loss_target, m_c_ctx=m_c_ctx, m_w_mod=m_w_mod, m_b_mod=m_b_mod, m_norm_w=m_norm_w, m_w_in=m_w_in, m_ssm_conv_w=m_ssm_conv_w, m_ssm_conv_b=m_ssm_conv_b, m_dt_bias=m_dt_bias, m_a_log=m_a_log, m_d_skip=m_d_skip, m_ssm_norm_w=m_ssm_norm_w, m_w_out_ssm=m_w_out_ssm, m_conf_conv_w=m_conf_conv_w, m_conf_conv_b=m_conf_conv_b, m_conf_ln_w=m_conf_ln_w, m_conf_ln_b=m_conf_ln_b, m_w_out_conf=m_w_out_conf, m_w_out=m_w_out, m_final_norm_w=m_final_norm_w, v_c_ctx=v_c_ctx, v_w_mod=v_w_mod, v_b_mod=v_b_mod, v_norm_w=v_norm_w, v_w_in=v_w_in, v_ssm_conv_w=v_ssm_conv_w, v_ssm_conv_b=v_ssm_conv_b, v_dt_bias=v_dt_bias, v_a_log=v_a_log, v_d_skip=v_d_skip, v_ssm_norm_w=v_ssm_norm_w, v_w_out_ssm=v_w_out_ssm, v_conf_conv_w=v_conf_conv_w, v_conf_conv_b=v_conf_conv_b, v_conf_ln_w=v_conf_ln_w, v_conf_ln_b=v_conf_ln_b, v_w_out_conf=v_w_out_conf, v_w_out=v_w_out, v_final_norm_w=v_final_norm_w)
    weights = {n: given[n] for n in TWIN_WEIGHTS}
    shared = {n: given[n] for n in SHARED_INPUTS}
    per_example = {n: given[n] for n in ['x', 'c', 'ctx']}
    grad_fn = _jax.value_and_grad(_loss, argnums=(0, 1))

    def one_microbatch(ex, loss_target):
        ex = dict(ex)
        diff = ex.pop(TWIN_DIFF_INPUT)
        return grad_fn(weights, diff, {**shared, **ex}, loss_target)

    if N_MICROBATCH == 1:
        loss, (grad_w, grad_x) = one_microbatch(per_example, given["loss_target"])
    else:
        def body(carry, xs):
            loss_sum, grad_sum = carry
            l_k, (gw_k, gx_k) = one_microbatch(xs[0], xs[1])
            with _jax.named_scope("update"):
                return (loss_sum + l_k, _jax.tree.map(_jnp.add, grad_sum, gw_k)), gx_k

        init = (_jnp.zeros((), _jnp.float32), _jax.tree.map(_jnp.zeros_like, weights))
        (loss, grad_w), grad_x = _jax.lax.scan(body, init, (per_example, given["loss_target"]))
    with _jax.named_scope("update"):
        delta_w, new_m, new_v = {}, {}, {}
        for n in TWIN_WEIGHTS:
            delta_w[n], new_m[n], new_v[n] = _adamw(weights[n], grad_w[n], given["m_" + n], given["v_" + n])
    return (loss, grad_x, *[grad_w[n] for n in TWIN_WEIGHTS], *[delta_w[n] for n in TWIN_WEIGHTS],
            *[new_m[n] for n in TWIN_WEIGHTS], *[new_v[n] for n in TWIN_WEIGHTS])
```

```python
import jax
import jax.numpy as jnp
from jax import lax
from jax.experimental import pallas as pl
from jax.experimental.pallas import tpu as pltpu

f32 = jnp.float32
bf16 = jnp.bfloat16

D = 1024
DI = 2048
NG = 8
HPG = 4
HD = 64
NS = 128
NH = 32
Q = 128
GRID_W = 64
CK = 31
SK = 4
EPS = 1e-6
RT = 256
N_DEV = 8
IN_COLS = 11328
X0, B0, C0, Z0, G10, G20, DT0, GV0, GG0, CG0, NP = 0, 2048, 3072, 4096, 6144, 7168, 8192, 9216, 10240, 11264, 12288
VMEM_LIMIT = 50 * 1024 * 1024
NEG = -1e30

ADAM_LR, ADAM_B1, ADAM_B2, ADAM_EPS, ADAM_WD, ADAM_STEP = 0.001, 0.9, 0.999, 1e-08, 0.01, 10

MESH = pl.DeviceIdType.MESH
S = jax.ShapeDtypeStruct


def _params(*sem):
    return pltpu.CompilerParams(dimension_semantics=tuple(sem) if sem else None, vmem_limit_bytes=VMEM_LIMIT)


def _sig(x):
    return 1.0 / (1.0 + jnp.exp(-x))


def _silu(x):
    return x * _sig(x)


def _dsilu(x, s):
    return s * (1.0 + x * (1.0 - s))


def _dot(a, b):
    return jnp.dot(a, b, preferred_element_type=f32)


def _dot_nt(a, b):
    return lax.dot_general(a, b, (((1,), (1,)), ((), ())), preferred_element_type=f32)


def _dot_tn(a, b):
    return lax.dot_general(a, b, (((0,), (0,)), ((), ())), preferred_element_type=f32)


def _dot3(t_bf, v):
    v1 = v.astype(bf16)
    r1 = v - v1.astype(f32)
    v2 = r1.astype(bf16)
    v3 = (r1 - v2.astype(f32)).astype(bf16)
    return _dot(t_bf, v1) + _dot(t_bf, v2) + _dot(t_bf, v3)


def _pick(n, prefs):
    for p in prefs:
        if n % p == 0:
            return p
    return n


def _full(shape):
    nd = len(shape)
    return pl.BlockSpec(shape, lambda *_: (0,) * nd)


def _matmul(a, b, out_dtype, name, tm=None, tn=None, tk=None):
    m, k = a.shape
    _, n = b.shape
    tm = tm or _pick(m, (768, 512, 256, 128))
    tn = tn or _pick(n, (1024, 512, 256, 128))
    tk = tk or _pick(k, (1024, 768, 512, 256, 128))
    nk = k // tk

    def kern(a_ref, b_ref, o_ref, acc_ref):
        kk = pl.program_id(2)
        part = _dot(a_ref[...], b_ref[...])
        if nk == 1:
            o_ref[...] = part.astype(o_ref.dtype)
        else:
            @pl.when(kk == 0)
            def _():
                acc_ref[...] = part

            @pl.when(kk > 0)
            def _():
                acc_ref[...] += part

            @pl.when(kk == nk - 1)
            def _():
                o_ref[...] = acc_ref[...].astype(o_ref.dtype)

    return pl.pallas_call(
        kern, out_shape=S((m, n), out_dtype), grid=(m // tm, n // tn, nk),
        in_specs=[pl.BlockSpec((tm, tk), lambda i, j, kk: (i, kk)), pl.BlockSpec((tk, tn), lambda i, j, kk: (kk, j))],
        out_specs=pl.BlockSpec((tm, tn), lambda i, j, kk: (i, j)),
        scratch_shapes=[pltpu.VMEM((tm, tn), f32)],
        compiler_params=_params("parallel", "parallel", "arbitrary"), name=name)(a, b)


def _mod_fwd(cc8, w_mod_bf, b_mod):
    def kern(c_ref, w_ref, b_ref, o_ref):
        o_ref[...] = _dot(_silu(c_ref[...]).astype(bf16), w_ref[...]) + b_ref[...]

    return pl.pallas_call(kern, out_shape=S((8, 3 * D), f32), compiler_params=_params(), name="mod_fwd")(cc8, w_mod_bf, b_mod)


def _mod_bwd(ct, dmod8, w_mod_bf):
    tc = 512
    nj = 3 * D // tc

    def kern(ct_ref, dm_ref, w_ref, dw_ref, db_ref, dc_ref):
        j = pl.program_id(0)
        c = ct_ref[:, 0:1]
        cx = ct_ref[:, 1:2]
        sx = _sig(cx)
        dmx = dm_ref[0:1, :]
        dmc = dm_ref[1:2, :]
        dw_ref[...] = _silu(c) * dmx + (cx * sx) * dmc
        db_ref[...] = dmx + dmc
        t = jnp.sum(w_ref[...].astype(f32) * dmc.astype(bf16).astype(f32), axis=1, keepdims=True) * _dsilu(cx, sx)

        @pl.when(j == 0)
        def _():
            dc_ref[...] = jnp.zeros_like(dc_ref)

        dc_ref[...] += jnp.broadcast_to(t, (D, 128))

    return pl.pallas_call(
        kern, out_shape=(S((D, 3 * D), f32), S((1, 3 * D), f32), S((D, 128), f32)), grid=(nj,),
        in_specs=[_full((D, 128)), pl.BlockSpec((8, tc), lambda j: (0, j)), pl.BlockSpec((D, tc), lambda j: (0, j))],
        out_specs=(pl.BlockSpec((D, tc), lambda j: (0, j)), pl.BlockSpec((1, tc), lambda j: (0, j)), _full((D, 128))),
        compiler_params=_params("arbitrary"), name="mod_bwd")(ct, dmod8, w_mod_bf)


def _prenorm(x, ctx, norm_w, mod):
    L, Lc = x.shape[0], ctx.shape[0]
    nlx, nt = L // RT, (L + Lc) // RT

    def kern(x_ref, c_ref, nw_ref, mod_ref, h_ref):
        i = pl.program_id(0)
        is_c = i >= nlx
        xv = jnp.where(is_c, c_ref[...], x_ref[...])
        shift = jnp.where(is_c, mod_ref[1:2, 0:D], mod_ref[0:1, 0:D])
        scale = jnp.where(is_c, mod_ref[1:2, D:2 * D], mod_ref[0:1, D:2 * D])
        r = lax.rsqrt(jnp.mean(xv * xv, axis=1, keepdims=True) + EPS)
        h_ref[...] = ((xv * r) * nw_ref[...] * (1.0 + scale) + shift).astype(bf16)

    return pl.pallas_call(
        kern, out_shape=S((L + Lc, D), bf16), grid=(nt,),
        in_specs=[pl.BlockSpec((RT, D), lambda i: (jnp.minimum(i, nlx - 1), 0)),
                  pl.BlockSpec((RT, D), lambda i: (jnp.maximum(i - nlx, 0), 0)),
                  _full((1, D)), _full((8, 3 * D))],
        out_specs=pl.BlockSpec((RT, D), lambda i: (i, 0)),
        compiler_params=_params("parallel"), name="prenorm")(x, ctx, norm_w, mod)


def _prenorm_bwd(x, ctx, dh, dx1, norm_w, mod):
    L, Lc = x.shape[0], ctx.shape[0]
    nlx, nt = L // RT, (L + Lc) // RT

    def kern(x_ref, c_ref, dh_ref, dx1_ref, nw_ref, mod_ref, gx_ref, dnw_ref, acc_ref):
        i = pl.program_id(0)
        is_c = i >= nlx

        @pl.when(i == 0)
        def _():
            dnw_ref[...] = jnp.zeros_like(dnw_ref)
            acc_ref[...] = jnp.zeros_like(acc_ref)

        xv = jnp.where(is_c, c_ref[...], x_ref[...])
        scale = jnp.where(is_c, mod_ref[1:2, D:2 * D], mod_ref[0:1, D:2 * D])
        nw = nw_ref[...]
        r = lax.rsqrt(jnp.mean(xv * xv, axis=1, keepdims=True) + EPS)
        xn = xv * r
        dh = dh_ref[...]
        dsh = jnp.sum(dh, axis=0, keepdims=True)
        dsc = jnp.sum(dh * (xn * nw), axis=0, keepdims=True)
        dxnw = dh * (1.0 + scale)
        dnw_ref[...] += jnp.sum(dxnw * xn, axis=0, keepdims=True)
        dxn = dxnw * nw
        dx = r * (dxn - xn * jnp.mean(dxn * xn, axis=1, keepdims=True))

        @pl.when(jnp.logical_not(is_c))
        def _():
            gx_ref[...] = dx1_ref[...] + dx
            acc_ref[0:1, :] += dsh
            acc_ref[1:2, :] += dsc

        @pl.when(is_c)
        def _():
            acc_ref[2:3, :] += dsh
            acc_ref[3:4, :] += dsc

    xmap = lambda i: (jnp.minimum(i, nlx - 1), 0)
    return pl.pallas_call(
        kern, out_shape=(S((L, D), f32), S((1, D), f32), S((8, D), f32)), grid=(nt,),
        in_specs=[pl.BlockSpec((RT, D), xmap), pl.BlockSpec((RT, D), lambda i: (jnp.maximum(i - nlx, 0), 0)),
                  pl.BlockSpec((RT, D), lambda i: (i, 0)), pl.BlockSpec((RT, D), xmap), _full((1, D)), _full((8, 3 * D))],
        out_specs=(pl.BlockSpec((RT, D), xmap), _full((1, D)), _full((8, D))),
        compiler_params=_params("arbitrary"), name="prenorm_bwd")(x, ctx, dh, dx1, norm_w, mod)


def _halo_specs(nt_rows, ct):
    cur = pl.BlockSpec((RT, ct), lambda i, j: (i, j))
    prev = pl.BlockSpec((8, ct), lambda i, j: (jnp.maximum(i * (RT // 8) - 1, 0), j))
    nxt = pl.BlockSpec((8, ct), lambda i, j: (jnp.minimum((i + 1) * (RT // 8), nt_rows // 8 - 1), j))
    return cur, prev, nxt


def _fill_halo(scr, cur_ref, prev_ref, next_ref, i, nlx, nt):
    prev_ok = jnp.logical_and(i != 0, i != nlx)
    next_ok = jnp.logical_and(i != nlx - 1, i != nt - 1)
    scr[0:8, :] = jnp.where(prev_ok, prev_ref[...], 0.0)
    scr[8:8 + RT, :] = cur_ref[...]
    scr[8 + RT:16 + RT, :] = jnp.where(next_ok, next_ref[...], 0.0)


def _ssm_conv_fwd(proj, w8, b, nlx):
    T = proj.shape[0]
    nt = T // RT
    ct = 1024
    cur, prev, nxt = _halo_specs(T, ct)

    def kern(cur_ref, prev_ref, next_ref, w_ref, b_ref, o_ref, scr):
        i = pl.program_id(0)
        _fill_halo(scr, cur_ref, prev_ref, next_ref, i, nlx, nt)
        acc = jnp.broadcast_to(b_ref[...], (RT, ct))
        for k in range(SK):
            acc = acc + w_ref[k:k + 1, :] * scr[pl.ds(6 + k, RT), :]
        o_ref[...] = _silu(acc)

    return pl.pallas_call(
        kern, out_shape=S((T, 4096), f32), grid=(nt, 4096 // ct),
        in_specs=[cur, prev, nxt, pl.BlockSpec((8, ct), lambda i, j: (0, j)), pl.BlockSpec((1, ct), lambda i, j: (0, j))],
        out_specs=pl.BlockSpec((RT, ct), lambda i, j: (i, j)),
        scratch_shapes=[pltpu.VMEM((RT + 16, ct), f32)],
        compiler_params=_params("parallel", "parallel"), name="ssm_conv_fwd")(proj, proj, proj, w8, b)


def _ssm_conv_dpre(dxbc, proj, w8, b, nlx):
    T = proj.shape[0]
    nt = T // RT
    ct = 1024
    cur = pl.BlockSpec((RT, ct), lambda j, i: (i, j))
    prev = pl.BlockSpec((8, ct), lambda j, i: (jnp.maximum(i * (RT // 8) - 1, 0), j))
    nxt = pl.BlockSpec((8, ct), lambda j, i: (jnp.minimum((i + 1) * (RT // 8), T // 8 - 1), j))

    def kern(d_ref, cur_ref, prev_ref, next_ref, w_ref, b_ref, dpre_ref, dw_ref, db_ref, scr):
        i = pl.program_id(1)
        _fill_halo(scr, cur_ref, prev_ref, next_ref, i, nlx, nt)

        @pl.when(i == 0)
        def _():
            dw_ref[...] = jnp.zeros_like(dw_ref)
            db_ref[...] = jnp.zeros_like(db_ref)

        pre = jnp.broadcast_to(b_ref[...], (RT, ct))
        for k in range(SK):
            pre = pre + w_ref[k:k + 1, :] * scr[pl.ds(6 + k, RT), :]
        dpre = d_ref[...] * _dsilu(pre, _sig(pre))
        dpre_ref[...] = dpre
        db_ref[...] += jnp.sum(dpre, axis=0, keepdims=True)
        for k in range(SK):
            dw_ref[k:k + 1, :] += jnp.sum(dpre * scr[pl.ds(6 + k, RT), :], axis=0, keepdims=True)

    return pl.pallas_call(
        kern, out_shape=(S((T, 4096), f32), S((8, 4096), f32), S((1, 4096), f32)), grid=(4096 // ct, nt),
        in_specs=[cur, cur, prev, nxt, pl.BlockSpec((8, ct), lambda j, i: (0, j)), pl.BlockSpec((1, ct), lambda j, i: (0, j))],
        out_specs=(cur, pl.BlockSpec((8, ct), lambda j, i: (0, j)), pl.BlockSpec((1, ct), lambda j, i: (0, j))),
        scratch_shapes=[pltpu.VMEM((RT + 16, ct), f32)],
        compiler_params=_params("parallel", "arbitrary"), name="ssm_conv_dpre")(dxbc, proj, proj, proj, w8, b)


def _ssm_conv_t(dpre, w8, dproj, nlx):
    T = dpre.shape[0]
    nt = T // RT
    ct = 1024
    cur, prev, nxt = _halo_specs(T, ct)

    def kern(cur_ref, prev_ref, next_ref, w_ref, _alias, o_ref, scr):
        i = pl.program_id(0)
        _fill_halo(scr, cur_ref, prev_ref, next_ref, i, nlx, nt)
        acc = jnp.zeros((RT, ct), f32)
        for k in range(SK):
            acc = acc + w_ref[k:k + 1, :] * scr[pl.ds(10 - k, RT), :]
        o_ref[...] = acc.astype(bf16)

    return pl.pallas_call(
        kern, out_shape=S(dproj.shape, bf16), grid=(nt, 4096 // ct),
        in_specs=[cur, prev, nxt, pl.BlockSpec((8, ct), lambda i, j: (0, j)), pl.BlockSpec(memory_space=pl.ANY)],
        out_specs=pl.BlockSpec((RT, ct), lambda i, j: (i, j)),
        scratch_shapes=[pltpu.VMEM((RT + 16, ct), f32)], input_output_aliases={4: 0},
        compiler_params=_params("parallel", "parallel"), name="ssm_conv_t")(dpre, dpre, dpre, w8, dproj)


def _tri():
    li = lax.broadcasted_iota(jnp.int32, (Q, Q), 0)
    si = lax.broadcasted_iota(jnp.int32, (Q, Q), 1)
    return (si <= li).astype(bf16), (si >= li).astype(bf16)


def _dt_prep(proj, bias_row, alog_row):
    T = proj.shape[0]
    nch = T // Q

    def kern(raw_ref, b_ref, al_ref, dt_ref, la_ref):
        lane = lax.broadcasted_iota(jnp.int32, (Q, 128), 1)
        v = raw_ref[...] + b_ref[...]
        dt = jnp.maximum(v, 0.0) + jnp.log1p(jnp.exp(-jnp.abs(v)))
        a = jnp.where(lane[0:1, :] < 2 * NH, -jnp.exp(al_ref[...]), 0.0)
        da = dt * a
        tri, trit = _tri()
        dt_ref[...] = dt
        la_ref[...] = jnp.where(lane < NH, _dot3(tri, da), _dot3(trit, da))

    return pl.pallas_call(
        kern, out_shape=(S((T, 128), f32), S((T, 128), f32)), grid=(nch,),
        in_specs=[pl.BlockSpec((Q, 128), lambda c: (c, DT0 // 128)), _full((1, 128)), _full((1, 128))],
        out_specs=(pl.BlockSpec((Q, 128), lambda c: (c, 0)), pl.BlockSpec((Q, 128), lambda c: (c, 0))),
        compiler_params=_params("parallel"), name="dt_prep")(proj, bias_row, alog_row)


def _dt_bwd(a1, a2, r2, sv, dt, la, proj, bias_row, alog_row, dproj):
    T = proj.shape[0]
    nch = T // Q
    blk = pl.BlockSpec((Q, 128), lambda c: (c, 0))

    def kern(a1_ref, a2_ref, r2_ref, s_ref, dt_ref, la_ref, raw_ref, b_ref, al_ref, _alias, o_ref, db_ref, dal_ref):
        c = pl.program_id(0)

        @pl.when(c == 0)
        def _():
            db_ref[...] = jnp.zeros_like(db_ref)
            dal_ref[...] = jnp.zeros_like(dal_ref)

        lane = lax.broadcasted_iota(jnp.int32, (Q, 128), 1)
        row = lax.broadcasted_iota(jnp.int32, (Q, 128), 0)
        fwd = lane < NH
        dt = dt_ref[...]
        la = la_ref[...]
        a2v = a2_ref[...]
        r2v = r2_ref[...]
        a = jnp.where(lane[0:1, :] < 2 * NH, -jnp.exp(al_ref[...]), 0.0)
        la_e = jnp.where(fwd[0:1, :], la[Q - 1:Q, :], la[0:1, :])
        is_end = row == jnp.where(fwd, Q - 1, 0)
        e_end = jnp.exp(la_e - la)
        wend = e_end * dt
        extra = s_ref[0:1, :] * jnp.exp(la_e) + jnp.sum(wend * a2v, axis=0, keepdims=True)
        dla = a1_ref[...] - dt * r2v - wend * a2v + jnp.where(is_end, extra, 0.0)
        tri, trit = _tri()
        rcs = jnp.where(fwd, _dot3(trit, dla), _dot3(tri, dla))
        ddt = r2v + e_end * a2v + a * rcs
        dal_ref[...] += a * jnp.sum(dt * rcs, axis=0, keepdims=True)
        draw = jnp.where(lane < 2 * NH, ddt * _sig(raw_ref[...] + b_ref[...]), 0.0)
        db_ref[...] += jnp.sum(draw, axis=0, keepdims=True)
        o_ref[...] = jnp.zeros_like(o_ref)
        o_ref[:, 0:128] = draw.astype(bf16)

    return pl.pallas_call(
        kern, out_shape=(S(dproj.shape, bf16), S((1, 128), f32), S((1, 128), f32)), grid=(nch,),
        in_specs=[blk, blk, blk, blk, blk, blk, pl.BlockSpec((Q, 128), lambda c: (c, DT0 // 128)),
                  _full((1, 128)), _full((1, 128)), pl.BlockSpec(memory_space=pl.ANY)],
        out_specs=(pl.BlockSpec((Q, 1024), lambda c: (c, DT0 // 1024)), _full((1, 128)), _full((1, 128))),
        input_output_aliases={9: 0},
        compiler_params=_params("arbitrary"), name="dt_bwd")(a1, a2, r2, sv, dt, la, proj, bias_row, alog_row, dproj)


def _scan_consts(rev):
    li = lax.broadcasted_iota(jnp.int32, (Q, Q), 0)
    si = lax.broadcasted_iota(jnp.int32, (Q, Q), 1)
    mask = (li <= si) if rev else (li >= si)
    lane = lax.broadcasted_iota(jnp.int32, (Q, HPG * HD), 1)
    hms = [jnp.logical_and(lane >= r * HD, lane < (r + 1) * HD) for r in range(HPG)]
    return mask, hms


def _chunk_of(j, rev, nxc, nch):
    return (nch - 1 - j) if rev else lax.rem(j + nxc, nch)


def _ssd_fwd(xbc, dt, la, rev, nxc, name):
    T = xbc.shape[0]
    nch = T // Q
    hoff = NH if rev else 0
    e = 0 if rev else Q - 1
    cm = lambda j: _chunk_of(j, rev, nxc, nch)

    def kern(xbc_ref, dt_ref, la_ref, y_ref, hp_ref, h_ref):
        j = pl.program_id(0)

        @pl.when(j == 0)
        def _():
            h_ref[...] = jnp.zeros_like(h_ref)

        hp_ref[...] = h_ref[...]
        mask, hms = _scan_consts(rev)
        la_all = la_ref[...]
        dt_all = dt_ref[...]
        la_t = jnp.transpose(la_all)
        dt_t = jnp.transpose(dt_all)
        for g in range(NG):
            x = xbc_ref[:, g * 256:(g + 1) * 256]
            bb = xbc_ref[:, B0 + g * NS:B0 + (g + 1) * NS].astype(bf16)
            cb = xbc_ref[:, C0 + g * NS:C0 + (g + 1) * NS].astype(bf16)
            hg = h_ref[g * 256:(g + 1) * 256, :]
            scores = _dot_nt(cb, bb)
            yoff = _dot_nt(cb, hg.astype(bf16))
            mixes, xstack = [], []
            expla = jnp.zeros((Q, 256), f32)
            wend = jnp.zeros((Q, 256), f32)
            for r in range(HPG):
                hc = hoff + g * HPG + r
                la_c = la_all[:, hc:hc + 1]
                dt_c = dt_all[:, hc:hc + 1]
                la_r = la_t[hc:hc + 1, :]
                dt_r = dt_t[hc:hc + 1, :]
                la_e = la_r[:, e:e + 1]
                decay = jnp.exp(jnp.where(mask, la_c - la_r, NEG))
                mixes.append((scores * decay * dt_r).astype(bf16))
                xstack.append(jnp.where(hms[r], x, 0.0).astype(bf16))
                expla = jnp.where(hms[r], jnp.exp(la_c), expla)
                wend = jnp.where(hms[r], jnp.exp(la_e - la_c) * dt_c, wend)
                h_ref[g * 256 + r * HD:g * 256 + (r + 1) * HD, :] = hg[r * HD:(r + 1) * HD, :] * jnp.exp(la_e)
            y = _dot(jnp.concatenate(mixes, axis=1), jnp.concatenate(xstack, axis=0)) + yoff * expla
            y_ref[:, g * 256:(g + 1) * 256] = y
            h_ref[g * 256:(g + 1) * 256, :] += _dot_tn((x * wend).astype(bf16), bb)

    return pl.pallas_call(
        kern, out_shape=(S((T, DI), f32), S((nch, DI, NS), f32)), grid=(nch,),
        in_specs=[pl.BlockSpec((Q, 4096), lambda j: (cm(j), 0)), pl.BlockSpec((Q, 128), lambda j: (cm(j), 0)),
                  pl.BlockSpec((Q, 128), lambda j: (cm(j), 0))],
        out_specs=(pl.BlockSpec((Q, DI), lambda j: (cm(j), 0)), pl.BlockSpec((None, DI, NS), lambda j: (cm(j), 0, 0))),
        scratch_shapes=[pltpu.VMEM((DI, NS), f32)],
        compiler_params=_params("arbitrary"), name=name)(xbc, dt, la)


def _ssd_bwd(xbc, dy, dt, la, hprev, dskip_full, rev, nxc, name, acc=None):
    T = xbc.shape[0]
    nch = T // Q
    hoff = NH if rev else 0
    e = 0 if rev else Q - 1
    cm = lambda j: _chunk_of(nch - 1 - j, rev, nxc, nch)
    has_acc = acc is not None

    def kern(*refs):
        xbc_ref, dy_ref, dt_ref, la_ref, hp_ref, dsk_ref = refs[:6]
        k = 6
        if has_acc:
            dxbc_in, a1_in, a2_in, r2_in, s_in = refs[k:k + 5]
            k += 5
        dxbc_ref, a1_ref, a2_ref, r2_ref, s_ref, g_ref, r2scr, sscr = refs[k:k + 8]
        j = pl.program_id(0)

        @pl.when(j == 0)
        def _():
            g_ref[...] = jnp.zeros_like(g_ref)

        mask, hms = _scan_consts(rev)
        lane128 = lax.broadcasted_iota(jnp.int32, (Q, 128), 1)
        la_all = la_ref[...]
        dt_all = dt_ref[...]
        la_t = jnp.transpose(la_all)
        dt_t = jnp.transpose(dt_all)
        r2scr[...] = jnp.zeros_like(r2scr)
        sscr[...] = jnp.zeros_like(sscr)
        a1acc = jnp.zeros((Q, 128), f32)
        a2acc = jnp.zeros((Q, 128), f32)
        for g in range(NG):
            x = xbc_ref[:, g * 256:(g + 1) * 256]
            bb = xbc_ref[:, B0 + g * NS:B0 + (g + 1) * NS].astype(bf16)
            cb = xbc_ref[:, C0 + g * NS:C0 + (g + 1) * NS].astype(bf16)
            dyv = dy_ref[:, g * 256:(g + 1) * 256]
            gg = g_ref[g * 256:(g + 1) * 256, :]
            hg = hp_ref[g * 256:(g + 1) * 256, :]
            gb = gg.astype(bf16)
            hb = hg.astype(bf16)
            xb = x.astype(bf16)
            scores = _dot_nt(cb, bb)
            bg = _dot_nt(bb, gb)
            yoff = _dot_nt(cb, hb)
            dym = jnp.concatenate([jnp.where(hms[r], dyv, 0.0).astype(bf16) for r in range(HPG)], axis=0)
            dyx_all = _dot_nt(dym, xb)
            expla = jnp.zeros((Q, 256), f32)
            wend = jnp.zeros((Q, 256), f32)
            mixes, a1s = [], []
            wsum = jnp.zeros((Q, Q), f32)
            for r in range(HPG):
                hc = hoff + g * HPG + r
                la_c = la_all[:, hc:hc + 1]
                dt_c = dt_all[:, hc:hc + 1]
                la_r = la_t[hc:hc + 1, :]
                dt_r = dt_t[hc:hc + 1, :]
                la_e = la_r[:, e:e + 1]
                decay = jnp.exp(jnp.where(mask, la_c - la_r, NEG))
                dyx = dyx_all[r * Q:(r + 1) * Q, :]
                sd = scores * decay
                fm = dyx * sd
                r2scr[hc:hc + 1, :] = jnp.sum(fm, axis=0, keepdims=True)
                a1s.append(jnp.sum(fm * dt_r, axis=1, keepdims=True))
                wsum = wsum + dyx * decay * dt_r
                mixes.append((sd * dt_r).astype(bf16))
                expla = jnp.where(hms[r], jnp.exp(la_c), expla)
                wend = jnp.where(hms[r], jnp.exp(la_e - la_c) * dt_c, wend)
                gh = gg[r * HD:(r + 1) * HD, :]
                sval = jnp.sum(jnp.sum(gh * hg[r * HD:(r + 1) * HD, :], axis=1, keepdims=True), axis=0, keepdims=True)
                sscr[hc:hc + 1, :] = jnp.broadcast_to(sval, (1, Q))
                g_ref[g * 256 + r * HD:g * 256 + (r + 1) * HD, :] = gh * jnp.exp(la_e)
            tm = dyv * yoff * expla
            um = x * bg
            for r in range(HPG):
                hc = hoff + g * HPG + r
                a1 = a1s[r] + jnp.sum(jnp.where(hms[r], tm, 0.0), axis=1, keepdims=True)
                qv = jnp.sum(jnp.where(hms[r], um, 0.0), axis=1, keepdims=True)
                a1acc = jnp.where(lane128 == hc, a1, a1acc)
                a2acc = jnp.where(lane128 == hc, qv, a2acc)
            dx = _dot_tn(jnp.concatenate(mixes, axis=0), dym) + wend * bg
            if not has_acc:
                dx = dx + dsk_ref[:, g * 256:(g + 1) * 256] * dyv
            wb = wsum.astype(bf16)
            dysb = (dyv * expla).astype(bf16)
            dc = _dot(wb, bb) + _dot(dysb, hb)
            db = _dot_tn(wb, cb) + _dot((x * wend).astype(bf16), gb)
            g_ref[g * 256:(g + 1) * 256, :] += _dot_tn(dysb, cb)
            if has_acc:
                dx = dx + dxbc_in[:, g * 256:(g + 1) * 256]
                db = db + dxbc_in[:, B0 + g * NS:B0 + (g + 1) * NS]
                dc = dc + dxbc_in[:, C0 + g * NS:C0 + (g + 1) * NS]
            dxbc_ref[:, g * 256:(g + 1) * 256] = dx
            dxbc_ref[:, B0 + g * NS:B0 + (g + 1) * NS] = db
            dxbc_ref[:, C0 + g * NS:C0 + (g + 1) * NS] = dc
        r2c = jnp.transpose(r2scr[...])
        sc = jnp.transpose(sscr[...])
        if has_acc:
            a1acc = a1acc + a1_in[...]
            a2acc = a2acc + a2_in[...]
            r2c = r2c + r2_in[...]
            sc = sc + s_in[...]
        a1_ref[...] = a1acc
        a2_ref[...] = a2acc
        r2_ref[...] = r2c
        s_ref[...] = sc

    blk = pl.BlockSpec((Q, 128), lambda j: (cm(j), 0))
    big = pl.BlockSpec((Q, 4096), lambda j: (cm(j), 0))
    in_specs = [big, pl.BlockSpec((Q, DI), lambda j: (cm(j), 0)), blk, blk,
                pl.BlockSpec((None, DI, NS), lambda j: (cm(j), 0, 0)), _full((1, DI))]
    args = [xbc, dy, dt, la, hprev, dskip_full]
    aliases = {}
    if has_acc:
        in_specs += [big, blk, blk, blk, blk]
        args += list(acc)
        aliases = {6: 0, 7: 1, 8: 2, 9: 3, 10: 4}
    return pl.pallas_call(
        kern, out_shape=(S((T, 4096), f32), S((T, 128), f32), S((T, 128), f32), S((T, 128), f32), S((T, 128), f32)),
        grid=(nch,), in_specs=in_specs, out_specs=(big, blk, blk, blk, blk),
        scratch_shapes=[pltpu.VMEM((DI, NS), f32), pltpu.VMEM((128, Q), f32), pltpu.VMEM((128, Q), f32)],
        input_output_aliases=aliases,
        compiler_params=_params("arbitrary"), name=name)(*args)


def _ynorm_fwd(yf, yb, xbc, proj, dskip_full, nw, L):
    nlx = L // RT

    def kern(yf_ref, yb_ref, xs_ref, z_ref, dsk_ref, nw_ref, y_ref, yn_ref):
        y = yf_ref[...] + yb_ref[...] + dsk_ref[...] * xs_ref[...]
        y_ref[...] = y
        yz = y * _silu(z_ref[...])
        for g in range(NG):
            sl = yz[:, g * 256:(g + 1) * 256]
            r = lax.rsqrt(jnp.mean(sl * sl, axis=1, keepdims=True) + EPS)
            yn_ref[:, g * 256:(g + 1) * 256] = ((sl * r) * nw_ref[:, g * 256:(g + 1) * 256]).astype(bf16)

    blk = pl.BlockSpec((RT, DI), lambda i: (i, 0))
    return pl.pallas_call(
        kern, out_shape=(S((L, DI), f32), S((L, DI), bf16)), grid=(nlx,),
        in_specs=[blk, blk, blk, pl.BlockSpec((RT, DI), lambda i: (i, Z0 // DI)), _full((1, DI)), _full((1, DI))],
        out_specs=(blk, blk), compiler_params=_params("parallel"), name="ynorm_fwd")(yf, yb, xbc, proj, dskip_full, nw)


def _ynorm_bwd(dyn, y, xbc, proj, dskip_full, nw, dproj):
    L = y.shape[0]
    T = proj.shape[0]
    nlx, nt = L // RT, T // RT

    def kern(dyn_ref, y_ref, xs_ref, z_ref, dsk_ref, nw_ref, _alias, dz_ref, dy_ref, dnw_ref, dsk_acc):
        i = pl.program_id(0)

        @pl.when(i == 0)
        def _():
            dnw_ref[...] = jnp.zeros_like(dnw_ref)
            dsk_acc[...] = jnp.zeros_like(dsk_acc)

        @pl.when(i >= nlx)
        def _():
            dz_ref[...] = jnp.zeros_like(dz_ref)
            dy_ref[...] = jnp.zeros_like(dy_ref)

        @pl.when(i < nlx)
        def _():
            y = y_ref[...]
            z = z_ref[...]
            sz = _sig(z)
            gz = z * sz
            yz = y * gz
            dynv = dyn_ref[...]
            for g in range(NG):
                cs = slice(g * 256, (g + 1) * 256)
                sl = yz[:, cs]
                r = lax.rsqrt(jnp.mean(sl * sl, axis=1, keepdims=True) + EPS)
                yhat = sl * r
                dn = dynv[:, cs]
                dnw_ref[:, cs] += jnp.sum(dn * yhat, axis=0, keepdims=True)
                dyh = dn * nw_ref[:, cs]
                dyz = r * (dyh - yhat * jnp.mean(dyh * yhat, axis=1, keepdims=True))
                dyv = dyz * gz[:, cs]
                dy_ref[:, cs] = dyv
                dz_ref[:, cs] = (dyz * y[:, cs] * _dsilu(z[:, cs], sz[:, cs])).astype(bf16)
                dsk_acc[:, cs] += jnp.sum(dyv * xs_ref[:, cs], axis=0, keepdims=True)

    xmap = lambda i: (jnp.minimum(i, nlx - 1), 0)
    return pl.pallas_call(
        kern, out_shape=(S(dproj.shape, bf16), S((T, DI), f32), S((1, DI), f32), S((1, DI), f32)), grid=(nt,),
        in_specs=[pl.BlockSpec((RT, DI), xmap), pl.BlockSpec((RT, DI), xmap), pl.BlockSpec((RT, DI), xmap),
                  pl.BlockSpec((RT, DI), lambda i: (jnp.minimum(i, nlx - 1), Z0 // DI)), _full((1, DI)), _full((1, DI)),
                  pl.BlockSpec(memory_space=pl.ANY)],
        out_specs=(pl.BlockSpec((RT, DI), lambda i: (i, Z0 // DI)), pl.BlockSpec((RT, DI), lambda i: (i, 0)),
                   _full((1, DI)), _full((1, DI))),
        input_output_aliases={6: 0},
        compiler_params=_params("arbitrary"), name="ynorm_bwd")(dyn, y, xbc, proj, dskip_full, nw, dproj)


def _head_sums(cols):
    def kern(c_ref, o_ref):
        o_ref[...] = jnp.broadcast_to(jnp.sum(c_ref[...], axis=1, keepdims=True), (NH, 128))

    return pl.pallas_call(kern, out_shape=S((NH, 128), f32), name="head_sums")(cols)


SEG_STRIDE = 96
SEG_PAD = 16
NSEG = RT // GRID_W
CONF_ROWS = SEG_PAD + NSEG * SEG_STRIDE


def _seg_fill(scr, val):
    scr[...] = jnp.zeros_like(scr)
    for s in range(NSEG):
        scr[SEG_PAD + s * SEG_STRIDE:SEG_PAD + s * SEG_STRIDE + GRID_W, :] = val[s * GRID_W:(s + 1) * GRID_W, :]


def _conf_fwd(proj, w32, cb, lnw, lnb, L):
    nlx = L // RT

    def kern(v_ref, g_ref, cg_ref, w_ref, cb_ref, lnw_ref, lnb_ref, u1_ref, u3_ref, scr):
        _seg_fill(scr, v_ref[...] * _sig(g_ref[...]))
        for s in range(NSEG):
            for cc in range(D // 256):
                cs = slice(cc * 256, (cc + 1) * 256)
                acc = jnp.broadcast_to(cb_ref[:, cs], (GRID_W, 256))
                for k in range(CK):
                    acc = acc + w_ref[k:k + 1, cs] * scr[pl.ds(SEG_PAD + s * SEG_STRIDE + k - CK // 2, GRID_W), cs]
                u1_ref[s * GRID_W:(s + 1) * GRID_W, cs] = acc
        u1 = u1_ref[...]
        mu = jnp.mean(u1, axis=1, keepdims=True)
        xc = u1 - mu
        r = lax.rsqrt(jnp.mean(xc * xc, axis=1, keepdims=True) + EPS)
        u2 = (xc * r) * lnw_ref[...] + lnb_ref[...]
        u3_ref[...] = (_silu(u2) * _silu(cg_ref[...])).astype(bf16)

    blk = pl.BlockSpec((RT, D), lambda i: (i, 0))
    return pl.pallas_call(
        kern, out_shape=(S((L, D), f32), S((L, D), bf16)), grid=(nlx,),
        in_specs=[pl.BlockSpec((RT, D), lambda i: (i, GV0 // D)), pl.BlockSpec((RT, D), lambda i: (i, GG0 // D)),
                  pl.BlockSpec((RT, D), lambda i: (i, CG0 // D)), _full((32, D)), _full((1, D)), _full((1, D)), _full((1, D))],
        out_specs=(blk, blk), scratch_shapes=[pltpu.VMEM((CONF_ROWS, D), f32)],
        compiler_params=_params("parallel"), name="conf_fwd")(proj, proj, proj, w32, cb, lnw, lnb)


def _conf_bwd(du3, u1, proj, w32, lnw, lnb, dproj):
    L = u1.shape[0]
    T = proj.shape[0]
    nlx, nt = L // RT, T // RT

    def kern(du3_ref, u1_ref, v_ref, g_ref, cg_ref, w_ref, lnw_ref, lnb_ref, _alias,
             o_ref, dw_ref, dcb_ref, dlw_ref, dlb_ref, scr_u, scr_d, du0_scr):
        i = pl.program_id(0)

        @pl.when(i == 0)
        def _():
            dw_ref[...] = jnp.zeros_like(dw_ref)
            dcb_ref[...] = jnp.zeros_like(dcb_ref)
            dlw_ref[...] = jnp.zeros_like(dlw_ref)
            dlb_ref[...] = jnp.zeros_like(dlb_ref)

        @pl.when(i >= nlx)
        def _():
            o_ref[...] = jnp.zeros_like(o_ref)

        @pl.when(i < nlx)
        def _():
            val = v_ref[...]
            sg = _sig(g_ref[...])
            cg = cg_ref[...]
            scg = _sig(cg)
            u1 = u1_ref[...]
            mu = jnp.mean(u1, axis=1, keepdims=True)
            xc = u1 - mu
            r = lax.rsqrt(jnp.mean(xc * xc, axis=1, keepdims=True) + EPS)
            xhat = xc * r
            u2 = xhat * lnw_ref[...] + lnb_ref[...]
            s2 = _sig(u2)
            du3v = du3_ref[...]
            du2 = du3v * (cg * scg) * _dsilu(u2, s2)
            o_ref[:, 2 * D:3 * D] = (du3v * (u2 * s2) * _dsilu(cg, scg)).astype(bf16)
            dlw_ref[...] += jnp.sum(du2 * xhat, axis=0, keepdims=True)
            dlb_ref[...] += jnp.sum(du2, axis=0, keepdims=True)
            dxh = du2 * lnw_ref[...]
            du1 = r * (dxh - jnp.mean(dxh, axis=1, keepdims=True) - xhat * jnp.mean(dxh * xhat, axis=1, keepdims=True))
            dcb_ref[...] += jnp.sum(du1, axis=0, keepdims=True)
            _seg_fill(scr_u, val * sg)
            _seg_fill(scr_d, du1)
            for cc in range(D // 256):
                cs = slice(cc * 256, (cc + 1) * 256)
                for k in range(CK):
                    t = jnp.zeros((GRID_W, 256), f32)
                    for s in range(NSEG):
                        base = SEG_PAD + s * SEG_STRIDE
                        t = t + scr_d[pl.ds(base, GRID_W), cs] * scr_u[pl.ds(base + k - CK // 2, GRID_W), cs]
                    dw_ref[k:k + 1, cs] += jnp.sum(t, axis=0, keepdims=True)
                for s in range(NSEG):
                    base = SEG_PAD + s * SEG_STRIDE
                    acc = jnp.zeros((GRID_W, 256), f32)
                    for k in range(CK):
                        acc = acc + w_ref[k:k + 1, cs] * scr_d[pl.ds(base + CK // 2 - k, GRID_W), cs]
                    du0_scr[s * GRID_W:(s + 1) * GRID_W, cs] = acc
            du0 = du0_scr[...]
            o_ref[:, 0:D] = (du0 * sg).astype(bf16)
            o_ref[:, D:2 * D] = (du0 * val * sg * (1.0 - sg)).astype(bf16)

    xmap = lambda i: (jnp.minimum(i, nlx - 1), 0)
    pmap = lambda cb: (lambda i: (jnp.minimum(i, nlx - 1), cb))
    return pl.pallas_call(
        kern, out_shape=(S(dproj.shape, bf16), S((32, D), f32), S((1, D), f32), S((1, D), f32), S((1, D), f32)), grid=(nt,),
        in_specs=[pl.BlockSpec((RT, D), xmap), pl.BlockSpec((RT, D), xmap),
                  pl.BlockSpec((RT, D), pmap(GV0 // D)), pl.BlockSpec((RT, D), pmap(GG0 // D)), pl.BlockSpec((RT, D), pmap(CG0 // D)),
                  _full((32, D)), _full((1, D)), _full((1, D)), pl.BlockSpec(memory_space=pl.ANY)],
        out_specs=(pl.BlockSpec((RT, 3 * D), lambda i: (i, GV0 // (3 * D))), _full((32, D)), _full((1, D)), _full((1, D)), _full((1, D))),
        scratch_shapes=[pltpu.VMEM((CONF_ROWS, D), f32), pltpu.VMEM((CONF_ROWS, D), f32), pltpu.VMEM((RT, D), f32)],
        input_output_aliases={8: 0},
        compiler_params=_params("arbitrary"), name="conf_bwd")(du3, u1, proj, proj, proj, w32, lnw, lnb, dproj)


def _merge_fwd(bs, bc, proj):
    L = bs.shape[0]

    def kern(bs_ref, bc_ref, g1_ref, g2_ref, o_ref):
        o_ref[...] = (_sig(g1_ref[...]) * bs_ref[...] + _sig(g2_ref[...]) * bc_ref[...]).astype(bf16)

    blk = pl.BlockSpec((RT, D), lambda i: (i, 0))
    return pl.pallas_call(
        kern, out_shape=S((L, D), bf16), grid=(L // RT,),
        in_specs=[blk, blk, pl.BlockSpec((RT, D), lambda i: (i, G10 // D)), pl.BlockSpec((RT, D), lambda i: (i, G20 // D))],
        out_specs=blk, compiler_params=_params("parallel"), name="merge_fwd")(bs, bc, proj, proj)


def _merge_bwd(dmerged, bs, bc, proj):
    L = bs.shape[0]
    T = proj.shape[0]
    nlx, nt = L // RT, T // RT

    def kern(dm_ref, bs_ref, bc_ref, g1_ref, g2_ref, o_ref, dbs_ref, dbc_ref):
        i = pl.program_id(0)

        @pl.when(i >= nlx)
        def _():
            o_ref[...] = jnp.zeros_like(o_ref)

        @pl.when(i < nlx)
        def _():
            dm = dm_ref[...]
            s1 = _sig(g1_ref[...])
            s2 = _sig(g2_ref[...])
            dbs_ref[...] = (dm * s1).astype(bf16)
            dbc_ref[...] = (dm * s2).astype(bf16)
            o_ref[:, 0:D] = (dm * bs_ref[...] * s1 * (1.0 - s1)).astype(bf16)
            o_ref[:, D:2 * D] = (dm * bc_ref[...] * s2 * (1.0 - s2)).astype(bf16)

    xmap = lambda i: (jnp.minimum(i, nlx - 1), 0)
    pmap = lambda cb: (lambda i: (jnp.minimum(i, nlx - 1), cb))
    xblk = pl.BlockSpec((RT, D), xmap)
    return pl.pallas_call(
        kern, out_shape=(S((T, NP), bf16), S((L, D), bf16), S((L, D), bf16)), grid=(nt,),
        in_specs=[xblk, xblk, xblk, pl.BlockSpec((RT, D), pmap(G10 // D)), pl.BlockSpec((RT, D), pmap(G20 // D))],
        out_specs=(pl.BlockSpec((RT, 2 * D), lambda i: (i, G10 // (2 * D))), xblk, xblk),
        compiler_params=_params("arbitrary"), name="merge_bwd")(dmerged, bs, bc, proj, proj)


def _final(x, out, target, mod, fw):
    L = x.shape[0]

    def kern(x_ref, o_ref, t_ref, mod_ref, fw_ref, dx1_ref, dout_ref, loss_ref, dfw_ref, dg_ref):
        i = pl.program_id(0)

        @pl.when(i == 0)
        def _():
            loss_ref[...] = jnp.zeros_like(loss_ref)
            dfw_ref[...] = jnp.zeros_like(dfw_ref)
            dg_ref[...] = jnp.zeros_like(dg_ref)

        gate = mod_ref[0:1, 2 * D:3 * D]
        ov = o_ref[...]
        x1 = x_ref[...] + gate * ov
        r = lax.rsqrt(jnp.mean(x1 * x1, axis=1, keepdims=True) + EPS)
        xn = x1 * r
        fw = fw_ref[...]
        err = xn * fw - t_ref[...]
        part = 0.5 * jnp.sum(jnp.mean(err * err, axis=1, keepdims=True), axis=0, keepdims=True)
        loss_ref[...] += jnp.broadcast_to(part, (8, 128))
        dy = err * (1.0 / D)
        dfw_ref[...] += jnp.sum(dy * xn, axis=0, keepdims=True)
        dyw = dy * fw
        dx1 = r * (dyw - xn * jnp.mean(dyw * xn, axis=1, keepdims=True))
        dx1_ref[...] = dx1
        dout_ref[...] = (gate * dx1).astype(bf16)
        dg_ref[...] += jnp.sum(dx1 * ov, axis=0, keepdims=True)

    blk = pl.BlockSpec((RT, D), lambda i: (i, 0))
    return pl.pallas_call(
        kern, out_shape=(S((L, D), f32), S((L, D), bf16), S((8, 128), f32), S((1, D), f32), S((1, D), f32)), grid=(L // RT,),
        in_specs=[blk, blk, blk, _full((8, 3 * D)), _full((1, D))],
        out_specs=(blk, blk, _full((8, 128)), _full((1, D)), _full((1, D))),
        compiler_params=_params("arbitrary"), name="final")(x, out, target, mod, fw)


def _exchange(arrs, scatter, name):
    n = len(arrs)
    out_shape = tuple(S((N_DEV,) + (a.shape[1:] if scatter else a.shape), a.dtype) for a in arrs)

    def kern(*refs):
        ins, outs = refs[:n], refs[n:2 * n]
        send_sems, recv_sems, loc_sems = refs[2 * n:]
        me = 4 * lax.axis_index("x") + 2 * lax.axis_index("y") + lax.axis_index("c")
        sends, locals_ = [], []
        for a in range(n):
            mine = ins[a].at[me] if scatter else ins[a]
            loc = pltpu.make_async_copy(mine, outs[a].at[me], loc_sems.at[a])
            loc.start()
            locals_.append(loc)
            for k in range(1, N_DEV):
                peer = lax.rem(me + k, N_DEV)
                pid = (peer // 4, lax.rem(peer // 2, 2), lax.rem(peer, 2))
                src = ins[a].at[peer] if scatter else ins[a]
                cp = pltpu.make_async_remote_copy(src_ref=src, dst_ref=outs[a].at[me], send_sem=send_sems.at[a, k - 1],
                                                  recv_sem=recv_sems.at[a, k - 1], device_id=pid, device_id_type=MESH)
                cp.start()
                sends.append(cp)
        for a in range(n):
            for k in range(1, N_DEV):
                frm = lax.rem(me + N_DEV - k, N_DEV)
                src = ins[a].at[frm] if scatter else ins[a]
                pltpu.make_async_remote_copy(src_ref=src, dst_ref=outs[a].at[frm], send_sem=send_sems.at[a, k - 1],
                                             recv_sem=recv_sems.at[a, k - 1], device_id=(0, 0, 0), device_id_type=MESH).wait_recv()
        for cp in sends:
            cp.wait_send()
        for cp in locals_:
            cp.wait()

    anyspec = pl.BlockSpec(memory_space=pl.ANY)
    return pl.pallas_call(
        kern, out_shape=out_shape, in_specs=[anyspec] * n, out_specs=tuple([anyspec] * n),
        scratch_shapes=[pltpu.SemaphoreType.DMA((n, N_DEV - 1)), pltpu.SemaphoreType.DMA((n, N_DEV - 1)), pltpu.SemaphoreType.DMA((n,))],
        name=name)(*arrs)


def _adamw(parts, w, m, v, name):
    r, c = w.shape
    tr = r
    for cand in (128, 64, 32, 16, 8):
        if r % cand == 0 and r > cand:
            tr = cand
            break
    c1 = 1.0 / (1.0 - ADAM_B1 ** ADAM_STEP)
    c2 = 1.0 / (1.0 - ADAM_B2 ** ADAM_STEP)

    def kern(p_ref, w_ref, m_ref, v_ref, g_ref, d_ref, m2_ref, v2_ref):
        g = p_ref[0]
        for i in range(1, N_DEV):
            g = g + p_ref[i]
        g_ref[...] = g
        m2 = ADAM_B1 * m_ref[...] + (1.0 - ADAM_B1) * g
        v2 = ADAM_B2 * v_ref[...] + (1.0 - ADAM_B2) * (g * g)
        m2_ref[...] = m2
        v2_ref[...] = v2
        d_ref[...] = -ADAM_LR * ((m2 * c1) / (jnp.sqrt(v2 * c2) + ADAM_EPS) + ADAM_WD * w_ref[...])

    blk = pl.BlockSpec((tr, c), lambda i: (i, 0))
    sh = S((r, c), f32)
    return pl.pallas_call(
        kern, out_shape=(sh, sh, sh, sh), grid=(r // tr,),
        in_specs=[pl.BlockSpec((N_DEV, tr, c), lambda i: (0, i, 0)), blk, blk, blk], out_specs=(blk, blk, blk, blk),
        compiler_params=_params("parallel"), name=name)(parts, w, m, v)


_SMALL = (("c_ctx", 1024), ("b_mod", 3072), ("norm_w", 1024), ("ssm_conv_b", 4096), ("dt_bias", 64), ("a_log", 64),
          ("d_skip", 32), ("ssm_norm_w", 2048), ("conf_conv_b", 1024), ("conf_ln_w", 1024), ("conf_ln_b", 1024),
          ("final_norm_w", 1024))
SMALL_ROWS = 128


def _pack_small(d):
    rows = []
    for name, n in _SMALL:
        v = d[name].reshape(-1).astype(f32)
        pad = (-n) % 128
        if pad:
            v = jnp.concatenate([v, jnp.zeros((pad,), f32)])
        rows.append(v.reshape(-1, 128))
    used = sum(r.shape[0] for r in rows)
    rows.append(jnp.zeros((SMALL_ROWS - used, 128), f32))
    return jnp.concatenate(rows, axis=0)


def _unpack_small(p, shapes):
    out, r0 = {}, 0
    for name, n in _SMALL:
        nr = (n + 127) // 128
        out[name] = p[r0:r0 + nr].reshape(-1)[:n].reshape(shapes[name])
        r0 += nr
    return out


def _permute_w_in(w):
    return jnp.concatenate([w[:, 0:4096], w[:, 4160:6208], w[:, 9280:11328], w[:, 4096:4160],
                            jnp.zeros((w.shape[0], 1024 - 64), w.dtype), w[:, 6208:9280]], axis=1)


def _unpermute_w_in(wp):
    return jnp.concatenate([wp[:, 0:4096], wp[:, DT0:DT0 + 64], wp[:, Z0:Z0 + 2048], wp[:, GV0:GV0 + 3072],
                            wp[:, G10:G10 + 2048]], axis=1)


def _cols_gathered(g):
    return jnp.transpose(g, (1, 0, 2)).reshape(g.shape[1], N_DEV * g.shape[2])


def _cols_to_blocks(a):
    r, c8 = a.shape
    return jnp.transpose(a.reshape(r, N_DEV, c8 // N_DEV), (1, 0, 2))


def kernel(x, c, ctx, c_ctx, w_mod, b_mod, norm_w, w_in, ssm_conv_w, ssm_conv_b, dt_bias, a_log, d_skip, ssm_norm_w, w_out_ssm, conf_conv_w, conf_conv_b, conf_ln_w, conf_ln_b, w_out_conf, w_out, final_norm_w, loss_target, m_c_ctx, m_w_mod, m_b_mod, m_norm_w, m_w_in, m_ssm_conv_w, m_ssm_conv_b, m_dt_bias, m_a_log, m_d_skip, m_ssm_norm_w, m_w_out_ssm, m_conf_conv_w, m_conf_conv_b, m_conf_ln_w, m_conf_ln_b, m_w_out_conf, m_w_out, m_final_norm_w, v_c_ctx, v_w_mod, v_b_mod, v_norm_w, v_w_in, v_ssm_conv_w, v_ssm_conv_b, v_dt_bias, v_a_log, v_d_skip, v_ssm_norm_w, v_w_out_ssm, v_conf_conv_w, v_conf_conv_b, v_conf_ln_w, v_conf_ln_b, v_w_out_conf, v_w_out, v_final_norm_w):
    L = x.shape[1]
    Lc = ctx.shape[1]
    T = L + Lc
    nlx = L // RT
    nxc = L // Q
    x2 = x.reshape(L, D)
    ctx2 = ctx.reshape(Lc, D)
    tgt = loss_target.reshape(L, D)

    gathered = _exchange(
        [w_in[0].astype(bf16), w_mod[0].astype(bf16), w_out_ssm[0].astype(bf16), w_out_conf[0].astype(bf16),
         w_out[0].astype(bf16), ssm_conv_w[0], conf_conv_w[0]], scatter=False, name="gather_weights")
    wp = _permute_w_in(_cols_gathered(gathered[0]))
    wmod_bf = _cols_gathered(gathered[1])
    wos_bf = gathered[2].reshape(DI, D)
    woc_bf = gathered[3].reshape(D, D)
    wo_bf = gathered[4].reshape(D, D)
    scw8 = jnp.concatenate([_cols_gathered(gathered[5]), jnp.zeros((8 - SK, 4096), f32)], axis=0)
    ccw32 = jnp.concatenate([_cols_gathered(gathered[6]), jnp.zeros((32 - CK, D), f32)], axis=0)

    norm_w1 = norm_w.reshape(1, D)
    scb = ssm_conv_b.reshape(1, 4096)
    bias_row = jnp.concatenate([dt_bias.reshape(1, 2 * NH), jnp.zeros((1, 128 - 2 * NH), f32)], axis=1)
    alog_row = jnp.concatenate([a_log.reshape(1, 2 * NH), jnp.zeros((1, 128 - 2 * NH), f32)], axis=1)
    dskip_full = jnp.repeat(d_skip.reshape(NH), HD).reshape(1, DI)
    snw = ssm_norm_w.reshape(1, DI)
    ccb = conf_conv_b.reshape(1, D)
    lnw = conf_ln_w.reshape(1, D)
    lnb = conf_ln_b.reshape(1, D)
    fw = final_norm_w.reshape(1, D)

    cc8 = jnp.concatenate([c.reshape(1, D), c_ctx.reshape(1, D), jnp.zeros((6, D), f32)], axis=0)
    mod = _mod_fwd(cc8, wmod_bf, b_mod.reshape(1, 3 * D))
    h = _prenorm(x2, ctx2, norm_w1, mod)
    proj = _matmul(h, wp, f32, "proj")
    xbc = _ssm_conv_fwd(proj, scw8, scb, nlx)
    dt, la = _dt_prep(proj, bias_row, alog_row)
    yf, hp_f = _ssd_fwd(xbc, dt, la, False, nxc, "ssd_fwd_f")
    yb, hp_b = _ssd_fwd(xbc, dt, la, True, nxc, "ssd_fwd_b")
    y, yn = _ynorm_fwd(yf, yb, xbc, proj, dskip_full, snw, L)
    bs = _matmul(yn, wos_bf, f32, "branch_ssm")
    u1, u3 = _conf_fwd(proj, ccw32, ccb, lnw, lnb, L)
    bc = _matmul(u3, woc_bf, f32, "branch_conf")
    merged = _merge_fwd(bs, bc, proj)
    out = _matmul(merged, wo_bf, f32, "out_proj")
    dx1, dout, loss_acc, dfw, dgate = _final(x2, out, tgt, mod, fw)

    dmerged = _matmul(dout, wo_bf.T, f32, "d_merged")
    g_wo = _matmul(merged.T, dout, f32, "g_w_out")
    dproj, dbs, dbc = _merge_bwd(dmerged, bs, bc, proj)
    dyn = _matmul(dbs, wos_bf.T, f32, "d_yn")
    g_wos = _matmul(yn.T, dbs, f32, "g_w_out_ssm")
    du3 = _matmul(dbc, woc_bf.T, f32, "d_u3")
    g_woc = _matmul(u3.T, dbc, f32, "g_w_out_conf")
    dproj, g_ccw, g_ccb, g_lnw, g_lnb = _conf_bwd(du3, u1, proj, ccw32, lnw, lnb, dproj)
    dproj, dy, g_snw, dsk_cols = _ynorm_bwd(dyn, y, xbc, proj, dskip_full, snw, dproj)
    acc_f = _ssd_bwd(xbc, dy, dt, la, hp_f, dskip_full, False, nxc, "ssd_bwd_f")
    dxbc, a1, a2, r2, sv = _ssd_bwd(xbc, dy, dt, la, hp_b, dskip_full, True, nxc, "ssd_bwd_b", acc=acc_f)
    dproj, g_dtb, g_alog = _dt_bwd(a1, a2, r2, sv, dt, la, proj, bias_row, alog_row, dproj)
    dpre, g_scw, g_scb = _ssm_conv_dpre(dxbc, proj, scw8, scb, nlx)
    dproj = _ssm_conv_t(dpre, scw8, dproj, nlx)
    dh = _matmul(dproj, wp.T, f32, "d_h")
    g_wp = _matmul(h.T, dproj, f32, "g_w_in", tm=1024)
    gx, g_nw, macc = _prenorm_bwd(x2, ctx2, dh, dx1, norm_w1, mod)
    z1 = jnp.zeros((1, D), f32)
    dmod8 = jnp.concatenate([jnp.concatenate([macc[0:1], macc[1:2], dgate], axis=1),
                             jnp.concatenate([macc[2:3], macc[3:4], z1], axis=1), jnp.zeros((6, 3 * D), f32)], axis=0)
    ct = jnp.concatenate([c.reshape(D, 1), c_ctx.reshape(D, 1), jnp.zeros((D, 126), f32)], axis=1)
    g_wmod, g_bmod, g_cctx = _mod_bwd(ct, dmod8, wmod_bf)
    g_dskip = _head_sums(dsk_cols.reshape(NH, HD))[:, 0]

    small_g = _pack_small({
        "c_ctx": g_cctx[:, 0], "b_mod": g_bmod, "norm_w": g_nw, "ssm_conv_b": g_scb, "dt_bias": g_dtb[0, :2 * NH],
        "a_log": g_alog[0, :2 * NH], "d_skip": g_dskip, "ssm_norm_w": g_snw, "conf_conv_b": g_ccb, "conf_ln_w": g_lnw,
        "conf_ln_b": g_lnb, "final_norm_w": dfw})
    parts = _exchange(
        [_cols_to_blocks(_unpermute_w_in(g_wp)), _cols_to_blocks(g_wmod), g_wos.reshape(N_DEV, DI // N_DEV, D),
         g_woc.reshape(N_DEV, D // N_DEV, D), g_wo.reshape(N_DEV, D // N_DEV, D), _cols_to_blocks(g_scw[:SK]),
         _cols_to_blocks(g_ccw[:CK])], scatter=True, name="scatter_grads")
    small_parts = _exchange([small_g], scatter=False, name="gather_small_grads")[0]

    given = dict(c_ctx=c_ctx, w_mod=w_mod, b_mod=b_mod, norm_w=norm_w, w_in=w_in, ssm_conv_w=ssm_conv_w, ssm_conv_b=ssm_conv_b,
                 dt_bias=dt_bias, a_log=a_log, d_skip=d_skip, ssm_norm_w=ssm_norm_w, w_out_ssm=w_out_ssm, conf_conv_w=conf_conv_w,
                 conf_conv_b=conf_conv_b, conf_ln_w=conf_ln_w, conf_ln_b=conf_ln_b, w_out_conf=w_out_conf, w_out=w_out,
                 final_norm_w=final_norm_w)
    ms = dict(c_ctx=m_c_ctx, w_mod=m_w_mod, b_mod=m_b_mod, norm_w=m_norm_w, w_in=m_w_in, ssm_conv_w=m_ssm_conv_w,
              ssm_conv_b=m_ssm_conv_b, dt_bias=m_dt_bias, a_log=m_a_log, d_skip=m_d_skip, ssm_norm_w=m_ssm_norm_w,
              w_out_ssm=m_w_out_ssm, conf_conv_w=m_conf_conv_w, conf_conv_b=m_conf_conv_b, conf_ln_w=m_conf_ln_w,
              conf_ln_b=m_conf_ln_b, w_out_conf=m_w_out_conf, w_out=m_w_out, final_norm_w=m_final_norm_w)
    vs = dict(c_ctx=v_c_ctx, w_mod=v_w_mod, b_mod=v_b_mod, norm_w=v_norm_w, w_in=v_w_in, ssm_conv_w=v_ssm_conv_w,
              ssm_conv_b=v_ssm_conv_b, dt_bias=v_dt_bias, a_log=v_a_log, d_skip=v_d_skip, ssm_norm_w=v_ssm_norm_w,
              w_out_ssm=v_w_out_ssm, conf_conv_w=v_conf_conv_w, conf_conv_b=v_conf_conv_b, conf_ln_w=v_conf_ln_w,
              conf_ln_b=v_conf_ln_b, w_out_conf=v_w_out_conf, w_out=v_w_out, final_norm_w=v_final_norm_w)
    grads, deltas, new_m, new_v = {}, {}, {}, {}
    sharded = ("w_in", "w_mod", "w_out_ssm", "w_out_conf", "w_out", "ssm_conv_w", "conf_conv_w")
    for i, nm in enumerate(sharded):
        shp = given[nm].shape
        w2 = given[nm].reshape(shp[1], shp[2])
        res = _adamw(parts[i], w2, ms[nm].reshape(w2.shape), vs[nm].reshape(w2.shape), "adamw_" + nm)
        grads[nm], deltas[nm], new_m[nm], new_v[nm] = [r.reshape(shp) for r in res]
    shapes = {nm: given[nm].shape for nm, _ in _SMALL}
    res = _adamw(small_parts, _pack_small(given), _pack_small(ms), _pack_small(vs), "adamw_small")
    for dst, packed in zip((grads, deltas, new_m, new_v), res):
        dst.update(_unpack_small(packed, shapes))

    loss = lax.psum(loss_acc[0, 0], ("x", "y", "c"))
    order = ("c_ctx", "w_mod", "b_mod", "norm_w", "w_in", "ssm_conv_w", "ssm_conv_b", "dt_bias", "a_log", "d_skip", "ssm_norm_w",
             "w_out_ssm", "conf_conv_w", "conf_conv_b", "conf_ln_w", "conf_ln_b", "w_out_conf", "w_out", "final_norm_w")
    return (loss, gx.reshape(1, L, D), *[grads[n] for n in order], *[deltas[n] for n in order],
            *[new_m[n] for n in order], *[new_v[n] for n in order])
```

```python
import jax
import jax.numpy as jnp
from jax import lax
from jax.experimental import pallas as pl
from jax.experimental.pallas import tpu as pltpu

f32 = jnp.float32
bf16 = jnp.bfloat16

D = 1024
DI = 2048
NG = 8
HPG = 4
HD = 64
NS = 128
NH = 32
Q = 128
GRID_W = 64
CK = 31
SK = 4
EPS = 1e-6
RT = 256
N_DEV = 8
IN_COLS = 11328
X0, B0, C0, Z0, G10, G20, DT0, GV0, GG0, CG0, NP = 0, 2048, 3072, 4096, 6144, 7168, 8192, 9216, 10240, 11264, 12288
VMEM_LIMIT = 50 * 1024 * 1024
NEG = -1e30

ADAM_LR, ADAM_B1, ADAM_B2, ADAM_EPS, ADAM_WD, ADAM_STEP = 0.001, 0.9, 0.999, 1e-08, 0.01, 10

MESH = pl.DeviceIdType.MESH
S = jax.ShapeDtypeStruct


def _params(*sem):
    return pltpu.CompilerParams(dimension_semantics=tuple(sem) if sem else None, vmem_limit_bytes=VMEM_LIMIT)


def _sig(x):
    return 1.0 / (1.0 + jnp.exp(-x))


def _silu(x):
    return x * _sig(x)


def _dsilu(x, s):
    return s * (1.0 + x * (1.0 - s))


def _dot(a, b):
    return jnp.dot(a, b, preferred_element_type=f32)


def _dot_nt(a, b):
    return lax.dot_general(a, b, (((1,), (1,)), ((), ())), preferred_element_type=f32)


def _dot_tn(a, b):
    return lax.dot_general(a, b, (((0,), (0,)), ((), ())), preferred_element_type=f32)


def _dot3(t_bf, v):
    v1 = v.astype(bf16)
    r1 = v - v1.astype(f32)
    v2 = r1.astype(bf16)
    v3 = (r1 - v2.astype(f32)).astype(bf16)
    return _dot(t_bf, v1) + _dot(t_bf, v2) + _dot(t_bf, v3)


def _pick(n, prefs):
    for p in prefs:
        if n % p == 0:
            return p
    return n


def _full(shape):
    nd = len(shape)
    return pl.BlockSpec(shape, lambda *_: (0,) * nd)


def _matmul(a, b, out_dtype, name, tm=None, tn=None, tk=None, tb=False, comm=None):
    m, k = a.shape
    n = b.shape[0] if tb else b.shape[1]
    tm = tm or _pick(m, (768, 512, 256, 128))
    tn = tn or _pick(n, (1024, 512, 256, 128))
    tk = tk or _pick(k, (1024, 768, 512, 256, 128))
    nk = k // tk
    gi, gj = m // tm, n // tn
    carrs, modes = comm if comm else ((), ())
    nc = len(carrs)

    def kern(*refs):
        a_ref, b_ref = refs[:2]
        cins = refs[2:2 + nc]
        o_ref = refs[2 + nc]
        couts = refs[3 + nc:3 + 2 * nc]
        acc_ref = refs[3 + 2 * nc]
        sems = refs[4 + 2 * nc:]
        i, j, kk = pl.program_id(0), pl.program_id(1), pl.program_id(2)
        if nc:
            @pl.when(jnp.logical_and(jnp.logical_and(i == 0, j == 0), kk == 0))
            def _():
                _xchg_start(cins, couts, *sems, modes)

        part = _dot_nt(a_ref[...], b_ref[...]) if tb else _dot(a_ref[...], b_ref[...])
        if nk == 1:
            o_ref[...] = part.astype(o_ref.dtype)
        else:
            @pl.when(kk == 0)
            def _():
                acc_ref[...] = part

            @pl.when(kk > 0)
            def _():
                acc_ref[...] += part

            @pl.when(kk == nk - 1)
            def _():
                o_ref[...] = acc_ref[...].astype(o_ref.dtype)

        if nc:
            @pl.when(jnp.logical_and(jnp.logical_and(i == gi - 1, j == gj - 1), kk == nk - 1))
            def _():
                _xchg_wait(cins, couts, *sems, modes)

    anyspec = pl.BlockSpec(memory_space=pl.ANY)
    bspec = pl.BlockSpec((tn, tk), lambda i, j, kk: (j, kk)) if tb else pl.BlockSpec((tk, tn), lambda i, j, kk: (kk, j))
    out_shape = (S((m, n), out_dtype),) + _xchg_out_shapes(carrs, modes)
    res = pl.pallas_call(
        kern, out_shape=out_shape, grid=(gi, gj, nk),
        in_specs=[pl.BlockSpec((tm, tk), lambda i, j, kk: (i, kk)), bspec] + [anyspec] * nc,
        out_specs=(pl.BlockSpec((tm, tn), lambda i, j, kk: (i, j)),) + (anyspec,) * nc,
        scratch_shapes=[pltpu.VMEM((tm, tn), f32)] + (_xchg_sems(nc) if nc else []),
        compiler_params=_params(*((("arbitrary",) * 3) if nc else ("parallel", "parallel", "arbitrary"))), name=name)(a, b, *carrs)
    return res if nc else res[0]


def _mod_fwd(cc8, w_mod_bf, b_mod):
    def kern(c_ref, w_ref, b_ref, o_ref):
        o_ref[...] = _dot(_silu(c_ref[...]).astype(bf16), w_ref[...]) + b_ref[...]

    return pl.pallas_call(kern, out_shape=S((8, 3 * D), f32), compiler_params=_params(), name="mod_fwd")(cc8, w_mod_bf, b_mod)


def _mod_bwd(ct, dmod8, w_mod_bf):
    tc = 512
    nj = 3 * D // tc

    def kern(ct_ref, dm_ref, w_ref, dw_ref, db_ref, dc_ref):
        j = pl.program_id(0)
        c = ct_ref[:, 0:1]
        cx = ct_ref[:, 1:2]
        sx = _sig(cx)
        dmx = dm_ref[0:1, :]
        dmc = dm_ref[1:2, :]
        dw_ref[...] = (_silu(c) * dmx + (cx * sx) * dmc).astype(bf16)
        db_ref[...] = dmx + dmc
        t = jnp.sum(w_ref[...].astype(f32) * dmc.astype(bf16).astype(f32), axis=1, keepdims=True) * _dsilu(cx, sx)

        @pl.when(j == 0)
        def _():
            dc_ref[...] = jnp.zeros_like(dc_ref)

        dc_ref[...] += jnp.broadcast_to(t, (D, 128))

    return pl.pallas_call(
        kern, out_shape=(S((D, 3 * D), bf16), S((1, 3 * D), f32), S((D, 128), f32)), grid=(nj,),
        in_specs=[_full((D, 128)), pl.BlockSpec((8, tc), lambda j: (0, j)), pl.BlockSpec((D, tc), lambda j: (0, j))],
        out_specs=(pl.BlockSpec((D, tc), lambda j: (0, j)), pl.BlockSpec((1, tc), lambda j: (0, j)), _full((D, 128))),
        compiler_params=_params("arbitrary"), name="mod_bwd")(ct, dmod8, w_mod_bf)


def _prenorm(x, ctx, norm_w, mod):
    L, Lc = x.shape[0], ctx.shape[0]
    nlx, nt = L // RT, (L + Lc) // RT

    def kern(x_ref, c_ref, nw_ref, mod_ref, h_ref, ht_ref):
        i = pl.program_id(0)
        is_c = i >= nlx
        xv = jnp.where(is_c, c_ref[...], x_ref[...])
        shift = jnp.where(is_c, mod_ref[1:2, 0:D], mod_ref[0:1, 0:D])
        scale = jnp.where(is_c, mod_ref[1:2, D:2 * D], mod_ref[0:1, D:2 * D])
        r = lax.rsqrt(jnp.mean(xv * xv, axis=1, keepdims=True) + EPS)
        hv = (xv * r) * nw_ref[...] * (1.0 + scale) + shift
        h_ref[...] = hv.astype(bf16)
        ht_ref[...] = jnp.transpose(hv).astype(bf16)

    return pl.pallas_call(
        kern, out_shape=(S((L + Lc, D), bf16), S((D, L + Lc), bf16)), grid=(nt,),
        in_specs=[pl.BlockSpec((RT, D), lambda i: (jnp.minimum(i, nlx - 1), 0)),
                  pl.BlockSpec((RT, D), lambda i: (jnp.maximum(i - nlx, 0), 0)),
                  _full((1, D)), _full((8, 3 * D))],
        out_specs=(pl.BlockSpec((RT, D), lambda i: (i, 0)), pl.BlockSpec((D, RT), lambda i: (0, i))),
        compiler_params=_params("parallel"), name="prenorm")(x, ctx, norm_w, mod)


def _prenorm_bwd(x, ctx, dh, dx1, norm_w, mod):
    L, Lc = x.shape[0], ctx.shape[0]
    nlx, nt = L // RT, (L + Lc) // RT

    def kern(x_ref, c_ref, dh_ref, dx1_ref, nw_ref, mod_ref, gx_ref, dnw_ref, acc_ref):
        i = pl.program_id(0)
        is_c = i >= nlx

        @pl.when(i == 0)
        def _():
            dnw_ref[...] = jnp.zeros_like(dnw_ref)
            acc_ref[...] = jnp.zeros_like(acc_ref)

        xv = jnp.where(is_c, c_ref[...], x_ref[...])
        scale = jnp.where(is_c, mod_ref[1:2, D:2 * D], mod_ref[0:1, D:2 * D])
        nw = nw_ref[...]
        r = lax.rsqrt(jnp.mean(xv * xv, axis=1, keepdims=True) + EPS)
        xn = xv * r
        dh = dh_ref[...]
        dsh = jnp.sum(dh, axis=0, keepdims=True)
        dsc = jnp.sum(dh * (xn * nw), axis=0, keepdims=True)
        dxnw = dh * (1.0 + scale)
        dnw_ref[...] += jnp.sum(dxnw * xn, axis=0, keepdims=True)
        dxn = dxnw * nw
        dx = r * (dxn - xn * jnp.mean(dxn * xn, axis=1, keepdims=True))

        @pl.when(jnp.logical_not(is_c))
        def _():
            gx_ref[...] = dx1_ref[...] + dx
            acc_ref[0:1, :] += dsh
            acc_ref[1:2, :] += dsc

        @pl.when(is_c)
        def _():
            acc_ref[2:3, :] += dsh
            acc_ref[3:4, :] += dsc

    xmap = lambda i: (jnp.minimum(i, nlx - 1), 0)
    return pl.pallas_call(
        kern, out_shape=(S((L, D), f32), S((1, D), f32), S((8, D), f32)), grid=(nt,),
        in_specs=[pl.BlockSpec((RT, D), xmap), pl.BlockSpec((RT, D), lambda i: (jnp.maximum(i - nlx, 0), 0)),
                  pl.BlockSpec((RT, D), lambda i: (i, 0)), pl.BlockSpec((RT, D), xmap), _full((1, D)), _full((8, 3 * D))],
        out_specs=(pl.BlockSpec((RT, D), xmap), _full((1, D)), _full((8, D))),
        compiler_params=_params("arbitrary"), name="prenorm_bwd")(x, ctx, dh, dx1, norm_w, mod)


def _halo_specs(nt_rows, ct):
    cur = pl.BlockSpec((RT, ct), lambda i, j: (i, j))
    prev = pl.BlockSpec((8, ct), lambda i, j: (jnp.maximum(i * (RT // 8) - 1, 0), j))
    nxt = pl.BlockSpec((8, ct), lambda i, j: (jnp.minimum((i + 1) * (RT // 8), nt_rows // 8 - 1), j))
    return cur, prev, nxt


def _fill_halo(scr, cur_ref, prev_ref, next_ref, i, nlx, nt):
    prev_ok = jnp.logical_and(i != 0, i != nlx)
    next_ok = jnp.logical_and(i != nlx - 1, i != nt - 1)
    scr[0:8, :] = jnp.where(prev_ok, prev_ref[...], 0.0)
    scr[8:8 + RT, :] = cur_ref[...]
    scr[8 + RT:16 + RT, :] = jnp.where(next_ok, next_ref[...], 0.0)


def _ssm_conv_fwd(proj, w8, b, nlx):
    T = proj.shape[0]
    nt = T // RT
    ct = 1024
    cur, prev, nxt = _halo_specs(T, ct)

    def kern(cur_ref, prev_ref, next_ref, w_ref, b_ref, o_ref, scr):
        i = pl.program_id(0)
        _fill_halo(scr, cur_ref, prev_ref, next_ref, i, nlx, nt)
        acc = jnp.broadcast_to(b_ref[...], (RT, ct))
        for k in range(SK):
            acc = acc + w_ref[k:k + 1, :] * scr[pl.ds(6 + k, RT), :]
        o_ref[...] = _silu(acc)

    return pl.pallas_call(
        kern, out_shape=S((T, 4096), f32), grid=(nt, 4096 // ct),
        in_specs=[cur, prev, nxt, pl.BlockSpec((8, ct), lambda i, j: (0, j)), pl.BlockSpec((1, ct), lambda i, j: (0, j))],
        out_specs=pl.BlockSpec((RT, ct), lambda i, j: (i, j)),
        scratch_shapes=[pltpu.VMEM((RT + 16, ct), f32)],
        compiler_params=_params("parallel", "parallel"), name="ssm_conv_fwd")(proj, proj, proj, w8, b)


def _ssm_conv_dpre(dxbc, proj, w8, b, nlx):
    T = proj.shape[0]
    nt = T // RT
    ct = 1024
    cur = pl.BlockSpec((RT, ct), lambda j, i: (i, j))
    prev = pl.BlockSpec((8, ct), lambda j, i: (jnp.maximum(i * (RT // 8) - 1, 0), j))
    nxt = pl.BlockSpec((8, ct), lambda j, i: (jnp.minimum((i + 1) * (RT // 8), T // 8 - 1), j))

    def kern(d_ref, cur_ref, prev_ref, next_ref, w_ref, b_ref, dpre_ref, dw_ref, db_ref, scr):
        i = pl.program_id(1)
        _fill_halo(scr, cur_ref, prev_ref, next_ref, i, nlx, nt)

        @pl.when(i == 0)
        def _():
            dw_ref[...] = jnp.zeros_like(dw_ref)
            db_ref[...] = jnp.zeros_like(db_ref)

        pre = jnp.broadcast_to(b_ref[...], (RT, ct))
        for k in range(SK):
            pre = pre + w_ref[k:k + 1, :] * scr[pl.ds(6 + k, RT), :]
        dpre = d_ref[...] * _dsilu(pre, _sig(pre))
        dpre_ref[...] = dpre
        db_ref[...] += jnp.sum(dpre, axis=0, keepdims=True)
        for k in range(SK):
            dw_ref[k:k + 1, :] += jnp.sum(dpre * scr[pl.ds(6 + k, RT), :], axis=0, keepdims=True)

    return pl.pallas_call(
        kern, out_shape=(S((T, 4096), f32), S((8, 4096), f32), S((1, 4096), f32)), grid=(4096 // ct, nt),
        in_specs=[cur, cur, prev, nxt, pl.BlockSpec((8, ct), lambda j, i: (0, j)), pl.BlockSpec((1, ct), lambda j, i: (0, j))],
        out_specs=(cur, pl.BlockSpec((8, ct), lambda j, i: (0, j)), pl.BlockSpec((1, ct), lambda j, i: (0, j))),
        scratch_shapes=[pltpu.VMEM((RT + 16, ct), f32)],
        compiler_params=_params("parallel", "arbitrary"), name="ssm_conv_dpre")(dxbc, proj, proj, proj, w8, b)


def _ssm_conv_t(dpre, w8, dproj, nlx):
    T = dpre.shape[0]
    nt = T // RT
    ct = 1024
    cur, prev, nxt = _halo_specs(T, ct)

    def kern(cur_ref, prev_ref, next_ref, w_ref, _alias, o_ref, scr):
        i = pl.program_id(0)
        _fill_halo(scr, cur_ref, prev_ref, next_ref, i, nlx, nt)
        acc = jnp.zeros((RT, ct), f32)
        for k in range(SK):
            acc = acc + w_ref[k:k + 1, :] * scr[pl.ds(10 - k, RT), :]
        o_ref[...] = acc.astype(bf16)

    return pl.pallas_call(
        kern, out_shape=S(dproj.shape, bf16), grid=(nt, 4096 // ct),
        in_specs=[cur, prev, nxt, pl.BlockSpec((8, ct), lambda i, j: (0, j)), pl.BlockSpec(memory_space=pl.ANY)],
        out_specs=pl.BlockSpec((RT, ct), lambda i, j: (i, j)),
        scratch_shapes=[pltpu.VMEM((RT + 16, ct), f32)], input_output_aliases={4: 0},
        compiler_params=_params("parallel", "parallel"), name="ssm_conv_t")(dpre, dpre, dpre, w8, dproj)


def _tri():
    li = lax.broadcasted_iota(jnp.int32, (Q, Q), 0)
    si = lax.broadcasted_iota(jnp.int32, (Q, Q), 1)
    return (si <= li).astype(bf16), (si >= li).astype(bf16)


def _dt_prep(proj, bias_row, alog_row):
    T = proj.shape[0]
    nch = T // Q

    def kern(raw_ref, b_ref, al_ref, dt_ref, la_ref):
        lane = lax.broadcasted_iota(jnp.int32, (Q, 128), 1)
        v = raw_ref[...] + b_ref[...]
        dt = jnp.maximum(v, 0.0) + jnp.log1p(jnp.exp(-jnp.abs(v)))
        a = jnp.where(lane[0:1, :] < 2 * NH, -jnp.exp(al_ref[...]), 0.0)
        da = dt * a
        tri, trit = _tri()
        dt_ref[...] = dt
        la_ref[...] = jnp.where(lane < NH, _dot3(tri, da), _dot3(trit, da))

    return pl.pallas_call(
        kern, out_shape=(S((T, 128), f32), S((T, 128), f32)), grid=(nch,),
        in_specs=[pl.BlockSpec((Q, 128), lambda c: (c, DT0 // 128)), _full((1, 128)), _full((1, 128))],
        out_specs=(pl.BlockSpec((Q, 128), lambda c: (c, 0)), pl.BlockSpec((Q, 128), lambda c: (c, 0))),
        compiler_params=_params("parallel"), name="dt_prep")(proj, bias_row, alog_row)


def _dt_bwd(a1, a2, r2, sv, dt, la, proj, bias_row, alog_row, dproj):
    T = proj.shape[0]
    nch = T // Q
    blk = pl.BlockSpec((Q, 128), lambda c: (c, 0))

    def kern(a1_ref, a2_ref, r2_ref, s_ref, dt_ref, la_ref, raw_ref, b_ref, al_ref, _alias, o_ref, db_ref, dal_ref):
        c = pl.program_id(0)

        @pl.when(c == 0)
        def _():
            db_ref[...] = jnp.zeros_like(db_ref)
            dal_ref[...] = jnp.zeros_like(dal_ref)

        lane = lax.broadcasted_iota(jnp.int32, (Q, 128), 1)
        row = lax.broadcasted_iota(jnp.int32, (Q, 128), 0)
        fwd = lane < NH
        dt = dt_ref[...]
        la = la_ref[...]
        a2v = a2_ref[...]
        r2v = r2_ref[...]
        a = jnp.where(lane[0:1, :] < 2 * NH, -jnp.exp(al_ref[...]), 0.0)
        la_e = jnp.where(fwd[0:1, :], la[Q - 1:Q, :], la[0:1, :])
        is_end = row == jnp.where(fwd, Q - 1, 0)
        e_end = jnp.exp(la_e - la)
        wend = e_end * dt
        extra = s_ref[0:1, :] * jnp.exp(la_e) + jnp.sum(wend * a2v, axis=0, keepdims=True)
        dla = a1_ref[...] - dt * r2v - wend * a2v + jnp.where(is_end, extra, 0.0)
        tri, trit = _tri()
        rcs = jnp.where(fwd, _dot3(trit, dla), _dot3(tri, dla))
        ddt = r2v + e_end * a2v + a * rcs
        dal_ref[...] += a * jnp.sum(dt * rcs, axis=0, keepdims=True)
        draw = jnp.where(lane < 2 * NH, ddt * _sig(raw_ref[...] + b_ref[...]), 0.0)
        db_ref[...] += jnp.sum(draw, axis=0, keepdims=True)
        o_ref[...] = jnp.zeros_like(o_ref)
        o_ref[:, 0:128] = draw.astype(bf16)

    return pl.pallas_call(
        kern, out_shape=(S(dproj.shape, bf16), S((1, 128), f32), S((1, 128), f32)), grid=(nch,),
        in_specs=[blk, blk, blk, blk, blk, blk, pl.BlockSpec((Q, 128), lambda c: (c, DT0 // 128)),
                  _full((1, 128)), _full((1, 128)), pl.BlockSpec(memory_space=pl.ANY)],
        out_specs=(pl.BlockSpec((Q, 1024), lambda c: (c, DT0 // 1024)), _full((1, 128)), _full((1, 128))),
        input_output_aliases={9: 0},
        compiler_params=_params("arbitrary"), name="dt_bwd")(a1, a2, r2, sv, dt, la, proj, bias_row, alog_row, dproj)


def _scan_consts(rev):
    li = lax.broadcasted_iota(jnp.int32, (Q, Q), 0)
    si = lax.broadcasted_iota(jnp.int32, (Q, Q), 1)
    mask = (li <= si) if rev else (li >= si)
    lane = lax.broadcasted_iota(jnp.int32, (Q, HPG * HD), 1)
    hms = [jnp.logical_and(lane >= r * HD, lane < (r + 1) * HD) for r in range(HPG)]
    return mask, hms


def _chunk_of(j, rev, nxc, nch):
    return (nch - 1 - j) if rev else lax.rem(j + nxc, nch)


def _ssd_fwd(xbc, dt, la, rev, nxc, name):
    T = xbc.shape[0]
    nch = T // Q
    hoff = NH if rev else 0
    e = 0 if rev else Q - 1
    cm = lambda j: _chunk_of(j, rev, nxc, nch)

    def kern(xbc_ref, dt_ref, la_ref, y_ref, hp_ref, h_ref):
        j = pl.program_id(0)

        @pl.when(j == 0)
        def _():
            h_ref[...] = jnp.zeros_like(h_ref)

        hp_ref[...] = h_ref[...]
        mask, hms = _scan_consts(rev)
        la_all = la_ref[...]
        dt_all = dt_ref[...]
        la_t = jnp.transpose(la_all)
        dt_t = jnp.transpose(dt_all)
        for g in range(NG):
            x = xbc_ref[:, g * 256:(g + 1) * 256]
            bb = xbc_ref[:, B0 + g * NS:B0 + (g + 1) * NS].astype(bf16)
            cb = xbc_ref[:, C0 + g * NS:C0 + (g + 1) * NS].astype(bf16)
            hg = h_ref[g * 256:(g + 1) * 256, :]
            scores = _dot_nt(cb, bb)
            yoff = _dot_nt(cb, hg.astype(bf16))
            mixes, xstack = [], []
            expla = jnp.zeros((Q, 256), f32)
            wend = jnp.zeros((Q, 256), f32)
            for r in range(HPG):
                hc = hoff + g * HPG + r
                la_c = la_all[:, hc:hc + 1]
                dt_c = dt_all[:, hc:hc + 1]
                la_r = la_t[hc:hc + 1, :]
                dt_r = dt_t[hc:hc + 1, :]
                la_e = la_r[:, e:e + 1]
                decay = jnp.exp(jnp.where(mask, la_c - la_r, NEG))
                mixes.append((scores * decay * dt_r).astype(bf16))
                xstack.append(jnp.where(hms[r], x, 0.0).astype(bf16))
                expla = jnp.where(hms[r], jnp.exp(la_c), expla)
                wend = jnp.where(hms[r], jnp.exp(la_e - la_c) * dt_c, wend)
                h_ref[g * 256 + r * HD:g * 256 + (r + 1) * HD, :] = hg[r * HD:(r + 1) * HD, :] * jnp.exp(la_e)
            y = _dot(jnp.concatenate(mixes, axis=1), jnp.concatenate(xstack, axis=0)) + yoff * expla
            y_ref[:, g * 256:(g + 1) * 256] = y
            h_ref[g * 256:(g + 1) * 256, :] += _dot_tn((x * wend).astype(bf16), bb)

    return pl.pallas_call(
        kern, out_shape=(S((T, DI), f32), S((nch, DI, NS), f32)), grid=(nch,),
        in_specs=[pl.BlockSpec((Q, 4096), lambda j: (cm(j), 0)), pl.BlockSpec((Q, 128), lambda j: (cm(j), 0)),
                  pl.BlockSpec((Q, 128), lambda j: (cm(j), 0))],
        out_specs=(pl.BlockSpec((Q, DI), lambda j: (cm(j), 0)), pl.BlockSpec((None, DI, NS), lambda j: (cm(j), 0, 0))),
        scratch_shapes=[pltpu.VMEM((DI, NS), f32)],
        compiler_params=_params("arbitrary"), name=name)(xbc, dt, la)


def _ssd_bwd(xbc, dy, dt, la, hprev, dskip_full, rev, nxc, name, acc=None):
    T = xbc.shape[0]
    nch = T // Q
    hoff = NH if rev else 0
    e = 0 if rev else Q - 1
    cm = lambda j: _chunk_of(nch - 1 - j, rev, nxc, nch)
    has_acc = acc is not None

    def kern(*refs):
        xbc_ref, dy_ref, dt_ref, la_ref, hp_ref, dsk_ref = refs[:6]
        k = 6
        if has_acc:
            dxbc_in, a1_in, a2_in, r2_in, s_in = refs[k:k + 5]
            k += 5
        dxbc_ref, a1_ref, a2_ref, r2_ref, s_ref, g_ref, r2scr, sscr = refs[k:k + 8]
        j = pl.program_id(0)

        @pl.when(j == 0)
        def _():
            g_ref[...] = jnp.zeros_like(g_ref)

        mask, hms = _scan_consts(rev)
        lane128 = lax.broadcasted_iota(jnp.int32, (Q, 128), 1)
        la_all = la_ref[...]
        dt_all = dt_ref[...]
        la_t = jnp.transpose(la_all)
        dt_t = jnp.transpose(dt_all)
        r2scr[...] = jnp.zeros_like(r2scr)
        sscr[...] = jnp.zeros_like(sscr)
        a1acc = jnp.zeros((Q, 128), f32)
        a2acc = jnp.zeros((Q, 128), f32)
        for g in range(NG):
            x = xbc_ref[:, g * 256:(g + 1) * 256]
            bb = xbc_ref[:, B0 + g * NS:B0 + (g + 1) * NS].astype(bf16)
            cb = xbc_ref[:, C0 + g * NS:C0 + (g + 1) * NS].astype(bf16)
            dyv = dy_ref[:, g * 256:(g + 1) * 256]
            gg = g_ref[g * 256:(g + 1) * 256, :]
            hg = hp_ref[g * 256:(g + 1) * 256, :]
            gb = gg.astype(bf16)
            hb = hg.astype(bf16)
            xb = x.astype(bf16)
            scores = _dot_nt(cb, bb)
            bg = _dot_nt(bb, gb)
            yoff = _dot_nt(cb, hb)
            dym = jnp.concatenate([jnp.where(hms[r], dyv, 0.0).astype(bf16) for r in range(HPG)], axis=0)
            dyx_all = _dot_nt(dym, xb)
            expla = jnp.zeros((Q, 256), f32)
            wend = jnp.zeros((Q, 256), f32)
            mixes, a1s = [], []
            wsum = jnp.zeros((Q, Q), f32)
            for r in range(HPG):
                hc = hoff + g * HPG + r
                la_c = la_all[:, hc:hc + 1]
                dt_c = dt_all[:, hc:hc + 1]
                la_r = la_t[hc:hc + 1, :]
                dt_r = dt_t[hc:hc + 1, :]
                la_e = la_r[:, e:e + 1]
                decay = jnp.exp(jnp.where(mask, la_c - la_r, NEG))
                dyx = dyx_all[r * Q:(r + 1) * Q, :]
                sd = scores * decay
                fm = dyx * sd
                r2scr[hc:hc + 1, :] = jnp.sum(fm, axis=0, keepdims=True)
                a1s.append(jnp.sum(fm * dt_r, axis=1, keepdims=True))
                wsum = wsum + dyx * decay * dt_r
                mixes.append((sd * dt_r).astype(bf16))
                expla = jnp.where(hms[r], jnp.exp(la_c), expla)
                wend = jnp.where(hms[r], jnp.exp(la_e - la_c) * dt_c, wend)
                gh = gg[r * HD:(r + 1) * HD, :]
                sval = jnp.sum(jnp.sum(gh * hg[r * HD:(r + 1) * HD, :], axis=1, keepdims=True), axis=0, keepdims=True)
                sscr[hc:hc + 1, :] = jnp.broadcast_to(sval, (1, Q))
                g_ref[g * 256 + r * HD:g * 256 + (r + 1) * HD, :] = gh * jnp.exp(la_e)
            tm = dyv * yoff * expla
            um = x * bg
            for r in range(HPG):
                hc = hoff + g * HPG + r
                a1 = a1s[r] + jnp.sum(jnp.where(hms[r], tm, 0.0), axis=1, keepdims=True)
                qv = jnp.sum(jnp.where(hms[r], um, 0.0), axis=1, keepdims=True)
                a1acc = jnp.where(lane128 == hc, a1, a1acc)
                a2acc = jnp.where(lane128 == hc, qv, a2acc)
            dx = _dot_tn(jnp.concatenate(mixes, axis=0), dym) + wend * bg
            if not has_acc:
                dx = dx + dsk_ref[:, g * 256:(g + 1) * 256] * dyv
            wb = wsum.astype(bf16)
            dysb = (dyv * expla).astype(bf16)
            dc = _dot(wb, bb) + _dot(dysb, hb)
            db = _dot_tn(wb, cb) + _dot((x * wend).astype(bf16), gb)
            g_ref[g * 256:(g + 1) * 256, :] += _dot_tn(dysb, cb)
            if has_acc:
                dx = dx + dxbc_in[:, g * 256:(g + 1) * 256]
                db = db + dxbc_in[:, B0 + g * NS:B0 + (g + 1) * NS]
                dc = dc + dxbc_in[:, C0 + g * NS:C0 + (g + 1) * NS]
            dxbc_ref[:, g * 256:(g + 1) * 256] = dx
            dxbc_ref[:, B0 + g * NS:B0 + (g + 1) * NS] = db
            dxbc_ref[:, C0 + g * NS:C0 + (g + 1) * NS] = dc
        r2c = jnp.transpose(r2scr[...])
        sc = jnp.transpose(sscr[...])
        if has_acc:
            a1acc = a1acc + a1_in[...]
            a2acc = a2acc + a2_in[...]
            r2c = r2c + r2_in[...]
            sc = sc + s_in[...]
        a1_ref[...] = a1acc
        a2_ref[...] = a2acc
        r2_ref[...] = r2c
        s_ref[...] = sc

    blk = pl.BlockSpec((Q, 128), lambda j: (cm(j), 0))
    big = pl.BlockSpec((Q, 4096), lambda j: (cm(j), 0))
    in_specs = [big, pl.BlockSpec((Q, DI), lambda j: (cm(j), 0)), blk, blk,
                pl.BlockSpec((None, DI, NS), lambda j: (cm(j), 0, 0)), _full((1, DI))]
    args = [xbc, dy, dt, la, hprev, dskip_full]
    aliases = {}
    if has_acc:
        in_specs += [big, blk, blk, blk, blk]
        args += list(acc)
        aliases = {6: 0, 7: 1, 8: 2, 9: 3, 10: 4}
    return pl.pallas_call(
        kern, out_shape=(S((T, 4096), f32), S((T, 128), f32), S((T, 128), f32), S((T, 128), f32), S((T, 128), f32)),
        grid=(nch,), in_specs=in_specs, out_specs=(big, blk, blk, blk, blk),
        scratch_shapes=[pltpu.VMEM((DI, NS), f32), pltpu.VMEM((128, Q), f32), pltpu.VMEM((128, Q), f32)],
        input_output_aliases=aliases,
        compiler_params=_params("arbitrary"), name=name)(*args)


def _ynorm_fwd(yf, yb, xbc, proj, dskip_full, nw, L):
    nlx = L // RT

    def kern(yf_ref, yb_ref, xs_ref, z_ref, dsk_ref, nw_ref, y_ref, yn_ref, ynt_ref):
        y = yf_ref[...] + yb_ref[...] + dsk_ref[...] * xs_ref[...]
        y_ref[...] = y
        yz = y * _silu(z_ref[...])
        for g in range(NG):
            sl = yz[:, g * 256:(g + 1) * 256]
            r = lax.rsqrt(jnp.mean(sl * sl, axis=1, keepdims=True) + EPS)
            yn = (sl * r) * nw_ref[:, g * 256:(g + 1) * 256]
            yn_ref[:, g * 256:(g + 1) * 256] = yn.astype(bf16)
            ynt_ref[g * 256:(g + 1) * 256, :] = jnp.transpose(yn).astype(bf16)

    blk = pl.BlockSpec((RT, DI), lambda i: (i, 0))
    return pl.pallas_call(
        kern, out_shape=(S((L, DI), f32), S((L, DI), bf16), S((DI, L), bf16)), grid=(nlx,),
        in_specs=[blk, blk, blk, pl.BlockSpec((RT, DI), lambda i: (i, Z0 // DI)), _full((1, DI)), _full((1, DI))],
        out_specs=(blk, blk, pl.BlockSpec((DI, RT), lambda i: (0, i))),
        compiler_params=_params("parallel"), name="ynorm_fwd")(yf, yb, xbc, proj, dskip_full, nw)


def _ynorm_bwd(dyn, y, xbc, proj, dskip_full, nw, dproj):
    L = y.shape[0]
    T = proj.shape[0]
    nlx, nt = L // RT, T // RT

    def kern(dyn_ref, y_ref, xs_ref, z_ref, dsk_ref, nw_ref, _alias, dz_ref, dy_ref, dnw_ref, dsk_acc):
        i = pl.program_id(0)

        @pl.when(i == 0)
        def _():
            dnw_ref[...] = jnp.zeros_like(dnw_ref)
            dsk_acc[...] = jnp.zeros_like(dsk_acc)

        @pl.when(i >= nlx)
        def _():
            dz_ref[...] = jnp.zeros_like(dz_ref)
            dy_ref[...] = jnp.zeros_like(dy_ref)

        @pl.when(i < nlx)
        def _():
            y = y_ref[...]
            z = z_ref[...]
            sz = _sig(z)
            gz = z * sz
            yz = y * gz
            dynv = dyn_ref[...]
            for g in range(NG):
                cs = slice(g * 256, (g + 1) * 256)
                sl = yz[:, cs]
                r = lax.rsqrt(jnp.mean(sl * sl, axis=1, keepdims=True) + EPS)
                yhat = sl * r
                dn = dynv[:, cs]
                dnw_ref[:, cs] += jnp.sum(dn * yhat, axis=0, keepdims=True)
                dyh = dn * nw_ref[:, cs]
                dyz = r * (dyh - yhat * jnp.mean(dyh * yhat, axis=1, keepdims=True))
                dyv = dyz * gz[:, cs]
                dy_ref[:, cs] = dyv
                dz_ref[:, cs] = (dyz * y[:, cs] * _dsilu(z[:, cs], sz[:, cs])).astype(bf16)
                dsk_acc[:, cs] += jnp.sum(dyv * xs_ref[:, cs], axis=0, keepdims=True)

    xmap = lambda i: (jnp.minimum(i, nlx - 1), 0)
    return pl.pallas_call(
        kern, out_shape=(S(dproj.shape, bf16), S((T, DI), f32), S((1, DI), f32), S((1, DI), f32)), grid=(nt,),
        in_specs=[pl.BlockSpec((RT, DI), xmap), pl.BlockSpec((RT, DI), xmap), pl.BlockSpec((RT, DI), xmap),
                  pl.BlockSpec((RT, DI), lambda i: (jnp.minimum(i, nlx - 1), Z0 // DI)), _full((1, DI)), _full((1, DI)),
                  pl.BlockSpec(memory_space=pl.ANY)],
        out_specs=(pl.BlockSpec((RT, DI), lambda i: (i, Z0 // DI)), pl.BlockSpec((RT, DI), lambda i: (i, 0)),
                   _full((1, DI)), _full((1, DI))),
        input_output_aliases={6: 0},
        compiler_params=_params("arbitrary"), name="ynorm_bwd")(dyn, y, xbc, proj, dskip_full, nw, dproj)


def _head_sums(cols):
    def kern(c_ref, o_ref):
        o_ref[...] = jnp.broadcast_to(jnp.sum(c_ref[...], axis=1, keepdims=True), (NH, 128))

    return pl.pallas_call(kern, out_shape=S((NH, 128), f32), name="head_sums")(cols)


SEG_STRIDE = 96
SEG_PAD = 16
NSEG = RT // GRID_W
CONF_ROWS = SEG_PAD + NSEG * SEG_STRIDE


def _seg_fill(scr, val):
    scr[...] = jnp.zeros_like(scr)
    for s in range(NSEG):
        scr[SEG_PAD + s * SEG_STRIDE:SEG_PAD + s * SEG_STRIDE + GRID_W, :] = val[s * GRID_W:(s + 1) * GRID_W, :]


def _conf_fwd(proj, w32, cb, lnw, lnb, L):
    nlx = L // RT

    def kern(v_ref, g_ref, cg_ref, w_ref, cb_ref, lnw_ref, lnb_ref, u1_ref, u3_ref, u3t_ref, scr):
        _seg_fill(scr, v_ref[...] * _sig(g_ref[...]))
        for s in range(NSEG):
            for cc in range(D // 256):
                cs = slice(cc * 256, (cc + 1) * 256)
                acc = jnp.broadcast_to(cb_ref[:, cs], (GRID_W, 256))
                for k in range(CK):
                    acc = acc + w_ref[k:k + 1, cs] * scr[pl.ds(SEG_PAD + s * SEG_STRIDE + k - CK // 2, GRID_W), cs]
                u1_ref[s * GRID_W:(s + 1) * GRID_W, cs] = acc
        u1 = u1_ref[...]
        mu = jnp.mean(u1, axis=1, keepdims=True)
        xc = u1 - mu
        r = lax.rsqrt(jnp.mean(xc * xc, axis=1, keepdims=True) + EPS)
        u2 = (xc * r) * lnw_ref[...] + lnb_ref[...]
        u3 = _silu(u2) * _silu(cg_ref[...])
        u3_ref[...] = u3.astype(bf16)
        u3t_ref[...] = jnp.transpose(u3).astype(bf16)

    blk = pl.BlockSpec((RT, D), lambda i: (i, 0))
    return pl.pallas_call(
        kern, out_shape=(S((L, D), f32), S((L, D), bf16), S((D, L), bf16)), grid=(nlx,),
        in_specs=[pl.BlockSpec((RT, D), lambda i: (i, GV0 // D)), pl.BlockSpec((RT, D), lambda i: (i, GG0 // D)),
                  pl.BlockSpec((RT, D), lambda i: (i, CG0 // D)), _full((32, D)), _full((1, D)), _full((1, D)), _full((1, D))],
        out_specs=(blk, blk, pl.BlockSpec((D, RT), lambda i: (0, i))), scratch_shapes=[pltpu.VMEM((CONF_ROWS, D), f32)],
        compiler_params=_params("parallel"), name="conf_fwd")(proj, proj, proj, w32, cb, lnw, lnb)


def _conf_bwd(du3, u1, proj, w32, lnw, lnb, dproj):
    L = u1.shape[0]
    T = proj.shape[0]
    nlx, nt = L // RT, T // RT

    def kern(du3_ref, u1_ref, v_ref, g_ref, cg_ref, w_ref, lnw_ref, lnb_ref, _alias,
             o_ref, dw_ref, dcb_ref, dlw_ref, dlb_ref, scr_u, scr_d, du0_scr):
        i = pl.program_id(0)

        @pl.when(i == 0)
        def _():
            dw_ref[...] = jnp.zeros_like(dw_ref)
            dcb_ref[...] = jnp.zeros_like(dcb_ref)
            dlw_ref[...] = jnp.zeros_like(dlw_ref)
            dlb_ref[...] = jnp.zeros_like(dlb_ref)

        @pl.when(i >= nlx)
        def _():
            o_ref[...] = jnp.zeros_like(o_ref)

        @pl.when(i < nlx)
        def _():
            val = v_ref[...]
            sg = _sig(g_ref[...])
            cg = cg_ref[...]
            scg = _sig(cg)
            u1 = u1_ref[...]
            mu = jnp.mean(u1, axis=1, keepdims=True)
            xc = u1 - mu
            r = lax.rsqrt(jnp.mean(xc * xc, axis=1, keepdims=True) + EPS)
            xhat = xc * r
            u2 = xhat * lnw_ref[...] + lnb_ref[...]
            s2 = _sig(u2)
            du3v = du3_ref[...]
            du2 = du3v * (cg * scg) * _dsilu(u2, s2)
            o_ref[:, 2 * D:3 * D] = (du3v * (u2 * s2) * _dsilu(cg, scg)).astype(bf16)
            dlw_ref[...] += jnp.sum(du2 * xhat, axis=0, keepdims=True)
            dlb_ref[...] += jnp.sum(du2, axis=0, keepdims=True)
            dxh = du2 * lnw_ref[...]
            du1 = r * (dxh - jnp.mean(dxh, axis=1, keepdims=True) - xhat * jnp.mean(dxh * xhat, axis=1, keepdims=True))
            dcb_ref[...] += jnp.sum(du1, axis=0, keepdims=True)
            _seg_fill(scr_u, val * sg)
            _seg_fill(scr_d, du1)
            for cc in range(D // 256):
                cs = slice(cc * 256, (cc + 1) * 256)
                for k in range(CK):
                    t = jnp.zeros((GRID_W, 256), f32)
                    for s in range(NSEG):
                        base = SEG_PAD + s * SEG_STRIDE
                        t = t + scr_d[pl.ds(base, GRID_W), cs] * scr_u[pl.ds(base + k - CK // 2, GRID_W), cs]
                    dw_ref[k:k + 1, cs] += jnp.sum(t, axis=0, keepdims=True)
                for s in range(NSEG):
                    base = SEG_PAD + s * SEG_STRIDE
                    acc = jnp.zeros((GRID_W, 256), f32)
                    for k in range(CK):
                        acc = acc + w_ref[k:k + 1, cs] * scr_d[pl.ds(base + CK // 2 - k, GRID_W), cs]
                    du0_scr[s * GRID_W:(s + 1) * GRID_W, cs] = acc
            du0 = du0_scr[...]
            o_ref[:, 0:D] = (du0 * sg).astype(bf16)
            o_ref[:, D:2 * D] = (du0 * val * sg * (1.0 - sg)).astype(bf16)

    xmap = lambda i: (jnp.minimum(i, nlx - 1), 0)
    pmap = lambda cb: (lambda i: (jnp.minimum(i, nlx - 1), cb))
    return pl.pallas_call(
        kern, out_shape=(S(dproj.shape, bf16), S((32, D), f32), S((1, D), f32), S((1, D), f32), S((1, D), f32)), grid=(nt,),
        in_specs=[pl.BlockSpec((RT, D), xmap), pl.BlockSpec((RT, D), xmap),
                  pl.BlockSpec((RT, D), pmap(GV0 // D)), pl.BlockSpec((RT, D), pmap(GG0 // D)), pl.BlockSpec((RT, D), pmap(CG0 // D)),
                  _full((32, D)), _full((1, D)), _full((1, D)), pl.BlockSpec(memory_space=pl.ANY)],
        out_specs=(pl.BlockSpec((RT, 3 * D), lambda i: (i, GV0 // (3 * D))), _full((32, D)), _full((1, D)), _full((1, D)), _full((1, D))),
        scratch_shapes=[pltpu.VMEM((CONF_ROWS, D), f32), pltpu.VMEM((CONF_ROWS, D), f32), pltpu.VMEM((RT, D), f32)],
        input_output_aliases={8: 0},
        compiler_params=_params("arbitrary"), name="conf_bwd")(du3, u1, proj, proj, proj, w32, lnw, lnb, dproj)


def _merge_fwd(bs, bc, proj):
    L = bs.shape[0]

    def kern(bs_ref, bc_ref, g1_ref, g2_ref, o_ref, ot_ref):
        mv = _sig(g1_ref[...]) * bs_ref[...] + _sig(g2_ref[...]) * bc_ref[...]
        o_ref[...] = mv.astype(bf16)
        ot_ref[...] = jnp.transpose(mv).astype(bf16)

    blk = pl.BlockSpec((RT, D), lambda i: (i, 0))
    return pl.pallas_call(
        kern, out_shape=(S((L, D), bf16), S((D, L), bf16)), grid=(L // RT,),
        in_specs=[blk, blk, pl.BlockSpec((RT, D), lambda i: (i, G10 // D)), pl.BlockSpec((RT, D), lambda i: (i, G20 // D))],
        out_specs=(blk, pl.BlockSpec((D, RT), lambda i: (0, i))),
        compiler_params=_params("parallel"), name="merge_fwd")(bs, bc, proj, proj)


def _merge_bwd(dmerged, bs, bc, proj):
    L = bs.shape[0]
    T = proj.shape[0]
    nlx, nt = L // RT, T // RT

    def kern(dm_ref, bs_ref, bc_ref, g1_ref, g2_ref, o_ref, dbs_ref, dbc_ref):
        i = pl.program_id(0)

        @pl.when(i >= nlx)
        def _():
            o_ref[...] = jnp.zeros_like(o_ref)

        @pl.when(i < nlx)
        def _():
            dm = dm_ref[...]
            s1 = _sig(g1_ref[...])
            s2 = _sig(g2_ref[...])
            dbs_ref[...] = (dm * s1).astype(bf16)
            dbc_ref[...] = (dm * s2).astype(bf16)
            o_ref[:, 0:D] = (dm * bs_ref[...] * s1 * (1.0 - s1)).astype(bf16)
            o_ref[:, D:2 * D] = (dm * bc_ref[...] * s2 * (1.0 - s2)).astype(bf16)

    xmap = lambda i: (jnp.minimum(i, nlx - 1), 0)
    pmap = lambda cb: (lambda i: (jnp.minimum(i, nlx - 1), cb))
    xblk = pl.BlockSpec((RT, D), xmap)
    return pl.pallas_call(
        kern, out_shape=(S((T, NP), bf16), S((L, D), bf16), S((L, D), bf16)), grid=(nt,),
        in_specs=[xblk, xblk, xblk, pl.BlockSpec((RT, D), pmap(G10 // D)), pl.BlockSpec((RT, D), pmap(G20 // D))],
        out_specs=(pl.BlockSpec((RT, 2 * D), lambda i: (i, G10 // (2 * D))), xblk, xblk),
        compiler_params=_params("arbitrary"), name="merge_bwd")(dmerged, bs, bc, proj, proj)


def _final(x, out, target, mod, fw):
    L = x.shape[0]

    def kern(x_ref, o_ref, t_ref, mod_ref, fw_ref, dx1_ref, dout_ref, loss_ref, dfw_ref, dg_ref):
        i = pl.program_id(0)

        @pl.when(i == 0)
        def _():
            loss_ref[...] = jnp.zeros_like(loss_ref)
            dfw_ref[...] = jnp.zeros_like(dfw_ref)
            dg_ref[...] = jnp.zeros_like(dg_ref)

        gate = mod_ref[0:1, 2 * D:3 * D]
        ov = o_ref[...]
        x1 = x_ref[...] + gate * ov
        r = lax.rsqrt(jnp.mean(x1 * x1, axis=1, keepdims=True) + EPS)
        xn = x1 * r
        fw = fw_ref[...]
        err = xn * fw - t_ref[...]
        part = 0.5 * jnp.sum(jnp.mean(err * err, axis=1, keepdims=True), axis=0, keepdims=True)
        loss_ref[...] += jnp.broadcast_to(part, (8, 128))
        dy = err * (1.0 / D)
        dfw_ref[...] += jnp.sum(dy * xn, axis=0, keepdims=True)
        dyw = dy * fw
        dx1 = r * (dyw - xn * jnp.mean(dyw * xn, axis=1, keepdims=True))
        dx1_ref[...] = dx1
        dout_ref[...] = (gate * dx1).astype(bf16)
        dg_ref[...] += jnp.sum(dx1 * ov, axis=0, keepdims=True)

    blk = pl.BlockSpec((RT, D), lambda i: (i, 0))
    return pl.pallas_call(
        kern, out_shape=(S((L, D), f32), S((L, D), bf16), S((8, 128), f32), S((1, D), f32), S((1, D), f32)), grid=(L // RT,),
        in_specs=[blk, blk, blk, _full((8, 3 * D)), _full((1, D))],
        out_specs=(blk, blk, _full((8, 128)), _full((1, D)), _full((1, D))),
        compiler_params=_params("arbitrary"), name="final")(x, out, target, mod, fw)


def _me():
    return 4 * lax.axis_index("x") + 2 * lax.axis_index("y") + lax.axis_index("c")


def _xchg_copy(ins, outs, send_sems, recv_sems, modes, a, k, me):
    peer = lax.rem(me + k, N_DEV)
    pid = (peer // 4, lax.rem(peer // 2, 2), lax.rem(peer, 2))
    src = ins[a].at[peer] if modes[a] else ins[a]
    return pltpu.make_async_remote_copy(src_ref=src, dst_ref=outs[a].at[me], send_sem=send_sems.at[a, k - 1],
                                        recv_sem=recv_sems.at[a, k - 1], device_id=pid, device_id_type=MESH)


def _xchg_local(ins, outs, loc_sems, modes, a, me):
    return pltpu.make_async_copy(ins[a].at[me] if modes[a] else ins[a], outs[a].at[me], loc_sems.at[a])


def _xchg_start(ins, outs, send_sems, recv_sems, loc_sems, modes):
    me = _me()
    for a in range(len(modes)):
        _xchg_local(ins, outs, loc_sems, modes, a, me).start()
        for k in range(1, N_DEV):
            _xchg_copy(ins, outs, send_sems, recv_sems, modes, a, k, me).start()


def _xchg_wait(ins, outs, send_sems, recv_sems, loc_sems, modes):
    me = _me()
    for a in range(len(modes)):
        for k in range(1, N_DEV):
            frm = lax.rem(me + N_DEV - k, N_DEV)
            src = ins[a].at[frm] if modes[a] else ins[a]
            pltpu.make_async_remote_copy(src_ref=src, dst_ref=outs[a].at[frm], send_sem=send_sems.at[a, k - 1],
                                         recv_sem=recv_sems.at[a, k - 1], device_id=(0, 0, 0), device_id_type=MESH).wait_recv()
    for a in range(len(modes)):
        for k in range(1, N_DEV):
            _xchg_copy(ins, outs, send_sems, recv_sems, modes, a, k, me).wait_send()
        _xchg_local(ins, outs, loc_sems, modes, a, me).wait()


def _xchg_out_shapes(arrs, modes):
    return tuple(S((N_DEV,) + (a.shape[1:] if sc else a.shape), a.dtype) for a, sc in zip(arrs, modes))


def _xchg_sems(n):
    return [pltpu.SemaphoreType.DMA((n, N_DEV - 1)), pltpu.SemaphoreType.DMA((n, N_DEV - 1)), pltpu.SemaphoreType.DMA((n,))]


def _exchange(arrs, modes, name):
    n = len(arrs)

    def kern(*refs):
        ins, outs, sems = refs[:n], refs[n:2 * n], refs[2 * n:]
        _xchg_start(ins, outs, *sems, modes)
        _xchg_wait(ins, outs, *sems, modes)

    anyspec = pl.BlockSpec(memory_space=pl.ANY)
    return pl.pallas_call(
        kern, out_shape=_xchg_out_shapes(arrs, modes), in_specs=[anyspec] * n, out_specs=tuple([anyspec] * n),
        scratch_shapes=_xchg_sems(n), name=name)(*arrs)


def _gather2(arrs, name):
    n = len(arrs)

    def kern(*refs):
        ins, outs = refs[:n], refs[n:2 * n]
        send_sems, recv_sems, loc_sems = refs[2 * n:]
        x, y, c = lax.axis_index("x"), lax.axis_index("y"), lax.axis_index("c")
        me, sib = (x, y, c), (x, y, 1 - c)
        chips = [(1 - x, y), (x, 1 - y), (1 - x, 1 - y)]

        def slot(a, p):
            return outs[a].at[4 * p[0] + 2 * p[1] + p[2]]

        def cp(a, k, block, to, own=False):
            return pltpu.make_async_remote_copy(src_ref=ins[a] if own else slot(a, block), dst_ref=slot(a, block),
                                                send_sem=send_sems.at[a, k], recv_sem=recv_sems.at[a, k],
                                                device_id=to, device_id_type=MESH)

        started = []
        for a in range(n):
            loc = pltpu.make_async_copy(ins[a], slot(a, me), loc_sems.at[a])
            loc.start()
            started.append(cp(a, 0, me, sib, own=True))
            started += [cp(a, 1 + j, me, (*chip, c), own=True) for j, chip in enumerate(chips)]
        for s in started:
            s.start()
        for j, chip in enumerate(chips):
            for a in range(n):
                cp(a, 1 + j, (*chip, c), me).wait_recv()
                fwd = cp(a, 4 + j, (*chip, c), sib)
                fwd.start()
                started.append(fwd)
        for a in range(n):
            cp(a, 0, sib, me).wait_recv()
            for j, chip in enumerate(chips):
                cp(a, 4 + j, (*chip, 1 - c), me).wait_recv()
        for s in started:
            s.wait_send()
        for a in range(n):
            pltpu.make_async_copy(ins[a], slot(a, me), loc_sems.at[a]).wait()

    anyspec = pl.BlockSpec(memory_space=pl.ANY)
    return pl.pallas_call(
        kern, out_shape=_xchg_out_shapes(arrs, (False,) * n), in_specs=[anyspec] * n, out_specs=tuple([anyspec] * n),
        scratch_shapes=[pltpu.SemaphoreType.DMA((n, 7)), pltpu.SemaphoreType.DMA((n, 7)), pltpu.SemaphoreType.DMA((n,))],
        name=name)(*arrs)


def _adamw(parts, w, m, v, name):
    r, c = w.shape
    tr = r
    for cand in (128, 64, 32, 16, 8):
        if r % cand == 0 and r > cand:
            tr = cand
            break
    c1 = 1.0 / (1.0 - ADAM_B1 ** ADAM_STEP)
    c2 = 1.0 / (1.0 - ADAM_B2 ** ADAM_STEP)

    def kern(p_ref, w_ref, m_ref, v_ref, g_ref, d_ref, m2_ref, v2_ref):
        g = p_ref[0].astype(f32)
        for i in range(1, N_DEV):
            g = g + p_ref[i].astype(f32)
        g_ref[...] = g
        m2 = ADAM_B1 * m_ref[...] + (1.0 - ADAM_B1) * g
        v2 = ADAM_B2 * v_ref[...] + (1.0 - ADAM_B2) * (g * g)
        m2_ref[...] = m2
        v2_ref[...] = v2
        d_ref[...] = -ADAM_LR * ((m2 * c1) / (jnp.sqrt(v2 * c2) + ADAM_EPS) + ADAM_WD * w_ref[...])

    blk = pl.BlockSpec((tr, c), lambda i: (i, 0))
    sh = S((r, c), f32)
    return pl.pallas_call(
        kern, out_shape=(sh, sh, sh, sh), grid=(r // tr,),
        in_specs=[pl.BlockSpec((N_DEV, tr, c), lambda i: (0, i, 0)), blk, blk, blk], out_specs=(blk, blk, blk, blk),
        compiler_params=_params("parallel"), name=name)(parts, w, m, v)


_SMALL = (("c_ctx", 1024), ("b_mod", 3072), ("norm_w", 1024), ("ssm_conv_b", 4096), ("dt_bias", 64), ("a_log", 64),
          ("d_skip", 32), ("ssm_norm_w", 2048), ("conf_conv_b", 1024), ("conf_ln_w", 1024), ("conf_ln_b", 1024),
          ("final_norm_w", 1024))
SMALL_TILE = 8 * 128


def _pack_small(d):
    rows = []
    for name, n in _SMALL:
        v = d[name].reshape(-1).astype(f32)
        pad = (-n) % SMALL_TILE
        if pad:
            v = jnp.concatenate([v, jnp.zeros((pad,), f32)])
        rows.append(v.reshape(-1, 128))
    return jnp.concatenate(rows, axis=0)


def _unpack_small(p, shapes):
    out, r0 = {}, 0
    for name, n in _SMALL:
        nr = 8 * ((n + SMALL_TILE - 1) // SMALL_TILE)
        out[name] = p[r0:r0 + nr].reshape(-1)[:n].reshape(shapes[name])
        r0 += nr
    return out


def _permute_w_in(w):
    return jnp.concatenate([w[:, 0:4096], w[:, 4160:6208], w[:, 9280:11328], w[:, 4096:4160],
                            jnp.zeros((w.shape[0], 1024 - 64), w.dtype), w[:, 6208:9280]], axis=1)


def _unpermute_w_in(wp):
    return jnp.concatenate([wp[:, 0:4096], wp[:, DT0:DT0 + 64], wp[:, Z0:Z0 + 2048], wp[:, GV0:GV0 + 3072],
                            wp[:, G10:G10 + 2048]], axis=1)


def _cols_gathered(g):
    return jnp.transpose(g, (1, 0, 2)).reshape(g.shape[1], N_DEV * g.shape[2])


def _cols_to_blocks(a):
    r, c8 = a.shape
    return jnp.transpose(a.reshape(r, N_DEV, c8 // N_DEV), (1, 0, 2))


def kernel(x, c, ctx, c_ctx, w_mod, b_mod, norm_w, w_in, ssm_conv_w, ssm_conv_b, dt_bias, a_log, d_skip, ssm_norm_w, w_out_ssm, conf_conv_w, conf_conv_b, conf_ln_w, conf_ln_b, w_out_conf, w_out, final_norm_w, loss_target, m_c_ctx, m_w_mod, m_b_mod, m_norm_w, m_w_in, m_ssm_conv_w, m_ssm_conv_b, m_dt_bias, m_a_log, m_d_skip, m_ssm_norm_w, m_w_out_ssm, m_conf_conv_w, m_conf_conv_b, m_conf_ln_w, m_conf_ln_b, m_w_out_conf, m_w_out, m_final_norm_w, v_c_ctx, v_w_mod, v_b_mod, v_norm_w, v_w_in, v_ssm_conv_w, v_ssm_conv_b, v_dt_bias, v_a_log, v_d_skip, v_ssm_norm_w, v_w_out_ssm, v_conf_conv_w, v_conf_conv_b, v_conf_ln_w, v_conf_ln_b, v_w_out_conf, v_w_out, v_final_norm_w):
    L = x.shape[1]
    Lc = ctx.shape[1]
    T = L + Lc
    nlx = L // RT
    nxc = L // Q
    x2 = x.reshape(L, D)
    ctx2 = ctx.reshape(Lc, D)
    tgt = loss_target.reshape(L, D)

    gathered = _gather2(
        [w_in[0].astype(bf16), w_mod[0].astype(bf16), w_out_ssm[0].astype(bf16), w_out_conf[0].astype(bf16),
         w_out[0].astype(bf16), ssm_conv_w[0], conf_conv_w[0]], name="gather_weights")
    wp = _permute_w_in(_cols_gathered(gathered[0]))
    wmod_bf = _cols_gathered(gathered[1])
    wos_bf = gathered[2].reshape(DI, D)
    woc_bf = gathered[3].reshape(D, D)
    wo_bf = gathered[4].reshape(D, D)
    scw8 = jnp.concatenate([_cols_gathered(gathered[5]), jnp.zeros((8 - SK, 4096), f32)], axis=0)
    ccw32 = jnp.concatenate([_cols_gathered(gathered[6]), jnp.zeros((32 - CK, D), f32)], axis=0)

    norm_w1 = norm_w.reshape(1, D)
    scb = ssm_conv_b.reshape(1, 4096)
    bias_row = jnp.concatenate([dt_bias.reshape(1, 2 * NH), jnp.zeros((1, 128 - 2 * NH), f32)], axis=1)
    alog_row = jnp.concatenate([a_log.reshape(1, 2 * NH), jnp.zeros((1, 128 - 2 * NH), f32)], axis=1)
    dskip_full = jnp.repeat(d_skip.reshape(NH), HD).reshape(1, DI)
    snw = ssm_norm_w.reshape(1, DI)
    ccb = conf_conv_b.reshape(1, D)
    lnw = conf_ln_w.reshape(1, D)
    lnb = conf_ln_b.reshape(1, D)
    fw = final_norm_w.reshape(1, D)

    cc8 = jnp.concatenate([c.reshape(1, D), c_ctx.reshape(1, D), jnp.zeros((6, D), f32)], axis=0)
    mod = _mod_fwd(cc8, wmod_bf, b_mod.reshape(1, 3 * D))
    h, h_t = _prenorm(x2, ctx2, norm_w1, mod)
    proj = _matmul(h, wp, f32, "proj")
    xbc = _ssm_conv_fwd(proj, scw8, scb, nlx)
    dt, la = _dt_prep(proj, bias_row, alog_row)
    yf, hp_f = _ssd_fwd(xbc, dt, la, False, nxc, "ssd_fwd_f")
    yb, hp_b = _ssd_fwd(xbc, dt, la, True, nxc, "ssd_fwd_b")
    y, yn, yn_t = _ynorm_fwd(yf, yb, xbc, proj, dskip_full, snw, L)
    bs = _matmul(yn, wos_bf, f32, "branch_ssm")
    u1, u3, u3_t = _conf_fwd(proj, ccw32, ccb, lnw, lnb, L)
    bc = _matmul(u3, woc_bf, f32, "branch_conf")
    merged, merged_t = _merge_fwd(bs, bc, proj)
    out = _matmul(merged, wo_bf, f32, "out_proj")
    dx1, dout, loss_acc, dfw, dgate = _final(x2, out, tgt, mod, fw)

    dmerged = _matmul(dout, wo_bf, f32, "d_merged", tb=True)
    g_wo = _matmul(merged_t, dout, bf16, "g_w_out")
    dproj, dbs, dbc = _merge_bwd(dmerged, bs, bc, proj)
    dyn = _matmul(dbs, wos_bf, f32, "d_yn", tb=True)
    g_wos = _matmul(yn_t, dbs, bf16, "g_w_out_ssm")
    du3 = _matmul(dbc, woc_bf, f32, "d_u3", tb=True)
    g_woc = _matmul(u3_t, dbc, bf16, "g_w_out_conf")
    dproj, g_ccw, g_ccb, g_lnw, g_lnb = _conf_bwd(du3, u1, proj, ccw32, lnw, lnb, dproj)
    dproj, dy, g_snw, dsk_cols = _ynorm_bwd(dyn, y, xbc, proj, dskip_full, snw, dproj)
    acc_f = _ssd_bwd(xbc, dy, dt, la, hp_f, dskip_full, False, nxc, "ssd_bwd_f")
    dxbc, a1, a2, r2, sv = _ssd_bwd(xbc, dy, dt, la, hp_b, dskip_full, True, nxc, "ssd_bwd_b", acc=acc_f)
    dproj, g_dtb, g_alog = _dt_bwd(a1, a2, r2, sv, dt, la, proj, bias_row, alog_row, dproj)
    dpre, g_scw, g_scb = _ssm_conv_dpre(dxbc, proj, scw8, scb, nlx)
    dproj = _ssm_conv_t(dpre, scw8, dproj, nlx)
    g_wp = _matmul(h_t, dproj, bf16, "g_w_in", tm=1024)
    dh, *parts = _matmul(
        dproj, wp, f32, "d_h_scatter", tb=True,
        comm=([_cols_to_blocks(_unpermute_w_in(g_wp)), g_wos.reshape(N_DEV, DI // N_DEV, D), g_woc.reshape(N_DEV, D // N_DEV, D),
               g_wo.reshape(N_DEV, D // N_DEV, D), _cols_to_blocks(g_scw[:SK]), _cols_to_blocks(g_ccw[:CK])], (True,) * 6))
    gx, g_nw, macc = _prenorm_bwd(x2, ctx2, dh, dx1, norm_w1, mod)
    z1 = jnp.zeros((1, D), f32)
    dmod8 = jnp.concatenate([jnp.concatenate([macc[0:1], macc[1:2], dgate], axis=1),
                             jnp.concatenate([macc[2:3], macc[3:4], z1], axis=1), jnp.zeros((6, 3 * D), f32)], axis=0)
    ct = jnp.concatenate([c.reshape(D, 1), c_ctx.reshape(D, 1), jnp.zeros((D, 126), f32)], axis=1)
    g_wmod, g_bmod, g_cctx = _mod_bwd(ct, dmod8, wmod_bf)
    g_dskip = _head_sums(dsk_cols.reshape(NH, HD))[:, 0]

    small_g = _pack_small({
        "c_ctx": g_cctx[:, 0], "b_mod": g_bmod, "norm_w": g_nw, "ssm_conv_b": g_scb, "dt_bias": g_dtb[0, :2 * NH],
        "a_log": g_alog[0, :2 * NH], "d_skip": g_dskip, "ssm_norm_w": g_snw, "conf_conv_b": g_ccb, "conf_ln_w": g_lnw,
        "conf_ln_b": g_lnb, "final_norm_w": dfw})
    wmod_parts, small_parts = _exchange([_cols_to_blocks(g_wmod), small_g], (True, False), name="exchange_tail")
    parts = [parts[0], wmod_parts] + parts[1:]

    given = dict(c_ctx=c_ctx, w_mod=w_mod, b_mod=b_mod, norm_w=norm_w, w_in=w_in, ssm_conv_w=ssm_conv_w, ssm_conv_b=ssm_conv_b,
                 dt_bias=dt_bias, a_log=a_log, d_skip=d_skip, ssm_norm_w=ssm_norm_w, w_out_ssm=w_out_ssm, conf_conv_w=conf_conv_w,
                 conf_conv_b=conf_conv_b, conf_ln_w=conf_ln_w, conf_ln_b=conf_ln_b, w_out_conf=w_out_conf, w_out=w_out,
                 final_norm_w=final_norm_w)
    ms = dict(c_ctx=m_c_ctx, w_mod=m_w_mod, b_mod=m_b_mod, norm_w=m_norm_w, w_in=m_w_in, ssm_conv_w=m_ssm_conv_w,
              ssm_conv_b=m_ssm_conv_b, dt_bias=m_dt_bias, a_log=m_a_log, d_skip=m_d_skip, ssm_norm_w=m_ssm_norm_w,
              w_out_ssm=m_w_out_ssm, conf_conv_w=m_conf_conv_w, conf_conv_b=m_conf_conv_b, conf_ln_w=m_conf_ln_w,
              conf_ln_b=m_conf_ln_b, w_out_conf=m_w_out_conf, w_out=m_w_out, final_norm_w=m_final_norm_w)
    vs = dict(c_ctx=v_c_ctx, w_mod=v_w_mod, b_mod=v_b_mod, norm_w=v_norm_w, w_in=v_w_in, ssm_conv_w=v_ssm_conv_w,
              ssm_conv_b=v_ssm_conv_b, dt_bias=v_dt_bias, a_log=v_a_log, d_skip=v_d_skip, ssm_norm_w=v_ssm_norm_w,
              w_out_ssm=v_w_out_ssm, conf_conv_w=v_conf_conv_w, conf_conv_b=v_conf_conv_b, conf_ln_w=v_conf_ln_w,
              conf_ln_b=v_conf_ln_b, w_out_conf=v_w_out_conf, w_out=v_w_out, final_norm_w=v_final_norm_w)
    grads, deltas, new_m, new_v = {}, {}, {}, {}
    sharded = ("w_in", "w_mod", "w_out_ssm", "w_out_conf", "w_out", "ssm_conv_w", "conf_conv_w")
    for i, nm in enumerate(sharded):
        shp = given[nm].shape
        w2 = given[nm].reshape(shp[1], shp[2])
        res = _adamw(parts[i], w2, ms[nm].reshape(w2.shape), vs[nm].reshape(w2.shape), "adamw_" + nm)
        grads[nm], deltas[nm], new_m[nm], new_v[nm] = [r.reshape(shp) for r in res]
    shapes = {nm: given[nm].shape for nm, _ in _SMALL}
    res = _adamw(small_parts, _pack_small(given), _pack_small(ms), _pack_small(vs), "adamw_small")
    for dst, packed in zip((grads, deltas, new_m, new_v), res):
        dst.update(_unpack_small(packed, shapes))

    loss = lax.psum(loss_acc[0, 0], ("x", "y", "c"))
    order = ("c_ctx", "w_mod", "b_mod", "norm_w", "w_in", "ssm_conv_w", "ssm_conv_b", "dt_bias", "a_log", "d_skip", "ssm_norm_w",
             "w_out_ssm", "conf_conv_w", "conf_conv_b", "conf_ln_w", "conf_ln_b", "w_out_conf", "w_out", "final_norm_w")
    return (loss, gx.reshape(1, L, D), *[grads[n] for n in order], *[deltas[n] for n in order],
            *[new_m[n] for n in order], *[new_v[n] for n in order])
```

```python
import jax
import jax.numpy as jnp
from jax import lax
from jax.experimental import pallas as pl
from jax.experimental.pallas import tpu as pltpu

f32 = jnp.float32
bf16 = jnp.bfloat16

D = 1024
DI = 2048
NG = 8
HPG = 4
HD = 64
NS = 128
NH = 32
Q = 128
GRID_W = 64
CK = 31
SK = 4
EPS = 1e-6
RT = 256
N_DEV = 8
IN_COLS = 11328
X0, B0, C0, Z0, G10, G20, DT0, GV0, GG0, CG0, NP = 0, 2048, 3072, 4096, 6144, 7168, 8192, 9216, 10240, 11264, 12288
VMEM_LIMIT = 50 * 1024 * 1024
NEG = -1e30

ADAM_LR, ADAM_B1, ADAM_B2, ADAM_EPS, ADAM_WD, ADAM_STEP = 0.001, 0.9, 0.999, 1e-08, 0.01, 10

MESH = pl.DeviceIdType.MESH
S = jax.ShapeDtypeStruct


def _params(*sem):
    return pltpu.CompilerParams(dimension_semantics=tuple(sem) if sem else None, vmem_limit_bytes=VMEM_LIMIT)


def _sig(x):
    return 1.0 / (1.0 + jnp.exp(-x))


def _silu(x):
    return x * _sig(x)


def _dsilu(x, s):
    return s * (1.0 + x * (1.0 - s))


def _dot(a, b):
    return jnp.dot(a, b, preferred_element_type=f32)


def _dot_nt(a, b):
    return lax.dot_general(a, b, (((1,), (1,)), ((), ())), preferred_element_type=f32)


def _dot_tn(a, b):
    return lax.dot_general(a, b, (((0,), (0,)), ((), ())), preferred_element_type=f32)


def _dot3(t_bf, v):
    v1 = v.astype(bf16)
    r1 = v - v1.astype(f32)
    v2 = r1.astype(bf16)
    v3 = (r1 - v2.astype(f32)).astype(bf16)
    return _dot(t_bf, v1) + _dot(t_bf, v2) + _dot(t_bf, v3)


def _pick(n, prefs):
    for p in prefs:
        if n % p == 0:
            return p
    return n


def _full(shape):
    nd = len(shape)
    return pl.BlockSpec(shape, lambda *_: (0,) * nd)


def _matmul(a, b, out_dtype, name, tm=None, tn=None, tk=None, tb=False, comm=None):
    m, k = a.shape
    n = b.shape[0] if tb else b.shape[1]
    tm = tm or _pick(m, (768, 512, 256, 128))
    tn = tn or _pick(n, (1024, 512, 256, 128))
    tk = tk or _pick(k, (1024, 768, 512, 256, 128))
    nk = k // tk
    gi, gj = m // tm, n // tn
    carrs, modes = comm if comm else ((), ())
    nc = len(carrs)

    def kern(*refs):
        a_ref, b_ref = refs[:2]
        cins = refs[2:2 + nc]
        o_ref = refs[2 + nc]
        couts = refs[3 + nc:3 + 2 * nc]
        acc_ref = refs[3 + 2 * nc]
        sems = refs[4 + 2 * nc:]
        i, j, kk = pl.program_id(0), pl.program_id(1), pl.program_id(2)
        if nc:
            @pl.when(jnp.logical_and(jnp.logical_and(i == 0, j == 0), kk == 0))
            def _():
                _xchg_start(cins, couts, *sems, modes)

        part = _dot_nt(a_ref[...], b_ref[...]) if tb else _dot(a_ref[...], b_ref[...])
        if nk == 1:
            o_ref[...] = part.astype(o_ref.dtype)
        else:
            @pl.when(kk == 0)
            def _():
                acc_ref[...] = part

            @pl.when(kk > 0)
            def _():
                acc_ref[...] += part

            @pl.when(kk == nk - 1)
            def _():
                o_ref[...] = acc_ref[...].astype(o_ref.dtype)

        if nc:
            @pl.when(jnp.logical_and(jnp.logical_and(i == gi - 1, j == gj - 1), kk == nk - 1))
            def _():
                _xchg_wait(cins, couts, *sems, modes)

    anyspec = pl.BlockSpec(memory_space=pl.ANY)
    bspec = pl.BlockSpec((tn, tk), lambda i, j, kk: (j, kk)) if tb else pl.BlockSpec((tk, tn), lambda i, j, kk: (kk, j))
    out_shape = (S((m, n), out_dtype),) + _xchg_out_shapes(carrs, modes)
    res = pl.pallas_call(
        kern, out_shape=out_shape, grid=(gi, gj, nk),
        in_specs=[pl.BlockSpec((tm, tk), lambda i, j, kk: (i, kk)), bspec] + [anyspec] * nc,
        out_specs=(pl.BlockSpec((tm, tn), lambda i, j, kk: (i, j)),) + (anyspec,) * nc,
        scratch_shapes=[pltpu.VMEM((tm, tn), f32)] + (_xchg_sems(nc) if nc else []),
        compiler_params=_params(*((("arbitrary",) * 3) if nc else ("parallel", "parallel", "arbitrary"))), name=name)(a, b, *carrs)
    return res if nc else res[0]


def _mod_fwd(cc8, w_mod_bf, b_mod):
    def kern(c_ref, w_ref, b_ref, o_ref):
        o_ref[...] = _dot(_silu(c_ref[...]).astype(bf16), w_ref[...]) + b_ref[...]

    return pl.pallas_call(kern, out_shape=S((8, 3 * D), f32), compiler_params=_params(), name="mod_fwd")(cc8, w_mod_bf, b_mod)


def _mod_bwd(ct, dmod8, w_mod_bf):
    tc = 512
    nj = 3 * D // tc

    def kern(ct_ref, dm_ref, w_ref, dw_ref, db_ref, dc_ref):
        j = pl.program_id(0)
        c = ct_ref[:, 0:1]
        cx = ct_ref[:, 1:2]
        sx = _sig(cx)
        dmx = dm_ref[0:1, :]
        dmc = dm_ref[1:2, :]
        dw_ref[...] = (_silu(c) * dmx + (cx * sx) * dmc).astype(bf16)
        db_ref[...] = dmx + dmc
        t = jnp.sum(w_ref[...].astype(f32) * dmc.astype(bf16).astype(f32), axis=1, keepdims=True) * _dsilu(cx, sx)

        @pl.when(j == 0)
        def _():
            dc_ref[...] = jnp.zeros_like(dc_ref)

        dc_ref[...] += jnp.broadcast_to(t, (D, 128))

    return pl.pallas_call(
        kern, out_shape=(S((D, 3 * D), bf16), S((1, 3 * D), f32), S((D, 128), f32)), grid=(nj,),
        in_specs=[_full((D, 128)), pl.BlockSpec((8, tc), lambda j: (0, j)), pl.BlockSpec((D, tc), lambda j: (0, j))],
        out_specs=(pl.BlockSpec((D, tc), lambda j: (0, j)), pl.BlockSpec((1, tc), lambda j: (0, j)), _full((D, 128))),
        compiler_params=_params("arbitrary"), name="mod_bwd")(ct, dmod8, w_mod_bf)


def _prenorm(x, ctx, norm_w, mod):
    L, Lc = x.shape[0], ctx.shape[0]
    nlx, nt = L // RT, (L + Lc) // RT

    def kern(x_ref, c_ref, nw_ref, mod_ref, h_ref, ht_ref):
        i = pl.program_id(0)
        is_c = i >= nlx
        xv = jnp.where(is_c, c_ref[...], x_ref[...])
        shift = jnp.where(is_c, mod_ref[1:2, 0:D], mod_ref[0:1, 0:D])
        scale = jnp.where(is_c, mod_ref[1:2, D:2 * D], mod_ref[0:1, D:2 * D])
        r = lax.rsqrt(jnp.mean(xv * xv, axis=1, keepdims=True) + EPS)
        hv = (xv * r) * nw_ref[...] * (1.0 + scale) + shift
        h_ref[...] = hv.astype(bf16)
        ht_ref[...] = jnp.transpose(hv).astype(bf16)

    return pl.pallas_call(
        kern, out_shape=(S((L + Lc, D), bf16), S((D, L + Lc), bf16)), grid=(nt,),
        in_specs=[pl.BlockSpec((RT, D), lambda i: (jnp.minimum(i, nlx - 1), 0)),
                  pl.BlockSpec((RT, D), lambda i: (jnp.maximum(i - nlx, 0), 0)),
                  _full((1, D)), _full((8, 3 * D))],
        out_specs=(pl.BlockSpec((RT, D), lambda i: (i, 0)), pl.BlockSpec((D, RT), lambda i: (0, i))),
        compiler_params=_params("parallel"), name="prenorm")(x, ctx, norm_w, mod)


def _prenorm_bwd(x, ctx, dh, dx1, norm_w, mod):
    L, Lc = x.shape[0], ctx.shape[0]
    nlx, nt = L // RT, (L + Lc) // RT

    def kern(x_ref, c_ref, dh_ref, dx1_ref, nw_ref, mod_ref, gx_ref, dnw_ref, acc_ref):
        i = pl.program_id(0)
        is_c = i >= nlx

        @pl.when(i == 0)
        def _():
            dnw_ref[...] = jnp.zeros_like(dnw_ref)
            acc_ref[...] = jnp.zeros_like(acc_ref)

        xv = jnp.where(is_c, c_ref[...], x_ref[...])
        scale = jnp.where(is_c, mod_ref[1:2, D:2 * D], mod_ref[0:1, D:2 * D])
        nw = nw_ref[...]
        r = lax.rsqrt(jnp.mean(xv * xv, axis=1, keepdims=True) + EPS)
        xn = xv * r
        dh = dh_ref[...]
        dsh = jnp.sum(dh, axis=0, keepdims=True)
        dsc = jnp.sum(dh * (xn * nw), axis=0, keepdims=True)
        dxnw = dh * (1.0 + scale)
        dnw_ref[...] += jnp.sum(dxnw * xn, axis=0, keepdims=True)
        dxn = dxnw * nw
        dx = r * (dxn - xn * jnp.mean(dxn * xn, axis=1, keepdims=True))

        @pl.when(jnp.logical_not(is_c))
        def _():
            gx_ref[...] = dx1_ref[...] + dx
            acc_ref[0:1, :] += dsh
            acc_ref[1:2, :] += dsc

        @pl.when(is_c)
        def _():
            acc_ref[2:3, :] += dsh
            acc_ref[3:4, :] += dsc

    xmap = lambda i: (jnp.minimum(i, nlx - 1), 0)
    return pl.pallas_call(
        kern, out_shape=(S((L, D), f32), S((1, D), f32), S((8, D), f32)), grid=(nt,),
        in_specs=[pl.BlockSpec((RT, D), xmap), pl.BlockSpec((RT, D), lambda i: (jnp.maximum(i - nlx, 0), 0)),
                  pl.BlockSpec((RT, D), lambda i: (i, 0)), pl.BlockSpec((RT, D), xmap), _full((1, D)), _full((8, 3 * D))],
        out_specs=(pl.BlockSpec((RT, D), xmap), _full((1, D)), _full((8, D))),
        compiler_params=_params("arbitrary"), name="prenorm_bwd")(x, ctx, dh, dx1, norm_w, mod)


def _halo_specs(nt_rows, ct):
    cur = pl.BlockSpec((RT, ct), lambda i, j: (i, j))
    prev = pl.BlockSpec((8, ct), lambda i, j: (jnp.maximum(i * (RT // 8) - 1, 0), j))
    nxt = pl.BlockSpec((8, ct), lambda i, j: (jnp.minimum((i + 1) * (RT // 8), nt_rows // 8 - 1), j))
    return cur, prev, nxt


def _fill_halo(scr, cur_ref, prev_ref, next_ref, i, nlx, nt):
    prev_ok = jnp.logical_and(i != 0, i != nlx)
    next_ok = jnp.logical_and(i != nlx - 1, i != nt - 1)
    scr[0:8, :] = jnp.where(prev_ok, prev_ref[...], 0.0)
    scr[8:8 + RT, :] = cur_ref[...]
    scr[8 + RT:16 + RT, :] = jnp.where(next_ok, next_ref[...], 0.0)


def _ssm_conv_fwd(proj, w8, b, nlx):
    T = proj.shape[0]
    nt = T // RT
    ct = 1024
    cur, prev, nxt = _halo_specs(T, ct)

    def kern(cur_ref, prev_ref, next_ref, w_ref, b_ref, o_ref, scr):
        i = pl.program_id(0)
        _fill_halo(scr, cur_ref, prev_ref, next_ref, i, nlx, nt)
        acc = jnp.broadcast_to(b_ref[...], (RT, ct))
        for k in range(SK):
            acc = acc + w_ref[k:k + 1, :] * scr[pl.ds(6 + k, RT), :]
        o_ref[...] = _silu(acc)

    return pl.pallas_call(
        kern, out_shape=S((T, 4096), f32), grid=(nt, 4096 // ct),
        in_specs=[cur, prev, nxt, pl.BlockSpec((8, ct), lambda i, j: (0, j)), pl.BlockSpec((1, ct), lambda i, j: (0, j))],
        out_specs=pl.BlockSpec((RT, ct), lambda i, j: (i, j)),
        scratch_shapes=[pltpu.VMEM((RT + 16, ct), f32)],
        compiler_params=_params("parallel", "parallel"), name="ssm_conv_fwd")(proj, proj, proj, w8, b)


def _ssm_conv_dpre(dxbc, proj, w8, b, nlx):
    T = proj.shape[0]
    nt = T // RT
    ct = 1024
    cur = pl.BlockSpec((RT, ct), lambda j, i: (i, j))
    prev = pl.BlockSpec((8, ct), lambda j, i: (jnp.maximum(i * (RT // 8) - 1, 0), j))
    nxt = pl.BlockSpec((8, ct), lambda j, i: (jnp.minimum((i + 1) * (RT // 8), T // 8 - 1), j))

    def kern(d_ref, cur_ref, prev_ref, next_ref, w_ref, b_ref, dpre_ref, dw_ref, db_ref, scr):
        i = pl.program_id(1)
        _fill_halo(scr, cur_ref, prev_ref, next_ref, i, nlx, nt)

        @pl.when(i == 0)
        def _():
            dw_ref[...] = jnp.zeros_like(dw_ref)
            db_ref[...] = jnp.zeros_like(db_ref)

        pre = jnp.broadcast_to(b_ref[...], (RT, ct))
        for k in range(SK):
            pre = pre + w_ref[k:k + 1, :] * scr[pl.ds(6 + k, RT), :]
        dpre = d_ref[...] * _dsilu(pre, _sig(pre))
        dpre_ref[...] = dpre
        db_ref[...] += jnp.sum(dpre, axis=0, keepdims=True)
        for k in range(SK):
            dw_ref[k:k + 1, :] += jnp.sum(dpre * scr[pl.ds(6 + k, RT), :], axis=0, keepdims=True)

    return pl.pallas_call(
        kern, out_shape=(S((T, 4096), f32), S((8, 4096), f32), S((1, 4096), f32)), grid=(4096 // ct, nt),
        in_specs=[cur, cur, prev, nxt, pl.BlockSpec((8, ct), lambda j, i: (0, j)), pl.BlockSpec((1, ct), lambda j, i: (0, j))],
        out_specs=(cur, pl.BlockSpec((8, ct), lambda j, i: (0, j)), pl.BlockSpec((1, ct), lambda j, i: (0, j))),
        scratch_shapes=[pltpu.VMEM((RT + 16, ct), f32)],
        compiler_params=_params("parallel", "arbitrary"), name="ssm_conv_dpre")(dxbc, proj, proj, proj, w8, b)


def _ssm_conv_t(dpre, w8, dproj, nlx):
    T = dpre.shape[0]
    nt = T // RT
    ct = 1024
    cur, prev, nxt = _halo_specs(T, ct)

    def kern(cur_ref, prev_ref, next_ref, w_ref, _alias, o_ref, scr):
        i = pl.program_id(0)
        _fill_halo(scr, cur_ref, prev_ref, next_ref, i, nlx, nt)
        acc = jnp.zeros((RT, ct), f32)
        for k in range(SK):
            acc = acc + w_ref[k:k + 1, :] * scr[pl.ds(10 - k, RT), :]
        o_ref[...] = acc.astype(bf16)

    return pl.pallas_call(
        kern, out_shape=S(dproj.shape, bf16), grid=(nt, 4096 // ct),
        in_specs=[cur, prev, nxt, pl.BlockSpec((8, ct), lambda i, j: (0, j)), pl.BlockSpec(memory_space=pl.ANY)],
        out_specs=pl.BlockSpec((RT, ct), lambda i, j: (i, j)),
        scratch_shapes=[pltpu.VMEM((RT + 16, ct), f32)], input_output_aliases={4: 0},
        compiler_params=_params("parallel", "parallel"), name="ssm_conv_t")(dpre, dpre, dpre, w8, dproj)


def _tri():
    li = lax.broadcasted_iota(jnp.int32, (Q, Q), 0)
    si = lax.broadcasted_iota(jnp.int32, (Q, Q), 1)
    return (si <= li).astype(bf16), (si >= li).astype(bf16)


def _dt_prep(proj, bias_row, alog_row):
    T = proj.shape[0]
    nch = T // Q

    def kern(raw_ref, b_ref, al_ref, dt_ref, la_ref):
        lane = lax.broadcasted_iota(jnp.int32, (Q, 128), 1)
        v = raw_ref[...] + b_ref[...]
        dt = jnp.maximum(v, 0.0) + jnp.log1p(jnp.exp(-jnp.abs(v)))
        a = jnp.where(lane[0:1, :] < 2 * NH, -jnp.exp(al_ref[...]), 0.0)
        da = dt * a
        tri, trit = _tri()
        dt_ref[...] = dt
        la_ref[...] = jnp.where(lane < NH, _dot3(tri, da), _dot3(trit, da))

    return pl.pallas_call(
        kern, out_shape=(S((T, 128), f32), S((T, 128), f32)), grid=(nch,),
        in_specs=[pl.BlockSpec((Q, 128), lambda c: (c, DT0 // 128)), _full((1, 128)), _full((1, 128))],
        out_specs=(pl.BlockSpec((Q, 128), lambda c: (c, 0)), pl.BlockSpec((Q, 128), lambda c: (c, 0))),
        compiler_params=_params("parallel"), name="dt_prep")(proj, bias_row, alog_row)


def _dt_bwd(a1, a2, r2, sv, dt, la, proj, bias_row, alog_row, dproj):
    T = proj.shape[0]
    nch = T // Q
    blk = pl.BlockSpec((Q, 128), lambda c: (c, 0))

    def kern(a1_ref, a2_ref, r2_ref, s_ref, dt_ref, la_ref, raw_ref, b_ref, al_ref, _alias, o_ref, db_ref, dal_ref):
        c = pl.program_id(0)

        @pl.when(c == 0)
        def _():
            db_ref[...] = jnp.zeros_like(db_ref)
            dal_ref[...] = jnp.zeros_like(dal_ref)

        lane = lax.broadcasted_iota(jnp.int32, (Q, 128), 1)
        row = lax.broadcasted_iota(jnp.int32, (Q, 128), 0)
        fwd = lane < NH
        dt = dt_ref[...]
        la = la_ref[...]
        a2v = a2_ref[...]
        r2v = r2_ref[...]
        a = jnp.where(lane[0:1, :] < 2 * NH, -jnp.exp(al_ref[...]), 0.0)
        la_e = jnp.where(fwd[0:1, :], la[Q - 1:Q, :], la[0:1, :])
        is_end = row == jnp.where(fwd, Q - 1, 0)
        e_end = jnp.exp(la_e - la)
        wend = e_end * dt
        extra = s_ref[0:1, :] * jnp.exp(la_e) + jnp.sum(wend * a2v, axis=0, keepdims=True)
        dla = a1_ref[...] - dt * r2v - wend * a2v + jnp.where(is_end, extra, 0.0)
        tri, trit = _tri()
        rcs = jnp.where(fwd, _dot3(trit, dla), _dot3(tri, dla))
        ddt = r2v + e_end * a2v + a * rcs
        dal_ref[...] += a * jnp.sum(dt * rcs, axis=0, keepdims=True)
        draw = jnp.where(lane < 2 * NH, ddt * _sig(raw_ref[...] + b_ref[...]), 0.0)
        db_ref[...] += jnp.sum(draw, axis=0, keepdims=True)
        o_ref[...] = jnp.zeros_like(o_ref)
        o_ref[:, 0:128] = draw.astype(bf16)

    return pl.pallas_call(
        kern, out_shape=(S(dproj.shape, bf16), S((1, 128), f32), S((1, 128), f32)), grid=(nch,),
        in_specs=[blk, blk, blk, blk, blk, blk, pl.BlockSpec((Q, 128), lambda c: (c, DT0 // 128)),
                  _full((1, 128)), _full((1, 128)), pl.BlockSpec(memory_space=pl.ANY)],
        out_specs=(pl.BlockSpec((Q, 1024), lambda c: (c, DT0 // 1024)), _full((1, 128)), _full((1, 128))),
        input_output_aliases={9: 0},
        compiler_params=_params("arbitrary"), name="dt_bwd")(a1, a2, r2, sv, dt, la, proj, bias_row, alog_row, dproj)


def _split2(v):
    hi = v.astype(bf16)
    lo = (v - hi.astype(f32)).astype(bf16)
    return jnp.concatenate([hi, lo], axis=1)


def _split3(v):
    hi = v.astype(bf16)
    r1 = v - hi.astype(f32)
    mid = r1.astype(bf16)
    lo = (r1 - mid.astype(f32)).astype(bf16)
    return jnp.concatenate([hi, mid, lo], axis=1)


def _scan_consts(rev):
    hoff = NH if rev else 0
    g = jnp.arange(NG, dtype=jnp.int32)[:, None, None]

    def rc(nr, ncol):
        return jnp.arange(nr, dtype=jnp.int32)[None, :, None], jnp.arange(ncol, dtype=jnp.int32)[None, None, :]

    r, c = rc(3 * 128, HPG * 128)
    sel_la = (lax.rem(r, 128) == hoff + HPG * g + c // 128).astype(bf16)
    r, c = rc(2 * 128, HPG * HD)
    sel_w = (lax.rem(r, 128) == hoff + HPG * g + c // HD).astype(bf16)
    r, c = rc(2 * HPG * HD, 128)
    ind_h = (c == hoff + HPG * g + lax.rem(r, HPG * HD) // HD).astype(bf16)
    r, c = rc(2 * HPG * Q, 128)
    ind_e = (c == hoff + HPG * g + lax.rem(r, HPG * Q) // Q).astype(bf16)
    return sel_la, sel_w, ind_h, ind_e


def _masks(rev):
    li = lax.broadcasted_iota(jnp.int32, (Q, Q), 0)
    si = lax.broadcasted_iota(jnp.int32, (Q, Q), 1)
    mask = (li <= si) if rev else (li >= si)
    mask_t = (li >= si) if rev else (li <= si)
    lane = lax.broadcasted_iota(jnp.int32, (Q, HPG * HD), 1)
    hms = [jnp.logical_and(lane >= r * HD, lane < (r + 1) * HD) for r in range(HPG)]
    return mask, mask_t, hms


def _mine(hoff):
    lane = lax.broadcasted_iota(jnp.int32, (Q, 128), 1)
    return jnp.logical_and(lane >= hoff, lane < hoff + NH)


def _head_row(vals, hc0):
    lane = lax.broadcasted_iota(jnp.int32, (1, HPG * HD), 1)
    out = jnp.zeros((1, HPG * HD), f32)
    for r in range(HPG):
        out = jnp.where(jnp.logical_and(lane >= r * HD, lane < (r + 1) * HD), vals[:, hc0 + r:hc0 + r + 1], out)
    return out


def _chunk_of(j, rev, nxc, nch):
    return (nch - 1 - j) if rev else lax.rem(j + nxc, nch)


def _ssd_fwd(xbc, dt, la, consts, rev, nxc, name):
    T = xbc.shape[0]
    nch = T // Q
    hoff = NH if rev else 0
    e = 0 if rev else Q - 1
    cm = lambda j: _chunk_of(j, rev, nxc, nch)
    sel_la, sel_w = consts[0], consts[1]

    def kern(xbc_ref, dt_ref, la_ref, sla_ref, sw_ref, y_ref, hp_ref, h_ref):
        j = pl.program_id(0)

        @pl.when(j == 0)
        def _():
            h_ref[...] = jnp.zeros_like(h_ref)

        hp_ref[...] = h_ref[...]
        mask, _, hms = _masks(rev)
        la_all = la_ref[...]
        dt_all = dt_ref[...]
        la_t = jnp.transpose(la_all)
        dt_t = jnp.transpose(dt_all)
        la_e = la_all[e:e + 1, :]
        la3 = _split3(la_all)
        w2 = _split2(jnp.exp(jnp.where(_mine(hoff), la_e - la_all, 0.0)) * dt_all)
        e2 = _split2(jnp.exp(la_all))
        ela_e = jnp.exp(la_e)
        for g in range(NG):
            hc0 = hoff + g * HPG
            x = xbc_ref[:, g * 256:(g + 1) * 256]
            bb = xbc_ref[:, B0 + g * NS:B0 + (g + 1) * NS].astype(bf16)
            cb = xbc_ref[:, C0 + g * NS:C0 + (g + 1) * NS].astype(bf16)
            ht = h_ref[g * NS:(g + 1) * NS, :]
            scores = _dot_nt(cb, bb)
            yoff = _dot(cb, ht.astype(bf16))
            la_rep4 = _dot(la3, sla_ref[g])
            wend = _dot(w2, sw_ref[g])
            expla = _dot(e2, sw_ref[g])
            mixes, xstack = [], []
            for r in range(HPG):
                hc = hc0 + r
                decay = jnp.exp(jnp.where(mask, la_rep4[:, r * 128:(r + 1) * 128] - la_t[hc:hc + 1, :], NEG))
                mixes.append((scores * decay * dt_t[hc:hc + 1, :]).astype(bf16))
                xstack.append(jnp.where(hms[r], x, 0.0).astype(bf16))
            y = _dot(jnp.concatenate(mixes, axis=1), jnp.concatenate(xstack, axis=0)) + yoff * expla
            y_ref[:, g * 256:(g + 1) * 256] = y
            h_ref[g * NS:(g + 1) * NS, :] = ht * _head_row(ela_e, hc0) + _dot_tn(bb, (x * wend).astype(bf16))

    row = lambda j: (cm(j), 0)
    return pl.pallas_call(
        kern, out_shape=(S((T, DI), f32), S((nch, NG * NS, HPG * HD), f32)), grid=(nch,),
        in_specs=[pl.BlockSpec((Q, 4096), row), pl.BlockSpec((Q, 128), row), pl.BlockSpec((Q, 128), row),
                  _full(sel_la.shape), _full(sel_w.shape)],
        out_specs=(pl.BlockSpec((Q, DI), row), pl.BlockSpec((None, NG * NS, HPG * HD), lambda j: (cm(j), 0, 0))),
        scratch_shapes=[pltpu.VMEM((NG * NS, HPG * HD), f32)],
        compiler_params=_params("arbitrary"), name=name)(xbc, dt, la, sel_la, sel_w)


def _ssd_bwd(xbc, dy, dt, la, hprev, dskip_full, consts, rev, nxc, name, acc=None):
    T = xbc.shape[0]
    nch = T // Q
    hoff = NH if rev else 0
    e = 0 if rev else Q - 1
    cm = lambda j: _chunk_of(nch - 1 - j, rev, nxc, nch)
    has_acc = acc is not None
    sel_la, sel_w, ind_h, ind_e = consts

    def kern(*refs):
        xbc_ref, dy_ref, dt_ref, la_ref, hp_ref, dsk_ref, sla_ref, sw_ref, ih_ref, ie_ref = refs[:10]
        k = 10
        if has_acc:
            dxbc_in, a1_in, a2_in, r2_in, s_in = refs[k:k + 5]
            k += 5
        dxbc_ref, a1_ref, a2_ref, r2_ref, s_ref, g_ref, r2scr = refs[k:k + 7]
        j = pl.program_id(0)

        @pl.when(j == 0)
        def _():
            g_ref[...] = jnp.zeros_like(g_ref)

        mask, mask_t, hms = _masks(rev)
        la_all = la_ref[...]
        dt_all = dt_ref[...]
        la_t = jnp.transpose(la_all)
        dt_t = jnp.transpose(dt_all)
        la_e = la_all[e:e + 1, :]
        la3 = _split3(la_all)
        w2 = _split2(jnp.exp(jnp.where(_mine(hoff), la_e - la_all, 0.0)) * dt_all)
        e2 = _split2(jnp.exp(la_all))
        d2 = _split2(dt_all)
        ela_e = jnp.exp(la_e)
        r2scr[...] = jnp.zeros_like(r2scr)
        a1acc = jnp.zeros((Q, 128), f32)
        a2acc = jnp.zeros((Q, 128), f32)
        sacc = jnp.zeros((1, 128), f32)
        for g in range(NG):
            hc0 = hoff + g * HPG
            x = xbc_ref[:, g * 256:(g + 1) * 256]
            bb = xbc_ref[:, B0 + g * NS:B0 + (g + 1) * NS].astype(bf16)
            cb = xbc_ref[:, C0 + g * NS:C0 + (g + 1) * NS].astype(bf16)
            dyv = dy_ref[:, g * 256:(g + 1) * 256]
            gt = g_ref[g * NS:(g + 1) * NS, :]
            ht = hp_ref[g * NS:(g + 1) * NS, :]
            gtb = gt.astype(bf16)
            htb = ht.astype(bf16)
            xb = x.astype(bf16)
            scores = _dot_nt(cb, bb)
            scores_t = _dot_nt(bb, cb)
            bg = _dot(bb, gtb)
            yoff = _dot(cb, htb)
            la_rep4 = _dot(la3, sla_ref[g])
            wend = _dot(w2, sw_ref[g])
            expla = _dot(e2, sw_ref[g])
            dtf = _dot(d2, sw_ref[g])
            dym = jnp.concatenate([jnp.where(hms[r], dyv, 0.0).astype(bf16) for r in range(HPG)], axis=0)
            dyx_all = _dot_nt(dym, xb)
            sdts, ehis, elos = [], [], []
            wsum = jnp.zeros((Q, Q), f32)
            for r in range(HPG):
                hc = hc0 + r
                la_rep = la_rep4[:, r * 128:(r + 1) * 128]
                la_r = la_t[hc:hc + 1, :]
                dt_r = dt_t[hc:hc + 1, :]
                decay = jnp.exp(jnp.where(mask, la_rep - la_r, NEG))
                decay_t = jnp.exp(jnp.where(mask_t, la_r - la_rep, NEG))
                dyx = dyx_all[r * Q:(r + 1) * Q, :]
                fm = dyx * (scores * decay)
                r2scr[hc:hc + 1, :] = jnp.sum(fm, axis=0, keepdims=True)
                em = fm * dt_r
                ehi = em.astype(bf16)
                ehis.append(ehi)
                elos.append((em - ehi.astype(f32)).astype(bf16))
                wsum = wsum + dyx * decay * dt_r
                sdts.append((scores_t * decay_t).astype(bf16))
            dx = dtf * _dot(jnp.concatenate(sdts, axis=1), dym) + wend * bg
            if not has_acc:
                dx = dx + dsk_ref[:, g * 256:(g + 1) * 256] * dyv
            a1acc = a1acc + _dot(jnp.concatenate(ehis + elos, axis=1), ie_ref[g]) \
                + _dot(_split2(dyv * yoff * expla), ih_ref[g])
            a2acc = a2acc + _dot(_split2(x * bg), ih_ref[g])
            sacc = sacc + jnp.sum(_dot(_split2(gt * ht), ih_ref[g]), axis=0, keepdims=True)
            wb = wsum.astype(bf16)
            dysb = (dyv * expla).astype(bf16)
            dc = _dot(wb, bb) + _dot_nt(dysb, htb)
            db = _dot_tn(wb, cb) + _dot_nt((x * wend).astype(bf16), gtb)
            g_ref[g * NS:(g + 1) * NS, :] = gt * _head_row(ela_e, hc0) + _dot_tn(cb, dysb)
            if has_acc:
                dx = dx + dxbc_in[:, g * 256:(g + 1) * 256]
                db = db + dxbc_in[:, B0 + g * NS:B0 + (g + 1) * NS]
                dc = dc + dxbc_in[:, C0 + g * NS:C0 + (g + 1) * NS]
            dxbc_ref[:, g * 256:(g + 1) * 256] = dx
            dxbc_ref[:, B0 + g * NS:B0 + (g + 1) * NS] = db
            dxbc_ref[:, C0 + g * NS:C0 + (g + 1) * NS] = dc
        r2c = jnp.transpose(r2scr[...])
        sc = jnp.broadcast_to(sacc, (Q, 128))
        if has_acc:
            a1acc = a1acc + a1_in[...]
            a2acc = a2acc + a2_in[...]
            r2c = r2c + r2_in[...]
            sc = sc + s_in[...]
        a1_ref[...] = a1acc
        a2_ref[...] = a2acc
        r2_ref[...] = r2c
        s_ref[...] = sc

    blk = pl.BlockSpec((Q, 128), lambda j: (cm(j), 0))
    big = pl.BlockSpec((Q, 4096), lambda j: (cm(j), 0))
    in_specs = [big, pl.BlockSpec((Q, DI), lambda j: (cm(j), 0)), blk, blk,
                pl.BlockSpec((None, NG * NS, HPG * HD), lambda j: (cm(j), 0, 0)), _full((1, DI)),
                _full(sel_la.shape), _full(sel_w.shape), _full(ind_h.shape), _full(ind_e.shape)]
    args = [xbc, dy, dt, la, hprev, dskip_full, sel_la, sel_w, ind_h, ind_e]
    aliases = {}
    if has_acc:
        in_specs += [big, blk, blk, blk, blk]
        args += list(acc)
        aliases = {10: 0, 11: 1, 12: 2, 13: 3, 14: 4}
    return pl.pallas_call(
        kern, out_shape=(S((T, 4096), f32), S((T, 128), f32), S((T, 128), f32), S((T, 128), f32), S((T, 128), f32)),
        grid=(nch,), in_specs=in_specs, out_specs=(big, blk, blk, blk, blk),
        scratch_shapes=[pltpu.VMEM((NG * NS, HPG * HD), f32), pltpu.VMEM((128, Q), f32)],
        input_output_aliases=aliases,
        compiler_params=_params("arbitrary"), name=name)(*args)


def _ynorm_fwd(yf, yb, xbc, proj, dskip_full, nw, L):
    nlx = L // RT

    def kern(yf_ref, yb_ref, xs_ref, z_ref, dsk_ref, nw_ref, y_ref, yn_ref, ynt_ref):
        y = yf_ref[...] + yb_ref[...] + dsk_ref[...] * xs_ref[...]
        y_ref[...] = y
        yz = y * _silu(z_ref[...])
        for g in range(NG):
            sl = yz[:, g * 256:(g + 1) * 256]
            r = lax.rsqrt(jnp.mean(sl * sl, axis=1, keepdims=True) + EPS)
            yn = (sl * r) * nw_ref[:, g * 256:(g + 1) * 256]
            yn_ref[:, g * 256:(g + 1) * 256] = yn.astype(bf16)
            ynt_ref[g * 256:(g + 1) * 256, :] = jnp.transpose(yn).astype(bf16)

    blk = pl.BlockSpec((RT, DI), lambda i: (i, 0))
    return pl.pallas_call(
        kern, out_shape=(S((L, DI), f32), S((L, DI), bf16), S((DI, L), bf16)), grid=(nlx,),
        in_specs=[blk, blk, blk, pl.BlockSpec((RT, DI), lambda i: (i, Z0 // DI)), _full((1, DI)), _full((1, DI))],
        out_specs=(blk, blk, pl.BlockSpec((DI, RT), lambda i: (0, i))),
        compiler_params=_params("parallel"), name="ynorm_fwd")(yf, yb, xbc, proj, dskip_full, nw)


def _ynorm_bwd(dyn, y, xbc, proj, dskip_full, nw, dproj):
    L = y.shape[0]
    T = proj.shape[0]
    nlx, nt = L // RT, T // RT

    def kern(dyn_ref, y_ref, xs_ref, z_ref, dsk_ref, nw_ref, _alias, dz_ref, dy_ref, dnw_ref, dsk_acc):
        i = pl.program_id(0)

        @pl.when(i == 0)
        def _():
            dnw_ref[...] = jnp.zeros_like(dnw_ref)
            dsk_acc[...] = jnp.zeros_like(dsk_acc)

        @pl.when(i >= nlx)
        def _():
            dz_ref[...] = jnp.zeros_like(dz_ref)
            dy_ref[...] = jnp.zeros_like(dy_ref)

        @pl.when(i < nlx)
        def _():
            y = y_ref[...]
            z = z_ref[...]
            sz = _sig(z)
            gz = z * sz
            yz = y * gz
            dynv = dyn_ref[...]
            for g in range(NG):
                cs = slice(g * 256, (g + 1) * 256)
                sl = yz[:, cs]
                r = lax.rsqrt(jnp.mean(sl * sl, axis=1, keepdims=True) + EPS)
                yhat = sl * r
                dn = dynv[:, cs]
                dnw_ref[:, cs] += jnp.sum(dn * yhat, axis=0, keepdims=True)
                dyh = dn * nw_ref[:, cs]
                dyz = r * (dyh - yhat * jnp.mean(dyh * yhat, axis=1, keepdims=True))
                dyv = dyz * gz[:, cs]
                dy_ref[:, cs] = dyv
                dz_ref[:, cs] = (dyz * y[:, cs] * _dsilu(z[:, cs], sz[:, cs])).astype(bf16)
                dsk_acc[:, cs] += jnp.sum(dyv * xs_ref[:, cs], axis=0, keepdims=True)

    xmap = lambda i: (jnp.minimum(i, nlx - 1), 0)
    return pl.pallas_call(
        kern, out_shape=(S(dproj.shape, bf16), S((T, DI), f32), S((1, DI), f32), S((1, DI), f32)), grid=(nt,),
        in_specs=[pl.BlockSpec((RT, DI), xmap), pl.BlockSpec((RT, DI), xmap), pl.BlockSpec((RT, DI), xmap),
                  pl.BlockSpec((RT, DI), lambda i: (jnp.minimum(i, nlx - 1), Z0 // DI)), _full((1, DI)), _full((1, DI)),
                  pl.BlockSpec(memory_space=pl.ANY)],
        out_specs=(pl.BlockSpec((RT, DI), lambda i: (i, Z0 // DI)), pl.BlockSpec((RT, DI), lambda i: (i, 0)),
                   _full((1, DI)), _full((1, DI))),
        input_output_aliases={6: 0},
        compiler_params=_params("arbitrary"), name="ynorm_bwd")(dyn, y, xbc, proj, dskip_full, nw, dproj)


def _head_sums(cols):
    def kern(c_ref, o_ref):
        o_ref[...] = jnp.broadcast_to(jnp.sum(c_ref[...], axis=1, keepdims=True), (NH, 128))

    return pl.pallas_call(kern, out_shape=S((NH, 128), f32), name="head_sums")(cols)


SEG_STRIDE = 96
SEG_PAD = 16
NSEG = RT // GRID_W
CONF_ROWS = SEG_PAD + NSEG * SEG_STRIDE


def _seg_fill(scr, val):
    scr[...] = jnp.zeros_like(scr)
    for s in range(NSEG):
        scr[SEG_PAD + s * SEG_STRIDE:SEG_PAD + s * SEG_STRIDE + GRID_W, :] = val[s * GRID_W:(s + 1) * GRID_W, :]


def _conf_fwd(proj, w32, cb, lnw, lnb, L):
    nlx = L // RT

    def kern(v_ref, g_ref, cg_ref, w_ref, cb_ref, lnw_ref, lnb_ref, u1_ref, u3_ref, u3t_ref, scr):
        _seg_fill(scr, v_ref[...] * _sig(g_ref[...]))
        for s in range(NSEG):
            for cc in range(D // 256):
                cs = slice(cc * 256, (cc + 1) * 256)
                acc = jnp.broadcast_to(cb_ref[:, cs], (GRID_W, 256))
                for k in range(CK):
                    acc = acc + w_ref[k:k + 1, cs] * scr[pl.ds(SEG_PAD + s * SEG_STRIDE + k - CK // 2, GRID_W), cs]
                u1_ref[s * GRID_W:(s + 1) * GRID_W, cs] = acc
        u1 = u1_ref[...]
        mu = jnp.mean(u1, axis=1, keepdims=True)
        xc = u1 - mu
        r = lax.rsqrt(jnp.mean(xc * xc, axis=1, keepdims=True) + EPS)
        u2 = (xc * r) * lnw_ref[...] + lnb_ref[...]
        u3 = _silu(u2) * _silu(cg_ref[...])
        u3_ref[...] = u3.astype(bf16)
        u3t_ref[...] = jnp.transpose(u3).astype(bf16)

    blk = pl.BlockSpec((RT, D), lambda i: (i, 0))
    return pl.pallas_call(
        kern, out_shape=(S((L, D), f32), S((L, D), bf16), S((D, L), bf16)), grid=(nlx,),
        in_specs=[pl.BlockSpec((RT, D), lambda i: (i, GV0 // D)), pl.BlockSpec((RT, D), lambda i: (i, GG0 // D)),
                  pl.BlockSpec((RT, D), lambda i: (i, CG0 // D)), _full((32, D)), _full((1, D)), _full((1, D)), _full((1, D))],
        out_specs=(blk, blk, pl.BlockSpec((D, RT), lambda i: (0, i))), scratch_shapes=[pltpu.VMEM((CONF_ROWS, D), f32)],
        compiler_params=_params("parallel"), name="conf_fwd")(proj, proj, proj, w32, cb, lnw, lnb)


def _conf_bwd(du3, u1, proj, w32, lnw, lnb, dproj):
    L = u1.shape[0]
    T = proj.shape[0]
    nlx, nt = L // RT, T // RT

    def kern(du3_ref, u1_ref, v_ref, g_ref, cg_ref, w_ref, lnw_ref, lnb_ref, _alias,
             o_ref, dw_ref, dcb_ref, dlw_ref, dlb_ref, scr_u, scr_d, du0_scr):
        i = pl.program_id(0)

        @pl.when(i == 0)
        def _():
            dw_ref[...] = jnp.zeros_like(dw_ref)
            dcb_ref[...] = jnp.zeros_like(dcb_ref)
            dlw_ref[...] = jnp.zeros_like(dlw_ref)
            dlb_ref[...] = jnp.zeros_like(dlb_ref)

        @pl.when(i >= nlx)
        def _():
            o_ref[...] = jnp.zeros_like(o_ref)

        @pl.when(i < nlx)
        def _():
            val = v_ref[...]
            sg = _sig(g_ref[...])
            cg = cg_ref[...]
            scg = _sig(cg)
            u1 = u1_ref[...]
            mu = jnp.mean(u1, axis=1, keepdims=True)
            xc = u1 - mu
            r = lax.rsqrt(jnp.mean(xc * xc, axis=1, keepdims=True) + EPS)
            xhat = xc * r
            u2 = xhat * lnw_ref[...] + lnb_ref[...]
            s2 = _sig(u2)
            du3v = du3_ref[...]
            du2 = du3v * (cg * scg) * _dsilu(u2, s2)
            o_ref[:, 2 * D:3 * D] = (du3v * (u2 * s2) * _dsilu(cg, scg)).astype(bf16)
            dlw_ref[...] += jnp.sum(du2 * xhat, axis=0, keepdims=True)
            dlb_ref[...] += jnp.sum(du2, axis=0, keepdims=True)
            dxh = du2 * lnw_ref[...]
            du1 = r * (dxh - jnp.mean(dxh, axis=1, keepdims=True) - xhat * jnp.mean(dxh * xhat, axis=1, keepdims=True))
            dcb_ref[...] += jnp.sum(du1, axis=0, keepdims=True)
            _seg_fill(scr_u, val * sg)
            _seg_fill(scr_d, du1)
            for cc in range(D // 256):
                cs = slice(cc * 256, (cc + 1) * 256)
                for k in range(CK):
                    t = jnp.zeros((GRID_W, 256), f32)
                    for s in range(NSEG):
                        base = SEG_PAD + s * SEG_STRIDE
                        t = t + scr_d[pl.ds(base, GRID_W), cs] * scr_u[pl.ds(base + k - CK // 2, GRID_W), cs]
                    dw_ref[k:k + 1, cs] += jnp.sum(t, axis=0, keepdims=True)
                for s in range(NSEG):
                    base = SEG_PAD + s * SEG_STRIDE
                    acc = jnp.zeros((GRID_W, 256), f32)
                    for k in range(CK):
                        acc = acc + w_ref[k:k + 1, cs] * scr_d[pl.ds(base + CK // 2 - k, GRID_W), cs]
                    du0_scr[s * GRID_W:(s + 1) * GRID_W, cs] = acc
            du0 = du0_scr[...]
            o_ref[:, 0:D] = (du0 * sg).astype(bf16)
            o_ref[:, D:2 * D] = (du0 * val * sg * (1.0 - sg)).astype(bf16)

    xmap = lambda i: (jnp.minimum(i, nlx - 1), 0)
    pmap = lambda cb: (lambda i: (jnp.minimum(i, nlx - 1), cb))
    return pl.pallas_call(
        kern, out_shape=(S(dproj.shape, bf16), S((32, D), f32), S((1, D), f32), S((1, D), f32), S((1, D), f32)), grid=(nt,),
        in_specs=[pl.BlockSpec((RT, D), xmap), pl.BlockSpec((RT, D), xmap),
                  pl.BlockSpec((RT, D), pmap(GV0 // D)), pl.BlockSpec((RT, D), pmap(GG0 // D)), pl.BlockSpec((RT, D), pmap(CG0 // D)),
                  _full((32, D)), _full((1, D)), _full((1, D)), pl.BlockSpec(memory_space=pl.ANY)],
        out_specs=(pl.BlockSpec((RT, 3 * D), lambda i: (i, GV0 // (3 * D))), _full((32, D)), _full((1, D)), _full((1, D)), _full((1, D))),
        scratch_shapes=[pltpu.VMEM((CONF_ROWS, D), f32), pltpu.VMEM((CONF_ROWS, D), f32), pltpu.VMEM((RT, D), f32)],
        input_output_aliases={8: 0},
        compiler_params=_params("arbitrary"), name="conf_bwd")(du3, u1, proj, proj, proj, w32, lnw, lnb, dproj)


def _merge_fwd(bs, bc, proj):
    L = bs.shape[0]

    def kern(bs_ref, bc_ref, g1_ref, g2_ref, o_ref, ot_ref):
        mv = _sig(g1_ref[...]) * bs_ref[...] + _sig(g2_ref[...]) * bc_ref[...]
        o_ref[...] = mv.astype(bf16)
        ot_ref[...] = jnp.transpose(mv).astype(bf16)

    blk = pl.BlockSpec((RT, D), lambda i: (i, 0))
    return pl.pallas_call(
        kern, out_shape=(S((L, D), bf16), S((D, L), bf16)), grid=(L // RT,),
        in_specs=[blk, blk, pl.BlockSpec((RT, D), lambda i: (i, G10 // D)), pl.BlockSpec((RT, D), lambda i: (i, G20 // D))],
        out_specs=(blk, pl.BlockSpec((D, RT), lambda i: (0, i))),
        compiler_params=_params("parallel"), name="merge_fwd")(bs, bc, proj, proj)


def _merge_bwd(dmerged, bs, bc, proj):
    L = bs.shape[0]
    T = proj.shape[0]
    nlx, nt = L // RT, T // RT

    def kern(dm_ref, bs_ref, bc_ref, g1_ref, g2_ref, o_ref, dbs_ref, dbc_ref):
        i = pl.program_id(0)

        @pl.when(i >= nlx)
        def _():
            o_ref[...] = jnp.zeros_like(o_ref)

        @pl.when(i < nlx)
        def _():
            dm = dm_ref[...]
            s1 = _sig(g1_ref[...])
            s2 = _sig(g2_ref[...])
            dbs_ref[...] = (dm * s1).astype(bf16)
            dbc_ref[...] = (dm * s2).astype(bf16)
            o_ref[:, 0:D] = (dm * bs_ref[...] * s1 * (1.0 - s1)).astype(bf16)
            o_ref[:, D:2 * D] = (dm * bc_ref[...] * s2 * (1.0 - s2)).astype(bf16)

    xmap = lambda i: (jnp.minimum(i, nlx - 1), 0)
    pmap = lambda cb: (lambda i: (jnp.minimum(i, nlx - 1), cb))
    xblk = pl.BlockSpec((RT, D), xmap)
    return pl.pallas_call(
        kern, out_shape=(S((T, NP), bf16), S((L, D), bf16), S((L, D), bf16)), grid=(nt,),
        in_specs=[xblk, xblk, xblk, pl.BlockSpec((RT, D), pmap(G10 // D)), pl.BlockSpec((RT, D), pmap(G20 // D))],
        out_specs=(pl.BlockSpec((RT, 2 * D), lambda i: (i, G10 // (2 * D))), xblk, xblk),
        compiler_params=_params("arbitrary"), name="merge_bwd")(dmerged, bs, bc, proj, proj)


def _final(x, out, target, mod, fw):
    L = x.shape[0]

    def kern(x_ref, o_ref, t_ref, mod_ref, fw_ref, dx1_ref, dout_ref, loss_ref, dfw_ref, dg_ref):
        i = pl.program_id(0)

        @pl.when(i == 0)
        def _():
            loss_ref[...] = jnp.zeros_like(loss_ref)
            dfw_ref[...] = jnp.zeros_like(dfw_ref)
            dg_ref[...] = jnp.zeros_like(dg_ref)

        gate = mod_ref[0:1, 2 * D:3 * D]
        ov = o_ref[...]
        x1 = x_ref[...] + gate * ov
        r = lax.rsqrt(jnp.mean(x1 * x1, axis=1, keepdims=True) + EPS)
        xn = x1 * r
        fw = fw_ref[...]
        err = xn * fw - t_ref[...]
        part = 0.5 * jnp.sum(jnp.mean(err * err, axis=1, keepdims=True), axis=0, keepdims=True)
        loss_ref[...] += jnp.broadcast_to(part, (8, 128))
        dy = err * (1.0 / D)
        dfw_ref[...] += jnp.sum(dy * xn, axis=0, keepdims=True)
        dyw = dy * fw
        dx1 = r * (dyw - xn * jnp.mean(dyw * xn, axis=1, keepdims=True))
        dx1_ref[...] = dx1
        dout_ref[...] = (gate * dx1).astype(bf16)
        dg_ref[...] += jnp.sum(dx1 * ov, axis=0, keepdims=True)

    blk = pl.BlockSpec((RT, D), lambda i: (i, 0))
    return pl.pallas_call(
        kern, out_shape=(S((L, D), f32), S((L, D), bf16), S((8, 128), f32), S((1, D), f32), S((1, D), f32)), grid=(L // RT,),
        in_specs=[blk, blk, blk, _full((8, 3 * D)), _full((1, D))],
        out_specs=(blk, blk, _full((8, 128)), _full((1, D)), _full((1, D))),
        compiler_params=_params("arbitrary"), name="final")(x, out, target, mod, fw)


def _me():
    return 4 * lax.axis_index("x") + 2 * lax.axis_index("y") + lax.axis_index("c")


def _xchg_copy(ins, outs, send_sems, recv_sems, modes, a, k, me):
    peer = lax.rem(me + k, N_DEV)
    pid = (peer // 4, lax.rem(peer // 2, 2), lax.rem(peer, 2))
    src = ins[a].at[peer] if modes[a] else ins[a]
    return pltpu.make_async_remote_copy(src_ref=src, dst_ref=outs[a].at[me], send_sem=send_sems.at[a, k - 1],
                                        recv_sem=recv_sems.at[a, k - 1], device_id=pid, device_id_type=MESH)


def _xchg_local(ins, outs, loc_sems, modes, a, me):
    return pltpu.make_async_copy(ins[a].at[me] if modes[a] else ins[a], outs[a].at[me], loc_sems.at[a])


def _xchg_start(ins, outs, send_sems, recv_sems, loc_sems, modes):
    me = _me()
    for a in range(len(modes)):
        _xchg_local(ins, outs, loc_sems, modes, a, me).start()
        for k in range(1, N_DEV):
            _xchg_copy(ins, outs, send_sems, recv_sems, modes, a, k, me).start()


def _xchg_wait(ins, outs, send_sems, recv_sems, loc_sems, modes):
    me = _me()
    for a in range(len(modes)):
        for k in range(1, N_DEV):
            frm = lax.rem(me + N_DEV - k, N_DEV)
            src = ins[a].at[frm] if modes[a] else ins[a]
            pltpu.make_async_remote_copy(src_ref=src, dst_ref=outs[a].at[frm], send_sem=send_sems.at[a, k - 1],
                                         recv_sem=recv_sems.at[a, k - 1], device_id=(0, 0, 0), device_id_type=MESH).wait_recv()
    for a in range(len(modes)):
        for k in range(1, N_DEV):
            _xchg_copy(ins, outs, send_sems, recv_sems, modes, a, k, me).wait_send()
        _xchg_local(ins, outs, loc_sems, modes, a, me).wait()


def _xchg_out_shapes(arrs, modes):
    return tuple(S((N_DEV,) + (a.shape[1:] if sc else a.shape), a.dtype) for a, sc in zip(arrs, modes))


def _xchg_sems(n):
    return [pltpu.SemaphoreType.DMA((n, N_DEV - 1)), pltpu.SemaphoreType.DMA((n, N_DEV - 1)), pltpu.SemaphoreType.DMA((n,))]


def _exchange(arrs, modes, name):
    n = len(arrs)

    def kern(*refs):
        ins, outs, sems = refs[:n], refs[n:2 * n], refs[2 * n:]
        _xchg_start(ins, outs, *sems, modes)
        _xchg_wait(ins, outs, *sems, modes)

    anyspec = pl.BlockSpec(memory_space=pl.ANY)
    return pl.pallas_call(
        kern, out_shape=_xchg_out_shapes(arrs, modes), in_specs=[anyspec] * n, out_specs=tuple([anyspec] * n),
        scratch_shapes=_xchg_sems(n), name=name)(*arrs)


def _gather2(arrs, name):
    n = len(arrs)

    def kern(*refs):
        ins, outs = refs[:n], refs[n:2 * n]
        send_sems, recv_sems, loc_sems = refs[2 * n:]
        x, y, c = lax.axis_index("x"), lax.axis_index("y"), lax.axis_index("c")
        me, sib = (x, y, c), (x, y, 1 - c)
        chips = [(1 - x, y), (x, 1 - y), (1 - x, 1 - y)]

        def slot(a, p):
            return outs[a].at[4 * p[0] + 2 * p[1] + p[2]]

        def cp(a, k, block, to, own=False):
            return pltpu.make_async_remote_copy(src_ref=ins[a] if own else slot(a, block), dst_ref=slot(a, block),
                                                send_sem=send_sems.at[a, k], recv_sem=recv_sems.at[a, k],
                                                device_id=to, device_id_type=MESH)

        started = []
        for a in range(n):
            loc = pltpu.make_async_copy(ins[a], slot(a, me), loc_sems.at[a])
            loc.start()
            started.append(cp(a, 0, me, sib, own=True))
            started += [cp(a, 1 + j, me, (*chip, c), own=True) for j, chip in enumerate(chips)]
        for s in started:
            s.start()
        for j, chip in enumerate(chips):
            for a in range(n):
                cp(a, 1 + j, (*chip, c), me).wait_recv()
                fwd = cp(a, 4 + j, (*chip, c), sib)
                fwd.start()
                started.append(fwd)
        for a in range(n):
            cp(a, 0, sib, me).wait_recv()
            for j, chip in enumerate(chips):
                cp(a, 4 + j, (*chip, 1 - c), me).wait_recv()
        for s in started:
            s.wait_send()
        for a in range(n):
            pltpu.make_async_copy(ins[a], slot(a, me), loc_sems.at[a]).wait()

    anyspec = pl.BlockSpec(memory_space=pl.ANY)
    return pl.pallas_call(
        kern, out_shape=_xchg_out_shapes(arrs, (False,) * n), in_specs=[anyspec] * n, out_specs=tuple([anyspec] * n),
        scratch_shapes=[pltpu.SemaphoreType.DMA((n, 7)), pltpu.SemaphoreType.DMA((n, 7)), pltpu.SemaphoreType.DMA((n,))],
        name=name)(*arrs)


def _adamw(parts, w, m, v, name):
    r, c = w.shape
    tr = r
    for cand in (128, 64, 32, 16, 8):
        if r % cand == 0 and r > cand:
            tr = cand
            break
    c1 = 1.0 / (1.0 - ADAM_B1 ** ADAM_STEP)
    c2 = 1.0 / (1.0 - ADAM_B2 ** ADAM_STEP)

    def kern(p_ref, w_ref, m_ref, v_ref, g_ref, d_ref, m2_ref, v2_ref):
        g = p_ref[0].astype(f32)
        for i in range(1, N_DEV):
            g = g + p_ref[i].astype(f32)
        g_ref[...] = g
        m2 = ADAM_B1 * m_ref[...] + (1.0 - ADAM_B1) * g
        v2 = ADAM_B2 * v_ref[...] + (1.0 - ADAM_B2) * (g * g)
        m2_ref[...] = m2
        v2_ref[...] = v2
        d_ref[...] = -ADAM_LR * ((m2 * c1) / (jnp.sqrt(v2 * c2) + ADAM_EPS) + ADAM_WD * w_ref[...])

    blk = pl.BlockSpec((tr, c), lambda i: (i, 0))
    sh = S((r, c), f32)
    return pl.pallas_call(
        kern, out_shape=(sh, sh, sh, sh), grid=(r // tr,),
        in_specs=[pl.BlockSpec((N_DEV, tr, c), lambda i: (0, i, 0)), blk, blk, blk], out_specs=(blk, blk, blk, blk),
        compiler_params=_params("parallel"), name=name)(parts, w, m, v)


_SMALL = (("c_ctx", 1024), ("b_mod", 3072), ("norm_w", 1024), ("ssm_conv_b", 4096), ("dt_bias", 64), ("a_log", 64),
          ("d_skip", 32), ("ssm_norm_w", 2048), ("conf_conv_b", 1024), ("conf_ln_w", 1024), ("conf_ln_b", 1024),
          ("final_norm_w", 1024))
SMALL_TILE = 8 * 128


def _pack_small(d):
    rows = []
    for name, n in _SMALL:
        v = d[name].reshape(-1).astype(f32)
        pad = (-n) % SMALL_TILE
        if pad:
            v = jnp.concatenate([v, jnp.zeros((pad,), f32)])
        rows.append(v.reshape(-1, 128))
    return jnp.concatenate(rows, axis=0)


def _unpack_small(p, shapes):
    out, r0 = {}, 0
    for name, n in _SMALL:
        nr = 8 * ((n + SMALL_TILE - 1) // SMALL_TILE)
        out[name] = p[r0:r0 + nr].reshape(-1)[:n].reshape(shapes[name])
        r0 += nr
    return out


def _permute_w_in(w):
    return jnp.concatenate([w[:, 0:4096], w[:, 4160:6208], w[:, 9280:11328], w[:, 4096:4160],
                            jnp.zeros((w.shape[0], 1024 - 64), w.dtype), w[:, 6208:9280]], axis=1)


def _unpermute_w_in(wp):
    return jnp.concatenate([wp[:, 0:4096], wp[:, DT0:DT0 + 64], wp[:, Z0:Z0 + 2048], wp[:, GV0:GV0 + 3072],
                            wp[:, G10:G10 + 2048]], axis=1)


def _cols_gathered(g):
    return jnp.transpose(g, (1, 0, 2)).reshape(g.shape[1], N_DEV * g.shape[2])


def _cols_to_blocks(a):
    r, c8 = a.shape
    return jnp.transpose(a.reshape(r, N_DEV, c8 // N_DEV), (1, 0, 2))


def kernel(x, c, ctx, c_ctx, w_mod, b_mod, norm_w, w_in, ssm_conv_w, ssm_conv_b, dt_bias, a_log, d_skip, ssm_norm_w, w_out_ssm, conf_conv_w, conf_conv_b, conf_ln_w, conf_ln_b, w_out_conf, w_out, final_norm_w, loss_target, m_c_ctx, m_w_mod, m_b_mod, m_norm_w, m_w_in, m_ssm_conv_w, m_ssm_conv_b, m_dt_bias, m_a_log, m_d_skip, m_ssm_norm_w, m_w_out_ssm, m_conf_conv_w, m_conf_conv_b, m_conf_ln_w, m_conf_ln_b, m_w_out_conf, m_w_out, m_final_norm_w, v_c_ctx, v_w_mod, v_b_mod, v_norm_w, v_w_in, v_ssm_conv_w, v_ssm_conv_b, v_dt_bias, v_a_log, v_d_skip, v_ssm_norm_w, v_w_out_ssm, v_conf_conv_w, v_conf_conv_b, v_conf_ln_w, v_conf_ln_b, v_w_out_conf, v_w_out, v_final_norm_w):
    L = x.shape[1]
    Lc = ctx.shape[1]
    T = L + Lc
    nlx = L // RT
    nxc = L // Q
    x2 = x.reshape(L, D)
    ctx2 = ctx.reshape(Lc, D)
    tgt = loss_target.reshape(L, D)

    gathered = _gather2(
        [w_in[0].astype(bf16), w_mod[0].astype(bf16), w_out_ssm[0].astype(bf16), w_out_conf[0].astype(bf16),
         w_out[0].astype(bf16), ssm_conv_w[0], conf_conv_w[0]], name="gather_weights")
    wp = _permute_w_in(_cols_gathered(gathered[0]))
    wmod_bf = _cols_gathered(gathered[1])
    wos_bf = gathered[2].reshape(DI, D)
    woc_bf = gathered[3].reshape(D, D)
    wo_bf = gathered[4].reshape(D, D)
    scw8 = jnp.concatenate([_cols_gathered(gathered[5]), jnp.zeros((8 - SK, 4096), f32)], axis=0)
    ccw32 = jnp.concatenate([_cols_gathered(gathered[6]), jnp.zeros((32 - CK, D), f32)], axis=0)

    norm_w1 = norm_w.reshape(1, D)
    scb = ssm_conv_b.reshape(1, 4096)
    bias_row = jnp.concatenate([dt_bias.reshape(1, 2 * NH), jnp.zeros((1, 128 - 2 * NH), f32)], axis=1)
    alog_row = jnp.concatenate([a_log.reshape(1, 2 * NH), jnp.zeros((1, 128 - 2 * NH), f32)], axis=1)
    dskip_full = jnp.repeat(d_skip.reshape(NH), HD).reshape(1, DI)
    snw = ssm_norm_w.reshape(1, DI)
    ccb = conf_conv_b.reshape(1, D)
    lnw = conf_ln_w.reshape(1, D)
    lnb = conf_ln_b.reshape(1, D)
    fw = final_norm_w.reshape(1, D)

    cc8 = jnp.concatenate([c.reshape(1, D), c_ctx.reshape(1, D), jnp.zeros((6, D), f32)], axis=0)
    mod = _mod_fwd(cc8, wmod_bf, b_mod.reshape(1, 3 * D))
    h, h_t = _prenorm(x2, ctx2, norm_w1, mod)
    proj = _matmul(h, wp, f32, "proj")
    xbc = _ssm_conv_fwd(proj, scw8, scb, nlx)
    dt, la = _dt_prep(proj, bias_row, alog_row)
    consts_f, consts_b = _scan_consts(False), _scan_consts(True)
    yf, hp_f = _ssd_fwd(xbc, dt, la, consts_f, False, nxc, "ssd_fwd_f")
    yb, hp_b = _ssd_fwd(xbc, dt, la, consts_b, True, nxc, "ssd_fwd_b")
    y, yn, yn_t = _ynorm_fwd(yf, yb, xbc, proj, dskip_full, snw, L)
    bs = _matmul(yn, wos_bf, f32, "branch_ssm")
    u1, u3, u3_t = _conf_fwd(proj, ccw32, ccb, lnw, lnb, L)
    bc = _matmul(u3, woc_bf, f32, "branch_conf")
    merged, merged_t = _merge_fwd(bs, bc, proj)
    out = _matmul(merged, wo_bf, f32, "out_proj")
    dx1, dout, loss_acc, dfw, dgate = _final(x2, out, tgt, mod, fw)

    dmerged = _matmul(dout, wo_bf, f32, "d_merged", tb=True)
    g_wo = _matmul(merged_t, dout, bf16, "g_w_out")
    dproj, dbs, dbc = _merge_bwd(dmerged, bs, bc, proj)
    dyn = _matmul(dbs, wos_bf, f32, "d_yn", tb=True)
    g_wos = _matmul(yn_t, dbs, bf16, "g_w_out_ssm")
    du3 = _matmul(dbc, woc_bf, f32, "d_u3", tb=True)
    g_woc = _matmul(u3_t, dbc, bf16, "g_w_out_conf")
    dproj, g_ccw, g_ccb, g_lnw, g_lnb = _conf_bwd(du3, u1, proj, ccw32, lnw, lnb, dproj)
    dproj, dy, g_snw, dsk_cols = _ynorm_bwd(dyn, y, xbc, proj, dskip_full, snw, dproj)
    acc_f = _ssd_bwd(xbc, dy, dt, la, hp_f, dskip_full, consts_f, False, nxc, "ssd_bwd_f")
    dxbc, a1, a2, r2, sv = _ssd_bwd(xbc, dy, dt, la, hp_b, dskip_full, consts_b, True, nxc, "ssd_bwd_b", acc=acc_f)
    dproj, g_dtb, g_alog = _dt_bwd(a1, a2, r2, sv, dt, la, proj, bias_row, alog_row, dproj)
    dpre, g_scw, g_scb = _ssm_conv_dpre(dxbc, proj, scw8, scb, nlx)
    dproj = _ssm_conv_t(dpre, scw8, dproj, nlx)
    g_wp = _matmul(h_t, dproj, bf16, "g_w_in", tm=1024)
    dh, *parts = _matmul(
        dproj, wp, f32, "d_h_scatter", tb=True,
        comm=([_cols_to_blocks(_unpermute_w_in(g_wp)), g_wos.reshape(N_DEV, DI // N_DEV, D), g_woc.reshape(N_DEV, D // N_DEV, D),
               g_wo.reshape(N_DEV, D // N_DEV, D), _cols_to_blocks(g_scw[:SK]), _cols_to_blocks(g_ccw[:CK])], (True,) * 6))
    gx, g_nw, macc = _prenorm_bwd(x2, ctx2, dh, dx1, norm_w1, mod)
    z1 = jnp.zeros((1, D), f32)
    dmod8 = jnp.concatenate([jnp.concatenate([macc[0:1], macc[1:2], dgate], axis=1),
                             jnp.concatenate([macc[2:3], macc[3:4], z1], axis=1), jnp.zeros((6, 3 * D), f32)], axis=0)
    ct = jnp.concatenate([c.reshape(D, 1), c_ctx.reshape(D, 1), jnp.zeros((D, 126), f32)], axis=1)
    g_wmod, g_bmod, g_cctx = _mod_bwd(ct, dmod8, wmod_bf)
    g_dskip = _head_sums(dsk_cols.reshape(NH, HD))[:, 0]

    small_g = _pack_small({
        "c_ctx": g_cctx[:, 0], "b_mod": g_bmod, "norm_w": g_nw, "ssm_conv_b": g_scb, "dt_bias": g_dtb[0, :2 * NH],
        "a_log": g_alog[0, :2 * NH], "d_skip": g_dskip, "ssm_norm_w": g_snw, "conf_conv_b": g_ccb, "conf_ln_w": g_lnw,
        "conf_ln_b": g_lnb, "final_norm_w": dfw})
    wmod_parts, small_parts = _exchange([_cols_to_blocks(g_wmod), small_g], (True, False), name="exchange_tail")
    parts = [parts[0], wmod_parts] + parts[1:]

    given = dict(c_ctx=c_ctx, w_mod=w_mod, b_mod=b_mod, norm_w=norm_w, w_in=w_in, ssm_conv_w=ssm_conv_w, ssm_conv_b=ssm_conv_b,
                 dt_bias=dt_bias, a_log=a_log, d_skip=d_skip, ssm_norm_w=ssm_norm_w, w_out_ssm=w_out_ssm, conf_conv_w=conf_conv_w,
                 conf_conv_b=conf_conv_b, conf_ln_w=conf_ln_w, conf_ln_b=conf_ln_b, w_out_conf=w_out_conf, w_out=w_out,
                 final_norm_w=final_norm_w)
    ms = dict(c_ctx=m_c_ctx, w_mod=m_w_mod, b_mod=m_b_mod, norm_w=m_norm_w, w_in=m_w_in, ssm_conv_w=m_ssm_conv_w,
              ssm_conv_b=m_ssm_conv_b, dt_bias=m_dt_bias, a_log=m_a_log, d_skip=m_d_skip, ssm_norm_w=m_ssm_norm_w,
              w_out_ssm=m_w_out_ssm, conf_conv_w=m_conf_conv_w, conf_conv_b=m_conf_conv_b, conf_ln_w=m_conf_ln_w,
              conf_ln_b=m_conf_ln_b, w_out_conf=m_w_out_conf, w_out=m_w_out, final_norm_w=m_final_norm_w)
    vs = dict(c_ctx=v_c_ctx, w_mod=v_w_mod, b_mod=v_b_mod, norm_w=v_norm_w, w_in=v_w_in, ssm_conv_w=v_ssm_conv_w,
              ssm_conv_b=v_ssm_conv_b, dt_bias=v_dt_bias, a_log=v_a_log, d_skip=v_d_skip, ssm_norm_w=v_ssm_norm_w,
              w_out_ssm=v_w_out_ssm, conf_conv_w=v_conf_conv_w, conf_conv_b=v_conf_conv_b, conf_ln_w=v_conf_ln_w,
              conf_ln_b=v_conf_ln_b, w_out_conf=v_w_out_conf, w_out=v_w_out, final_norm_w=v_final_norm_w)
    grads, deltas, new_m, new_v = {}, {}, {}, {}
    sharded = ("w_in", "w_mod", "w_out_ssm", "w_out_conf", "w_out", "ssm_conv_w", "conf_conv_w")
    for i, nm in enumerate(sharded):
        shp = given[nm].shape
        w2 = given[nm].reshape(shp[1], shp[2])
        res = _adamw(parts[i], w2, ms[nm].reshape(w2.shape), vs[nm].reshape(w2.shape), "adamw_" + nm)
        grads[nm], deltas[nm], new_m[nm], new_v[nm] = [r.reshape(shp) for r in res]
    shapes = {nm: given[nm].shape for nm, _ in _SMALL}
    res = _adamw(small_parts, _pack_small(given), _pack_small(ms), _pack_small(vs), "adamw_small")
    for dst, packed in zip((grads, deltas, new_m, new_v), res):
        dst.update(_unpack_small(packed, shapes))

    loss = lax.psum(loss_acc[0, 0], ("x", "y", "c"))
    order = ("c_ctx", "w_mod", "b_mod", "norm_w", "w_in", "ssm_conv_w", "ssm_conv_b", "dt_bias", "a_log", "d_skip", "ssm_norm_w",
             "w_out_ssm", "conf_conv_w", "conf_conv_b", "conf_ln_w", "conf_ln_b", "w_out_conf", "w_out", "final_norm_w")
    return (loss, gx.reshape(1, L, D), *[grads[n] for n in order], *[deltas[n] for n in order],
            *[new_m[n] for n in order], *[new_v[n] for n in order])
```

```python
import jax
import jax.numpy as jnp
from jax import lax
from jax.experimental import pallas as pl
from jax.experimental.pallas import tpu as pltpu

f32 = jnp.float32
bf16 = jnp.bfloat16

D = 1024
DI = 2048
NG = 8
HPG = 4
HD = 64
NS = 128
NH = 32
Q = 128
GRID_W = 64
CK = 31
SK = 4
EPS = 1e-6
RT = 256
N_DEV = 8
IN_COLS = 11328
X0, B0, C0, Z0, G10, G20, DT0, GV0, GG0, CG0, NP = 0, 2048, 3072, 4096, 6144, 7168, 8192, 9216, 10240, 11264, 12288
VMEM_LIMIT = 50 * 1024 * 1024
NEG = -1e30

ADAM_LR, ADAM_B1, ADAM_B2, ADAM_EPS, ADAM_WD, ADAM_STEP = 0.001, 0.9, 0.999, 1e-08, 0.01, 10

MESH = pl.DeviceIdType.MESH
S = jax.ShapeDtypeStruct


def _params(*sem):
    return pltpu.CompilerParams(dimension_semantics=tuple(sem) if sem else None, vmem_limit_bytes=VMEM_LIMIT)


def _sig(x):
    return 1.0 / (1.0 + jnp.exp(-x))


def _silu(x):
    return x * _sig(x)


def _dsilu(x, s):
    return s * (1.0 + x * (1.0 - s))


def _dot(a, b):
    return jnp.dot(a, b, preferred_element_type=f32)


def _dot_nt(a, b):
    return lax.dot_general(a, b, (((1,), (1,)), ((), ())), preferred_element_type=f32)


def _dot_tn(a, b):
    return lax.dot_general(a, b, (((0,), (0,)), ((), ())), preferred_element_type=f32)


def _dot3(t_bf, v):
    v1 = v.astype(bf16)
    r1 = v - v1.astype(f32)
    v2 = r1.astype(bf16)
    v3 = (r1 - v2.astype(f32)).astype(bf16)
    return _dot(t_bf, v1) + _dot(t_bf, v2) + _dot(t_bf, v3)


def _pick(n, prefs):
    for p in prefs:
        if n % p == 0:
            return p
    return n


def _full(shape):
    nd = len(shape)
    return pl.BlockSpec(shape, lambda *_: (0,) * nd)


def _matmul(a, b, out_dtype, name, tm=None, tn=None, tk=None, tb=False, comm=None):
    m, k = a.shape
    n = b.shape[0] if tb else b.shape[1]
    tm = tm or _pick(m, (768, 512, 256, 128))
    tn = tn or _pick(n, (1024, 512, 256, 128))
    tk = tk or _pick(k, (1024, 768, 512, 256, 128))
    nk = k // tk
    gi, gj = m // tm, n // tn
    carrs, modes = comm if comm else ((), ())
    nc = len(carrs)

    def kern(*refs):
        a_ref, b_ref = refs[:2]
        cins = refs[2:2 + nc]
        o_ref = refs[2 + nc]
        couts = refs[3 + nc:3 + 2 * nc]
        acc_ref = refs[3 + 2 * nc]
        sems = refs[4 + 2 * nc:]
        i, j, kk = pl.program_id(0), pl.program_id(1), pl.program_id(2)
        if nc:
            @pl.when(jnp.logical_and(jnp.logical_and(i == 0, j == 0), kk == 0))
            def _():
                _xchg_start(cins, couts, *sems, modes)

        part = _dot_nt(a_ref[...], b_ref[...]) if tb else _dot(a_ref[...], b_ref[...])
        if nk == 1:
            o_ref[...] = part.astype(o_ref.dtype)
        else:
            @pl.when(kk == 0)
            def _():
                acc_ref[...] = part

            @pl.when(kk > 0)
            def _():
                acc_ref[...] += part

            @pl.when(kk == nk - 1)
            def _():
                o_ref[...] = acc_ref[...].astype(o_ref.dtype)

        if nc:
            @pl.when(jnp.logical_and(jnp.logical_and(i == gi - 1, j == gj - 1), kk == nk - 1))
            def _():
                _xchg_wait(cins, couts, *sems, modes)

    anyspec = pl.BlockSpec(memory_space=pl.ANY)
    bspec = pl.BlockSpec((tn, tk), lambda i, j, kk: (j, kk)) if tb else pl.BlockSpec((tk, tn), lambda i, j, kk: (kk, j))
    out_shape = (S((m, n), out_dtype),) + _xchg_out_shapes(carrs, modes)
    res = pl.pallas_call(
        kern, out_shape=out_shape, grid=(gi, gj, nk),
        in_specs=[pl.BlockSpec((tm, tk), lambda i, j, kk: (i, kk)), bspec] + [anyspec] * nc,
        out_specs=(pl.BlockSpec((tm, tn), lambda i, j, kk: (i, j)),) + (anyspec,) * nc,
        scratch_shapes=[pltpu.VMEM((tm, tn), f32)] + (_xchg_sems(nc) if nc else []),
        compiler_params=_params(*((("arbitrary",) * 3) if nc else ("parallel", "parallel", "arbitrary"))), name=name)(a, b, *carrs)
    return res if nc else res[0]


def _mod_fwd(cc8, w_mod_bf, b_mod):
    def kern(c_ref, w_ref, b_ref, o_ref):
        o_ref[...] = _dot(_silu(c_ref[...]).astype(bf16), w_ref[...]) + b_ref[...]

    return pl.pallas_call(kern, out_shape=S((8, 3 * D), f32), compiler_params=_params(), name="mod_fwd")(cc8, w_mod_bf, b_mod)


def _mod_bwd(ct, dmod8, w_mod_bf):
    tc = 512
    nj = 3 * D // tc

    def kern(ct_ref, dm_ref, w_ref, dw_ref, db_ref, dc_ref):
        j = pl.program_id(0)
        c = ct_ref[:, 0:1]
        cx = ct_ref[:, 1:2]
        sx = _sig(cx)
        dmx = dm_ref[0:1, :]
        dmc = dm_ref[1:2, :]
        dw_ref[...] = (_silu(c) * dmx + (cx * sx) * dmc).astype(bf16)
        db_ref[...] = dmx + dmc
        t = jnp.sum(w_ref[...].astype(f32) * dmc.astype(bf16).astype(f32), axis=1, keepdims=True) * _dsilu(cx, sx)

        @pl.when(j == 0)
        def _():
            dc_ref[...] = jnp.zeros_like(dc_ref)

        dc_ref[...] += jnp.broadcast_to(t, (D, 128))

    return pl.pallas_call(
        kern, out_shape=(S((D, 3 * D), bf16), S((1, 3 * D), f32), S((D, 128), f32)), grid=(nj,),
        in_specs=[_full((D, 128)), pl.BlockSpec((8, tc), lambda j: (0, j)), pl.BlockSpec((D, tc), lambda j: (0, j))],
        out_specs=(pl.BlockSpec((D, tc), lambda j: (0, j)), pl.BlockSpec((1, tc), lambda j: (0, j)), _full((D, 128))),
        compiler_params=_params("arbitrary"), name="mod_bwd")(ct, dmod8, w_mod_bf)


def _prenorm(x, ctx, norm_w, mod):
    L, Lc = x.shape[0], ctx.shape[0]
    nlx, nt = L // RT, (L + Lc) // RT

    def kern(x_ref, c_ref, nw_ref, mod_ref, h_ref, ht_ref):
        i = pl.program_id(0)
        is_c = i >= nlx
        xv = jnp.where(is_c, c_ref[...], x_ref[...])
        shift = jnp.where(is_c, mod_ref[1:2, 0:D], mod_ref[0:1, 0:D])
        scale = jnp.where(is_c, mod_ref[1:2, D:2 * D], mod_ref[0:1, D:2 * D])
        r = lax.rsqrt(jnp.mean(xv * xv, axis=1, keepdims=True) + EPS)
        hv = (xv * r) * nw_ref[...] * (1.0 + scale) + shift
        h_ref[...] = hv.astype(bf16)
        ht_ref[...] = jnp.transpose(hv).astype(bf16)

    return pl.pallas_call(
        kern, out_shape=(S((L + Lc, D), bf16), S((D, L + Lc), bf16)), grid=(nt,),
        in_specs=[pl.BlockSpec((RT, D), lambda i: (jnp.minimum(i, nlx - 1), 0)),
                  pl.BlockSpec((RT, D), lambda i: (jnp.maximum(i - nlx, 0), 0)),
                  _full((1, D)), _full((8, 3 * D))],
        out_specs=(pl.BlockSpec((RT, D), lambda i: (i, 0)), pl.BlockSpec((D, RT), lambda i: (0, i))),
        compiler_params=_params("parallel"), name="prenorm")(x, ctx, norm_w, mod)


def _prenorm_bwd(x, ctx, dh, dx1, norm_w, mod):
    L, Lc = x.shape[0], ctx.shape[0]
    nlx, nt = L // RT, (L + Lc) // RT

    def kern(x_ref, c_ref, dh_ref, dx1_ref, nw_ref, mod_ref, gx_ref, dnw_ref, acc_ref):
        i = pl.program_id(0)
        is_c = i >= nlx

        @pl.when(i == 0)
        def _():
            dnw_ref[...] = jnp.zeros_like(dnw_ref)
            acc_ref[...] = jnp.zeros_like(acc_ref)

        xv = jnp.where(is_c, c_ref[...], x_ref[...])
        scale = jnp.where(is_c, mod_ref[1:2, D:2 * D], mod_ref[0:1, D:2 * D])
        nw = nw_ref[...]
        r = lax.rsqrt(jnp.mean(xv * xv, axis=1, keepdims=True) + EPS)
        xn = xv * r
        dh = dh_ref[...]
        dsh = jnp.sum(dh, axis=0, keepdims=True)
        dsc = jnp.sum(dh * (xn * nw), axis=0, keepdims=True)
        dxnw = dh * (1.0 + scale)
        dnw_ref[...] += jnp.sum(dxnw * xn, axis=0, keepdims=True)
        dxn = dxnw * nw
        dx = r * (dxn - xn * jnp.mean(dxn * xn, axis=1, keepdims=True))

        @pl.when(jnp.logical_not(is_c))
        def _():
            gx_ref[...] = dx1_ref[...] + dx
            acc_ref[0:1, :] += dsh
            acc_ref[1:2, :] += dsc

        @pl.when(is_c)
        def _():
            acc_ref[2:3, :] += dsh
            acc_ref[3:4, :] += dsc

    xmap = lambda i: (jnp.minimum(i, nlx - 1), 0)
    return pl.pallas_call(
        kern, out_shape=(S((L, D), f32), S((1, D), f32), S((8, D), f32)), grid=(nt,),
        in_specs=[pl.BlockSpec((RT, D), xmap), pl.BlockSpec((RT, D), lambda i: (jnp.maximum(i - nlx, 0), 0)),
                  pl.BlockSpec((RT, D), lambda i: (i, 0)), pl.BlockSpec((RT, D), xmap), _full((1, D)), _full((8, 3 * D))],
        out_specs=(pl.BlockSpec((RT, D), xmap), _full((1, D)), _full((8, D))),
        compiler_params=_params("arbitrary"), name="prenorm_bwd")(x, ctx, dh, dx1, norm_w, mod)


def _halo_specs(nt_rows, ct):
    cur = pl.BlockSpec((RT, ct), lambda i, j: (i, j))
    prev = pl.BlockSpec((8, ct), lambda i, j: (jnp.maximum(i * (RT // 8) - 1, 0), j))
    nxt = pl.BlockSpec((8, ct), lambda i, j: (jnp.minimum((i + 1) * (RT // 8), nt_rows // 8 - 1), j))
    return cur, prev, nxt


def _fill_halo(scr, cur_ref, prev_ref, next_ref, i, nlx, nt):
    prev_ok = jnp.logical_and(i != 0, i != nlx)
    next_ok = jnp.logical_and(i != nlx - 1, i != nt - 1)
    scr[0:8, :] = jnp.where(prev_ok, prev_ref[...], 0.0)
    scr[8:8 + RT, :] = cur_ref[...]
    scr[8 + RT:16 + RT, :] = jnp.where(next_ok, next_ref[...], 0.0)


CONV_RB = 32


def _conv_blocks(ct):
    return [(slice(cb * 128, (cb + 1) * 128), r0) for cb in range(ct // 128) for r0 in range(0, RT, CONV_RB)]


def _ssm_conv_fwd(proj, w8, b, nlx):
    T = proj.shape[0]
    nt = T // RT
    ct = 1024
    cur, prev, nxt = _halo_specs(T, ct)

    def kern(cur_ref, prev_ref, next_ref, w_ref, b_ref, o_ref, scr):
        i = pl.program_id(0)
        _fill_halo(scr, cur_ref, prev_ref, next_ref, i, nlx, nt)
        for cs, r0 in _conv_blocks(ct):
            acc = jnp.broadcast_to(b_ref[:, cs], (CONV_RB, 128))
            for k in range(SK):
                acc = acc + w_ref[k:k + 1, cs] * scr[pl.ds(6 + k + r0, CONV_RB), cs]
            o_ref[r0:r0 + CONV_RB, cs] = _silu(acc)

    return pl.pallas_call(
        kern, out_shape=S((T, 4096), f32), grid=(nt, 4096 // ct),
        in_specs=[cur, prev, nxt, pl.BlockSpec((8, ct), lambda i, j: (0, j)), pl.BlockSpec((1, ct), lambda i, j: (0, j))],
        out_specs=pl.BlockSpec((RT, ct), lambda i, j: (i, j)),
        scratch_shapes=[pltpu.VMEM((RT + 16, ct), f32)],
        compiler_params=_params("parallel", "parallel"), name="ssm_conv_fwd")(proj, proj, proj, w8, b)


def _ssm_conv_dpre(dxbc, proj, w8, b, nlx):
    T = proj.shape[0]
    nt = T // RT
    ct = 1024
    cur = pl.BlockSpec((RT, ct), lambda j, i: (i, j))
    prev = pl.BlockSpec((8, ct), lambda j, i: (jnp.maximum(i * (RT // 8) - 1, 0), j))
    nxt = pl.BlockSpec((8, ct), lambda j, i: (jnp.minimum((i + 1) * (RT // 8), T // 8 - 1), j))

    def kern(d_ref, cur_ref, prev_ref, next_ref, w_ref, b_ref, dpre_ref, dw_ref, db_ref, scr):
        i = pl.program_id(1)
        _fill_halo(scr, cur_ref, prev_ref, next_ref, i, nlx, nt)

        @pl.when(i == 0)
        def _():
            dw_ref[...] = jnp.zeros_like(dw_ref)
            db_ref[...] = jnp.zeros_like(db_ref)

        for cb in range(ct // 128):
            cs = slice(cb * 128, (cb + 1) * 128)
            db_acc = jnp.zeros((CONV_RB, 128), f32)
            dw_acc = [jnp.zeros((CONV_RB, 128), f32) for _ in range(SK)]
            for r0 in range(0, RT, CONV_RB):
                taps = [scr[pl.ds(6 + k + r0, CONV_RB), cs] for k in range(SK)]
                pre = jnp.broadcast_to(b_ref[:, cs], (CONV_RB, 128))
                for k in range(SK):
                    pre = pre + w_ref[k:k + 1, cs] * taps[k]
                dpre = d_ref[r0:r0 + CONV_RB, cs] * _dsilu(pre, _sig(pre))
                dpre_ref[r0:r0 + CONV_RB, cs] = dpre
                db_acc = db_acc + dpre
                dw_acc = [dw_acc[k] + dpre * taps[k] for k in range(SK)]
            db_ref[:, cs] += jnp.sum(db_acc, axis=0, keepdims=True)
            for k in range(SK):
                dw_ref[k:k + 1, cs] += jnp.sum(dw_acc[k], axis=0, keepdims=True)

    return pl.pallas_call(
        kern, out_shape=(S((T, 4096), f32), S((8, 4096), f32), S((1, 4096), f32)), grid=(4096 // ct, nt),
        in_specs=[cur, cur, prev, nxt, pl.BlockSpec((8, ct), lambda j, i: (0, j)), pl.BlockSpec((1, ct), lambda j, i: (0, j))],
        out_specs=(cur, pl.BlockSpec((8, ct), lambda j, i: (0, j)), pl.BlockSpec((1, ct), lambda j, i: (0, j))),
        scratch_shapes=[pltpu.VMEM((RT + 16, ct), f32)],
        compiler_params=_params("parallel", "arbitrary"), name="ssm_conv_dpre")(dxbc, proj, proj, proj, w8, b)


def _ssm_conv_t(dpre, w8, dproj, nlx):
    T = dpre.shape[0]
    nt = T // RT
    ct = 1024
    cur, prev, nxt = _halo_specs(T, ct)

    def kern(cur_ref, prev_ref, next_ref, w_ref, _alias, o_ref, scr):
        i = pl.program_id(0)
        _fill_halo(scr, cur_ref, prev_ref, next_ref, i, nlx, nt)
        for cs, r0 in _conv_blocks(ct):
            acc = jnp.zeros((CONV_RB, 128), f32)
            for k in range(SK):
                acc = acc + w_ref[k:k + 1, cs] * scr[pl.ds(10 - k + r0, CONV_RB), cs]
            o_ref[r0:r0 + CONV_RB, cs] = acc.astype(bf16)

    return pl.pallas_call(
        kern, out_shape=S(dproj.shape, bf16), grid=(nt, 4096 // ct),
        in_specs=[cur, prev, nxt, pl.BlockSpec((8, ct), lambda i, j: (0, j)), pl.BlockSpec(memory_space=pl.ANY)],
        out_specs=pl.BlockSpec((RT, ct), lambda i, j: (i, j)),
        scratch_shapes=[pltpu.VMEM((RT + 16, ct), f32)], input_output_aliases={4: 0},
        compiler_params=_params("parallel", "parallel"), name="ssm_conv_t")(dpre, dpre, dpre, w8, dproj)


def _tri():
    li = lax.broadcasted_iota(jnp.int32, (Q, Q), 0)
    si = lax.broadcasted_iota(jnp.int32, (Q, Q), 1)
    return (si <= li).astype(bf16), (si >= li).astype(bf16)


def _dt_prep(proj, bias_row, alog_row):
    T = proj.shape[0]
    nch = T // Q

    def kern(raw_ref, b_ref, al_ref, dt_ref, la_ref):
        lane = lax.broadcasted_iota(jnp.int32, (Q, 128), 1)
        v = raw_ref[...] + b_ref[...]
        dt = jnp.maximum(v, 0.0) + jnp.log1p(jnp.exp(-jnp.abs(v)))
        a = jnp.where(lane[0:1, :] < 2 * NH, -jnp.exp(al_ref[...]), 0.0)
        da = dt * a
        tri, trit = _tri()
        dt_ref[...] = dt
        la_ref[...] = jnp.where(lane < NH, _dot3(tri, da), _dot3(trit, da))

    return pl.pallas_call(
        kern, out_shape=(S((T, 128), f32), S((T, 128), f32)), grid=(nch,),
        in_specs=[pl.BlockSpec((Q, 128), lambda c: (c, DT0 // 128)), _full((1, 128)), _full((1, 128))],
        out_specs=(pl.BlockSpec((Q, 128), lambda c: (c, 0)), pl.BlockSpec((Q, 128), lambda c: (c, 0))),
        compiler_params=_params("parallel"), name="dt_prep")(proj, bias_row, alog_row)


def _dt_bwd(a1, a2, r2, sv, dt, la, proj, bias_row, alog_row, dproj):
    T = proj.shape[0]
    nch = T // Q
    blk = pl.BlockSpec((Q, 128), lambda c: (c, 0))

    def kern(a1_ref, a2_ref, r2_ref, s_ref, dt_ref, la_ref, raw_ref, b_ref, al_ref, _alias, o_ref, db_ref, dal_ref):
        c = pl.program_id(0)

        @pl.when(c == 0)
        def _():
            db_ref[...] = jnp.zeros_like(db_ref)
            dal_ref[...] = jnp.zeros_like(dal_ref)

        lane = lax.broadcasted_iota(jnp.int32, (Q, 128), 1)
        row = lax.broadcasted_iota(jnp.int32, (Q, 128), 0)
        fwd = lane < NH
        dt = dt_ref[...]
        la = la_ref[...]
        a2v = a2_ref[...]
        r2v = r2_ref[...]
        a = jnp.where(lane[0:1, :] < 2 * NH, -jnp.exp(al_ref[...]), 0.0)
        la_e = jnp.where(fwd[0:1, :], la[Q - 1:Q, :], la[0:1, :])
        is_end = row == jnp.where(fwd, Q - 1, 0)
        e_end = jnp.exp(la_e - la)
        wend = e_end * dt
        extra = s_ref[0:1, :] * jnp.exp(la_e) + jnp.sum(wend * a2v, axis=0, keepdims=True)
        dla = a1_ref[...] - dt * r2v - wend * a2v + jnp.where(is_end, extra, 0.0)
        tri, trit = _tri()
        rcs = jnp.where(fwd, _dot3(trit, dla), _dot3(tri, dla))
        ddt = r2v + e_end * a2v + a * rcs
        dal_ref[...] += a * jnp.sum(dt * rcs, axis=0, keepdims=True)
        draw = jnp.where(lane < 2 * NH, ddt * _sig(raw_ref[...] + b_ref[...]), 0.0)
        db_ref[...] += jnp.sum(draw, axis=0, keepdims=True)
        o_ref[...] = jnp.zeros_like(o_ref)
        o_ref[:, 0:128] = draw.astype(bf16)

    return pl.pallas_call(
        kern, out_shape=(S(dproj.shape, bf16), S((1, 128), f32), S((1, 128), f32)), grid=(nch,),
        in_specs=[blk, blk, blk, blk, blk, blk, pl.BlockSpec((Q, 128), lambda c: (c, DT0 // 128)),
                  _full((1, 128)), _full((1, 128)), pl.BlockSpec(memory_space=pl.ANY)],
        out_specs=(pl.BlockSpec((Q, 1024), lambda c: (c, DT0 // 1024)), _full((1, 128)), _full((1, 128))),
        input_output_aliases={9: 0},
        compiler_params=_params("arbitrary"), name="dt_bwd")(a1, a2, r2, sv, dt, la, proj, bias_row, alog_row, dproj)


def _split2(v):
    hi = v.astype(bf16)
    lo = (v - hi.astype(f32)).astype(bf16)
    return jnp.concatenate([hi, lo], axis=1)


def _split3(v):
    hi = v.astype(bf16)
    r1 = v - hi.astype(f32)
    mid = r1.astype(bf16)
    lo = (r1 - mid.astype(f32)).astype(bf16)
    return jnp.concatenate([hi, mid, lo], axis=1)


def _scan_consts(rev):
    hoff = NH if rev else 0
    g = jnp.arange(NG, dtype=jnp.int32)[:, None, None]

    def rc(nr, ncol):
        return jnp.arange(nr, dtype=jnp.int32)[None, :, None], jnp.arange(ncol, dtype=jnp.int32)[None, None, :]

    r, c = rc(3 * 128, HPG * 128)
    sel_la = (lax.rem(r, 128) == hoff + HPG * g + c // 128).astype(bf16)
    r, c = rc(2 * 128, HPG * HD)
    sel_w = (lax.rem(r, 128) == hoff + HPG * g + c // HD).astype(bf16)
    r, c = rc(2 * HPG * HD, 128)
    ind_h = (c == hoff + HPG * g + lax.rem(r, HPG * HD) // HD).astype(bf16)
    r, c = rc(2 * HPG * Q, 128)
    ind_e = (c == hoff + HPG * g + lax.rem(r, HPG * Q) // Q).astype(bf16)
    return sel_la, sel_w, ind_h, ind_e


def _masks(rev):
    li = lax.broadcasted_iota(jnp.int32, (Q, Q), 0)
    si = lax.broadcasted_iota(jnp.int32, (Q, Q), 1)
    mask = (li <= si) if rev else (li >= si)
    mask_t = (li >= si) if rev else (li <= si)
    lane = lax.broadcasted_iota(jnp.int32, (Q, HPG * HD), 1)
    hms = [jnp.logical_and(lane >= r * HD, lane < (r + 1) * HD) for r in range(HPG)]
    return mask, mask_t, hms


def _mine(hoff):
    lane = lax.broadcasted_iota(jnp.int32, (Q, 128), 1)
    return jnp.logical_and(lane >= hoff, lane < hoff + NH)


def _head_row(vals, hc0):
    lane = lax.broadcasted_iota(jnp.int32, (1, HPG * HD), 1)
    out = jnp.zeros((1, HPG * HD), f32)
    for r in range(HPG):
        out = jnp.where(jnp.logical_and(lane >= r * HD, lane < (r + 1) * HD), vals[:, hc0 + r:hc0 + r + 1], out)
    return out


def _chunk_of(j, rev, nxc, nch):
    return (nch - 1 - j) if rev else lax.rem(j + nxc, nch)


def _ssd_fwd(xbc, dt, la, consts, rev, nxc, name):
    T = xbc.shape[0]
    nch = T // Q
    hoff = NH if rev else 0
    e = 0 if rev else Q - 1
    cm = lambda j: _chunk_of(j, rev, nxc, nch)
    sel_la, sel_w = consts[0], consts[1]

    def kern(xbc_ref, dt_ref, la_ref, sla_ref, sw_ref, y_ref, hp_ref, h_ref):
        j = pl.program_id(0)

        @pl.when(j == 0)
        def _():
            h_ref[...] = jnp.zeros_like(h_ref)

        hp_ref[...] = h_ref[...]
        mask, _, hms = _masks(rev)
        la_all = la_ref[...]
        dt_all = dt_ref[...]
        la_t = jnp.transpose(la_all)
        dt_t = jnp.transpose(dt_all)
        la_e = la_all[e:e + 1, :]
        la3 = _split3(la_all)
        w2 = _split2(jnp.exp(jnp.where(_mine(hoff), la_e - la_all, 0.0)) * dt_all)
        e2 = _split2(jnp.exp(la_all))
        ela_e = jnp.exp(la_e)
        for g in range(NG):
            hc0 = hoff + g * HPG
            x = xbc_ref[:, g * 256:(g + 1) * 256]
            bb = xbc_ref[:, B0 + g * NS:B0 + (g + 1) * NS].astype(bf16)
            cb = xbc_ref[:, C0 + g * NS:C0 + (g + 1) * NS].astype(bf16)
            ht = h_ref[g * NS:(g + 1) * NS, :]
            scores = _dot_nt(cb, bb)
            yoff = _dot(cb, ht.astype(bf16))
            la_rep4 = _dot(la3, sla_ref[g])
            wend = _dot(w2, sw_ref[g])
            expla = _dot(e2, sw_ref[g])
            mixes, xstack = [], []
            for r in range(HPG):
                hc = hc0 + r
                decay = jnp.exp(jnp.where(mask, la_rep4[:, r * 128:(r + 1) * 128] - la_t[hc:hc + 1, :], NEG))
                mixes.append((scores * decay * dt_t[hc:hc + 1, :]).astype(bf16))
                xstack.append(jnp.where(hms[r], x, 0.0).astype(bf16))
            y = _dot(jnp.concatenate(mixes, axis=1), jnp.concatenate(xstack, axis=0)) + yoff * expla
            y_ref[:, g * 256:(g + 1) * 256] = y
            h_ref[g * NS:(g + 1) * NS, :] = ht * _head_row(ela_e, hc0) + _dot_tn(bb, (x * wend).astype(bf16))

    row = lambda j: (cm(j), 0)
    return pl.pallas_call(
        kern, out_shape=(S((T, DI), f32), S((nch, NG * NS, HPG * HD), f32)), grid=(nch,),
        in_specs=[pl.BlockSpec((Q, 4096), row), pl.BlockSpec((Q, 128), row), pl.BlockSpec((Q, 128), row),
                  _full(sel_la.shape), _full(sel_w.shape)],
        out_specs=(pl.BlockSpec((Q, DI), row), pl.BlockSpec((None, NG * NS, HPG * HD), lambda j: (cm(j), 0, 0))),
        scratch_shapes=[pltpu.VMEM((NG * NS, HPG * HD), f32)],
        compiler_params=_params("arbitrary"), name=name)(xbc, dt, la, sel_la, sel_w)


def _ssd_bwd(xbc, dy, dt, la, hprev, dskip_full, consts, rev, nxc, name, acc=None):
    T = xbc.shape[0]
    nch = T // Q
    hoff = NH if rev else 0
    e = 0 if rev else Q - 1
    cm = lambda j: _chunk_of(nch - 1 - j, rev, nxc, nch)
    has_acc = acc is not None
    sel_la, sel_w, ind_h, ind_e = consts

    def kern(*refs):
        xbc_ref, dy_ref, dt_ref, la_ref, hp_ref, dsk_ref, sla_ref, sw_ref, ih_ref, ie_ref = refs[:10]
        k = 10
        if has_acc:
            dxbc_in, a1_in, a2_in, r2_in, s_in = refs[k:k + 5]
            k += 5
        dxbc_ref, a1_ref, a2_ref, r2_ref, s_ref, g_ref, r2scr = refs[k:k + 7]
        j = pl.program_id(0)

        @pl.when(j == 0)
        def _():
            g_ref[...] = jnp.zeros_like(g_ref)

        mask, mask_t, hms = _masks(rev)
        la_all = la_ref[...]
        dt_all = dt_ref[...]
        la_t = jnp.transpose(la_all)
        dt_t = jnp.transpose(dt_all)
        la_e = la_all[e:e + 1, :]
        la3 = _split3(la_all)
        w2 = _split2(jnp.exp(jnp.where(_mine(hoff), la_e - la_all, 0.0)) * dt_all)
        e2 = _split2(jnp.exp(la_all))
        d2 = _split2(dt_all)
        ela_e = jnp.exp(la_e)
        r2scr[...] = jnp.zeros_like(r2scr)
        a1acc = jnp.zeros((Q, 128), f32)
        a2acc = jnp.zeros((Q, 128), f32)
        sacc = jnp.zeros((1, 128), f32)
        for g in range(NG):
            hc0 = hoff + g * HPG
            x = xbc_ref[:, g * 256:(g + 1) * 256]
            bb = xbc_ref[:, B0 + g * NS:B0 + (g + 1) * NS].astype(bf16)
            cb = xbc_ref[:, C0 + g * NS:C0 + (g + 1) * NS].astype(bf16)
            dyv = dy_ref[:, g * 256:(g + 1) * 256]
            gt = g_ref[g * NS:(g + 1) * NS, :]
            ht = hp_ref[g * NS:(g + 1) * NS, :]
            gtb = gt.astype(bf16)
            htb = ht.astype(bf16)
            xb = x.astype(bf16)
            scores = _dot_nt(cb, bb)
            scores_t = _dot_nt(bb, cb)
            bg = _dot(bb, gtb)
            yoff = _dot(cb, htb)
            la_rep4 = _dot(la3, sla_ref[g])
            wend = _dot(w2, sw_ref[g])
            expla = _dot(e2, sw_ref[g])
            dtf = _dot(d2, sw_ref[g])
            dym = jnp.concatenate([jnp.where(hms[r], dyv, 0.0).astype(bf16) for r in range(HPG)], axis=0)
            dyx_all = _dot_nt(dym, xb)
            sdts, ehis, elos = [], [], []
            wsum = jnp.zeros((Q, Q), f32)
            for r in range(HPG):
                hc = hc0 + r
                la_rep = la_rep4[:, r * 128:(r + 1) * 128]
                la_r = la_t[hc:hc + 1, :]
                dt_r = dt_t[hc:hc + 1, :]
                decay = jnp.exp(jnp.where(mask, la_rep - la_r, NEG))
                decay_t = jnp.exp(jnp.where(mask_t, la_r - la_rep, NEG))
                dyx = dyx_all[r * Q:(r + 1) * Q, :]
                fm = dyx * (scores * decay)
                r2scr[hc:hc + 1, :] = jnp.sum(fm, axis=0, keepdims=True)
                em = fm * dt_r
                ehi = em.astype(bf16)
                ehis.append(ehi)
                elos.append((em - ehi.astype(f32)).astype(bf16))
                wsum = wsum + dyx * decay * dt_r
                sdts.append((scores_t * decay_t).astype(bf16))
            dx = dtf * _dot(jnp.concatenate(sdts, axis=1), dym) + wend * bg
            if not has_acc:
                dx = dx + dsk_ref[:, g * 256:(g + 1) * 256] * dyv
            a1acc = a1acc + _dot(jnp.concatenate(ehis + elos, axis=1), ie_ref[g]) \
                + _dot(_split2(dyv * yoff * expla), ih_ref[g])
            a2acc = a2acc + _dot(_split2(x * bg), ih_ref[g])
            sacc = sacc + jnp.sum(_dot(_split2(gt * ht), ih_ref[g]), axis=0, keepdims=True)
            wb = wsum.astype(bf16)
            dysb = (dyv * expla).astype(bf16)
            dc = _dot(wb, bb) + _dot_nt(dysb, htb)
            db = _dot_tn(wb, cb) + _dot_nt((x * wend).astype(bf16), gtb)
            g_ref[g * NS:(g + 1) * NS, :] = gt * _head_row(ela_e, hc0) + _dot_tn(cb, dysb)
            if has_acc:
                dx = dx + dxbc_in[:, g * 256:(g + 1) * 256]
                db = db + dxbc_in[:, B0 + g * NS:B0 + (g + 1) * NS]
                dc = dc + dxbc_in[:, C0 + g * NS:C0 + (g + 1) * NS]
            dxbc_ref[:, g * 256:(g + 1) * 256] = dx
            dxbc_ref[:, B0 + g * NS:B0 + (g + 1) * NS] = db
            dxbc_ref[:, C0 + g * NS:C0 + (g + 1) * NS] = dc
        r2c = jnp.transpose(r2scr[...])
        sc = jnp.broadcast_to(sacc, (Q, 128))
        if has_acc:
            a1acc = a1acc + a1_in[...]
            a2acc = a2acc + a2_in[...]
            r2c = r2c + r2_in[...]
            sc = sc + s_in[...]
        a1_ref[...] = a1acc
        a2_ref[...] = a2acc
        r2_ref[...] = r2c
        s_ref[...] = sc

    blk = pl.BlockSpec((Q, 128), lambda j: (cm(j), 0))
    big = pl.BlockSpec((Q, 4096), lambda j: (cm(j), 0))
    in_specs = [big, pl.BlockSpec((Q, DI), lambda j: (cm(j), 0)), blk, blk,
                pl.BlockSpec((None, NG * NS, HPG * HD), lambda j: (cm(j), 0, 0)), _full((1, DI)),
                _full(sel_la.shape), _full(sel_w.shape), _full(ind_h.shape), _full(ind_e.shape)]
    args = [xbc, dy, dt, la, hprev, dskip_full, sel_la, sel_w, ind_h, ind_e]
    aliases = {}
    if has_acc:
        in_specs += [big, blk, blk, blk, blk]
        args += list(acc)
        aliases = {10: 0, 11: 1, 12: 2, 13: 3, 14: 4}
    return pl.pallas_call(
        kern, out_shape=(S((T, 4096), f32), S((T, 128), f32), S((T, 128), f32), S((T, 128), f32), S((T, 128), f32)),
        grid=(nch,), in_specs=in_specs, out_specs=(big, blk, blk, blk, blk),
        scratch_shapes=[pltpu.VMEM((NG * NS, HPG * HD), f32), pltpu.VMEM((128, Q), f32)],
        input_output_aliases=aliases,
        compiler_params=_params("arbitrary"), name=name)(*args)


def _ynorm_fwd(yf, yb, xbc, proj, dskip_full, nw, L):
    nlx = L // RT

    def kern(yf_ref, yb_ref, xs_ref, z_ref, dsk_ref, nw_ref, y_ref, yn_ref, ynt_ref):
        y = yf_ref[...] + yb_ref[...] + dsk_ref[...] * xs_ref[...]
        y_ref[...] = y
        yz = y * _silu(z_ref[...])
        for g in range(NG):
            sl = yz[:, g * 256:(g + 1) * 256]
            r = lax.rsqrt(jnp.mean(sl * sl, axis=1, keepdims=True) + EPS)
            yn = (sl * r) * nw_ref[:, g * 256:(g + 1) * 256]
            yn_ref[:, g * 256:(g + 1) * 256] = yn.astype(bf16)
            ynt_ref[g * 256:(g + 1) * 256, :] = jnp.transpose(yn).astype(bf16)

    blk = pl.BlockSpec((RT, DI), lambda i: (i, 0))
    return pl.pallas_call(
        kern, out_shape=(S((L, DI), f32), S((L, DI), bf16), S((DI, L), bf16)), grid=(nlx,),
        in_specs=[blk, blk, blk, pl.BlockSpec((RT, DI), lambda i: (i, Z0 // DI)), _full((1, DI)), _full((1, DI))],
        out_specs=(blk, blk, pl.BlockSpec((DI, RT), lambda i: (0, i))),
        compiler_params=_params("parallel"), name="ynorm_fwd")(yf, yb, xbc, proj, dskip_full, nw)


def _ynorm_bwd(dyn, y, xbc, proj, dskip_full, nw, dproj):
    L = y.shape[0]
    T = proj.shape[0]
    nlx, nt = L // RT, T // RT

    def kern(dyn_ref, y_ref, xs_ref, z_ref, dsk_ref, nw_ref, _alias, dz_ref, dy_ref, dnw_ref, dsk_acc):
        i = pl.program_id(0)

        @pl.when(i == 0)
        def _():
            dnw_ref[...] = jnp.zeros_like(dnw_ref)
            dsk_acc[...] = jnp.zeros_like(dsk_acc)

        @pl.when(i >= nlx)
        def _():
            dz_ref[...] = jnp.zeros_like(dz_ref)
            dy_ref[...] = jnp.zeros_like(dy_ref)

        @pl.when(i < nlx)
        def _():
            y = y_ref[...]
            z = z_ref[...]
            sz = _sig(z)
            gz = z * sz
            yz = y * gz
            dynv = dyn_ref[...]
            for g in range(NG):
                cs = slice(g * 256, (g + 1) * 256)
                sl = yz[:, cs]
                r = lax.rsqrt(jnp.mean(sl * sl, axis=1, keepdims=True) + EPS)
                yhat = sl * r
                dn = dynv[:, cs]
                dnw_ref[:, cs] += jnp.sum(dn * yhat, axis=0, keepdims=True)
                dyh = dn * nw_ref[:, cs]
                dyz = r * (dyh - yhat * jnp.mean(dyh * yhat, axis=1, keepdims=True))
                dyv = dyz * gz[:, cs]
                dy_ref[:, cs] = dyv
                dz_ref[:, cs] = (dyz * y[:, cs] * _dsilu(z[:, cs], sz[:, cs])).astype(bf16)
                dsk_acc[:, cs] += jnp.sum(dyv * xs_ref[:, cs], axis=0, keepdims=True)

    xmap = lambda i: (jnp.minimum(i, nlx - 1), 0)
    return pl.pallas_call(
        kern, out_shape=(S(dproj.shape, bf16), S((T, DI), f32), S((1, DI), f32), S((1, DI), f32)), grid=(nt,),
        in_specs=[pl.BlockSpec((RT, DI), xmap), pl.BlockSpec((RT, DI), xmap), pl.BlockSpec((RT, DI), xmap),
                  pl.BlockSpec((RT, DI), lambda i: (jnp.minimum(i, nlx - 1), Z0 // DI)), _full((1, DI)), _full((1, DI)),
                  pl.BlockSpec(memory_space=pl.ANY)],
        out_specs=(pl.BlockSpec((RT, DI), lambda i: (i, Z0 // DI)), pl.BlockSpec((RT, DI), lambda i: (i, 0)),
                   _full((1, DI)), _full((1, DI))),
        input_output_aliases={6: 0},
        compiler_params=_params("arbitrary"), name="ynorm_bwd")(dyn, y, xbc, proj, dskip_full, nw, dproj)


def _head_sums(cols):
    def kern(c_ref, o_ref):
        o_ref[...] = jnp.broadcast_to(jnp.sum(c_ref[...], axis=1, keepdims=True), (NH, 128))

    return pl.pallas_call(kern, out_shape=S((NH, 128), f32), name="head_sums")(cols)


SEG_STRIDE = 96
SEG_PAD = 16
NSEG = RT // GRID_W
CONF_ROWS = SEG_PAD + NSEG * SEG_STRIDE


SHIFT_ROWS = CONF_ROWS - 8
CONF_CW = 256


CONF_RB = 32


def _seg_zero_pads(scr):
    scr[0:SEG_PAD, :] = jnp.zeros((SEG_PAD, scr.shape[1]), f32)
    for s in range(NSEG):
        lo = SEG_PAD + s * SEG_STRIDE + GRID_W
        scr[lo:lo + SEG_STRIDE - GRID_W, :] = jnp.zeros((SEG_STRIDE - GRID_W, scr.shape[1]), f32)


def _seg_row(r0):
    return SEG_PAD + (r0 // GRID_W) * SEG_STRIDE + r0 % GRID_W


def _shift_copies(cps, scr, cs):
    for s in range(1, 8):
        cps[s - 1, :, :] = scr[pl.ds(s, SHIFT_ROWS), cs]


def _tap(cps, scr, cs, o):
    rs = o % 8
    return scr[pl.ds(o, GRID_W), cs] if rs == 0 else cps[rs - 1, pl.ds(o - rs, GRID_W), :]


def _conf_fwd(proj, w32, cb, lnw, lnb, L):
    nlx = L // RT

    def kern(v_ref, g_ref, cg_ref, w_ref, cb_ref, lnw_ref, lnb_ref, u1_ref, u3_ref, u3t_ref, scr, cps, u3_scr):
        _seg_zero_pads(scr)
        for r0 in range(0, RT, CONF_RB):
            rows = slice(r0, r0 + CONF_RB)
            scr[_seg_row(r0):_seg_row(r0) + CONF_RB, :] = v_ref[rows, :] * _sig(g_ref[rows, :])
        for cc in range(D // CONF_CW):
            cs = slice(cc * CONF_CW, (cc + 1) * CONF_CW)
            _shift_copies(cps, scr, cs)
            for s in range(NSEG):
                acc = jnp.broadcast_to(cb_ref[:, cs], (GRID_W, CONF_CW))
                for k in range(CK):
                    acc = acc + w_ref[k:k + 1, cs] * _tap(cps, scr, cs, SEG_PAD + s * SEG_STRIDE + k - CK // 2)
                u1_ref[s * GRID_W:(s + 1) * GRID_W, cs] = acc
        for r0 in range(0, RT, CONF_RB):
            rows = slice(r0, r0 + CONF_RB)
            u1 = u1_ref[rows, :]
            xc = u1 - jnp.mean(u1, axis=1, keepdims=True)
            r = lax.rsqrt(jnp.mean(xc * xc, axis=1, keepdims=True) + EPS)
            u2 = (xc * r) * lnw_ref[...] + lnb_ref[...]
            u3 = _silu(u2) * _silu(cg_ref[rows, :])
            u3_ref[rows, :] = u3.astype(bf16)
            u3_scr[rows, :] = u3
        u3t_ref[...] = jnp.transpose(u3_scr[...]).astype(bf16)

    blk = pl.BlockSpec((RT, D), lambda i: (i, 0))
    return pl.pallas_call(
        kern, out_shape=(S((L, D), f32), S((L, D), bf16), S((D, L), bf16)), grid=(nlx,),
        in_specs=[pl.BlockSpec((RT, D), lambda i: (i, GV0 // D)), pl.BlockSpec((RT, D), lambda i: (i, GG0 // D)),
                  pl.BlockSpec((RT, D), lambda i: (i, CG0 // D)), _full((32, D)), _full((1, D)), _full((1, D)), _full((1, D))],
        out_specs=(blk, blk, pl.BlockSpec((D, RT), lambda i: (0, i))),
        scratch_shapes=[pltpu.VMEM((CONF_ROWS, D), f32), pltpu.VMEM((7, SHIFT_ROWS, CONF_CW), f32), pltpu.VMEM((RT, D), f32)],
        compiler_params=_params("parallel"), name="conf_fwd")(proj, proj, proj, w32, cb, lnw, lnb)


def _conf_bwd(du3, u1, proj, w32, lnw, lnb, dproj):
    L = u1.shape[0]
    T = proj.shape[0]
    nlx, nt = L // RT, T // RT

    def kern(du3_ref, u1_ref, v_ref, g_ref, cg_ref, w_ref, lnw_ref, lnb_ref, _alias,
             o_ref, dw_ref, dcb_ref, dlw_ref, dlb_ref, scr_u, scr_d, du0_scr, cps_u, cps_d):
        i = pl.program_id(0)

        @pl.when(i == 0)
        def _():
            dw_ref[...] = jnp.zeros_like(dw_ref)
            dcb_ref[...] = jnp.zeros_like(dcb_ref)
            dlw_ref[...] = jnp.zeros_like(dlw_ref)
            dlb_ref[...] = jnp.zeros_like(dlb_ref)

        @pl.when(i >= nlx)
        def _():
            o_ref[...] = jnp.zeros_like(o_ref)

        @pl.when(i < nlx)
        def _():
            _seg_zero_pads(scr_u)
            _seg_zero_pads(scr_d)
            for r0 in range(0, RT, CONF_RB):
                rows = slice(r0, r0 + CONF_RB)
                cg = cg_ref[rows, :]
                scg = _sig(cg)
                u1 = u1_ref[rows, :]
                xc = u1 - jnp.mean(u1, axis=1, keepdims=True)
                r = lax.rsqrt(jnp.mean(xc * xc, axis=1, keepdims=True) + EPS)
                xhat = xc * r
                u2 = xhat * lnw_ref[...] + lnb_ref[...]
                s2 = _sig(u2)
                du3v = du3_ref[rows, :]
                du2 = du3v * (cg * scg) * _dsilu(u2, s2)
                o_ref[rows, 2 * D:3 * D] = (du3v * (u2 * s2) * _dsilu(cg, scg)).astype(bf16)
                dlw_ref[...] += jnp.sum(du2 * xhat, axis=0, keepdims=True)
                dlb_ref[...] += jnp.sum(du2, axis=0, keepdims=True)
                dxh = du2 * lnw_ref[...]
                du1 = r * (dxh - jnp.mean(dxh, axis=1, keepdims=True) - xhat * jnp.mean(dxh * xhat, axis=1, keepdims=True))
                dcb_ref[...] += jnp.sum(du1, axis=0, keepdims=True)
                scr_u[_seg_row(r0):_seg_row(r0) + CONF_RB, :] = v_ref[rows, :] * _sig(g_ref[rows, :])
                scr_d[_seg_row(r0):_seg_row(r0) + CONF_RB, :] = du1
            for cc in range(D // CONF_CW):
                cs = slice(cc * CONF_CW, (cc + 1) * CONF_CW)
                _shift_copies(cps_u, scr_u, cs)
                _shift_copies(cps_d, scr_d, cs)
                for k in range(CK):
                    t = jnp.zeros((GRID_W, CONF_CW), f32)
                    for s in range(NSEG):
                        base = SEG_PAD + s * SEG_STRIDE
                        t = t + scr_d[pl.ds(base, GRID_W), cs] * _tap(cps_u, scr_u, cs, base + k - CK // 2)
                    dw_ref[k:k + 1, cs] += jnp.sum(t, axis=0, keepdims=True)
                for s in range(NSEG):
                    base = SEG_PAD + s * SEG_STRIDE
                    acc = jnp.zeros((GRID_W, CONF_CW), f32)
                    for k in range(CK):
                        acc = acc + w_ref[k:k + 1, cs] * _tap(cps_d, scr_d, cs, base + CK // 2 - k)
                    du0_scr[s * GRID_W:(s + 1) * GRID_W, cs] = acc
            for r0 in range(0, RT, CONF_RB):
                rows = slice(r0, r0 + CONF_RB)
                du0 = du0_scr[rows, :]
                sg = _sig(g_ref[rows, :])
                o_ref[rows, 0:D] = (du0 * sg).astype(bf16)
                o_ref[rows, D:2 * D] = (du0 * v_ref[rows, :] * sg * (1.0 - sg)).astype(bf16)

    xmap = lambda i: (jnp.minimum(i, nlx - 1), 0)
    pmap = lambda cb: (lambda i: (jnp.minimum(i, nlx - 1), cb))
    return pl.pallas_call(
        kern, out_shape=(S(dproj.shape, bf16), S((32, D), f32), S((1, D), f32), S((1, D), f32), S((1, D), f32)), grid=(nt,),
        in_specs=[pl.BlockSpec((RT, D), xmap), pl.BlockSpec((RT, D), xmap),
                  pl.BlockSpec((RT, D), pmap(GV0 // D)), pl.BlockSpec((RT, D), pmap(GG0 // D)), pl.BlockSpec((RT, D), pmap(CG0 // D)),
                  _full((32, D)), _full((1, D)), _full((1, D)), pl.BlockSpec(memory_space=pl.ANY)],
        out_specs=(pl.BlockSpec((RT, 3 * D), lambda i: (i, GV0 // (3 * D))), _full((32, D)), _full((1, D)), _full((1, D)), _full((1, D))),
        scratch_shapes=[pltpu.VMEM((CONF_ROWS, D), f32), pltpu.VMEM((CONF_ROWS, D), f32), pltpu.VMEM((RT, D), f32),
                        pltpu.VMEM((7, SHIFT_ROWS, CONF_CW), f32), pltpu.VMEM((7, SHIFT_ROWS, CONF_CW), f32)],
        input_output_aliases={8: 0},
        compiler_params=_params("arbitrary"), name="conf_bwd")(du3, u1, proj, proj, proj, w32, lnw, lnb, dproj)


def _merge_fwd(bs, bc, proj):
    L = bs.shape[0]

    def kern(bs_ref, bc_ref, g1_ref, g2_ref, o_ref, ot_ref):
        mv = _sig(g1_ref[...]) * bs_ref[...] + _sig(g2_ref[...]) * bc_ref[...]
        o_ref[...] = mv.astype(bf16)
        ot_ref[...] = jnp.transpose(mv).astype(bf16)

    blk = pl.BlockSpec((RT, D), lambda i: (i, 0))
    return pl.pallas_call(
        kern, out_shape=(S((L, D), bf16), S((D, L), bf16)), grid=(L // RT,),
        in_specs=[blk, blk, pl.BlockSpec((RT, D), lambda i: (i, G10 // D)), pl.BlockSpec((RT, D), lambda i: (i, G20 // D))],
        out_specs=(blk, pl.BlockSpec((D, RT), lambda i: (0, i))),
        compiler_params=_params("parallel"), name="merge_fwd")(bs, bc, proj, proj)


def _merge_bwd(dmerged, bs, bc, proj):
    L = bs.shape[0]
    T = proj.shape[0]
    nlx, nt = L // RT, T // RT

    def kern(dm_ref, bs_ref, bc_ref, g1_ref, g2_ref, o_ref, dbs_ref, dbc_ref):
        i = pl.program_id(0)

        @pl.when(i >= nlx)
        def _():
            o_ref[...] = jnp.zeros_like(o_ref)

        @pl.when(i < nlx)
        def _():
            dm = dm_ref[...]
            s1 = _sig(g1_ref[...])
            s2 = _sig(g2_ref[...])
            dbs_ref[...] = (dm * s1).astype(bf16)
            dbc_ref[...] = (dm * s2).astype(bf16)
            o_ref[:, 0:D] = (dm * bs_ref[...] * s1 * (1.0 - s1)).astype(bf16)
            o_ref[:, D:2 * D] = (dm * bc_ref[...] * s2 * (1.0 - s2)).astype(bf16)

    xmap = lambda i: (jnp.minimum(i, nlx - 1), 0)
    pmap = lambda cb: (lambda i: (jnp.minimum(i, nlx - 1), cb))
    xblk = pl.BlockSpec((RT, D), xmap)
    return pl.pallas_call(
        kern, out_shape=(S((T, NP), bf16), S((L, D), bf16), S((L, D), bf16)), grid=(nt,),
        in_specs=[xblk, xblk, xblk, pl.BlockSpec((RT, D), pmap(G10 // D)), pl.BlockSpec((RT, D), pmap(G20 // D))],
        out_specs=(pl.BlockSpec((RT, 2 * D), lambda i: (i, G10 // (2 * D))), xblk, xblk),
        compiler_params=_params("arbitrary"), name="merge_bwd")(dmerged, bs, bc, proj, proj)


def _final(x, out, target, mod, fw):
    L = x.shape[0]

    def kern(x_ref, o_ref, t_ref, mod_ref, fw_ref, dx1_ref, dout_ref, loss_ref, dfw_ref, dg_ref):
        i = pl.program_id(0)

        @pl.when(i == 0)
        def _():
            loss_ref[...] = jnp.zeros_like(loss_ref)
            dfw_ref[...] = jnp.zeros_like(dfw_ref)
            dg_ref[...] = jnp.zeros_like(dg_ref)

        gate = mod_ref[0:1, 2 * D:3 * D]
        ov = o_ref[...]
        x1 = x_ref[...] + gate * ov
        r = lax.rsqrt(jnp.mean(x1 * x1, axis=1, keepdims=True) + EPS)
        xn = x1 * r
        fw = fw_ref[...]
        err = xn * fw - t_ref[...]
        part = 0.5 * jnp.sum(jnp.mean(err * err, axis=1, keepdims=True), axis=0, keepdims=True)
        loss_ref[...] += jnp.broadcast_to(part, (8, 128))
        dy = err * (1.0 / D)
        dfw_ref[...] += jnp.sum(dy * xn, axis=0, keepdims=True)
        dyw = dy * fw
        dx1 = r * (dyw - xn * jnp.mean(dyw * xn, axis=1, keepdims=True))
        dx1_ref[...] = dx1
        dout_ref[...] = (gate * dx1).astype(bf16)
        dg_ref[...] += jnp.sum(dx1 * ov, axis=0, keepdims=True)

    blk = pl.BlockSpec((RT, D), lambda i: (i, 0))
    return pl.pallas_call(
        kern, out_shape=(S((L, D), f32), S((L, D), bf16), S((8, 128), f32), S((1, D), f32), S((1, D), f32)), grid=(L // RT,),
        in_specs=[blk, blk, blk, _full((8, 3 * D)), _full((1, D))],
        out_specs=(blk, blk, _full((8, 128)), _full((1, D)), _full((1, D))),
        compiler_params=_params("arbitrary"), name="final")(x, out, target, mod, fw)


def _me():
    return 4 * lax.axis_index("x") + 2 * lax.axis_index("y") + lax.axis_index("c")


def _xchg_copy(ins, outs, send_sems, recv_sems, modes, a, k, me):
    peer = lax.rem(me + k, N_DEV)
    pid = (peer // 4, lax.rem(peer // 2, 2), lax.rem(peer, 2))
    src = ins[a].at[peer] if modes[a] else ins[a]
    return pltpu.make_async_remote_copy(src_ref=src, dst_ref=outs[a].at[me], send_sem=send_sems.at[a, k - 1],
                                        recv_sem=recv_sems.at[a, k - 1], device_id=pid, device_id_type=MESH)


def _xchg_local(ins, outs, loc_sems, modes, a, me):
    return pltpu.make_async_copy(ins[a].at[me] if modes[a] else ins[a], outs[a].at[me], loc_sems.at[a])


def _xchg_start(ins, outs, send_sems, recv_sems, loc_sems, modes):
    me = _me()
    for a in range(len(modes)):
        _xchg_local(ins, outs, loc_sems, modes, a, me).start()
        for k in range(1, N_DEV):
            _xchg_copy(ins, outs, send_sems, recv_sems, modes, a, k, me).start()


def _xchg_wait(ins, outs, send_sems, recv_sems, loc_sems, modes):
    me = _me()
    for a in range(len(modes)):
        for k in range(1, N_DEV):
            frm = lax.rem(me + N_DEV - k, N_DEV)
            src = ins[a].at[frm] if modes[a] else ins[a]
            pltpu.make_async_remote_copy(src_ref=src, dst_ref=outs[a].at[frm], send_sem=send_sems.at[a, k - 1],
                                         recv_sem=recv_sems.at[a, k - 1], device_id=(0, 0, 0), device_id_type=MESH).wait_recv()
    for a in range(len(modes)):
        for k in range(1, N_DEV):
            _xchg_copy(ins, outs, send_sems, recv_sems, modes, a, k, me).wait_send()
        _xchg_local(ins, outs, loc_sems, modes, a, me).wait()


def _xchg_out_shapes(arrs, modes):
    return tuple(S((N_DEV,) + (a.shape[1:] if sc else a.shape), a.dtype) for a, sc in zip(arrs, modes))


def _xchg_sems(n):
    return [pltpu.SemaphoreType.DMA((n, N_DEV - 1)), pltpu.SemaphoreType.DMA((n, N_DEV - 1)), pltpu.SemaphoreType.DMA((n,))]


def _exchange(arrs, modes, name):
    n = len(arrs)

    def kern(*refs):
        ins, outs, sems = refs[:n], refs[n:2 * n], refs[2 * n:]
        _xchg_start(ins, outs, *sems, modes)
        _xchg_wait(ins, outs, *sems, modes)

    anyspec = pl.BlockSpec(memory_space=pl.ANY)
    return pl.pallas_call(
        kern, out_shape=_xchg_out_shapes(arrs, modes), in_specs=[anyspec] * n, out_specs=tuple([anyspec] * n),
        scratch_shapes=_xchg_sems(n), name=name)(*arrs)


def _gather2(arrs, name):
    n = len(arrs)

    def kern(*refs):
        ins, outs = refs[:n], refs[n:2 * n]
        send_sems, recv_sems, loc_sems = refs[2 * n:]
        x, y, c = lax.axis_index("x"), lax.axis_index("y"), lax.axis_index("c")
        me, sib = (x, y, c), (x, y, 1 - c)
        chips = [(1 - x, y), (x, 1 - y), (1 - x, 1 - y)]

        def slot(a, p):
            return outs[a].at[4 * p[0] + 2 * p[1] + p[2]]

        def cp(a, k, block, to, own=False):
            return pltpu.make_async_remote_copy(src_ref=ins[a] if own else slot(a, block), dst_ref=slot(a, block),
                                                send_sem=send_sems.at[a, k], recv_sem=recv_sems.at[a, k],
                                                device_id=to, device_id_type=MESH)

        started = []
        for a in range(n):
            loc = pltpu.make_async_copy(ins[a], slot(a, me), loc_sems.at[a])
            loc.start()
            started.append(cp(a, 0, me, sib, own=True))
            started += [cp(a, 1 + j, me, (*chip, c), own=True) for j, chip in enumerate(chips)]
        for s in started:
            s.start()
        for j, chip in enumerate(chips):
            for a in range(n):
                cp(a, 1 + j, (*chip, c), me).wait_recv()
                fwd = cp(a, 4 + j, (*chip, c), sib)
                fwd.start()
                started.append(fwd)
        for a in range(n):
            cp(a, 0, sib, me).wait_recv()
            for j, chip in enumerate(chips):
                cp(a, 4 + j, (*chip, 1 - c), me).wait_recv()
        for s in started:
            s.wait_send()
        for a in range(n):
            pltpu.make_async_copy(ins[a], slot(a, me), loc_sems.at[a]).wait()

    anyspec = pl.BlockSpec(memory_space=pl.ANY)
    return pl.pallas_call(
        kern, out_shape=_xchg_out_shapes(arrs, (False,) * n), in_specs=[anyspec] * n, out_specs=tuple([anyspec] * n),
        scratch_shapes=[pltpu.SemaphoreType.DMA((n, 7)), pltpu.SemaphoreType.DMA((n, 7)), pltpu.SemaphoreType.DMA((n,))],
        name=name)(*arrs)


def _adamw(parts, w, m, v, name):
    r, c = w.shape
    tr = r
    for cand in (128, 64, 32, 16, 8):
        if r % cand == 0 and r > cand:
            tr = cand
            break
    c1 = 1.0 / (1.0 - ADAM_B1 ** ADAM_STEP)
    c2 = 1.0 / (1.0 - ADAM_B2 ** ADAM_STEP)

    def kern(p_ref, w_ref, m_ref, v_ref, g_ref, d_ref, m2_ref, v2_ref):
        g = p_ref[0].astype(f32)
        for i in range(1, N_DEV):
            g = g + p_ref[i].astype(f32)
        g_ref[...] = g
        m2 = ADAM_B1 * m_ref[...] + (1.0 - ADAM_B1) * g
        v2 = ADAM_B2 * v_ref[...] + (1.0 - ADAM_B2) * (g * g)
        m2_ref[...] = m2
        v2_ref[...] = v2
        d_ref[...] = -ADAM_LR * ((m2 * c1) / (jnp.sqrt(v2 * c2) + ADAM_EPS) + ADAM_WD * w_ref[...])

    blk = pl.BlockSpec((tr, c), lambda i: (i, 0))
    sh = S((r, c), f32)
    return pl.pallas_call(
        kern, out_shape=(sh, sh, sh, sh), grid=(r // tr,),
        in_specs=[pl.BlockSpec((N_DEV, tr, c), lambda i: (0, i, 0)), blk, blk, blk], out_specs=(blk, blk, blk, blk),
        compiler_params=_params("parallel"), name=name)(parts, w, m, v)


_SMALL = (("c_ctx", 1024), ("b_mod", 3072), ("norm_w", 1024), ("ssm_conv_b", 4096), ("dt_bias", 64), ("a_log", 64),
          ("d_skip", 32), ("ssm_norm_w", 2048), ("conf_conv_b", 1024), ("conf_ln_w", 1024), ("conf_ln_b", 1024),
          ("final_norm_w", 1024))
SMALL_TILE = 8 * 128


def _pack_small(d):
    rows = []
    for name, n in _SMALL:
        v = d[name].reshape(-1).astype(f32)
        pad = (-n) % SMALL_TILE
        if pad:
            v = jnp.concatenate([v, jnp.zeros((pad,), f32)])
        rows.append(v.reshape(-1, 128))
    return jnp.concatenate(rows, axis=0)


def _unpack_small(p, shapes):
    out, r0 = {}, 0
    for name, n in _SMALL:
        nr = 8 * ((n + SMALL_TILE - 1) // SMALL_TILE)
        out[name] = p[r0:r0 + nr].reshape(-1)[:n].reshape(shapes[name])
        r0 += nr
    return out


def _permute_w_in(w):
    return jnp.concatenate([w[:, 0:4096], w[:, 4160:6208], w[:, 9280:11328], w[:, 4096:4160],
                            jnp.zeros((w.shape[0], 1024 - 64), w.dtype), w[:, 6208:9280]], axis=1)


def _unpermute_w_in(wp):
    return jnp.concatenate([wp[:, 0:4096], wp[:, DT0:DT0 + 64], wp[:, Z0:Z0 + 2048], wp[:, GV0:GV0 + 3072],
                            wp[:, G10:G10 + 2048]], axis=1)


def _cols_gathered(g):
    return jnp.transpose(g, (1, 0, 2)).reshape(g.shape[1], N_DEV * g.shape[2])


def _cols_to_blocks(a):
    r, c8 = a.shape
    return jnp.transpose(a.reshape(r, N_DEV, c8 // N_DEV), (1, 0, 2))


def kernel(x, c, ctx, c_ctx, w_mod, b_mod, norm_w, w_in, ssm_conv_w, ssm_conv_b, dt_bias, a_log, d_skip, ssm_norm_w, w_out_ssm, conf_conv_w, conf_conv_b, conf_ln_w, conf_ln_b, w_out_conf, w_out, final_norm_w, loss_target, m_c_ctx, m_w_mod, m_b_mod, m_norm_w, m_w_in, m_ssm_conv_w, m_ssm_conv_b, m_dt_bias, m_a_log, m_d_skip, m_ssm_norm_w, m_w_out_ssm, m_conf_conv_w, m_conf_conv_b, m_conf_ln_w, m_conf_ln_b, m_w_out_conf, m_w_out, m_final_norm_w, v_c_ctx, v_w_mod, v_b_mod, v_norm_w, v_w_in, v_ssm_conv_w, v_ssm_conv_b, v_dt_bias, v_a_log, v_d_skip, v_ssm_norm_w, v_w_out_ssm, v_conf_conv_w, v_conf_conv_b, v_conf_ln_w, v_conf_ln_b, v_w_out_conf, v_w_out, v_final_norm_w):
    L = x.shape[1]
    Lc = ctx.shape[1]
    T = L + Lc
    nlx = L // RT
    nxc = L // Q
    x2 = x.reshape(L, D)
    ctx2 = ctx.reshape(Lc, D)
    tgt = loss_target.reshape(L, D)

    gathered = _gather2([w_in[0].astype(bf16), w_mod[0].astype(bf16), ssm_conv_w[0], conf_conv_w[0]], name="gather_weights")
    wp = _permute_w_in(_cols_gathered(gathered[0]))
    wmod_bf = _cols_gathered(gathered[1])
    scw8 = jnp.concatenate([_cols_gathered(gathered[2]), jnp.zeros((8 - SK, 4096), f32)], axis=0)
    ccw32 = jnp.concatenate([_cols_gathered(gathered[3]), jnp.zeros((32 - CK, D), f32)], axis=0)

    norm_w1 = norm_w.reshape(1, D)
    scb = ssm_conv_b.reshape(1, 4096)
    bias_row = jnp.concatenate([dt_bias.reshape(1, 2 * NH), jnp.zeros((1, 128 - 2 * NH), f32)], axis=1)
    alog_row = jnp.concatenate([a_log.reshape(1, 2 * NH), jnp.zeros((1, 128 - 2 * NH), f32)], axis=1)
    dskip_full = jnp.repeat(d_skip.reshape(NH), HD).reshape(1, DI)
    snw = ssm_norm_w.reshape(1, DI)
    ccb = conf_conv_b.reshape(1, D)
    lnw = conf_ln_w.reshape(1, D)
    lnb = conf_ln_b.reshape(1, D)
    fw = final_norm_w.reshape(1, D)

    cc8 = jnp.concatenate([c.reshape(1, D), c_ctx.reshape(1, D), jnp.zeros((6, D), f32)], axis=0)
    mod = _mod_fwd(cc8, wmod_bf, b_mod.reshape(1, 3 * D))
    h, h_t = _prenorm(x2, ctx2, norm_w1, mod)
    proj, wos_g, woc_g, wo_g = _matmul(
        h, wp, f32, "proj_gather", tn=2048,
        comm=([w_out_ssm[0].astype(bf16), w_out_conf[0].astype(bf16), w_out[0].astype(bf16)], (False,) * 3))
    wos_bf = wos_g.reshape(DI, D)
    woc_bf = woc_g.reshape(D, D)
    wo_bf = wo_g.reshape(D, D)
    xbc = _ssm_conv_fwd(proj, scw8, scb, nlx)
    dt, la = _dt_prep(proj, bias_row, alog_row)
    consts_f, consts_b = _scan_consts(False), _scan_consts(True)
    yf, hp_f = _ssd_fwd(xbc, dt, la, consts_f, False, nxc, "ssd_fwd_f")
    yb, hp_b = _ssd_fwd(xbc, dt, la, consts_b, True, nxc, "ssd_fwd_b")
    y, yn, yn_t = _ynorm_fwd(yf, yb, xbc, proj, dskip_full, snw, L)
    bs = _matmul(yn, wos_bf, f32, "branch_ssm")
    u1, u3, u3_t = _conf_fwd(proj, ccw32, ccb, lnw, lnb, L)
    bc = _matmul(u3, woc_bf, f32, "branch_conf")
    merged, merged_t = _merge_fwd(bs, bc, proj)
    out = _matmul(merged, wo_bf, f32, "out_proj")
    dx1, dout, loss_acc, dfw, dgate = _final(x2, out, tgt, mod, fw)

    dmerged = _matmul(dout, wo_bf, f32, "d_merged", tb=True)
    g_wo = _matmul(merged_t, dout, bf16, "g_w_out")
    dproj, dbs, dbc = _merge_bwd(dmerged, bs, bc, proj)
    dyn = _matmul(dbs, wos_bf, f32, "d_yn", tb=True)
    g_wos = _matmul(yn_t, dbs, bf16, "g_w_out_ssm")
    du3 = _matmul(dbc, woc_bf, f32, "d_u3", tb=True)
    g_woc = _matmul(u3_t, dbc, bf16, "g_w_out_conf")
    dproj, g_ccw, g_ccb, g_lnw, g_lnb = _conf_bwd(du3, u1, proj, ccw32, lnw, lnb, dproj)
    dproj, dy, g_snw, dsk_cols = _ynorm_bwd(dyn, y, xbc, proj, dskip_full, snw, dproj)
    acc_f = _ssd_bwd(xbc, dy, dt, la, hp_f, dskip_full, consts_f, False, nxc, "ssd_bwd_f")
    dxbc, a1, a2, r2, sv = _ssd_bwd(xbc, dy, dt, la, hp_b, dskip_full, consts_b, True, nxc, "ssd_bwd_b", acc=acc_f)
    dproj, g_dtb, g_alog = _dt_bwd(a1, a2, r2, sv, dt, la, proj, bias_row, alog_row, dproj)
    dpre, g_scw, g_scb = _ssm_conv_dpre(dxbc, proj, scw8, scb, nlx)
    dproj = _ssm_conv_t(dpre, scw8, dproj, nlx)
    g_wp = _matmul(h_t, dproj, bf16, "g_w_in", tm=1024, tn=2048)
    dh, *parts = _matmul(
        dproj, wp, f32, "d_h_scatter", tb=True, tk=2048,
        comm=([_cols_to_blocks(_unpermute_w_in(g_wp)), g_wos.reshape(N_DEV, DI // N_DEV, D), g_woc.reshape(N_DEV, D // N_DEV, D),
               g_wo.reshape(N_DEV, D // N_DEV, D), _cols_to_blocks(g_scw[:SK]), _cols_to_blocks(g_ccw[:CK])], (True,) * 6))
    gx, g_nw, macc = _prenorm_bwd(x2, ctx2, dh, dx1, norm_w1, mod)
    z1 = jnp.zeros((1, D), f32)
    dmod8 = jnp.concatenate([jnp.concatenate([macc[0:1], macc[1:2], dgate], axis=1),
                             jnp.concatenate([macc[2:3], macc[3:4], z1], axis=1), jnp.zeros((6, 3 * D), f32)], axis=0)
    ct = jnp.concatenate([c.reshape(D, 1), c_ctx.reshape(D, 1), jnp.zeros((D, 126), f32)], axis=1)
    g_wmod, g_bmod, g_cctx = _mod_bwd(ct, dmod8, wmod_bf)
    g_dskip = _head_sums(dsk_cols.reshape(NH, HD))[:, 0]

    small_g = _pack_small({
        "c_ctx": g_cctx[:, 0], "b_mod": g_bmod, "norm_w": g_nw, "ssm_conv_b": g_scb, "dt_bias": g_dtb[0, :2 * NH],
        "a_log": g_alog[0, :2 * NH], "d_skip": g_dskip, "ssm_norm_w": g_snw, "conf_conv_b": g_ccb, "conf_ln_w": g_lnw,
        "conf_ln_b": g_lnb, "final_norm_w": dfw})
    wmod_parts, small_parts = _exchange([_cols_to_blocks(g_wmod), small_g], (True, False), name="exchange_tail")
    parts = [parts[0], wmod_parts] + parts[1:]

    given = dict(c_ctx=c_ctx, w_mod=w_mod, b_mod=b_mod, norm_w=norm_w, w_in=w_in, ssm_conv_w=ssm_conv_w, ssm_conv_b=ssm_conv_b,
                 dt_bias=dt_bias, a_log=a_log, d_skip=d_skip, ssm_norm_w=ssm_norm_w, w_out_ssm=w_out_ssm, conf_conv_w=conf_conv_w,
                 conf_conv_b=conf_conv_b, conf_ln_w=conf_ln_w, conf_ln_b=conf_ln_b, w_out_conf=w_out_conf, w_out=w_out,
                 final_norm_w=final_norm_w)
    ms = dict(c_ctx=m_c_ctx, w_mod=m_w_mod, b_mod=m_b_mod, norm_w=m_norm_w, w_in=m_w_in, ssm_conv_w=m_ssm_conv_w,
              ssm_conv_b=m_ssm_conv_b, dt_bias=m_dt_bias, a_log=m_a_log, d_skip=m_d_skip, ssm_norm_w=m_ssm_norm_w,
              w_out_ssm=m_w_out_ssm, conf_conv_w=m_conf_conv_w, conf_conv_b=m_conf_conv_b, conf_ln_w=m_conf_ln_w,
              conf_ln_b=m_conf_ln_b, w_out_conf=m_w_out_conf, w_out=m_w_out, final_norm_w=m_final_norm_w)
    vs = dict(c_ctx=v_c_ctx, w_mod=v_w_mod, b_mod=v_b_mod, norm_w=v_norm_w, w_in=v_w_in, ssm_conv_w=v_ssm_conv_w,
              ssm_conv_b=v_ssm_conv_b, dt_bias=v_dt_bias, a_log=v_a_log, d_skip=v_d_skip, ssm_norm_w=v_ssm_norm_w,
              w_out_ssm=v_w_out_ssm, conf_conv_w=v_conf_conv_w, conf_conv_b=v_conf_conv_b, conf_ln_w=v_conf_ln_w,
              conf_ln_b=v_conf_ln_b, w_out_conf=v_w_out_conf, w_out=v_w_out, final_norm_w=v_final_norm_w)
    grads, deltas, new_m, new_v = {}, {}, {}, {}
    sharded = ("w_in", "w_mod", "w_out_ssm", "w_out_conf", "w_out", "ssm_conv_w", "conf_conv_w")
    for i, nm in enumerate(sharded):
        shp = given[nm].shape
        w2 = given[nm].reshape(shp[1], shp[2])
        res = _adamw(parts[i], w2, ms[nm].reshape(w2.shape), vs[nm].reshape(w2.shape), "adamw_" + nm)
        grads[nm], deltas[nm], new_m[nm], new_v[nm] = [r.reshape(shp) for r in res]
    shapes = {nm: given[nm].shape for nm, _ in _SMALL}
    res = _adamw(small_parts, _pack_small(given), _pack_small(ms), _pack_small(vs), "adamw_small")
    for dst, packed in zip((grads, deltas, new_m, new_v), res):
        dst.update(_unpack_small(packed, shapes))

    loss = lax.psum(loss_acc[0, 0], ("x", "y", "c"))
    order = ("c_ctx", "w_mod", "b_mod", "norm_w", "w_in", "ssm_conv_w", "ssm_conv_b", "dt_bias", "a_log", "d_skip", "ssm_norm_w",
             "w_out_ssm", "conf_conv_w", "conf_conv_b", "conf_ln_w", "conf_ln_b", "w_out_conf", "w_out", "final_norm_w")
    return (loss, gx.reshape(1, L, D), *[grads[n] for n in order], *[deltas[n] for n in order],
            *[new_m[n] for n in order], *[new_v[n] for n in order])
```

```python
import jax
import jax.numpy as jnp
from jax import lax
from jax.experimental import pallas as pl
from jax.experimental.pallas import tpu as pltpu

f32 = jnp.float32
bf16 = jnp.bfloat16

D = 1024
DI = 2048
NG = 8
HPG = 4
HD = 64
NS = 128
NH = 32
Q = 128
GRID_W = 64
CK = 31
SK = 4
EPS = 1e-6
RT = 256
N_DEV = 8
IN_COLS = 11328
X0, B0, C0, Z0, G10, G20, DT0, GV0, GG0, CG0, NP = 0, 2048, 3072, 4096, 6144, 7168, 8192, 9216, 10240, 11264, 12288
VMEM_LIMIT = 50 * 1024 * 1024
NEG = -1e30

ADAM_LR, ADAM_B1, ADAM_B2, ADAM_EPS, ADAM_WD, ADAM_STEP = 0.001, 0.9, 0.999, 1e-08, 0.01, 10

MESH = pl.DeviceIdType.MESH
S = jax.ShapeDtypeStruct


def _params(*sem):
    return pltpu.CompilerParams(dimension_semantics=tuple(sem) if sem else None, vmem_limit_bytes=VMEM_LIMIT)


def _sig(x):
    return 1.0 / (1.0 + jnp.exp(-x))


def _silu(x):
    return x * _sig(x)


def _dsilu(x, s):
    return s * (1.0 + x * (1.0 - s))


def _dot(a, b):
    return jnp.dot(a, b, preferred_element_type=f32)


def _dot_nt(a, b):
    return lax.dot_general(a, b, (((1,), (1,)), ((), ())), preferred_element_type=f32)


def _dot_tn(a, b):
    return lax.dot_general(a, b, (((0,), (0,)), ((), ())), preferred_element_type=f32)


def _dot3(t_bf, v):
    v1 = v.astype(bf16)
    r1 = v - v1.astype(f32)
    v2 = r1.astype(bf16)
    v3 = (r1 - v2.astype(f32)).astype(bf16)
    return _dot(t_bf, v1) + _dot(t_bf, v2) + _dot(t_bf, v3)


def _pick(n, prefs):
    for p in prefs:
        if n % p == 0:
            return p
    return n


def _full(shape):
    nd = len(shape)
    return pl.BlockSpec(shape, lambda *_: (0,) * nd)


def _matmul(a, b, out_dtype, name, tm=None, tn=None, tk=None, tb=False, comm=None):
    m, k = a.shape
    n = b.shape[0] if tb else b.shape[1]
    tm = tm if tm and m % tm == 0 else _pick(m, (768, 512, 256, 128))
    tn = tn if tn and n % tn == 0 else _pick(n, (1024, 512, 256, 128))
    tk = tk if tk and k % tk == 0 else _pick(k, (1024, 768, 512, 256, 128))
    nk = k // tk
    gi, gj = m // tm, n // tn
    carrs, modes = comm if comm else ((), ())
    nc = len(carrs)

    def kern(*refs):
        a_ref, b_ref = refs[:2]
        cins = refs[2:2 + nc]
        o_ref = refs[2 + nc]
        couts = refs[3 + nc:3 + 2 * nc]
        acc_ref = refs[3 + 2 * nc]
        sems = refs[4 + 2 * nc:]
        i, j, kk = pl.program_id(0), pl.program_id(1), pl.program_id(2)
        if nc:
            @pl.when(jnp.logical_and(jnp.logical_and(i == 0, j == 0), kk == 0))
            def _():
                _xchg_start(cins, couts, *sems, modes)

        part = _dot_nt(a_ref[...], b_ref[...]) if tb else _dot(a_ref[...], b_ref[...])
        if nk == 1:
            o_ref[...] = part.astype(o_ref.dtype)
        else:
            @pl.when(kk == 0)
            def _():
                acc_ref[...] = part

            @pl.when(kk > 0)
            def _():
                acc_ref[...] += part

            @pl.when(kk == nk - 1)
            def _():
                o_ref[...] = acc_ref[...].astype(o_ref.dtype)

        if nc:
            @pl.when(jnp.logical_and(jnp.logical_and(i == gi - 1, j == gj - 1), kk == nk - 1))
            def _():
                _xchg_wait(cins, couts, *sems, modes)

    anyspec = pl.BlockSpec(memory_space=pl.ANY)
    bspec = pl.BlockSpec((tn, tk), lambda i, j, kk: (j, kk)) if tb else pl.BlockSpec((tk, tn), lambda i, j, kk: (kk, j))
    out_shape = (S((m, n), out_dtype),) + _xchg_out_shapes(carrs, modes)
    res = pl.pallas_call(
        kern, out_shape=out_shape, grid=(gi, gj, nk),
        in_specs=[pl.BlockSpec((tm, tk), lambda i, j, kk: (i, kk)), bspec] + [anyspec] * nc,
        out_specs=(pl.BlockSpec((tm, tn), lambda i, j, kk: (i, j)),) + (anyspec,) * nc,
        scratch_shapes=[pltpu.VMEM((tm, tn), f32)] + (_xchg_sems(nc) if nc else []),
        compiler_params=_params(*((("arbitrary",) * 3) if nc else ("parallel", "parallel", "arbitrary"))), name=name)(a, b, *carrs)
    return res if nc else res[0]


def _mod_fwd(cc8, w_mod_bf, b_mod):
    def kern(c_ref, w_ref, b_ref, o_ref):
        o_ref[...] = _dot(_silu(c_ref[...]).astype(bf16), w_ref[...]) + b_ref[...]

    return pl.pallas_call(kern, out_shape=S((8, 3 * D), f32), compiler_params=_params(), name="mod_fwd")(cc8, w_mod_bf, b_mod)


def _mod_bwd(ct, dmod8, w_mod_bf):
    tc = 512
    nj = 3 * D // tc

    def kern(ct_ref, dm_ref, w_ref, dw_ref, db_ref, dc_ref):
        j = pl.program_id(0)
        c = ct_ref[:, 0:1]
        cx = ct_ref[:, 1:2]
        sx = _sig(cx)
        dmx = dm_ref[0:1, :]
        dmc = dm_ref[1:2, :]
        dw_ref[...] = (_silu(c) * dmx + (cx * sx) * dmc).astype(bf16)
        db_ref[...] = dmx + dmc
        t = jnp.sum(w_ref[...].astype(f32) * dmc.astype(bf16).astype(f32), axis=1, keepdims=True) * _dsilu(cx, sx)

        @pl.when(j == 0)
        def _():
            dc_ref[...] = jnp.zeros_like(dc_ref)

        dc_ref[...] += jnp.broadcast_to(t, (D, 128))

    return pl.pallas_call(
        kern, out_shape=(S((D, 3 * D), bf16), S((1, 3 * D), f32), S((D, 128), f32)), grid=(nj,),
        in_specs=[_full((D, 128)), pl.BlockSpec((8, tc), lambda j: (0, j)), pl.BlockSpec((D, tc), lambda j: (0, j))],
        out_specs=(pl.BlockSpec((D, tc), lambda j: (0, j)), pl.BlockSpec((1, tc), lambda j: (0, j)), _full((D, 128))),
        compiler_params=_params("arbitrary"), name="mod_bwd")(ct, dmod8, w_mod_bf)


def _prenorm(x, ctx, norm_w, mod):
    L, Lc = x.shape[0], ctx.shape[0]
    nlx, nt = L // RT, (L + Lc) // RT

    def kern(x_ref, c_ref, nw_ref, mod_ref, h_ref, ht_ref):
        i = pl.program_id(0)
        is_c = i >= nlx
        xv = jnp.where(is_c, c_ref[...], x_ref[...])
        shift = jnp.where(is_c, mod_ref[1:2, 0:D], mod_ref[0:1, 0:D])
        scale = jnp.where(is_c, mod_ref[1:2, D:2 * D], mod_ref[0:1, D:2 * D])
        r = lax.rsqrt(jnp.mean(xv * xv, axis=1, keepdims=True) + EPS)
        hv = (xv * r) * nw_ref[...] * (1.0 + scale) + shift
        h_ref[...] = hv.astype(bf16)
        ht_ref[...] = jnp.transpose(hv).astype(bf16)

    return pl.pallas_call(
        kern, out_shape=(S((L + Lc, D), bf16), S((D, L + Lc), bf16)), grid=(nt,),
        in_specs=[pl.BlockSpec((RT, D), lambda i: (jnp.minimum(i, nlx - 1), 0)),
                  pl.BlockSpec((RT, D), lambda i: (jnp.maximum(i - nlx, 0), 0)),
                  _full((1, D)), _full((8, 3 * D))],
        out_specs=(pl.BlockSpec((RT, D), lambda i: (i, 0)), pl.BlockSpec((D, RT), lambda i: (0, i))),
        compiler_params=_params("parallel"), name="prenorm")(x, ctx, norm_w, mod)


def _prenorm_bwd(x, ctx, dh, dx1, norm_w, mod):
    L, Lc = x.shape[0], ctx.shape[0]
    nlx, nt = L // RT, (L + Lc) // RT

    def kern(x_ref, c_ref, dh_ref, dx1_ref, nw_ref, mod_ref, gx_ref, dnw_ref, acc_ref):
        i = pl.program_id(0)
        is_c = i >= nlx

        @pl.when(i == 0)
        def _():
            dnw_ref[...] = jnp.zeros_like(dnw_ref)
            acc_ref[...] = jnp.zeros_like(acc_ref)

        xv = jnp.where(is_c, c_ref[...], x_ref[...])
        scale = jnp.where(is_c, mod_ref[1:2, D:2 * D], mod_ref[0:1, D:2 * D])
        nw = nw_ref[...]
        r = lax.rsqrt(jnp.mean(xv * xv, axis=1, keepdims=True) + EPS)
        xn = xv * r
        dh = dh_ref[...]
        dsh = jnp.sum(dh, axis=0, keepdims=True)
        dsc = jnp.sum(dh * (xn * nw), axis=0, keepdims=True)
        dxnw = dh * (1.0 + scale)
        dnw_ref[...] += jnp.sum(dxnw * xn, axis=0, keepdims=True)
        dxn = dxnw * nw
        dx = r * (dxn - xn * jnp.mean(dxn * xn, axis=1, keepdims=True))

        @pl.when(jnp.logical_not(is_c))
        def _():
            gx_ref[...] = dx1_ref[...] + dx
            acc_ref[0:1, :] += dsh
            acc_ref[1:2, :] += dsc

        @pl.when(is_c)
        def _():
            acc_ref[2:3, :] += dsh
            acc_ref[3:4, :] += dsc

    xmap = lambda i: (jnp.minimum(i, nlx - 1), 0)
    return pl.pallas_call(
        kern, out_shape=(S((L, D), f32), S((1, D), f32), S((8, D), f32)), grid=(nt,),
        in_specs=[pl.BlockSpec((RT, D), xmap), pl.BlockSpec((RT, D), lambda i: (jnp.maximum(i - nlx, 0), 0)),
                  pl.BlockSpec((RT, D), lambda i: (i, 0)), pl.BlockSpec((RT, D), xmap), _full((1, D)), _full((8, 3 * D))],
        out_specs=(pl.BlockSpec((RT, D), xmap), _full((1, D)), _full((8, D))),
        compiler_params=_params("arbitrary"), name="prenorm_bwd")(x, ctx, dh, dx1, norm_w, mod)


def _halo_specs(nt_rows, ct):
    cur = pl.BlockSpec((RT, ct), lambda i, j: (i, j))
    prev = pl.BlockSpec((8, ct), lambda i, j: (jnp.maximum(i * (RT // 8) - 1, 0), j))
    nxt = pl.BlockSpec((8, ct), lambda i, j: (jnp.minimum((i + 1) * (RT // 8), nt_rows // 8 - 1), j))
    return cur, prev, nxt


def _fill_halo(scr, cur_ref, prev_ref, next_ref, i, nlx, nt):
    prev_ok = jnp.logical_and(i != 0, i != nlx)
    next_ok = jnp.logical_and(i != nlx - 1, i != nt - 1)
    scr[0:8, :] = jnp.where(prev_ok, prev_ref[...], 0.0)
    scr[8:8 + RT, :] = cur_ref[...]
    scr[8 + RT:16 + RT, :] = jnp.where(next_ok, next_ref[...], 0.0)


CONV_RB = 32


def _conv_blocks(ct):
    return [(slice(cb * 128, (cb + 1) * 128), r0) for cb in range(ct // 128) for r0 in range(0, RT, CONV_RB)]


def _ssm_conv_fwd(proj, w8, b, nlx):
    T = proj.shape[0]
    nt = T // RT
    ct = 1024
    cur, prev, nxt = _halo_specs(T, ct)

    def kern(cur_ref, prev_ref, next_ref, w_ref, b_ref, o_ref, scr):
        i = pl.program_id(0)
        _fill_halo(scr, cur_ref, prev_ref, next_ref, i, nlx, nt)
        for cs, r0 in _conv_blocks(ct):
            acc = jnp.broadcast_to(b_ref[:, cs], (CONV_RB, 128))
            for k in range(SK):
                acc = acc + w_ref[k:k + 1, cs] * scr[pl.ds(6 + k + r0, CONV_RB), cs]
            o_ref[r0:r0 + CONV_RB, cs] = _silu(acc)

    return pl.pallas_call(
        kern, out_shape=S((T, 4096), f32), grid=(nt, 4096 // ct),
        in_specs=[cur, prev, nxt, pl.BlockSpec((8, ct), lambda i, j: (0, j)), pl.BlockSpec((1, ct), lambda i, j: (0, j))],
        out_specs=pl.BlockSpec((RT, ct), lambda i, j: (i, j)),
        scratch_shapes=[pltpu.VMEM((RT + 16, ct), f32)],
        compiler_params=_params("parallel", "parallel"), name="ssm_conv_fwd")(proj, proj, proj, w8, b)


def _ssm_conv_dpre(dxbc, proj, w8, b, nlx):
    T = proj.shape[0]
    nt = T // RT
    ct = 1024
    cur = pl.BlockSpec((RT, ct), lambda j, i: (i, j))
    prev = pl.BlockSpec((8, ct), lambda j, i: (jnp.maximum(i * (RT // 8) - 1, 0), j))
    nxt = pl.BlockSpec((8, ct), lambda j, i: (jnp.minimum((i + 1) * (RT // 8), T // 8 - 1), j))

    def kern(d_ref, cur_ref, prev_ref, next_ref, w_ref, b_ref, dpre_ref, dw_ref, db_ref, scr):
        i = pl.program_id(1)
        _fill_halo(scr, cur_ref, prev_ref, next_ref, i, nlx, nt)

        @pl.when(i == 0)
        def _():
            dw_ref[...] = jnp.zeros_like(dw_ref)
            db_ref[...] = jnp.zeros_like(db_ref)

        for cb in range(ct // 128):
            cs = slice(cb * 128, (cb + 1) * 128)
            db_acc = jnp.zeros((CONV_RB, 128), f32)
            dw_acc = [jnp.zeros((CONV_RB, 128), f32) for _ in range(SK)]
            for r0 in range(0, RT, CONV_RB):
                taps = [scr[pl.ds(6 + k + r0, CONV_RB), cs] for k in range(SK)]
                pre = jnp.broadcast_to(b_ref[:, cs], (CONV_RB, 128))
                for k in range(SK):
                    pre = pre + w_ref[k:k + 1, cs] * taps[k]
                dpre = d_ref[r0:r0 + CONV_RB, cs] * _dsilu(pre, _sig(pre))
                dpre_ref[r0:r0 + CONV_RB, cs] = dpre
                db_acc = db_acc + dpre
                dw_acc = [dw_acc[k] + dpre * taps[k] for k in range(SK)]
            db_ref[:, cs] += jnp.sum(db_acc, axis=0, keepdims=True)
            for k in range(SK):
                dw_ref[k:k + 1, cs] += jnp.sum(dw_acc[k], axis=0, keepdims=True)

    return pl.pallas_call(
        kern, out_shape=(S((T, 4096), f32), S((8, 4096), f32), S((1, 4096), f32)), grid=(4096 // ct, nt),
        in_specs=[cur, cur, prev, nxt, pl.BlockSpec((8, ct), lambda j, i: (0, j)), pl.BlockSpec((1, ct), lambda j, i: (0, j))],
        out_specs=(cur, pl.BlockSpec((8, ct), lambda j, i: (0, j)), pl.BlockSpec((1, ct), lambda j, i: (0, j))),
        scratch_shapes=[pltpu.VMEM((RT + 16, ct), f32)],
        compiler_params=_params("parallel", "arbitrary"), name="ssm_conv_dpre")(dxbc, proj, proj, proj, w8, b)


def _ssm_conv_t(dpre, w8, dproj, nlx):
    T = dpre.shape[0]
    nt = T // RT
    ct = 1024
    cur, prev, nxt = _halo_specs(T, ct)

    def kern(cur_ref, prev_ref, next_ref, w_ref, _alias, o_ref, scr):
        i = pl.program_id(0)
        _fill_halo(scr, cur_ref, prev_ref, next_ref, i, nlx, nt)
        for cs, r0 in _conv_blocks(ct):
            acc = jnp.zeros((CONV_RB, 128), f32)
            for k in range(SK):
                acc = acc + w_ref[k:k + 1, cs] * scr[pl.ds(10 - k + r0, CONV_RB), cs]
            o_ref[r0:r0 + CONV_RB, cs] = acc.astype(bf16)

    return pl.pallas_call(
        kern, out_shape=S(dproj.shape, bf16), grid=(nt, 4096 // ct),
        in_specs=[cur, prev, nxt, pl.BlockSpec((8, ct), lambda i, j: (0, j)), pl.BlockSpec(memory_space=pl.ANY)],
        out_specs=pl.BlockSpec((RT, ct), lambda i, j: (i, j)),
        scratch_shapes=[pltpu.VMEM((RT + 16, ct), f32)], input_output_aliases={4: 0},
        compiler_params=_params("parallel", "parallel"), name="ssm_conv_t")(dpre, dpre, dpre, w8, dproj)


def _tri():
    li = lax.broadcasted_iota(jnp.int32, (Q, Q), 0)
    si = lax.broadcasted_iota(jnp.int32, (Q, Q), 1)
    return (si <= li).astype(bf16), (si >= li).astype(bf16)


def _dt_prep(proj, bias_row, alog_row):
    T = proj.shape[0]
    nch = T // Q

    def kern(raw_ref, b_ref, al_ref, dt_ref, la_ref):
        lane = lax.broadcasted_iota(jnp.int32, (Q, 128), 1)
        v = raw_ref[...] + b_ref[...]
        dt = jnp.maximum(v, 0.0) + jnp.log1p(jnp.exp(-jnp.abs(v)))
        a = jnp.where(lane[0:1, :] < 2 * NH, -jnp.exp(al_ref[...]), 0.0)
        da = dt * a
        tri, trit = _tri()
        dt_ref[...] = dt
        la_ref[...] = jnp.where(lane < NH, _dot3(tri, da), _dot3(trit, da))

    return pl.pallas_call(
        kern, out_shape=(S((T, 128), f32), S((T, 128), f32)), grid=(nch,),
        in_specs=[pl.BlockSpec((Q, 128), lambda c: (c, DT0 // 128)), _full((1, 128)), _full((1, 128))],
        out_specs=(pl.BlockSpec((Q, 128), lambda c: (c, 0)), pl.BlockSpec((Q, 128), lambda c: (c, 0))),
        compiler_params=_params("parallel"), name="dt_prep")(proj, bias_row, alog_row)


def _dt_bwd(a1, a2, r2, sv, dt, la, proj, bias_row, alog_row, dproj):
    T = proj.shape[0]
    nch = T // Q
    blk = pl.BlockSpec((Q, 128), lambda c: (c, 0))

    def kern(a1_ref, a2_ref, r2_ref, s_ref, dt_ref, la_ref, raw_ref, b_ref, al_ref, _alias, o_ref, db_ref, dal_ref):
        c = pl.program_id(0)

        @pl.when(c == 0)
        def _():
            db_ref[...] = jnp.zeros_like(db_ref)
            dal_ref[...] = jnp.zeros_like(dal_ref)

        lane = lax.broadcasted_iota(jnp.int32, (Q, 128), 1)
        row = lax.broadcasted_iota(jnp.int32, (Q, 128), 0)
        fwd = lane < NH
        dt = dt_ref[...]
        la = la_ref[...]
        a2v = a2_ref[...]
        r2v = r2_ref[...]
        a = jnp.where(lane[0:1, :] < 2 * NH, -jnp.exp(al_ref[...]), 0.0)
        la_e = jnp.where(fwd[0:1, :], la[Q - 1:Q, :], la[0:1, :])
        is_end = row == jnp.where(fwd, Q - 1, 0)
        e_end = jnp.exp(la_e - la)
        wend = e_end * dt
        extra = s_ref[0:1, :] * jnp.exp(la_e) + jnp.sum(wend * a2v, axis=0, keepdims=True)
        dla = a1_ref[...] - dt * r2v - wend * a2v + jnp.where(is_end, extra, 0.0)
        tri, trit = _tri()
        rcs = jnp.where(fwd, _dot3(trit, dla), _dot3(tri, dla))
        ddt = r2v + e_end * a2v + a * rcs
        dal_ref[...] += a * jnp.sum(dt * rcs, axis=0, keepdims=True)
        draw = jnp.where(lane < 2 * NH, ddt * _sig(raw_ref[...] + b_ref[...]), 0.0)
        db_ref[...] += jnp.sum(draw, axis=0, keepdims=True)
        o_ref[...] = jnp.zeros_like(o_ref)
        o_ref[:, 0:128] = draw.astype(bf16)

    return pl.pallas_call(
        kern, out_shape=(S(dproj.shape, bf16), S((1, 128), f32), S((1, 128), f32)), grid=(nch,),
        in_specs=[blk, blk, blk, blk, blk, blk, pl.BlockSpec((Q, 128), lambda c: (c, DT0 // 128)),
                  _full((1, 128)), _full((1, 128)), pl.BlockSpec(memory_space=pl.ANY)],
        out_specs=(pl.BlockSpec((Q, 1024), lambda c: (c, DT0 // 1024)), _full((1, 128)), _full((1, 128))),
        input_output_aliases={9: 0},
        compiler_params=_params("arbitrary"), name="dt_bwd")(a1, a2, r2, sv, dt, la, proj, bias_row, alog_row, dproj)


def _split2(v):
    hi = v.astype(bf16)
    lo = (v - hi.astype(f32)).astype(bf16)
    return jnp.concatenate([hi, lo], axis=1)


def _scan_consts(rev):
    hoff = NH if rev else 0
    g = jnp.arange(NG, dtype=jnp.int32)[:, None, None]

    def rc(nr, ncol):
        return jnp.arange(nr, dtype=jnp.int32)[None, :, None], jnp.arange(ncol, dtype=jnp.int32)[None, None, :]

    r, c = rc(2 * 128, HPG * HD)
    sel_w = (lax.rem(r, 128) == hoff + HPG * g + c // HD).astype(bf16)
    r, c = rc(2 * HPG * HD, 128)
    ind_h = (c == hoff + HPG * g + lax.rem(r, HPG * HD) // HD).astype(bf16)
    r, c = rc(2 * HPG * Q, 128)
    ind_e = (c == hoff + HPG * g + lax.rem(r, HPG * Q) // Q).astype(bf16)
    return sel_w, ind_h, ind_e


def _masks(rev):
    li = lax.broadcasted_iota(jnp.int32, (Q, Q), 0)
    si = lax.broadcasted_iota(jnp.int32, (Q, Q), 1)
    mask = (li <= si) if rev else (li >= si)
    mask_t = (li >= si) if rev else (li <= si)
    lane = lax.broadcasted_iota(jnp.int32, (Q, HPG * HD), 1)
    hms = [jnp.logical_and(lane >= r * HD, lane < (r + 1) * HD) for r in range(HPG)]
    return mask, mask_t, hms


def _mine(hoff):
    lane = lax.broadcasted_iota(jnp.int32, (Q, 128), 1)
    return jnp.logical_and(lane >= hoff, lane < hoff + NH)


def _head_row(vals, hc0):
    lane = lax.broadcasted_iota(jnp.int32, (1, HPG * HD), 1)
    out = jnp.zeros((1, HPG * HD), f32)
    for r in range(HPG):
        out = jnp.where(jnp.logical_and(lane >= r * HD, lane < (r + 1) * HD), vals[:, hc0 + r:hc0 + r + 1], out)
    return out


def _chunk_of(j, rev, nxc, nch):
    return (nch - 1 - j) if rev else lax.rem(j + nxc, nch)


def _ssd_fwd(xbc, dt, la, consts, rev, nxc, name, y_acc=None):
    T = xbc.shape[0]
    nch = T // Q
    hoff = NH if rev else 0
    e = 0 if rev else Q - 1
    cm = lambda j: _chunk_of(j, rev, nxc, nch)
    sel_w = consts[0]
    has_acc = y_acc is not None

    def kern(*refs):
        xbc_ref, dt_ref, la_ref, sw_ref = refs[:4]
        yacc_ref = refs[4] if has_acc else None
        y_ref, hp_ref, h_ref = refs[4 + has_acc:]
        j = pl.program_id(0)

        @pl.when(j == 0)
        def _():
            h_ref[...] = jnp.zeros_like(h_ref)

        hp_ref[...] = h_ref[...]
        mask, _, hms = _masks(rev)
        la_all = la_ref[...]
        dt_all = dt_ref[...]
        la_t = jnp.transpose(la_all)
        dt_t = jnp.transpose(dt_all)
        la_e = la_all[e:e + 1, :]
        w2 = _split2(jnp.exp(jnp.where(_mine(hoff), la_e - la_all, 0.0)) * dt_all)
        e2 = _split2(jnp.exp(la_all))
        ela_e = jnp.exp(la_e)
        for g in range(NG):
            hc0 = hoff + g * HPG
            x = xbc_ref[:, g * 256:(g + 1) * 256]
            bb = xbc_ref[:, B0 + g * NS:B0 + (g + 1) * NS].astype(bf16)
            cb = xbc_ref[:, C0 + g * NS:C0 + (g + 1) * NS].astype(bf16)
            ht = h_ref[g * NS:(g + 1) * NS, :]
            scores = _dot_nt(cb, bb)
            yoff = _dot(cb, ht.astype(bf16))
            wend = _dot(w2, sw_ref[g])
            expla = _dot(e2, sw_ref[g])
            mixes, xstack = [], []
            for r in range(HPG):
                hc = hc0 + r
                la_rep = jnp.broadcast_to(la_all[:, hc:hc + 1], (Q, 128))
                decay = jnp.exp(jnp.where(mask, la_rep - la_t[hc:hc + 1, :], NEG))
                mixes.append((scores * decay * dt_t[hc:hc + 1, :]).astype(bf16))
                xstack.append(jnp.where(hms[r], x, 0.0).astype(bf16))
            y = _dot(jnp.concatenate(mixes, axis=1), jnp.concatenate(xstack, axis=0)) + yoff * expla
            if has_acc:
                y = y + yacc_ref[:, g * 256:(g + 1) * 256]
            y_ref[:, g * 256:(g + 1) * 256] = y
            h_ref[g * NS:(g + 1) * NS, :] = ht * _head_row(ela_e, hc0) + _dot_tn(bb, (x * wend).astype(bf16))

    row = lambda j: (cm(j), 0)
    yblk = pl.BlockSpec((Q, DI), row)
    return pl.pallas_call(
        kern, out_shape=(S((T, DI), f32), S((nch, NG * NS, HPG * HD), f32)), grid=(nch,),
        in_specs=[pl.BlockSpec((Q, 4096), row), pl.BlockSpec((Q, 128), row), pl.BlockSpec((Q, 128), row),
                  _full(sel_w.shape)] + ([yblk] if has_acc else []),
        out_specs=(yblk, pl.BlockSpec((None, NG * NS, HPG * HD), lambda j: (cm(j), 0, 0))),
        scratch_shapes=[pltpu.VMEM((NG * NS, HPG * HD), f32)],
        input_output_aliases={4: 0} if has_acc else {},
        compiler_params=_params("arbitrary"), name=name)(xbc, dt, la, sel_w, *([y_acc] if has_acc else []))


def _ssd_bwd(xbc, dy, dt, la, hprev, dskip_full, consts, rev, nxc, name, acc=None):
    T = xbc.shape[0]
    nch = T // Q
    hoff = NH if rev else 0
    e = 0 if rev else Q - 1
    cm = lambda j: _chunk_of(nch - 1 - j, rev, nxc, nch)
    has_acc = acc is not None
    sel_w, ind_h, ind_e = consts

    def kern(*refs):
        xbc_ref, dy_ref, dt_ref, la_ref, hp_ref, dsk_ref, sw_ref, ih_ref, ie_ref = refs[:9]
        k = 9
        if has_acc:
            dxbc_in, a1_in, a2_in, r2_in, s_in = refs[k:k + 5]
            k += 5
        dxbc_ref, a1_ref, a2_ref, r2_ref, s_ref, g_ref, r2scr = refs[k:k + 7]
        j = pl.program_id(0)

        @pl.when(j == 0)
        def _():
            g_ref[...] = jnp.zeros_like(g_ref)

        mask, mask_t, hms = _masks(rev)
        la_all = la_ref[...]
        dt_all = dt_ref[...]
        la_t = jnp.transpose(la_all)
        dt_t = jnp.transpose(dt_all)
        la_e = la_all[e:e + 1, :]
        w2 = _split2(jnp.exp(jnp.where(_mine(hoff), la_e - la_all, 0.0)) * dt_all)
        e2 = _split2(jnp.exp(la_all))
        d2 = _split2(dt_all)
        ela_e = jnp.exp(la_e)
        r2scr[...] = jnp.zeros_like(r2scr)
        a1acc = jnp.zeros((Q, 128), f32)
        a2acc = jnp.zeros((Q, 128), f32)
        sacc = jnp.zeros((1, 128), f32)
        for g in range(NG):
            hc0 = hoff + g * HPG
            x = xbc_ref[:, g * 256:(g + 1) * 256]
            bb = xbc_ref[:, B0 + g * NS:B0 + (g + 1) * NS].astype(bf16)
            cb = xbc_ref[:, C0 + g * NS:C0 + (g + 1) * NS].astype(bf16)
            dyv = dy_ref[:, g * 256:(g + 1) * 256]
            gt = g_ref[g * NS:(g + 1) * NS, :]
            ht = hp_ref[g * NS:(g + 1) * NS, :]
            gtb = gt.astype(bf16)
            htb = ht.astype(bf16)
            xb = x.astype(bf16)
            scores = _dot_nt(cb, bb)
            scores_t = _dot_nt(bb, cb)
            bg = _dot(bb, gtb)
            yoff = _dot(cb, htb)
            wend = _dot(w2, sw_ref[g])
            expla = _dot(e2, sw_ref[g])
            dtf = _dot(d2, sw_ref[g])
            dym = jnp.concatenate([jnp.where(hms[r], dyv, 0.0).astype(bf16) for r in range(HPG)], axis=0)
            dyx_all = _dot_nt(dym, xb)
            sdts, ehis, elos = [], [], []
            wsum = jnp.zeros((Q, Q), f32)
            for r in range(HPG):
                hc = hc0 + r
                la_rep = jnp.broadcast_to(la_all[:, hc:hc + 1], (Q, 128))
                la_r = la_t[hc:hc + 1, :]
                dt_r = dt_t[hc:hc + 1, :]
                decay = jnp.exp(jnp.where(mask, la_rep - la_r, NEG))
                decay_t = jnp.exp(jnp.where(mask_t, la_r - la_rep, NEG))
                dyx = dyx_all[r * Q:(r + 1) * Q, :]
                fm = dyx * (scores * decay)
                r2scr[hc:hc + 1, :] = jnp.sum(fm, axis=0, keepdims=True)
                em = fm * dt_r
                ehi = em.astype(bf16)
                ehis.append(ehi)
                elos.append((em - ehi.astype(f32)).astype(bf16))
                wsum = wsum + dyx * decay * dt_r
                sdts.append((scores_t * decay_t).astype(bf16))
            dx = dtf * _dot(jnp.concatenate(sdts, axis=1), dym) + wend * bg
            if not has_acc:
                dx = dx + dsk_ref[:, g * 256:(g + 1) * 256] * dyv
            a1acc = a1acc + _dot(jnp.concatenate(ehis + elos, axis=1), ie_ref[g]) \
                + _dot(_split2(dyv * yoff * expla), ih_ref[g])
            a2acc = a2acc + _dot(_split2(x * bg), ih_ref[g])
            sacc = sacc + jnp.sum(_dot(_split2(gt * ht), ih_ref[g]), axis=0, keepdims=True)
            wb = wsum.astype(bf16)
            dysb = (dyv * expla).astype(bf16)
            dc = _dot(wb, bb) + _dot_nt(dysb, htb)
            db = _dot_tn(wb, cb) + _dot_nt((x * wend).astype(bf16), gtb)
            g_ref[g * NS:(g + 1) * NS, :] = gt * _head_row(ela_e, hc0) + _dot_tn(cb, dysb)
            if has_acc:
                dx = dx + dxbc_in[:, g * 256:(g + 1) * 256]
                db = db + dxbc_in[:, B0 + g * NS:B0 + (g + 1) * NS]
                dc = dc + dxbc_in[:, C0 + g * NS:C0 + (g + 1) * NS]
            dxbc_ref[:, g * 256:(g + 1) * 256] = dx
            dxbc_ref[:, B0 + g * NS:B0 + (g + 1) * NS] = db
            dxbc_ref[:, C0 + g * NS:C0 + (g + 1) * NS] = dc
        r2c = jnp.transpose(r2scr[...])
        sc = jnp.broadcast_to(sacc, (Q, 128))
        if has_acc:
            a1acc = a1acc + a1_in[...]
            a2acc = a2acc + a2_in[...]
            r2c = r2c + r2_in[...]
            sc = sc + s_in[...]
        a1_ref[...] = a1acc
        a2_ref[...] = a2acc
        r2_ref[...] = r2c
        s_ref[...] = sc

    blk = pl.BlockSpec((Q, 128), lambda j: (cm(j), 0))
    big = pl.BlockSpec((Q, 4096), lambda j: (cm(j), 0))
    in_specs = [big, pl.BlockSpec((Q, DI), lambda j: (cm(j), 0)), blk, blk,
                pl.BlockSpec((None, NG * NS, HPG * HD), lambda j: (cm(j), 0, 0)), _full((1, DI)),
                _full(sel_w.shape), _full(ind_h.shape), _full(ind_e.shape)]
    args = [xbc, dy, dt, la, hprev, dskip_full, sel_w, ind_h, ind_e]
    aliases = {}
    if has_acc:
        in_specs += [big, blk, blk, blk, blk]
        args += list(acc)
        aliases = {9: 0, 10: 1, 11: 2, 12: 3, 13: 4}
    return pl.pallas_call(
        kern, out_shape=(S((T, 4096), f32), S((T, 128), f32), S((T, 128), f32), S((T, 128), f32), S((T, 128), f32)),
        grid=(nch,), in_specs=in_specs, out_specs=(big, blk, blk, blk, blk),
        scratch_shapes=[pltpu.VMEM((NG * NS, HPG * HD), f32), pltpu.VMEM((128, Q), f32)],
        input_output_aliases=aliases,
        compiler_params=_params("arbitrary"), name=name)(*args)


def _ynorm_fwd(ysum, xbc, proj, dskip_full, nw, L):
    nlx = L // RT

    def kern(ys_ref, xs_ref, z_ref, dsk_ref, nw_ref, y_ref, yn_ref, ynt_ref):
        y = ys_ref[...] + dsk_ref[...] * xs_ref[...]
        y_ref[...] = y
        yz = y * _silu(z_ref[...])
        for g in range(NG):
            sl = yz[:, g * 256:(g + 1) * 256]
            r = lax.rsqrt(jnp.mean(sl * sl, axis=1, keepdims=True) + EPS)
            yn = (sl * r) * nw_ref[:, g * 256:(g + 1) * 256]
            yn_ref[:, g * 256:(g + 1) * 256] = yn.astype(bf16)
            ynt_ref[g * 256:(g + 1) * 256, :] = jnp.transpose(yn).astype(bf16)

    blk = pl.BlockSpec((RT, DI), lambda i: (i, 0))
    return pl.pallas_call(
        kern, out_shape=(S((L, DI), f32), S((L, DI), bf16), S((DI, L), bf16)), grid=(nlx,),
        in_specs=[blk, blk, pl.BlockSpec((RT, DI), lambda i: (i, Z0 // DI)), _full((1, DI)), _full((1, DI))],
        out_specs=(blk, blk, pl.BlockSpec((DI, RT), lambda i: (0, i))),
        compiler_params=_params("parallel"), name="ynorm_fwd")(ysum, xbc, proj, dskip_full, nw)


def _ynorm_bwd(dyn, y, xbc, proj, dskip_full, nw, dproj):
    L = y.shape[0]
    T = proj.shape[0]
    nlx, nt = L // RT, T // RT

    def kern(dyn_ref, y_ref, xs_ref, z_ref, dsk_ref, nw_ref, _alias, dz_ref, dy_ref, dnw_ref, dsk_acc):
        i = pl.program_id(0)

        @pl.when(i == 0)
        def _():
            dnw_ref[...] = jnp.zeros_like(dnw_ref)
            dsk_acc[...] = jnp.zeros_like(dsk_acc)

        @pl.when(i >= nlx)
        def _():
            dz_ref[...] = jnp.zeros_like(dz_ref)
            dy_ref[...] = jnp.zeros_like(dy_ref)

        @pl.when(i < nlx)
        def _():
            y = y_ref[...]
            z = z_ref[...]
            sz = _sig(z)
            gz = z * sz
            yz = y * gz
            dynv = dyn_ref[...]
            for g in range(NG):
                cs = slice(g * 256, (g + 1) * 256)
                sl = yz[:, cs]
                r = lax.rsqrt(jnp.mean(sl * sl, axis=1, keepdims=True) + EPS)
                yhat = sl * r
                dn = dynv[:, cs]
                dnw_ref[:, cs] += jnp.sum(dn * yhat, axis=0, keepdims=True)
                dyh = dn * nw_ref[:, cs]
                dyz = r * (dyh - yhat * jnp.mean(dyh * yhat, axis=1, keepdims=True))
                dyv = dyz * gz[:, cs]
                dy_ref[:, cs] = dyv
                dz_ref[:, cs] = (dyz * y[:, cs] * _dsilu(z[:, cs], sz[:, cs])).astype(bf16)
                dsk_acc[:, cs] += jnp.sum(dyv * xs_ref[:, cs], axis=0, keepdims=True)

    xmap = lambda i: (jnp.minimum(i, nlx - 1), 0)
    return pl.pallas_call(
        kern, out_shape=(S(dproj.shape, bf16), S((T, DI), f32), S((1, DI), f32), S((1, DI), f32)), grid=(nt,),
        in_specs=[pl.BlockSpec((RT, DI), xmap), pl.BlockSpec((RT, DI), xmap), pl.BlockSpec((RT, DI), xmap),
                  pl.BlockSpec((RT, DI), lambda i: (jnp.minimum(i, nlx - 1), Z0 // DI)), _full((1, DI)), _full((1, DI)),
                  pl.BlockSpec(memory_space=pl.ANY)],
        out_specs=(pl.BlockSpec((RT, DI), lambda i: (i, Z0 // DI)), pl.BlockSpec((RT, DI), lambda i: (i, 0)),
                   _full((1, DI)), _full((1, DI))),
        input_output_aliases={6: 0},
        compiler_params=_params("arbitrary"), name="ynorm_bwd")(dyn, y, xbc, proj, dskip_full, nw, dproj)


def _head_sums(cols):
    def kern(c_ref, o_ref):
        o_ref[...] = jnp.broadcast_to(jnp.sum(c_ref[...], axis=1, keepdims=True), (NH, 128))

    return pl.pallas_call(kern, out_shape=S((NH, 128), f32), name="head_sums")(cols)


SEG_STRIDE = 96
SEG_PAD = 16
NSEG = RT // GRID_W
CONF_ROWS = SEG_PAD + NSEG * SEG_STRIDE


SHIFT_ROWS = CONF_ROWS - 8
CONF_CW = 256


CONF_RB = 32


def _seg_zero_pads(scr):
    scr[0:SEG_PAD, :] = jnp.zeros((SEG_PAD, scr.shape[1]), f32)
    for s in range(NSEG):
        lo = SEG_PAD + s * SEG_STRIDE + GRID_W
        scr[lo:lo + SEG_STRIDE - GRID_W, :] = jnp.zeros((SEG_STRIDE - GRID_W, scr.shape[1]), f32)


def _seg_row(r0):
    return SEG_PAD + (r0 // GRID_W) * SEG_STRIDE + r0 % GRID_W


def _shift_copies(cps, scr, cs):
    for s in range(1, 8):
        cps[s - 1, :, :] = scr[pl.ds(s, SHIFT_ROWS), cs]


def _tap(cps, scr, cs, o):
    rs = o % 8
    return scr[pl.ds(o, GRID_W), cs] if rs == 0 else cps[rs - 1, pl.ds(o - rs, GRID_W), :]


def _conf_fwd(proj, w32, cb, lnw, lnb, L):
    nlx = L // RT

    def kern(v_ref, g_ref, cg_ref, w_ref, cb_ref, lnw_ref, lnb_ref, u1_ref, u3_ref, u3t_ref, scr, cps, u3_scr):
        _seg_zero_pads(scr)
        for r0 in range(0, RT, CONF_RB):
            rows = slice(r0, r0 + CONF_RB)
            scr[_seg_row(r0):_seg_row(r0) + CONF_RB, :] = v_ref[rows, :] * _sig(g_ref[rows, :])
        for cc in range(D // CONF_CW):
            cs = slice(cc * CONF_CW, (cc + 1) * CONF_CW)
            _shift_copies(cps, scr, cs)
            for s in range(NSEG):
                acc = jnp.broadcast_to(cb_ref[:, cs], (GRID_W, CONF_CW))
                for k in range(CK):
                    acc = acc + w_ref[k:k + 1, cs] * _tap(cps, scr, cs, SEG_PAD + s * SEG_STRIDE + k - CK // 2)
                u1_ref[s * GRID_W:(s + 1) * GRID_W, cs] = acc
        for r0 in range(0, RT, CONF_RB):
            rows = slice(r0, r0 + CONF_RB)
            u1 = u1_ref[rows, :]
            xc = u1 - jnp.mean(u1, axis=1, keepdims=True)
            r = lax.rsqrt(jnp.mean(xc * xc, axis=1, keepdims=True) + EPS)
            u2 = (xc * r) * lnw_ref[...] + lnb_ref[...]
            u3 = _silu(u2) * _silu(cg_ref[rows, :])
            u3_ref[rows, :] = u3.astype(bf16)
            u3_scr[rows, :] = u3
        u3t_ref[...] = jnp.transpose(u3_scr[...]).astype(bf16)

    blk = pl.BlockSpec((RT, D), lambda i: (i, 0))
    return pl.pallas_call(
        kern, out_shape=(S((L, D), f32), S((L, D), bf16), S((D, L), bf16)), grid=(nlx,),
        in_specs=[pl.BlockSpec((RT, D), lambda i: (i, GV0 // D)), pl.BlockSpec((RT, D), lambda i: (i, GG0 // D)),
                  pl.BlockSpec((RT, D), lambda i: (i, CG0 // D)), _full((32, D)), _full((1, D)), _full((1, D)), _full((1, D))],
        out_specs=(blk, blk, pl.BlockSpec((D, RT), lambda i: (0, i))),
        scratch_shapes=[pltpu.VMEM((CONF_ROWS, D), f32), pltpu.VMEM((7, SHIFT_ROWS, CONF_CW), f32), pltpu.VMEM((RT, D), f32)],
        compiler_params=_params("parallel"), name="conf_fwd")(proj, proj, proj, w32, cb, lnw, lnb)


def _conf_bwd(du3, u1, proj, w32, lnw, lnb, dproj):
    L = u1.shape[0]
    T = proj.shape[0]
    nlx, nt = L // RT, T // RT

    def kern(du3_ref, u1_ref, v_ref, g_ref, cg_ref, w_ref, lnw_ref, lnb_ref, _alias,
             o_ref, dw_ref, dcb_ref, dlw_ref, dlb_ref, scr_u, scr_d, du0_scr, cps_u, cps_d):
        i = pl.program_id(0)

        @pl.when(i == 0)
        def _():
            dw_ref[...] = jnp.zeros_like(dw_ref)
            dcb_ref[...] = jnp.zeros_like(dcb_ref)
            dlw_ref[...] = jnp.zeros_like(dlw_ref)
            dlb_ref[...] = jnp.zeros_like(dlb_ref)

        @pl.when(i >= nlx)
        def _():
            o_ref[...] = jnp.zeros_like(o_ref)

        @pl.when(i < nlx)
        def _():
            _seg_zero_pads(scr_u)
            _seg_zero_pads(scr_d)
            for r0 in range(0, RT, CONF_RB):
                rows = slice(r0, r0 + CONF_RB)
                cg = cg_ref[rows, :]
                scg = _sig(cg)
                u1 = u1_ref[rows, :]
                xc = u1 - jnp.mean(u1, axis=1, keepdims=True)
                r = lax.rsqrt(jnp.mean(xc * xc, axis=1, keepdims=True) + EPS)
                xhat = xc * r
                u2 = xhat * lnw_ref[...] + lnb_ref[...]
                s2 = _sig(u2)
                du3v = du3_ref[rows, :]
                du2 = du3v * (cg * scg) * _dsilu(u2, s2)
                o_ref[rows, 2 * D:3 * D] = (du3v * (u2 * s2) * _dsilu(cg, scg)).astype(bf16)
                dlw_ref[...] += jnp.sum(du2 * xhat, axis=0, keepdims=True)
                dlb_ref[...] += jnp.sum(du2, axis=0, keepdims=True)
                dxh = du2 * lnw_ref[...]
                du1 = r * (dxh - jnp.mean(dxh, axis=1, keepdims=True) - xhat * jnp.mean(dxh * xhat, axis=1, keepdims=True))
                dcb_ref[...] += jnp.sum(du1, axis=0, keepdims=True)
                scr_u[_seg_row(r0):_seg_row(r0) + CONF_RB, :] = v_ref[rows, :] * _sig(g_ref[rows, :])
                scr_d[_seg_row(r0):_seg_row(r0) + CONF_RB, :] = du1
            for cc in range(D // CONF_CW):
                cs = slice(cc * CONF_CW, (cc + 1) * CONF_CW)
                _shift_copies(cps_u, scr_u, cs)
                _shift_copies(cps_d, scr_d, cs)
                for k in range(CK):
                    t = jnp.zeros((GRID_W, CONF_CW), f32)
                    for s in range(NSEG):
                        base = SEG_PAD + s * SEG_STRIDE
                        t = t + scr_d[pl.ds(base, GRID_W), cs] * _tap(cps_u, scr_u, cs, base + k - CK // 2)
                    dw_ref[k:k + 1, cs] += jnp.sum(t, axis=0, keepdims=True)
                for s in range(NSEG):
                    base = SEG_PAD + s * SEG_STRIDE
                    acc = jnp.zeros((GRID_W, CONF_CW), f32)
                    for k in range(CK):
                        acc = acc + w_ref[k:k + 1, cs] * _tap(cps_d, scr_d, cs, base + CK // 2 - k)
                    du0_scr[s * GRID_W:(s + 1) * GRID_W, cs] = acc
            for r0 in range(0, RT, CONF_RB):
                rows = slice(r0, r0 + CONF_RB)
                du0 = du0_scr[rows, :]
                sg = _sig(g_ref[rows, :])
                o_ref[rows, 0:D] = (du0 * sg).astype(bf16)
                o_ref[rows, D:2 * D] = (du0 * v_ref[rows, :] * sg * (1.0 - sg)).astype(bf16)

    xmap = lambda i: (jnp.minimum(i, nlx - 1), 0)
    pmap = lambda cb: (lambda i: (jnp.minimum(i, nlx - 1), cb))
    return pl.pallas_call(
        kern, out_shape=(S(dproj.shape, bf16), S((32, D), f32), S((1, D), f32), S((1, D), f32), S((1, D), f32)), grid=(nt,),
        in_specs=[pl.BlockSpec((RT, D), xmap), pl.BlockSpec((RT, D), xmap),
                  pl.BlockSpec((RT, D), pmap(GV0 // D)), pl.BlockSpec((RT, D), pmap(GG0 // D)), pl.BlockSpec((RT, D), pmap(CG0 // D)),
                  _full((32, D)), _full((1, D)), _full((1, D)), pl.BlockSpec(memory_space=pl.ANY)],
        out_specs=(pl.BlockSpec((RT, 3 * D), lambda i: (i, GV0 // (3 * D))), _full((32, D)), _full((1, D)), _full((1, D)), _full((1, D))),
        scratch_shapes=[pltpu.VMEM((CONF_ROWS, D), f32), pltpu.VMEM((CONF_ROWS, D), f32), pltpu.VMEM((RT, D), f32),
                        pltpu.VMEM((7, SHIFT_ROWS, CONF_CW), f32), pltpu.VMEM((7, SHIFT_ROWS, CONF_CW), f32)],
        input_output_aliases={8: 0},
        compiler_params=_params("arbitrary"), name="conf_bwd")(du3, u1, proj, proj, proj, w32, lnw, lnb, dproj)


def _merge_fwd(bs, bc, proj):
    L = bs.shape[0]

    def kern(bs_ref, bc_ref, g1_ref, g2_ref, o_ref, ot_ref):
        mv = _sig(g1_ref[...]) * bs_ref[...] + _sig(g2_ref[...]) * bc_ref[...]
        o_ref[...] = mv.astype(bf16)
        ot_ref[...] = jnp.transpose(mv).astype(bf16)

    blk = pl.BlockSpec((RT, D), lambda i: (i, 0))
    return pl.pallas_call(
        kern, out_shape=(S((L, D), bf16), S((D, L), bf16)), grid=(L // RT,),
        in_specs=[blk, blk, pl.BlockSpec((RT, D), lambda i: (i, G10 // D)), pl.BlockSpec((RT, D), lambda i: (i, G20 // D))],
        out_specs=(blk, pl.BlockSpec((D, RT), lambda i: (0, i))),
        compiler_params=_params("parallel"), name="merge_fwd")(bs, bc, proj, proj)


def _merge_bwd(dmerged, bs, bc, proj):
    L = bs.shape[0]
    T = proj.shape[0]
    nlx, nt = L // RT, T // RT

    def kern(dm_ref, bs_ref, bc_ref, g1_ref, g2_ref, o_ref, dbs_ref, dbc_ref):
        i = pl.program_id(0)

        @pl.when(i >= nlx)
        def _():
            o_ref[...] = jnp.zeros_like(o_ref)

        @pl.when(i < nlx)
        def _():
            dm = dm_ref[...]
            s1 = _sig(g1_ref[...])
            s2 = _sig(g2_ref[...])
            dbs_ref[...] = (dm * s1).astype(bf16)
            dbc_ref[...] = (dm * s2).astype(bf16)
            o_ref[:, 0:D] = (dm * bs_ref[...] * s1 * (1.0 - s1)).astype(bf16)
            o_ref[:, D:2 * D] = (dm * bc_ref[...] * s2 * (1.0 - s2)).astype(bf16)

    xmap = lambda i: (jnp.minimum(i, nlx - 1), 0)
    pmap = lambda cb: (lambda i: (jnp.minimum(i, nlx - 1), cb))
    xblk = pl.BlockSpec((RT, D), xmap)
    return pl.pallas_call(
        kern, out_shape=(S((T, NP), bf16), S((L, D), bf16), S((L, D), bf16)), grid=(nt,),
        in_specs=[xblk, xblk, xblk, pl.BlockSpec((RT, D), pmap(G10 // D)), pl.BlockSpec((RT, D), pmap(G20 // D))],
        out_specs=(pl.BlockSpec((RT, 2 * D), lambda i: (i, G10 // (2 * D))), xblk, xblk),
        compiler_params=_params("arbitrary"), name="merge_bwd")(dmerged, bs, bc, proj, proj)


def _final(x, out, target, mod, fw):
    L = x.shape[0]

    def kern(x_ref, o_ref, t_ref, mod_ref, fw_ref, dx1_ref, dout_ref, loss_ref, dfw_ref, dg_ref):
        i = pl.program_id(0)

        @pl.when(i == 0)
        def _():
            loss_ref[...] = jnp.zeros_like(loss_ref)
            dfw_ref[...] = jnp.zeros_like(dfw_ref)
            dg_ref[...] = jnp.zeros_like(dg_ref)

        gate = mod_ref[0:1, 2 * D:3 * D]
        ov = o_ref[...]
        x1 = x_ref[...] + gate * ov
        r = lax.rsqrt(jnp.mean(x1 * x1, axis=1, keepdims=True) + EPS)
        xn = x1 * r
        fw = fw_ref[...]
        err = xn * fw - t_ref[...]
        part = 0.5 * jnp.sum(jnp.mean(err * err, axis=1, keepdims=True), axis=0, keepdims=True)
        loss_ref[...] += jnp.broadcast_to(part, (8, 128))
        dy = err * (1.0 / D)
        dfw_ref[...] += jnp.sum(dy * xn, axis=0, keepdims=True)
        dyw = dy * fw
        dx1 = r * (dyw - xn * jnp.mean(dyw * xn, axis=1, keepdims=True))
        dx1_ref[...] = dx1
        dout_ref[...] = (gate * dx1).astype(bf16)
        dg_ref[...] += jnp.sum(dx1 * ov, axis=0, keepdims=True)

    blk = pl.BlockSpec((RT, D), lambda i: (i, 0))
    return pl.pallas_call(
        kern, out_shape=(S((L, D), f32), S((L, D), bf16), S((8, 128), f32), S((1, D), f32), S((1, D), f32)), grid=(L // RT,),
        in_specs=[blk, blk, blk, _full((8, 3 * D)), _full((1, D))],
        out_specs=(blk, blk, _full((8, 128)), _full((1, D)), _full((1, D))),
        compiler_params=_params("arbitrary"), name="final")(x, out, target, mod, fw)


def _me():
    return 4 * lax.axis_index("x") + 2 * lax.axis_index("y") + lax.axis_index("c")


def _xchg_copy(ins, outs, send_sems, recv_sems, modes, a, k, me):
    peer = lax.rem(me + k, N_DEV)
    pid = (peer // 4, lax.rem(peer // 2, 2), lax.rem(peer, 2))
    src = ins[a].at[peer] if modes[a] else ins[a]
    return pltpu.make_async_remote_copy(src_ref=src, dst_ref=outs[a].at[me], send_sem=send_sems.at[a, k - 1],
                                        recv_sem=recv_sems.at[a, k - 1], device_id=pid, device_id_type=MESH)


def _xchg_local(ins, outs, loc_sems, modes, a, me):
    return pltpu.make_async_copy(ins[a].at[me] if modes[a] else ins[a], outs[a].at[me], loc_sems.at[a])


def _xchg_start(ins, outs, send_sems, recv_sems, loc_sems, modes):
    me = _me()
    for a in range(len(modes)):
        _xchg_local(ins, outs, loc_sems, modes, a, me).start()
        for k in range(1, N_DEV):
            _xchg_copy(ins, outs, send_sems, recv_sems, modes, a, k, me).start()


def _xchg_wait(ins, outs, send_sems, recv_sems, loc_sems, modes):
    me = _me()
    for a in range(len(modes)):
        for k in range(1, N_DEV):
            frm = lax.rem(me + N_DEV - k, N_DEV)
            src = ins[a].at[frm] if modes[a] else ins[a]
            pltpu.make_async_remote_copy(src_ref=src, dst_ref=outs[a].at[frm], send_sem=send_sems.at[a, k - 1],
                                         recv_sem=recv_sems.at[a, k - 1], device_id=(0, 0, 0), device_id_type=MESH).wait_recv()
    for a in range(len(modes)):
        for k in range(1, N_DEV):
            _xchg_copy(ins, outs, send_sems, recv_sems, modes, a, k, me).wait_send()
        _xchg_local(ins, outs, loc_sems, modes, a, me).wait()


def _xchg_out_shapes(arrs, modes):
    return tuple(S((N_DEV,) + (a.shape[1:] if sc else a.shape), a.dtype) for a, sc in zip(arrs, modes))


def _xchg_sems(n):
    return [pltpu.SemaphoreType.DMA((n, N_DEV - 1)), pltpu.SemaphoreType.DMA((n, N_DEV - 1)), pltpu.SemaphoreType.DMA((n,))]


def _exchange(arrs, modes, name):
    n = len(arrs)

    def kern(*refs):
        ins, outs, sems = refs[:n], refs[n:2 * n], refs[2 * n:]
        _xchg_start(ins, outs, *sems, modes)
        _xchg_wait(ins, outs, *sems, modes)

    anyspec = pl.BlockSpec(memory_space=pl.ANY)
    return pl.pallas_call(
        kern, out_shape=_xchg_out_shapes(arrs, modes), in_specs=[anyspec] * n, out_specs=tuple([anyspec] * n),
        scratch_shapes=_xchg_sems(n), name=name)(*arrs)


def _gather2(arrs, name):
    n = len(arrs)

    def kern(*refs):
        ins, outs = refs[:n], refs[n:2 * n]
        send_sems, recv_sems, loc_sems = refs[2 * n:]
        x, y, c = lax.axis_index("x"), lax.axis_index("y"), lax.axis_index("c")
        me, sib = (x, y, c), (x, y, 1 - c)
        chips = [(1 - x, y), (x, 1 - y), (1 - x, 1 - y)]

        def slot(a, p):
            return outs[a].at[4 * p[0] + 2 * p[1] + p[2]]

        def cp(a, k, block, to, own=False):
            return pltpu.make_async_remote_copy(src_ref=ins[a] if own else slot(a, block), dst_ref=slot(a, block),
                                                send_sem=send_sems.at[a, k], recv_sem=recv_sems.at[a, k],
                                                device_id=to, device_id_type=MESH)

        started = []
        for a in range(n):
            loc = pltpu.make_async_copy(ins[a], slot(a, me), loc_sems.at[a])
            loc.start()
            started.append(cp(a, 0, me, sib, own=True))
            started += [cp(a, 1 + j, me, (*chip, c), own=True) for j, chip in enumerate(chips)]
        for s in started:
            s.start()
        for j, chip in enumerate(chips):
            for a in range(n):
                cp(a, 1 + j, (*chip, c), me).wait_recv()
                fwd = cp(a, 4 + j, (*chip, c), sib)
                fwd.start()
                started.append(fwd)
        for a in range(n):
            cp(a, 0, sib, me).wait_recv()
            for j, chip in enumerate(chips):
                cp(a, 4 + j, (*chip, 1 - c), me).wait_recv()
        for s in started:
            s.wait_send()
        for a in range(n):
            pltpu.make_async_copy(ins[a], slot(a, me), loc_sems.at[a]).wait()

    anyspec = pl.BlockSpec(memory_space=pl.ANY)
    return pl.pallas_call(
        kern, out_shape=_xchg_out_shapes(arrs, (False,) * n), in_specs=[anyspec] * n, out_specs=tuple([anyspec] * n),
        scratch_shapes=[pltpu.SemaphoreType.DMA((n, 7)), pltpu.SemaphoreType.DMA((n, 7)), pltpu.SemaphoreType.DMA((n,))],
        name=name)(*arrs)


def _adamw(parts, w, m, v, name):
    r, c = w.shape
    tr = r
    for cand in (128, 64, 32, 16, 8):
        if r % cand == 0 and r > cand:
            tr = cand
            break
    c1 = 1.0 / (1.0 - ADAM_B1 ** ADAM_STEP)
    c2 = 1.0 / (1.0 - ADAM_B2 ** ADAM_STEP)

    def kern(p_ref, w_ref, m_ref, v_ref, g_ref, d_ref, m2_ref, v2_ref):
        g = p_ref[0].astype(f32)
        for i in range(1, N_DEV):
            g = g + p_ref[i].astype(f32)
        g_ref[...] = g
        m2 = ADAM_B1 * m_ref[...] + (1.0 - ADAM_B1) * g
        v2 = ADAM_B2 * v_ref[...] + (1.0 - ADAM_B2) * (g * g)
        m2_ref[...] = m2
        v2_ref[...] = v2
        d_ref[...] = -ADAM_LR * ((m2 * c1) / (jnp.sqrt(v2 * c2) + ADAM_EPS) + ADAM_WD * w_ref[...])

    blk = pl.BlockSpec((tr, c), lambda i: (i, 0))
    sh = S((r, c), f32)
    return pl.pallas_call(
        kern, out_shape=(sh, sh, sh, sh), grid=(r // tr,),
        in_specs=[pl.BlockSpec((N_DEV, tr, c), lambda i: (0, i, 0)), blk, blk, blk], out_specs=(blk, blk, blk, blk),
        compiler_params=_params("parallel"), name=name)(parts, w, m, v)


_SMALL = (("c_ctx", 1024), ("b_mod", 3072), ("norm_w", 1024), ("ssm_conv_b", 4096), ("dt_bias", 64), ("a_log", 64),
          ("d_skip", 32), ("ssm_norm_w", 2048), ("conf_conv_b", 1024), ("conf_ln_w", 1024), ("conf_ln_b", 1024),
          ("final_norm_w", 1024))
SMALL_TILE = 8 * 128


def _pack_small(d):
    rows = []
    for name, n in _SMALL:
        v = d[name].reshape(-1).astype(f32)
        pad = (-n) % SMALL_TILE
        if pad:
            v = jnp.concatenate([v, jnp.zeros((pad,), f32)])
        rows.append(v.reshape(-1, 128))
    return jnp.concatenate(rows, axis=0)


def _unpack_small(p, shapes):
    out, r0 = {}, 0
    for name, n in _SMALL:
        nr = 8 * ((n + SMALL_TILE - 1) // SMALL_TILE)
        out[name] = p[r0:r0 + nr].reshape(-1)[:n].reshape(shapes[name])
        r0 += nr
    return out


def _permute_w_in(w):
    return jnp.concatenate([w[:, 0:4096], w[:, 4160:6208], w[:, 9280:11328], w[:, 4096:4160],
                            jnp.zeros((w.shape[0], 1024 - 64), w.dtype), w[:, 6208:9280]], axis=1)


def _unpermute_w_in(wp):
    return jnp.concatenate([wp[:, 0:4096], wp[:, DT0:DT0 + 64], wp[:, Z0:Z0 + 2048], wp[:, GV0:GV0 + 3072],
                            wp[:, G10:G10 + 2048]], axis=1)


def _cols_gathered(g):
    return jnp.transpose(g, (1, 0, 2)).reshape(g.shape[1], N_DEV * g.shape[2])


def _cols_to_blocks(a):
    r, c8 = a.shape
    return jnp.transpose(a.reshape(r, N_DEV, c8 // N_DEV), (1, 0, 2))


def kernel(x, c, ctx, c_ctx, w_mod, b_mod, norm_w, w_in, ssm_conv_w, ssm_conv_b, dt_bias, a_log, d_skip, ssm_norm_w, w_out_ssm, conf_conv_w, conf_conv_b, conf_ln_w, conf_ln_b, w_out_conf, w_out, final_norm_w, loss_target, m_c_ctx, m_w_mod, m_b_mod, m_norm_w, m_w_in, m_ssm_conv_w, m_ssm_conv_b, m_dt_bias, m_a_log, m_d_skip, m_ssm_norm_w, m_w_out_ssm, m_conf_conv_w, m_conf_conv_b, m_conf_ln_w, m_conf_ln_b, m_w_out_conf, m_w_out, m_final_norm_w, v_c_ctx, v_w_mod, v_b_mod, v_norm_w, v_w_in, v_ssm_conv_w, v_ssm_conv_b, v_dt_bias, v_a_log, v_d_skip, v_ssm_norm_w, v_w_out_ssm, v_conf_conv_w, v_conf_conv_b, v_conf_ln_w, v_conf_ln_b, v_w_out_conf, v_w_out, v_final_norm_w):
    L = x.shape[1]
    Lc = ctx.shape[1]
    T = L + Lc
    nlx = L // RT
    nxc = L // Q
    x2 = x.reshape(L, D)
    ctx2 = ctx.reshape(Lc, D)
    tgt = loss_target.reshape(L, D)

    gathered = _gather2([w_in[0].astype(bf16), w_mod[0].astype(bf16), ssm_conv_w[0], conf_conv_w[0]], name="gather_weights")
    wp = _permute_w_in(_cols_gathered(gathered[0]))
    wmod_bf = _cols_gathered(gathered[1])
    scw8 = jnp.concatenate([_cols_gathered(gathered[2]), jnp.zeros((8 - SK, 4096), f32)], axis=0)
    ccw32 = jnp.concatenate([_cols_gathered(gathered[3]), jnp.zeros((32 - CK, D), f32)], axis=0)

    norm_w1 = norm_w.reshape(1, D)
    scb = ssm_conv_b.reshape(1, 4096)
    bias_row = jnp.concatenate([dt_bias.reshape(1, 2 * NH), jnp.zeros((1, 128 - 2 * NH), f32)], axis=1)
    alog_row = jnp.concatenate([a_log.reshape(1, 2 * NH), jnp.zeros((1, 128 - 2 * NH), f32)], axis=1)
    dskip_full = jnp.repeat(d_skip.reshape(NH), HD).reshape(1, DI)
    snw = ssm_norm_w.reshape(1, DI)
    ccb = conf_conv_b.reshape(1, D)
    lnw = conf_ln_w.reshape(1, D)
    lnb = conf_ln_b.reshape(1, D)
    fw = final_norm_w.reshape(1, D)

    cc8 = jnp.concatenate([c.reshape(1, D), c_ctx.reshape(1, D), jnp.zeros((6, D), f32)], axis=0)
    mod = _mod_fwd(cc8, wmod_bf, b_mod.reshape(1, 3 * D))
    h, h_t = _prenorm(x2, ctx2, norm_w1, mod)
    proj, wos_g, woc_g, wo_g = _matmul(
        h, wp, f32, "proj_gather", tn=2048,
        comm=([w_out_ssm[0].astype(bf16), w_out_conf[0].astype(bf16), w_out[0].astype(bf16)], (False,) * 3))
    wos_bf = wos_g.reshape(DI, D)
    woc_bf = woc_g.reshape(D, D)
    wo_bf = wo_g.reshape(D, D)
    xbc = _ssm_conv_fwd(proj, scw8, scb, nlx)
    dt, la = _dt_prep(proj, bias_row, alog_row)
    consts_f, consts_b = _scan_consts(False), _scan_consts(True)
    yf, hp_f = _ssd_fwd(xbc, dt, la, consts_f, False, nxc, "ssd_fwd_f")
    ysum, hp_b = _ssd_fwd(xbc, dt, la, consts_b, True, nxc, "ssd_fwd_b", y_acc=yf)
    y, yn, yn_t = _ynorm_fwd(ysum, xbc, proj, dskip_full, snw, L)
    bs = _matmul(yn, wos_bf, f32, "branch_ssm", tm=1024, tk=2048)
    u1, u3, u3_t = _conf_fwd(proj, ccw32, ccb, lnw, lnb, L)
    bc = _matmul(u3, woc_bf, f32, "branch_conf", tm=2048)
    merged, merged_t = _merge_fwd(bs, bc, proj)
    out = _matmul(merged, wo_bf, f32, "out_proj", tm=2048)
    dx1, dout, loss_acc, dfw, dgate = _final(x2, out, tgt, mod, fw)

    dmerged = _matmul(dout, wo_bf, f32, "d_merged", tb=True, tm=2048)
    g_wo = _matmul(merged_t, dout, bf16, "g_w_out", tm=1024, tk=2048)
    dproj, dbs, dbc = _merge_bwd(dmerged, bs, bc, proj)
    dyn = _matmul(dbs, wos_bf, f32, "d_yn", tb=True, tm=1024, tn=2048)
    g_wos = _matmul(yn_t, dbs, bf16, "g_w_out_ssm", tm=1024, tk=2048)
    du3 = _matmul(dbc, woc_bf, f32, "d_u3", tb=True, tm=2048)
    g_woc = _matmul(u3_t, dbc, bf16, "g_w_out_conf", tm=1024, tk=2048)
    dproj, g_ccw, g_ccb, g_lnw, g_lnb = _conf_bwd(du3, u1, proj, ccw32, lnw, lnb, dproj)
    dproj, dy, g_snw, dsk_cols = _ynorm_bwd(dyn, y, xbc, proj, dskip_full, snw, dproj)
    acc_f = _ssd_bwd(xbc, dy, dt, la, hp_f, dskip_full, consts_f, False, nxc, "ssd_bwd_f")
    dxbc, a1, a2, r2, sv = _ssd_bwd(xbc, dy, dt, la, hp_b, dskip_full, consts_b, True, nxc, "ssd_bwd_b", acc=acc_f)
    dproj, g_dtb, g_alog = _dt_bwd(a1, a2, r2, sv, dt, la, proj, bias_row, alog_row, dproj)
    dpre, g_scw, g_scb = _ssm_conv_dpre(dxbc, proj, scw8, scb, nlx)
    dproj = _ssm_conv_t(dpre, scw8, dproj, nlx)
    g_wp, *parts_b = _matmul(
        h_t, dproj, bf16, "g_w_in_scatter", tm=1024, tn=2048,
        comm=([g_wos.reshape(N_DEV, DI // N_DEV, D), g_woc.reshape(N_DEV, D // N_DEV, D), g_wo.reshape(N_DEV, D // N_DEV, D),
               _cols_to_blocks(g_scw[:SK]), _cols_to_blocks(g_ccw[:CK])], (True,) * 5))
    dh, parts_a = _matmul(dproj, wp, f32, "d_h_scatter", tb=True, tk=2048,
                          comm=([_cols_to_blocks(_unpermute_w_in(g_wp))], (True,)))
    parts = [parts_a] + parts_b
    gx, g_nw, macc = _prenorm_bwd(x2, ctx2, dh, dx1, norm_w1, mod)
    z1 = jnp.zeros((1, D), f32)
    dmod8 = jnp.concatenate([jnp.concatenate([macc[0:1], macc[1:2], dgate], axis=1),
                             jnp.concatenate([macc[2:3], macc[3:4], z1], axis=1), jnp.zeros((6, 3 * D), f32)], axis=0)
    ct = jnp.concatenate([c.reshape(D, 1), c_ctx.reshape(D, 1), jnp.zeros((D, 126), f32)], axis=1)
    g_wmod, g_bmod, g_cctx = _mod_bwd(ct, dmod8, wmod_bf)
    g_dskip = _head_sums(dsk_cols.reshape(NH, HD))[:, 0]

    small_g = _pack_small({
        "c_ctx": g_cctx[:, 0], "b_mod": g_bmod, "norm_w": g_nw, "ssm_conv_b": g_scb, "dt_bias": g_dtb[0, :2 * NH],
        "a_log": g_alog[0, :2 * NH], "d_skip": g_dskip, "ssm_norm_w": g_snw, "conf_conv_b": g_ccb, "conf_ln_w": g_lnw,
        "conf_ln_b": g_lnb, "final_norm_w": dfw})
    wmod_parts, small_parts = _exchange([_cols_to_blocks(g_wmod), small_g], (True, False), name="exchange_tail")
    parts = [parts[0], wmod_parts] + parts[1:]

    given = dict(c_ctx=c_ctx, w_mod=w_mod, b_mod=b_mod, norm_w=norm_w, w_in=w_in, ssm_conv_w=ssm_conv_w, ssm_conv_b=ssm_conv_b,
                 dt_bias=dt_bias, a_log=a_log, d_skip=d_skip, ssm_norm_w=ssm_norm_w, w_out_ssm=w_out_ssm, conf_conv_w=conf_conv_w,
                 conf_conv_b=conf_conv_b, conf_ln_w=conf_ln_w, conf_ln_b=conf_ln_b, w_out_conf=w_out_conf, w_out=w_out,
                 final_norm_w=final_norm_w)
    ms = dict(c_ctx=m_c_ctx, w_mod=m_w_mod, b_mod=m_b_mod, norm_w=m_norm_w, w_in=m_w_in, ssm_conv_w=m_ssm_conv_w,
              ssm_conv_b=m_ssm_conv_b, dt_bias=m_dt_bias, a_log=m_a_log, d_skip=m_d_skip, ssm_norm_w=m_ssm_norm_w,
              w_out_ssm=m_w_out_ssm, conf_conv_w=m_conf_conv_w, conf_conv_b=m_conf_conv_b, conf_ln_w=m_conf_ln_w,
              conf_ln_b=m_conf_ln_b, w_out_conf=m_w_out_conf, w_out=m_w_out, final_norm_w=m_final_norm_w)
    vs = dict(c_ctx=v_c_ctx, w_mod=v_w_mod, b_mod=v_b_mod, norm_w=v_norm_w, w_in=v_w_in, ssm_conv_w=v_ssm_conv_w,
              ssm_conv_b=v_ssm_conv_b, dt_bias=v_dt_bias, a_log=v_a_log, d_skip=v_d_skip, ssm_norm_w=v_ssm_norm_w,
              w_out_ssm=v_w_out_ssm, conf_conv_w=v_conf_conv_w, conf_conv_b=v_conf_conv_b, conf_ln_w=v_conf_ln_w,
              conf_ln_b=v_conf_ln_b, w_out_conf=v_w_out_conf, w_out=v_w_out, final_norm_w=v_final_norm_w)
    grads, deltas, new_m, new_v = {}, {}, {}, {}
    sharded = ("w_in", "w_mod", "w_out_ssm", "w_out_conf", "w_out", "ssm_conv_w", "conf_conv_w")
    for i, nm in enumerate(sharded):
        shp = given[nm].shape
        w2 = given[nm].reshape(shp[1], shp[2])
        res = _adamw(parts[i], w2, ms[nm].reshape(w2.shape), vs[nm].reshape(w2.shape), "adamw_" + nm)
        grads[nm], deltas[nm], new_m[nm], new_v[nm] = [r.reshape(shp) for r in res]
    shapes = {nm: given[nm].shape for nm, _ in _SMALL}
    res = _adamw(small_parts, _pack_small(given), _pack_small(ms), _pack_small(vs), "adamw_small")
    for dst, packed in zip((grads, deltas, new_m, new_v), res):
        dst.update(_unpack_small(packed, shapes))

    loss = lax.psum(loss_acc[0, 0], ("x", "y", "c"))
    order = ("c_ctx", "w_mod", "b_mod", "norm_w", "w_in", "ssm_conv_w", "ssm_conv_b", "dt_bias", "a_log", "d_skip", "ssm_norm_w",
             "w_out_ssm", "conf_conv_w", "conf_conv_b", "conf_ln_w", "conf_ln_b", "w_out_conf", "w_out", "final_norm_w")
    return (loss, gx.reshape(1, L, D), *[grads[n] for n in order], *[deltas[n] for n in order],
            *[new_m[n] for n in order], *[new_v[n] for n in order])
```

```python
import jax
import jax.numpy as jnp
from jax import lax
from jax.experimental import pallas as pl
from jax.experimental.pallas import tpu as pltpu

f32 = jnp.float32
bf16 = jnp.bfloat16

D = 1024
DI = 2048
NG = 8
HPG = 4
HD = 64
NS = 128
NH = 32
Q = 128
GRID_W = 64
CK = 31
SK = 4
EPS = 1e-6
RT = 256
N_DEV = 8
IN_COLS = 11328
G10, G20, Z0, PX0, GV0, GG0, CG0, PBC0, DT0, NP = 0, 1024, 2048, 4096, 6144, 7168, 8192, 9216, 11264, 11520
B0, C0 = 2048, 3072
VMEM_LIMIT = 50 * 1024 * 1024
NEG = -1e30

ADAM_LR, ADAM_B1, ADAM_B2, ADAM_EPS, ADAM_WD, ADAM_STEP = 0.001, 0.9, 0.999, 1e-08, 0.01, 10

MESH = pl.DeviceIdType.MESH
S = jax.ShapeDtypeStruct


def _params(*sem):
    return pltpu.CompilerParams(dimension_semantics=tuple(sem) if sem else None, vmem_limit_bytes=VMEM_LIMIT)


def _sig(x):
    return 1.0 / (1.0 + jnp.exp(-x))


def _silu(x):
    return x * _sig(x)


def _dsilu(x, s):
    return s * (1.0 + x * (1.0 - s))


def _dot(a, b):
    return jnp.dot(a, b, preferred_element_type=f32)


def _dot_nt(a, b):
    return lax.dot_general(a, b, (((1,), (1,)), ((), ())), preferred_element_type=f32)


def _dot_tn(a, b):
    return lax.dot_general(a, b, (((0,), (0,)), ((), ())), preferred_element_type=f32)


def _dot3(t_bf, v):
    v1 = v.astype(bf16)
    r1 = v - v1.astype(f32)
    v2 = r1.astype(bf16)
    v3 = (r1 - v2.astype(f32)).astype(bf16)
    return _dot(t_bf, v1) + _dot(t_bf, v2) + _dot(t_bf, v3)


def _pick(n, prefs):
    for p in prefs:
        if n % p == 0:
            return p
    return n


def _full(shape):
    nd = len(shape)
    return pl.BlockSpec(shape, lambda *_: (0,) * nd)


def _matmul(a, b, out_dtype, name, tm=None, tn=None, tk=None, tb=False, comm=None):
    m, k = a.shape
    n = b.shape[0] if tb else b.shape[1]
    tm = tm if tm and m % tm == 0 else _pick(m, (768, 512, 256, 128))
    tn = tn if tn and n % tn == 0 else _pick(n, (1024, 512, 256, 128))
    tk = tk if tk and k % tk == 0 else _pick(k, (1024, 768, 512, 256, 128))
    nk = k // tk
    gi, gj = m // tm, n // tn
    carrs, modes = comm if comm else ((), ())
    nc = len(carrs)

    def kern(*refs):
        a_ref, b_ref = refs[:2]
        cins = refs[2:2 + nc]
        o_ref = refs[2 + nc]
        couts = refs[3 + nc:3 + 2 * nc]
        acc_ref = refs[3 + 2 * nc]
        sems = refs[4 + 2 * nc:]
        i, j, kk = pl.program_id(0), pl.program_id(1), pl.program_id(2)
        if nc:
            @pl.when(jnp.logical_and(jnp.logical_and(i == 0, j == 0), kk == 0))
            def _():
                _xchg_start(cins, couts, *sems, modes)

        part = _dot_nt(a_ref[...], b_ref[...]) if tb else _dot(a_ref[...], b_ref[...])
        if nk == 1:
            o_ref[...] = part.astype(o_ref.dtype)
        else:
            @pl.when(kk == 0)
            def _():
                acc_ref[...] = part

            @pl.when(kk > 0)
            def _():
                acc_ref[...] += part

            @pl.when(kk == nk - 1)
            def _():
                o_ref[...] = acc_ref[...].astype(o_ref.dtype)

        if nc:
            @pl.when(jnp.logical_and(jnp.logical_and(i == gi - 1, j == gj - 1), kk == nk - 1))
            def _():
                _xchg_wait(cins, couts, *sems, modes)

    anyspec = pl.BlockSpec(memory_space=pl.ANY)
    bspec = pl.BlockSpec((tn, tk), lambda i, j, kk: (j, kk)) if tb else pl.BlockSpec((tk, tn), lambda i, j, kk: (kk, j))
    out_shape = (S((m, n), out_dtype),) + _xchg_out_shapes(carrs, modes)
    res = pl.pallas_call(
        kern, out_shape=out_shape, grid=(gi, gj, nk),
        in_specs=[pl.BlockSpec((tm, tk), lambda i, j, kk: (i, kk)), bspec] + [anyspec] * nc,
        out_specs=(pl.BlockSpec((tm, tn), lambda i, j, kk: (i, j)),) + (anyspec,) * nc,
        scratch_shapes=[pltpu.VMEM((tm, tn), f32)] + (_xchg_sems(nc) if nc else []),
        compiler_params=_params(*((("arbitrary",) * 3) if nc else ("parallel", "parallel", "arbitrary"))), name=name)(a, b, *carrs)
    return res if nc else res[0]


def _mod_fwd(cc8, w_mod_bf, b_mod):
    def kern(c_ref, w_ref, b_ref, o_ref):
        o_ref[...] = _dot(_silu(c_ref[...]).astype(bf16), w_ref[...]) + b_ref[...]

    return pl.pallas_call(kern, out_shape=S((8, 3 * D), f32), compiler_params=_params(), name="mod_fwd")(cc8, w_mod_bf, b_mod)


def _mod_bwd(ct, dmod8, w_mod_bf):
    tc = 512
    nj = 3 * D // tc

    def kern(ct_ref, dm_ref, w_ref, dw_ref, db_ref, dc_ref):
        j = pl.program_id(0)
        c = ct_ref[:, 0:1]
        cx = ct_ref[:, 1:2]
        sx = _sig(cx)
        dmx = dm_ref[0:1, :]
        dmc = dm_ref[1:2, :]
        dw_ref[...] = (_silu(c) * dmx + (cx * sx) * dmc).astype(bf16)
        db_ref[...] = dmx + dmc
        t = jnp.sum(w_ref[...].astype(f32) * dmc.astype(bf16).astype(f32), axis=1, keepdims=True) * _dsilu(cx, sx)

        @pl.when(j == 0)
        def _():
            dc_ref[...] = jnp.zeros_like(dc_ref)

        dc_ref[...] += jnp.broadcast_to(t, (D, 128))

    return pl.pallas_call(
        kern, out_shape=(S((D, 3 * D), bf16), S((1, 3 * D), f32), S((D, 128), f32)), grid=(nj,),
        in_specs=[_full((D, 128)), pl.BlockSpec((8, tc), lambda j: (0, j)), pl.BlockSpec((D, tc), lambda j: (0, j))],
        out_specs=(pl.BlockSpec((D, tc), lambda j: (0, j)), pl.BlockSpec((1, tc), lambda j: (0, j)), _full((D, 128))),
        compiler_params=_params("arbitrary"), name="mod_bwd")(ct, dmod8, w_mod_bf)


def _prenorm(x, ctx, norm_w, mod):
    L, Lc = x.shape[0], ctx.shape[0]
    nlx, nt = L // RT, (L + Lc) // RT

    def kern(x_ref, c_ref, nw_ref, mod_ref, h_ref, ht_ref):
        i = pl.program_id(0)
        is_c = i >= nlx
        xv = jnp.where(is_c, c_ref[...], x_ref[...])
        shift = jnp.where(is_c, mod_ref[1:2, 0:D], mod_ref[0:1, 0:D])
        scale = jnp.where(is_c, mod_ref[1:2, D:2 * D], mod_ref[0:1, D:2 * D])
        r = lax.rsqrt(jnp.mean(xv * xv, axis=1, keepdims=True) + EPS)
        hv = (xv * r) * nw_ref[...] * (1.0 + scale) + shift
        h_ref[...] = hv.astype(bf16)
        ht_ref[...] = jnp.transpose(hv).astype(bf16)

    return pl.pallas_call(
        kern, out_shape=(S((L + Lc, D), bf16), S((D, L + Lc), bf16)), grid=(nt,),
        in_specs=[pl.BlockSpec((RT, D), lambda i: (jnp.minimum(i, nlx - 1), 0)),
                  pl.BlockSpec((RT, D), lambda i: (jnp.maximum(i - nlx, 0), 0)),
                  _full((1, D)), _full((8, 3 * D))],
        out_specs=(pl.BlockSpec((RT, D), lambda i: (i, 0)), pl.BlockSpec((D, RT), lambda i: (0, i))),
        compiler_params=_params("parallel"), name="prenorm")(x, ctx, norm_w, mod)


def _prenorm_bwd(x, ctx, dh, dx1, norm_w, mod):
    L, Lc = x.shape[0], ctx.shape[0]
    nlx, nt = L // RT, (L + Lc) // RT

    def kern(x_ref, c_ref, dh_ref, dx1_ref, nw_ref, mod_ref, gx_ref, dnw_ref, acc_ref):
        i = pl.program_id(0)
        is_c = i >= nlx

        @pl.when(i == 0)
        def _():
            dnw_ref[...] = jnp.zeros_like(dnw_ref)
            acc_ref[...] = jnp.zeros_like(acc_ref)

        xv = jnp.where(is_c, c_ref[...], x_ref[...])
        scale = jnp.where(is_c, mod_ref[1:2, D:2 * D], mod_ref[0:1, D:2 * D])
        nw = nw_ref[...]
        r = lax.rsqrt(jnp.mean(xv * xv, axis=1, keepdims=True) + EPS)
        xn = xv * r
        dh = dh_ref[...]
        dsh = jnp.sum(dh, axis=0, keepdims=True)
        dsc = jnp.sum(dh * (xn * nw), axis=0, keepdims=True)
        dxnw = dh * (1.0 + scale)
        dnw_ref[...] += jnp.sum(dxnw * xn, axis=0, keepdims=True)
        dxn = dxnw * nw
        dx = r * (dxn - xn * jnp.mean(dxn * xn, axis=1, keepdims=True))

        @pl.when(jnp.logical_not(is_c))
        def _():
            gx_ref[...] = dx1_ref[...] + dx
            acc_ref[0:1, :] += dsh
            acc_ref[1:2, :] += dsc

        @pl.when(is_c)
        def _():
            acc_ref[2:3, :] += dsh
            acc_ref[3:4, :] += dsc

    xmap = lambda i: (jnp.minimum(i, nlx - 1), 0)
    return pl.pallas_call(
        kern, out_shape=(S((L, D), f32), S((1, D), f32), S((8, D), f32)), grid=(nt,),
        in_specs=[pl.BlockSpec((RT, D), xmap), pl.BlockSpec((RT, D), lambda i: (jnp.maximum(i - nlx, 0), 0)),
                  pl.BlockSpec((RT, D), lambda i: (i, 0)), pl.BlockSpec((RT, D), xmap), _full((1, D)), _full((8, 3 * D))],
        out_specs=(pl.BlockSpec((RT, D), xmap), _full((1, D)), _full((8, D))),
        compiler_params=_params("arbitrary"), name="prenorm_bwd")(x, ctx, dh, dx1, norm_w, mod)


def _xbc_col(j):
    return jnp.where(j < 2, PX0 // 1024 + j, PBC0 // 1024 + j - 2)


def _halo_specs(nt_rows, ct, col=lambda j: j):
    cur = pl.BlockSpec((RT, ct), lambda i, j: (i, col(j)))
    prev = pl.BlockSpec((8, ct), lambda i, j: (jnp.maximum(i * (RT // 8) - 1, 0), col(j)))
    nxt = pl.BlockSpec((8, ct), lambda i, j: (jnp.minimum((i + 1) * (RT // 8), nt_rows // 8 - 1), col(j)))
    return cur, prev, nxt


def _fill_halo(scr, cur_ref, prev_ref, next_ref, i, nlx, nt):
    prev_ok = jnp.logical_and(i != 0, i != nlx)
    next_ok = jnp.logical_and(i != nlx - 1, i != nt - 1)
    scr[0:8, :] = jnp.where(prev_ok, prev_ref[...], 0.0)
    scr[8:8 + RT, :] = cur_ref[...]
    scr[8 + RT:16 + RT, :] = jnp.where(next_ok, next_ref[...], 0.0)


CONV_RB = 32


def _conv_blocks(ct):
    return [(slice(cb * 128, (cb + 1) * 128), r0) for cb in range(ct // 128) for r0 in range(0, RT, CONV_RB)]


def _ssm_conv_fwd(proj, w8, b, nlx):
    T = proj.shape[0]
    nt = T // RT
    ct = 1024
    cur, prev, nxt = _halo_specs(T, ct, _xbc_col)

    def kern(cur_ref, prev_ref, next_ref, w_ref, b_ref, o_ref, scr):
        i = pl.program_id(0)
        _fill_halo(scr, cur_ref, prev_ref, next_ref, i, nlx, nt)
        for cs, r0 in _conv_blocks(ct):
            acc = jnp.broadcast_to(b_ref[:, cs], (CONV_RB, 128))
            for k in range(SK):
                acc = acc + w_ref[k:k + 1, cs] * scr[pl.ds(6 + k + r0, CONV_RB), cs]
            o_ref[r0:r0 + CONV_RB, cs] = _silu(acc)

    return pl.pallas_call(
        kern, out_shape=S((T, 4096), f32), grid=(nt, 4096 // ct),
        in_specs=[cur, prev, nxt, pl.BlockSpec((8, ct), lambda i, j: (0, j)), pl.BlockSpec((1, ct), lambda i, j: (0, j))],
        out_specs=pl.BlockSpec((RT, ct), lambda i, j: (i, j)),
        scratch_shapes=[pltpu.VMEM((RT + 16, ct), f32)],
        compiler_params=_params("parallel", "parallel"), name="ssm_conv_fwd")(proj, proj, proj, w8, b)


def _ssm_conv_dpre(dxbc, proj, w8, b, nlx):
    T = proj.shape[0]
    nt = T // RT
    ct = 1024
    cur = pl.BlockSpec((RT, ct), lambda j, i: (i, j))
    pcur = pl.BlockSpec((RT, ct), lambda j, i: (i, _xbc_col(j)))
    prev = pl.BlockSpec((8, ct), lambda j, i: (jnp.maximum(i * (RT // 8) - 1, 0), _xbc_col(j)))
    nxt = pl.BlockSpec((8, ct), lambda j, i: (jnp.minimum((i + 1) * (RT // 8), T // 8 - 1), _xbc_col(j)))

    def kern(d_ref, cur_ref, prev_ref, next_ref, w_ref, b_ref, dpre_ref, dw_ref, db_ref, scr):
        i = pl.program_id(1)
        _fill_halo(scr, cur_ref, prev_ref, next_ref, i, nlx, nt)

        @pl.when(i == 0)
        def _():
            dw_ref[...] = jnp.zeros_like(dw_ref)
            db_ref[...] = jnp.zeros_like(db_ref)

        for cb in range(ct // 128):
            cs = slice(cb * 128, (cb + 1) * 128)
            db_acc = jnp.zeros((CONV_RB, 128), f32)
            dw_acc = [jnp.zeros((CONV_RB, 128), f32) for _ in range(SK)]
            for r0 in range(0, RT, CONV_RB):
                taps = [scr[pl.ds(6 + k + r0, CONV_RB), cs] for k in range(SK)]
                pre = jnp.broadcast_to(b_ref[:, cs], (CONV_RB, 128))
                for k in range(SK):
                    pre = pre + w_ref[k:k + 1, cs] * taps[k]
                dpre = d_ref[r0:r0 + CONV_RB, cs] * _dsilu(pre, _sig(pre))
                dpre_ref[r0:r0 + CONV_RB, cs] = dpre
                db_acc = db_acc + dpre
                dw_acc = [dw_acc[k] + dpre * taps[k] for k in range(SK)]
            db_ref[:, cs] += jnp.sum(db_acc, axis=0, keepdims=True)
            for k in range(SK):
                dw_ref[k:k + 1, cs] += jnp.sum(dw_acc[k], axis=0, keepdims=True)

    return pl.pallas_call(
        kern, out_shape=(S((T, 4096), f32), S((8, 4096), f32), S((1, 4096), f32)), grid=(4096 // ct, nt),
        in_specs=[cur, pcur, prev, nxt, pl.BlockSpec((8, ct), lambda j, i: (0, j)), pl.BlockSpec((1, ct), lambda j, i: (0, j))],
        out_specs=(cur, pl.BlockSpec((8, ct), lambda j, i: (0, j)), pl.BlockSpec((1, ct), lambda j, i: (0, j))),
        scratch_shapes=[pltpu.VMEM((RT + 16, ct), f32)],
        compiler_params=_params("parallel", "arbitrary"), name="ssm_conv_dpre")(dxbc, proj, proj, proj, w8, b)


def _ssm_conv_t(dpre, w8, dproj, nlx):
    T = dpre.shape[0]
    nt = T // RT
    ct = 1024
    cur, prev, nxt = _halo_specs(T, ct)

    def kern(cur_ref, prev_ref, next_ref, w_ref, _alias, o_ref, scr):
        i = pl.program_id(0)
        _fill_halo(scr, cur_ref, prev_ref, next_ref, i, nlx, nt)
        for cs, r0 in _conv_blocks(ct):
            acc = jnp.zeros((CONV_RB, 128), f32)
            for k in range(SK):
                acc = acc + w_ref[k:k + 1, cs] * scr[pl.ds(10 - k + r0, CONV_RB), cs]
            o_ref[r0:r0 + CONV_RB, cs] = acc.astype(bf16)

    return pl.pallas_call(
        kern, out_shape=S(dproj.shape, bf16), grid=(nt, 4096 // ct),
        in_specs=[cur, prev, nxt, pl.BlockSpec((8, ct), lambda i, j: (0, j)), pl.BlockSpec(memory_space=pl.ANY)],
        out_specs=pl.BlockSpec((RT, ct), lambda i, j: (i, _xbc_col(j))),
        scratch_shapes=[pltpu.VMEM((RT + 16, ct), f32)], input_output_aliases={4: 0},
        compiler_params=_params("parallel", "parallel"), name="ssm_conv_t")(dpre, dpre, dpre, w8, dproj)


def _tri():
    li = lax.broadcasted_iota(jnp.int32, (Q, Q), 0)
    si = lax.broadcasted_iota(jnp.int32, (Q, Q), 1)
    return (si <= li).astype(bf16), (si >= li).astype(bf16)


def _dt_prep(proj, bias_row, alog_row):
    T = proj.shape[0]
    nch = T // Q

    def kern(raw_ref, b_ref, al_ref, dt_ref, la_ref):
        lane = lax.broadcasted_iota(jnp.int32, (Q, 128), 1)
        v = raw_ref[...] + b_ref[...]
        dt = jnp.maximum(v, 0.0) + jnp.log1p(jnp.exp(-jnp.abs(v)))
        a = jnp.where(lane[0:1, :] < 2 * NH, -jnp.exp(al_ref[...]), 0.0)
        da = dt * a
        tri, trit = _tri()
        dt_ref[...] = dt
        la_ref[...] = jnp.where(lane < NH, _dot3(tri, da), _dot3(trit, da))

    return pl.pallas_call(
        kern, out_shape=(S((T, 128), f32), S((T, 128), f32)), grid=(nch,),
        in_specs=[pl.BlockSpec((Q, 128), lambda c: (c, DT0 // 128)), _full((1, 128)), _full((1, 128))],
        out_specs=(pl.BlockSpec((Q, 128), lambda c: (c, 0)), pl.BlockSpec((Q, 128), lambda c: (c, 0))),
        compiler_params=_params("parallel"), name="dt_prep")(proj, bias_row, alog_row)


def _dt_bwd(a1, a2, r2, sv, dt, la, proj, bias_row, alog_row, dproj):
    T = proj.shape[0]
    nch = T // Q
    blk = pl.BlockSpec((Q, 128), lambda c: (c, 0))

    def kern(a1_ref, a2_ref, r2_ref, s_ref, dt_ref, la_ref, raw_ref, b_ref, al_ref, _alias, o_ref, db_ref, dal_ref):
        c = pl.program_id(0)

        @pl.when(c == 0)
        def _():
            db_ref[...] = jnp.zeros_like(db_ref)
            dal_ref[...] = jnp.zeros_like(dal_ref)

        lane = lax.broadcasted_iota(jnp.int32, (Q, 128), 1)
        row = lax.broadcasted_iota(jnp.int32, (Q, 128), 0)
        fwd = lane < NH
        dt = dt_ref[...]
        la = la_ref[...]
        a2v = a2_ref[...]
        r2v = r2_ref[...]
        a = jnp.where(lane[0:1, :] < 2 * NH, -jnp.exp(al_ref[...]), 0.0)
        la_e = jnp.where(fwd[0:1, :], la[Q - 1:Q, :], la[0:1, :])
        is_end = row == jnp.where(fwd, Q - 1, 0)
        e_end = jnp.exp(la_e - la)
        wend = e_end * dt
        extra = s_ref[0:1, :] * jnp.exp(la_e) + jnp.sum(wend * a2v, axis=0, keepdims=True)
        dla = a1_ref[...] - dt * r2v - wend * a2v + jnp.where(is_end, extra, 0.0)
        tri, trit = _tri()
        rcs = jnp.where(fwd, _dot3(trit, dla), _dot3(tri, dla))
        ddt = r2v + e_end * a2v + a * rcs
        dal_ref[...] += a * jnp.sum(dt * rcs, axis=0, keepdims=True)
        draw = jnp.where(lane < 2 * NH, ddt * _sig(raw_ref[...] + b_ref[...]), 0.0)
        db_ref[...] += jnp.sum(draw, axis=0, keepdims=True)
        o_ref[...] = jnp.zeros_like(o_ref)
        o_ref[:, 0:128] = draw.astype(bf16)

    return pl.pallas_call(
        kern, out_shape=(S(dproj.shape, bf16), S((1, 128), f32), S((1, 128), f32)), grid=(nch,),
        in_specs=[blk, blk, blk, blk, blk, blk, pl.BlockSpec((Q, 128), lambda c: (c, DT0 // 128)),
                  _full((1, 128)), _full((1, 128)), pl.BlockSpec(memory_space=pl.ANY)],
        out_specs=(pl.BlockSpec((Q, NP - DT0), lambda c: (c, DT0 // (NP - DT0))), _full((1, 128)), _full((1, 128))),
        input_output_aliases={9: 0},
        compiler_params=_params("arbitrary"), name="dt_bwd")(a1, a2, r2, sv, dt, la, proj, bias_row, alog_row, dproj)


def _split2(v):
    hi = v.astype(bf16)
    lo = (v - hi.astype(f32)).astype(bf16)
    return jnp.concatenate([hi, lo], axis=1)


def _scan_consts(rev):
    hoff = NH if rev else 0
    g = jnp.arange(NG, dtype=jnp.int32)[:, None, None]

    def rc(nr, ncol):
        return jnp.arange(nr, dtype=jnp.int32)[None, :, None], jnp.arange(ncol, dtype=jnp.int32)[None, None, :]

    r, c = rc(2 * 128, HPG * HD)
    sel_w = (lax.rem(r, 128) == hoff + HPG * g + c // HD).astype(bf16)
    r, c = rc(HPG * HD, 128)
    ind_h = (c == hoff + HPG * g + r // HD).astype(bf16)
    r, c = rc(2 * HPG * Q, 128)
    ind_e = (c == hoff + HPG * g + lax.rem(r, HPG * Q) // Q).astype(bf16)
    return sel_w, ind_h, ind_e


def _masks(rev):
    li = lax.broadcasted_iota(jnp.int32, (Q, Q), 0)
    si = lax.broadcasted_iota(jnp.int32, (Q, Q), 1)
    mask = (li <= si) if rev else (li >= si)
    mask_t = (li >= si) if rev else (li <= si)
    lane = lax.broadcasted_iota(jnp.int32, (Q, HPG * HD), 1)
    hms = [jnp.logical_and(lane >= r * HD, lane < (r + 1) * HD) for r in range(HPG)]
    return mask, mask_t, hms


def _mine(hoff):
    lane = lax.broadcasted_iota(jnp.int32, (Q, 128), 1)
    return jnp.logical_and(lane >= hoff, lane < hoff + NH)


def _head_row(vals, hc0):
    lane = lax.broadcasted_iota(jnp.int32, (1, HPG * HD), 1)
    out = jnp.zeros((1, HPG * HD), f32)
    for r in range(HPG):
        out = jnp.where(jnp.logical_and(lane >= r * HD, lane < (r + 1) * HD), vals[:, hc0 + r:hc0 + r + 1], out)
    return out


def _chunk_of(j, rev, nxc, nch):
    return (nch - 1 - j) if rev else lax.rem(j + nxc, nch)


def _ssd_fwd(xbc, dt, la, consts, rev, nxc, name, y_acc=None):
    T = xbc.shape[0]
    nch = T // Q
    hoff = NH if rev else 0
    e = 0 if rev else Q - 1
    cm = lambda j: _chunk_of(j, rev, nxc, nch)
    sel_w = consts[0]
    has_acc = y_acc is not None

    def kern(*refs):
        xbc_ref, dt_ref, la_ref, sw_ref = refs[:4]
        yacc_ref = refs[4] if has_acc else None
        y_ref, hp_ref, h_ref = refs[4 + has_acc:]
        j = pl.program_id(0)

        @pl.when(j == 0)
        def _():
            h_ref[...] = jnp.zeros_like(h_ref)

        hp_ref[...] = h_ref[...]
        mask, _, hms = _masks(rev)
        la_all = la_ref[...]
        dt_all = dt_ref[...]
        la_t = jnp.transpose(la_all)
        dt_t = jnp.transpose(dt_all)
        la_e = la_all[e:e + 1, :]
        w2 = _split2(jnp.exp(jnp.where(_mine(hoff), la_e - la_all, 0.0)) * dt_all)
        e2 = _split2(jnp.exp(la_all))
        ela_e = jnp.exp(la_e)
        for g in range(NG):
            hc0 = hoff + g * HPG
            x = xbc_ref[:, g * 256:(g + 1) * 256]
            bb = xbc_ref[:, B0 + g * NS:B0 + (g + 1) * NS].astype(bf16)
            cb = xbc_ref[:, C0 + g * NS:C0 + (g + 1) * NS].astype(bf16)
            ht = h_ref[g * NS:(g + 1) * NS, :]
            scores = _dot_nt(cb, bb)
            yoff = _dot(cb, ht.astype(bf16))
            wend = _dot(w2, sw_ref[g])
            expla = _dot(e2, sw_ref[g])
            mixes, xstack = [], []
            for r in range(HPG):
                hc = hc0 + r
                la_rep = jnp.broadcast_to(la_all[:, hc:hc + 1], (Q, 128))
                decay = jnp.exp(jnp.where(mask, la_rep - la_t[hc:hc + 1, :], NEG))
                mixes.append((scores * decay * dt_t[hc:hc + 1, :]).astype(bf16))
                xstack.append(jnp.where(hms[r], x, 0.0).astype(bf16))
            y = _dot(jnp.concatenate(mixes, axis=1), jnp.concatenate(xstack, axis=0)) + yoff * expla
            if has_acc:
                y = y + yacc_ref[:, g * 256:(g + 1) * 256]
            y_ref[:, g * 256:(g + 1) * 256] = y
            h_ref[g * NS:(g + 1) * NS, :] = ht * _head_row(ela_e, hc0) + _dot_tn(bb, (x * wend).astype(bf16))

    row = lambda j: (cm(j), 0)
    yblk = pl.BlockSpec((Q, DI), row)
    return pl.pallas_call(
        kern, out_shape=(S((T, DI), f32), S((nch, NG * NS, HPG * HD), f32)), grid=(nch,),
        in_specs=[pl.BlockSpec((Q, 4096), row), pl.BlockSpec((Q, 128), row), pl.BlockSpec((Q, 128), row),
                  _full(sel_w.shape)] + ([yblk] if has_acc else []),
        out_specs=(yblk, pl.BlockSpec((None, NG * NS, HPG * HD), lambda j: (cm(j), 0, 0))),
        scratch_shapes=[pltpu.VMEM((NG * NS, HPG * HD), f32)],
        input_output_aliases={4: 0} if has_acc else {},
        compiler_params=_params("arbitrary"), name=name)(xbc, dt, la, sel_w, *([y_acc] if has_acc else []))


def _ssd_bwd(xbc, dy, dt, la, hprev, dskip_full, consts, rev, nxc, name, acc=None):
    T = xbc.shape[0]
    nch = T // Q
    hoff = NH if rev else 0
    e = 0 if rev else Q - 1
    cm = lambda j: _chunk_of(nch - 1 - j, rev, nxc, nch)
    has_acc = acc is not None
    sel_w, ind_h, ind_e = consts

    def kern(*refs):
        xbc_ref, dy_ref, dt_ref, la_ref, hp_ref, dsk_ref, sw_ref, ih_ref, ie_ref = refs[:9]
        k = 9
        if has_acc:
            dxbc_in, a1_in, a2_in, r2_in, s_in = refs[k:k + 5]
            k += 5
        dxbc_ref, a1_ref, a2_ref, r2_ref, s_ref, g_ref, r2scr = refs[k:k + 7]
        j = pl.program_id(0)

        @pl.when(j == 0)
        def _():
            g_ref[...] = jnp.zeros_like(g_ref)

        mask, mask_t, hms = _masks(rev)
        la_all = la_ref[...]
        dt_all = dt_ref[...]
        la_t = jnp.transpose(la_all)
        dt_t = jnp.transpose(dt_all)
        la_e = la_all[e:e + 1, :]
        w2 = _split2(jnp.exp(jnp.where(_mine(hoff), la_e - la_all, 0.0)) * dt_all)
        e2 = _split2(jnp.exp(la_all))
        wed2 = jnp.concatenate([w2, e2, _split2(dt_all)], axis=0)
        ela_e = jnp.exp(la_e)
        r2scr[...] = jnp.zeros_like(r2scr)
        a1acc = jnp.zeros((Q, 128), f32)
        a2acc = jnp.zeros((Q, 128), f32)
        sacc = jnp.zeros((1, 128), f32)
        for g in range(NG):
            hc0 = hoff + g * HPG
            x = xbc_ref[:, g * 256:(g + 1) * 256]
            bb = xbc_ref[:, B0 + g * NS:B0 + (g + 1) * NS].astype(bf16)
            cb = xbc_ref[:, C0 + g * NS:C0 + (g + 1) * NS].astype(bf16)
            dyv = dy_ref[:, g * 256:(g + 1) * 256]
            gt = g_ref[g * NS:(g + 1) * NS, :]
            ht = hp_ref[g * NS:(g + 1) * NS, :]
            gtb = gt.astype(bf16)
            htb = ht.astype(bf16)
            xb = x.astype(bf16)
            scores = _dot_nt(cb, bb)
            scores_t = _dot_nt(bb, cb)
            bg = _dot(bb, gtb)
            yoff = _dot(cb, htb)
            sel3 = _dot(wed2, sw_ref[g])
            wend, expla, dtf = sel3[0:Q], sel3[Q:2 * Q], sel3[2 * Q:3 * Q]
            dym = jnp.concatenate([jnp.where(hms[r], dyv, 0.0).astype(bf16) for r in range(HPG)], axis=0)
            dyx_all = _dot_nt(dym, xb)
            sdts, ems = [], []
            wsum = jnp.zeros((Q, Q), f32)
            for r in range(HPG):
                hc = hc0 + r
                la_rep = jnp.broadcast_to(la_all[:, hc:hc + 1], (Q, 128))
                la_r = la_t[hc:hc + 1, :]
                dt_r = dt_t[hc:hc + 1, :]
                decay = jnp.exp(jnp.where(mask, la_rep - la_r, NEG))
                decay_t = jnp.exp(jnp.where(mask_t, la_r - la_rep, NEG))
                dyx = dyx_all[r * Q:(r + 1) * Q, :]
                fm = dyx * (scores * decay)
                r2scr[hc:hc + 1, :] = jnp.sum(fm, axis=0, keepdims=True)
                ems.append(fm * dt_r)
                wsum = wsum + dyx * decay * dt_r
                sdts.append((scores_t * decay_t).astype(bf16))
            dx = dtf * _dot(jnp.concatenate(sdts, axis=1), dym) + wend * bg
            if not has_acc:
                dx = dx + dsk_ref[:, g * 256:(g + 1) * 256] * dyv
            red3 = _dot(jnp.concatenate([(dyv * yoff * expla).astype(bf16), (x * bg).astype(bf16), (gt * ht).astype(bf16)],
                                        axis=0), ih_ref[g])
            a1acc = a1acc + _dot(_split2(jnp.concatenate(ems, axis=1)), ie_ref[g]) + red3[0:Q]
            a2acc = a2acc + red3[Q:2 * Q]
            sacc = sacc + jnp.sum(red3[2 * Q:3 * Q], axis=0, keepdims=True)
            wb = wsum.astype(bf16)
            dysb = (dyv * expla).astype(bf16)
            dc = _dot(wb, bb) + _dot_nt(dysb, htb)
            db = _dot_tn(wb, cb) + _dot_nt((x * wend).astype(bf16), gtb)
            g_ref[g * NS:(g + 1) * NS, :] = gt * _head_row(ela_e, hc0) + _dot_tn(cb, dysb)
            if has_acc:
                dx = dx + dxbc_in[:, g * 256:(g + 1) * 256]
                db = db + dxbc_in[:, B0 + g * NS:B0 + (g + 1) * NS]
                dc = dc + dxbc_in[:, C0 + g * NS:C0 + (g + 1) * NS]
            dxbc_ref[:, g * 256:(g + 1) * 256] = dx
            dxbc_ref[:, B0 + g * NS:B0 + (g + 1) * NS] = db
            dxbc_ref[:, C0 + g * NS:C0 + (g + 1) * NS] = dc
        r2c = jnp.transpose(r2scr[...])
        sc = jnp.broadcast_to(sacc, (Q, 128))
        if has_acc:
            a1acc = a1acc + a1_in[...]
            a2acc = a2acc + a2_in[...]
            r2c = r2c + r2_in[...]
            sc = sc + s_in[...]
        a1_ref[...] = a1acc
        a2_ref[...] = a2acc
        r2_ref[...] = r2c
        s_ref[...] = sc

    blk = pl.BlockSpec((Q, 128), lambda j: (cm(j), 0))
    big = pl.BlockSpec((Q, 4096), lambda j: (cm(j), 0))
    in_specs = [big, pl.BlockSpec((Q, DI), lambda j: (cm(j), 0)), blk, blk,
                pl.BlockSpec((None, NG * NS, HPG * HD), lambda j: (cm(j), 0, 0)), _full((1, DI)),
                _full(sel_w.shape), _full(ind_h.shape), _full(ind_e.shape)]
    args = [xbc, dy, dt, la, hprev, dskip_full, sel_w, ind_h, ind_e]
    aliases = {}
    if has_acc:
        in_specs += [big, blk, blk, blk, blk]
        args += list(acc)
        aliases = {9: 0, 10: 1, 11: 2, 12: 3, 13: 4}
    return pl.pallas_call(
        kern, out_shape=(S((T, 4096), f32), S((T, 128), f32), S((T, 128), f32), S((T, 128), f32), S((T, 128), f32)),
        grid=(nch,), in_specs=in_specs, out_specs=(big, blk, blk, blk, blk),
        scratch_shapes=[pltpu.VMEM((NG * NS, HPG * HD), f32), pltpu.VMEM((128, Q), f32)],
        input_output_aliases=aliases,
        compiler_params=_params("arbitrary"), name=name)(*args)


def _ynorm_fwd(ysum, xbc, proj, dskip_full, nw, L):
    nlx = L // RT

    def kern(ys_ref, xs_ref, z_ref, dsk_ref, nw_ref, y_ref, yn_ref, ynt_ref):
        y = ys_ref[...] + dsk_ref[...] * xs_ref[...]
        y_ref[...] = y
        yz = y * _silu(z_ref[...])
        for g in range(NG):
            sl = yz[:, g * 256:(g + 1) * 256]
            r = lax.rsqrt(jnp.mean(sl * sl, axis=1, keepdims=True) + EPS)
            yn = (sl * r) * nw_ref[:, g * 256:(g + 1) * 256]
            yn_ref[:, g * 256:(g + 1) * 256] = yn.astype(bf16)
            ynt_ref[g * 256:(g + 1) * 256, :] = jnp.transpose(yn).astype(bf16)

    blk = pl.BlockSpec((RT, DI), lambda i: (i, 0))
    return pl.pallas_call(
        kern, out_shape=(S((L, DI), f32), S((L, DI), bf16), S((DI, L), bf16)), grid=(nlx,),
        in_specs=[blk, blk, pl.BlockSpec((RT, DI), lambda i: (i, Z0 // DI)), _full((1, DI)), _full((1, DI))],
        out_specs=(blk, blk, pl.BlockSpec((DI, RT), lambda i: (0, i))),
        compiler_params=_params("parallel"), name="ynorm_fwd")(ysum, xbc, proj, dskip_full, nw)


def _ynorm_bwd(dyn, y, xbc, proj, dskip_full, nw, dproj):
    L = y.shape[0]
    T = proj.shape[0]
    nlx, nt = L // RT, T // RT

    def kern(dyn_ref, y_ref, xs_ref, z_ref, dsk_ref, nw_ref, _alias, dz_ref, dy_ref, dnw_ref, dsk_acc):
        i = pl.program_id(0)

        @pl.when(i == 0)
        def _():
            dnw_ref[...] = jnp.zeros_like(dnw_ref)
            dsk_acc[...] = jnp.zeros_like(dsk_acc)

        @pl.when(i >= nlx)
        def _():
            dz_ref[...] = jnp.zeros_like(dz_ref)
            dy_ref[...] = jnp.zeros_like(dy_ref)

        @pl.when(i < nlx)
        def _():
            y = y_ref[...]
            z = z_ref[...]
            sz = _sig(z)
            gz = z * sz
            yz = y * gz
            dynv = dyn_ref[...]
            for g in range(NG):
                cs = slice(g * 256, (g + 1) * 256)
                sl = yz[:, cs]
                r = lax.rsqrt(jnp.mean(sl * sl, axis=1, keepdims=True) + EPS)
                yhat = sl * r
                dn = dynv[:, cs]
                dnw_ref[:, cs] += jnp.sum(dn * yhat, axis=0, keepdims=True)
                dyh = dn * nw_ref[:, cs]
                dyz = r * (dyh - yhat * jnp.mean(dyh * yhat, axis=1, keepdims=True))
                dyv = dyz * gz[:, cs]
                dy_ref[:, cs] = dyv
                dz_ref[:, cs] = (dyz * y[:, cs] * _dsilu(z[:, cs], sz[:, cs])).astype(bf16)
                dsk_acc[:, cs] += jnp.sum(dyv * xs_ref[:, cs], axis=0, keepdims=True)

    xmap = lambda i: (jnp.minimum(i, nlx - 1), 0)
    return pl.pallas_call(
        kern, out_shape=(S(dproj.shape, bf16), S((T, DI), f32), S((1, DI), f32), S((1, DI), f32)), grid=(nt,),
        in_specs=[pl.BlockSpec((RT, DI), xmap), pl.BlockSpec((RT, DI), xmap), pl.BlockSpec((RT, DI), xmap),
                  pl.BlockSpec((RT, DI), lambda i: (jnp.minimum(i, nlx - 1), Z0 // DI)), _full((1, DI)), _full((1, DI)),
                  pl.BlockSpec(memory_space=pl.ANY)],
        out_specs=(pl.BlockSpec((RT, DI), lambda i: (i, Z0 // DI)), pl.BlockSpec((RT, DI), lambda i: (i, 0)),
                   _full((1, DI)), _full((1, DI))),
        input_output_aliases={6: 0},
        compiler_params=_params("arbitrary"), name="ynorm_bwd")(dyn, y, xbc, proj, dskip_full, nw, dproj)


def _head_sums(cols):
    def kern(c_ref, o_ref):
        o_ref[...] = jnp.broadcast_to(jnp.sum(c_ref[...], axis=1, keepdims=True), (NH, 128))

    return pl.pallas_call(kern, out_shape=S((NH, 128), f32), name="head_sums")(cols)


SEG_STRIDE = 96
SEG_PAD = 16
NSEG = RT // GRID_W
CONF_ROWS = SEG_PAD + NSEG * SEG_STRIDE


SHIFT_ROWS = CONF_ROWS - 8
CONF_CW = 256


CONF_RB = 32


def _seg_zero_pads(scr):
    scr[0:SEG_PAD, :] = jnp.zeros((SEG_PAD, scr.shape[1]), f32)
    for s in range(NSEG):
        lo = SEG_PAD + s * SEG_STRIDE + GRID_W
        scr[lo:lo + SEG_STRIDE - GRID_W, :] = jnp.zeros((SEG_STRIDE - GRID_W, scr.shape[1]), f32)


def _seg_row(r0):
    return SEG_PAD + (r0 // GRID_W) * SEG_STRIDE + r0 % GRID_W


def _shift_copies(cps, scr, cs):
    for s in range(1, 8):
        cps[s - 1, :, :] = scr[pl.ds(s, SHIFT_ROWS), cs]


def _tap(cps, scr, cs, o):
    rs = o % 8
    return scr[pl.ds(o, GRID_W), cs] if rs == 0 else cps[rs - 1, pl.ds(o - rs, GRID_W), :]


def _conf_fwd(proj, w32, cb, lnw, lnb, L):
    nlx = L // RT

    def kern(v_ref, g_ref, cg_ref, w_ref, cb_ref, lnw_ref, lnb_ref, u1_ref, u3_ref, u3t_ref, scr, cps, u3_scr):
        _seg_zero_pads(scr)
        for r0 in range(0, RT, CONF_RB):
            rows = slice(r0, r0 + CONF_RB)
            scr[_seg_row(r0):_seg_row(r0) + CONF_RB, :] = v_ref[rows, :] * _sig(g_ref[rows, :])
        for cc in range(D // CONF_CW):
            cs = slice(cc * CONF_CW, (cc + 1) * CONF_CW)
            _shift_copies(cps, scr, cs)
            for s in range(NSEG):
                acc = jnp.broadcast_to(cb_ref[:, cs], (GRID_W, CONF_CW))
                for k in range(CK):
                    acc = acc + w_ref[k:k + 1, cs] * _tap(cps, scr, cs, SEG_PAD + s * SEG_STRIDE + k - CK // 2)
                u1_ref[s * GRID_W:(s + 1) * GRID_W, cs] = acc
        for r0 in range(0, RT, CONF_RB):
            rows = slice(r0, r0 + CONF_RB)
            u1 = u1_ref[rows, :]
            xc = u1 - jnp.mean(u1, axis=1, keepdims=True)
            r = lax.rsqrt(jnp.mean(xc * xc, axis=1, keepdims=True) + EPS)
            u2 = (xc * r) * lnw_ref[...] + lnb_ref[...]
            u3 = _silu(u2) * _silu(cg_ref[rows, :])
            u3_ref[rows, :] = u3.astype(bf16)
            u3_scr[rows, :] = u3
        u3t_ref[...] = jnp.transpose(u3_scr[...]).astype(bf16)

    blk = pl.BlockSpec((RT, D), lambda i: (i, 0))
    return pl.pallas_call(
        kern, out_shape=(S((L, D), f32), S((L, D), bf16), S((D, L), bf16)), grid=(nlx,),
        in_specs=[pl.BlockSpec((RT, D), lambda i: (i, GV0 // D)), pl.BlockSpec((RT, D), lambda i: (i, GG0 // D)),
                  pl.BlockSpec((RT, D), lambda i: (i, CG0 // D)), _full((32, D)), _full((1, D)), _full((1, D)), _full((1, D))],
        out_specs=(blk, blk, pl.BlockSpec((D, RT), lambda i: (0, i))),
        scratch_shapes=[pltpu.VMEM((CONF_ROWS, D), f32), pltpu.VMEM((7, SHIFT_ROWS, CONF_CW), f32), pltpu.VMEM((RT, D), f32)],
        compiler_params=_params("parallel"), name="conf_fwd")(proj, proj, proj, w32, cb, lnw, lnb)


def _conf_bwd(du3, u1, proj, w32, lnw, lnb, dproj):
    L = u1.shape[0]
    T = proj.shape[0]
    nlx, nt = L // RT, T // RT

    def kern(du3_ref, u1_ref, v_ref, g_ref, cg_ref, w_ref, lnw_ref, lnb_ref, _alias,
             o_ref, dw_ref, dcb_ref, dlw_ref, dlb_ref, scr_u, scr_d, du0_scr, cps_u, cps_d):
        i = pl.program_id(0)

        @pl.when(i == 0)
        def _():
            dw_ref[...] = jnp.zeros_like(dw_ref)
            dcb_ref[...] = jnp.zeros_like(dcb_ref)
            dlw_ref[...] = jnp.zeros_like(dlw_ref)
            dlb_ref[...] = jnp.zeros_like(dlb_ref)

        @pl.when(i >= nlx)
        def _():
            o_ref[...] = jnp.zeros_like(o_ref)

        @pl.when(i < nlx)
        def _():
            _seg_zero_pads(scr_u)
            _seg_zero_pads(scr_d)
            for r0 in range(0, RT, CONF_RB):
                rows = slice(r0, r0 + CONF_RB)
                cg = cg_ref[rows, :]
                scg = _sig(cg)
                u1 = u1_ref[rows, :]
                xc = u1 - jnp.mean(u1, axis=1, keepdims=True)
                r = lax.rsqrt(jnp.mean(xc * xc, axis=1, keepdims=True) + EPS)
                xhat = xc * r
                u2 = xhat * lnw_ref[...] + lnb_ref[...]
                s2 = _sig(u2)
                du3v = du3_ref[rows, :]
                du2 = du3v * (cg * scg) * _dsilu(u2, s2)
                o_ref[rows, 2 * D:3 * D] = (du3v * (u2 * s2) * _dsilu(cg, scg)).astype(bf16)
                dlw_ref[...] += jnp.sum(du2 * xhat, axis=0, keepdims=True)
                dlb_ref[...] += jnp.sum(du2, axis=0, keepdims=True)
                dxh = du2 * lnw_ref[...]
                du1 = r * (dxh - jnp.mean(dxh, axis=1, keepdims=True) - xhat * jnp.mean(dxh * xhat, axis=1, keepdims=True))
                dcb_ref[...] += jnp.sum(du1, axis=0, keepdims=True)
                scr_u[_seg_row(r0):_seg_row(r0) + CONF_RB, :] = v_ref[rows, :] * _sig(g_ref[rows, :])
                scr_d[_seg_row(r0):_seg_row(r0) + CONF_RB, :] = du1
            for cc in range(D // CONF_CW):
                cs = slice(cc * CONF_CW, (cc + 1) * CONF_CW)
                _shift_copies(cps_u, scr_u, cs)
                _shift_copies(cps_d, scr_d, cs)
                for k in range(CK):
                    t = jnp.zeros((GRID_W, CONF_CW), f32)
                    for s in range(NSEG):
                        base = SEG_PAD + s * SEG_STRIDE
                        t = t + scr_d[pl.ds(base, GRID_W), cs] * _tap(cps_u, scr_u, cs, base + k - CK // 2)
                    dw_ref[k:k + 1, cs] += jnp.sum(t, axis=0, keepdims=True)
                for s in range(NSEG):
                    base = SEG_PAD + s * SEG_STRIDE
                    acc = jnp.zeros((GRID_W, CONF_CW), f32)
                    for k in range(CK):
                        acc = acc + w_ref[k:k + 1, cs] * _tap(cps_d, scr_d, cs, base + CK // 2 - k)
                    du0_scr[s * GRID_W:(s + 1) * GRID_W, cs] = acc
            for r0 in range(0, RT, CONF_RB):
                rows = slice(r0, r0 + CONF_RB)
                du0 = du0_scr[rows, :]
                sg = _sig(g_ref[rows, :])
                o_ref[rows, 0:D] = (du0 * sg).astype(bf16)
                o_ref[rows, D:2 * D] = (du0 * v_ref[rows, :] * sg * (1.0 - sg)).astype(bf16)

    xmap = lambda i: (jnp.minimum(i, nlx - 1), 0)
    pmap = lambda cb: (lambda i: (jnp.minimum(i, nlx - 1), cb))
    return pl.pallas_call(
        kern, out_shape=(S(dproj.shape, bf16), S((32, D), f32), S((1, D), f32), S((1, D), f32), S((1, D), f32)), grid=(nt,),
        in_specs=[pl.BlockSpec((RT, D), xmap), pl.BlockSpec((RT, D), xmap),
                  pl.BlockSpec((RT, D), pmap(GV0 // D)), pl.BlockSpec((RT, D), pmap(GG0 // D)), pl.BlockSpec((RT, D), pmap(CG0 // D)),
                  _full((32, D)), _full((1, D)), _full((1, D)), pl.BlockSpec(memory_space=pl.ANY)],
        out_specs=(pl.BlockSpec((RT, 3 * D), lambda i: (i, GV0 // (3 * D))), _full((32, D)), _full((1, D)), _full((1, D)), _full((1, D))),
        scratch_shapes=[pltpu.VMEM((CONF_ROWS, D), f32), pltpu.VMEM((CONF_ROWS, D), f32), pltpu.VMEM((RT, D), f32),
                        pltpu.VMEM((7, SHIFT_ROWS, CONF_CW), f32), pltpu.VMEM((7, SHIFT_ROWS, CONF_CW), f32)],
        input_output_aliases={8: 0},
        compiler_params=_params("arbitrary"), name="conf_bwd")(du3, u1, proj, proj, proj, w32, lnw, lnb, dproj)


def _merge_fwd(bs, bc, proj):
    L = bs.shape[0]

    def kern(bs_ref, bc_ref, g1_ref, g2_ref, o_ref, ot_ref):
        mv = _sig(g1_ref[...]) * bs_ref[...] + _sig(g2_ref[...]) * bc_ref[...]
        o_ref[...] = mv.astype(bf16)
        ot_ref[...] = jnp.transpose(mv).astype(bf16)

    blk = pl.BlockSpec((RT, D), lambda i: (i, 0))
    return pl.pallas_call(
        kern, out_shape=(S((L, D), bf16), S((D, L), bf16)), grid=(L // RT,),
        in_specs=[blk, blk, pl.BlockSpec((RT, D), lambda i: (i, G10 // D)), pl.BlockSpec((RT, D), lambda i: (i, G20 // D))],
        out_specs=(blk, pl.BlockSpec((D, RT), lambda i: (0, i))),
        compiler_params=_params("parallel"), name="merge_fwd")(bs, bc, proj, proj)


def _merge_bwd(dmerged, bs, bc, proj):
    L = bs.shape[0]
    T = proj.shape[0]
    nlx, nt = L // RT, T // RT

    def kern(dm_ref, bs_ref, bc_ref, g1_ref, g2_ref, o_ref, dbs_ref, dbc_ref):
        i = pl.program_id(0)

        @pl.when(i >= nlx)
        def _():
            o_ref[...] = jnp.zeros_like(o_ref)

        @pl.when(i < nlx)
        def _():
            dm = dm_ref[...]
            s1 = _sig(g1_ref[...])
            s2 = _sig(g2_ref[...])
            dbs_ref[...] = (dm * s1).astype(bf16)
            dbc_ref[...] = (dm * s2).astype(bf16)
            o_ref[:, 0:D] = (dm * bs_ref[...] * s1 * (1.0 - s1)).astype(bf16)
            o_ref[:, D:2 * D] = (dm * bc_ref[...] * s2 * (1.0 - s2)).astype(bf16)

    xmap = lambda i: (jnp.minimum(i, nlx - 1), 0)
    pmap = lambda cb: (lambda i: (jnp.minimum(i, nlx - 1), cb))
    xblk = pl.BlockSpec((RT, D), xmap)
    return pl.pallas_call(
        kern, out_shape=(S((T, NP), bf16), S((L, D), bf16), S((L, D), bf16)), grid=(nt,),
        in_specs=[xblk, xblk, xblk, pl.BlockSpec((RT, D), pmap(G10 // D)), pl.BlockSpec((RT, D), pmap(G20 // D))],
        out_specs=(pl.BlockSpec((RT, 2 * D), lambda i: (i, G10 // (2 * D))), xblk, xblk),
        compiler_params=_params("arbitrary"), name="merge_bwd")(dmerged, bs, bc, proj, proj)


def _final(x, out, target, mod, fw):
    L = x.shape[0]

    def kern(x_ref, o_ref, t_ref, mod_ref, fw_ref, dx1_ref, dout_ref, loss_ref, dfw_ref, dg_ref):
        i = pl.program_id(0)

        @pl.when(i == 0)
        def _():
            loss_ref[...] = jnp.zeros_like(loss_ref)
            dfw_ref[...] = jnp.zeros_like(dfw_ref)
            dg_ref[...] = jnp.zeros_like(dg_ref)

        gate = mod_ref[0:1, 2 * D:3 * D]
        ov = o_ref[...]
        x1 = x_ref[...] + gate * ov
        r = lax.rsqrt(jnp.mean(x1 * x1, axis=1, keepdims=True) + EPS)
        xn = x1 * r
        fw = fw_ref[...]
        err = xn * fw - t_ref[...]
        part = 0.5 * jnp.sum(jnp.mean(err * err, axis=1, keepdims=True), axis=0, keepdims=True)
        loss_ref[...] += jnp.broadcast_to(part, (8, 128))
        dy = err * (1.0 / D)
        dfw_ref[...] += jnp.sum(dy * xn, axis=0, keepdims=True)
        dyw = dy * fw
        dx1 = r * (dyw - xn * jnp.mean(dyw * xn, axis=1, keepdims=True))
        dx1_ref[...] = dx1
        dout_ref[...] = (gate * dx1).astype(bf16)
        dg_ref[...] += jnp.sum(dx1 * ov, axis=0, keepdims=True)

    blk = pl.BlockSpec((RT, D), lambda i: (i, 0))
    return pl.pallas_call(
        kern, out_shape=(S((L, D), f32), S((L, D), bf16), S((8, 128), f32), S((1, D), f32), S((1, D), f32)), grid=(L // RT,),
        in_specs=[blk, blk, blk, _full((8, 3 * D)), _full((1, D))],
        out_specs=(blk, blk, _full((8, 128)), _full((1, D)), _full((1, D))),
        compiler_params=_params("arbitrary"), name="final")(x, out, target, mod, fw)


def _me():
    return 4 * lax.axis_index("x") + 2 * lax.axis_index("y") + lax.axis_index("c")


def _xchg_copy(ins, outs, send_sems, recv_sems, modes, a, k, me):
    peer = lax.rem(me + k, N_DEV)
    pid = (peer // 4, lax.rem(peer // 2, 2), lax.rem(peer, 2))
    src = ins[a].at[peer] if modes[a] else ins[a]
    return pltpu.make_async_remote_copy(src_ref=src, dst_ref=outs[a].at[me], send_sem=send_sems.at[a, k - 1],
                                        recv_sem=recv_sems.at[a, k - 1], device_id=pid, device_id_type=MESH)


def _xchg_local(ins, outs, loc_sems, modes, a, me):
    return pltpu.make_async_copy(ins[a].at[me] if modes[a] else ins[a], outs[a].at[me], loc_sems.at[a])


def _xchg_start(ins, outs, send_sems, recv_sems, loc_sems, modes):
    me = _me()
    for a in range(len(modes)):
        _xchg_local(ins, outs, loc_sems, modes, a, me).start()
        for k in range(1, N_DEV):
            _xchg_copy(ins, outs, send_sems, recv_sems, modes, a, k, me).start()


def _xchg_wait(ins, outs, send_sems, recv_sems, loc_sems, modes):
    me = _me()
    for a in range(len(modes)):
        for k in range(1, N_DEV):
            frm = lax.rem(me + N_DEV - k, N_DEV)
            src = ins[a].at[frm] if modes[a] else ins[a]
            pltpu.make_async_remote_copy(src_ref=src, dst_ref=outs[a].at[frm], send_sem=send_sems.at[a, k - 1],
                                         recv_sem=recv_sems.at[a, k - 1], device_id=(0, 0, 0), device_id_type=MESH).wait_recv()
    for a in range(len(modes)):
        for k in range(1, N_DEV):
            _xchg_copy(ins, outs, send_sems, recv_sems, modes, a, k, me).wait_send()
        _xchg_local(ins, outs, loc_sems, modes, a, me).wait()


def _xchg_out_shapes(arrs, modes):
    return tuple(S((N_DEV,) + (a.shape[1:] if sc else a.shape), a.dtype) for a, sc in zip(arrs, modes))


def _xchg_sems(n):
    return [pltpu.SemaphoreType.DMA((n, N_DEV - 1)), pltpu.SemaphoreType.DMA((n, N_DEV - 1)), pltpu.SemaphoreType.DMA((n,))]


def _exchange(arrs, modes, name):
    n = len(arrs)

    def kern(*refs):
        ins, outs, sems = refs[:n], refs[n:2 * n], refs[2 * n:]
        _xchg_start(ins, outs, *sems, modes)
        _xchg_wait(ins, outs, *sems, modes)

    anyspec = pl.BlockSpec(memory_space=pl.ANY)
    return pl.pallas_call(
        kern, out_shape=_xchg_out_shapes(arrs, modes), in_specs=[anyspec] * n, out_specs=tuple([anyspec] * n),
        scratch_shapes=_xchg_sems(n), name=name)(*arrs)


def _gather2(arrs, name):
    n = len(arrs)

    def kern(*refs):
        ins, outs = refs[:n], refs[n:2 * n]
        send_sems, recv_sems, loc_sems = refs[2 * n:]
        x, y, c = lax.axis_index("x"), lax.axis_index("y"), lax.axis_index("c")
        me, sib = (x, y, c), (x, y, 1 - c)
        chips = [(1 - x, y), (x, 1 - y), (1 - x, 1 - y)]

        def slot(a, p):
            return outs[a].at[4 * p[0] + 2 * p[1] + p[2]]

        def cp(a, k, block, to, own=False):
            return pltpu.make_async_remote_copy(src_ref=ins[a] if own else slot(a, block), dst_ref=slot(a, block),
                                                send_sem=send_sems.at[a, k], recv_sem=recv_sems.at[a, k],
                                                device_id=to, device_id_type=MESH)

        started = []
        for a in range(n):
            loc = pltpu.make_async_copy(ins[a], slot(a, me), loc_sems.at[a])
            loc.start()
            started.append(cp(a, 0, me, sib, own=True))
            started += [cp(a, 1 + j, me, (*chip, c), own=True) for j, chip in enumerate(chips)]
        for s in started:
            s.start()
        for j, chip in enumerate(chips):
            for a in range(n):
                cp(a, 1 + j, (*chip, c), me).wait_recv()
                fwd = cp(a, 4 + j, (*chip, c), sib)
                fwd.start()
                started.append(fwd)
        for a in range(n):
            cp(a, 0, sib, me).wait_recv()
            for j, chip in enumerate(chips):
                cp(a, 4 + j, (*chip, 1 - c), me).wait_recv()
        for s in started:
            s.wait_send()
        for a in range(n):
            pltpu.make_async_copy(ins[a], slot(a, me), loc_sems.at[a]).wait()

    anyspec = pl.BlockSpec(memory_space=pl.ANY)
    return pl.pallas_call(
        kern, out_shape=_xchg_out_shapes(arrs, (False,) * n), in_specs=[anyspec] * n, out_specs=tuple([anyspec] * n),
        scratch_shapes=[pltpu.SemaphoreType.DMA((n, 7)), pltpu.SemaphoreType.DMA((n, 7)), pltpu.SemaphoreType.DMA((n,))],
        name=name)(*arrs)


def _adamw(parts, w, m, v, name):
    r, c = w.shape
    tr = r
    for cand in (128, 64, 32, 16, 8):
        if r % cand == 0 and r > cand:
            tr = cand
            break
    c1 = 1.0 / (1.0 - ADAM_B1 ** ADAM_STEP)
    c2 = 1.0 / (1.0 - ADAM_B2 ** ADAM_STEP)

    def kern(p_ref, w_ref, m_ref, v_ref, g_ref, d_ref, m2_ref, v2_ref):
        g = p_ref[0].astype(f32)
        for i in range(1, N_DEV):
            g = g + p_ref[i].astype(f32)
        g_ref[...] = g
        m2 = ADAM_B1 * m_ref[...] + (1.0 - ADAM_B1) * g
        v2 = ADAM_B2 * v_ref[...] + (1.0 - ADAM_B2) * (g * g)
        m2_ref[...] = m2
        v2_ref[...] = v2
        d_ref[...] = -ADAM_LR * ((m2 * c1) / (jnp.sqrt(v2 * c2) + ADAM_EPS) + ADAM_WD * w_ref[...])

    blk = pl.BlockSpec((tr, c), lambda i: (i, 0))
    sh = S((r, c), f32)
    return pl.pallas_call(
        kern, out_shape=(sh, sh, sh, sh), grid=(r // tr,),
        in_specs=[pl.BlockSpec((N_DEV, tr, c), lambda i: (0, i, 0)), blk, blk, blk], out_specs=(blk, blk, blk, blk),
        compiler_params=_params("parallel"), name=name)(parts, w, m, v)


_SMALL = (("c_ctx", 1024), ("b_mod", 3072), ("norm_w", 1024), ("ssm_conv_b", 4096), ("dt_bias", 64), ("a_log", 64),
          ("d_skip", 32), ("ssm_norm_w", 2048), ("conf_conv_b", 1024), ("conf_ln_w", 1024), ("conf_ln_b", 1024),
          ("final_norm_w", 1024))
SMALL_TILE = 8 * 128


def _pack_small(d):
    rows = []
    for name, n in _SMALL:
        v = d[name].reshape(-1).astype(f32)
        pad = (-n) % SMALL_TILE
        if pad:
            v = jnp.concatenate([v, jnp.zeros((pad,), f32)])
        rows.append(v.reshape(-1, 128))
    return jnp.concatenate(rows, axis=0)


def _unpack_small(p, shapes):
    out, r0 = {}, 0
    for name, n in _SMALL:
        nr = 8 * ((n + SMALL_TILE - 1) // SMALL_TILE)
        out[name] = p[r0:r0 + nr].reshape(-1)[:n].reshape(shapes[name])
        r0 += nr
    return out


def _permute_w_in(w):
    return jnp.concatenate([w[:, 9280:11328], w[:, 4160:6208], w[:, 0:2048], w[:, 6208:9280], w[:, 2048:4096],
                            w[:, 4096:4160], jnp.zeros((w.shape[0], NP - DT0 - 64), w.dtype)], axis=1)


def _unpermute_w_in(wp):
    return jnp.concatenate([wp[:, PX0:PX0 + 2048], wp[:, PBC0:PBC0 + 2048], wp[:, DT0:DT0 + 64], wp[:, Z0:Z0 + 2048],
                            wp[:, GV0:GV0 + 3072], wp[:, G10:G10 + 2048]], axis=1)


def _cols_gathered(g):
    return jnp.transpose(g, (1, 0, 2)).reshape(g.shape[1], N_DEV * g.shape[2])


def _cols_to_blocks(a):
    r, c8 = a.shape
    return jnp.transpose(a.reshape(r, N_DEV, c8 // N_DEV), (1, 0, 2))


def kernel(x, c, ctx, c_ctx, w_mod, b_mod, norm_w, w_in, ssm_conv_w, ssm_conv_b, dt_bias, a_log, d_skip, ssm_norm_w, w_out_ssm, conf_conv_w, conf_conv_b, conf_ln_w, conf_ln_b, w_out_conf, w_out, final_norm_w, loss_target, m_c_ctx, m_w_mod, m_b_mod, m_norm_w, m_w_in, m_ssm_conv_w, m_ssm_conv_b, m_dt_bias, m_a_log, m_d_skip, m_ssm_norm_w, m_w_out_ssm, m_conf_conv_w, m_conf_conv_b, m_conf_ln_w, m_conf_ln_b, m_w_out_conf, m_w_out, m_final_norm_w, v_c_ctx, v_w_mod, v_b_mod, v_norm_w, v_w_in, v_ssm_conv_w, v_ssm_conv_b, v_dt_bias, v_a_log, v_d_skip, v_ssm_norm_w, v_w_out_ssm, v_conf_conv_w, v_conf_conv_b, v_conf_ln_w, v_conf_ln_b, v_w_out_conf, v_w_out, v_final_norm_w):
    L = x.shape[1]
    Lc = ctx.shape[1]
    T = L + Lc
    nlx = L // RT
    nxc = L // Q
    x2 = x.reshape(L, D)
    ctx2 = ctx.reshape(Lc, D)
    tgt = loss_target.reshape(L, D)

    gathered = _gather2([w_in[0].astype(bf16), w_mod[0].astype(bf16), ssm_conv_w[0], conf_conv_w[0]], name="gather_weights")
    wp = _permute_w_in(_cols_gathered(gathered[0]))
    wmod_bf = _cols_gathered(gathered[1])
    scw8 = jnp.concatenate([_cols_gathered(gathered[2]), jnp.zeros((8 - SK, 4096), f32)], axis=0)
    ccw32 = jnp.concatenate([_cols_gathered(gathered[3]), jnp.zeros((32 - CK, D), f32)], axis=0)

    norm_w1 = norm_w.reshape(1, D)
    scb = ssm_conv_b.reshape(1, 4096)
    bias_row = jnp.concatenate([dt_bias.reshape(1, 2 * NH), jnp.zeros((1, 128 - 2 * NH), f32)], axis=1)
    alog_row = jnp.concatenate([a_log.reshape(1, 2 * NH), jnp.zeros((1, 128 - 2 * NH), f32)], axis=1)
    dskip_full = jnp.repeat(d_skip.reshape(NH), HD).reshape(1, DI)
    snw = ssm_norm_w.reshape(1, DI)
    ccb = conf_conv_b.reshape(1, D)
    lnw = conf_ln_w.reshape(1, D)
    lnb = conf_ln_b.reshape(1, D)
    fw = final_norm_w.reshape(1, D)

    cc8 = jnp.concatenate([c.reshape(1, D), c_ctx.reshape(1, D), jnp.zeros((6, D), f32)], axis=0)
    mod = _mod_fwd(cc8, wmod_bf, b_mod.reshape(1, 3 * D))
    h, h_t = _prenorm(x2, ctx2, norm_w1, mod)
    proj, wos_g, woc_g, wo_g = _matmul(
        h, wp, f32, "proj_gather", tn=NP // 5,
        comm=([w_out_ssm[0].astype(bf16), w_out_conf[0].astype(bf16), w_out[0].astype(bf16)], (False,) * 3))
    wos_bf = wos_g.reshape(DI, D)
    woc_bf = woc_g.reshape(D, D)
    wo_bf = wo_g.reshape(D, D)
    xbc = _ssm_conv_fwd(proj, scw8, scb, nlx)
    dt, la = _dt_prep(proj, bias_row, alog_row)
    consts_f, consts_b = _scan_consts(False), _scan_consts(True)
    yf, hp_f = _ssd_fwd(xbc, dt, la, consts_f, False, nxc, "ssd_fwd_f")
    ysum, hp_b = _ssd_fwd(xbc, dt, la, consts_b, True, nxc, "ssd_fwd_b", y_acc=yf)
    y, yn, yn_t = _ynorm_fwd(ysum, xbc, proj, dskip_full, snw, L)
    bs = _matmul(yn, wos_bf, f32, "branch_ssm", tm=1024, tk=2048)
    u1, u3, u3_t = _conf_fwd(proj, ccw32, ccb, lnw, lnb, L)
    bc = _matmul(u3, woc_bf, f32, "branch_conf", tm=2048)
    merged, merged_t = _merge_fwd(bs, bc, proj)
    out = _matmul(merged, wo_bf, f32, "out_proj", tm=2048)
    dx1, dout, loss_acc, dfw, dgate = _final(x2, out, tgt, mod, fw)

    dmerged = _matmul(dout, wo_bf, f32, "d_merged", tb=True, tm=2048)
    g_wo = _matmul(merged_t, dout, bf16, "g_w_out", tm=1024, tk=2048)
    dproj, dbs, dbc = _merge_bwd(dmerged, bs, bc, proj)
    dyn = _matmul(dbs, wos_bf, f32, "d_yn", tb=True, tm=1024, tn=2048)
    g_wos = _matmul(yn_t, dbs, bf16, "g_w_out_ssm", tm=1024, tk=2048)
    du3 = _matmul(dbc, woc_bf, f32, "d_u3", tb=True, tm=2048)
    g_woc = _matmul(u3_t, dbc, bf16, "g_w_out_conf", tm=1024, tk=2048)
    dproj, g_ccw, g_ccb, g_lnw, g_lnb = _conf_bwd(du3, u1, proj, ccw32, lnw, lnb, dproj)
    dproj, dy, g_snw, dsk_cols = _ynorm_bwd(dyn, y, xbc, proj, dskip_full, snw, dproj)
    acc_f = _ssd_bwd(xbc, dy, dt, la, hp_f, dskip_full, consts_f, False, nxc, "ssd_bwd_f")
    dxbc, a1, a2, r2, sv = _ssd_bwd(xbc, dy, dt, la, hp_b, dskip_full, consts_b, True, nxc, "ssd_bwd_b", acc=acc_f)
    dproj, g_dtb, g_alog = _dt_bwd(a1, a2, r2, sv, dt, la, proj, bias_row, alog_row, dproj)
    dpre, g_scw, g_scb = _ssm_conv_dpre(dxbc, proj, scw8, scb, nlx)
    dproj = _ssm_conv_t(dpre, scw8, dproj, nlx)
    g_wp, *parts_b = _matmul(
        h_t, dproj, bf16, "g_w_in_scatter", tm=1024, tn=NP // 5,
        comm=([g_wos.reshape(N_DEV, DI // N_DEV, D), g_woc.reshape(N_DEV, D // N_DEV, D), g_wo.reshape(N_DEV, D // N_DEV, D),
               _cols_to_blocks(g_scw[:SK]), _cols_to_blocks(g_ccw[:CK])], (True,) * 5))
    dh, parts_a = _matmul(dproj, wp, f32, "d_h_scatter", tb=True, tk=NP // 5,
                          comm=([_cols_to_blocks(_unpermute_w_in(g_wp))], (True,)))
    parts = [parts_a] + parts_b
    gx, g_nw, macc = _prenorm_bwd(x2, ctx2, dh, dx1, norm_w1, mod)
    z1 = jnp.zeros((1, D), f32)
    dmod8 = jnp.concatenate([jnp.concatenate([macc[0:1], macc[1:2], dgate], axis=1),
                             jnp.concatenate([macc[2:3], macc[3:4], z1], axis=1), jnp.zeros((6, 3 * D), f32)], axis=0)
    ct = jnp.concatenate([c.reshape(D, 1), c_ctx.reshape(D, 1), jnp.zeros((D, 126), f32)], axis=1)
    g_wmod, g_bmod, g_cctx = _mod_bwd(ct, dmod8, wmod_bf)
    g_dskip = _head_sums(dsk_cols.reshape(NH, HD))[:, 0]

    small_g = _pack_small({
        "c_ctx": g_cctx[:, 0], "b_mod": g_bmod, "norm_w": g_nw, "ssm_conv_b": g_scb, "dt_bias": g_dtb[0, :2 * NH],
        "a_log": g_alog[0, :2 * NH], "d_skip": g_dskip, "ssm_norm_w": g_snw, "conf_conv_b": g_ccb, "conf_ln_w": g_lnw,
        "conf_ln_b": g_lnb, "final_norm_w": dfw})
    wmod_parts, small_parts = _exchange([_cols_to_blocks(g_wmod), small_g], (True, False), name="exchange_tail")
    parts = [parts[0], wmod_parts] + parts[1:]

    given = dict(c_ctx=c_ctx, w_mod=w_mod, b_mod=b_mod, norm_w=norm_w, w_in=w_in, ssm_conv_w=ssm_conv_w, ssm_conv_b=ssm_conv_b,
                 dt_bias=dt_bias, a_log=a_log, d_skip=d_skip, ssm_norm_w=ssm_norm_w, w_out_ssm=w_out_ssm, conf_conv_w=conf_conv_w,
                 conf_conv_b=conf_conv_b, conf_ln_w=conf_ln_w, conf_ln_b=conf_ln_b, w_out_conf=w_out_conf, w_out=w_out,
                 final_norm_w=final_norm_w)
    ms = dict(c_ctx=m_c_ctx, w_mod=m_w_mod, b_mod=m_b_mod, norm_w=m_norm_w, w_in=m_w_in, ssm_conv_w=m_ssm_conv_w,
              ssm_conv_b=m_ssm_conv_b, dt_bias=m_dt_bias, a_log=m_a_log, d_skip=m_d_skip, ssm_norm_w=m_ssm_norm_w,
              w_out_ssm=m_w_out_ssm, conf_conv_w=m_conf_conv_w, conf_conv_b=m_conf_conv_b, conf_ln_w=m_conf_ln_w,
              conf_ln_b=m_conf_ln_b, w_out_conf=m_w_out_conf, w_out=m_w_out, final_norm_w=m_final_norm_w)
    vs = dict(c_ctx=v_c_ctx, w_mod=v_w_mod, b_mod=v_b_mod, norm_w=v_norm_w, w_in=v_w_in, ssm_conv_w=v_ssm_conv_w,
              ssm_conv_b=v_ssm_conv_b, dt_bias=v_dt_bias, a_log=v_a_log, d_skip=v_d_skip, ssm_norm_w=v_ssm_norm_w,
              w_out_ssm=v_w_out_ssm, conf_conv_w=v_conf_conv_w, conf_conv_b=v_conf_conv_b, conf_ln_w=v_conf_ln_w,
              conf_ln_b=v_conf_ln_b, w_out_conf=v_w_out_conf, w_out=v_w_out, final_norm_w=v_final_norm_w)
    grads, deltas, new_m, new_v = {}, {}, {}, {}
    sharded = ("w_in", "w_mod", "w_out_ssm", "w_out_conf", "w_out", "ssm_conv_w", "conf_conv_w")
    for i, nm in enumerate(sharded):
        shp = given[nm].shape
        w2 = given[nm].reshape(shp[1], shp[2])
        res = _adamw(parts[i], w2, ms[nm].reshape(w2.shape), vs[nm].reshape(w2.shape), "adamw_" + nm)
        grads[nm], deltas[nm], new_m[nm], new_v[nm] = [r.reshape(shp) for r in res]
    shapes = {nm: given[nm].shape for nm, _ in _SMALL}
    res = _adamw(small_parts, _pack_small(given), _pack_small(ms), _pack_small(vs), "adamw_small")
    for dst, packed in zip((grads, deltas, new_m, new_v), res):
        dst.update(_unpack_small(packed, shapes))

    loss = lax.psum(loss_acc[0, 0], ("x", "y", "c"))
    order = ("c_ctx", "w_mod", "b_mod", "norm_w", "w_in", "ssm_conv_w", "ssm_conv_b", "dt_bias", "a_log", "d_skip", "ssm_norm_w",
             "w_out_ssm", "conf_conv_w", "conf_conv_b", "conf_ln_w", "conf_ln_b", "w_out_conf", "w_out", "final_norm_w")
    return (loss, gx.reshape(1, L, D), *[grads[n] for n in order], *[deltas[n] for n in order],
            *[new_m[n] for n in order], *[new_v[n] for n in order])
```

```python
import jax
import jax.numpy as jnp
from jax import lax
from jax.experimental import pallas as pl
from jax.experimental.pallas import tpu as pltpu

f32 = jnp.float32
bf16 = jnp.bfloat16

D = 1024
DI = 2048
NG = 8
HPG = 4
HD = 64
GW = HPG * HD
NS = 128
NH = 32
Q = 128
GRID_W = 64
CK = 31
SK = 4
EPS = 1e-6
RT = 256
N_DEV = 8
IN_COLS = 11328
G10, G20, Z0, PX0, GV0, GG0, CG0, PBC0, DT0, NP = 0, 1024, 2048, 4096, 6144, 7168, 8192, 9216, 11264, 11520
B0, C0 = 2048, 3072
VMEM_LIMIT = 50 * 1024 * 1024
NEG = -1e30

ADAM_LR, ADAM_B1, ADAM_B2, ADAM_EPS, ADAM_WD, ADAM_STEP = 0.001, 0.9, 0.999, 1e-08, 0.01, 10

MESH = pl.DeviceIdType.MESH
S = jax.ShapeDtypeStruct


def _params(*sem):
    return pltpu.CompilerParams(dimension_semantics=tuple(sem) if sem else None, vmem_limit_bytes=VMEM_LIMIT)


def _sig(x):
    return 1.0 / (1.0 + jnp.exp(-x))


def _silu(x):
    return x * _sig(x)


def _dsilu(x, s):
    return s * (1.0 + x * (1.0 - s))


def _dot(a, b):
    return jnp.dot(a, b, preferred_element_type=f32)


def _dot_nt(a, b):
    return lax.dot_general(a, b, (((1,), (1,)), ((), ())), preferred_element_type=f32)


def _dot_tn(a, b):
    return lax.dot_general(a, b, (((0,), (0,)), ((), ())), preferred_element_type=f32)


def _dot3(t_bf, v):
    v1 = v.astype(bf16)
    r1 = v - v1.astype(f32)
    v2 = r1.astype(bf16)
    v3 = (r1 - v2.astype(f32)).astype(bf16)
    return _dot(t_bf, v1) + _dot(t_bf, v2) + _dot(t_bf, v3)


def _pick(n, prefs):
    for p in prefs:
        if n % p == 0:
            return p
    return n


def _full(shape):
    nd = len(shape)
    return pl.BlockSpec(shape, lambda *_: (0,) * nd)


def _matmul(a, b, out_dtype, name, tm=None, tn=None, tk=None, tb=False, comm=None):
    m, k = a.shape
    n = b.shape[0] if tb else b.shape[1]
    tm = tm if tm and m % tm == 0 else _pick(m, (768, 512, 256, 128))
    tn = tn if tn and n % tn == 0 else _pick(n, (1024, 512, 256, 128))
    tk = tk if tk and k % tk == 0 else _pick(k, (1024, 768, 512, 256, 128))
    nk = k // tk
    gi, gj = m // tm, n // tn
    carrs, modes = comm if comm else ((), ())
    nc = len(carrs)

    def kern(*refs):
        a_ref, b_ref = refs[:2]
        cins = refs[2:2 + nc]
        o_ref = refs[2 + nc]
        couts = refs[3 + nc:3 + 2 * nc]
        acc_ref = refs[3 + 2 * nc]
        sems = refs[4 + 2 * nc:]
        i, j, kk = pl.program_id(0), pl.program_id(1), pl.program_id(2)
        if nc:
            @pl.when(jnp.logical_and(jnp.logical_and(i == 0, j == 0), kk == 0))
            def _():
                _xchg_start(cins, couts, *sems, modes)

        part = _dot_nt(a_ref[...], b_ref[...]) if tb else _dot(a_ref[...], b_ref[...])
        if nk == 1:
            o_ref[...] = part.astype(o_ref.dtype)
        else:
            @pl.when(kk == 0)
            def _():
                acc_ref[...] = part

            @pl.when(kk > 0)
            def _():
                acc_ref[...] += part

            @pl.when(kk == nk - 1)
            def _():
                o_ref[...] = acc_ref[...].astype(o_ref.dtype)

        if nc:
            @pl.when(jnp.logical_and(jnp.logical_and(i == gi - 1, j == gj - 1), kk == nk - 1))
            def _():
                _xchg_wait(cins, couts, *sems, modes)

    anyspec = pl.BlockSpec(memory_space=pl.ANY)
    bspec = pl.BlockSpec((tn, tk), lambda i, j, kk: (j, kk)) if tb else pl.BlockSpec((tk, tn), lambda i, j, kk: (kk, j))
    out_shape = (S((m, n), out_dtype),) + _xchg_out_shapes(carrs, modes)
    res = pl.pallas_call(
        kern, out_shape=out_shape, grid=(gi, gj, nk),
        in_specs=[pl.BlockSpec((tm, tk), lambda i, j, kk: (i, kk)), bspec] + [anyspec] * nc,
        out_specs=(pl.BlockSpec((tm, tn), lambda i, j, kk: (i, j)),) + (anyspec,) * nc,
        scratch_shapes=[pltpu.VMEM((tm, tn), f32)] + (_xchg_sems(nc) if nc else []),
        compiler_params=_params(*((("arbitrary",) * 3) if nc else ("parallel", "parallel", "arbitrary"))), name=name)(a, b, *carrs)
    return res if nc else res[0]


def _mod_fwd(cc8, w_mod_bf, b_mod):
    def kern(c_ref, w_ref, b_ref, o_ref, s_ref):
        s = _silu(c_ref[...])
        s_ref[...] = s
        o_ref[...] = _dot(s.astype(bf16), w_ref[...]) + b_ref[...]

    return pl.pallas_call(kern, out_shape=(S((8, 3 * D), f32), S((8, D), f32)), compiler_params=_params(),
                          name="mod_fwd")(cc8, w_mod_bf, b_mod)


def _mod_bwd(ct, dmod8, w_mod_bf):
    tc = 512
    nj = 3 * D // tc

    def kern(ct_ref, dm_ref, w_ref, db_ref, dc_ref):
        j = pl.program_id(0)
        cx = ct_ref[:, 1:2]
        sx = _sig(cx)
        dmc = dm_ref[1:2, :]
        db_ref[...] = dm_ref[0:1, :] + dmc
        t = jnp.sum(w_ref[...].astype(f32) * dmc.astype(bf16).astype(f32), axis=1, keepdims=True) * _dsilu(cx, sx)

        @pl.when(j == 0)
        def _():
            dc_ref[...] = jnp.zeros_like(dc_ref)

        dc_ref[...] += jnp.broadcast_to(t, (D, 128))

    return pl.pallas_call(
        kern, out_shape=(S((1, 3 * D), f32), S((D, 128), f32)), grid=(nj,),
        in_specs=[_full((D, 128)), pl.BlockSpec((8, tc), lambda j: (0, j)), pl.BlockSpec((D, tc), lambda j: (0, j))],
        out_specs=(pl.BlockSpec((1, tc), lambda j: (0, j)), _full((D, 128))),
        compiler_params=_params("arbitrary"), name="mod_bwd")(ct, dmod8, w_mod_bf)


def _wmod_grad(sct, dmx, dmc):
    cols = dmx.shape[1]

    def kern(s_ref, dmx_ref, dmc_ref, g_ref):
        dmc_sum = dmc_ref[0:1, :]
        for d in range(1, N_DEV):
            dmc_sum = dmc_sum + dmc_ref[d:d + 1, :]
        g = s_ref[:, N_DEV:N_DEV + 1] * dmc_sum
        for d in range(N_DEV):
            g = g + s_ref[:, d:d + 1] * dmx_ref[d:d + 1, :]
        g_ref[...] = g

    return pl.pallas_call(kern, out_shape=S((D, cols), f32), compiler_params=_params(), name="wmod_grad")(sct, dmx, dmc)


def _prenorm(x, ctx, norm_w, mod):
    L, Lc = x.shape[0], ctx.shape[0]
    nlx, nt = L // RT, (L + Lc) // RT

    def kern(x_ref, c_ref, nw_ref, mod_ref, h_ref, ht_ref):
        i = pl.program_id(0)
        is_c = i >= nlx
        xv = jnp.where(is_c, c_ref[...], x_ref[...])
        shift = jnp.where(is_c, mod_ref[1:2, 0:D], mod_ref[0:1, 0:D])
        scale = jnp.where(is_c, mod_ref[1:2, D:2 * D], mod_ref[0:1, D:2 * D])
        r = lax.rsqrt(jnp.mean(xv * xv, axis=1, keepdims=True) + EPS)
        hv = (xv * r) * nw_ref[...] * (1.0 + scale) + shift
        h_ref[...] = hv.astype(bf16)
        ht_ref[...] = jnp.transpose(hv).astype(bf16)

    return pl.pallas_call(
        kern, out_shape=(S((L + Lc, D), bf16), S((D, L + Lc), bf16)), grid=(nt,),
        in_specs=[pl.BlockSpec((RT, D), lambda i: (jnp.minimum(i, nlx - 1), 0)),
                  pl.BlockSpec((RT, D), lambda i: (jnp.maximum(i - nlx, 0), 0)),
                  _full((1, D)), _full((8, 3 * D))],
        out_specs=(pl.BlockSpec((RT, D), lambda i: (i, 0)), pl.BlockSpec((D, RT), lambda i: (0, i))),
        compiler_params=_params("parallel"), name="prenorm")(x, ctx, norm_w, mod)


def _prenorm_bwd(x, ctx, dh, dx1, norm_w, mod):
    L, Lc = x.shape[0], ctx.shape[0]
    nlx, nt = L // RT, (L + Lc) // RT

    def kern(x_ref, c_ref, dh_ref, dx1_ref, nw_ref, mod_ref, gx_ref, dnw_ref, acc_ref):
        i = pl.program_id(0)
        is_c = i >= nlx

        @pl.when(i == 0)
        def _():
            dnw_ref[...] = jnp.zeros_like(dnw_ref)
            acc_ref[...] = jnp.zeros_like(acc_ref)

        xv = jnp.where(is_c, c_ref[...], x_ref[...])
        scale = jnp.where(is_c, mod_ref[1:2, D:2 * D], mod_ref[0:1, D:2 * D])
        nw = nw_ref[...]
        r = lax.rsqrt(jnp.mean(xv * xv, axis=1, keepdims=True) + EPS)
        xn = xv * r
        dh = dh_ref[...]
        dsh = jnp.sum(dh, axis=0, keepdims=True)
        dsc = jnp.sum(dh * (xn * nw), axis=0, keepdims=True)
        dxnw = dh * (1.0 + scale)
        dnw_ref[...] += jnp.sum(dxnw * xn, axis=0, keepdims=True)
        dxn = dxnw * nw
        dx = r * (dxn - xn * jnp.mean(dxn * xn, axis=1, keepdims=True))

        @pl.when(jnp.logical_not(is_c))
        def _():
            gx_ref[...] = dx1_ref[...] + dx
            acc_ref[0:1, :] += dsh
            acc_ref[1:2, :] += dsc

        @pl.when(is_c)
        def _():
            acc_ref[2:3, :] += dsh
            acc_ref[3:4, :] += dsc

    xmap = lambda i: (jnp.minimum(i, nlx - 1), 0)
    return pl.pallas_call(
        kern, out_shape=(S((L, D), f32), S((1, D), f32), S((8, D), f32)), grid=(nt,),
        in_specs=[pl.BlockSpec((RT, D), xmap), pl.BlockSpec((RT, D), lambda i: (jnp.maximum(i - nlx, 0), 0)),
                  pl.BlockSpec((RT, D), lambda i: (i, 0)), pl.BlockSpec((RT, D), xmap), _full((1, D)), _full((8, 3 * D))],
        out_specs=(pl.BlockSpec((RT, D), xmap), _full((1, D)), _full((8, D))),
        compiler_params=_params("arbitrary"), name="prenorm_bwd")(x, ctx, dh, dx1, norm_w, mod)


def _xbc_col(j):
    return jnp.where(j < 2, PX0 // 1024 + j, PBC0 // 1024 + j - 2)


def _halo_specs(nt_rows, ct, col=lambda j: j):
    cur = pl.BlockSpec((RT, ct), lambda i, j: (i, col(j)))
    prev = pl.BlockSpec((8, ct), lambda i, j: (jnp.maximum(i * (RT // 8) - 1, 0), col(j)))
    nxt = pl.BlockSpec((8, ct), lambda i, j: (jnp.minimum((i + 1) * (RT // 8), nt_rows // 8 - 1), col(j)))
    return cur, prev, nxt


def _fill_halo(scr, cur_ref, prev_ref, next_ref, i, nlx, nt):
    prev_ok = jnp.logical_and(i != 0, i != nlx)
    next_ok = jnp.logical_and(i != nlx - 1, i != nt - 1)
    scr[0:8, :] = jnp.where(prev_ok, prev_ref[...], 0.0)
    scr[8:8 + RT, :] = cur_ref[...]
    scr[8 + RT:16 + RT, :] = jnp.where(next_ok, next_ref[...], 0.0)


CONV_RB = 32


def _conv_blocks(ct):
    return [(slice(cb * 128, (cb + 1) * 128), r0) for cb in range(ct // 128) for r0 in range(0, RT, CONV_RB)]


def _taps(scr, cs, r0, shifts):
    blk = scr[r0:r0 + CONV_RB + 16, cs]
    n = CONV_RB + 16
    return [(blk if d == 0 else pltpu.roll(blk, (-d) % n, 0))[8:8 + CONV_RB, :] for d in shifts]


def _ssm_conv_fwd(proj, w8, b, nlx):
    T = proj.shape[0]
    nt = T // RT
    ct = 1024
    cur, prev, nxt = _halo_specs(T, ct, _xbc_col)

    def kern(cur_ref, prev_ref, next_ref, w_ref, b_ref, o_ref, scr):
        i = pl.program_id(0)
        _fill_halo(scr, cur_ref, prev_ref, next_ref, i, nlx, nt)
        for cs, r0 in _conv_blocks(ct):
            taps = _taps(scr, cs, r0, [k - 2 for k in range(SK)])
            acc = jnp.broadcast_to(b_ref[:, cs], (CONV_RB, 128))
            for k in range(SK):
                acc = acc + w_ref[k:k + 1, cs] * taps[k]
            o_ref[r0:r0 + CONV_RB, cs] = _silu(acc)

    return pl.pallas_call(
        kern, out_shape=S((T, 4096), f32), grid=(nt, 4096 // ct),
        in_specs=[cur, prev, nxt, pl.BlockSpec((8, ct), lambda i, j: (0, j)), pl.BlockSpec((1, ct), lambda i, j: (0, j))],
        out_specs=pl.BlockSpec((RT, ct), lambda i, j: (i, j)),
        scratch_shapes=[pltpu.VMEM((RT + 16, ct), f32)],
        compiler_params=_params("parallel", "parallel"), name="ssm_conv_fwd")(proj, proj, proj, w8, b)


def _ssm_conv_dpre(dxbc, proj, w8, b, nlx):
    T = proj.shape[0]
    nt = T // RT
    ct = 1024
    cur = pl.BlockSpec((RT, ct), lambda j, i: (i, j))
    pcur = pl.BlockSpec((RT, ct), lambda j, i: (i, _xbc_col(j)))
    prev = pl.BlockSpec((8, ct), lambda j, i: (jnp.maximum(i * (RT // 8) - 1, 0), _xbc_col(j)))
    nxt = pl.BlockSpec((8, ct), lambda j, i: (jnp.minimum((i + 1) * (RT // 8), T // 8 - 1), _xbc_col(j)))

    def kern(d_ref, cur_ref, prev_ref, next_ref, w_ref, b_ref, dpre_ref, dw_ref, db_ref, scr):
        i = pl.program_id(1)
        _fill_halo(scr, cur_ref, prev_ref, next_ref, i, nlx, nt)

        @pl.when(i == 0)
        def _():
            dw_ref[...] = jnp.zeros_like(dw_ref)
            db_ref[...] = jnp.zeros_like(db_ref)

        for cb in range(ct // 128):
            cs = slice(cb * 128, (cb + 1) * 128)
            db_acc = jnp.zeros((CONV_RB, 128), f32)
            dw_acc = [jnp.zeros((CONV_RB, 128), f32) for _ in range(SK)]
            for r0 in range(0, RT, CONV_RB):
                taps = _taps(scr, cs, r0, [k - 2 for k in range(SK)])
                pre = jnp.broadcast_to(b_ref[:, cs], (CONV_RB, 128))
                for k in range(SK):
                    pre = pre + w_ref[k:k + 1, cs] * taps[k]
                dpre = d_ref[r0:r0 + CONV_RB, cs] * _dsilu(pre, _sig(pre))
                dpre_ref[r0:r0 + CONV_RB, cs] = dpre
                db_acc = db_acc + dpre
                dw_acc = [dw_acc[k] + dpre * taps[k] for k in range(SK)]
            db_ref[:, cs] += jnp.sum(db_acc, axis=0, keepdims=True)
            for k in range(SK):
                dw_ref[k:k + 1, cs] += jnp.sum(dw_acc[k], axis=0, keepdims=True)

    return pl.pallas_call(
        kern, out_shape=(S((T, 4096), f32), S((8, 4096), f32), S((1, 4096), f32)), grid=(4096 // ct, nt),
        in_specs=[cur, pcur, prev, nxt, pl.BlockSpec((8, ct), lambda j, i: (0, j)), pl.BlockSpec((1, ct), lambda j, i: (0, j))],
        out_specs=(cur, pl.BlockSpec((8, ct), lambda j, i: (0, j)), pl.BlockSpec((1, ct), lambda j, i: (0, j))),
        scratch_shapes=[pltpu.VMEM((RT + 16, ct), f32)],
        compiler_params=_params("parallel", "arbitrary"), name="ssm_conv_dpre")(dxbc, proj, proj, proj, w8, b)


def _ssm_conv_t(dpre, w8, dproj, nlx):
    T = dpre.shape[0]
    nt = T // RT
    ct = 1024
    cur, prev, nxt = _halo_specs(T, ct)

    def kern(cur_ref, prev_ref, next_ref, w_ref, _alias, o_ref, scr):
        i = pl.program_id(0)
        _fill_halo(scr, cur_ref, prev_ref, next_ref, i, nlx, nt)
        for cs, r0 in _conv_blocks(ct):
            taps = _taps(scr, cs, r0, [2 - k for k in range(SK)])
            acc = jnp.zeros((CONV_RB, 128), f32)
            for k in range(SK):
                acc = acc + w_ref[k:k + 1, cs] * taps[k]
            o_ref[r0:r0 + CONV_RB, cs] = acc.astype(bf16)

    return pl.pallas_call(
        kern, out_shape=S(dproj.shape, bf16), grid=(nt, 4096 // ct),
        in_specs=[cur, prev, nxt, pl.BlockSpec((8, ct), lambda i, j: (0, j)), pl.BlockSpec(memory_space=pl.ANY)],
        out_specs=pl.BlockSpec((RT, ct), lambda i, j: (i, _xbc_col(j))),
        scratch_shapes=[pltpu.VMEM((RT + 16, ct), f32)], input_output_aliases={4: 0},
        compiler_params=_params("parallel", "parallel"), name="ssm_conv_t")(dpre, dpre, dpre, w8, dproj)


def _tri():
    li = lax.broadcasted_iota(jnp.int32, (Q, Q), 0)
    si = lax.broadcasted_iota(jnp.int32, (Q, Q), 1)
    return (si <= li).astype(bf16), (si >= li).astype(bf16)


def _dt_prep(proj, bias_row, alog_row):
    T = proj.shape[0]
    nch = T // Q

    def kern(raw_ref, b_ref, al_ref, dt_ref, la_ref):
        lane = lax.broadcasted_iota(jnp.int32, (Q, 128), 1)
        v = raw_ref[...] + b_ref[...]
        dt = jnp.maximum(v, 0.0) + jnp.log1p(jnp.exp(-jnp.abs(v)))
        a = jnp.where(lane[0:1, :] < 2 * NH, -jnp.exp(al_ref[...]), 0.0)
        da = dt * a
        tri, trit = _tri()
        dt_ref[...] = dt
        la_ref[...] = jnp.where(lane < NH, _dot3(tri, da), _dot3(trit, da))

    return pl.pallas_call(
        kern, out_shape=(S((T, 128), f32), S((T, 128), f32)), grid=(nch,),
        in_specs=[pl.BlockSpec((Q, 128), lambda c: (c, DT0 // 128)), _full((1, 128)), _full((1, 128))],
        out_specs=(pl.BlockSpec((Q, 128), lambda c: (c, 0)), pl.BlockSpec((Q, 128), lambda c: (c, 0))),
        compiler_params=_params("parallel"), name="dt_prep")(proj, bias_row, alog_row)


def _dt_bwd(a1, a2, r2, sv, dt, la, proj, bias_row, alog_row, dproj):
    T = proj.shape[0]
    nch = T // Q
    blk = pl.BlockSpec((Q, 128), lambda c: (c, 0))

    def kern(a1_ref, a2_ref, r2_ref, s_ref, dt_ref, la_ref, raw_ref, b_ref, al_ref, _alias, o_ref, db_ref, dal_ref):
        c = pl.program_id(0)

        @pl.when(c == 0)
        def _():
            db_ref[...] = jnp.zeros_like(db_ref)
            dal_ref[...] = jnp.zeros_like(dal_ref)

        lane = lax.broadcasted_iota(jnp.int32, (Q, 128), 1)
        row = lax.broadcasted_iota(jnp.int32, (Q, 128), 0)
        fwd = lane < NH
        dt = dt_ref[...]
        la = la_ref[...]
        a2v = a2_ref[...]
        r2v = r2_ref[...]
        a = jnp.where(lane[0:1, :] < 2 * NH, -jnp.exp(al_ref[...]), 0.0)
        la_e = jnp.where(fwd[0:1, :], la[Q - 1:Q, :], la[0:1, :])
        is_end = row == jnp.where(fwd, Q - 1, 0)
        e_end = jnp.exp(la_e - la)
        wend = e_end * dt
        extra = s_ref[0:1, :] * jnp.exp(la_e) + jnp.sum(wend * a2v, axis=0, keepdims=True)
        dla = a1_ref[...] - dt * r2v - wend * a2v + jnp.where(is_end, extra, 0.0)
        tri, trit = _tri()
        rcs = jnp.where(fwd, _dot3(trit, dla), _dot3(tri, dla))
        ddt = r2v + e_end * a2v + a * rcs
        dal_ref[...] += a * jnp.sum(dt * rcs, axis=0, keepdims=True)
        draw = jnp.where(lane < 2 * NH, ddt * _sig(raw_ref[...] + b_ref[...]), 0.0)
        db_ref[...] += jnp.sum(draw, axis=0, keepdims=True)
        o_ref[...] = jnp.zeros_like(o_ref)
        o_ref[:, 0:128] = draw.astype(bf16)

    return pl.pallas_call(
        kern, out_shape=(S(dproj.shape, bf16), S((1, 128), f32), S((1, 128), f32)), grid=(nch,),
        in_specs=[blk, blk, blk, blk, blk, blk, pl.BlockSpec((Q, 128), lambda c: (c, DT0 // 128)),
                  _full((1, 128)), _full((1, 128)), pl.BlockSpec(memory_space=pl.ANY)],
        out_specs=(pl.BlockSpec((Q, NP - DT0), lambda c: (c, DT0 // (NP - DT0))), _full((1, 128)), _full((1, 128))),
        input_output_aliases={9: 0},
        compiler_params=_params("arbitrary"), name="dt_bwd")(a1, a2, r2, sv, dt, la, proj, bias_row, alog_row, dproj)


def _split2(v):
    hi = v.astype(bf16)
    lo = (v - hi.astype(f32)).astype(bf16)
    return jnp.concatenate([hi, lo], axis=1)


def _scan_consts(rev):
    hoff = NH if rev else 0
    g = jnp.arange(NG, dtype=jnp.int32)[:, None, None]

    def rc(nr, ncol):
        return jnp.arange(nr, dtype=jnp.int32)[None, :, None], jnp.arange(ncol, dtype=jnp.int32)[None, None, :]

    r, c = rc(2 * 128, HPG * HD)
    sel_w = (lax.rem(r, 128) == hoff + HPG * g + c // HD).astype(bf16)
    r, c = rc(HPG * HD, 128)
    ind_h = (c == hoff + HPG * g + r // HD).astype(bf16)
    r, c = rc(2 * HPG * Q, 128)
    ind_e = (c == hoff + HPG * g + lax.rem(r, HPG * Q) // Q).astype(bf16)
    return sel_w, ind_h, ind_e


def _masks(rev):
    li = lax.broadcasted_iota(jnp.int32, (Q, Q), 0)
    si = lax.broadcasted_iota(jnp.int32, (Q, Q), 1)
    mask = (li <= si) if rev else (li >= si)
    mask_t = (li >= si) if rev else (li <= si)
    lane = lax.broadcasted_iota(jnp.int32, (Q, HPG * HD), 1)
    hms = [jnp.logical_and(lane >= r * HD, lane < (r + 1) * HD) for r in range(HPG)]
    return mask, mask_t, hms


def _mine(hoff):
    lane = lax.broadcasted_iota(jnp.int32, (Q, 128), 1)
    return jnp.logical_and(lane >= hoff, lane < hoff + NH)


def _head_row(vals, hc0):
    lane = lax.broadcasted_iota(jnp.int32, (1, HPG * HD), 1)
    out = jnp.zeros((1, HPG * HD), f32)
    for r in range(HPG):
        out = jnp.where(jnp.logical_and(lane >= r * HD, lane < (r + 1) * HD), vals[:, hc0 + r:hc0 + r + 1], out)
    return out


def _chunk_of(j, rev, nxc, nch):
    return (nch - 1 - j) if rev else lax.rem(j + nxc, nch)


def _ssd_fwd(xbc, dt, la, consts, rev, nxc, name, y_acc=None):
    T = xbc.shape[0]
    nch = T // Q
    hoff = NH if rev else 0
    e = 0 if rev else Q - 1
    cm = lambda j: _chunk_of(j, rev, nxc, nch)
    sel_w = consts[0]
    has_acc = y_acc is not None

    def kern(*refs):
        xbc_ref, dt_ref, la_ref, sw_ref = refs[:4]
        yacc_ref = refs[4] if has_acc else None
        y_ref, hp_ref, h_ref = refs[4 + has_acc:]
        j = pl.program_id(0)

        @pl.when(j == 0)
        def _():
            h_ref[...] = jnp.zeros_like(h_ref)

        hp_ref[...] = h_ref[...]
        mask, _, hms = _masks(rev)
        la_all = la_ref[...]
        dt_all = dt_ref[...]
        la_t = jnp.transpose(la_all)
        dt_t = jnp.transpose(dt_all)
        la_e = la_all[e:e + 1, :]
        w2 = _split2(jnp.exp(jnp.where(_mine(hoff), la_e - la_all, 0.0)) * dt_all)
        e2 = _split2(jnp.exp(la_all))
        ela_e = jnp.exp(la_e)
        for g in range(NG):
            hc0 = hoff + g * HPG
            x = xbc_ref[:, g * GW:(g + 1) * GW]
            bb = xbc_ref[:, B0 + g * NS:B0 + (g + 1) * NS].astype(bf16)
            cb = xbc_ref[:, C0 + g * NS:C0 + (g + 1) * NS].astype(bf16)
            ht = h_ref[g * NS:(g + 1) * NS, :]
            scores = _dot_nt(cb, bb)
            yoff = _dot(cb, ht.astype(bf16))
            wend = _dot(w2, sw_ref[g])
            expla = _dot(e2, sw_ref[g])
            mixes, xstack = [], []
            for r in range(HPG):
                hc = hc0 + r
                la_rep = jnp.broadcast_to(la_all[:, hc:hc + 1], (Q, 128))
                decay = jnp.exp(jnp.where(mask, la_rep - la_t[hc:hc + 1, :], NEG))
                mixes.append((scores * decay * dt_t[hc:hc + 1, :]).astype(bf16))
                xstack.append(jnp.where(hms[r], x, 0.0).astype(bf16))
            y = _dot(jnp.concatenate(mixes, axis=1), jnp.concatenate(xstack, axis=0)) + yoff * expla
            if has_acc:
                y = y + yacc_ref[:, g * GW:(g + 1) * GW]
            y_ref[:, g * GW:(g + 1) * GW] = y
            h_ref[g * NS:(g + 1) * NS, :] = ht * _head_row(ela_e, hc0) + _dot_tn(bb, (x * wend).astype(bf16))

    row = lambda j: (cm(j), 0)
    yblk = pl.BlockSpec((Q, DI), row)
    return pl.pallas_call(
        kern, out_shape=(S((T, DI), f32), S((nch, NG * NS, HPG * HD), f32)), grid=(nch,),
        in_specs=[pl.BlockSpec((Q, 4096), row), pl.BlockSpec((Q, 128), row), pl.BlockSpec((Q, 128), row),
                  _full(sel_w.shape)] + ([yblk] if has_acc else []),
        out_specs=(yblk, pl.BlockSpec((None, NG * NS, HPG * HD), lambda j: (cm(j), 0, 0))),
        scratch_shapes=[pltpu.VMEM((NG * NS, HPG * HD), f32)],
        input_output_aliases={4: 0} if has_acc else {},
        compiler_params=_params("arbitrary"), name=name)(xbc, dt, la, sel_w, *([y_acc] if has_acc else []))


def _ssd_bwd(xbc, dy, dt, la, hprev, dskip_full, consts, rev, nxc, name, acc=None):
    T = xbc.shape[0]
    nch = T // Q
    hoff = NH if rev else 0
    e = 0 if rev else Q - 1
    cm = lambda j: _chunk_of(nch - 1 - j, rev, nxc, nch)
    has_acc = acc is not None
    sel_w, ind_h, ind_e = consts

    def kern(*refs):
        xbc_ref, dy_ref, dt_ref, la_ref, hp_ref, dsk_ref, sw_ref, ih_ref, ie_ref = refs[:9]
        k = 9
        if has_acc:
            dxbc_in, a1_in, a2_in, r2_in, s_in = refs[k:k + 5]
            k += 5
        dxbc_ref, a1_ref, a2_ref, r2_ref, s_ref, g_ref, r2scr = refs[k:k + 7]
        j = pl.program_id(0)

        @pl.when(j == 0)
        def _():
            g_ref[...] = jnp.zeros_like(g_ref)

        mask, mask_t, hms = _masks(rev)
        la_all = la_ref[...]
        dt_all = dt_ref[...]
        la_t = jnp.transpose(la_all)
        dt_t = jnp.transpose(dt_all)
        la_e = la_all[e:e + 1, :]
        w2 = _split2(jnp.exp(jnp.where(_mine(hoff), la_e - la_all, 0.0)) * dt_all)
        e2 = _split2(jnp.exp(la_all))
        wed2 = jnp.concatenate([w2, e2, _split2(dt_all)], axis=0)
        ela_e = jnp.exp(la_e)
        r2scr[...] = jnp.zeros_like(r2scr)
        a1acc = jnp.zeros((Q, 128), f32)
        a2acc = jnp.zeros((Q, 128), f32)
        sacc = jnp.zeros((1, 128), f32)
        for g in range(NG):
            hc0 = hoff + g * HPG
            x = xbc_ref[:, g * GW:(g + 1) * GW]
            bb = xbc_ref[:, B0 + g * NS:B0 + (g + 1) * NS].astype(bf16)
            cb = xbc_ref[:, C0 + g * NS:C0 + (g + 1) * NS].astype(bf16)
            dyv = dy_ref[:, g * GW:(g + 1) * GW]
            gt = g_ref[g * NS:(g + 1) * NS, :]
            ht = hp_ref[g * NS:(g + 1) * NS, :]
            gtb = gt.astype(bf16)
            htb = ht.astype(bf16)
            xb = x.astype(bf16)
            scores = _dot_nt(cb, bb)
            scores_t = _dot_nt(bb, cb)
            bg = _dot(bb, gtb)
            yoff = _dot(cb, htb)
            sel3 = _dot(wed2, sw_ref[g])
            wend, expla, dtf = sel3[0:Q], sel3[Q:2 * Q], sel3[2 * Q:3 * Q]
            dym = jnp.concatenate([jnp.where(hms[r], dyv, 0.0).astype(bf16) for r in range(HPG)], axis=0)
            dyx_all = _dot_nt(dym, xb)
            sdts, ems = [], []
            wsum = jnp.zeros((Q, Q), f32)
            for r in range(HPG):
                hc = hc0 + r
                la_rep = jnp.broadcast_to(la_all[:, hc:hc + 1], (Q, 128))
                la_r = la_t[hc:hc + 1, :]
                dt_r = dt_t[hc:hc + 1, :]
                decay = jnp.exp(jnp.where(mask, la_rep - la_r, NEG))
                decay_t = jnp.exp(jnp.where(mask_t, la_r - la_rep, NEG))
                dyx = dyx_all[r * Q:(r + 1) * Q, :]
                fm = dyx * (scores * decay)
                r2scr[hc:hc + 1, :] = jnp.sum(fm, axis=0, keepdims=True)
                ems.append(fm * dt_r)
                wsum = wsum + dyx * decay * dt_r
                sdts.append((scores_t * decay_t).astype(bf16))
            dx = dtf * _dot(jnp.concatenate(sdts, axis=1), dym) + wend * bg
            if not has_acc:
                dx = dx + dsk_ref[:, g * GW:(g + 1) * GW] * dyv
            red3 = _dot(jnp.concatenate([(dyv * yoff * expla).astype(bf16), (x * bg).astype(bf16), (gt * ht).astype(bf16)],
                                        axis=0), ih_ref[g])
            a1acc = a1acc + _dot(_split2(jnp.concatenate(ems, axis=1)), ie_ref[g]) + red3[0:Q]
            a2acc = a2acc + red3[Q:2 * Q]
            sacc = sacc + jnp.sum(red3[2 * Q:3 * Q], axis=0, keepdims=True)
            wb = wsum.astype(bf16)
            dysb = (dyv * expla).astype(bf16)
            dc = _dot(wb, bb) + _dot_nt(dysb, htb)
            db = _dot_tn(wb, cb) + _dot_nt((x * wend).astype(bf16), gtb)
            g_ref[g * NS:(g + 1) * NS, :] = gt * _head_row(ela_e, hc0) + _dot_tn(cb, dysb)
            if has_acc:
                dx = dx + dxbc_in[:, g * GW:(g + 1) * GW]
                db = db + dxbc_in[:, B0 + g * NS:B0 + (g + 1) * NS]
                dc = dc + dxbc_in[:, C0 + g * NS:C0 + (g + 1) * NS]
            dxbc_ref[:, g * GW:(g + 1) * GW] = dx
            dxbc_ref[:, B0 + g * NS:B0 + (g + 1) * NS] = db
            dxbc_ref[:, C0 + g * NS:C0 + (g + 1) * NS] = dc
        r2c = jnp.transpose(r2scr[...])
        sc = jnp.broadcast_to(sacc, (Q, 128))
        if has_acc:
            a1acc = a1acc + a1_in[...]
            a2acc = a2acc + a2_in[...]
            r2c = r2c + r2_in[...]
            sc = sc + s_in[...]
        a1_ref[...] = a1acc
        a2_ref[...] = a2acc
        r2_ref[...] = r2c
        s_ref[...] = sc

    blk = pl.BlockSpec((Q, 128), lambda j: (cm(j), 0))
    big = pl.BlockSpec((Q, 4096), lambda j: (cm(j), 0))
    in_specs = [big, pl.BlockSpec((Q, DI), lambda j: (cm(j), 0)), blk, blk,
                pl.BlockSpec((None, NG * NS, HPG * HD), lambda j: (cm(j), 0, 0)), _full((1, DI)),
                _full(sel_w.shape), _full(ind_h.shape), _full(ind_e.shape)]
    args = [xbc, dy, dt, la, hprev, dskip_full, sel_w, ind_h, ind_e]
    aliases = {}
    if has_acc:
        in_specs += [big, blk, blk, blk, blk]
        args += list(acc)
        aliases = {9: 0, 10: 1, 11: 2, 12: 3, 13: 4}
    return pl.pallas_call(
        kern, out_shape=(S((T, 4096), f32), S((T, 128), f32), S((T, 128), f32), S((T, 128), f32), S((T, 128), f32)),
        grid=(nch,), in_specs=in_specs, out_specs=(big, blk, blk, blk, blk),
        scratch_shapes=[pltpu.VMEM((NG * NS, HPG * HD), f32), pltpu.VMEM((128, Q), f32)],
        input_output_aliases=aliases,
        compiler_params=_params("arbitrary"), name=name)(*args)


def _ynorm_fwd(ysum, xbc, proj, dskip_full, nw, L):
    nlx = L // RT

    def kern(ys_ref, xs_ref, z_ref, dsk_ref, nw_ref, y_ref, yn_ref, ynt_ref):
        y = ys_ref[...] + dsk_ref[...] * xs_ref[...]
        y_ref[...] = y
        yz = y * _silu(z_ref[...])
        for g in range(NG):
            sl = yz[:, g * GW:(g + 1) * GW]
            r = lax.rsqrt(jnp.mean(sl * sl, axis=1, keepdims=True) + EPS)
            yn = (sl * r) * nw_ref[:, g * GW:(g + 1) * GW]
            yn_ref[:, g * GW:(g + 1) * GW] = yn.astype(bf16)
            ynt_ref[g * GW:(g + 1) * GW, :] = jnp.transpose(yn).astype(bf16)

    blk = pl.BlockSpec((RT, DI), lambda i: (i, 0))
    return pl.pallas_call(
        kern, out_shape=(S((L, DI), f32), S((L, DI), bf16), S((DI, L), bf16)), grid=(nlx,),
        in_specs=[blk, blk, pl.BlockSpec((RT, DI), lambda i: (i, Z0 // DI)), _full((1, DI)), _full((1, DI))],
        out_specs=(blk, blk, pl.BlockSpec((DI, RT), lambda i: (0, i))),
        compiler_params=_params("parallel"), name="ynorm_fwd")(ysum, xbc, proj, dskip_full, nw)


def _ynorm_bwd(dyn, y, xbc, proj, dskip_full, nw, dproj):
    L = y.shape[0]
    T = proj.shape[0]
    nlx, nt = L // RT, T // RT

    def kern(dyn_ref, y_ref, xs_ref, z_ref, dsk_ref, nw_ref, _alias, dz_ref, dy_ref, dnw_ref, dsk_acc):
        i = pl.program_id(0)

        @pl.when(i == 0)
        def _():
            dnw_ref[...] = jnp.zeros_like(dnw_ref)
            dsk_acc[...] = jnp.zeros_like(dsk_acc)

        @pl.when(i >= nlx)
        def _():
            dz_ref[...] = jnp.zeros_like(dz_ref)
            dy_ref[...] = jnp.zeros_like(dy_ref)

        @pl.when(i < nlx)
        def _():
            y = y_ref[...]
            z = z_ref[...]
            sz = _sig(z)
            gz = z * sz
            yz = y * gz
            dynv = dyn_ref[...]
            for g in range(NG):
                cs = slice(g * 256, (g + 1) * 256)
                sl = yz[:, cs]
                r = lax.rsqrt(jnp.mean(sl * sl, axis=1, keepdims=True) + EPS)
                yhat = sl * r
                dn = dynv[:, cs]
                dnw_ref[:, cs] += jnp.sum(dn * yhat, axis=0, keepdims=True)
                dyh = dn * nw_ref[:, cs]
                dyz = r * (dyh - yhat * jnp.mean(dyh * yhat, axis=1, keepdims=True))
                dyv = dyz * gz[:, cs]
                dy_ref[:, cs] = dyv
                dz_ref[:, cs] = (dyz * y[:, cs] * _dsilu(z[:, cs], sz[:, cs])).astype(bf16)
                dsk_acc[:, cs] += jnp.sum(dyv * xs_ref[:, cs], axis=0, keepdims=True)

    xmap = lambda i: (jnp.minimum(i, nlx - 1), 0)
    return pl.pallas_call(
        kern, out_shape=(S(dproj.shape, bf16), S((T, DI), f32), S((1, DI), f32), S((1, DI), f32)), grid=(nt,),
        in_specs=[pl.BlockSpec((RT, DI), xmap), pl.BlockSpec((RT, DI), xmap), pl.BlockSpec((RT, DI), xmap),
                  pl.BlockSpec((RT, DI), lambda i: (jnp.minimum(i, nlx - 1), Z0 // DI)), _full((1, DI)), _full((1, DI)),
                  pl.BlockSpec(memory_space=pl.ANY)],
        out_specs=(pl.BlockSpec((RT, DI), lambda i: (i, Z0 // DI)), pl.BlockSpec((RT, DI), lambda i: (i, 0)),
                   _full((1, DI)), _full((1, DI))),
        input_output_aliases={6: 0},
        compiler_params=_params("arbitrary"), name="ynorm_bwd")(dyn, y, xbc, proj, dskip_full, nw, dproj)


def _head_sums(cols):
    def kern(c_ref, o_ref):
        o_ref[...] = jnp.broadcast_to(jnp.sum(c_ref[...], axis=1, keepdims=True), (NH, 128))

    return pl.pallas_call(kern, out_shape=S((NH, 128), f32), name="head_sums")(cols)


SEG_STRIDE = 96
SEG_PAD = 16
NSEG = RT // GRID_W
CONF_ROWS = SEG_PAD + NSEG * SEG_STRIDE


SHIFT_ROWS = CONF_ROWS - 8
CONF_CW = 256


CONF_RB = 32


def _seg_zero_pads(scr):
    scr[0:SEG_PAD, :] = jnp.zeros((SEG_PAD, scr.shape[1]), f32)
    for s in range(NSEG):
        lo = SEG_PAD + s * SEG_STRIDE + GRID_W
        scr[lo:lo + SEG_STRIDE - GRID_W, :] = jnp.zeros((SEG_STRIDE - GRID_W, scr.shape[1]), f32)


def _seg_row(r0):
    return SEG_PAD + (r0 // GRID_W) * SEG_STRIDE + r0 % GRID_W


def _shift_copies(cps, scr, cs):
    full = scr[:, cs]
    for s in range(1, 8):
        cps[s - 1, :, :] = pltpu.roll(full, CONF_ROWS - s, 0)[0:SHIFT_ROWS, :]


def _tap(cps, scr, cs, o):
    rs = o % 8
    return scr[pl.ds(o, GRID_W), cs] if rs == 0 else cps[rs - 1, pl.ds(o - rs, GRID_W), :]


def _conf_fwd(proj, w32, cb, lnw, lnb, L):
    nlx = L // RT

    def kern(v_ref, g_ref, cg_ref, w_ref, cb_ref, lnw_ref, lnb_ref, u1_ref, u3_ref, u3t_ref, scr, cps, u3_scr):
        _seg_zero_pads(scr)
        for r0 in range(0, RT, CONF_RB):
            rows = slice(r0, r0 + CONF_RB)
            scr[_seg_row(r0):_seg_row(r0) + CONF_RB, :] = v_ref[rows, :] * _sig(g_ref[rows, :])
        for cc in range(D // CONF_CW):
            cs = slice(cc * CONF_CW, (cc + 1) * CONF_CW)
            _shift_copies(cps, scr, cs)
            for s in range(NSEG):
                acc = jnp.broadcast_to(cb_ref[:, cs], (GRID_W, CONF_CW))
                for k in range(CK):
                    acc = acc + w_ref[k:k + 1, cs] * _tap(cps, scr, cs, SEG_PAD + s * SEG_STRIDE + k - CK // 2)
                u1_ref[s * GRID_W:(s + 1) * GRID_W, cs] = acc
        for r0 in range(0, RT, CONF_RB):
            rows = slice(r0, r0 + CONF_RB)
            u1 = u1_ref[rows, :]
            xc = u1 - jnp.mean(u1, axis=1, keepdims=True)
            r = lax.rsqrt(jnp.mean(xc * xc, axis=1, keepdims=True) + EPS)
            u2 = (xc * r) * lnw_ref[...] + lnb_ref[...]
            u3 = _silu(u2) * _silu(cg_ref[rows, :])
            u3_ref[rows, :] = u3.astype(bf16)
            u3_scr[rows, :] = u3
        u3t_ref[...] = jnp.transpose(u3_scr[...]).astype(bf16)

    blk = pl.BlockSpec((RT, D), lambda i: (i, 0))
    return pl.pallas_call(
        kern, out_shape=(S((L, D), f32), S((L, D), bf16), S((D, L), bf16)), grid=(nlx,),
        in_specs=[pl.BlockSpec((RT, D), lambda i: (i, GV0 // D)), pl.BlockSpec((RT, D), lambda i: (i, GG0 // D)),
                  pl.BlockSpec((RT, D), lambda i: (i, CG0 // D)), _full((32, D)), _full((1, D)), _full((1, D)), _full((1, D))],
        out_specs=(blk, blk, pl.BlockSpec((D, RT), lambda i: (0, i))),
        scratch_shapes=[pltpu.VMEM((CONF_ROWS, D), f32), pltpu.VMEM((7, SHIFT_ROWS, CONF_CW), f32), pltpu.VMEM((RT, D), f32)],
        compiler_params=_params("parallel"), name="conf_fwd")(proj, proj, proj, w32, cb, lnw, lnb)


def _conf_bwd(du3, u1, proj, w32, lnw, lnb, dproj):
    L = u1.shape[0]
    T = proj.shape[0]
    nlx, nt = L // RT, T // RT

    def kern(du3_ref, u1_ref, v_ref, g_ref, cg_ref, w_ref, lnw_ref, lnb_ref, _alias,
             o_ref, dw_ref, dcb_ref, dlw_ref, dlb_ref, scr_u, scr_d, du0_scr, cps_u, cps_d):
        i = pl.program_id(0)

        @pl.when(i == 0)
        def _():
            dw_ref[...] = jnp.zeros_like(dw_ref)
            dcb_ref[...] = jnp.zeros_like(dcb_ref)
            dlw_ref[...] = jnp.zeros_like(dlw_ref)
            dlb_ref[...] = jnp.zeros_like(dlb_ref)

        @pl.when(i >= nlx)
        def _():
            o_ref[...] = jnp.zeros_like(o_ref)

        @pl.when(i < nlx)
        def _():
            _seg_zero_pads(scr_u)
            _seg_zero_pads(scr_d)
            for r0 in range(0, RT, CONF_RB):
                rows = slice(r0, r0 + CONF_RB)
                cg = cg_ref[rows, :]
                scg = _sig(cg)
                u1 = u1_ref[rows, :]
                xc = u1 - jnp.mean(u1, axis=1, keepdims=True)
                r = lax.rsqrt(jnp.mean(xc * xc, axis=1, keepdims=True) + EPS)
                xhat = xc * r
                u2 = xhat * lnw_ref[...] + lnb_ref[...]
                s2 = _sig(u2)
                du3v = du3_ref[rows, :]
                du2 = du3v * (cg * scg) * _dsilu(u2, s2)
                o_ref[rows, 2 * D:3 * D] = (du3v * (u2 * s2) * _dsilu(cg, scg)).astype(bf16)
                dlw_ref[...] += jnp.sum(du2 * xhat, axis=0, keepdims=True)
                dlb_ref[...] += jnp.sum(du2, axis=0, keepdims=True)
                dxh = du2 * lnw_ref[...]
                du1 = r * (dxh - jnp.mean(dxh, axis=1, keepdims=True) - xhat * jnp.mean(dxh * xhat, axis=1, keepdims=True))
                dcb_ref[...] += jnp.sum(du1, axis=0, keepdims=True)
                scr_u[_seg_row(r0):_seg_row(r0) + CONF_RB, :] = v_ref[rows, :] * _sig(g_ref[rows, :])
                scr_d[_seg_row(r0):_seg_row(r0) + CONF_RB, :] = du1
            for cc in range(D // CONF_CW):
                cs = slice(cc * CONF_CW, (cc + 1) * CONF_CW)
                _shift_copies(cps_u, scr_u, cs)
                _shift_copies(cps_d, scr_d, cs)
                for k in range(CK):
                    t = jnp.zeros((GRID_W, CONF_CW), f32)
                    for s in range(NSEG):
                        base = SEG_PAD + s * SEG_STRIDE
                        t = t + scr_d[pl.ds(base, GRID_W), cs] * _tap(cps_u, scr_u, cs, base + k - CK // 2)
                    dw_ref[k:k + 1, cs] += jnp.sum(t, axis=0, keepdims=True)
                for s in range(NSEG):
                    base = SEG_PAD + s * SEG_STRIDE
                    acc = jnp.zeros((GRID_W, CONF_CW), f32)
                    for k in range(CK):
                        acc = acc + w_ref[k:k + 1, cs] * _tap(cps_d, scr_d, cs, base + CK // 2 - k)
                    du0_scr[s * GRID_W:(s + 1) * GRID_W, cs] = acc
            for r0 in range(0, RT, CONF_RB):
                rows = slice(r0, r0 + CONF_RB)
                du0 = du0_scr[rows, :]
                sg = _sig(g_ref[rows, :])
                o_ref[rows, 0:D] = (du0 * sg).astype(bf16)
                o_ref[rows, D:2 * D] = (du0 * v_ref[rows, :] * sg * (1.0 - sg)).astype(bf16)

    xmap = lambda i: (jnp.minimum(i, nlx - 1), 0)
    pmap = lambda cb: (lambda i: (jnp.minimum(i, nlx - 1), cb))
    return pl.pallas_call(
        kern, out_shape=(S(dproj.shape, bf16), S((32, D), f32), S((1, D), f32), S((1, D), f32), S((1, D), f32)), grid=(nt,),
        in_specs=[pl.BlockSpec((RT, D), xmap), pl.BlockSpec((RT, D), xmap),
                  pl.BlockSpec((RT, D), pmap(GV0 // D)), pl.BlockSpec((RT, D), pmap(GG0 // D)), pl.BlockSpec((RT, D), pmap(CG0 // D)),
                  _full((32, D)), _full((1, D)), _full((1, D)), pl.BlockSpec(memory_space=pl.ANY)],
        out_specs=(pl.BlockSpec((RT, 3 * D), lambda i: (i, GV0 // (3 * D))), _full((32, D)), _full((1, D)), _full((1, D)), _full((1, D))),
        scratch_shapes=[pltpu.VMEM((CONF_ROWS, D), f32), pltpu.VMEM((CONF_ROWS, D), f32), pltpu.VMEM((RT, D), f32),
                        pltpu.VMEM((7, SHIFT_ROWS, CONF_CW), f32), pltpu.VMEM((7, SHIFT_ROWS, CONF_CW), f32)],
        input_output_aliases={8: 0},
        compiler_params=_params("arbitrary"), name="conf_bwd")(du3, u1, proj, proj, proj, w32, lnw, lnb, dproj)


def _merge_fwd(bs, bc, proj):
    L = bs.shape[0]

    def kern(bs_ref, bc_ref, g1_ref, g2_ref, o_ref, ot_ref):
        mv = _sig(g1_ref[...]) * bs_ref[...] + _sig(g2_ref[...]) * bc_ref[...]
        o_ref[...] = mv.astype(bf16)
        ot_ref[...] = jnp.transpose(mv).astype(bf16)

    blk = pl.BlockSpec((RT, D), lambda i: (i, 0))
    return pl.pallas_call(
        kern, out_shape=(S((L, D), bf16), S((D, L), bf16)), grid=(L // RT,),
        in_specs=[blk, blk, pl.BlockSpec((RT, D), lambda i: (i, G10 // D)), pl.BlockSpec((RT, D), lambda i: (i, G20 // D))],
        out_specs=(blk, pl.BlockSpec((D, RT), lambda i: (0, i))),
        compiler_params=_params("parallel"), name="merge_fwd")(bs, bc, proj, proj)


def _merge_bwd(dmerged, bs, bc, proj):
    L = bs.shape[0]
    T = proj.shape[0]
    nlx, nt = L // RT, T // RT

    def kern(dm_ref, bs_ref, bc_ref, g1_ref, g2_ref, o_ref, dbs_ref, dbc_ref):
        i = pl.program_id(0)

        @pl.when(i >= nlx)
        def _():
            o_ref[...] = jnp.zeros_like(o_ref)

        @pl.when(i < nlx)
        def _():
            dm = dm_ref[...]
            s1 = _sig(g1_ref[...])
            s2 = _sig(g2_ref[...])
            dbs_ref[...] = (dm * s1).astype(bf16)
            dbc_ref[...] = (dm * s2).astype(bf16)
            o_ref[:, 0:D] = (dm * bs_ref[...] * s1 * (1.0 - s1)).astype(bf16)
            o_ref[:, D:2 * D] = (dm * bc_ref[...] * s2 * (1.0 - s2)).astype(bf16)

    xmap = lambda i: (jnp.minimum(i, nlx - 1), 0)
    pmap = lambda cb: (lambda i: (jnp.minimum(i, nlx - 1), cb))
    xblk = pl.BlockSpec((RT, D), xmap)
    return pl.pallas_call(
        kern, out_shape=(S((T, NP), bf16), S((L, D), bf16), S((L, D), bf16)), grid=(nt,),
        in_specs=[xblk, xblk, xblk, pl.BlockSpec((RT, D), pmap(G10 // D)), pl.BlockSpec((RT, D), pmap(G20 // D))],
        out_specs=(pl.BlockSpec((RT, 2 * D), lambda i: (i, G10 // (2 * D))), xblk, xblk),
        compiler_params=_params("arbitrary"), name="merge_bwd")(dmerged, bs, bc, proj, proj)


def _final(x, out, target, mod, fw):
    L = x.shape[0]

    def kern(x_ref, o_ref, t_ref, mod_ref, fw_ref, dx1_ref, dout_ref, loss_ref, dfw_ref, dg_ref):
        i = pl.program_id(0)

        @pl.when(i == 0)
        def _():
            loss_ref[...] = jnp.zeros_like(loss_ref)
            dfw_ref[...] = jnp.zeros_like(dfw_ref)
            dg_ref[...] = jnp.zeros_like(dg_ref)

        gate = mod_ref[0:1, 2 * D:3 * D]
        ov = o_ref[...]
        x1 = x_ref[...] + gate * ov
        r = lax.rsqrt(jnp.mean(x1 * x1, axis=1, keepdims=True) + EPS)
        xn = x1 * r
        fw = fw_ref[...]
        err = xn * fw - t_ref[...]
        part = 0.5 * jnp.sum(jnp.mean(err * err, axis=1, keepdims=True), axis=0, keepdims=True)
        loss_ref[...] += jnp.broadcast_to(part, (8, 128))
        dy = err * (1.0 / D)
        dfw_ref[...] += jnp.sum(dy * xn, axis=0, keepdims=True)
        dyw = dy * fw
        dx1 = r * (dyw - xn * jnp.mean(dyw * xn, axis=1, keepdims=True))
        dx1_ref[...] = dx1
        dout_ref[...] = (gate * dx1).astype(bf16)
        dg_ref[...] += jnp.sum(dx1 * ov, axis=0, keepdims=True)

    blk = pl.BlockSpec((RT, D), lambda i: (i, 0))
    return pl.pallas_call(
        kern, out_shape=(S((L, D), f32), S((L, D), bf16), S((8, 128), f32), S((1, D), f32), S((1, D), f32)), grid=(L // RT,),
        in_specs=[blk, blk, blk, _full((8, 3 * D)), _full((1, D))],
        out_specs=(blk, blk, _full((8, 128)), _full((1, D)), _full((1, D))),
        compiler_params=_params("arbitrary"), name="final")(x, out, target, mod, fw)


def _me():
    return 4 * lax.axis_index("x") + 2 * lax.axis_index("y") + lax.axis_index("c")


def _xchg_copy(ins, outs, send_sems, recv_sems, modes, a, k, me):
    peer = lax.rem(me + k, N_DEV)
    pid = (peer // 4, lax.rem(peer // 2, 2), lax.rem(peer, 2))
    src = ins[a].at[peer] if modes[a] else ins[a]
    return pltpu.make_async_remote_copy(src_ref=src, dst_ref=outs[a].at[me], send_sem=send_sems.at[a, k - 1],
                                        recv_sem=recv_sems.at[a, k - 1], device_id=pid, device_id_type=MESH)


def _xchg_local(ins, outs, loc_sems, modes, a, me):
    return pltpu.make_async_copy(ins[a].at[me] if modes[a] else ins[a], outs[a].at[me], loc_sems.at[a])


def _xchg_start(ins, outs, send_sems, recv_sems, loc_sems, modes):
    me = _me()
    for a in range(len(modes)):
        _xchg_local(ins, outs, loc_sems, modes, a, me).start()
        for k in range(1, N_DEV):
            _xchg_copy(ins, outs, send_sems, recv_sems, modes, a, k, me).start()


def _xchg_wait(ins, outs, send_sems, recv_sems, loc_sems, modes):
    me = _me()
    for a in range(len(modes)):
        for k in range(1, N_DEV):
            frm = lax.rem(me + N_DEV - k, N_DEV)
            src = ins[a].at[frm] if modes[a] else ins[a]
            pltpu.make_async_remote_copy(src_ref=src, dst_ref=outs[a].at[frm], send_sem=send_sems.at[a, k - 1],
                                         recv_sem=recv_sems.at[a, k - 1], device_id=(0, 0, 0), device_id_type=MESH).wait_recv()
    for a in range(len(modes)):
        for k in range(1, N_DEV):
            _xchg_copy(ins, outs, send_sems, recv_sems, modes, a, k, me).wait_send()
        _xchg_local(ins, outs, loc_sems, modes, a, me).wait()


def _xchg_out_shapes(arrs, modes):
    return tuple(S((N_DEV,) + (a.shape[1:] if sc else a.shape), a.dtype) for a, sc in zip(arrs, modes))


def _xchg_sems(n):
    return [pltpu.SemaphoreType.DMA((n, N_DEV - 1)), pltpu.SemaphoreType.DMA((n, N_DEV - 1)), pltpu.SemaphoreType.DMA((n,))]


def _exchange(arrs, modes, name):
    n = len(arrs)

    def kern(*refs):
        ins, outs, sems = refs[:n], refs[n:2 * n], refs[2 * n:]
        _xchg_start(ins, outs, *sems, modes)
        _xchg_wait(ins, outs, *sems, modes)

    anyspec = pl.BlockSpec(memory_space=pl.ANY)
    return pl.pallas_call(
        kern, out_shape=_xchg_out_shapes(arrs, modes), in_specs=[anyspec] * n, out_specs=tuple([anyspec] * n),
        scratch_shapes=_xchg_sems(n), name=name)(*arrs)


def _gather2(arrs, name):
    n = len(arrs)

    def kern(*refs):
        ins, outs = refs[:n], refs[n:2 * n]
        send_sems, recv_sems, loc_sems = refs[2 * n:]
        x, y, c = lax.axis_index("x"), lax.axis_index("y"), lax.axis_index("c")
        me, sib = (x, y, c), (x, y, 1 - c)
        chips = [(1 - x, y), (x, 1 - y), (1 - x, 1 - y)]

        def slot(a, p):
            return outs[a].at[4 * p[0] + 2 * p[1] + p[2]]

        def cp(a, k, block, to, own=False):
            return pltpu.make_async_remote_copy(src_ref=ins[a] if own else slot(a, block), dst_ref=slot(a, block),
                                                send_sem=send_sems.at[a, k], recv_sem=recv_sems.at[a, k],
                                                device_id=to, device_id_type=MESH)

        started = []
        for a in range(n):
            loc = pltpu.make_async_copy(ins[a], slot(a, me), loc_sems.at[a])
            loc.start()
            started.append(cp(a, 0, me, sib, own=True))
            started += [cp(a, 1 + j, me, (*chip, c), own=True) for j, chip in enumerate(chips)]
        for s in started:
            s.start()
        for j, chip in enumerate(chips):
            for a in range(n):
                cp(a, 1 + j, (*chip, c), me).wait_recv()
                fwd = cp(a, 4 + j, (*chip, c), sib)
                fwd.start()
                started.append(fwd)
        for a in range(n):
            cp(a, 0, sib, me).wait_recv()
            for j, chip in enumerate(chips):
                cp(a, 4 + j, (*chip, 1 - c), me).wait_recv()
        for s in started:
            s.wait_send()
        for a in range(n):
            pltpu.make_async_copy(ins[a], slot(a, me), loc_sems.at[a]).wait()

    anyspec = pl.BlockSpec(memory_space=pl.ANY)
    return pl.pallas_call(
        kern, out_shape=_xchg_out_shapes(arrs, (False,) * n), in_specs=[anyspec] * n, out_specs=tuple([anyspec] * n),
        scratch_shapes=[pltpu.SemaphoreType.DMA((n, 7)), pltpu.SemaphoreType.DMA((n, 7)), pltpu.SemaphoreType.DMA((n,))],
        name=name)(*arrs)


def _adamw(parts, w, m, v, name):
    r, c = w.shape
    n_parts = parts.shape[0]
    tr = r
    for cand in (128, 64, 32, 16, 8):
        if r % cand == 0 and r > cand:
            tr = cand
            break
    c1 = 1.0 / (1.0 - ADAM_B1 ** ADAM_STEP)
    c2 = 1.0 / (1.0 - ADAM_B2 ** ADAM_STEP)

    def kern(p_ref, w_ref, m_ref, v_ref, g_ref, d_ref, m2_ref, v2_ref):
        g = p_ref[0].astype(f32)
        for i in range(1, n_parts):
            g = g + p_ref[i].astype(f32)
        g_ref[...] = g
        m2 = ADAM_B1 * m_ref[...] + (1.0 - ADAM_B1) * g
        v2 = ADAM_B2 * v_ref[...] + (1.0 - ADAM_B2) * (g * g)
        m2_ref[...] = m2
        v2_ref[...] = v2
        d_ref[...] = -ADAM_LR * ((m2 * c1) / (jnp.sqrt(v2 * c2) + ADAM_EPS) + ADAM_WD * w_ref[...])

    blk = pl.BlockSpec((tr, c), lambda i: (i, 0))
    sh = S((r, c), f32)
    return pl.pallas_call(
        kern, out_shape=(sh, sh, sh, sh), grid=(r // tr,),
        in_specs=[pl.BlockSpec((n_parts, tr, c), lambda i: (0, i, 0)), blk, blk, blk], out_specs=(blk, blk, blk, blk),
        compiler_params=_params("parallel"), name=name)(parts, w, m, v)


_SMALL = (("c_ctx", 1024), ("b_mod", 3072), ("norm_w", 1024), ("ssm_conv_b", 4096), ("dt_bias", 64), ("a_log", 64),
          ("d_skip", 32), ("ssm_norm_w", 2048), ("conf_conv_b", 1024), ("conf_ln_w", 1024), ("conf_ln_b", 1024),
          ("final_norm_w", 1024))
SMALL_TILE = 8 * 128


def _pack_small(d):
    rows = []
    for name, n in _SMALL:
        v = d[name].reshape(-1).astype(f32)
        pad = (-n) % SMALL_TILE
        if pad:
            v = jnp.concatenate([v, jnp.zeros((pad,), f32)])
        rows.append(v.reshape(-1, 128))
    return jnp.concatenate(rows, axis=0)


def _unpack_small(p, shapes):
    out, r0 = {}, 0
    for name, n in _SMALL:
        nr = 8 * ((n + SMALL_TILE - 1) // SMALL_TILE)
        out[name] = p[r0:r0 + nr].reshape(-1)[:n].reshape(shapes[name])
        r0 += nr
    return out


def _permute_w_in(w):
    return jnp.concatenate([w[:, 9280:11328], w[:, 4160:6208], w[:, 0:2048], w[:, 6208:9280], w[:, 2048:4096],
                            w[:, 4096:4160], jnp.zeros((w.shape[0], NP - DT0 - 64), w.dtype)], axis=1)


def _unpermute_w_in(wp):
    return jnp.concatenate([wp[:, PX0:PX0 + 2048], wp[:, PBC0:PBC0 + 2048], wp[:, DT0:DT0 + 64], wp[:, Z0:Z0 + 2048],
                            wp[:, GV0:GV0 + 3072], wp[:, G10:G10 + 2048]], axis=1)


def _cols_gathered(g):
    return jnp.transpose(g, (1, 0, 2)).reshape(g.shape[1], N_DEV * g.shape[2])


def _cols_to_blocks(a):
    r, c8 = a.shape
    return jnp.transpose(a.reshape(r, N_DEV, c8 // N_DEV), (1, 0, 2))


def kernel(x, c, ctx, c_ctx, w_mod, b_mod, norm_w, w_in, ssm_conv_w, ssm_conv_b, dt_bias, a_log, d_skip, ssm_norm_w, w_out_ssm, conf_conv_w, conf_conv_b, conf_ln_w, conf_ln_b, w_out_conf, w_out, final_norm_w, loss_target, m_c_ctx, m_w_mod, m_b_mod, m_norm_w, m_w_in, m_ssm_conv_w, m_ssm_conv_b, m_dt_bias, m_a_log, m_d_skip, m_ssm_norm_w, m_w_out_ssm, m_conf_conv_w, m_conf_conv_b, m_conf_ln_w, m_conf_ln_b, m_w_out_conf, m_w_out, m_final_norm_w, v_c_ctx, v_w_mod, v_b_mod, v_norm_w, v_w_in, v_ssm_conv_w, v_ssm_conv_b, v_dt_bias, v_a_log, v_d_skip, v_ssm_norm_w, v_w_out_ssm, v_conf_conv_w, v_conf_conv_b, v_conf_ln_w, v_conf_ln_b, v_w_out_conf, v_w_out, v_final_norm_w):
    L = x.shape[1]
    Lc = ctx.shape[1]
    T = L + Lc
    nlx = L // RT
    nxc = L // Q
    x2 = x.reshape(L, D)
    ctx2 = ctx.reshape(Lc, D)
    tgt = loss_target.reshape(L, D)

    gathered = _gather2([w_in[0].astype(bf16), w_mod[0].astype(bf16), ssm_conv_w[0], conf_conv_w[0]], name="gather_weights")
    wp = _permute_w_in(_cols_gathered(gathered[0]))
    wmod_bf = _cols_gathered(gathered[1])
    scw8 = jnp.concatenate([_cols_gathered(gathered[2]), jnp.zeros((8 - SK, 4096), f32)], axis=0)
    ccw32 = jnp.concatenate([_cols_gathered(gathered[3]), jnp.zeros((32 - CK, D), f32)], axis=0)

    norm_w1 = norm_w.reshape(1, D)
    scb = ssm_conv_b.reshape(1, 4096)
    bias_row = jnp.concatenate([dt_bias.reshape(1, 2 * NH), jnp.zeros((1, 128 - 2 * NH), f32)], axis=1)
    alog_row = jnp.concatenate([a_log.reshape(1, 2 * NH), jnp.zeros((1, 128 - 2 * NH), f32)], axis=1)
    dskip_full = jnp.repeat(d_skip.reshape(NH), HD).reshape(1, DI)
    snw = ssm_norm_w.reshape(1, DI)
    ccb = conf_conv_b.reshape(1, D)
    lnw = conf_ln_w.reshape(1, D)
    lnb = conf_ln_b.reshape(1, D)
    fw = final_norm_w.reshape(1, D)

    cc8 = jnp.concatenate([c.reshape(1, D), c_ctx.reshape(1, D), jnp.zeros((6, D), f32)], axis=0)
    mod, silu_rows = _mod_fwd(cc8, wmod_bf, b_mod.reshape(1, 3 * D))
    h, h_t = _prenorm(x2, ctx2, norm_w1, mod)
    proj, wos_g, woc_g, wo_g = _matmul(
        h, wp, f32, "proj_gather", tn=NP // 5,
        comm=([w_out_ssm[0].astype(bf16), w_out_conf[0].astype(bf16), w_out[0].astype(bf16)], (False,) * 3))
    wos_bf = wos_g.reshape(DI, D)
    woc_bf = woc_g.reshape(D, D)
    wo_bf = wo_g.reshape(D, D)
    xbc = _ssm_conv_fwd(proj, scw8, scb, nlx)
    dt, la = _dt_prep(proj, bias_row, alog_row)
    consts_f, consts_b = _scan_consts(False), _scan_consts(True)
    yf, hp_f = _ssd_fwd(xbc, dt, la, consts_f, False, nxc, "ssd_fwd_f")
    ysum, hp_b = _ssd_fwd(xbc, dt, la, consts_b, True, nxc, "ssd_fwd_b", y_acc=yf)
    y, yn, yn_t = _ynorm_fwd(ysum, xbc, proj, dskip_full, snw, L)
    bs = _matmul(yn, wos_bf, f32, "branch_ssm", tm=1024, tk=2048)
    u1, u3, u3_t = _conf_fwd(proj, ccw32, ccb, lnw, lnb, L)
    bc = _matmul(u3, woc_bf, f32, "branch_conf", tm=2048)
    merged, merged_t = _merge_fwd(bs, bc, proj)
    out = _matmul(merged, wo_bf, f32, "out_proj", tm=2048)
    dx1, dout, loss_acc, dfw, dgate = _final(x2, out, tgt, mod, fw)

    dmerged = _matmul(dout, wo_bf, f32, "d_merged", tb=True, tm=2048)
    g_wo = _matmul(merged_t, dout, bf16, "g_w_out", tm=1024, tk=2048)
    dproj, dbs, dbc = _merge_bwd(dmerged, bs, bc, proj)
    dyn = _matmul(dbs, wos_bf, f32, "d_yn", tb=True, tm=1024, tn=2048)
    g_wos = _matmul(yn_t, dbs, bf16, "g_w_out_ssm", tm=1024, tk=2048)
    du3 = _matmul(dbc, woc_bf, f32, "d_u3", tb=True, tm=2048)
    g_woc = _matmul(u3_t, dbc, bf16, "g_w_out_conf", tm=1024, tk=2048)
    dproj, g_ccw, g_ccb, g_lnw, g_lnb = _conf_bwd(du3, u1, proj, ccw32, lnw, lnb, dproj)
    dproj, dy, g_snw, dsk_cols = _ynorm_bwd(dyn, y, xbc, proj, dskip_full, snw, dproj)
    acc_f = _ssd_bwd(xbc, dy, dt, la, hp_f, dskip_full, consts_f, False, nxc, "ssd_bwd_f")
    dxbc, a1, a2, r2, sv = _ssd_bwd(xbc, dy, dt, la, hp_b, dskip_full, consts_b, True, nxc, "ssd_bwd_b", acc=acc_f)
    dproj, g_dtb, g_alog = _dt_bwd(a1, a2, r2, sv, dt, la, proj, bias_row, alog_row, dproj)
    dpre, g_scw, g_scb = _ssm_conv_dpre(dxbc, proj, scw8, scb, nlx)
    dproj = _ssm_conv_t(dpre, scw8, dproj, nlx)
    g_wp, *parts_b = _matmul(
        h_t, dproj, bf16, "g_w_in_scatter", tm=1024, tn=NP // 5,
        comm=([g_wos.reshape(N_DEV, DI // N_DEV, D), g_woc.reshape(N_DEV, D // N_DEV, D), g_wo.reshape(N_DEV, D // N_DEV, D),
               _cols_to_blocks(g_scw[:SK]), _cols_to_blocks(g_ccw[:CK])], (True,) * 5))
    dh, parts_a = _matmul(dproj, wp, f32, "d_h_scatter", tb=True, tk=NP // 5,
                          comm=([_cols_to_blocks(_unpermute_w_in(g_wp))], (True,)))
    parts = [parts_a] + parts_b
    gx, g_nw, macc = _prenorm_bwd(x2, ctx2, dh, dx1, norm_w1, mod)
    dmod_x = jnp.concatenate([macc[0:1], macc[1:2], dgate], axis=1)
    dmod_c = jnp.concatenate([macc[2:3], macc[3:4], jnp.zeros((1, D), f32)], axis=1)
    dmod8 = jnp.concatenate([dmod_x, dmod_c, jnp.zeros((6, 3 * D), f32)], axis=0)
    ct = jnp.concatenate([c.reshape(D, 1), c_ctx.reshape(D, 1), jnp.zeros((D, 126), f32)], axis=1)
    g_bmod, g_cctx = _mod_bwd(ct, dmod8, wmod_bf)
    g_dskip = _head_sums(dsk_cols.reshape(NH, HD))[:, 0]

    small_g = _pack_small({
        "c_ctx": g_cctx[:, 0], "b_mod": g_bmod, "norm_w": g_nw, "ssm_conv_b": g_scb, "dt_bias": g_dtb[0, :2 * NH],
        "a_log": g_alog[0, :2 * NH], "d_skip": g_dskip, "ssm_norm_w": g_snw, "conf_conv_b": g_ccb, "conf_ln_w": g_lnw,
        "conf_ln_b": g_lnb, "final_norm_w": dfw})
    fac = jnp.concatenate([silu_rows[0:1].reshape(D // 128, 128), dmod_x.reshape(3 * D // 128, 128),
                           dmod_c.reshape(3 * D // 128, 128)], axis=0)
    small_parts, fac_all = _exchange([small_g, fac], (False, False), name="exchange_tail")
    nr = D // 128
    sct = jnp.concatenate([fac_all[:, 0:nr].reshape(N_DEV, D).T, silu_rows[1:2].T, jnp.zeros((D, 128 - N_DEV - 1), f32)], axis=1)
    my_cols = (4 * lax.axis_index("x") + 2 * lax.axis_index("y") + lax.axis_index("c")) * (3 * D // N_DEV)
    dmx_all = lax.dynamic_slice(fac_all[:, nr:4 * nr].reshape(N_DEV, 3 * D), (0, my_cols), (N_DEV, 3 * D // N_DEV))
    dmc_all = lax.dynamic_slice(fac_all[:, 4 * nr:7 * nr].reshape(N_DEV, 3 * D), (0, my_cols), (N_DEV, 3 * D // N_DEV))
    g_wmod = _wmod_grad(sct, dmx_all, dmc_all)
    parts = [parts[0], g_wmod[None]] + parts[1:]

    given = dict(c_ctx=c_ctx, w_mod=w_mod, b_mod=b_mod, norm_w=norm_w, w_in=w_in, ssm_conv_w=ssm_conv_w, ssm_conv_b=ssm_conv_b,
                 dt_bias=dt_bias, a_log=a_log, d_skip=d_skip, ssm_norm_w=ssm_norm_w, w_out_ssm=w_out_ssm, conf_conv_w=conf_conv_w,
                 conf_conv_b=conf_conv_b, conf_ln_w=conf_ln_w, conf_ln_b=conf_ln_b, w_out_conf=w_out_conf, w_out=w_out,
                 final_norm_w=final_norm_w)
    ms = dict(c_ctx=m_c_ctx, w_mod=m_w_mod, b_mod=m_b_mod, norm_w=m_norm_w, w_in=m_w_in, ssm_conv_w=m_ssm_conv_w,
              ssm_conv_b=m_ssm_conv_b, dt_bias=m_dt_bias, a_log=m_a_log, d_skip=m_d_skip, ssm_norm_w=m_ssm_norm_w,
              w_out_ssm=m_w_out_ssm, conf_conv_w=m_conf_conv_w, conf_conv_b=m_conf_conv_b, conf_ln_w=m_conf_ln_w,
              conf_ln_b=m_conf_ln_b, w_out_conf=m_w_out_conf, w_out=m_w_out, final_norm_w=m_final_norm_w)
    vs = dict(c_ctx=v_c_ctx, w_mod=v_w_mod, b_mod=v_b_mod, norm_w=v_norm_w, w_in=v_w_in, ssm_conv_w=v_ssm_conv_w,
              ssm_conv_b=v_ssm_conv_b, dt_bias=v_dt_bias, a_log=v_a_log, d_skip=v_d_skip, ssm_norm_w=v_ssm_norm_w,
              w_out_ssm=v_w_out_ssm, conf_conv_w=v_conf_conv_w, conf_conv_b=v_conf_conv_b, conf_ln_w=v_conf_ln_w,
              conf_ln_b=v_conf_ln_b, w_out_conf=v_w_out_conf, w_out=v_w_out, final_norm_w=v_final_norm_w)
    grads, deltas, new_m, new_v = {}, {}, {}, {}
    sharded = ("w_in", "w_mod", "w_out_ssm", "w_out_conf", "w_out", "ssm_conv_w", "conf_conv_w")
    for i, nm in enumerate(sharded):
        shp = given[nm].shape
        w2 = given[nm].reshape(shp[1], shp[2])
        res = _adamw(parts[i], w2, ms[nm].reshape(w2.shape), vs[nm].reshape(w2.shape), "adamw_" + nm)
        grads[nm], deltas[nm], new_m[nm], new_v[nm] = [r.reshape(shp) for r in res]
    shapes = {nm: given[nm].shape for nm, _ in _SMALL}
    res = _adamw(small_parts, _pack_small(given), _pack_small(ms), _pack_small(vs), "adamw_small")
    for dst, packed in zip((grads, deltas, new_m, new_v), res):
        dst.update(_unpack_small(packed, shapes))

    loss = lax.psum(loss_acc[0, 0], ("x", "y", "c"))
    order = ("c_ctx", "w_mod", "b_mod", "norm_w", "w_in", "ssm_conv_w", "ssm_conv_b", "dt_bias", "a_log", "d_skip", "ssm_norm_w",
             "w_out_ssm", "conf_conv_w", "conf_conv_b", "conf_ln_w", "conf_ln_b", "w_out_conf", "w_out", "final_norm_w")
    return (loss, gx.reshape(1, L, D), *[grads[n] for n in order], *[deltas[n] for n in order],
            *[new_m[n] for n in order], *[new_v[n] for n in order])
```

```python
import jax
import jax.numpy as jnp
from jax import lax
from jax.experimental import pallas as pl
from jax.experimental.pallas import tpu as pltpu

f32 = jnp.float32
bf16 = jnp.bfloat16

D = 1024
DI = 2048
NG = 8
HPG = 4
HD = 64
GW = HPG * HD
NS = 128
NH = 32
Q = 128
GRID_W = 64
CK = 31
SK = 4
EPS = 1e-6
RT = 256
N_DEV = 8
IN_COLS = 11328
G10, G20, Z0, PX0, GV0, GG0, CG0, PBC0, DT0, NP = 0, 1024, 2048, 4096, 6144, 7168, 8192, 9216, 11264, 11520
B0, C0 = 2048, 3072
VMEM_LIMIT = 50 * 1024 * 1024
NEG = -1e30

ADAM_LR, ADAM_B1, ADAM_B2, ADAM_EPS, ADAM_WD, ADAM_STEP = 0.001, 0.9, 0.999, 1e-08, 0.01, 10

MESH = pl.DeviceIdType.MESH
S = jax.ShapeDtypeStruct


def _params(*sem):
    return pltpu.CompilerParams(dimension_semantics=tuple(sem) if sem else None, vmem_limit_bytes=VMEM_LIMIT)


def _sig(x):
    return 1.0 / (1.0 + jnp.exp(-x))


def _silu(x):
    return x * _sig(x)


def _dsilu(x, s):
    return s * (1.0 + x * (1.0 - s))


def _dot(a, b):
    return jnp.dot(a, b, preferred_element_type=f32)


def _dot_nt(a, b):
    return lax.dot_general(a, b, (((1,), (1,)), ((), ())), preferred_element_type=f32)


def _dot_tn(a, b):
    return lax.dot_general(a, b, (((0,), (0,)), ((), ())), preferred_element_type=f32)


def _dot3(t_bf, v):
    v1 = v.astype(bf16)
    r1 = v - v1.astype(f32)
    v2 = r1.astype(bf16)
    v3 = (r1 - v2.astype(f32)).astype(bf16)
    return _dot(t_bf, v1) + _dot(t_bf, v2) + _dot(t_bf, v3)


def _pick(n, prefs):
    for p in prefs:
        if n % p == 0:
            return p
    return n


def _full(shape):
    nd = len(shape)
    return pl.BlockSpec(shape, lambda *_: (0,) * nd)


def _matmul(a, b, out_dtype, name, tm=None, tn=None, tk=None, tb=False, comm=None):
    m, k = a.shape
    n = b.shape[0] if tb else b.shape[1]
    tm = tm if tm and m % tm == 0 else _pick(m, (768, 512, 256, 128))
    tn = tn if tn and n % tn == 0 else _pick(n, (1024, 512, 256, 128))
    tk = tk if tk and k % tk == 0 else _pick(k, (1024, 768, 512, 256, 128))
    nk = k // tk
    gi, gj = m // tm, n // tn
    carrs, modes = comm if comm else ((), ())
    nc = len(carrs)

    def kern(*refs):
        a_ref, b_ref = refs[:2]
        cins = refs[2:2 + nc]
        o_ref = refs[2 + nc]
        couts = refs[3 + nc:3 + 2 * nc]
        acc_ref = refs[3 + 2 * nc]
        sems = refs[4 + 2 * nc:]
        i, j, kk = pl.program_id(0), pl.program_id(1), pl.program_id(2)
        if nc:
            @pl.when(jnp.logical_and(jnp.logical_and(i == 0, j == 0), kk == 0))
            def _():
                _xchg_start(cins, couts, *sems, modes)

        part = _dot_nt(a_ref[...], b_ref[...]) if tb else _dot(a_ref[...], b_ref[...])
        if nk == 1:
            o_ref[...] = part.astype(o_ref.dtype)
        else:
            @pl.when(kk == 0)
            def _():
                acc_ref[...] = part

            @pl.when(kk > 0)
            def _():
                acc_ref[...] += part

            @pl.when(kk == nk - 1)
            def _():
                o_ref[...] = acc_ref[...].astype(o_ref.dtype)

        if nc:
            @pl.when(jnp.logical_and(jnp.logical_and(i == gi - 1, j == gj - 1), kk == nk - 1))
            def _():
                _xchg_wait(cins, couts, *sems, modes)

    anyspec = pl.BlockSpec(memory_space=pl.ANY)
    bspec = pl.BlockSpec((tn, tk), lambda i, j, kk: (j, kk)) if tb else pl.BlockSpec((tk, tn), lambda i, j, kk: (kk, j))
    out_shape = (S((m, n), out_dtype),) + _xchg_out_shapes(carrs, modes)
    res = pl.pallas_call(
        kern, out_shape=out_shape, grid=(gi, gj, nk),
        in_specs=[pl.BlockSpec((tm, tk), lambda i, j, kk: (i, kk)), bspec] + [anyspec] * nc,
        out_specs=(pl.BlockSpec((tm, tn), lambda i, j, kk: (i, j)),) + (anyspec,) * nc,
        scratch_shapes=[pltpu.VMEM((tm, tn), f32)] + (_xchg_sems(nc) if nc else []),
        compiler_params=_params(*((("arbitrary",) * 3) if nc else ("parallel", "parallel", "arbitrary"))), name=name)(a, b, *carrs)
    return res if nc else res[0]


def _mod_fwd(cc8, w_mod_bf, b_mod):
    def kern(c_ref, w_ref, b_ref, o_ref, s_ref):
        s = _silu(c_ref[...])
        s_ref[...] = s
        o_ref[...] = _dot(s.astype(bf16), w_ref[...]) + b_ref[...]

    return pl.pallas_call(kern, out_shape=(S((8, 3 * D), f32), S((8, D), f32)), compiler_params=_params(),
                          name="mod_fwd")(cc8, w_mod_bf, b_mod)


def _mod_bwd(ct, dmod8, w_mod_bf):
    tc = 512
    nj = 3 * D // tc

    def kern(ct_ref, dm_ref, w_ref, db_ref, dc_ref):
        j = pl.program_id(0)
        cx = ct_ref[:, 1:2]
        sx = _sig(cx)
        dmc = dm_ref[1:2, :]
        db_ref[...] = dm_ref[0:1, :] + dmc
        t = jnp.sum(w_ref[...].astype(f32) * dmc.astype(bf16).astype(f32), axis=1, keepdims=True) * _dsilu(cx, sx)

        @pl.when(j == 0)
        def _():
            dc_ref[...] = jnp.zeros_like(dc_ref)

        dc_ref[...] += jnp.broadcast_to(t, (D, 128))

    return pl.pallas_call(
        kern, out_shape=(S((1, 3 * D), f32), S((D, 128), f32)), grid=(nj,),
        in_specs=[_full((D, 128)), pl.BlockSpec((8, tc), lambda j: (0, j)), pl.BlockSpec((D, tc), lambda j: (0, j))],
        out_specs=(pl.BlockSpec((1, tc), lambda j: (0, j)), _full((D, 128))),
        compiler_params=_params("arbitrary"), name="mod_bwd")(ct, dmod8, w_mod_bf)


def _wmod_grad(sct, dmx, dmc):
    cols = dmx.shape[1]

    def kern(s_ref, dmx_ref, dmc_ref, g_ref):
        dmc_sum = dmc_ref[0:1, :]
        for d in range(1, N_DEV):
            dmc_sum = dmc_sum + dmc_ref[d:d + 1, :]
        g = s_ref[:, N_DEV:N_DEV + 1] * dmc_sum
        for d in range(N_DEV):
            g = g + s_ref[:, d:d + 1] * dmx_ref[d:d + 1, :]
        g_ref[...] = g

    return pl.pallas_call(kern, out_shape=S((D, cols), f32), compiler_params=_params(), name="wmod_grad")(sct, dmx, dmc)


def _prenorm(x, ctx, norm_w, mod):
    L, Lc = x.shape[0], ctx.shape[0]
    nlx, nt = L // RT, (L + Lc) // RT

    def kern(x_ref, c_ref, nw_ref, mod_ref, h_ref, ht_ref):
        i = pl.program_id(0)
        is_c = i >= nlx
        xv = jnp.where(is_c, c_ref[...], x_ref[...])
        shift = jnp.where(is_c, mod_ref[1:2, 0:D], mod_ref[0:1, 0:D])
        scale = jnp.where(is_c, mod_ref[1:2, D:2 * D], mod_ref[0:1, D:2 * D])
        r = lax.rsqrt(jnp.mean(xv * xv, axis=1, keepdims=True) + EPS)
        hv = (xv * r) * nw_ref[...] * (1.0 + scale) + shift
        h_ref[...] = hv.astype(bf16)
        ht_ref[...] = jnp.transpose(hv).astype(bf16)

    return pl.pallas_call(
        kern, out_shape=(S((L + Lc, D), bf16), S((D, L + Lc), bf16)), grid=(nt,),
        in_specs=[pl.BlockSpec((RT, D), lambda i: (jnp.minimum(i, nlx - 1), 0)),
                  pl.BlockSpec((RT, D), lambda i: (jnp.maximum(i - nlx, 0), 0)),
                  _full((1, D)), _full((8, 3 * D))],
        out_specs=(pl.BlockSpec((RT, D), lambda i: (i, 0)), pl.BlockSpec((D, RT), lambda i: (0, i))),
        compiler_params=_params("parallel"), name="prenorm")(x, ctx, norm_w, mod)


def _prenorm_bwd(x, ctx, dh, dx1, norm_w, mod):
    L, Lc = x.shape[0], ctx.shape[0]
    nlx, nt = L // RT, (L + Lc) // RT

    def kern(x_ref, c_ref, dh_ref, dx1_ref, nw_ref, mod_ref, gx_ref, dnw_ref, acc_ref):
        i = pl.program_id(0)
        is_c = i >= nlx

        @pl.when(i == 0)
        def _():
            dnw_ref[...] = jnp.zeros_like(dnw_ref)
            acc_ref[...] = jnp.zeros_like(acc_ref)

        xv = jnp.where(is_c, c_ref[...], x_ref[...])
        scale = jnp.where(is_c, mod_ref[1:2, D:2 * D], mod_ref[0:1, D:2 * D])
        nw = nw_ref[...]
        r = lax.rsqrt(jnp.mean(xv * xv, axis=1, keepdims=True) + EPS)
        xn = xv * r
        dh = dh_ref[...]
        dsh = jnp.sum(dh, axis=0, keepdims=True)
        dsc = jnp.sum(dh * (xn * nw), axis=0, keepdims=True)
        dxnw = dh * (1.0 + scale)
        dnw_ref[...] += jnp.sum(dxnw * xn, axis=0, keepdims=True)
        dxn = dxnw * nw
        dx = r * (dxn - xn * jnp.mean(dxn * xn, axis=1, keepdims=True))

        @pl.when(jnp.logical_not(is_c))
        def _():
            gx_ref[...] = dx1_ref[...] + dx
            acc_ref[0:1, :] += dsh
            acc_ref[1:2, :] += dsc

        @pl.when(is_c)
        def _():
            acc_ref[2:3, :] += dsh
            acc_ref[3:4, :] += dsc

    xmap = lambda i: (jnp.minimum(i, nlx - 1), 0)
    return pl.pallas_call(
        kern, out_shape=(S((L, D), f32), S((1, D), f32), S((8, D), f32)), grid=(nt,),
        in_specs=[pl.BlockSpec((RT, D), xmap), pl.BlockSpec((RT, D), lambda i: (jnp.maximum(i - nlx, 0), 0)),
                  pl.BlockSpec((RT, D), lambda i: (i, 0)), pl.BlockSpec((RT, D), xmap), _full((1, D)), _full((8, 3 * D))],
        out_specs=(pl.BlockSpec((RT, D), xmap), _full((1, D)), _full((8, D))),
        compiler_params=_params("arbitrary"), name="prenorm_bwd")(x, ctx, dh, dx1, norm_w, mod)


def _xbc_col(j):
    return jnp.where(j < 2, PX0 // 1024 + j, PBC0 // 1024 + j - 2)


def _halo_specs(nt_rows, ct, col=lambda j: j):
    cur = pl.BlockSpec((RT, ct), lambda i, j: (i, col(j)))
    prev = pl.BlockSpec((8, ct), lambda i, j: (jnp.maximum(i * (RT // 8) - 1, 0), col(j)))
    nxt = pl.BlockSpec((8, ct), lambda i, j: (jnp.minimum((i + 1) * (RT // 8), nt_rows // 8 - 1), col(j)))
    return cur, prev, nxt


def _fill_halo(scr, cur_ref, prev_ref, next_ref, i, nlx, nt):
    prev_ok = jnp.logical_and(i != 0, i != nlx)
    next_ok = jnp.logical_and(i != nlx - 1, i != nt - 1)
    scr[0:8, :] = jnp.where(prev_ok, prev_ref[...], 0.0)
    scr[8:8 + RT, :] = cur_ref[...]
    scr[8 + RT:16 + RT, :] = jnp.where(next_ok, next_ref[...], 0.0)


CONV_RB = 32


def _conv_blocks(ct):
    return [(slice(cb * 128, (cb + 1) * 128), r0) for cb in range(ct // 128) for r0 in range(0, RT, CONV_RB)]


def _taps(scr, cs, r0, shifts):
    blk = scr[r0:r0 + CONV_RB + 16, cs]
    n = CONV_RB + 16
    return [(blk if d == 0 else pltpu.roll(blk, (-d) % n, 0))[8:8 + CONV_RB, :] for d in shifts]


def _ssm_conv_fwd(proj, w8, b, nlx, half, out_dtype, name):
    T = proj.shape[0]
    nt = T // RT
    ct = 1024
    cur, prev, nxt = _halo_specs(T, ct, lambda j: _xbc_col(j + 2 * half))

    def kern(cur_ref, prev_ref, next_ref, w_ref, b_ref, o_ref, scr):
        i = pl.program_id(0)
        _fill_halo(scr, cur_ref, prev_ref, next_ref, i, nlx, nt)
        for cs, r0 in _conv_blocks(ct):
            taps = _taps(scr, cs, r0, [k - 2 for k in range(SK)])
            acc = jnp.broadcast_to(b_ref[:, cs], (CONV_RB, 128))
            for k in range(SK):
                acc = acc + w_ref[k:k + 1, cs] * taps[k]
            o_ref[r0:r0 + CONV_RB, cs] = _silu(acc).astype(out_dtype)

    return pl.pallas_call(
        kern, out_shape=S((T, 2 * ct), out_dtype), grid=(nt, 2),
        in_specs=[cur, prev, nxt, pl.BlockSpec((8, ct), lambda i, j: (0, j + 2 * half)),
                  pl.BlockSpec((1, ct), lambda i, j: (0, j + 2 * half))],
        out_specs=pl.BlockSpec((RT, ct), lambda i, j: (i, j)),
        scratch_shapes=[pltpu.VMEM((RT + 16, ct), f32)],
        compiler_params=_params("parallel", "parallel"), name=name)(proj, proj, proj, w8, b)


def _ssm_conv_dpre(dxbc, proj, w8, b, nlx):
    T = proj.shape[0]
    nt = T // RT
    ct = 1024
    cur = pl.BlockSpec((RT, ct), lambda j, i: (i, j))
    pcur = pl.BlockSpec((RT, ct), lambda j, i: (i, _xbc_col(j)))
    prev = pl.BlockSpec((8, ct), lambda j, i: (jnp.maximum(i * (RT // 8) - 1, 0), _xbc_col(j)))
    nxt = pl.BlockSpec((8, ct), lambda j, i: (jnp.minimum((i + 1) * (RT // 8), T // 8 - 1), _xbc_col(j)))

    def kern(d_ref, cur_ref, prev_ref, next_ref, w_ref, b_ref, dpre_ref, dw_ref, db_ref, scr):
        i = pl.program_id(1)
        _fill_halo(scr, cur_ref, prev_ref, next_ref, i, nlx, nt)

        @pl.when(i == 0)
        def _():
            dw_ref[...] = jnp.zeros_like(dw_ref)
            db_ref[...] = jnp.zeros_like(db_ref)

        for cb in range(ct // 128):
            cs = slice(cb * 128, (cb + 1) * 128)
            db_acc = jnp.zeros((CONV_RB, 128), f32)
            dw_acc = [jnp.zeros((CONV_RB, 128), f32) for _ in range(SK)]
            for r0 in range(0, RT, CONV_RB):
                taps = _taps(scr, cs, r0, [k - 2 for k in range(SK)])
                pre = jnp.broadcast_to(b_ref[:, cs], (CONV_RB, 128))
                for k in range(SK):
                    pre = pre + w_ref[k:k + 1, cs] * taps[k]
                dpre = d_ref[r0:r0 + CONV_RB, cs] * _dsilu(pre, _sig(pre))
                dpre_ref[r0:r0 + CONV_RB, cs] = dpre
                db_acc = db_acc + dpre
                dw_acc = [dw_acc[k] + dpre * taps[k] for k in range(SK)]
            db_ref[:, cs] += jnp.sum(db_acc, axis=0, keepdims=True)
            for k in range(SK):
                dw_ref[k:k + 1, cs] += jnp.sum(dw_acc[k], axis=0, keepdims=True)

    return pl.pallas_call(
        kern, out_shape=(S((T, 4096), f32), S((8, 4096), f32), S((1, 4096), f32)), grid=(4096 // ct, nt),
        in_specs=[cur, pcur, prev, nxt, pl.BlockSpec((8, ct), lambda j, i: (0, j)), pl.BlockSpec((1, ct), lambda j, i: (0, j))],
        out_specs=(cur, pl.BlockSpec((8, ct), lambda j, i: (0, j)), pl.BlockSpec((1, ct), lambda j, i: (0, j))),
        scratch_shapes=[pltpu.VMEM((RT + 16, ct), f32)],
        compiler_params=_params("parallel", "arbitrary"), name="ssm_conv_dpre")(dxbc, proj, proj, proj, w8, b)


def _ssm_conv_t(dpre, w8, dproj, nlx):
    T = dpre.shape[0]
    nt = T // RT
    ct = 1024
    cur, prev, nxt = _halo_specs(T, ct)

    def kern(cur_ref, prev_ref, next_ref, w_ref, _alias, o_ref, scr):
        i = pl.program_id(0)
        _fill_halo(scr, cur_ref, prev_ref, next_ref, i, nlx, nt)
        for cs, r0 in _conv_blocks(ct):
            taps = _taps(scr, cs, r0, [2 - k for k in range(SK)])
            acc = jnp.zeros((CONV_RB, 128), f32)
            for k in range(SK):
                acc = acc + w_ref[k:k + 1, cs] * taps[k]
            o_ref[r0:r0 + CONV_RB, cs] = acc.astype(bf16)

    return pl.pallas_call(
        kern, out_shape=S(dproj.shape, bf16), grid=(nt, 4096 // ct),
        in_specs=[cur, prev, nxt, pl.BlockSpec((8, ct), lambda i, j: (0, j)), pl.BlockSpec(memory_space=pl.ANY)],
        out_specs=pl.BlockSpec((RT, ct), lambda i, j: (i, _xbc_col(j))),
        scratch_shapes=[pltpu.VMEM((RT + 16, ct), f32)], input_output_aliases={4: 0},
        compiler_params=_params("parallel", "parallel"), name="ssm_conv_t")(dpre, dpre, dpre, w8, dproj)


def _tri():
    li = lax.broadcasted_iota(jnp.int32, (Q, Q), 0)
    si = lax.broadcasted_iota(jnp.int32, (Q, Q), 1)
    return (si <= li).astype(bf16), (si >= li).astype(bf16)


def _dt_prep(proj, bias_row, alog_row):
    T = proj.shape[0]
    nch = T // Q

    def kern(raw_ref, b_ref, al_ref, dt_ref, la_ref):
        lane = lax.broadcasted_iota(jnp.int32, (Q, 128), 1)
        v = raw_ref[...] + b_ref[...]
        dt = jnp.maximum(v, 0.0) + jnp.log1p(jnp.exp(-jnp.abs(v)))
        a = jnp.where(lane[0:1, :] < 2 * NH, -jnp.exp(al_ref[...]), 0.0)
        da = dt * a
        tri, trit = _tri()
        dt_ref[...] = dt
        la_ref[...] = jnp.where(lane < NH, _dot3(tri, da), _dot3(trit, da))

    return pl.pallas_call(
        kern, out_shape=(S((T, 128), f32), S((T, 128), f32)), grid=(nch,),
        in_specs=[pl.BlockSpec((Q, 128), lambda c: (c, DT0 // 128)), _full((1, 128)), _full((1, 128))],
        out_specs=(pl.BlockSpec((Q, 128), lambda c: (c, 0)), pl.BlockSpec((Q, 128), lambda c: (c, 0))),
        compiler_params=_params("parallel"), name="dt_prep")(proj, bias_row, alog_row)


def _dt_bwd(a1, a2, r2, sv, dt, la, proj, bias_row, alog_row, dproj):
    T = proj.shape[0]
    nch = T // Q
    blk = pl.BlockSpec((Q, 128), lambda c: (c, 0))

    def kern(a1_ref, a2_ref, r2_ref, s_ref, dt_ref, la_ref, raw_ref, b_ref, al_ref, _alias, o_ref, db_ref, dal_ref):
        c = pl.program_id(0)

        @pl.when(c == 0)
        def _():
            db_ref[...] = jnp.zeros_like(db_ref)
            dal_ref[...] = jnp.zeros_like(dal_ref)

        lane = lax.broadcasted_iota(jnp.int32, (Q, 128), 1)
        row = lax.broadcasted_iota(jnp.int32, (Q, 128), 0)
        fwd = lane < NH
        dt = dt_ref[...]
        la = la_ref[...]
        a2v = a2_ref[...]
        r2v = r2_ref[...]
        a = jnp.where(lane[0:1, :] < 2 * NH, -jnp.exp(al_ref[...]), 0.0)
        la_e = jnp.where(fwd[0:1, :], la[Q - 1:Q, :], la[0:1, :])
        is_end = row == jnp.where(fwd, Q - 1, 0)
        e_end = jnp.exp(la_e - la)
        wend = e_end * dt
        extra = s_ref[0:1, :] * jnp.exp(la_e) + jnp.sum(wend * a2v, axis=0, keepdims=True)
        dla = a1_ref[...] - dt * r2v - wend * a2v + jnp.where(is_end, extra, 0.0)
        tri, trit = _tri()
        rcs = jnp.where(fwd, _dot3(trit, dla), _dot3(tri, dla))
        ddt = r2v + e_end * a2v + a * rcs
        dal_ref[...] += a * jnp.sum(dt * rcs, axis=0, keepdims=True)
        draw = jnp.where(lane < 2 * NH, ddt * _sig(raw_ref[...] + b_ref[...]), 0.0)
        db_ref[...] += jnp.sum(draw, axis=0, keepdims=True)
        o_ref[...] = jnp.zeros_like(o_ref)
        o_ref[:, 0:128] = draw.astype(bf16)

    return pl.pallas_call(
        kern, out_shape=(S(dproj.shape, bf16), S((1, 128), f32), S((1, 128), f32)), grid=(nch,),
        in_specs=[blk, blk, blk, blk, blk, blk, pl.BlockSpec((Q, 128), lambda c: (c, DT0 // 128)),
                  _full((1, 128)), _full((1, 128)), pl.BlockSpec(memory_space=pl.ANY)],
        out_specs=(pl.BlockSpec((Q, NP - DT0), lambda c: (c, DT0 // (NP - DT0))), _full((1, 128)), _full((1, 128))),
        input_output_aliases={9: 0},
        compiler_params=_params("arbitrary"), name="dt_bwd")(a1, a2, r2, sv, dt, la, proj, bias_row, alog_row, dproj)


def _split2(v):
    hi = v.astype(bf16)
    lo = (v - hi.astype(f32)).astype(bf16)
    return jnp.concatenate([hi, lo], axis=1)


def _scan_consts(rev):
    hoff = NH if rev else 0
    g = jnp.arange(NG, dtype=jnp.int32)[:, None, None]

    def rc(nr, ncol):
        return jnp.arange(nr, dtype=jnp.int32)[None, :, None], jnp.arange(ncol, dtype=jnp.int32)[None, None, :]

    r, c = rc(2 * 128, HPG * HD)
    sel_w = (lax.rem(r, 128) == hoff + HPG * g + c // HD).astype(bf16)
    r, c = rc(HPG * HD, 128)
    ind_h = (c == hoff + HPG * g + r // HD).astype(bf16)
    r, c = rc(2 * HPG * Q, 128)
    ind_e = (c == hoff + HPG * g + lax.rem(r, HPG * Q) // Q).astype(bf16)
    return sel_w, ind_h, ind_e


def _masks(rev):
    li = lax.broadcasted_iota(jnp.int32, (Q, Q), 0)
    si = lax.broadcasted_iota(jnp.int32, (Q, Q), 1)
    mask = (li <= si) if rev else (li >= si)
    mask_t = (li >= si) if rev else (li <= si)
    lane = lax.broadcasted_iota(jnp.int32, (Q, HPG * HD), 1)
    hms = [jnp.logical_and(lane >= r * HD, lane < (r + 1) * HD) for r in range(HPG)]
    return mask, mask_t, hms


def _mine(hoff):
    lane = lax.broadcasted_iota(jnp.int32, (Q, 128), 1)
    return jnp.logical_and(lane >= hoff, lane < hoff + NH)


def _head_row(vals, hc0):
    lane = lax.broadcasted_iota(jnp.int32, (1, HPG * HD), 1)
    out = jnp.zeros((1, HPG * HD), f32)
    for r in range(HPG):
        out = jnp.where(jnp.logical_and(lane >= r * HD, lane < (r + 1) * HD), vals[:, hc0 + r:hc0 + r + 1], out)
    return out


def _chunk_of(j, rev, nxc, nch):
    return (nch - 1 - j) if rev else lax.rem(j + nxc, nch)


def _ssd_fwd(xs, bc, dt, la, consts, rev, nxc, name, y_acc=None):
    T = xs.shape[0]
    nch = T // Q
    hoff = NH if rev else 0
    e = 0 if rev else Q - 1
    cm = lambda j: _chunk_of(j, rev, nxc, nch)
    sel_w = consts[0]
    has_acc = y_acc is not None

    def kern(*refs):
        xs_ref, bc_ref, dt_ref, la_ref, sw_ref = refs[:5]
        yacc_ref = refs[5] if has_acc else None
        y_ref, hp_ref, h_ref = refs[5 + has_acc:]
        j = pl.program_id(0)

        @pl.when(j == 0)
        def _():
            h_ref[...] = jnp.zeros_like(h_ref)

        hp_ref[...] = h_ref[...]
        mask, _, hms = _masks(rev)
        la_all = la_ref[...]
        dt_all = dt_ref[...]
        la_t = jnp.transpose(la_all)
        dt_t = jnp.transpose(dt_all)
        la_e = la_all[e:e + 1, :]
        w2 = _split2(jnp.exp(jnp.where(_mine(hoff), la_e - la_all, 0.0)) * dt_all)
        e2 = _split2(jnp.exp(la_all))
        ela_e = jnp.exp(la_e)
        for g in range(NG):
            hc0 = hoff + g * HPG
            x = xs_ref[:, g * GW:(g + 1) * GW]
            bb = bc_ref[:, g * NS:(g + 1) * NS]
            cb = bc_ref[:, NG * NS + g * NS:NG * NS + (g + 1) * NS]
            ht = h_ref[g * NS:(g + 1) * NS, :]
            scores = _dot_nt(cb, bb)
            yoff = _dot(cb, ht.astype(bf16))
            wend = _dot(w2, sw_ref[g])
            expla = _dot(e2, sw_ref[g])
            mixes, xstack = [], []
            for r in range(HPG):
                hc = hc0 + r
                la_rep = jnp.broadcast_to(la_all[:, hc:hc + 1], (Q, 128))
                decay = jnp.exp(jnp.where(mask, la_rep - la_t[hc:hc + 1, :], NEG))
                mixes.append((scores * decay * dt_t[hc:hc + 1, :]).astype(bf16))
                xstack.append(jnp.where(hms[r], x, 0.0).astype(bf16))
            y = _dot(jnp.concatenate(mixes, axis=1), jnp.concatenate(xstack, axis=0)) + yoff * expla
            if has_acc:
                y = y + yacc_ref[:, g * GW:(g + 1) * GW]
            y_ref[:, g * GW:(g + 1) * GW] = y
            h_ref[g * NS:(g + 1) * NS, :] = ht * _head_row(ela_e, hc0) + _dot_tn(bb, (x * wend).astype(bf16))

    row = lambda j: (cm(j), 0)
    yblk = pl.BlockSpec((Q, DI), row)
    return pl.pallas_call(
        kern, out_shape=(S((T, DI), f32), S((nch, NG * NS, HPG * HD), f32)), grid=(nch,),
        in_specs=[yblk, pl.BlockSpec((Q, 2 * NG * NS), row), pl.BlockSpec((Q, 128), row), pl.BlockSpec((Q, 128), row),
                  _full(sel_w.shape)] + ([yblk] if has_acc else []),
        out_specs=(yblk, pl.BlockSpec((None, NG * NS, HPG * HD), lambda j: (cm(j), 0, 0))),
        scratch_shapes=[pltpu.VMEM((NG * NS, HPG * HD), f32)],
        input_output_aliases={5: 0} if has_acc else {},
        compiler_params=_params("arbitrary"), name=name)(xs, bc, dt, la, sel_w, *([y_acc] if has_acc else []))


def _ssd_bwd(xs, bc, dy, dt, la, hprev, dskip_full, consts, rev, nxc, name, acc=None):
    T = xs.shape[0]
    nch = T // Q
    hoff = NH if rev else 0
    e = 0 if rev else Q - 1
    cm = lambda j: _chunk_of(nch - 1 - j, rev, nxc, nch)
    has_acc = acc is not None
    sel_w, ind_h, ind_e = consts

    def kern(*refs):
        xs_ref, bc_ref, dy_ref, dt_ref, la_ref, hp_ref, dsk_ref, sw_ref, ih_ref, ie_ref = refs[:10]
        k = 10
        if has_acc:
            dxbc_in, a1_in, a2_in, r2_in, s_in = refs[k:k + 5]
            k += 5
        dxbc_ref, a1_ref, a2_ref, r2_ref, s_ref, g_ref, r2scr = refs[k:k + 7]
        j = pl.program_id(0)

        @pl.when(j == 0)
        def _():
            g_ref[...] = jnp.zeros_like(g_ref)

        mask, mask_t, hms = _masks(rev)
        la_all = la_ref[...]
        dt_all = dt_ref[...]
        la_t = jnp.transpose(la_all)
        dt_t = jnp.transpose(dt_all)
        la_e = la_all[e:e + 1, :]
        w2 = _split2(jnp.exp(jnp.where(_mine(hoff), la_e - la_all, 0.0)) * dt_all)
        e2 = _split2(jnp.exp(la_all))
        wed2 = jnp.concatenate([w2, e2, _split2(dt_all)], axis=0)
        ela_e = jnp.exp(la_e)
        r2scr[...] = jnp.zeros_like(r2scr)
        a1acc = jnp.zeros((Q, 128), f32)
        a2acc = jnp.zeros((Q, 128), f32)
        sacc = jnp.zeros((1, 128), f32)
        for g in range(NG):
            hc0 = hoff + g * HPG
            x = xs_ref[:, g * GW:(g + 1) * GW]
            bb = bc_ref[:, g * NS:(g + 1) * NS]
            cb = bc_ref[:, NG * NS + g * NS:NG * NS + (g + 1) * NS]
            dyv = dy_ref[:, g * GW:(g + 1) * GW]
            gt = g_ref[g * NS:(g + 1) * NS, :]
            ht = hp_ref[g * NS:(g + 1) * NS, :]
            gtb = gt.astype(bf16)
            htb = ht.astype(bf16)
            xb = x.astype(bf16)
            scores = _dot_nt(cb, bb)
            scores_t = _dot_nt(bb, cb)
            bg = _dot(bb, gtb)
            yoff = _dot(cb, htb)
            sel3 = _dot(wed2, sw_ref[g])
            wend, expla, dtf = sel3[0:Q], sel3[Q:2 * Q], sel3[2 * Q:3 * Q]
            dym = jnp.concatenate([jnp.where(hms[r], dyv, 0.0).astype(bf16) for r in range(HPG)], axis=0)
            dyx_all = _dot_nt(dym, xb)
            sdts, ems = [], []
            wsum = jnp.zeros((Q, Q), f32)
            for r in range(HPG):
                hc = hc0 + r
                la_rep = jnp.broadcast_to(la_all[:, hc:hc + 1], (Q, 128))
                la_r = la_t[hc:hc + 1, :]
                dt_r = dt_t[hc:hc + 1, :]
                decay = jnp.exp(jnp.where(mask, la_rep - la_r, NEG))
                decay_t = jnp.exp(jnp.where(mask_t, la_r - la_rep, NEG))
                dyx = dyx_all[r * Q:(r + 1) * Q, :]
                fm = dyx * (scores * decay)
                r2scr[hc:hc + 1, :] = jnp.sum(fm, axis=0, keepdims=True)
                ems.append(fm * dt_r)
                wsum = wsum + dyx * decay * dt_r
                sdts.append((scores_t * decay_t).astype(bf16))
            dx = dtf * _dot(jnp.concatenate(sdts, axis=1), dym) + wend * bg
            if not has_acc:
                dx = dx + dsk_ref[:, g * GW:(g + 1) * GW] * dyv
            red3 = _dot(jnp.concatenate([(dyv * yoff * expla).astype(bf16), (x * bg).astype(bf16), (gt * ht).astype(bf16)],
                                        axis=0), ih_ref[g])
            a1acc = a1acc + _dot(_split2(jnp.concatenate(ems, axis=1)), ie_ref[g]) + red3[0:Q]
            a2acc = a2acc + red3[Q:2 * Q]
            sacc = sacc + jnp.sum(red3[2 * Q:3 * Q], axis=0, keepdims=True)
            wb = wsum.astype(bf16)
            dysb = (dyv * expla).astype(bf16)
            dc = _dot(wb, bb) + _dot_nt(dysb, htb)
            db = _dot_tn(wb, cb) + _dot_nt((x * wend).astype(bf16), gtb)
            g_ref[g * NS:(g + 1) * NS, :] = gt * _head_row(ela_e, hc0) + _dot_tn(cb, dysb)
            if has_acc:
                dx = dx + dxbc_in[:, g * GW:(g + 1) * GW]
                db = db + dxbc_in[:, B0 + g * NS:B0 + (g + 1) * NS]
                dc = dc + dxbc_in[:, C0 + g * NS:C0 + (g + 1) * NS]
            dxbc_ref[:, g * GW:(g + 1) * GW] = dx
            dxbc_ref[:, B0 + g * NS:B0 + (g + 1) * NS] = db
            dxbc_ref[:, C0 + g * NS:C0 + (g + 1) * NS] = dc
        r2c = jnp.transpose(r2scr[...])
        sc = jnp.broadcast_to(sacc, (Q, 128))
        if has_acc:
            a1acc = a1acc + a1_in[...]
            a2acc = a2acc + a2_in[...]
            r2c = r2c + r2_in[...]
            sc = sc + s_in[...]
        a1_ref[...] = a1acc
        a2_ref[...] = a2acc
        r2_ref[...] = r2c
        s_ref[...] = sc

    blk = pl.BlockSpec((Q, 128), lambda j: (cm(j), 0))
    big = pl.BlockSpec((Q, 4096), lambda j: (cm(j), 0))
    wide = pl.BlockSpec((Q, DI), lambda j: (cm(j), 0))
    in_specs = [wide, pl.BlockSpec((Q, 2 * NG * NS), lambda j: (cm(j), 0)), wide, blk, blk,
                pl.BlockSpec((None, NG * NS, HPG * HD), lambda j: (cm(j), 0, 0)), _full((1, DI)),
                _full(sel_w.shape), _full(ind_h.shape), _full(ind_e.shape)]
    args = [xs, bc, dy, dt, la, hprev, dskip_full, sel_w, ind_h, ind_e]
    aliases = {}
    if has_acc:
        in_specs += [big, blk, blk, blk, blk]
        args += list(acc)
        aliases = {10: 0, 11: 1, 12: 2, 13: 3, 14: 4}
    return pl.pallas_call(
        kern, out_shape=(S((T, 4096), f32), S((T, 128), f32), S((T, 128), f32), S((T, 128), f32), S((T, 128), f32)),
        grid=(nch,), in_specs=in_specs, out_specs=(big, blk, blk, blk, blk),
        scratch_shapes=[pltpu.VMEM((NG * NS, HPG * HD), f32), pltpu.VMEM((128, Q), f32)],
        input_output_aliases=aliases,
        compiler_params=_params("arbitrary"), name=name)(*args)


def _ynorm_fwd(ysum, xs, proj, dskip_full, nw, L):
    nlx = L // RT

    def kern(ys_ref, xs_ref, z_ref, dsk_ref, nw_ref, y_ref, yn_ref, ynt_ref):
        y = ys_ref[...] + dsk_ref[...] * xs_ref[...]
        y_ref[...] = y
        yz = y * _silu(z_ref[...])
        for g in range(NG):
            sl = yz[:, g * GW:(g + 1) * GW]
            r = lax.rsqrt(jnp.mean(sl * sl, axis=1, keepdims=True) + EPS)
            yn = (sl * r) * nw_ref[:, g * GW:(g + 1) * GW]
            yn_ref[:, g * GW:(g + 1) * GW] = yn.astype(bf16)
            ynt_ref[g * GW:(g + 1) * GW, :] = jnp.transpose(yn).astype(bf16)

    blk = pl.BlockSpec((RT, DI), lambda i: (i, 0))
    return pl.pallas_call(
        kern, out_shape=(S((L, DI), f32), S((L, DI), bf16), S((DI, L), bf16)), grid=(nlx,),
        in_specs=[blk, blk, pl.BlockSpec((RT, DI), lambda i: (i, Z0 // DI)), _full((1, DI)), _full((1, DI))],
        out_specs=(blk, blk, pl.BlockSpec((DI, RT), lambda i: (0, i))),
        compiler_params=_params("parallel"), name="ynorm_fwd")(ysum, xs, proj, dskip_full, nw)


def _ynorm_bwd(dyn, y, xs, proj, dskip_full, nw, dproj):
    L = y.shape[0]
    T = proj.shape[0]
    nlx, nt = L // RT, T // RT

    def kern(dyn_ref, y_ref, xs_ref, z_ref, dsk_ref, nw_ref, _alias, dz_ref, dy_ref, dnw_ref, dsk_acc):
        i = pl.program_id(0)

        @pl.when(i == 0)
        def _():
            dnw_ref[...] = jnp.zeros_like(dnw_ref)
            dsk_acc[...] = jnp.zeros_like(dsk_acc)

        @pl.when(i >= nlx)
        def _():
            dz_ref[...] = jnp.zeros_like(dz_ref)
            dy_ref[...] = jnp.zeros_like(dy_ref)

        @pl.when(i < nlx)
        def _():
            y = y_ref[...]
            z = z_ref[...]
            sz = _sig(z)
            gz = z * sz
            yz = y * gz
            dynv = dyn_ref[...]
            for g in range(NG):
                cs = slice(g * 256, (g + 1) * 256)
                sl = yz[:, cs]
                r = lax.rsqrt(jnp.mean(sl * sl, axis=1, keepdims=True) + EPS)
                yhat = sl * r
                dn = dynv[:, cs]
                dnw_ref[:, cs] += jnp.sum(dn * yhat, axis=0, keepdims=True)
                dyh = dn * nw_ref[:, cs]
                dyz = r * (dyh - yhat * jnp.mean(dyh * yhat, axis=1, keepdims=True))
                dyv = dyz * gz[:, cs]
                dy_ref[:, cs] = dyv
                dz_ref[:, cs] = (dyz * y[:, cs] * _dsilu(z[:, cs], sz[:, cs])).astype(bf16)
                dsk_acc[:, cs] += jnp.sum(dyv * xs_ref[:, cs], axis=0, keepdims=True)

    xmap = lambda i: (jnp.minimum(i, nlx - 1), 0)
    return pl.pallas_call(
        kern, out_shape=(S(dproj.shape, bf16), S((T, DI), f32), S((1, DI), f32), S((1, DI), f32)), grid=(nt,),
        in_specs=[pl.BlockSpec((RT, DI), xmap), pl.BlockSpec((RT, DI), xmap), pl.BlockSpec((RT, DI), xmap),
                  pl.BlockSpec((RT, DI), lambda i: (jnp.minimum(i, nlx - 1), Z0 // DI)), _full((1, DI)), _full((1, DI)),
                  pl.BlockSpec(memory_space=pl.ANY)],
        out_specs=(pl.BlockSpec((RT, DI), lambda i: (i, Z0 // DI)), pl.BlockSpec((RT, DI), lambda i: (i, 0)),
                   _full((1, DI)), _full((1, DI))),
        input_output_aliases={6: 0},
        compiler_params=_params("arbitrary"), name="ynorm_bwd")(dyn, y, xs, proj, dskip_full, nw, dproj)


def _head_sums(cols):
    def kern(c_ref, o_ref):
        o_ref[...] = jnp.broadcast_to(jnp.sum(c_ref[...], axis=1, keepdims=True), (NH, 128))

    return pl.pallas_call(kern, out_shape=S((NH, 128), f32), name="head_sums")(cols)


SEG_STRIDE = 96
SEG_PAD = 16
NSEG = RT // GRID_W
CONF_ROWS = SEG_PAD + NSEG * SEG_STRIDE


SHIFT_ROWS = CONF_ROWS - 8
CONF_CW = 256


CONF_RB = 32


def _seg_zero_pads(scr):
    scr[0:SEG_PAD, :] = jnp.zeros((SEG_PAD, scr.shape[1]), f32)
    for s in range(NSEG):
        lo = SEG_PAD + s * SEG_STRIDE + GRID_W
        scr[lo:lo + SEG_STRIDE - GRID_W, :] = jnp.zeros((SEG_STRIDE - GRID_W, scr.shape[1]), f32)


def _seg_row(r0):
    return SEG_PAD + (r0 // GRID_W) * SEG_STRIDE + r0 % GRID_W


def _shift_copies(cps, scr, cs):
    full = scr[:, cs]
    for s in range(1, 8):
        cps[s - 1, :, :] = pltpu.roll(full, CONF_ROWS - s, 0)[0:SHIFT_ROWS, :]


def _tap(cps, scr, cs, o):
    rs = o % 8
    return scr[pl.ds(o, GRID_W), cs] if rs == 0 else cps[rs - 1, pl.ds(o - rs, GRID_W), :]


def _conf_fwd(proj, w32, cb, lnw, lnb, L):
    nlx = L // RT

    def kern(v_ref, g_ref, cg_ref, w_ref, cb_ref, lnw_ref, lnb_ref, u1_ref, u3_ref, u3t_ref, scr, cps, u3_scr):
        _seg_zero_pads(scr)
        for r0 in range(0, RT, CONF_RB):
            rows = slice(r0, r0 + CONF_RB)
            scr[_seg_row(r0):_seg_row(r0) + CONF_RB, :] = v_ref[rows, :] * _sig(g_ref[rows, :])
        for cc in range(D // CONF_CW):
            cs = slice(cc * CONF_CW, (cc + 1) * CONF_CW)
            _shift_copies(cps, scr, cs)
            for s in range(NSEG):
                acc = jnp.broadcast_to(cb_ref[:, cs], (GRID_W, CONF_CW))
                for k in range(CK):
                    acc = acc + w_ref[k:k + 1, cs] * _tap(cps, scr, cs, SEG_PAD + s * SEG_STRIDE + k - CK // 2)
                u1_ref[s * GRID_W:(s + 1) * GRID_W, cs] = acc
        for r0 in range(0, RT, CONF_RB):
            rows = slice(r0, r0 + CONF_RB)
            u1 = u1_ref[rows, :]
            xc = u1 - jnp.mean(u1, axis=1, keepdims=True)
            r = lax.rsqrt(jnp.mean(xc * xc, axis=1, keepdims=True) + EPS)
            u2 = (xc * r) * lnw_ref[...] + lnb_ref[...]
            u3 = _silu(u2) * _silu(cg_ref[rows, :])
            u3_ref[rows, :] = u3.astype(bf16)
            u3_scr[rows, :] = u3
        u3t_ref[...] = jnp.transpose(u3_scr[...]).astype(bf16)

    blk = pl.BlockSpec((RT, D), lambda i: (i, 0))
    return pl.pallas_call(
        kern, out_shape=(S((L, D), f32), S((L, D), bf16), S((D, L), bf16)), grid=(nlx,),
        in_specs=[pl.BlockSpec((RT, D), lambda i: (i, GV0 // D)), pl.BlockSpec((RT, D), lambda i: (i, GG0 // D)),
                  pl.BlockSpec((RT, D), lambda i: (i, CG0 // D)), _full((32, D)), _full((1, D)), _full((1, D)), _full((1, D))],
        out_specs=(blk, blk, pl.BlockSpec((D, RT), lambda i: (0, i))),
        scratch_shapes=[pltpu.VMEM((CONF_ROWS, D), f32), pltpu.VMEM((7, SHIFT_ROWS, CONF_CW), f32), pltpu.VMEM((RT, D), f32)],
        compiler_params=_params("parallel"), name="conf_fwd")(proj, proj, proj, w32, cb, lnw, lnb)


def _conf_bwd(du3, u1, proj, w32, lnw, lnb, dproj):
    L = u1.shape[0]
    T = proj.shape[0]
    nlx, nt = L // RT, T // RT

    def kern(du3_ref, u1_ref, v_ref, g_ref, cg_ref, w_ref, lnw_ref, lnb_ref, _alias,
             o_ref, dw_ref, dcb_ref, dlw_ref, dlb_ref, scr_u, scr_d, du0_scr, cps_u, cps_d):
        i = pl.program_id(0)

        @pl.when(i == 0)
        def _():
            dw_ref[...] = jnp.zeros_like(dw_ref)
            dcb_ref[...] = jnp.zeros_like(dcb_ref)
            dlw_ref[...] = jnp.zeros_like(dlw_ref)
            dlb_ref[...] = jnp.zeros_like(dlb_ref)

        @pl.when(i >= nlx)
        def _():
            o_ref[...] = jnp.zeros_like(o_ref)

        @pl.when(i < nlx)
        def _():
            _seg_zero_pads(scr_u)
            _seg_zero_pads(scr_d)
            for r0 in range(0, RT, CONF_RB):
                rows = slice(r0, r0 + CONF_RB)
                cg = cg_ref[rows, :]
                scg = _sig(cg)
                u1 = u1_ref[rows, :]
                xc = u1 - jnp.mean(u1, axis=1, keepdims=True)
                r = lax.rsqrt(jnp.mean(xc * xc, axis=1, keepdims=True) + EPS)
                xhat = xc * r
                u2 = xhat * lnw_ref[...] + lnb_ref[...]
                s2 = _sig(u2)
                du3v = du3_ref[rows, :]
                du2 = du3v * (cg * scg) * _dsilu(u2, s2)
                o_ref[rows, 2 * D:3 * D] = (du3v * (u2 * s2) * _dsilu(cg, scg)).astype(bf16)
                dlw_ref[...] += jnp.sum(du2 * xhat, axis=0, keepdims=True)
                dlb_ref[...] += jnp.sum(du2, axis=0, keepdims=True)
                dxh = du2 * lnw_ref[...]
                du1 = r * (dxh - jnp.mean(dxh, axis=1, keepdims=True) - xhat * jnp.mean(dxh * xhat, axis=1, keepdims=True))
                dcb_ref[...] += jnp.sum(du1, axis=0, keepdims=True)
                scr_u[_seg_row(r0):_seg_row(r0) + CONF_RB, :] = v_ref[rows, :] * _sig(g_ref[rows, :])
                scr_d[_seg_row(r0):_seg_row(r0) + CONF_RB, :] = du1
            for cc in range(D // CONF_CW):
                cs = slice(cc * CONF_CW, (cc + 1) * CONF_CW)
                _shift_copies(cps_u, scr_u, cs)
                _shift_copies(cps_d, scr_d, cs)
                for k in range(CK):
                    t = jnp.zeros((GRID_W, CONF_CW), f32)
                    for s in range(NSEG):
                        base = SEG_PAD + s * SEG_STRIDE
                        t = t + scr_d[pl.ds(base, GRID_W), cs] * _tap(cps_u, scr_u, cs, base + k - CK // 2)
                    dw_ref[k:k + 1, cs] += jnp.sum(t, axis=0, keepdims=True)
                for s in range(NSEG):
                    base = SEG_PAD + s * SEG_STRIDE
                    acc = jnp.zeros((GRID_W, CONF_CW), f32)
                    for k in range(CK):
                        acc = acc + w_ref[k:k + 1, cs] * _tap(cps_d, scr_d, cs, base + CK // 2 - k)
                    du0_scr[s * GRID_W:(s + 1) * GRID_W, cs] = acc
            for r0 in range(0, RT, CONF_RB):
                rows = slice(r0, r0 + CONF_RB)
                du0 = du0_scr[rows, :]
                sg = _sig(g_ref[rows, :])
                o_ref[rows, 0:D] = (du0 * sg).astype(bf16)
                o_ref[rows, D:2 * D] = (du0 * v_ref[rows, :] * sg * (1.0 - sg)).astype(bf16)

    xmap = lambda i: (jnp.minimum(i, nlx - 1), 0)
    pmap = lambda cb: (lambda i: (jnp.minimum(i, nlx - 1), cb))
    return pl.pallas_call(
        kern, out_shape=(S(dproj.shape, bf16), S((32, D), f32), S((1, D), f32), S((1, D), f32), S((1, D), f32)), grid=(nt,),
        in_specs=[pl.BlockSpec((RT, D), xmap), pl.BlockSpec((RT, D), xmap),
                  pl.BlockSpec((RT, D), pmap(GV0 // D)), pl.BlockSpec((RT, D), pmap(GG0 // D)), pl.BlockSpec((RT, D), pmap(CG0 // D)),
                  _full((32, D)), _full((1, D)), _full((1, D)), pl.BlockSpec(memory_space=pl.ANY)],
        out_specs=(pl.BlockSpec((RT, 3 * D), lambda i: (i, GV0 // (3 * D))), _full((32, D)), _full((1, D)), _full((1, D)), _full((1, D))),
        scratch_shapes=[pltpu.VMEM((CONF_ROWS, D), f32), pltpu.VMEM((CONF_ROWS, D), f32), pltpu.VMEM((RT, D), f32),
                        pltpu.VMEM((7, SHIFT_ROWS, CONF_CW), f32), pltpu.VMEM((7, SHIFT_ROWS, CONF_CW), f32)],
        input_output_aliases={8: 0},
        compiler_params=_params("arbitrary"), name="conf_bwd")(du3, u1, proj, proj, proj, w32, lnw, lnb, dproj)


def _merge_fwd(bs, bc, proj):
    L = bs.shape[0]

    def kern(bs_ref, bc_ref, g1_ref, g2_ref, o_ref, ot_ref):
        mv = _sig(g1_ref[...]) * bs_ref[...] + _sig(g2_ref[...]) * bc_ref[...]
        o_ref[...] = mv.astype(bf16)
        ot_ref[...] = jnp.transpose(mv).astype(bf16)

    blk = pl.BlockSpec((RT, D), lambda i: (i, 0))
    return pl.pallas_call(
        kern, out_shape=(S((L, D), bf16), S((D, L), bf16)), grid=(L // RT,),
        in_specs=[blk, blk, pl.BlockSpec((RT, D), lambda i: (i, G10 // D)), pl.BlockSpec((RT, D), lambda i: (i, G20 // D))],
        out_specs=(blk, pl.BlockSpec((D, RT), lambda i: (0, i))),
        compiler_params=_params("parallel"), name="merge_fwd")(bs, bc, proj, proj)


def _merge_bwd(dmerged, bs, bc, proj):
    L = bs.shape[0]
    T = proj.shape[0]
    nlx, nt = L // RT, T // RT

    def kern(dm_ref, bs_ref, bc_ref, g1_ref, g2_ref, o_ref, dbs_ref, dbc_ref):
        i = pl.program_id(0)

        @pl.when(i >= nlx)
        def _():
            o_ref[...] = jnp.zeros_like(o_ref)

        @pl.when(i < nlx)
        def _():
            dm = dm_ref[...]
            s1 = _sig(g1_ref[...])
            s2 = _sig(g2_ref[...])
            dbs_ref[...] = (dm * s1).astype(bf16)
            dbc_ref[...] = (dm * s2).astype(bf16)
            o_ref[:, 0:D] = (dm * bs_ref[...] * s1 * (1.0 - s1)).astype(bf16)
            o_ref[:, D:2 * D] = (dm * bc_ref[...] * s2 * (1.0 - s2)).astype(bf16)

    xmap = lambda i: (jnp.minimum(i, nlx - 1), 0)
    pmap = lambda cb: (lambda i: (jnp.minimum(i, nlx - 1), cb))
    xblk = pl.BlockSpec((RT, D), xmap)
    return pl.pallas_call(
        kern, out_shape=(S((T, NP), bf16), S((L, D), bf16), S((L, D), bf16)), grid=(nt,),
        in_specs=[xblk, xblk, xblk, pl.BlockSpec((RT, D), pmap(G10 // D)), pl.BlockSpec((RT, D), pmap(G20 // D))],
        out_specs=(pl.BlockSpec((RT, 2 * D), lambda i: (i, G10 // (2 * D))), xblk, xblk),
        compiler_params=_params("arbitrary"), name="merge_bwd")(dmerged, bs, bc, proj, proj)


def _final(x, out, target, mod, fw):
    L = x.shape[0]

    def kern(x_ref, o_ref, t_ref, mod_ref, fw_ref, dx1_ref, dout_ref, loss_ref, dfw_ref, dg_ref):
        i = pl.program_id(0)

        @pl.when(i == 0)
        def _():
            loss_ref[...] = jnp.zeros_like(loss_ref)
            dfw_ref[...] = jnp.zeros_like(dfw_ref)
            dg_ref[...] = jnp.zeros_like(dg_ref)

        gate = mod_ref[0:1, 2 * D:3 * D]
        ov = o_ref[...]
        x1 = x_ref[...] + gate * ov
        r = lax.rsqrt(jnp.mean(x1 * x1, axis=1, keepdims=True) + EPS)
        xn = x1 * r
        fw = fw_ref[...]
        err = xn * fw - t_ref[...]
        part = 0.5 * jnp.sum(jnp.mean(err * err, axis=1, keepdims=True), axis=0, keepdims=True)
        loss_ref[...] += jnp.broadcast_to(part, (8, 128))
        dy = err * (1.0 / D)
        dfw_ref[...] += jnp.sum(dy * xn, axis=0, keepdims=True)
        dyw = dy * fw
        dx1 = r * (dyw - xn * jnp.mean(dyw * xn, axis=1, keepdims=True))
        dx1_ref[...] = dx1
        dout_ref[...] = (gate * dx1).astype(bf16)
        dg_ref[...] += jnp.sum(dx1 * ov, axis=0, keepdims=True)

    blk = pl.BlockSpec((RT, D), lambda i: (i, 0))
    return pl.pallas_call(
        kern, out_shape=(S((L, D), f32), S((L, D), bf16), S((8, 128), f32), S((1, D), f32), S((1, D), f32)), grid=(L // RT,),
        in_specs=[blk, blk, blk, _full((8, 3 * D)), _full((1, D))],
        out_specs=(blk, blk, _full((8, 128)), _full((1, D)), _full((1, D))),
        compiler_params=_params("arbitrary"), name="final")(x, out, target, mod, fw)


def _me():
    return 4 * lax.axis_index("x") + 2 * lax.axis_index("y") + lax.axis_index("c")


def _xchg_copy(ins, outs, send_sems, recv_sems, modes, a, k, me):
    peer = lax.rem(me + k, N_DEV)
    pid = (peer // 4, lax.rem(peer // 2, 2), lax.rem(peer, 2))
    src = ins[a].at[peer] if modes[a] else ins[a]
    return pltpu.make_async_remote_copy(src_ref=src, dst_ref=outs[a].at[me], send_sem=send_sems.at[a, k - 1],
                                        recv_sem=recv_sems.at[a, k - 1], device_id=pid, device_id_type=MESH)


def _xchg_local(ins, outs, loc_sems, modes, a, me):
    return pltpu.make_async_copy(ins[a].at[me] if modes[a] else ins[a], outs[a].at[me], loc_sems.at[a])


def _xchg_start(ins, outs, send_sems, recv_sems, loc_sems, modes):
    me = _me()
    for a in range(len(modes)):
        _xchg_local(ins, outs, loc_sems, modes, a, me).start()
        for k in range(1, N_DEV):
            _xchg_copy(ins, outs, send_sems, recv_sems, modes, a, k, me).start()


def _xchg_wait(ins, outs, send_sems, recv_sems, loc_sems, modes):
    me = _me()
    for a in range(len(modes)):
        for k in range(1, N_DEV):
            frm = lax.rem(me + N_DEV - k, N_DEV)
            src = ins[a].at[frm] if modes[a] else ins[a]
            pltpu.make_async_remote_copy(src_ref=src, dst_ref=outs[a].at[frm], send_sem=send_sems.at[a, k - 1],
                                         recv_sem=recv_sems.at[a, k - 1], device_id=(0, 0, 0), device_id_type=MESH).wait_recv()
    for a in range(len(modes)):
        for k in range(1, N_DEV):
            _xchg_copy(ins, outs, send_sems, recv_sems, modes, a, k, me).wait_send()
        _xchg_local(ins, outs, loc_sems, modes, a, me).wait()


def _xchg_out_shapes(arrs, modes):
    return tuple(S((N_DEV,) + (a.shape[1:] if sc else a.shape), a.dtype) for a, sc in zip(arrs, modes))


def _xchg_sems(n):
    return [pltpu.SemaphoreType.DMA((n, N_DEV - 1)), pltpu.SemaphoreType.DMA((n, N_DEV - 1)), pltpu.SemaphoreType.DMA((n,))]


def _exchange(arrs, modes, name):
    n = len(arrs)

    def kern(*refs):
        ins, outs, sems = refs[:n], refs[n:2 * n], refs[2 * n:]
        _xchg_start(ins, outs, *sems, modes)
        _xchg_wait(ins, outs, *sems, modes)

    anyspec = pl.BlockSpec(memory_space=pl.ANY)
    return pl.pallas_call(
        kern, out_shape=_xchg_out_shapes(arrs, modes), in_specs=[anyspec] * n, out_specs=tuple([anyspec] * n),
        scratch_shapes=_xchg_sems(n), name=name)(*arrs)


def _gather2(arrs, name):
    n = len(arrs)

    def kern(*refs):
        ins, outs = refs[:n], refs[n:2 * n]
        send_sems, recv_sems, loc_sems = refs[2 * n:]
        x, y, c = lax.axis_index("x"), lax.axis_index("y"), lax.axis_index("c")
        me, sib = (x, y, c), (x, y, 1 - c)
        chips = [(1 - x, y), (x, 1 - y), (1 - x, 1 - y)]

        def slot(a, p):
            return outs[a].at[4 * p[0] + 2 * p[1] + p[2]]

        def cp(a, k, block, to, own=False):
            return pltpu.make_async_remote_copy(src_ref=ins[a] if own else slot(a, block), dst_ref=slot(a, block),
                                                send_sem=send_sems.at[a, k], recv_sem=recv_sems.at[a, k],
                                                device_id=to, device_id_type=MESH)

        started = []
        for a in range(n):
            loc = pltpu.make_async_copy(ins[a], slot(a, me), loc_sems.at[a])
            loc.start()
            started.append(cp(a, 0, me, sib, own=True))
            started += [cp(a, 1 + j, me, (*chip, c), own=True) for j, chip in enumerate(chips)]
        for s in started:
            s.start()
        for j, chip in enumerate(chips):
            for a in range(n):
                cp(a, 1 + j, (*chip, c), me).wait_recv()
                fwd = cp(a, 4 + j, (*chip, c), sib)
                fwd.start()
                started.append(fwd)
        for a in range(n):
            cp(a, 0, sib, me).wait_recv()
            for j, chip in enumerate(chips):
                cp(a, 4 + j, (*chip, 1 - c), me).wait_recv()
        for s in started:
            s.wait_send()
        for a in range(n):
            pltpu.make_async_copy(ins[a], slot(a, me), loc_sems.at[a]).wait()

    anyspec = pl.BlockSpec(memory_space=pl.ANY)
    return pl.pallas_call(
        kern, out_shape=_xchg_out_shapes(arrs, (False,) * n), in_specs=[anyspec] * n, out_specs=tuple([anyspec] * n),
        scratch_shapes=[pltpu.SemaphoreType.DMA((n, 7)), pltpu.SemaphoreType.DMA((n, 7)), pltpu.SemaphoreType.DMA((n,))],
        name=name)(*arrs)


def _adamw(parts, w, m, v, name):
    r, c = w.shape
    n_parts = parts.shape[0]
    tr = r
    for cand in (128, 64, 32, 16, 8):
        if r % cand == 0 and r > cand:
            tr = cand
            break
    c1 = 1.0 / (1.0 - ADAM_B1 ** ADAM_STEP)
    c2 = 1.0 / (1.0 - ADAM_B2 ** ADAM_STEP)

    def kern(p_ref, w_ref, m_ref, v_ref, g_ref, d_ref, m2_ref, v2_ref):
        g = p_ref[0].astype(f32)
        for i in range(1, n_parts):
            g = g + p_ref[i].astype(f32)
        g_ref[...] = g
        m2 = ADAM_B1 * m_ref[...] + (1.0 - ADAM_B1) * g
        v2 = ADAM_B2 * v_ref[...] + (1.0 - ADAM_B2) * (g * g)
        m2_ref[...] = m2
        v2_ref[...] = v2
        d_ref[...] = -ADAM_LR * ((m2 * c1) / (jnp.sqrt(v2 * c2) + ADAM_EPS) + ADAM_WD * w_ref[...])

    blk = pl.BlockSpec((tr, c), lambda i: (i, 0))
    sh = S((r, c), f32)
    return pl.pallas_call(
        kern, out_shape=(sh, sh, sh, sh), grid=(r // tr,),
        in_specs=[pl.BlockSpec((n_parts, tr, c), lambda i: (0, i, 0)), blk, blk, blk], out_specs=(blk, blk, blk, blk),
        compiler_params=_params("parallel"), name=name)(parts, w, m, v)


_SMALL = (("c_ctx", 1024), ("b_mod", 3072), ("norm_w", 1024), ("ssm_conv_b", 4096), ("dt_bias", 64), ("a_log", 64),
          ("d_skip", 32), ("ssm_norm_w", 2048), ("conf_conv_b", 1024), ("conf_ln_w", 1024), ("conf_ln_b", 1024),
          ("final_norm_w", 1024))
SMALL_TILE = 8 * 128


def _pack_small(d):
    rows = []
    for name, n in _SMALL:
        v = d[name].reshape(-1).astype(f32)
        pad = (-n) % SMALL_TILE
        if pad:
            v = jnp.concatenate([v, jnp.zeros((pad,), f32)])
        rows.append(v.reshape(-1, 128))
    return jnp.concatenate(rows, axis=0)


def _unpack_small(p, shapes):
    out, r0 = {}, 0
    for name, n in _SMALL:
        nr = 8 * ((n + SMALL_TILE - 1) // SMALL_TILE)
        out[name] = p[r0:r0 + nr].reshape(-1)[:n].reshape(shapes[name])
        r0 += nr
    return out


def _permute_w_in(w):
    return jnp.concatenate([w[:, 9280:11328], w[:, 4160:6208], w[:, 0:2048], w[:, 6208:9280], w[:, 2048:4096],
                            w[:, 4096:4160], jnp.zeros((w.shape[0], NP - DT0 - 64), w.dtype)], axis=1)


def _unpermute_w_in(wp):
    return jnp.concatenate([wp[:, PX0:PX0 + 2048], wp[:, PBC0:PBC0 + 2048], wp[:, DT0:DT0 + 64], wp[:, Z0:Z0 + 2048],
                            wp[:, GV0:GV0 + 3072], wp[:, G10:G10 + 2048]], axis=1)


def _cols_gathered(g):
    return jnp.transpose(g, (1, 0, 2)).reshape(g.shape[1], N_DEV * g.shape[2])


def _cols_to_blocks(a):
    r, c8 = a.shape
    return jnp.transpose(a.reshape(r, N_DEV, c8 // N_DEV), (1, 0, 2))


def kernel(x, c, ctx, c_ctx, w_mod, b_mod, norm_w, w_in, ssm_conv_w, ssm_conv_b, dt_bias, a_log, d_skip, ssm_norm_w, w_out_ssm, conf_conv_w, conf_conv_b, conf_ln_w, conf_ln_b, w_out_conf, w_out, final_norm_w, loss_target, m_c_ctx, m_w_mod, m_b_mod, m_norm_w, m_w_in, m_ssm_conv_w, m_ssm_conv_b, m_dt_bias, m_a_log, m_d_skip, m_ssm_norm_w, m_w_out_ssm, m_conf_conv_w, m_conf_conv_b, m_conf_ln_w, m_conf_ln_b, m_w_out_conf, m_w_out, m_final_norm_w, v_c_ctx, v_w_mod, v_b_mod, v_norm_w, v_w_in, v_ssm_conv_w, v_ssm_conv_b, v_dt_bias, v_a_log, v_d_skip, v_ssm_norm_w, v_w_out_ssm, v_conf_conv_w, v_conf_conv_b, v_conf_ln_w, v_conf_ln_b, v_w_out_conf, v_w_out, v_final_norm_w):
    L = x.shape[1]
    Lc = ctx.shape[1]
    T = L + Lc
    nlx = L // RT
    nxc = L // Q
    x2 = x.reshape(L, D)
    ctx2 = ctx.reshape(Lc, D)
    tgt = loss_target.reshape(L, D)

    gathered = _gather2([w_in[0].astype(bf16), w_mod[0].astype(bf16), ssm_conv_w[0], conf_conv_w[0]], name="gather_weights")
    wp = _permute_w_in(_cols_gathered(gathered[0]))
    wmod_bf = _cols_gathered(gathered[1])
    scw8 = jnp.concatenate([_cols_gathered(gathered[2]), jnp.zeros((8 - SK, 4096), f32)], axis=0)
    ccw32 = jnp.concatenate([_cols_gathered(gathered[3]), jnp.zeros((32 - CK, D), f32)], axis=0)

    norm_w1 = norm_w.reshape(1, D)
    scb = ssm_conv_b.reshape(1, 4096)
    bias_row = jnp.concatenate([dt_bias.reshape(1, 2 * NH), jnp.zeros((1, 128 - 2 * NH), f32)], axis=1)
    alog_row = jnp.concatenate([a_log.reshape(1, 2 * NH), jnp.zeros((1, 128 - 2 * NH), f32)], axis=1)
    dskip_full = jnp.repeat(d_skip.reshape(NH), HD).reshape(1, DI)
    snw = ssm_norm_w.reshape(1, DI)
    ccb = conf_conv_b.reshape(1, D)
    lnw = conf_ln_w.reshape(1, D)
    lnb = conf_ln_b.reshape(1, D)
    fw = final_norm_w.reshape(1, D)

    cc8 = jnp.concatenate([c.reshape(1, D), c_ctx.reshape(1, D), jnp.zeros((6, D), f32)], axis=0)
    mod, silu_rows = _mod_fwd(cc8, wmod_bf, b_mod.reshape(1, 3 * D))
    h, h_t = _prenorm(x2, ctx2, norm_w1, mod)
    proj, wos_g, woc_g, wo_g = _matmul(
        h, wp, f32, "proj_gather", tn=NP // 5,
        comm=([w_out_ssm[0].astype(bf16), w_out_conf[0].astype(bf16), w_out[0].astype(bf16)], (False,) * 3))
    wos_bf = wos_g.reshape(DI, D)
    woc_bf = woc_g.reshape(D, D)
    wo_bf = wo_g.reshape(D, D)
    xs = _ssm_conv_fwd(proj, scw8, scb, nlx, 0, f32, "ssm_conv_fwd_x")
    bcm = _ssm_conv_fwd(proj, scw8, scb, nlx, 1, bf16, "ssm_conv_fwd_bc")
    dt, la = _dt_prep(proj, bias_row, alog_row)
    consts_f, consts_b = _scan_consts(False), _scan_consts(True)
    yf, hp_f = _ssd_fwd(xs, bcm, dt, la, consts_f, False, nxc, "ssd_fwd_f")
    ysum, hp_b = _ssd_fwd(xs, bcm, dt, la, consts_b, True, nxc, "ssd_fwd_b", y_acc=yf)
    y, yn, yn_t = _ynorm_fwd(ysum, xs, proj, dskip_full, snw, L)
    bs = _matmul(yn, wos_bf, f32, "branch_ssm", tm=1024, tk=2048)
    u1, u3, u3_t = _conf_fwd(proj, ccw32, ccb, lnw, lnb, L)
    bc = _matmul(u3, woc_bf, f32, "branch_conf", tm=2048)
    merged, merged_t = _merge_fwd(bs, bc, proj)
    out = _matmul(merged, wo_bf, f32, "out_proj", tm=2048)
    dx1, dout, loss_acc, dfw, dgate = _final(x2, out, tgt, mod, fw)

    dmerged = _matmul(dout, wo_bf, f32, "d_merged", tb=True, tm=2048)
    g_wo = _matmul(merged_t, dout, bf16, "g_w_out", tm=1024, tk=2048)
    dproj, dbs, dbc = _merge_bwd(dmerged, bs, bc, proj)
    dyn = _matmul(dbs, wos_bf, f32, "d_yn", tb=True, tm=1024, tn=2048)
    g_wos = _matmul(yn_t, dbs, bf16, "g_w_out_ssm", tm=1024, tk=2048)
    du3 = _matmul(dbc, woc_bf, f32, "d_u3", tb=True, tm=2048)
    g_woc = _matmul(u3_t, dbc, bf16, "g_w_out_conf", tm=1024, tk=2048)
    dproj, g_ccw, g_ccb, g_lnw, g_lnb = _conf_bwd(du3, u1, proj, ccw32, lnw, lnb, dproj)
    dproj, dy, g_snw, dsk_cols = _ynorm_bwd(dyn, y, xs, proj, dskip_full, snw, dproj)
    acc_f = _ssd_bwd(xs, bcm, dy, dt, la, hp_f, dskip_full, consts_f, False, nxc, "ssd_bwd_f")
    dxbc, a1, a2, r2, sv = _ssd_bwd(xs, bcm, dy, dt, la, hp_b, dskip_full, consts_b, True, nxc, "ssd_bwd_b", acc=acc_f)
    dproj, g_dtb, g_alog = _dt_bwd(a1, a2, r2, sv, dt, la, proj, bias_row, alog_row, dproj)
    dpre, g_scw, g_scb = _ssm_conv_dpre(dxbc, proj, scw8, scb, nlx)
    dproj = _ssm_conv_t(dpre, scw8, dproj, nlx)
    g_wp, *parts_b = _matmul(
        h_t, dproj, bf16, "g_w_in_scatter", tm=1024, tn=NP // 5,
        comm=([g_wos.reshape(N_DEV, DI // N_DEV, D), g_woc.reshape(N_DEV, D // N_DEV, D), g_wo.reshape(N_DEV, D // N_DEV, D),
               _cols_to_blocks(g_scw[:SK]), _cols_to_blocks(g_ccw[:CK])], (True,) * 5))
    dh, parts_a = _matmul(dproj, wp, f32, "d_h_scatter", tb=True, tk=NP // 5,
                          comm=([_cols_to_blocks(_unpermute_w_in(g_wp))], (True,)))
    parts = [parts_a] + parts_b
    gx, g_nw, macc = _prenorm_bwd(x2, ctx2, dh, dx1, norm_w1, mod)
    dmod_x = jnp.concatenate([macc[0:1], macc[1:2], dgate], axis=1)
    dmod_c = jnp.concatenate([macc[2:3], macc[3:4], jnp.zeros((1, D), f32)], axis=1)
    dmod8 = jnp.concatenate([dmod_x, dmod_c, jnp.zeros((6, 3 * D), f32)], axis=0)
    ct = jnp.concatenate([c.reshape(D, 1), c_ctx.reshape(D, 1), jnp.zeros((D, 126), f32)], axis=1)
    g_bmod, g_cctx = _mod_bwd(ct, dmod8, wmod_bf)
    g_dskip = _head_sums(dsk_cols.reshape(NH, HD))[:, 0]

    small_g = _pack_small({
        "c_ctx": g_cctx[:, 0], "b_mod": g_bmod, "norm_w": g_nw, "ssm_conv_b": g_scb, "dt_bias": g_dtb[0, :2 * NH],
        "a_log": g_alog[0, :2 * NH], "d_skip": g_dskip, "ssm_norm_w": g_snw, "conf_conv_b": g_ccb, "conf_ln_w": g_lnw,
        "conf_ln_b": g_lnb, "final_norm_w": dfw})
    fac = jnp.concatenate([silu_rows[0:1].reshape(D // 128, 128), dmod_x.reshape(3 * D // 128, 128),
                           dmod_c.reshape(3 * D // 128, 128)], axis=0)
    small_parts, fac_all = _exchange([small_g, fac], (False, False), name="exchange_tail")
    nr = D // 128
    sct = jnp.concatenate([fac_all[:, 0:nr].reshape(N_DEV, D).T, silu_rows[1:2].T, jnp.zeros((D, 128 - N_DEV - 1), f32)], axis=1)
    my_cols = (4 * lax.axis_index("x") + 2 * lax.axis_index("y") + lax.axis_index("c")) * (3 * D // N_DEV)
    dmx_all = lax.dynamic_slice(fac_all[:, nr:4 * nr].reshape(N_DEV, 3 * D), (0, my_cols), (N_DEV, 3 * D // N_DEV))
    dmc_all = lax.dynamic_slice(fac_all[:, 4 * nr:7 * nr].reshape(N_DEV, 3 * D), (0, my_cols), (N_DEV, 3 * D // N_DEV))
    g_wmod = _wmod_grad(sct, dmx_all, dmc_all)
    parts = [parts[0], g_wmod[None]] + parts[1:]

    given = dict(c_ctx=c_ctx, w_mod=w_mod, b_mod=b_mod, norm_w=norm_w, w_in=w_in, ssm_conv_w=ssm_conv_w, ssm_conv_b=ssm_conv_b,
                 dt_bias=dt_bias, a_log=a_log, d_skip=d_skip, ssm_norm_w=ssm_norm_w, w_out_ssm=w_out_ssm, conf_conv_w=conf_conv_w,
                 conf_conv_b=conf_conv_b, conf_ln_w=conf_ln_w, conf_ln_b=conf_ln_b, w_out_conf=w_out_conf, w_out=w_out,
                 final_norm_w=final_norm_w)
    ms = dict(c_ctx=m_c_ctx, w_mod=m_w_mod, b_mod=m_b_mod, norm_w=m_norm_w, w_in=m_w_in, ssm_conv_w=m_ssm_conv_w,
              ssm_conv_b=m_ssm_conv_b, dt_bias=m_dt_bias, a_log=m_a_log, d_skip=m_d_skip, ssm_norm_w=m_ssm_norm_w,
              w_out_ssm=m_w_out_ssm, conf_conv_w=m_conf_conv_w, conf_conv_b=m_conf_conv_b, conf_ln_w=m_conf_ln_w,
              conf_ln_b=m_conf_ln_b, w_out_conf=m_w_out_conf, w_out=m_w_out, final_norm_w=m_final_norm_w)
    vs = dict(c_ctx=v_c_ctx, w_mod=v_w_mod, b_mod=v_b_mod, norm_w=v_norm_w, w_in=v_w_in, ssm_conv_w=v_ssm_conv_w,
              ssm_conv_b=v_ssm_conv_b, dt_bias=v_dt_bias, a_log=v_a_log, d_skip=v_d_skip, ssm_norm_w=v_ssm_norm_w,
              w_out_ssm=v_w_out_ssm, conf_conv_w=v_conf_conv_w, conf_conv_b=v_conf_conv_b, conf_ln_w=v_conf_ln_w,
              conf_ln_b=v_conf_ln_b, w_out_conf=v_w_out_conf, w_out=v_w_out, final_norm_w=v_final_norm_w)
    grads, deltas, new_m, new_v = {}, {}, {}, {}
    sharded = ("w_in", "w_mod", "w_out_ssm", "w_out_conf", "w_out", "ssm_conv_w", "conf_conv_w")
    for i, nm in enumerate(sharded):
        shp = given[nm].shape
        w2 = given[nm].reshape(shp[1], shp[2])
        res = _adamw(parts[i], w2, ms[nm].reshape(w2.shape), vs[nm].reshape(w2.shape), "adamw_" + nm)
        grads[nm], deltas[nm], new_m[nm], new_v[nm] = [r.reshape(shp) for r in res]
    shapes = {nm: given[nm].shape for nm, _ in _SMALL}
    res = _adamw(small_parts, _pack_small(given), _pack_small(ms), _pack_small(vs), "adamw_small")
    for dst, packed in zip((grads, deltas, new_m, new_v), res):
        dst.update(_unpack_small(packed, shapes))

    loss = lax.psum(loss_acc[0, 0], ("x", "y", "c"))
    order = ("c_ctx", "w_mod", "b_mod", "norm_w", "w_in", "ssm_conv_w", "ssm_conv_b", "dt_bias", "a_log", "d_skip", "ssm_norm_w",
             "w_out_ssm", "conf_conv_w", "conf_conv_b", "conf_ln_w", "conf_ln_b", "w_out_conf", "w_out", "final_norm_w")
    return (loss, gx.reshape(1, L, D), *[grads[n] for n in order], *[deltas[n] for n in order],
            *[new_m[n] for n in order], *[new_v[n] for n in order])
```

```python
import jax
import jax.numpy as jnp
from jax import lax
from jax.experimental import pallas as pl
from jax.experimental.pallas import tpu as pltpu

f32 = jnp.float32
bf16 = jnp.bfloat16

D = 1024
DI = 2048
NG = 8
HPG = 4
HD = 64
GW = HPG * HD
NS = 128
NH = 32
Q = 128
GRID_W = 64
CK = 31
SK = 4
EPS = 1e-6
RT = 256
N_DEV = 8
IN_COLS = 11328
G10, G20, PBC0, PX0, GV0, GG0, CG0, Z0, DT0, NP = 0, 1024, 2048, 4096, 6144, 7168, 8192, 9216, 11264, 11520
CONV_CT = 2048
B0, C0 = 2048, 3072
VMEM_LIMIT = 50 * 1024 * 1024
NEG = -1e30

ADAM_LR, ADAM_B1, ADAM_B2, ADAM_EPS, ADAM_WD, ADAM_STEP = 0.001, 0.9, 0.999, 1e-08, 0.01, 10

MESH = pl.DeviceIdType.MESH
S = jax.ShapeDtypeStruct


def _params(*sem):
    return pltpu.CompilerParams(dimension_semantics=tuple(sem) if sem else None, vmem_limit_bytes=VMEM_LIMIT)


def _sig(x):
    return 1.0 / (1.0 + jnp.exp(-x))


def _silu(x):
    return x * _sig(x)


def _dsilu(x, s):
    return s * (1.0 + x * (1.0 - s))


def _dot(a, b):
    return jnp.dot(a, b, preferred_element_type=f32)


def _dot_nt(a, b):
    return lax.dot_general(a, b, (((1,), (1,)), ((), ())), preferred_element_type=f32)


def _dot_tn(a, b):
    return lax.dot_general(a, b, (((0,), (0,)), ((), ())), preferred_element_type=f32)


def _dot3(t_bf, v):
    v1 = v.astype(bf16)
    r1 = v - v1.astype(f32)
    v2 = r1.astype(bf16)
    v3 = (r1 - v2.astype(f32)).astype(bf16)
    return _dot(t_bf, v1) + _dot(t_bf, v2) + _dot(t_bf, v3)


def _pick(n, prefs):
    for p in prefs:
        if n % p == 0:
            return p
    return n


def _full(shape):
    nd = len(shape)
    return pl.BlockSpec(shape, lambda *_: (0,) * nd)


def _matmul(a, b, out_dtype, name, tm=None, tn=None, tk=None, tb=False, comm=None):
    m, k = a.shape
    n = b.shape[0] if tb else b.shape[1]
    tm = tm if tm and m % tm == 0 else _pick(m, (768, 512, 256, 128))
    tn = tn if tn and n % tn == 0 else _pick(n, (1024, 512, 256, 128))
    tk = tk if tk and k % tk == 0 else _pick(k, (1024, 768, 512, 256, 128))
    nk = k // tk
    gi, gj = m // tm, n // tn
    carrs, modes = comm if comm else ((), ())
    nc = len(carrs)

    def kern(*refs):
        a_ref, b_ref = refs[:2]
        cins = refs[2:2 + nc]
        o_ref = refs[2 + nc]
        couts = refs[3 + nc:3 + 2 * nc]
        acc_ref = refs[3 + 2 * nc]
        sems = refs[4 + 2 * nc:]
        i, j, kk = pl.program_id(0), pl.program_id(1), pl.program_id(2)
        if nc:
            @pl.when(jnp.logical_and(jnp.logical_and(i == 0, j == 0), kk == 0))
            def _():
                _xchg_start(cins, couts, *sems, modes)

        part = _dot_nt(a_ref[...], b_ref[...]) if tb else _dot(a_ref[...], b_ref[...])
        if nk == 1:
            o_ref[...] = part.astype(o_ref.dtype)
        else:
            @pl.when(kk == 0)
            def _():
                acc_ref[...] = part

            @pl.when(kk > 0)
            def _():
                acc_ref[...] += part

            @pl.when(kk == nk - 1)
            def _():
                o_ref[...] = acc_ref[...].astype(o_ref.dtype)

        if nc:
            @pl.when(jnp.logical_and(jnp.logical_and(i == gi - 1, j == gj - 1), kk == nk - 1))
            def _():
                _xchg_wait(cins, couts, *sems, modes)

    anyspec = pl.BlockSpec(memory_space=pl.ANY)
    bspec = pl.BlockSpec((tn, tk), lambda i, j, kk: (j, kk)) if tb else pl.BlockSpec((tk, tn), lambda i, j, kk: (kk, j))
    out_shape = (S((m, n), out_dtype),) + _xchg_out_shapes(carrs, modes)
    res = pl.pallas_call(
        kern, out_shape=out_shape, grid=(gi, gj, nk),
        in_specs=[pl.BlockSpec((tm, tk), lambda i, j, kk: (i, kk)), bspec] + [anyspec] * nc,
        out_specs=(pl.BlockSpec((tm, tn), lambda i, j, kk: (i, j)),) + (anyspec,) * nc,
        scratch_shapes=[pltpu.VMEM((tm, tn), f32)] + (_xchg_sems(nc) if nc else []),
        compiler_params=_params(*((("arbitrary",) * 3) if nc else ("parallel", "parallel", "arbitrary"))), name=name)(a, b, *carrs)
    return res if nc else res[0]


def _mod_fwd(cc8, w_mod_bf, b_mod):
    def kern(c_ref, w_ref, b_ref, o_ref, s_ref):
        s = _silu(c_ref[...])
        s_ref[...] = s
        o_ref[...] = _dot(s.astype(bf16), w_ref[...]) + b_ref[...]

    return pl.pallas_call(kern, out_shape=(S((8, 3 * D), f32), S((8, D), f32)), compiler_params=_params(),
                          name="mod_fwd")(cc8, w_mod_bf, b_mod)


def _mod_bwd(ct, dmod8, w_mod_bf):
    tc = 512
    nj = 3 * D // tc

    def kern(ct_ref, dm_ref, w_ref, db_ref, dc_ref):
        j = pl.program_id(0)
        cx = ct_ref[:, 1:2]
        sx = _sig(cx)
        dmc = dm_ref[1:2, :]
        db_ref[...] = dm_ref[0:1, :] + dmc
        t = jnp.sum(w_ref[...].astype(f32) * dmc.astype(bf16).astype(f32), axis=1, keepdims=True) * _dsilu(cx, sx)

        @pl.when(j == 0)
        def _():
            dc_ref[...] = jnp.zeros_like(dc_ref)

        dc_ref[...] += jnp.broadcast_to(t, (D, 128))

    return pl.pallas_call(
        kern, out_shape=(S((1, 3 * D), f32), S((D, 128), f32)), grid=(nj,),
        in_specs=[_full((D, 128)), pl.BlockSpec((8, tc), lambda j: (0, j)), pl.BlockSpec((D, tc), lambda j: (0, j))],
        out_specs=(pl.BlockSpec((1, tc), lambda j: (0, j)), _full((D, 128))),
        compiler_params=_params("arbitrary"), name="mod_bwd")(ct, dmod8, w_mod_bf)


def _wmod_grad(sct, dmx, dmc):
    cols = dmx.shape[1]

    def kern(s_ref, dmx_ref, dmc_ref, g_ref):
        dmc_sum = dmc_ref[0:1, :]
        for d in range(1, N_DEV):
            dmc_sum = dmc_sum + dmc_ref[d:d + 1, :]
        g = s_ref[:, N_DEV:N_DEV + 1] * dmc_sum
        for d in range(N_DEV):
            g = g + s_ref[:, d:d + 1] * dmx_ref[d:d + 1, :]
        g_ref[...] = g

    return pl.pallas_call(kern, out_shape=S((D, cols), f32), compiler_params=_params(), name="wmod_grad")(sct, dmx, dmc)


def _prenorm(x, ctx, norm_w, mod):
    L, Lc = x.shape[0], ctx.shape[0]
    nlx, nt = L // RT, (L + Lc) // RT

    def kern(x_ref, c_ref, nw_ref, mod_ref, h_ref, ht_ref):
        i = pl.program_id(0)
        is_c = i >= nlx
        xv = jnp.where(is_c, c_ref[...], x_ref[...])
        shift = jnp.where(is_c, mod_ref[1:2, 0:D], mod_ref[0:1, 0:D])
        scale = jnp.where(is_c, mod_ref[1:2, D:2 * D], mod_ref[0:1, D:2 * D])
        r = lax.rsqrt(jnp.mean(xv * xv, axis=1, keepdims=True) + EPS)
        hv = (xv * r) * nw_ref[...] * (1.0 + scale) + shift
        h_ref[...] = hv.astype(bf16)
        ht_ref[...] = jnp.transpose(hv).astype(bf16)

    return pl.pallas_call(
        kern, out_shape=(S((L + Lc, D), bf16), S((D, L + Lc), bf16)), grid=(nt,),
        in_specs=[pl.BlockSpec((RT, D), lambda i: (jnp.minimum(i, nlx - 1), 0)),
                  pl.BlockSpec((RT, D), lambda i: (jnp.maximum(i - nlx, 0), 0)),
                  _full((1, D)), _full((8, 3 * D))],
        out_specs=(pl.BlockSpec((RT, D), lambda i: (i, 0)), pl.BlockSpec((D, RT), lambda i: (0, i))),
        compiler_params=_params("parallel"), name="prenorm")(x, ctx, norm_w, mod)


def _prenorm_bwd(x, ctx, dh, dx1, norm_w, mod):
    L, Lc = x.shape[0], ctx.shape[0]
    nlx, nt = L // RT, (L + Lc) // RT

    def kern(x_ref, c_ref, dh_ref, dx1_ref, nw_ref, mod_ref, gx_ref, dnw_ref, acc_ref):
        i = pl.program_id(0)
        is_c = i >= nlx

        @pl.when(i == 0)
        def _():
            dnw_ref[...] = jnp.zeros_like(dnw_ref)
            acc_ref[...] = jnp.zeros_like(acc_ref)

        xv = jnp.where(is_c, c_ref[...], x_ref[...])
        scale = jnp.where(is_c, mod_ref[1:2, D:2 * D], mod_ref[0:1, D:2 * D])
        nw = nw_ref[...]
        r = lax.rsqrt(jnp.mean(xv * xv, axis=1, keepdims=True) + EPS)
        xn = xv * r
        dh = dh_ref[...]
        dsh = jnp.sum(dh, axis=0, keepdims=True)
        dsc = jnp.sum(dh * (xn * nw), axis=0, keepdims=True)
        dxnw = dh * (1.0 + scale)
        dnw_ref[...] += jnp.sum(dxnw * xn, axis=0, keepdims=True)
        dxn = dxnw * nw
        dx = r * (dxn - xn * jnp.mean(dxn * xn, axis=1, keepdims=True))

        @pl.when(jnp.logical_not(is_c))
        def _():
            gx_ref[...] = dx1_ref[...] + dx
            acc_ref[0:1, :] += dsh
            acc_ref[1:2, :] += dsc

        @pl.when(is_c)
        def _():
            acc_ref[2:3, :] += dsh
            acc_ref[3:4, :] += dsc

    xmap = lambda i: (jnp.minimum(i, nlx - 1), 0)
    return pl.pallas_call(
        kern, out_shape=(S((L, D), f32), S((1, D), f32), S((8, D), f32)), grid=(nt,),
        in_specs=[pl.BlockSpec((RT, D), xmap), pl.BlockSpec((RT, D), lambda i: (jnp.maximum(i - nlx, 0), 0)),
                  pl.BlockSpec((RT, D), lambda i: (i, 0)), pl.BlockSpec((RT, D), xmap), _full((1, D)), _full((8, 3 * D))],
        out_specs=(pl.BlockSpec((RT, D), xmap), _full((1, D)), _full((8, D))),
        compiler_params=_params("arbitrary"), name="prenorm_bwd")(x, ctx, dh, dx1, norm_w, mod)


def _xbc_col(j):
    return jnp.where(j == 0, PX0 // CONV_CT, PBC0 // CONV_CT)


def _halo_specs(nt_rows, ct, col=lambda j: j):
    cur = pl.BlockSpec((RT, ct), lambda i, j: (i, col(j)))
    prev = pl.BlockSpec((8, ct), lambda i, j: (jnp.maximum(i * (RT // 8) - 1, 0), col(j)))
    nxt = pl.BlockSpec((8, ct), lambda i, j: (jnp.minimum((i + 1) * (RT // 8), nt_rows // 8 - 1), col(j)))
    return cur, prev, nxt


def _fill_halo(scr, cur_ref, prev_ref, next_ref, i, nlx, nt):
    prev_ok = jnp.logical_and(i != 0, i != nlx)
    next_ok = jnp.logical_and(i != nlx - 1, i != nt - 1)
    scr[0:8, :] = jnp.where(prev_ok, prev_ref[...], 0.0)
    scr[8:8 + RT, :] = cur_ref[...]
    scr[8 + RT:16 + RT, :] = jnp.where(next_ok, next_ref[...], 0.0)


CONV_RB = 32


def _conv_blocks(ct):
    return [(slice(cb * 128, (cb + 1) * 128), r0) for cb in range(ct // 128) for r0 in range(0, RT, CONV_RB)]


def _taps(scr, cs, r0, shifts):
    blk = scr[r0:r0 + CONV_RB + 16, cs]
    n = CONV_RB + 16
    return [(blk if d == 0 else pltpu.roll(blk, (-d) % n, 0))[8:8 + CONV_RB, :] for d in shifts]


def _ssm_conv_fwd(proj, w8, b, nlx, half, out_dtype, name):
    T = proj.shape[0]
    nt = T // RT
    ct = CONV_CT
    cur, prev, nxt = _halo_specs(T, ct, lambda j: _xbc_col(j + half))

    def kern(cur_ref, prev_ref, next_ref, w_ref, b_ref, o_ref, scr):
        i = pl.program_id(0)
        _fill_halo(scr, cur_ref, prev_ref, next_ref, i, nlx, nt)
        for cs, r0 in _conv_blocks(ct):
            taps = _taps(scr, cs, r0, [k - 2 for k in range(SK)])
            acc = jnp.broadcast_to(b_ref[:, cs], (CONV_RB, 128))
            for k in range(SK):
                acc = acc + w_ref[k:k + 1, cs] * taps[k]
            o_ref[r0:r0 + CONV_RB, cs] = _silu(acc).astype(out_dtype)

    return pl.pallas_call(
        kern, out_shape=S((T, ct), out_dtype), grid=(nt, 1),
        in_specs=[cur, prev, nxt, pl.BlockSpec((8, ct), lambda i, j: (0, j + half)),
                  pl.BlockSpec((1, ct), lambda i, j: (0, j + half))],
        out_specs=pl.BlockSpec((RT, ct), lambda i, j: (i, j)),
        scratch_shapes=[pltpu.VMEM((RT + 16, ct), f32)],
        compiler_params=_params("parallel", "parallel"), name=name)(proj, proj, proj, w8, b)


def _ssm_conv_dpre(dxbc, proj, w8, b, nlx):
    T = proj.shape[0]
    nt = T // RT
    ct = CONV_CT
    cur = pl.BlockSpec((RT, ct), lambda j, i: (i, j))
    pcur = pl.BlockSpec((RT, ct), lambda j, i: (i, _xbc_col(j)))
    prev = pl.BlockSpec((8, ct), lambda j, i: (jnp.maximum(i * (RT // 8) - 1, 0), _xbc_col(j)))
    nxt = pl.BlockSpec((8, ct), lambda j, i: (jnp.minimum((i + 1) * (RT // 8), T // 8 - 1), _xbc_col(j)))

    def kern(d_ref, cur_ref, prev_ref, next_ref, w_ref, b_ref, dpre_ref, dw_ref, db_ref, scr):
        i = pl.program_id(1)
        _fill_halo(scr, cur_ref, prev_ref, next_ref, i, nlx, nt)

        @pl.when(i == 0)
        def _():
            dw_ref[...] = jnp.zeros_like(dw_ref)
            db_ref[...] = jnp.zeros_like(db_ref)

        for cb in range(ct // 128):
            cs = slice(cb * 128, (cb + 1) * 128)
            db_acc = jnp.zeros((CONV_RB, 128), f32)
            dw_acc = [jnp.zeros((CONV_RB, 128), f32) for _ in range(SK)]
            for r0 in range(0, RT, CONV_RB):
                taps = _taps(scr, cs, r0, [k - 2 for k in range(SK)])
                pre = jnp.broadcast_to(b_ref[:, cs], (CONV_RB, 128))
                for k in range(SK):
                    pre = pre + w_ref[k:k + 1, cs] * taps[k]
                dpre = d_ref[r0:r0 + CONV_RB, cs] * _dsilu(pre, _sig(pre))
                dpre_ref[r0:r0 + CONV_RB, cs] = dpre
                db_acc = db_acc + dpre
                dw_acc = [dw_acc[k] + dpre * taps[k] for k in range(SK)]
            db_ref[:, cs] += jnp.sum(db_acc, axis=0, keepdims=True)
            for k in range(SK):
                dw_ref[k:k + 1, cs] += jnp.sum(dw_acc[k], axis=0, keepdims=True)

    return pl.pallas_call(
        kern, out_shape=(S((T, 4096), f32), S((8, 4096), f32), S((1, 4096), f32)), grid=(4096 // ct, nt),
        in_specs=[cur, pcur, prev, nxt, pl.BlockSpec((8, ct), lambda j, i: (0, j)), pl.BlockSpec((1, ct), lambda j, i: (0, j))],
        out_specs=(cur, pl.BlockSpec((8, ct), lambda j, i: (0, j)), pl.BlockSpec((1, ct), lambda j, i: (0, j))),
        scratch_shapes=[pltpu.VMEM((RT + 16, ct), f32)],
        compiler_params=_params("parallel", "arbitrary"), name="ssm_conv_dpre")(dxbc, proj, proj, proj, w8, b)


def _ssm_conv_t(dpre, w8, dproj, nlx):
    T = dpre.shape[0]
    nt = T // RT
    ct = CONV_CT
    cur, prev, nxt = _halo_specs(T, ct)

    def kern(cur_ref, prev_ref, next_ref, w_ref, _alias, o_ref, scr):
        i = pl.program_id(0)
        _fill_halo(scr, cur_ref, prev_ref, next_ref, i, nlx, nt)
        for cs, r0 in _conv_blocks(ct):
            taps = _taps(scr, cs, r0, [2 - k for k in range(SK)])
            acc = jnp.zeros((CONV_RB, 128), f32)
            for k in range(SK):
                acc = acc + w_ref[k:k + 1, cs] * taps[k]
            o_ref[r0:r0 + CONV_RB, cs] = acc.astype(bf16)

    return pl.pallas_call(
        kern, out_shape=S(dproj.shape, bf16), grid=(nt, 4096 // ct),
        in_specs=[cur, prev, nxt, pl.BlockSpec((8, ct), lambda i, j: (0, j)), pl.BlockSpec(memory_space=pl.ANY)],
        out_specs=pl.BlockSpec((RT, ct), lambda i, j: (i, _xbc_col(j))),
        scratch_shapes=[pltpu.VMEM((RT + 16, ct), f32)], input_output_aliases={4: 0},
        compiler_params=_params("parallel", "parallel"), name="ssm_conv_t")(dpre, dpre, dpre, w8, dproj)


def _tri():
    li = lax.broadcasted_iota(jnp.int32, (Q, Q), 0)
    si = lax.broadcasted_iota(jnp.int32, (Q, Q), 1)
    return (si <= li).astype(bf16), (si >= li).astype(bf16)


def _dt_prep(proj, bias_row, alog_row):
    T = proj.shape[0]
    nch = T // Q

    def kern(raw_ref, b_ref, al_ref, dt_ref, la_ref):
        lane = lax.broadcasted_iota(jnp.int32, (Q, 128), 1)
        v = raw_ref[...] + b_ref[...]
        dt = jnp.maximum(v, 0.0) + jnp.log1p(jnp.exp(-jnp.abs(v)))
        a = jnp.where(lane[0:1, :] < 2 * NH, -jnp.exp(al_ref[...]), 0.0)
        da = dt * a
        tri, trit = _tri()
        dt_ref[...] = dt
        la_ref[...] = jnp.where(lane < NH, _dot3(tri, da), _dot3(trit, da))

    return pl.pallas_call(
        kern, out_shape=(S((T, 128), f32), S((T, 128), f32)), grid=(nch,),
        in_specs=[pl.BlockSpec((Q, 128), lambda c: (c, DT0 // 128)), _full((1, 128)), _full((1, 128))],
        out_specs=(pl.BlockSpec((Q, 128), lambda c: (c, 0)), pl.BlockSpec((Q, 128), lambda c: (c, 0))),
        compiler_params=_params("parallel"), name="dt_prep")(proj, bias_row, alog_row)


def _dt_bwd(a1, a2, r2, sv, dt, la, proj, bias_row, alog_row, dproj):
    T = proj.shape[0]
    nch = T // Q
    blk = pl.BlockSpec((Q, 128), lambda c: (c, 0))

    def kern(a1_ref, a2_ref, r2_ref, s_ref, dt_ref, la_ref, raw_ref, b_ref, al_ref, _alias, o_ref, db_ref, dal_ref):
        c = pl.program_id(0)

        @pl.when(c == 0)
        def _():
            db_ref[...] = jnp.zeros_like(db_ref)
            dal_ref[...] = jnp.zeros_like(dal_ref)

        lane = lax.broadcasted_iota(jnp.int32, (Q, 128), 1)
        row = lax.broadcasted_iota(jnp.int32, (Q, 128), 0)
        fwd = lane < NH
        dt = dt_ref[...]
        la = la_ref[...]
        a2v = a2_ref[...]
        r2v = r2_ref[...]
        a = jnp.where(lane[0:1, :] < 2 * NH, -jnp.exp(al_ref[...]), 0.0)
        la_e = jnp.where(fwd[0:1, :], la[Q - 1:Q, :], la[0:1, :])
        is_end = row == jnp.where(fwd, Q - 1, 0)
        e_end = jnp.exp(la_e - la)
        wend = e_end * dt
        extra = s_ref[0:1, :] * jnp.exp(la_e) + jnp.sum(wend * a2v, axis=0, keepdims=True)
        dla = a1_ref[...] - dt * r2v - wend * a2v + jnp.where(is_end, extra, 0.0)
        tri, trit = _tri()
        rcs = jnp.where(fwd, _dot3(trit, dla), _dot3(tri, dla))
        ddt = r2v + e_end * a2v + a * rcs
        dal_ref[...] += a * jnp.sum(dt * rcs, axis=0, keepdims=True)
        draw = jnp.where(lane < 2 * NH, ddt * _sig(raw_ref[...] + b_ref[...]), 0.0)
        db_ref[...] += jnp.sum(draw, axis=0, keepdims=True)
        o_ref[...] = jnp.zeros_like(o_ref)
        o_ref[:, 0:128] = draw.astype(bf16)

    return pl.pallas_call(
        kern, out_shape=(S(dproj.shape, bf16), S((1, 128), f32), S((1, 128), f32)), grid=(nch,),
        in_specs=[blk, blk, blk, blk, blk, blk, pl.BlockSpec((Q, 128), lambda c: (c, DT0 // 128)),
                  _full((1, 128)), _full((1, 128)), pl.BlockSpec(memory_space=pl.ANY)],
        out_specs=(pl.BlockSpec((Q, NP - DT0), lambda c: (c, DT0 // (NP - DT0))), _full((1, 128)), _full((1, 128))),
        input_output_aliases={9: 0},
        compiler_params=_params("arbitrary"), name="dt_bwd")(a1, a2, r2, sv, dt, la, proj, bias_row, alog_row, dproj)


def _split2(v):
    hi = v.astype(bf16)
    lo = (v - hi.astype(f32)).astype(bf16)
    return jnp.concatenate([hi, lo], axis=1)


def _scan_consts(rev):
    hoff = NH if rev else 0
    g = jnp.arange(NG, dtype=jnp.int32)[:, None, None]

    def rc(nr, ncol):
        return jnp.arange(nr, dtype=jnp.int32)[None, :, None], jnp.arange(ncol, dtype=jnp.int32)[None, None, :]

    r, c = rc(2 * 128, HPG * HD)
    sel_w = (lax.rem(r, 128) == hoff + HPG * g + c // HD).astype(bf16)
    r, c = rc(HPG * HD, 128)
    ind_h = (c == hoff + HPG * g + r // HD).astype(bf16)
    r, c = rc(2 * HPG * Q, 128)
    ind_e = (c == hoff + HPG * g + lax.rem(r, HPG * Q) // Q).astype(bf16)
    return sel_w, ind_h, ind_e


def _masks(rev):
    li = lax.broadcasted_iota(jnp.int32, (Q, Q), 0)
    si = lax.broadcasted_iota(jnp.int32, (Q, Q), 1)
    mask = (li <= si) if rev else (li >= si)
    mask_t = (li >= si) if rev else (li <= si)
    lane = lax.broadcasted_iota(jnp.int32, (Q, HPG * HD), 1)
    hms = [jnp.logical_and(lane >= r * HD, lane < (r + 1) * HD) for r in range(HPG)]
    return mask, mask_t, hms


def _mine(hoff):
    lane = lax.broadcasted_iota(jnp.int32, (Q, 128), 1)
    return jnp.logical_and(lane >= hoff, lane < hoff + NH)


def _head_row(vals, hc0):
    lane = lax.broadcasted_iota(jnp.int32, (1, HPG * HD), 1)
    out = jnp.zeros((1, HPG * HD), f32)
    for r in range(HPG):
        out = jnp.where(jnp.logical_and(lane >= r * HD, lane < (r + 1) * HD), vals[:, hc0 + r:hc0 + r + 1], out)
    return out


def _chunk_of(j, rev, nxc, nch):
    return (nch - 1 - j) if rev else lax.rem(j + nxc, nch)


def _ssd_fwd(xs, bc, dt, la, consts, rev, nxc, name, y_acc=None):
    T = xs.shape[0]
    nch = T // Q
    hoff = NH if rev else 0
    e = 0 if rev else Q - 1
    cm = lambda j: _chunk_of(j, rev, nxc, nch)
    sel_w = consts[0]
    has_acc = y_acc is not None

    def kern(*refs):
        xs_ref, bc_ref, dt_ref, la_ref, sw_ref = refs[:5]
        yacc_ref = refs[5] if has_acc else None
        y_ref, hp_ref, h_ref = refs[5 + has_acc:]
        j = pl.program_id(0)

        @pl.when(j == 0)
        def _():
            h_ref[...] = jnp.zeros_like(h_ref)

        hp_ref[...] = h_ref[...]
        mask, _, hms = _masks(rev)
        la_all = la_ref[...]
        dt_all = dt_ref[...]
        la_t = jnp.transpose(la_all)
        dt_t = jnp.transpose(dt_all)
        la_e = la_all[e:e + 1, :]
        w2 = _split2(jnp.exp(jnp.where(_mine(hoff), la_e - la_all, 0.0)) * dt_all)
        e2 = _split2(jnp.exp(la_all))
        ela_e = jnp.exp(la_e)
        for g in range(NG):
            hc0 = hoff + g * HPG
            x = xs_ref[:, g * GW:(g + 1) * GW]
            bb = bc_ref[:, g * NS:(g + 1) * NS]
            cb = bc_ref[:, NG * NS + g * NS:NG * NS + (g + 1) * NS]
            ht = h_ref[g * NS:(g + 1) * NS, :]
            scores = _dot_nt(cb, bb)
            yoff = _dot(cb, ht.astype(bf16))
            wend = _dot(w2, sw_ref[g])
            expla = _dot(e2, sw_ref[g])
            mixes, xstack = [], []
            for r in range(HPG):
                hc = hc0 + r
                la_rep = jnp.broadcast_to(la_all[:, hc:hc + 1], (Q, 128))
                decay = jnp.exp(jnp.where(mask, la_rep - la_t[hc:hc + 1, :], NEG))
                mixes.append((scores * decay * dt_t[hc:hc + 1, :]).astype(bf16))
                xstack.append(jnp.where(hms[r], x, 0.0).astype(bf16))
            y = _dot(jnp.concatenate(mixes, axis=1), jnp.concatenate(xstack, axis=0)) + yoff * expla
            if has_acc:
                y = y + yacc_ref[:, g * GW:(g + 1) * GW]
            y_ref[:, g * GW:(g + 1) * GW] = y
            h_ref[g * NS:(g + 1) * NS, :] = ht * _head_row(ela_e, hc0) + _dot_tn(bb, (x * wend).astype(bf16))

    row = lambda j: (cm(j), 0)
    yblk = pl.BlockSpec((Q, DI), row)
    return pl.pallas_call(
        kern, out_shape=(S((T, DI), f32), S((nch, NG * NS, HPG * HD), f32)), grid=(nch,),
        in_specs=[yblk, pl.BlockSpec((Q, 2 * NG * NS), row), pl.BlockSpec((Q, 128), row), pl.BlockSpec((Q, 128), row),
                  _full(sel_w.shape)] + ([yblk] if has_acc else []),
        out_specs=(yblk, pl.BlockSpec((None, NG * NS, HPG * HD), lambda j: (cm(j), 0, 0))),
        scratch_shapes=[pltpu.VMEM((NG * NS, HPG * HD), f32)],
        input_output_aliases={5: 0} if has_acc else {},
        compiler_params=_params("arbitrary"), name=name)(xs, bc, dt, la, sel_w, *([y_acc] if has_acc else []))


def _ssd_bwd(xs, bc, dy, dt, la, hprev, dskip_full, consts, rev, nxc, name, acc=None):
    T = xs.shape[0]
    nch = T // Q
    hoff = NH if rev else 0
    e = 0 if rev else Q - 1
    cm = lambda j: _chunk_of(nch - 1 - j, rev, nxc, nch)
    has_acc = acc is not None
    sel_w, ind_h, ind_e = consts

    def kern(*refs):
        xs_ref, bc_ref, dy_ref, dt_ref, la_ref, hp_ref, dsk_ref, sw_ref, ih_ref, ie_ref = refs[:10]
        k = 10
        if has_acc:
            dxbc_in, a1_in, a2_in, r2_in, s_in = refs[k:k + 5]
            k += 5
        dxbc_ref, a1_ref, a2_ref, r2_ref, s_ref, g_ref, r2scr = refs[k:k + 7]
        j = pl.program_id(0)

        @pl.when(j == 0)
        def _():
            g_ref[...] = jnp.zeros_like(g_ref)

        mask, mask_t, hms = _masks(rev)
        la_all = la_ref[...]
        dt_all = dt_ref[...]
        la_t = jnp.transpose(la_all)
        dt_t = jnp.transpose(dt_all)
        la_e = la_all[e:e + 1, :]
        w2 = _split2(jnp.exp(jnp.where(_mine(hoff), la_e - la_all, 0.0)) * dt_all)
        e2 = _split2(jnp.exp(la_all))
        wed2 = jnp.concatenate([w2, e2, _split2(dt_all)], axis=0)
        ela_e = jnp.exp(la_e)
        r2scr[...] = jnp.zeros_like(r2scr)
        a1acc = jnp.zeros((Q, 128), f32)
        a2acc = jnp.zeros((Q, 128), f32)
        sacc = jnp.zeros((1, 128), f32)
        for g in range(NG):
            hc0 = hoff + g * HPG
            x = xs_ref[:, g * GW:(g + 1) * GW]
            bb = bc_ref[:, g * NS:(g + 1) * NS]
            cb = bc_ref[:, NG * NS + g * NS:NG * NS + (g + 1) * NS]
            dyv = dy_ref[:, g * GW:(g + 1) * GW]
            gt = g_ref[g * NS:(g + 1) * NS, :]
            ht = hp_ref[g * NS:(g + 1) * NS, :]
            gtb = gt.astype(bf16)
            htb = ht.astype(bf16)
            xb = x.astype(bf16)
            scores = _dot_nt(cb, bb)
            scores_t = _dot_nt(bb, cb)
            bg = _dot(bb, gtb)
            yoff = _dot(cb, htb)
            sel3 = _dot(wed2, sw_ref[g])
            wend, expla, dtf = sel3[0:Q], sel3[Q:2 * Q], sel3[2 * Q:3 * Q]
            dym = jnp.concatenate([jnp.where(hms[r], dyv, 0.0).astype(bf16) for r in range(HPG)], axis=0)
            dyx_all = _dot_nt(dym, xb)
            sdts, ems = [], []
            wsum = jnp.zeros((Q, Q), f32)
            for r in range(HPG):
                hc = hc0 + r
                la_rep = jnp.broadcast_to(la_all[:, hc:hc + 1], (Q, 128))
                la_r = la_t[hc:hc + 1, :]
                dt_r = dt_t[hc:hc + 1, :]
                decay = jnp.exp(jnp.where(mask, la_rep - la_r, NEG))
                decay_t = jnp.exp(jnp.where(mask_t, la_r - la_rep, NEG))
                dyx = dyx_all[r * Q:(r + 1) * Q, :]
                fm = dyx * (scores * decay)
                r2scr[hc:hc + 1, :] = jnp.sum(fm, axis=0, keepdims=True)
                ems.append(fm * dt_r)
                wsum = wsum + dyx * decay * dt_r
                sdts.append((scores_t * decay_t).astype(bf16))
            dx = dtf * _dot(jnp.concatenate(sdts, axis=1), dym) + wend * bg
            if not has_acc:
                dx = dx + dsk_ref[:, g * GW:(g + 1) * GW] * dyv
            red3 = _dot(jnp.concatenate([(dyv * yoff * expla).astype(bf16), (x * bg).astype(bf16), (gt * ht).astype(bf16)],
                                        axis=0), ih_ref[g])
            a1acc = a1acc + _dot(_split2(jnp.concatenate(ems, axis=1)), ie_ref[g]) + red3[0:Q]
            a2acc = a2acc + red3[Q:2 * Q]
            sacc = sacc + jnp.sum(red3[2 * Q:3 * Q], axis=0, keepdims=True)
            wb = wsum.astype(bf16)
            dysb = (dyv * expla).astype(bf16)
            dc = _dot(wb, bb) + _dot_nt(dysb, htb)
            db = _dot_tn(wb, cb) + _dot_nt((x * wend).astype(bf16), gtb)
            g_ref[g * NS:(g + 1) * NS, :] = gt * _head_row(ela_e, hc0) + _dot_tn(cb, dysb)
            if has_acc:
                dx = dx + dxbc_in[:, g * GW:(g + 1) * GW]
                db = db + dxbc_in[:, B0 + g * NS:B0 + (g + 1) * NS]
                dc = dc + dxbc_in[:, C0 + g * NS:C0 + (g + 1) * NS]
            dxbc_ref[:, g * GW:(g + 1) * GW] = dx
            dxbc_ref[:, B0 + g * NS:B0 + (g + 1) * NS] = db
            dxbc_ref[:, C0 + g * NS:C0 + (g + 1) * NS] = dc
        r2c = jnp.transpose(r2scr[...])
        sc = jnp.broadcast_to(sacc, (Q, 128))
        if has_acc:
            a1acc = a1acc + a1_in[...]
            a2acc = a2acc + a2_in[...]
            r2c = r2c + r2_in[...]
            sc = sc + s_in[...]
        a1_ref[...] = a1acc
        a2_ref[...] = a2acc
        r2_ref[...] = r2c
        s_ref[...] = sc

    blk = pl.BlockSpec((Q, 128), lambda j: (cm(j), 0))
    big = pl.BlockSpec((Q, 4096), lambda j: (cm(j), 0))
    wide = pl.BlockSpec((Q, DI), lambda j: (cm(j), 0))
    in_specs = [wide, pl.BlockSpec((Q, 2 * NG * NS), lambda j: (cm(j), 0)), wide, blk, blk,
                pl.BlockSpec((None, NG * NS, HPG * HD), lambda j: (cm(j), 0, 0)), _full((1, DI)),
                _full(sel_w.shape), _full(ind_h.shape), _full(ind_e.shape)]
    args = [xs, bc, dy, dt, la, hprev, dskip_full, sel_w, ind_h, ind_e]
    aliases = {}
    if has_acc:
        in_specs += [big, blk, blk, blk, blk]
        args += list(acc)
        aliases = {10: 0, 11: 1, 12: 2, 13: 3, 14: 4}
    return pl.pallas_call(
        kern, out_shape=(S((T, 4096), f32), S((T, 128), f32), S((T, 128), f32), S((T, 128), f32), S((T, 128), f32)),
        grid=(nch,), in_specs=in_specs, out_specs=(big, blk, blk, blk, blk),
        scratch_shapes=[pltpu.VMEM((NG * NS, HPG * HD), f32), pltpu.VMEM((128, Q), f32)],
        input_output_aliases=aliases,
        compiler_params=_params("arbitrary"), name=name)(*args)


def _ynorm_fwd(ysum, xs, proj, dskip_full, nw, L):
    nlx = L // RT

    def kern(ys_ref, xs_ref, za_ref, zb_ref, dsk_ref, nw_ref, y_ref, yn_ref, ynt_ref):
        y = ys_ref[...] + dsk_ref[...] * xs_ref[...]
        y_ref[...] = y
        hg = NG // 2
        for g in range(NG):
            z_ref = za_ref if g < hg else zb_ref
            sl = y[:, g * GW:(g + 1) * GW] * _silu(z_ref[:, (g % hg) * GW:(g % hg + 1) * GW])
            r = lax.rsqrt(jnp.mean(sl * sl, axis=1, keepdims=True) + EPS)
            yn = (sl * r) * nw_ref[:, g * GW:(g + 1) * GW]
            yn_ref[:, g * GW:(g + 1) * GW] = yn.astype(bf16)
            ynt_ref[g * GW:(g + 1) * GW, :] = jnp.transpose(yn).astype(bf16)

    blk = pl.BlockSpec((RT, DI), lambda i: (i, 0))
    return pl.pallas_call(
        kern, out_shape=(S((L, DI), f32), S((L, DI), bf16), S((DI, L), bf16)), grid=(nlx,),
        in_specs=[blk, blk, pl.BlockSpec((RT, DI // 2), lambda i: (i, Z0 // (DI // 2))),
                  pl.BlockSpec((RT, DI // 2), lambda i: (i, Z0 // (DI // 2) + 1)), _full((1, DI)), _full((1, DI))],
        out_specs=(blk, blk, pl.BlockSpec((DI, RT), lambda i: (0, i))),
        compiler_params=_params("parallel"), name="ynorm_fwd")(ysum, xs, proj, proj, dskip_full, nw)


def _ynorm_bwd(dyn, y, xs, proj, dskip_full, nw, dproj):
    L = y.shape[0]
    T = proj.shape[0]
    nlx, nt = L // RT, T // RT

    hw = DI // 2

    def kern(dyn_ref, y_ref, xs_ref, z_ref, dsk_ref, nw_ref, _alias, dz_ref, dy_ref, dnw_ref, dsk_acc):
        i = pl.program_id(1)

        @pl.when(i == 0)
        def _():
            dnw_ref[...] = jnp.zeros_like(dnw_ref)
            dsk_acc[...] = jnp.zeros_like(dsk_acc)

        @pl.when(i >= nlx)
        def _():
            dz_ref[...] = jnp.zeros_like(dz_ref)
            dy_ref[...] = jnp.zeros_like(dy_ref)

        @pl.when(i < nlx)
        def _():
            y = y_ref[...]
            z = z_ref[...]
            sz = _sig(z)
            gz = z * sz
            yz = y * gz
            dynv = dyn_ref[...]
            for g in range(hw // GW):
                cs = slice(g * GW, (g + 1) * GW)
                sl = yz[:, cs]
                r = lax.rsqrt(jnp.mean(sl * sl, axis=1, keepdims=True) + EPS)
                yhat = sl * r
                dn = dynv[:, cs]
                dnw_ref[:, cs] += jnp.sum(dn * yhat, axis=0, keepdims=True)
                dyh = dn * nw_ref[:, cs]
                dyz = r * (dyh - yhat * jnp.mean(dyh * yhat, axis=1, keepdims=True))
                dyv = dyz * gz[:, cs]
                dy_ref[:, cs] = dyv
                dz_ref[:, cs] = (dyz * y[:, cs] * _dsilu(z[:, cs], sz[:, cs])).astype(bf16)
                dsk_acc[:, cs] += jnp.sum(dyv * xs_ref[:, cs], axis=0, keepdims=True)

    xblk = pl.BlockSpec((RT, hw), lambda j, i: (jnp.minimum(i, nlx - 1), j))
    row = pl.BlockSpec((1, hw), lambda j, i: (0, j))
    return pl.pallas_call(
        kern, out_shape=(S(dproj.shape, bf16), S((T, DI), f32), S((1, DI), f32), S((1, DI), f32)), grid=(2, nt),
        in_specs=[xblk, xblk, xblk, pl.BlockSpec((RT, hw), lambda j, i: (jnp.minimum(i, nlx - 1), Z0 // hw + j)), row, row,
                  pl.BlockSpec(memory_space=pl.ANY)],
        out_specs=(pl.BlockSpec((RT, hw), lambda j, i: (i, Z0 // hw + j)), pl.BlockSpec((RT, hw), lambda j, i: (i, j)), row, row),
        input_output_aliases={6: 0},
        compiler_params=_params("arbitrary", "arbitrary"), name="ynorm_bwd")(dyn, y, xs, proj, dskip_full, nw, dproj)


def _head_sums(cols):
    def kern(c_ref, o_ref):
        o_ref[...] = jnp.broadcast_to(jnp.sum(c_ref[...], axis=1, keepdims=True), (NH, 128))

    return pl.pallas_call(kern, out_shape=S((NH, 128), f32), name="head_sums")(cols)


SEG_STRIDE = 96
SEG_PAD = 16
NSEG = RT // GRID_W
CONF_ROWS = SEG_PAD + NSEG * SEG_STRIDE


SHIFT_ROWS = CONF_ROWS - 8
CONF_CW = 256


CONF_RB = 32


def _seg_zero_pads(scr):
    scr[0:SEG_PAD, :] = jnp.zeros((SEG_PAD, scr.shape[1]), f32)
    for s in range(NSEG):
        lo = SEG_PAD + s * SEG_STRIDE + GRID_W
        scr[lo:lo + SEG_STRIDE - GRID_W, :] = jnp.zeros((SEG_STRIDE - GRID_W, scr.shape[1]), f32)


def _seg_row(r0):
    return SEG_PAD + (r0 // GRID_W) * SEG_STRIDE + r0 % GRID_W


def _shift_copies(cps, scr, cs):
    full = scr[:, cs]
    for s in range(1, 8):
        cps[s - 1, :, :] = pltpu.roll(full, CONF_ROWS - s, 0)[0:SHIFT_ROWS, :]


def _tap(cps, scr, cs, o):
    rs = o % 8
    return scr[pl.ds(o, GRID_W), cs] if rs == 0 else cps[rs - 1, pl.ds(o - rs, GRID_W), :]


def _conf_fwd(proj, w32, cb, lnw, lnb, L):
    nlx = L // RT

    def kern(v_ref, g_ref, cg_ref, w_ref, cb_ref, lnw_ref, lnb_ref, u1_ref, u3_ref, u3t_ref, scr, cps, u3_scr):
        _seg_zero_pads(scr)
        for r0 in range(0, RT, CONF_RB):
            rows = slice(r0, r0 + CONF_RB)
            scr[_seg_row(r0):_seg_row(r0) + CONF_RB, :] = v_ref[rows, :] * _sig(g_ref[rows, :])
        for cc in range(D // CONF_CW):
            cs = slice(cc * CONF_CW, (cc + 1) * CONF_CW)
            _shift_copies(cps, scr, cs)
            for s in range(NSEG):
                acc = jnp.broadcast_to(cb_ref[:, cs], (GRID_W, CONF_CW))
                for k in range(CK):
                    acc = acc + w_ref[k:k + 1, cs] * _tap(cps, scr, cs, SEG_PAD + s * SEG_STRIDE + k - CK // 2)
                u1_ref[s * GRID_W:(s + 1) * GRID_W, cs] = acc
        for r0 in range(0, RT, CONF_RB):
            rows = slice(r0, r0 + CONF_RB)
            u1 = u1_ref[rows, :]
            xc = u1 - jnp.mean(u1, axis=1, keepdims=True)
            r = lax.rsqrt(jnp.mean(xc * xc, axis=1, keepdims=True) + EPS)
            u2 = (xc * r) * lnw_ref[...] + lnb_ref[...]
            u3 = _silu(u2) * _silu(cg_ref[rows, :])
            u3_ref[rows, :] = u3.astype(bf16)
            u3_scr[rows, :] = u3
        u3t_ref[...] = jnp.transpose(u3_scr[...]).astype(bf16)

    blk = pl.BlockSpec((RT, D), lambda i: (i, 0))
    return pl.pallas_call(
        kern, out_shape=(S((L, D), f32), S((L, D), bf16), S((D, L), bf16)), grid=(nlx,),
        in_specs=[pl.BlockSpec((RT, D), lambda i: (i, GV0 // D)), pl.BlockSpec((RT, D), lambda i: (i, GG0 // D)),
                  pl.BlockSpec((RT, D), lambda i: (i, CG0 // D)), _full((32, D)), _full((1, D)), _full((1, D)), _full((1, D))],
        out_specs=(blk, blk, pl.BlockSpec((D, RT), lambda i: (0, i))),
        scratch_shapes=[pltpu.VMEM((CONF_ROWS, D), f32), pltpu.VMEM((7, SHIFT_ROWS, CONF_CW), f32), pltpu.VMEM((RT, D), f32)],
        compiler_params=_params("parallel"), name="conf_fwd")(proj, proj, proj, w32, cb, lnw, lnb)


def _conf_bwd(du3, u1, proj, w32, lnw, lnb, dproj):
    L = u1.shape[0]
    T = proj.shape[0]
    nlx, nt = L // RT, T // RT

    def kern(du3_ref, u1_ref, v_ref, g_ref, cg_ref, w_ref, lnw_ref, lnb_ref, _alias,
             o_ref, dw_ref, dcb_ref, dlw_ref, dlb_ref, scr_u, scr_d, du0_scr, cps_u, cps_d):
        i = pl.program_id(0)

        @pl.when(i == 0)
        def _():
            dw_ref[...] = jnp.zeros_like(dw_ref)
            dcb_ref[...] = jnp.zeros_like(dcb_ref)
            dlw_ref[...] = jnp.zeros_like(dlw_ref)
            dlb_ref[...] = jnp.zeros_like(dlb_ref)

        @pl.when(i >= nlx)
        def _():
            o_ref[...] = jnp.zeros_like(o_ref)

        @pl.when(i < nlx)
        def _():
            _seg_zero_pads(scr_u)
            _seg_zero_pads(scr_d)
            for r0 in range(0, RT, CONF_RB):
                rows = slice(r0, r0 + CONF_RB)
                cg = cg_ref[rows, :]
                scg = _sig(cg)
                u1 = u1_ref[rows, :]
                xc = u1 - jnp.mean(u1, axis=1, keepdims=True)
                r = lax.rsqrt(jnp.mean(xc * xc, axis=1, keepdims=True) + EPS)
                xhat = xc * r
                u2 = xhat * lnw_ref[...] + lnb_ref[...]
                s2 = _sig(u2)
                du3v = du3_ref[rows, :]
                du2 = du3v * (cg * scg) * _dsilu(u2, s2)
                o_ref[rows, 2 * D:3 * D] = (du3v * (u2 * s2) * _dsilu(cg, scg)).astype(bf16)
                dlw_ref[...] += jnp.sum(du2 * xhat, axis=0, keepdims=True)
                dlb_ref[...] += jnp.sum(du2, axis=0, keepdims=True)
                dxh = du2 * lnw_ref[...]
                du1 = r * (dxh - jnp.mean(dxh, axis=1, keepdims=True) - xhat * jnp.mean(dxh * xhat, axis=1, keepdims=True))
                dcb_ref[...] += jnp.sum(du1, axis=0, keepdims=True)
                scr_u[_seg_row(r0):_seg_row(r0) + CONF_RB, :] = v_ref[rows, :] * _sig(g_ref[rows, :])
                scr_d[_seg_row(r0):_seg_row(r0) + CONF_RB, :] = du1
            for cc in range(D // CONF_CW):
                cs = slice(cc * CONF_CW, (cc + 1) * CONF_CW)
                _shift_copies(cps_u, scr_u, cs)
                _shift_copies(cps_d, scr_d, cs)
                for k in range(CK):
                    t = jnp.zeros((GRID_W, CONF_CW), f32)
                    for s in range(NSEG):
                        base = SEG_PAD + s * SEG_STRIDE
                        t = t + scr_d[pl.ds(base, GRID_W), cs] * _tap(cps_u, scr_u, cs, base + k - CK // 2)
                    dw_ref[k:k + 1, cs] += jnp.sum(t, axis=0, keepdims=True)
                for s in range(NSEG):
                    base = SEG_PAD + s * SEG_STRIDE
                    acc = jnp.zeros((GRID_W, CONF_CW), f32)
                    for k in range(CK):
                        acc = acc + w_ref[k:k + 1, cs] * _tap(cps_d, scr_d, cs, base + CK // 2 - k)
                    du0_scr[s * GRID_W:(s + 1) * GRID_W, cs] = acc
            for r0 in range(0, RT, CONF_RB):
                rows = slice(r0, r0 + CONF_RB)
                du0 = du0_scr[rows, :]
                sg = _sig(g_ref[rows, :])
                o_ref[rows, 0:D] = (du0 * sg).astype(bf16)
                o_ref[rows, D:2 * D] = (du0 * v_ref[rows, :] * sg * (1.0 - sg)).astype(bf16)

    xmap = lambda i: (jnp.minimum(i, nlx - 1), 0)
    pmap = lambda cb: (lambda i: (jnp.minimum(i, nlx - 1), cb))
    return pl.pallas_call(
        kern, out_shape=(S(dproj.shape, bf16), S((32, D), f32), S((1, D), f32), S((1, D), f32), S((1, D), f32)), grid=(nt,),
        in_specs=[pl.BlockSpec((RT, D), xmap), pl.BlockSpec((RT, D), xmap),
                  pl.BlockSpec((RT, D), pmap(GV0 // D)), pl.BlockSpec((RT, D), pmap(GG0 // D)), pl.BlockSpec((RT, D), pmap(CG0 // D)),
                  _full((32, D)), _full((1, D)), _full((1, D)), pl.BlockSpec(memory_space=pl.ANY)],
        out_specs=(pl.BlockSpec((RT, 3 * D), lambda i: (i, GV0 // (3 * D))), _full((32, D)), _full((1, D)), _full((1, D)), _full((1, D))),
        scratch_shapes=[pltpu.VMEM((CONF_ROWS, D), f32), pltpu.VMEM((CONF_ROWS, D), f32), pltpu.VMEM((RT, D), f32),
                        pltpu.VMEM((7, SHIFT_ROWS, CONF_CW), f32), pltpu.VMEM((7, SHIFT_ROWS, CONF_CW), f32)],
        input_output_aliases={8: 0},
        compiler_params=_params("arbitrary"), name="conf_bwd")(du3, u1, proj, proj, proj, w32, lnw, lnb, dproj)


def _merge_fwd(bs, bc, proj):
    L = bs.shape[0]

    def kern(bs_ref, bc_ref, g1_ref, g2_ref, o_ref, ot_ref):
        mv = _sig(g1_ref[...]) * bs_ref[...] + _sig(g2_ref[...]) * bc_ref[...]
        o_ref[...] = mv.astype(bf16)
        ot_ref[...] = jnp.transpose(mv).astype(bf16)

    blk = pl.BlockSpec((RT, D), lambda i: (i, 0))
    return pl.pallas_call(
        kern, out_shape=(S((L, D), bf16), S((D, L), bf16)), grid=(L // RT,),
        in_specs=[blk, blk, pl.BlockSpec((RT, D), lambda i: (i, G10 // D)), pl.BlockSpec((RT, D), lambda i: (i, G20 // D))],
        out_specs=(blk, pl.BlockSpec((D, RT), lambda i: (0, i))),
        compiler_params=_params("parallel"), name="merge_fwd")(bs, bc, proj, proj)


def _merge_bwd(dmerged, bs, bc, proj):
    L = bs.shape[0]
    T = proj.shape[0]
    nlx, nt = L // RT, T // RT

    def kern(dm_ref, bs_ref, bc_ref, g1_ref, g2_ref, o_ref, dbs_ref, dbc_ref):
        i = pl.program_id(0)

        @pl.when(i >= nlx)
        def _():
            o_ref[...] = jnp.zeros_like(o_ref)

        @pl.when(i < nlx)
        def _():
            dm = dm_ref[...]
            s1 = _sig(g1_ref[...])
            s2 = _sig(g2_ref[...])
            dbs_ref[...] = (dm * s1).astype(bf16)
            dbc_ref[...] = (dm * s2).astype(bf16)
            o_ref[:, 0:D] = (dm * bs_ref[...] * s1 * (1.0 - s1)).astype(bf16)
            o_ref[:, D:2 * D] = (dm * bc_ref[...] * s2 * (1.0 - s2)).astype(bf16)

    xmap = lambda i: (jnp.minimum(i, nlx - 1), 0)
    pmap = lambda cb: (lambda i: (jnp.minimum(i, nlx - 1), cb))
    xblk = pl.BlockSpec((RT, D), xmap)
    return pl.pallas_call(
        kern, out_shape=(S((T, NP), bf16), S((L, D), bf16), S((L, D), bf16)), grid=(nt,),
        in_specs=[xblk, xblk, xblk, pl.BlockSpec((RT, D), pmap(G10 // D)), pl.BlockSpec((RT, D), pmap(G20 // D))],
        out_specs=(pl.BlockSpec((RT, 2 * D), lambda i: (i, G10 // (2 * D))), xblk, xblk),
        compiler_params=_params("arbitrary"), name="merge_bwd")(dmerged, bs, bc, proj, proj)


def _final(x, out, target, mod, fw):
    L = x.shape[0]

    def kern(x_ref, o_ref, t_ref, mod_ref, fw_ref, dx1_ref, dout_ref, loss_ref, dfw_ref, dg_ref):
        i = pl.program_id(0)

        @pl.when(i == 0)
        def _():
            loss_ref[...] = jnp.zeros_like(loss_ref)
            dfw_ref[...] = jnp.zeros_like(dfw_ref)
            dg_ref[...] = jnp.zeros_like(dg_ref)

        gate = mod_ref[0:1, 2 * D:3 * D]
        ov = o_ref[...]
        x1 = x_ref[...] + gate * ov
        r = lax.rsqrt(jnp.mean(x1 * x1, axis=1, keepdims=True) + EPS)
        xn = x1 * r
        fw = fw_ref[...]
        err = xn * fw - t_ref[...]
        part = 0.5 * jnp.sum(jnp.mean(err * err, axis=1, keepdims=True), axis=0, keepdims=True)
        loss_ref[...] += jnp.broadcast_to(part, (8, 128))
        dy = err * (1.0 / D)
        dfw_ref[...] += jnp.sum(dy * xn, axis=0, keepdims=True)
        dyw = dy * fw
        dx1 = r * (dyw - xn * jnp.mean(dyw * xn, axis=1, keepdims=True))
        dx1_ref[...] = dx1
        dout_ref[...] = (gate * dx1).astype(bf16)
        dg_ref[...] += jnp.sum(dx1 * ov, axis=0, keepdims=True)

    blk = pl.BlockSpec((RT, D), lambda i: (i, 0))
    return pl.pallas_call(
        kern, out_shape=(S((L, D), f32), S((L, D), bf16), S((8, 128), f32), S((1, D), f32), S((1, D), f32)), grid=(L // RT,),
        in_specs=[blk, blk, blk, _full((8, 3 * D)), _full((1, D))],
        out_specs=(blk, blk, _full((8, 128)), _full((1, D)), _full((1, D))),
        compiler_params=_params("arbitrary"), name="final")(x, out, target, mod, fw)


def _me():
    return 4 * lax.axis_index("x") + 2 * lax.axis_index("y") + lax.axis_index("c")


def _xchg_copy(ins, outs, send_sems, recv_sems, modes, a, k, me):
    peer = lax.rem(me + k, N_DEV)
    pid = (peer // 4, lax.rem(peer // 2, 2), lax.rem(peer, 2))
    src = ins[a].at[peer] if modes[a] else ins[a]
    return pltpu.make_async_remote_copy(src_ref=src, dst_ref=outs[a].at[me], send_sem=send_sems.at[a, k - 1],
                                        recv_sem=recv_sems.at[a, k - 1], device_id=pid, device_id_type=MESH)


def _xchg_local(ins, outs, loc_sems, modes, a, me):
    return pltpu.make_async_copy(ins[a].at[me] if modes[a] else ins[a], outs[a].at[me], loc_sems.at[a])


def _xchg_start(ins, outs, send_sems, recv_sems, loc_sems, modes):
    me = _me()
    for a in range(len(modes)):
        _xchg_local(ins, outs, loc_sems, modes, a, me).start()
        for k in range(1, N_DEV):
            _xchg_copy(ins, outs, send_sems, recv_sems, modes, a, k, me).start()


def _xchg_wait(ins, outs, send_sems, recv_sems, loc_sems, modes):
    me = _me()
    for a in range(len(modes)):
        for k in range(1, N_DEV):
            frm = lax.rem(me + N_DEV - k, N_DEV)
            src = ins[a].at[frm] if modes[a] else ins[a]
            pltpu.make_async_remote_copy(src_ref=src, dst_ref=outs[a].at[frm], send_sem=send_sems.at[a, k - 1],
                                         recv_sem=recv_sems.at[a, k - 1], device_id=(0, 0, 0), device_id_type=MESH).wait_recv()
    for a in range(len(modes)):
        for k in range(1, N_DEV):
            _xchg_copy(ins, outs, send_sems, recv_sems, modes, a, k, me).wait_send()
        _xchg_local(ins, outs, loc_sems, modes, a, me).wait()


def _xchg_out_shapes(arrs, modes):
    return tuple(S((N_DEV,) + (a.shape[1:] if sc else a.shape), a.dtype) for a, sc in zip(arrs, modes))


def _xchg_sems(n):
    return [pltpu.SemaphoreType.DMA((n, N_DEV - 1)), pltpu.SemaphoreType.DMA((n, N_DEV - 1)), pltpu.SemaphoreType.DMA((n,))]


def _exchange(arrs, modes, name):
    n = len(arrs)

    def kern(*refs):
        ins, outs, sems = refs[:n], refs[n:2 * n], refs[2 * n:]
        _xchg_start(ins, outs, *sems, modes)
        _xchg_wait(ins, outs, *sems, modes)

    anyspec = pl.BlockSpec(memory_space=pl.ANY)
    return pl.pallas_call(
        kern, out_shape=_xchg_out_shapes(arrs, modes), in_specs=[anyspec] * n, out_specs=tuple([anyspec] * n),
        scratch_shapes=_xchg_sems(n), name=name)(*arrs)


def _gather2(arrs, name):
    n = len(arrs)

    def kern(*refs):
        ins, outs = refs[:n], refs[n:2 * n]
        send_sems, recv_sems, loc_sems = refs[2 * n:]
        x, y, c = lax.axis_index("x"), lax.axis_index("y"), lax.axis_index("c")
        me, sib = (x, y, c), (x, y, 1 - c)
        chips = [(1 - x, y), (x, 1 - y), (1 - x, 1 - y)]

        def slot(a, p):
            return outs[a].at[4 * p[0] + 2 * p[1] + p[2]]

        def cp(a, k, block, to, own=False):
            return pltpu.make_async_remote_copy(src_ref=ins[a] if own else slot(a, block), dst_ref=slot(a, block),
                                                send_sem=send_sems.at[a, k], recv_sem=recv_sems.at[a, k],
                                                device_id=to, device_id_type=MESH)

        started = []
        for a in range(n):
            loc = pltpu.make_async_copy(ins[a], slot(a, me), loc_sems.at[a])
            loc.start()
            started.append(cp(a, 0, me, sib, own=True))
            started += [cp(a, 1 + j, me, (*chip, c), own=True) for j, chip in enumerate(chips)]
        for s in started:
            s.start()
        for j, chip in enumerate(chips):
            for a in range(n):
                cp(a, 1 + j, (*chip, c), me).wait_recv()
                fwd = cp(a, 4 + j, (*chip, c), sib)
                fwd.start()
                started.append(fwd)
        for a in range(n):
            cp(a, 0, sib, me).wait_recv()
            for j, chip in enumerate(chips):
                cp(a, 4 + j, (*chip, 1 - c), me).wait_recv()
        for s in started:
            s.wait_send()
        for a in range(n):
            pltpu.make_async_copy(ins[a], slot(a, me), loc_sems.at[a]).wait()

    anyspec = pl.BlockSpec(memory_space=pl.ANY)
    return pl.pallas_call(
        kern, out_shape=_xchg_out_shapes(arrs, (False,) * n), in_specs=[anyspec] * n, out_specs=tuple([anyspec] * n),
        scratch_shapes=[pltpu.SemaphoreType.DMA((n, 7)), pltpu.SemaphoreType.DMA((n, 7)), pltpu.SemaphoreType.DMA((n,))],
        name=name)(*arrs)


def _adamw(parts, w, m, v, name):
    r, c = w.shape
    n_parts = parts.shape[0]
    tr = r
    for cand in (128, 64, 32, 16, 8):
        if r % cand == 0 and r > cand:
            tr = cand
            break
    c1 = 1.0 / (1.0 - ADAM_B1 ** ADAM_STEP)
    c2 = 1.0 / (1.0 - ADAM_B2 ** ADAM_STEP)

    def kern(p_ref, w_ref, m_ref, v_ref, g_ref, d_ref, m2_ref, v2_ref):
        g = p_ref[0].astype(f32)
        for i in range(1, n_parts):
            g = g + p_ref[i].astype(f32)
        g_ref[...] = g
        m2 = ADAM_B1 * m_ref[...] + (1.0 - ADAM_B1) * g
        v2 = ADAM_B2 * v_ref[...] + (1.0 - ADAM_B2) * (g * g)
        m2_ref[...] = m2
        v2_ref[...] = v2
        d_ref[...] = -ADAM_LR * ((m2 * c1) / (jnp.sqrt(v2 * c2) + ADAM_EPS) + ADAM_WD * w_ref[...])

    blk = pl.BlockSpec((tr, c), lambda i: (i, 0))
    sh = S((r, c), f32)
    return pl.pallas_call(
        kern, out_shape=(sh, sh, sh, sh), grid=(r // tr,),
        in_specs=[pl.BlockSpec((n_parts, tr, c), lambda i: (0, i, 0)), blk, blk, blk], out_specs=(blk, blk, blk, blk),
        compiler_params=_params("parallel"), name=name)(parts, w, m, v)


_SMALL = (("c_ctx", 1024), ("b_mod", 3072), ("norm_w", 1024), ("ssm_conv_b", 4096), ("dt_bias", 64), ("a_log", 64),
          ("d_skip", 32), ("ssm_norm_w", 2048), ("conf_conv_b", 1024), ("conf_ln_w", 1024), ("conf_ln_b", 1024),
          ("final_norm_w", 1024))
SMALL_TILE = 8 * 128


def _pack_small(d):
    rows = []
    for name, n in _SMALL:
        v = d[name].reshape(-1).astype(f32)
        pad = (-n) % SMALL_TILE
        if pad:
            v = jnp.concatenate([v, jnp.zeros((pad,), f32)])
        rows.append(v.reshape(-1, 128))
    return jnp.concatenate(rows, axis=0)


def _unpack_small(p, shapes):
    out, r0 = {}, 0
    for name, n in _SMALL:
        nr = 8 * ((n + SMALL_TILE - 1) // SMALL_TILE)
        out[name] = p[r0:r0 + nr].reshape(-1)[:n].reshape(shapes[name])
        r0 += nr
    return out


def _permute_w_in(w):
    return jnp.concatenate([w[:, 9280:11328], w[:, 2048:4096], w[:, 0:2048], w[:, 6208:9280], w[:, 4160:6208],
                            w[:, 4096:4160], jnp.zeros((w.shape[0], NP - DT0 - 64), w.dtype)], axis=1)


def _unpermute_w_in(wp):
    return jnp.concatenate([wp[:, PX0:PX0 + 2048], wp[:, PBC0:PBC0 + 2048], wp[:, DT0:DT0 + 64], wp[:, Z0:Z0 + 2048],
                            wp[:, GV0:GV0 + 3072], wp[:, G10:G10 + 2048]], axis=1)


def _cols_gathered(g):
    return jnp.transpose(g, (1, 0, 2)).reshape(g.shape[1], N_DEV * g.shape[2])


def _cols_to_blocks(a):
    r, c8 = a.shape
    return jnp.transpose(a.reshape(r, N_DEV, c8 // N_DEV), (1, 0, 2))


def kernel(x, c, ctx, c_ctx, w_mod, b_mod, norm_w, w_in, ssm_conv_w, ssm_conv_b, dt_bias, a_log, d_skip, ssm_norm_w, w_out_ssm, conf_conv_w, conf_conv_b, conf_ln_w, conf_ln_b, w_out_conf, w_out, final_norm_w, loss_target, m_c_ctx, m_w_mod, m_b_mod, m_norm_w, m_w_in, m_ssm_conv_w, m_ssm_conv_b, m_dt_bias, m_a_log, m_d_skip, m_ssm_norm_w, m_w_out_ssm, m_conf_conv_w, m_conf_conv_b, m_conf_ln_w, m_conf_ln_b, m_w_out_conf, m_w_out, m_final_norm_w, v_c_ctx, v_w_mod, v_b_mod, v_norm_w, v_w_in, v_ssm_conv_w, v_ssm_conv_b, v_dt_bias, v_a_log, v_d_skip, v_ssm_norm_w, v_w_out_ssm, v_conf_conv_w, v_conf_conv_b, v_conf_ln_w, v_conf_ln_b, v_w_out_conf, v_w_out, v_final_norm_w):
    L = x.shape[1]
    Lc = ctx.shape[1]
    T = L + Lc
    nlx = L // RT
    nxc = L // Q
    x2 = x.reshape(L, D)
    ctx2 = ctx.reshape(Lc, D)
    tgt = loss_target.reshape(L, D)

    gathered = _gather2([w_in[0].astype(bf16), w_mod[0].astype(bf16), ssm_conv_w[0], conf_conv_w[0]], name="gather_weights")
    wp = _permute_w_in(_cols_gathered(gathered[0]))
    wmod_bf = _cols_gathered(gathered[1])
    scw8 = jnp.concatenate([_cols_gathered(gathered[2]), jnp.zeros((8 - SK, 4096), f32)], axis=0)
    ccw32 = jnp.concatenate([_cols_gathered(gathered[3]), jnp.zeros((32 - CK, D), f32)], axis=0)

    norm_w1 = norm_w.reshape(1, D)
    scb = ssm_conv_b.reshape(1, 4096)
    bias_row = jnp.concatenate([dt_bias.reshape(1, 2 * NH), jnp.zeros((1, 128 - 2 * NH), f32)], axis=1)
    alog_row = jnp.concatenate([a_log.reshape(1, 2 * NH), jnp.zeros((1, 128 - 2 * NH), f32)], axis=1)
    dskip_full = jnp.repeat(d_skip.reshape(NH), HD).reshape(1, DI)
    snw = ssm_norm_w.reshape(1, DI)
    ccb = conf_conv_b.reshape(1, D)
    lnw = conf_ln_w.reshape(1, D)
    lnb = conf_ln_b.reshape(1, D)
    fw = final_norm_w.reshape(1, D)

    cc8 = jnp.concatenate([c.reshape(1, D), c_ctx.reshape(1, D), jnp.zeros((6, D), f32)], axis=0)
    mod, silu_rows = _mod_fwd(cc8, wmod_bf, b_mod.reshape(1, 3 * D))
    h, h_t = _prenorm(x2, ctx2, norm_w1, mod)
    proj, wos_g, woc_g, wo_g = _matmul(
        h, wp, f32, "proj_gather", tn=NP // 5,
        comm=([w_out_ssm[0].astype(bf16), w_out_conf[0].astype(bf16), w_out[0].astype(bf16)], (False,) * 3))
    wos_bf = wos_g.reshape(DI, D)
    woc_bf = woc_g.reshape(D, D)
    wo_bf = wo_g.reshape(D, D)
    xs = _ssm_conv_fwd(proj, scw8, scb, nlx, 0, f32, "ssm_conv_fwd_x")
    bcm = _ssm_conv_fwd(proj, scw8, scb, nlx, 1, bf16, "ssm_conv_fwd_bc")
    dt, la = _dt_prep(proj, bias_row, alog_row)
    consts_f, consts_b = _scan_consts(False), _scan_consts(True)
    yf, hp_f = _ssd_fwd(xs, bcm, dt, la, consts_f, False, nxc, "ssd_fwd_f")
    ysum, hp_b = _ssd_fwd(xs, bcm, dt, la, consts_b, True, nxc, "ssd_fwd_b", y_acc=yf)
    y, yn, yn_t = _ynorm_fwd(ysum, xs, proj, dskip_full, snw, L)
    bs = _matmul(yn, wos_bf, f32, "branch_ssm", tm=1024, tk=2048)
    u1, u3, u3_t = _conf_fwd(proj, ccw32, ccb, lnw, lnb, L)
    bc = _matmul(u3, woc_bf, f32, "branch_conf", tm=2048)
    merged, merged_t = _merge_fwd(bs, bc, proj)
    out = _matmul(merged, wo_bf, f32, "out_proj", tm=2048)
    dx1, dout, loss_acc, dfw, dgate = _final(x2, out, tgt, mod, fw)

    dmerged = _matmul(dout, wo_bf, f32, "d_merged", tb=True, tm=2048)
    g_wo = _matmul(merged_t, dout, bf16, "g_w_out", tm=1024, tk=2048)
    dproj, dbs, dbc = _merge_bwd(dmerged, bs, bc, proj)
    dyn = _matmul(dbs, wos_bf, f32, "d_yn", tb=True, tm=1024, tn=2048)
    g_wos = _matmul(yn_t, dbs, bf16, "g_w_out_ssm", tm=1024, tk=2048)
    du3 = _matmul(dbc, woc_bf, f32, "d_u3", tb=True, tm=2048)
    g_woc = _matmul(u3_t, dbc, bf16, "g_w_out_conf", tm=1024, tk=2048)
    dproj, g_ccw, g_ccb, g_lnw, g_lnb = _conf_bwd(du3, u1, proj, ccw32, lnw, lnb, dproj)
    dproj, dy, g_snw, dsk_cols = _ynorm_bwd(dyn, y, xs, proj, dskip_full, snw, dproj)
    acc_f = _ssd_bwd(xs, bcm, dy, dt, la, hp_f, dskip_full, consts_f, False, nxc, "ssd_bwd_f")
    dxbc, a1, a2, r2, sv = _ssd_bwd(xs, bcm, dy, dt, la, hp_b, dskip_full, consts_b, True, nxc, "ssd_bwd_b", acc=acc_f)
    dproj, g_dtb, g_alog = _dt_bwd(a1, a2, r2, sv, dt, la, proj, bias_row, alog_row, dproj)
    dpre, g_scw, g_scb = _ssm_conv_dpre(dxbc, proj, scw8, scb, nlx)
    dproj = _ssm_conv_t(dpre, scw8, dproj, nlx)
    g_wp, *parts_b = _matmul(
        h_t, dproj, bf16, "g_w_in_scatter", tm=1024, tn=NP // 5,
        comm=([g_wos.reshape(N_DEV, DI // N_DEV, D), g_woc.reshape(N_DEV, D // N_DEV, D), g_wo.reshape(N_DEV, D // N_DEV, D),
               _cols_to_blocks(g_scw[:SK]), _cols_to_blocks(g_ccw[:CK])], (True,) * 5))
    dh, parts_a = _matmul(dproj, wp, f32, "d_h_scatter", tb=True, tk=NP // 5,
                          comm=([_cols_to_blocks(_unpermute_w_in(g_wp))], (True,)))
    parts = [parts_a] + parts_b
    gx, g_nw, macc = _prenorm_bwd(x2, ctx2, dh, dx1, norm_w1, mod)
    dmod_x = jnp.concatenate([macc[0:1], macc[1:2], dgate], axis=1)
    dmod_c = jnp.concatenate([macc[2:3], macc[3:4], jnp.zeros((1, D), f32)], axis=1)
    dmod8 = jnp.concatenate([dmod_x, dmod_c, jnp.zeros((6, 3 * D), f32)], axis=0)
    ct = jnp.concatenate([c.reshape(D, 1), c_ctx.reshape(D, 1), jnp.zeros((D, 126), f32)], axis=1)
    g_bmod, g_cctx = _mod_bwd(ct, dmod8, wmod_bf)
    g_dskip = _head_sums(dsk_cols.reshape(NH, HD))[:, 0]

    small_g = _pack_small({
        "c_ctx": g_cctx[:, 0], "b_mod": g_bmod, "norm_w": g_nw, "ssm_conv_b": g_scb, "dt_bias": g_dtb[0, :2 * NH],
        "a_log": g_alog[0, :2 * NH], "d_skip": g_dskip, "ssm_norm_w": g_snw, "conf_conv_b": g_ccb, "conf_ln_w": g_lnw,
        "conf_ln_b": g_lnb, "final_norm_w": dfw})
    fac = jnp.concatenate([silu_rows[0:1].reshape(D // 128, 128), dmod_x.reshape(3 * D // 128, 128),
                           dmod_c.reshape(3 * D // 128, 128)], axis=0)
    small_parts, fac_all = _exchange([small_g, fac], (False, False), name="exchange_tail")
    nr = D // 128
    sct = jnp.concatenate([fac_all[:, 0:nr].reshape(N_DEV, D).T, silu_rows[1:2].T, jnp.zeros((D, 128 - N_DEV - 1), f32)], axis=1)
    my_cols = (4 * lax.axis_index("x") + 2 * lax.axis_index("y") + lax.axis_index("c")) * (3 * D // N_DEV)
    dmx_all = lax.dynamic_slice(fac_all[:, nr:4 * nr].reshape(N_DEV, 3 * D), (0, my_cols), (N_DEV, 3 * D // N_DEV))
    dmc_all = lax.dynamic_slice(fac_all[:, 4 * nr:7 * nr].reshape(N_DEV, 3 * D), (0, my_cols), (N_DEV, 3 * D // N_DEV))
    g_wmod = _wmod_grad(sct, dmx_all, dmc_all)
    parts = [parts[0], g_wmod[None]] + parts[1:]

    given = dict(c_ctx=c_ctx, w_mod=w_mod, b_mod=b_mod, norm_w=norm_w, w_in=w_in, ssm_conv_w=ssm_conv_w, ssm_conv_b=ssm_conv_b,
                 dt_bias=dt_bias, a_log=a_log, d_skip=d_skip, ssm_norm_w=ssm_norm_w, w_out_ssm=w_out_ssm, conf_conv_w=conf_conv_w,
                 conf_conv_b=conf_conv_b, conf_ln_w=conf_ln_w, conf_ln_b=conf_ln_b, w_out_conf=w_out_conf, w_out=w_out,
                 final_norm_w=final_norm_w)
    ms = dict(c_ctx=m_c_ctx, w_mod=m_w_mod, b_mod=m_b_mod, norm_w=m_norm_w, w_in=m_w_in, ssm_conv_w=m_ssm_conv_w,
              ssm_conv_b=m_ssm_conv_b, dt_bias=m_dt_bias, a_log=m_a_log, d_skip=m_d_skip, ssm_norm_w=m_ssm_norm_w,
              w_out_ssm=m_w_out_ssm, conf_conv_w=m_conf_conv_w, conf_conv_b=m_conf_conv_b, conf_ln_w=m_conf_ln_w,
              conf_ln_b=m_conf_ln_b, w_out_conf=m_w_out_conf, w_out=m_w_out, final_norm_w=m_final_norm_w)
    vs = dict(c_ctx=v_c_ctx, w_mod=v_w_mod, b_mod=v_b_mod, norm_w=v_norm_w, w_in=v_w_in, ssm_conv_w=v_ssm_conv_w,
              ssm_conv_b=v_ssm_conv_b, dt_bias=v_dt_bias, a_log=v_a_log, d_skip=v_d_skip, ssm_norm_w=v_ssm_norm_w,
              w_out_ssm=v_w_out_ssm, conf_conv_w=v_conf_conv_w, conf_conv_b=v_conf_conv_b, conf_ln_w=v_conf_ln_w,
              conf_ln_b=v_conf_ln_b, w_out_conf=v_w_out_conf, w_out=v_w_out, final_norm_w=v_final_norm_w)
    grads, deltas, new_m, new_v = {}, {}, {}, {}
    sharded = ("w_in", "w_mod", "w_out_ssm", "w_out_conf", "w_out", "ssm_conv_w", "conf_conv_w")
    for i, nm in enumerate(sharded):
        shp = given[nm].shape
        w2 = given[nm].reshape(shp[1], shp[2])
        res = _adamw(parts[i], w2, ms[nm].reshape(w2.shape), vs[nm].reshape(w2.shape), "adamw_" + nm)
        grads[nm], deltas[nm], new_m[nm], new_v[nm] = [r.reshape(shp) for r in res]
    shapes = {nm: given[nm].shape for nm, _ in _SMALL}
    res = _adamw(small_parts, _pack_small(given), _pack_small(ms), _pack_small(vs), "adamw_small")
    for dst, packed in zip((grads, deltas, new_m, new_v), res):
        dst.update(_unpack_small(packed, shapes))

    loss = lax.psum(loss_acc[0, 0], ("x", "y", "c"))
    order = ("c_ctx", "w_mod", "b_mod", "norm_w", "w_in", "ssm_conv_w", "ssm_conv_b", "dt_bias", "a_log", "d_skip", "ssm_norm_w",
             "w_out_ssm", "conf_conv_w", "conf_conv_b", "conf_ln_w", "conf_ln_b", "w_out_conf", "w_out", "final_norm_w")
    return (loss, gx.reshape(1, L, D), *[grads[n] for n in order], *[deltas[n] for n in order],
            *[new_m[n] for n in order], *[new_v[n] for n in order])
```

```python
import jax
import jax.numpy as jnp
from jax import lax
from jax.experimental import pallas as pl
from jax.experimental.pallas import tpu as pltpu

f32 = jnp.float32
bf16 = jnp.bfloat16

D = 1024
DI = 2048
NG = 8
HPG = 4
HD = 64
GW = HPG * HD
NS = 128
NH = 32
Q = 128
GRID_W = 64
CK = 31
SK = 4
EPS = 1e-6
RT = 256
N_DEV = 8
IN_COLS = 11328
G10, G20, PBC0, PX0, GV0, GG0, CG0, Z0, DT0, NP = 0, 1024, 2048, 4096, 6144, 7168, 8192, 9216, 11264, 11520
CONV_CT = 2048
B0, C0 = 2048, 3072
VMEM_LIMIT = 50 * 1024 * 1024
NEG = -1e30

ADAM_LR, ADAM_B1, ADAM_B2, ADAM_EPS, ADAM_WD, ADAM_STEP = 0.001, 0.9, 0.999, 1e-08, 0.01, 10

MESH = pl.DeviceIdType.MESH
S = jax.ShapeDtypeStruct


def _params(*sem):
    return pltpu.CompilerParams(dimension_semantics=tuple(sem) if sem else None, vmem_limit_bytes=VMEM_LIMIT)


def _sig(x):
    return 1.0 / (1.0 + jnp.exp(-x))


def _silu(x):
    return x * _sig(x)


def _dsilu(x, s):
    return s * (1.0 + x * (1.0 - s))


def _dot(a, b):
    return jnp.dot(a, b, preferred_element_type=f32)


def _dot_nt(a, b):
    return lax.dot_general(a, b, (((1,), (1,)), ((), ())), preferred_element_type=f32)


def _dot_tn(a, b):
    return lax.dot_general(a, b, (((0,), (0,)), ((), ())), preferred_element_type=f32)


def _dot3(t_bf, v):
    v1 = v.astype(bf16)
    r1 = v - v1.astype(f32)
    v2 = r1.astype(bf16)
    v3 = (r1 - v2.astype(f32)).astype(bf16)
    return _dot(t_bf, v1) + _dot(t_bf, v2) + _dot(t_bf, v3)


def _pick(n, prefs):
    for p in prefs:
        if n % p == 0:
            return p
    return n


def _full(shape):
    nd = len(shape)
    return pl.BlockSpec(shape, lambda *_: (0,) * nd)


def _matmul(a, b, out_dtype, name, tm=None, tn=None, tk=None, tb=False, comm=None):
    m, k = a.shape
    n = b.shape[0] if tb else b.shape[1]
    tm = tm if tm and m % tm == 0 else _pick(m, (768, 512, 256, 128))
    tn = tn if tn and n % tn == 0 else _pick(n, (1024, 512, 256, 128))
    tk = tk if tk and k % tk == 0 else _pick(k, (1024, 768, 512, 256, 128))
    nk = k // tk
    gi, gj = m // tm, n // tn
    carrs, modes = comm if comm else ((), ())
    nc = len(carrs)

    def kern(*refs):
        a_ref, b_ref = refs[:2]
        cins = refs[2:2 + nc]
        o_ref = refs[2 + nc]
        couts = refs[3 + nc:3 + 2 * nc]
        acc_ref = refs[3 + 2 * nc]
        sems = refs[4 + 2 * nc:]
        i, j, kk = pl.program_id(0), pl.program_id(1), pl.program_id(2)
        if nc:
            @pl.when(jnp.logical_and(jnp.logical_and(i == 0, j == 0), kk == 0))
            def _():
                _xchg_start(cins, couts, *sems, modes)

        part = _dot_nt(a_ref[...], b_ref[...]) if tb else _dot(a_ref[...], b_ref[...])
        if nk == 1:
            o_ref[...] = part.astype(o_ref.dtype)
        else:
            @pl.when(kk == 0)
            def _():
                acc_ref[...] = part

            @pl.when(kk > 0)
            def _():
                acc_ref[...] += part

            @pl.when(kk == nk - 1)
            def _():
                o_ref[...] = acc_ref[...].astype(o_ref.dtype)

        if nc:
            @pl.when(jnp.logical_and(jnp.logical_and(i == gi - 1, j == gj - 1), kk == nk - 1))
            def _():
                _xchg_wait(cins, couts, *sems, modes)

    anyspec = pl.BlockSpec(memory_space=pl.ANY)
    bspec = pl.BlockSpec((tn, tk), lambda i, j, kk: (j, kk)) if tb else pl.BlockSpec((tk, tn), lambda i, j, kk: (kk, j))
    out_shape = (S((m, n), out_dtype),) + _xchg_out_shapes(carrs, modes)
    res = pl.pallas_call(
        kern, out_shape=out_shape, grid=(gi, gj, nk),
        in_specs=[pl.BlockSpec((tm, tk), lambda i, j, kk: (i, kk)), bspec] + [anyspec] * nc,
        out_specs=(pl.BlockSpec((tm, tn), lambda i, j, kk: (i, j)),) + (anyspec,) * nc,
        scratch_shapes=[pltpu.VMEM((tm, tn), f32)] + (_xchg_sems(nc) if nc else []),
        compiler_params=_params(*((("arbitrary",) * 3) if nc else ("parallel", "parallel", "arbitrary"))), name=name)(a, b, *carrs)
    return res if nc else res[0]


def _mod_fwd(cc8, w_mod_bf, b_mod):
    def kern(c_ref, w_ref, b_ref, o_ref, s_ref):
        s = _silu(c_ref[...])
        s_ref[...] = s
        o_ref[...] = _dot(s.astype(bf16), w_ref[...]) + b_ref[...]

    return pl.pallas_call(kern, out_shape=(S((8, 3 * D), f32), S((8, D), f32)), compiler_params=_params(),
                          name="mod_fwd")(cc8, w_mod_bf, b_mod)


def _mod_bwd(ct, dmod8, w_mod_bf):
    tc = 512
    nj = 3 * D // tc

    def kern(ct_ref, dm_ref, w_ref, db_ref, dc_ref):
        j = pl.program_id(0)
        cx = ct_ref[:, 1:2]
        sx = _sig(cx)
        dmc = dm_ref[1:2, :]
        db_ref[...] = dm_ref[0:1, :] + dmc
        t = jnp.sum(w_ref[...].astype(f32) * dmc.astype(bf16).astype(f32), axis=1, keepdims=True) * _dsilu(cx, sx)

        @pl.when(j == 0)
        def _():
            dc_ref[...] = jnp.zeros_like(dc_ref)

        dc_ref[...] += jnp.broadcast_to(t, (D, 128))

    return pl.pallas_call(
        kern, out_shape=(S((1, 3 * D), f32), S((D, 128), f32)), grid=(nj,),
        in_specs=[_full((D, 128)), pl.BlockSpec((8, tc), lambda j: (0, j)), pl.BlockSpec((D, tc), lambda j: (0, j))],
        out_specs=(pl.BlockSpec((1, tc), lambda j: (0, j)), _full((D, 128))),
        compiler_params=_params("arbitrary"), name="mod_bwd")(ct, dmod8, w_mod_bf)


def _wmod_grad(sct, dmx, dmc):
    cols = dmx.shape[1]

    def kern(s_ref, dmx_ref, dmc_ref, g_ref):
        dmc_sum = dmc_ref[0:1, :]
        for d in range(1, N_DEV):
            dmc_sum = dmc_sum + dmc_ref[d:d + 1, :]
        g = s_ref[:, N_DEV:N_DEV + 1] * dmc_sum
        for d in range(N_DEV):
            g = g + s_ref[:, d:d + 1] * dmx_ref[d:d + 1, :]
        g_ref[...] = g

    return pl.pallas_call(kern, out_shape=S((D, cols), f32), compiler_params=_params(), name="wmod_grad")(sct, dmx, dmc)


def _prenorm(x, ctx, norm_w, mod):
    L, Lc = x.shape[0], ctx.shape[0]
    nlx, nt = L // RT, (L + Lc) // RT

    def kern(x_ref, c_ref, nw_ref, mod_ref, h_ref, ht_ref):
        i = pl.program_id(0)
        is_c = i >= nlx
        xv = jnp.where(is_c, c_ref[...], x_ref[...])
        shift = jnp.where(is_c, mod_ref[1:2, 0:D], mod_ref[0:1, 0:D])
        scale = jnp.where(is_c, mod_ref[1:2, D:2 * D], mod_ref[0:1, D:2 * D])
        r = lax.rsqrt(jnp.mean(xv * xv, axis=1, keepdims=True) + EPS)
        hv = (xv * r) * nw_ref[...] * (1.0 + scale) + shift
        h_ref[...] = hv.astype(bf16)
        ht_ref[...] = jnp.transpose(hv).astype(bf16)

    return pl.pallas_call(
        kern, out_shape=(S((L + Lc, D), bf16), S((D, L + Lc), bf16)), grid=(nt,),
        in_specs=[pl.BlockSpec((RT, D), lambda i: (jnp.minimum(i, nlx - 1), 0)),
                  pl.BlockSpec((RT, D), lambda i: (jnp.maximum(i - nlx, 0), 0)),
                  _full((1, D)), _full((8, 3 * D))],
        out_specs=(pl.BlockSpec((RT, D), lambda i: (i, 0)), pl.BlockSpec((D, RT), lambda i: (0, i))),
        compiler_params=_params("parallel"), name="prenorm")(x, ctx, norm_w, mod)


def _prenorm_bwd(x, ctx, dh, dx1, norm_w, mod):
    L, Lc = x.shape[0], ctx.shape[0]
    nlx, nt = L // RT, (L + Lc) // RT

    def kern(x_ref, c_ref, dh_ref, dx1_ref, nw_ref, mod_ref, gx_ref, dnw_ref, acc_ref):
        i = pl.program_id(0)
        is_c = i >= nlx

        @pl.when(i == 0)
        def _():
            dnw_ref[...] = jnp.zeros_like(dnw_ref)
            acc_ref[...] = jnp.zeros_like(acc_ref)

        xv = jnp.where(is_c, c_ref[...], x_ref[...])
        scale = jnp.where(is_c, mod_ref[1:2, D:2 * D], mod_ref[0:1, D:2 * D])
        nw = nw_ref[...]
        r = lax.rsqrt(jnp.mean(xv * xv, axis=1, keepdims=True) + EPS)
        xn = xv * r
        dh = dh_ref[...]
        dsh = jnp.sum(dh, axis=0, keepdims=True)
        dsc = jnp.sum(dh * (xn * nw), axis=0, keepdims=True)
        dxnw = dh * (1.0 + scale)
        dnw_ref[...] += jnp.sum(dxnw * xn, axis=0, keepdims=True)
        dxn = dxnw * nw
        dx = r * (dxn - xn * jnp.mean(dxn * xn, axis=1, keepdims=True))

        @pl.when(jnp.logical_not(is_c))
        def _():
            gx_ref[...] = dx1_ref[...] + dx
            acc_ref[0:1, :] += dsh
            acc_ref[1:2, :] += dsc

        @pl.when(is_c)
        def _():
            acc_ref[2:3, :] += dsh
            acc_ref[3:4, :] += dsc

    xmap = lambda i: (jnp.minimum(i, nlx - 1), 0)
    return pl.pallas_call(
        kern, out_shape=(S((L, D), f32), S((1, D), f32), S((8, D), f32)), grid=(nt,),
        in_specs=[pl.BlockSpec((RT, D), xmap), pl.BlockSpec((RT, D), lambda i: (jnp.maximum(i - nlx, 0), 0)),
                  pl.BlockSpec((RT, D), lambda i: (i, 0)), pl.BlockSpec((RT, D), xmap), _full((1, D)), _full((8, 3 * D))],
        out_specs=(pl.BlockSpec((RT, D), xmap), _full((1, D)), _full((8, D))),
        compiler_params=_params("arbitrary"), name="prenorm_bwd")(x, ctx, dh, dx1, norm_w, mod)


def _xbc_col(j):
    return jnp.where(j == 0, PX0 // CONV_CT, PBC0 // CONV_CT)


def _halo_specs(nt_rows, ct, col=lambda j: j):
    cur = pl.BlockSpec((RT, ct), lambda i, j: (i, col(j)))
    prev = pl.BlockSpec((8, ct), lambda i, j: (jnp.maximum(i * (RT // 8) - 1, 0), col(j)))
    nxt = pl.BlockSpec((8, ct), lambda i, j: (jnp.minimum((i + 1) * (RT // 8), nt_rows // 8 - 1), col(j)))
    return cur, prev, nxt


def _fill_halo(scr, cur_ref, prev_ref, next_ref, i, nlx, nt):
    prev_ok = jnp.logical_and(i != 0, i != nlx)
    next_ok = jnp.logical_and(i != nlx - 1, i != nt - 1)
    scr[0:8, :] = jnp.where(prev_ok, prev_ref[...], 0.0)
    scr[8:8 + RT, :] = cur_ref[...]
    scr[8 + RT:16 + RT, :] = jnp.where(next_ok, next_ref[...], 0.0)


CONV_RB = 32


def _conv_blocks(ct):
    return [(slice(cb * 128, (cb + 1) * 128), r0) for cb in range(ct // 128) for r0 in range(0, RT, CONV_RB)]


def _taps(scr, cs, r0, shifts):
    blk = scr[r0:r0 + CONV_RB + 16, cs]
    n = CONV_RB + 16
    return [(blk if d == 0 else pltpu.roll(blk, (-d) % n, 0))[8:8 + CONV_RB, :] for d in shifts]


def _ssm_conv_fwd(proj, w8, b, nlx, half, out_dtype, name):
    T = proj.shape[0]
    nt = T // RT
    ct = CONV_CT
    cur, prev, nxt = _halo_specs(T, ct, lambda j: _xbc_col(j + half))

    def kern(cur_ref, prev_ref, next_ref, w_ref, b_ref, o_ref, scr):
        i = pl.program_id(0)
        _fill_halo(scr, cur_ref, prev_ref, next_ref, i, nlx, nt)
        for cs, r0 in _conv_blocks(ct):
            taps = _taps(scr, cs, r0, [k - 2 for k in range(SK)])
            acc = jnp.broadcast_to(b_ref[:, cs], (CONV_RB, 128))
            for k in range(SK):
                acc = acc + w_ref[k:k + 1, cs] * taps[k]
            o_ref[r0:r0 + CONV_RB, cs] = _silu(acc).astype(out_dtype)

    return pl.pallas_call(
        kern, out_shape=S((T, ct), out_dtype), grid=(nt, 1),
        in_specs=[cur, prev, nxt, pl.BlockSpec((8, ct), lambda i, j: (0, j + half)),
                  pl.BlockSpec((1, ct), lambda i, j: (0, j + half))],
        out_specs=pl.BlockSpec((RT, ct), lambda i, j: (i, j)),
        scratch_shapes=[pltpu.VMEM((RT + 16, ct), f32)],
        compiler_params=_params("parallel", "parallel"), name=name)(proj, proj, proj, w8, b)


def _ssm_conv_dpre(dxbc, proj, w8, b, nlx):
    T = proj.shape[0]
    nt = T // RT
    ct = CONV_CT
    cur = pl.BlockSpec((RT, ct), lambda j, i: (i, j))
    pcur = pl.BlockSpec((RT, ct), lambda j, i: (i, _xbc_col(j)))
    prev = pl.BlockSpec((8, ct), lambda j, i: (jnp.maximum(i * (RT // 8) - 1, 0), _xbc_col(j)))
    nxt = pl.BlockSpec((8, ct), lambda j, i: (jnp.minimum((i + 1) * (RT // 8), T // 8 - 1), _xbc_col(j)))

    def kern(d_ref, cur_ref, prev_ref, next_ref, w_ref, b_ref, dpre_ref, dw_ref, db_ref, scr):
        i = pl.program_id(1)
        _fill_halo(scr, cur_ref, prev_ref, next_ref, i, nlx, nt)

        @pl.when(i == 0)
        def _():
            dw_ref[...] = jnp.zeros_like(dw_ref)
            db_ref[...] = jnp.zeros_like(db_ref)

        for cb in range(ct // 128):
            cs = slice(cb * 128, (cb + 1) * 128)
            db_acc = jnp.zeros((CONV_RB, 128), f32)
            dw_acc = [jnp.zeros((CONV_RB, 128), f32) for _ in range(SK)]
            for r0 in range(0, RT, CONV_RB):
                taps = _taps(scr, cs, r0, [k - 2 for k in range(SK)])
                pre = jnp.broadcast_to(b_ref[:, cs], (CONV_RB, 128))
                for k in range(SK):
                    pre = pre + w_ref[k:k + 1, cs] * taps[k]
                dpre = d_ref[r0:r0 + CONV_RB, cs] * _dsilu(pre, _sig(pre))
                dpre_ref[r0:r0 + CONV_RB, cs] = dpre
                db_acc = db_acc + dpre
                dw_acc = [dw_acc[k] + dpre * taps[k] for k in range(SK)]
            db_ref[:, cs] += jnp.sum(db_acc, axis=0, keepdims=True)
            for k in range(SK):
                dw_ref[k:k + 1, cs] += jnp.sum(dw_acc[k], axis=0, keepdims=True)

    return pl.pallas_call(
        kern, out_shape=(S((T, 4096), f32), S((8, 4096), f32), S((1, 4096), f32)), grid=(4096 // ct, nt),
        in_specs=[cur, pcur, prev, nxt, pl.BlockSpec((8, ct), lambda j, i: (0, j)), pl.BlockSpec((1, ct), lambda j, i: (0, j))],
        out_specs=(cur, pl.BlockSpec((8, ct), lambda j, i: (0, j)), pl.BlockSpec((1, ct), lambda j, i: (0, j))),
        scratch_shapes=[pltpu.VMEM((RT + 16, ct), f32)],
        compiler_params=_params("parallel", "arbitrary"), name="ssm_conv_dpre")(dxbc, proj, proj, proj, w8, b)


def _ssm_conv_t(dpre, w8, dproj, nlx):
    T = dpre.shape[0]
    nt = T // RT
    ct = CONV_CT
    cur, prev, nxt = _halo_specs(T, ct)

    def kern(cur_ref, prev_ref, next_ref, w_ref, _alias, o_ref, scr):
        i = pl.program_id(0)
        _fill_halo(scr, cur_ref, prev_ref, next_ref, i, nlx, nt)
        for cs, r0 in _conv_blocks(ct):
            taps = _taps(scr, cs, r0, [2 - k for k in range(SK)])
            acc = jnp.zeros((CONV_RB, 128), f32)
            for k in range(SK):
                acc = acc + w_ref[k:k + 1, cs] * taps[k]
            o_ref[r0:r0 + CONV_RB, cs] = acc.astype(bf16)

    return pl.pallas_call(
        kern, out_shape=S(dproj.shape, bf16), grid=(nt, 4096 // ct),
        in_specs=[cur, prev, nxt, pl.BlockSpec((8, ct), lambda i, j: (0, j)), pl.BlockSpec(memory_space=pl.ANY)],
        out_specs=pl.BlockSpec((RT, ct), lambda i, j: (i, _xbc_col(j))),
        scratch_shapes=[pltpu.VMEM((RT + 16, ct), f32)], input_output_aliases={4: 0},
        compiler_params=_params("parallel", "parallel"), name="ssm_conv_t")(dpre, dpre, dpre, w8, dproj)


def _tri():
    li = lax.broadcasted_iota(jnp.int32, (Q, Q), 0)
    si = lax.broadcasted_iota(jnp.int32, (Q, Q), 1)
    return (si <= li).astype(bf16), (si >= li).astype(bf16)


def _dt_prep(proj, bias_row, alog_row):
    T = proj.shape[0]
    nch = T // Q

    def kern(raw_ref, b_ref, al_ref, dt_ref, la_ref):
        lane = lax.broadcasted_iota(jnp.int32, (Q, 128), 1)
        v = raw_ref[...] + b_ref[...]
        dt = jnp.maximum(v, 0.0) + jnp.log1p(jnp.exp(-jnp.abs(v)))
        a = jnp.where(lane[0:1, :] < 2 * NH, -jnp.exp(al_ref[...]), 0.0)
        da = dt * a
        tri, trit = _tri()
        dt_ref[...] = dt
        la_ref[...] = jnp.where(lane < NH, _dot3(tri, da), _dot3(trit, da))

    return pl.pallas_call(
        kern, out_shape=(S((T, 128), f32), S((T, 128), f32)), grid=(nch,),
        in_specs=[pl.BlockSpec((Q, 128), lambda c: (c, DT0 // 128)), _full((1, 128)), _full((1, 128))],
        out_specs=(pl.BlockSpec((Q, 128), lambda c: (c, 0)), pl.BlockSpec((Q, 128), lambda c: (c, 0))),
        compiler_params=_params("parallel"), name="dt_prep")(proj, bias_row, alog_row)


def _dt_bwd(a1, a2, r2, sv, dt, la, proj, bias_row, alog_row, dproj):
    T = proj.shape[0]
    nch = T // Q
    blk = pl.BlockSpec((Q, 128), lambda c: (c, 0))

    def kern(a1_ref, a2_ref, r2_ref, s_ref, dt_ref, la_ref, raw_ref, b_ref, al_ref, _alias, o_ref, db_ref, dal_ref):
        c = pl.program_id(0)

        @pl.when(c == 0)
        def _():
            db_ref[...] = jnp.zeros_like(db_ref)
            dal_ref[...] = jnp.zeros_like(dal_ref)

        lane = lax.broadcasted_iota(jnp.int32, (Q, 128), 1)
        row = lax.broadcasted_iota(jnp.int32, (Q, 128), 0)
        fwd = lane < NH
        dt = dt_ref[...]
        la = la_ref[...]
        a2v = a2_ref[...]
        r2v = r2_ref[...]
        a = jnp.where(lane[0:1, :] < 2 * NH, -jnp.exp(al_ref[...]), 0.0)
        la_e = jnp.where(fwd[0:1, :], la[Q - 1:Q, :], la[0:1, :])
        is_end = row == jnp.where(fwd, Q - 1, 0)
        e_end = jnp.exp(la_e - la)
        wend = e_end * dt
        extra = s_ref[0:1, :] * jnp.exp(la_e) + jnp.sum(wend * a2v, axis=0, keepdims=True)
        dla = a1_ref[...] - dt * r2v - wend * a2v + jnp.where(is_end, extra, 0.0)
        tri, trit = _tri()
        rcs = jnp.where(fwd, _dot3(trit, dla), _dot3(tri, dla))
        ddt = r2v + e_end * a2v + a * rcs
        dal_ref[...] += a * jnp.sum(dt * rcs, axis=0, keepdims=True)
        draw = jnp.where(lane < 2 * NH, ddt * _sig(raw_ref[...] + b_ref[...]), 0.0)
        db_ref[...] += jnp.sum(draw, axis=0, keepdims=True)
        o_ref[...] = jnp.zeros_like(o_ref)
        o_ref[:, 0:128] = draw.astype(bf16)

    return pl.pallas_call(
        kern, out_shape=(S(dproj.shape, bf16), S((1, 128), f32), S((1, 128), f32)), grid=(nch,),
        in_specs=[blk, blk, blk, blk, blk, blk, pl.BlockSpec((Q, 128), lambda c: (c, DT0 // 128)),
                  _full((1, 128)), _full((1, 128)), pl.BlockSpec(memory_space=pl.ANY)],
        out_specs=(pl.BlockSpec((Q, NP - DT0), lambda c: (c, DT0 // (NP - DT0))), _full((1, 128)), _full((1, 128))),
        input_output_aliases={9: 0},
        compiler_params=_params("arbitrary"), name="dt_bwd")(a1, a2, r2, sv, dt, la, proj, bias_row, alog_row, dproj)


def _split2(v):
    hi = v.astype(bf16)
    lo = (v - hi.astype(f32)).astype(bf16)
    return jnp.concatenate([hi, lo], axis=1)


def _scan_consts(rev):
    hoff = NH if rev else 0
    g = jnp.arange(NG, dtype=jnp.int32)[:, None, None]

    def rc(nr, ncol):
        return jnp.arange(nr, dtype=jnp.int32)[None, :, None], jnp.arange(ncol, dtype=jnp.int32)[None, None, :]

    r, c = rc(2 * 128, HPG * HD)
    sel_w = (lax.rem(r, 128) == hoff + HPG * g + c // HD).astype(bf16)
    r, c = rc(HPG * HD, 128)
    ind_h = (c == hoff + HPG * g + r // HD).astype(bf16)
    r, c = rc(2 * HPG * Q, 128)
    ind_e = (c == hoff + HPG * g + lax.rem(r, HPG * Q) // Q).astype(bf16)
    return sel_w, ind_h, ind_e


def _masks(rev):
    li = lax.broadcasted_iota(jnp.int32, (Q, Q), 0)
    si = lax.broadcasted_iota(jnp.int32, (Q, Q), 1)
    mask = (li <= si) if rev else (li >= si)
    mask_t = (li >= si) if rev else (li <= si)
    lane = lax.broadcasted_iota(jnp.int32, (Q, HPG * HD), 1)
    hms = [jnp.logical_and(lane >= r * HD, lane < (r + 1) * HD) for r in range(HPG)]
    return mask, mask_t, hms


def _mine(hoff):
    lane = lax.broadcasted_iota(jnp.int32, (Q, 128), 1)
    return jnp.logical_and(lane >= hoff, lane < hoff + NH)


def _head_row(vals, hc0):
    lane = lax.broadcasted_iota(jnp.int32, (1, HPG * HD), 1)
    out = jnp.zeros((1, HPG * HD), f32)
    for r in range(HPG):
        out = jnp.where(jnp.logical_and(lane >= r * HD, lane < (r + 1) * HD), vals[:, hc0 + r:hc0 + r + 1], out)
    return out


def _chunk_of(j, rev, nxc, nch):
    return (nch - 1 - j) if rev else lax.rem(j + nxc, nch)


def _ssd_fwd(xs, bc, dt, la, consts, rev, nxc, name, y_acc=None):
    T = xs.shape[0]
    nch = T // Q
    hoff = NH if rev else 0
    e = 0 if rev else Q - 1
    cm = lambda j: _chunk_of(j, rev, nxc, nch)
    sel_w = consts[0]
    has_acc = y_acc is not None

    def kern(*refs):
        xs_ref, bc_ref, dt_ref, la_ref, sw_ref = refs[:5]
        yacc_ref = refs[5] if has_acc else None
        y_ref, hp_ref, h_ref = refs[5 + has_acc:]
        j = pl.program_id(0)

        @pl.when(j == 0)
        def _():
            h_ref[...] = jnp.zeros_like(h_ref)

        hp_ref[...] = h_ref[...]
        mask, _, hms = _masks(rev)
        la_all = la_ref[...]
        dt_all = dt_ref[...]
        la_t = jnp.transpose(la_all)
        dt_t = jnp.transpose(dt_all)
        la_e = la_all[e:e + 1, :]
        w2 = _split2(jnp.exp(jnp.where(_mine(hoff), la_e - la_all, 0.0)) * dt_all)
        e2 = _split2(jnp.exp(la_all))
        ela_e = jnp.exp(la_e)
        for g in range(NG):
            hc0 = hoff + g * HPG
            x = xs_ref[:, g * GW:(g + 1) * GW]
            bb = bc_ref[:, g * NS:(g + 1) * NS]
            cb = bc_ref[:, NG * NS + g * NS:NG * NS + (g + 1) * NS]
            ht = h_ref[g * NS:(g + 1) * NS, :]
            scores = _dot_nt(cb, bb)
            yoff = _dot(cb, ht.astype(bf16))
            wend = _dot(w2, sw_ref[g])
            expla = _dot(e2, sw_ref[g])
            mixes, xstack = [], []
            for r in range(HPG):
                hc = hc0 + r
                la_rep = jnp.broadcast_to(la_all[:, hc:hc + 1], (Q, 128))
                decay = jnp.exp(jnp.where(mask, la_rep - la_t[hc:hc + 1, :], NEG))
                mixes.append((scores * decay * dt_t[hc:hc + 1, :]).astype(bf16))
                xstack.append(jnp.where(hms[r], x, 0.0).astype(bf16))
            y = _dot(jnp.concatenate(mixes, axis=1), jnp.concatenate(xstack, axis=0)) + yoff * expla
            if has_acc:
                y = y + yacc_ref[:, g * GW:(g + 1) * GW]
            y_ref[:, g * GW:(g + 1) * GW] = y
            h_ref[g * NS:(g + 1) * NS, :] = ht * _head_row(ela_e, hc0) + _dot_tn(bb, (x * wend).astype(bf16))

    row = lambda j: (cm(j), 0)
    yblk = pl.BlockSpec((Q, DI), row)
    return pl.pallas_call(
        kern, out_shape=(S((T, DI), f32), S((nch, NG * NS, HPG * HD), f32)), grid=(nch,),
        in_specs=[yblk, pl.BlockSpec((Q, 2 * NG * NS), row), pl.BlockSpec((Q, 128), row), pl.BlockSpec((Q, 128), row),
                  _full(sel_w.shape)] + ([yblk] if has_acc else []),
        out_specs=(yblk, pl.BlockSpec((None, NG * NS, HPG * HD), lambda j: (cm(j), 0, 0))),
        scratch_shapes=[pltpu.VMEM((NG * NS, HPG * HD), f32)],
        input_output_aliases={5: 0} if has_acc else {},
        compiler_params=_params("arbitrary"), name=name)(xs, bc, dt, la, sel_w, *([y_acc] if has_acc else []))


def _ssd_bwd(xs, bc, dy, dt, la, hprev, dskip_full, consts, rev, nxc, name, acc=None):
    T = xs.shape[0]
    nch = T // Q
    hoff = NH if rev else 0
    e = 0 if rev else Q - 1
    cm = lambda j: _chunk_of(nch - 1 - j, rev, nxc, nch)
    has_acc = acc is not None
    sel_w, ind_h, ind_e = consts

    def kern(*refs):
        xs_ref, bc_ref, dy_ref, dt_ref, la_ref, hp_ref, dsk_ref, sw_ref, ih_ref, ie_ref = refs[:10]
        k = 10
        if has_acc:
            dxbc_in, a1_in, a2_in, r2_in, s_in = refs[k:k + 5]
            k += 5
        dxbc_ref, a1_ref, a2_ref, r2_ref, s_ref, g_ref, r2scr = refs[k:k + 7]
        j = pl.program_id(0)

        @pl.when(j == 0)
        def _():
            g_ref[...] = jnp.zeros_like(g_ref)

        mask, mask_t, hms = _masks(rev)
        la_all = la_ref[...]
        dt_all = dt_ref[...]
        la_t = jnp.transpose(la_all)
        dt_t = jnp.transpose(dt_all)
        la_e = la_all[e:e + 1, :]
        w2 = _split2(jnp.exp(jnp.where(_mine(hoff), la_e - la_all, 0.0)) * dt_all)
        e2 = _split2(jnp.exp(la_all))
        wed2 = jnp.concatenate([w2, e2, _split2(dt_all)], axis=0)
        ela_e = jnp.exp(la_e)
        r2scr[...] = jnp.zeros_like(r2scr)
        a1acc = jnp.zeros((Q, 128), f32)
        a2acc = jnp.zeros((Q, 128), f32)
        sacc = jnp.zeros((1, 128), f32)
        for g in range(NG):
            hc0 = hoff + g * HPG
            x = xs_ref[:, g * GW:(g + 1) * GW]
            bb = bc_ref[:, g * NS:(g + 1) * NS]
            cb = bc_ref[:, NG * NS + g * NS:NG * NS + (g + 1) * NS]
            dyv = dy_ref[:, g * GW:(g + 1) * GW]
            gt = g_ref[g * NS:(g + 1) * NS, :]
            ht = hp_ref[g * NS:(g + 1) * NS, :]
            gtb = gt.astype(bf16)
            htb = ht.astype(bf16)
            xb = x.astype(bf16)
            scores = _dot_nt(cb, bb)
            scores_t = _dot_nt(bb, cb)
            bg = _dot(bb, gtb)
            yoff = _dot(cb, htb)
            sel3 = _dot(wed2, sw_ref[g])
            wend, expla, dtf = sel3[0:Q], sel3[Q:2 * Q], sel3[2 * Q:3 * Q]
            dym = jnp.concatenate([jnp.where(hms[r], dyv, 0.0).astype(bf16) for r in range(HPG)], axis=0)
            dyx_all = _dot_nt(dym, xb)
            sdts, ems = [], []
            wsum = jnp.zeros((Q, Q), f32)
            for r in range(HPG):
                hc = hc0 + r
                la_rep = jnp.broadcast_to(la_all[:, hc:hc + 1], (Q, 128))
                la_r = la_t[hc:hc + 1, :]
                dt_r = dt_t[hc:hc + 1, :]
                decay = jnp.exp(jnp.where(mask, la_rep - la_r, NEG))
                decay_t = jnp.exp(jnp.where(mask_t, la_r - la_rep, NEG))
                dyx = dyx_all[r * Q:(r + 1) * Q, :]
                fm = dyx * (scores * decay)
                r2scr[hc:hc + 1, :] = jnp.sum(fm, axis=0, keepdims=True)
                ems.append(fm * dt_r)
                wsum = wsum + dyx * decay * dt_r
                sdts.append((scores_t * decay_t).astype(bf16))
            dx = dtf * _dot(jnp.concatenate(sdts, axis=1), dym) + wend * bg
            if not has_acc:
                dx = dx + dsk_ref[:, g * GW:(g + 1) * GW] * dyv
            red3 = _dot(jnp.concatenate([(dyv * yoff * expla).astype(bf16), (x * bg).astype(bf16), (gt * ht).astype(bf16)],
                                        axis=0), ih_ref[g])
            a1acc = a1acc + _dot(_split2(jnp.concatenate(ems, axis=1)), ie_ref[g]) + red3[0:Q]
            a2acc = a2acc + red3[Q:2 * Q]
            sacc = sacc + jnp.sum(red3[2 * Q:3 * Q], axis=0, keepdims=True)
            wb = wsum.astype(bf16)
            dysb = (dyv * expla).astype(bf16)
            dc = _dot(wb, bb) + _dot_nt(dysb, htb)
            db = _dot_tn(wb, cb) + _dot_nt((x * wend).astype(bf16), gtb)
            g_ref[g * NS:(g + 1) * NS, :] = gt * _head_row(ela_e, hc0) + _dot_tn(cb, dysb)
            if has_acc:
                dx = dx + dxbc_in[:, g * GW:(g + 1) * GW]
                db = db + dxbc_in[:, B0 + g * NS:B0 + (g + 1) * NS]
                dc = dc + dxbc_in[:, C0 + g * NS:C0 + (g + 1) * NS]
            dxbc_ref[:, g * GW:(g + 1) * GW] = dx
            dxbc_ref[:, B0 + g * NS:B0 + (g + 1) * NS] = db
            dxbc_ref[:, C0 + g * NS:C0 + (g + 1) * NS] = dc
        r2c = jnp.transpose(r2scr[...])
        sc = jnp.broadcast_to(sacc, (Q, 128))
        if has_acc:
            a1acc = a1acc + a1_in[...]
            a2acc = a2acc + a2_in[...]
            r2c = r2c + r2_in[...]
            sc = sc + s_in[...]
        a1_ref[...] = a1acc
        a2_ref[...] = a2acc
        r2_ref[...] = r2c
        s_ref[...] = sc

    blk = pl.BlockSpec((Q, 128), lambda j: (cm(j), 0))
    big = pl.BlockSpec((Q, 4096), lambda j: (cm(j), 0))
    wide = pl.BlockSpec((Q, DI), lambda j: (cm(j), 0))
    in_specs = [wide, pl.BlockSpec((Q, 2 * NG * NS), lambda j: (cm(j), 0)), wide, blk, blk,
                pl.BlockSpec((None, NG * NS, HPG * HD), lambda j: (cm(j), 0, 0)), _full((1, DI)),
                _full(sel_w.shape), _full(ind_h.shape), _full(ind_e.shape)]
    args = [xs, bc, dy, dt, la, hprev, dskip_full, sel_w, ind_h, ind_e]
    aliases = {}
    if has_acc:
        in_specs += [big, blk, blk, blk, blk]
        args += list(acc)
        aliases = {10: 0, 11: 1, 12: 2, 13: 3, 14: 4}
    return pl.pallas_call(
        kern, out_shape=(S((T, 4096), f32), S((T, 128), f32), S((T, 128), f32), S((T, 128), f32), S((T, 128), f32)),
        grid=(nch,), in_specs=in_specs, out_specs=(big, blk, blk, blk, blk),
        scratch_shapes=[pltpu.VMEM((NG * NS, HPG * HD), f32), pltpu.VMEM((128, Q), f32)],
        input_output_aliases=aliases,
        compiler_params=_params("arbitrary"), name=name)(*args)


def _ynorm_fwd(ysum, xs, proj, dskip_full, nw, L):
    nlx = L // RT

    def kern(ys_ref, xs_ref, za_ref, zb_ref, dsk_ref, nw_ref, y_ref, yn_ref, ynt_ref):
        y = ys_ref[...] + dsk_ref[...] * xs_ref[...]
        y_ref[...] = y
        hg = NG // 2
        for g in range(NG):
            z_ref = za_ref if g < hg else zb_ref
            sl = y[:, g * GW:(g + 1) * GW] * _silu(z_ref[:, (g % hg) * GW:(g % hg + 1) * GW])
            r = lax.rsqrt(jnp.mean(sl * sl, axis=1, keepdims=True) + EPS)
            yn = (sl * r) * nw_ref[:, g * GW:(g + 1) * GW]
            yn_ref[:, g * GW:(g + 1) * GW] = yn.astype(bf16)
            ynt_ref[g * GW:(g + 1) * GW, :] = jnp.transpose(yn).astype(bf16)

    blk = pl.BlockSpec((RT, DI), lambda i: (i, 0))
    return pl.pallas_call(
        kern, out_shape=(S((L, DI), f32), S((L, DI), bf16), S((DI, L), bf16)), grid=(nlx,),
        in_specs=[blk, blk, pl.BlockSpec((RT, DI // 2), lambda i: (i, Z0 // (DI // 2))),
                  pl.BlockSpec((RT, DI // 2), lambda i: (i, Z0 // (DI // 2) + 1)), _full((1, DI)), _full((1, DI))],
        out_specs=(blk, blk, pl.BlockSpec((DI, RT), lambda i: (0, i))),
        compiler_params=_params("parallel"), name="ynorm_fwd")(ysum, xs, proj, proj, dskip_full, nw)


def _ynorm_bwd(dyn, y, xs, proj, dskip_full, nw, dproj):
    L = y.shape[0]
    T = proj.shape[0]
    nlx, nt = L // RT, T // RT

    hw = DI // 2

    def kern(dyn_ref, y_ref, xs_ref, z_ref, dsk_ref, nw_ref, _alias, dz_ref, dy_ref, dnw_ref, dsk_acc):
        i = pl.program_id(1)

        @pl.when(i == 0)
        def _():
            dnw_ref[...] = jnp.zeros_like(dnw_ref)
            dsk_acc[...] = jnp.zeros_like(dsk_acc)

        @pl.when(i >= nlx)
        def _():
            dz_ref[...] = jnp.zeros_like(dz_ref)
            dy_ref[...] = jnp.zeros_like(dy_ref)

        @pl.when(i < nlx)
        def _():
            y = y_ref[...]
            z = z_ref[...]
            sz = _sig(z)
            gz = z * sz
            yz = y * gz
            dynv = dyn_ref[...]
            for g in range(hw // GW):
                cs = slice(g * GW, (g + 1) * GW)
                sl = yz[:, cs]
                r = lax.rsqrt(jnp.mean(sl * sl, axis=1, keepdims=True) + EPS)
                yhat = sl * r
                dn = dynv[:, cs]
                dnw_ref[:, cs] += jnp.sum(dn * yhat, axis=0, keepdims=True)
                dyh = dn * nw_ref[:, cs]
                dyz = r * (dyh - yhat * jnp.mean(dyh * yhat, axis=1, keepdims=True))
                dyv = dyz * gz[:, cs]
                dy_ref[:, cs] = dyv
                dz_ref[:, cs] = (dyz * y[:, cs] * _dsilu(z[:, cs], sz[:, cs])).astype(bf16)
                dsk_acc[:, cs] += jnp.sum(dyv * xs_ref[:, cs], axis=0, keepdims=True)

    xblk = pl.BlockSpec((RT, hw), lambda j, i: (jnp.minimum(i, nlx - 1), j))
    row = pl.BlockSpec((1, hw), lambda j, i: (0, j))
    return pl.pallas_call(
        kern, out_shape=(S(dproj.shape, bf16), S((T, DI), f32), S((1, DI), f32), S((1, DI), f32)), grid=(2, nt),
        in_specs=[xblk, xblk, xblk, pl.BlockSpec((RT, hw), lambda j, i: (jnp.minimum(i, nlx - 1), Z0 // hw + j)), row, row,
                  pl.BlockSpec(memory_space=pl.ANY)],
        out_specs=(pl.BlockSpec((RT, hw), lambda j, i: (i, Z0 // hw + j)), pl.BlockSpec((RT, hw), lambda j, i: (i, j)), row, row),
        input_output_aliases={6: 0},
        compiler_params=_params("arbitrary", "arbitrary"), name="ynorm_bwd")(dyn, y, xs, proj, dskip_full, nw, dproj)


def _head_sums(cols):
    def kern(c_ref, o_ref):
        o_ref[...] = jnp.broadcast_to(jnp.sum(c_ref[...], axis=1, keepdims=True), (NH, 128))

    return pl.pallas_call(kern, out_shape=S((NH, 128), f32), name="head_sums")(cols)


SEG_STRIDE = 96
SEG_PAD = 16
NSEG = RT // GRID_W
CONF_ROWS = SEG_PAD + NSEG * SEG_STRIDE


SHIFT_ROWS = CONF_ROWS - 8
CONF_CW = 256


CONF_RB = 32


def _seg_zero_pads(scr):
    scr[0:SEG_PAD, :] = jnp.zeros((SEG_PAD, scr.shape[1]), f32)
    for s in range(NSEG):
        lo = SEG_PAD + s * SEG_STRIDE + GRID_W
        scr[lo:lo + SEG_STRIDE - GRID_W, :] = jnp.zeros((SEG_STRIDE - GRID_W, scr.shape[1]), f32)


def _seg_row(r0):
    return SEG_PAD + (r0 // GRID_W) * SEG_STRIDE + r0 % GRID_W


def _shift_copies(cps, scr, cs):
    full = scr[:, cs]
    for s in range(1, 8):
        cps[s - 1, :, :] = pltpu.roll(full, CONF_ROWS - s, 0)[0:SHIFT_ROWS, :]


def _tap(cps, scr, cs, o):
    rs = o % 8
    return scr[pl.ds(o, GRID_W), cs] if rs == 0 else cps[rs - 1, pl.ds(o - rs, GRID_W), :]


def _conf_fwd(proj, w32, cb, lnw, lnb, L):
    nlx = L // RT

    def kern(v_ref, g_ref, cg_ref, w_ref, cb_ref, lnw_ref, lnb_ref, u1_ref, u3_ref, u3t_ref, scr, cps, u3_scr):
        _seg_zero_pads(scr)
        for r0 in range(0, RT, CONF_RB):
            rows = slice(r0, r0 + CONF_RB)
            scr[_seg_row(r0):_seg_row(r0) + CONF_RB, :] = v_ref[rows, :] * _sig(g_ref[rows, :])
        for cc in range(D // CONF_CW):
            cs = slice(cc * CONF_CW, (cc + 1) * CONF_CW)
            _shift_copies(cps, scr, cs)
            for s in range(NSEG):
                acc = jnp.broadcast_to(cb_ref[:, cs], (GRID_W, CONF_CW))
                for k in range(CK):
                    acc = acc + w_ref[k:k + 1, cs] * _tap(cps, scr, cs, SEG_PAD + s * SEG_STRIDE + k - CK // 2)
                u1_ref[s * GRID_W:(s + 1) * GRID_W, cs] = acc
        for r0 in range(0, RT, CONF_RB):
            rows = slice(r0, r0 + CONF_RB)
            u1 = u1_ref[rows, :]
            xc = u1 - jnp.mean(u1, axis=1, keepdims=True)
            r = lax.rsqrt(jnp.mean(xc * xc, axis=1, keepdims=True) + EPS)
            u2 = (xc * r) * lnw_ref[...] + lnb_ref[...]
            u3 = _silu(u2) * _silu(cg_ref[rows, :])
            u3_ref[rows, :] = u3.astype(bf16)
            u3_scr[rows, :] = u3
        u3t_ref[...] = jnp.transpose(u3_scr[...]).astype(bf16)

    blk = pl.BlockSpec((RT, D), lambda i: (i, 0))
    return pl.pallas_call(
        kern, out_shape=(S((L, D), f32), S((L, D), bf16), S((D, L), bf16)), grid=(nlx,),
        in_specs=[pl.BlockSpec((RT, D), lambda i: (i, GV0 // D)), pl.BlockSpec((RT, D), lambda i: (i, GG0 // D)),
                  pl.BlockSpec((RT, D), lambda i: (i, CG0 // D)), _full((32, D)), _full((1, D)), _full((1, D)), _full((1, D))],
        out_specs=(blk, blk, pl.BlockSpec((D, RT), lambda i: (0, i))),
        scratch_shapes=[pltpu.VMEM((CONF_ROWS, D), f32), pltpu.VMEM((7, SHIFT_ROWS, CONF_CW), f32), pltpu.VMEM((RT, D), f32)],
        compiler_params=_params("parallel"), name="conf_fwd")(proj, proj, proj, w32, cb, lnw, lnb)


def _conf_bwd(du3, u1, proj, w32, lnw, lnb, dproj):
    L = u1.shape[0]
    T = proj.shape[0]
    nlx, nt = L // RT, T // RT

    def kern(du3_ref, u1_ref, v_ref, g_ref, cg_ref, w_ref, lnw_ref, lnb_ref, _alias,
             o_ref, dw_ref, dcb_ref, dlw_ref, dlb_ref, scr_u, scr_d, du0_scr, cps_u, cps_d):
        i = pl.program_id(0)

        @pl.when(i == 0)
        def _():
            dw_ref[...] = jnp.zeros_like(dw_ref)
            dcb_ref[...] = jnp.zeros_like(dcb_ref)
            dlw_ref[...] = jnp.zeros_like(dlw_ref)
            dlb_ref[...] = jnp.zeros_like(dlb_ref)

        @pl.when(i >= nlx)
        def _():
            o_ref[...] = jnp.zeros_like(o_ref)

        @pl.when(i < nlx)
        def _():
            _seg_zero_pads(scr_u)
            _seg_zero_pads(scr_d)
            for r0 in range(0, RT, CONF_RB):
                rows = slice(r0, r0 + CONF_RB)
                cg = cg_ref[rows, :]
                scg = _sig(cg)
                u1 = u1_ref[rows, :]
                xc = u1 - jnp.mean(u1, axis=1, keepdims=True)
                r = lax.rsqrt(jnp.mean(xc * xc, axis=1, keepdims=True) + EPS)
                xhat = xc * r
                u2 = xhat * lnw_ref[...] + lnb_ref[...]
                s2 = _sig(u2)
                du3v = du3_ref[rows, :]
                du2 = du3v * (cg * scg) * _dsilu(u2, s2)
                o_ref[rows, 2 * D:3 * D] = (du3v * (u2 * s2) * _dsilu(cg, scg)).astype(bf16)
                dlw_ref[...] += jnp.sum(du2 * xhat, axis=0, keepdims=True)
                dlb_ref[...] += jnp.sum(du2, axis=0, keepdims=True)
                dxh = du2 * lnw_ref[...]
                du1 = r * (dxh - jnp.mean(dxh, axis=1, keepdims=True) - xhat * jnp.mean(dxh * xhat, axis=1, keepdims=True))
                dcb_ref[...] += jnp.sum(du1, axis=0, keepdims=True)
                scr_u[_seg_row(r0):_seg_row(r0) + CONF_RB, :] = v_ref[rows, :] * _sig(g_ref[rows, :])
                scr_d[_seg_row(r0):_seg_row(r0) + CONF_RB, :] = du1
            for cc in range(D // CONF_CW):
                cs = slice(cc * CONF_CW, (cc + 1) * CONF_CW)
                _shift_copies(cps_u, scr_u, cs)
                _shift_copies(cps_d, scr_d, cs)
                for k in range(CK):
                    t = jnp.zeros((GRID_W, CONF_CW), f32)
                    for s in range(NSEG):
                        base = SEG_PAD + s * SEG_STRIDE
                        t = t + scr_d[pl.ds(base, GRID_W), cs] * _tap(cps_u, scr_u, cs, base + k - CK // 2)
                    dw_ref[k:k + 1, cs] += jnp.sum(t, axis=0, keepdims=True)
                for s in range(NSEG):
                    base = SEG_PAD + s * SEG_STRIDE
                    acc = jnp.zeros((GRID_W, CONF_CW), f32)
                    for k in range(CK):
                        acc = acc + w_ref[k:k + 1, cs] * _tap(cps_d, scr_d, cs, base + CK // 2 - k)
                    du0_scr[s * GRID_W:(s + 1) * GRID_W, cs] = acc
            for r0 in range(0, RT, CONF_RB):
                rows = slice(r0, r0 + CONF_RB)
                du0 = du0_scr[rows, :]
                sg = _sig(g_ref[rows, :])
                o_ref[rows, 0:D] = (du0 * sg).astype(bf16)
                o_ref[rows, D:2 * D] = (du0 * v_ref[rows, :] * sg * (1.0 - sg)).astype(bf16)

    xmap = lambda i: (jnp.minimum(i, nlx - 1), 0)
    pmap = lambda cb: (lambda i: (jnp.minimum(i, nlx - 1), cb))
    return pl.pallas_call(
        kern, out_shape=(S(dproj.shape, bf16), S((32, D), f32), S((1, D), f32), S((1, D), f32), S((1, D), f32)), grid=(nt,),
        in_specs=[pl.BlockSpec((RT, D), xmap), pl.BlockSpec((RT, D), xmap),
                  pl.BlockSpec((RT, D), pmap(GV0 // D)), pl.BlockSpec((RT, D), pmap(GG0 // D)), pl.BlockSpec((RT, D), pmap(CG0 // D)),
                  _full((32, D)), _full((1, D)), _full((1, D)), pl.BlockSpec(memory_space=pl.ANY)],
        out_specs=(pl.BlockSpec((RT, 3 * D), lambda i: (i, GV0 // (3 * D))), _full((32, D)), _full((1, D)), _full((1, D)), _full((1, D))),
        scratch_shapes=[pltpu.VMEM((CONF_ROWS, D), f32), pltpu.VMEM((CONF_ROWS, D), f32), pltpu.VMEM((RT, D), f32),
                        pltpu.VMEM((7, SHIFT_ROWS, CONF_CW), f32), pltpu.VMEM((7, SHIFT_ROWS, CONF_CW), f32)],
        input_output_aliases={8: 0},
        compiler_params=_params("arbitrary"), name="conf_bwd")(du3, u1, proj, proj, proj, w32, lnw, lnb, dproj)


def _merge_fwd(bs, bc, proj):
    L = bs.shape[0]

    def kern(bs_ref, bc_ref, g1_ref, g2_ref, o_ref, ot_ref):
        mv = _sig(g1_ref[...]) * bs_ref[...] + _sig(g2_ref[...]) * bc_ref[...]
        o_ref[...] = mv.astype(bf16)
        ot_ref[...] = jnp.transpose(mv).astype(bf16)

    blk = pl.BlockSpec((RT, D), lambda i: (i, 0))
    return pl.pallas_call(
        kern, out_shape=(S((L, D), bf16), S((D, L), bf16)), grid=(L // RT,),
        in_specs=[blk, blk, pl.BlockSpec((RT, D), lambda i: (i, G10 // D)), pl.BlockSpec((RT, D), lambda i: (i, G20 // D))],
        out_specs=(blk, pl.BlockSpec((D, RT), lambda i: (0, i))),
        compiler_params=_params("parallel"), name="merge_fwd")(bs, bc, proj, proj)


def _merge_bwd(dmerged, bs, bc, proj):
    L = bs.shape[0]
    T = proj.shape[0]
    nlx, nt = L // RT, T // RT

    def kern(dm_ref, bs_ref, bc_ref, g1_ref, g2_ref, o_ref, dbs_ref, dbc_ref):
        i = pl.program_id(0)

        @pl.when(i >= nlx)
        def _():
            o_ref[...] = jnp.zeros_like(o_ref)

        @pl.when(i < nlx)
        def _():
            dm = dm_ref[...]
            s1 = _sig(g1_ref[...])
            s2 = _sig(g2_ref[...])
            dbs_ref[...] = (dm * s1).astype(bf16)
            dbc_ref[...] = (dm * s2).astype(bf16)
            o_ref[:, 0:D] = (dm * bs_ref[...] * s1 * (1.0 - s1)).astype(bf16)
            o_ref[:, D:2 * D] = (dm * bc_ref[...] * s2 * (1.0 - s2)).astype(bf16)

    xmap = lambda i: (jnp.minimum(i, nlx - 1), 0)
    pmap = lambda cb: (lambda i: (jnp.minimum(i, nlx - 1), cb))
    xblk = pl.BlockSpec((RT, D), xmap)
    return pl.pallas_call(
        kern, out_shape=(S((T, NP), bf16), S((L, D), bf16), S((L, D), bf16)), grid=(nt,),
        in_specs=[xblk, xblk, xblk, pl.BlockSpec((RT, D), pmap(G10 // D)), pl.BlockSpec((RT, D), pmap(G20 // D))],
        out_specs=(pl.BlockSpec((RT, 2 * D), lambda i: (i, G10 // (2 * D))), xblk, xblk),
        compiler_params=_params("arbitrary"), name="merge_bwd")(dmerged, bs, bc, proj, proj)


def _final(x, out, target, mod, fw):
    L = x.shape[0]

    def kern(x_ref, o_ref, t_ref, mod_ref, fw_ref, dx1_ref, dout_ref, loss_ref, dfw_ref, dg_ref):
        i = pl.program_id(0)

        @pl.when(i == 0)
        def _():
            loss_ref[...] = jnp.zeros_like(loss_ref)
            dfw_ref[...] = jnp.zeros_like(dfw_ref)
            dg_ref[...] = jnp.zeros_like(dg_ref)

        gate = mod_ref[0:1, 2 * D:3 * D]
        ov = o_ref[...]
        x1 = x_ref[...] + gate * ov
        r = lax.rsqrt(jnp.mean(x1 * x1, axis=1, keepdims=True) + EPS)
        xn = x1 * r
        fw = fw_ref[...]
        err = xn * fw - t_ref[...]
        part = 0.5 * jnp.sum(jnp.mean(err * err, axis=1, keepdims=True), axis=0, keepdims=True)
        loss_ref[...] += jnp.broadcast_to(part, (8, 128))
        dy = err * (1.0 / D)
        dfw_ref[...] += jnp.sum(dy * xn, axis=0, keepdims=True)
        dyw = dy * fw
        dx1 = r * (dyw - xn * jnp.mean(dyw * xn, axis=1, keepdims=True))
        dx1_ref[...] = dx1
        dout_ref[...] = (gate * dx1).astype(bf16)
        dg_ref[...] += jnp.sum(dx1 * ov, axis=0, keepdims=True)

    blk = pl.BlockSpec((RT, D), lambda i: (i, 0))
    return pl.pallas_call(
        kern, out_shape=(S((L, D), f32), S((L, D), bf16), S((8, 128), f32), S((1, D), f32), S((1, D), f32)), grid=(L // RT,),
        in_specs=[blk, blk, blk, _full((8, 3 * D)), _full((1, D))],
        out_specs=(blk, blk, _full((8, 128)), _full((1, D)), _full((1, D))),
        compiler_params=_params("arbitrary"), name="final")(x, out, target, mod, fw)


def _me():
    return 4 * lax.axis_index("x") + 2 * lax.axis_index("y") + lax.axis_index("c")


def _xchg_copy(ins, outs, send_sems, recv_sems, modes, a, k, me):
    peer = lax.rem(me + k, N_DEV)
    pid = (peer // 4, lax.rem(peer // 2, 2), lax.rem(peer, 2))
    src = ins[a].at[peer] if modes[a] else ins[a]
    return pltpu.make_async_remote_copy(src_ref=src, dst_ref=outs[a].at[me], send_sem=send_sems.at[a, k - 1],
                                        recv_sem=recv_sems.at[a, k - 1], device_id=pid, device_id_type=MESH)


def _xchg_local(ins, outs, loc_sems, modes, a, me):
    return pltpu.make_async_copy(ins[a].at[me] if modes[a] else ins[a], outs[a].at[me], loc_sems.at[a])


def _xchg_start(ins, outs, send_sems, recv_sems, loc_sems, modes):
    me = _me()
    for a in range(len(modes)):
        _xchg_local(ins, outs, loc_sems, modes, a, me).start()
        for k in range(1, N_DEV):
            _xchg_copy(ins, outs, send_sems, recv_sems, modes, a, k, me).start()


def _xchg_wait(ins, outs, send_sems, recv_sems, loc_sems, modes):
    me = _me()
    for a in range(len(modes)):
        for k in range(1, N_DEV):
            frm = lax.rem(me + N_DEV - k, N_DEV)
            src = ins[a].at[frm] if modes[a] else ins[a]
            pltpu.make_async_remote_copy(src_ref=src, dst_ref=outs[a].at[frm], send_sem=send_sems.at[a, k - 1],
                                         recv_sem=recv_sems.at[a, k - 1], device_id=(0, 0, 0), device_id_type=MESH).wait_recv()
    for a in range(len(modes)):
        for k in range(1, N_DEV):
            _xchg_copy(ins, outs, send_sems, recv_sems, modes, a, k, me).wait_send()
        _xchg_local(ins, outs, loc_sems, modes, a, me).wait()


def _xchg_out_shapes(arrs, modes):
    return tuple(S((N_DEV,) + (a.shape[1:] if sc else a.shape), a.dtype) for a, sc in zip(arrs, modes))


def _xchg_sems(n):
    return [pltpu.SemaphoreType.DMA((n, N_DEV - 1)), pltpu.SemaphoreType.DMA((n, N_DEV - 1)), pltpu.SemaphoreType.DMA((n,))]


def _exchange(arrs, modes, name):
    n = len(arrs)

    def kern(*refs):
        ins, outs, sems = refs[:n], refs[n:2 * n], refs[2 * n:]
        _xchg_start(ins, outs, *sems, modes)
        _xchg_wait(ins, outs, *sems, modes)

    anyspec = pl.BlockSpec(memory_space=pl.ANY)
    return pl.pallas_call(
        kern, out_shape=_xchg_out_shapes(arrs, modes), in_specs=[anyspec] * n, out_specs=tuple([anyspec] * n),
        scratch_shapes=_xchg_sems(n), name=name)(*arrs)


def _gather2(arrs, name):
    n = len(arrs)

    def kern(*refs):
        ins, outs = refs[:n], refs[n:2 * n]
        send_sems, recv_sems, loc_sems = refs[2 * n:]
        x, y, c = lax.axis_index("x"), lax.axis_index("y"), lax.axis_index("c")
        me, sib = (x, y, c), (x, y, 1 - c)
        chips = [(1 - x, y), (x, 1 - y), (1 - x, 1 - y)]

        def slot(a, p):
            return outs[a].at[4 * p[0] + 2 * p[1] + p[2]]

        def cp(a, k, block, to, own=False):
            return pltpu.make_async_remote_copy(src_ref=ins[a] if own else slot(a, block), dst_ref=slot(a, block),
                                                send_sem=send_sems.at[a, k], recv_sem=recv_sems.at[a, k],
                                                device_id=to, device_id_type=MESH)

        started = []
        for a in range(n):
            pltpu.make_async_copy(ins[a], slot(a, me), loc_sems.at[a]).start()
            started.append(cp(a, 0, me, sib, own=True))
            started += [cp(a, 1 + j, me, (*chips[j], c), own=True) for j in range(2)]
        for s in started:
            s.start()
        for j in range(2):
            for a in range(n):
                cp(a, 1 + j, (*chips[j], c), me).wait_recv()
                fwd = cp(a, 4 + j, (*chips[j], c), sib)
                fwd.start()
                started.append(fwd)

            @pl.when(c == j)
            def _():
                for a in range(n):
                    cp(a, 3, (*chips[j], c), (*chips[1 - j], c)).start()
        for a in range(n):
            cp(a, 3, (*chips[2], c), me).wait_recv()
            fwd = cp(a, 6, (*chips[2], c), sib)
            fwd.start()
            started.append(fwd)
        for a in range(n):
            cp(a, 0, sib, me).wait_recv()
            for j in range(3):
                cp(a, 4 + j, (*chips[j], 1 - c), me).wait_recv()
        for s in started:
            s.wait_send()
        for a in range(n):
            cp(a, 3, me, me).wait_send()
            pltpu.make_async_copy(ins[a], slot(a, me), loc_sems.at[a]).wait()

    anyspec = pl.BlockSpec(memory_space=pl.ANY)
    return pl.pallas_call(
        kern, out_shape=_xchg_out_shapes(arrs, (False,) * n), in_specs=[anyspec] * n, out_specs=tuple([anyspec] * n),
        scratch_shapes=[pltpu.SemaphoreType.DMA((n, 7)), pltpu.SemaphoreType.DMA((n, 7)), pltpu.SemaphoreType.DMA((n,))],
        name=name)(*arrs)


def _adamw(parts, w, m, v, name):
    r, c = w.shape
    n_parts = parts.shape[0]
    tr = r
    for cand in (128, 64, 32, 16, 8):
        if r % cand == 0 and r > cand:
            tr = cand
            break
    c1 = 1.0 / (1.0 - ADAM_B1 ** ADAM_STEP)
    c2 = 1.0 / (1.0 - ADAM_B2 ** ADAM_STEP)

    def kern(p_ref, w_ref, m_ref, v_ref, g_ref, d_ref, m2_ref, v2_ref):
        g = p_ref[0].astype(f32)
        for i in range(1, n_parts):
            g = g + p_ref[i].astype(f32)
        g_ref[...] = g
        m2 = ADAM_B1 * m_ref[...] + (1.0 - ADAM_B1) * g
        v2 = ADAM_B2 * v_ref[...] + (1.0 - ADAM_B2) * (g * g)
        m2_ref[...] = m2
        v2_ref[...] = v2
        d_ref[...] = -ADAM_LR * ((m2 * c1) / (jnp.sqrt(v2 * c2) + ADAM_EPS) + ADAM_WD * w_ref[...])

    blk = pl.BlockSpec((tr, c), lambda i: (i, 0))
    sh = S((r, c), f32)
    return pl.pallas_call(
        kern, out_shape=(sh, sh, sh, sh), grid=(r // tr,),
        in_specs=[pl.BlockSpec((n_parts, tr, c), lambda i: (0, i, 0)), blk, blk, blk], out_specs=(blk, blk, blk, blk),
        compiler_params=_params("parallel"), name=name)(parts, w, m, v)


_SMALL = (("c_ctx", 1024), ("b_mod", 3072), ("norm_w", 1024), ("ssm_conv_b", 4096), ("dt_bias", 64), ("a_log", 64),
          ("d_skip", 32), ("ssm_norm_w", 2048), ("conf_conv_b", 1024), ("conf_ln_w", 1024), ("conf_ln_b", 1024),
          ("final_norm_w", 1024))
SMALL_TILE = 8 * 128


def _pack_small(d):
    rows = []
    for name, n in _SMALL:
        v = d[name].reshape(-1).astype(f32)
        pad = (-n) % SMALL_TILE
        if pad:
            v = jnp.concatenate([v, jnp.zeros((pad,), f32)])
        rows.append(v.reshape(-1, 128))
    return jnp.concatenate(rows, axis=0)


def _unpack_small(p, shapes):
    out, r0 = {}, 0
    for name, n in _SMALL:
        nr = 8 * ((n + SMALL_TILE - 1) // SMALL_TILE)
        out[name] = p[r0:r0 + nr].reshape(-1)[:n].reshape(shapes[name])
        r0 += nr
    return out


def _permute_w_in(w):
    return jnp.concatenate([w[:, 9280:11328], w[:, 2048:4096], w[:, 0:2048], w[:, 6208:9280], w[:, 4160:6208],
                            w[:, 4096:4160], jnp.zeros((w.shape[0], NP - DT0 - 64), w.dtype)], axis=1)


def _unpermute_w_in(wp):
    return jnp.concatenate([wp[:, PX0:PX0 + 2048], wp[:, PBC0:PBC0 + 2048], wp[:, DT0:DT0 + 64], wp[:, Z0:Z0 + 2048],
                            wp[:, GV0:GV0 + 3072], wp[:, G10:G10 + 2048]], axis=1)


def _cols_gathered(g):
    return jnp.transpose(g, (1, 0, 2)).reshape(g.shape[1], N_DEV * g.shape[2])


def _cols_to_blocks(a):
    r, c8 = a.shape
    return jnp.transpose(a.reshape(r, N_DEV, c8 // N_DEV), (1, 0, 2))


def kernel(x, c, ctx, c_ctx, w_mod, b_mod, norm_w, w_in, ssm_conv_w, ssm_conv_b, dt_bias, a_log, d_skip, ssm_norm_w, w_out_ssm, conf_conv_w, conf_conv_b, conf_ln_w, conf_ln_b, w_out_conf, w_out, final_norm_w, loss_target, m_c_ctx, m_w_mod, m_b_mod, m_norm_w, m_w_in, m_ssm_conv_w, m_ssm_conv_b, m_dt_bias, m_a_log, m_d_skip, m_ssm_norm_w, m_w_out_ssm, m_conf_conv_w, m_conf_conv_b, m_conf_ln_w, m_conf_ln_b, m_w_out_conf, m_w_out, m_final_norm_w, v_c_ctx, v_w_mod, v_b_mod, v_norm_w, v_w_in, v_ssm_conv_w, v_ssm_conv_b, v_dt_bias, v_a_log, v_d_skip, v_ssm_norm_w, v_w_out_ssm, v_conf_conv_w, v_conf_conv_b, v_conf_ln_w, v_conf_ln_b, v_w_out_conf, v_w_out, v_final_norm_w):
    L = x.shape[1]
    Lc = ctx.shape[1]
    T = L + Lc
    nlx = L // RT
    nxc = L // Q
    x2 = x.reshape(L, D)
    ctx2 = ctx.reshape(Lc, D)
    tgt = loss_target.reshape(L, D)

    gathered = _gather2([w_in[0].astype(bf16), w_mod[0].astype(bf16), ssm_conv_w[0], conf_conv_w[0]], name="gather_weights")
    wp = _permute_w_in(_cols_gathered(gathered[0]))
    wmod_bf = _cols_gathered(gathered[1])
    scw8 = jnp.concatenate([_cols_gathered(gathered[2]), jnp.zeros((8 - SK, 4096), f32)], axis=0)
    ccw32 = jnp.concatenate([_cols_gathered(gathered[3]), jnp.zeros((32 - CK, D), f32)], axis=0)

    norm_w1 = norm_w.reshape(1, D)
    scb = ssm_conv_b.reshape(1, 4096)
    bias_row = jnp.concatenate([dt_bias.reshape(1, 2 * NH), jnp.zeros((1, 128 - 2 * NH), f32)], axis=1)
    alog_row = jnp.concatenate([a_log.reshape(1, 2 * NH), jnp.zeros((1, 128 - 2 * NH), f32)], axis=1)
    dskip_full = jnp.repeat(d_skip.reshape(NH), HD).reshape(1, DI)
    snw = ssm_norm_w.reshape(1, DI)
    ccb = conf_conv_b.reshape(1, D)
    lnw = conf_ln_w.reshape(1, D)
    lnb = conf_ln_b.reshape(1, D)
    fw = final_norm_w.reshape(1, D)

    cc8 = jnp.concatenate([c.reshape(1, D), c_ctx.reshape(1, D), jnp.zeros((6, D), f32)], axis=0)
    mod, silu_rows = _mod_fwd(cc8, wmod_bf, b_mod.reshape(1, 3 * D))
    h, h_t = _prenorm(x2, ctx2, norm_w1, mod)
    proj, wos_g, woc_g, wo_g = _matmul(
        h, wp, f32, "proj_gather", tn=NP // 5,
        comm=([w_out_ssm[0].astype(bf16), w_out_conf[0].astype(bf16), w_out[0].astype(bf16)], (False,) * 3))
    wos_bf = wos_g.reshape(DI, D)
    woc_bf = woc_g.reshape(D, D)
    wo_bf = wo_g.reshape(D, D)
    xs = _ssm_conv_fwd(proj, scw8, scb, nlx, 0, f32, "ssm_conv_fwd_x")
    bcm = _ssm_conv_fwd(proj, scw8, scb, nlx, 1, bf16, "ssm_conv_fwd_bc")
    dt, la = _dt_prep(proj, bias_row, alog_row)
    consts_f, consts_b = _scan_consts(False), _scan_consts(True)
    yf, hp_f = _ssd_fwd(xs, bcm, dt, la, consts_f, False, nxc, "ssd_fwd_f")
    ysum, hp_b = _ssd_fwd(xs, bcm, dt, la, consts_b, True, nxc, "ssd_fwd_b", y_acc=yf)
    y, yn, yn_t = _ynorm_fwd(ysum, xs, proj, dskip_full, snw, L)
    bs = _matmul(yn, wos_bf, f32, "branch_ssm", tm=1024, tk=2048)
    u1, u3, u3_t = _conf_fwd(proj, ccw32, ccb, lnw, lnb, L)
    bc = _matmul(u3, woc_bf, f32, "branch_conf", tm=2048)
    merged, merged_t = _merge_fwd(bs, bc, proj)
    out = _matmul(merged, wo_bf, f32, "out_proj", tm=2048)
    dx1, dout, loss_acc, dfw, dgate = _final(x2, out, tgt, mod, fw)

    dmerged = _matmul(dout, wo_bf, f32, "d_merged", tb=True, tm=2048)
    g_wo = _matmul(merged_t, dout, bf16, "g_w_out", tm=1024, tk=2048)
    dproj, dbs, dbc = _merge_bwd(dmerged, bs, bc, proj)
    dyn = _matmul(dbs, wos_bf, f32, "d_yn", tb=True, tm=1024, tn=2048)
    g_wos = _matmul(yn_t, dbs, bf16, "g_w_out_ssm", tm=1024, tk=2048)
    du3 = _matmul(dbc, woc_bf, f32, "d_u3", tb=True, tm=2048)
    g_woc = _matmul(u3_t, dbc, bf16, "g_w_out_conf", tm=1024, tk=2048)
    dproj, g_ccw, g_ccb, g_lnw, g_lnb = _conf_bwd(du3, u1, proj, ccw32, lnw, lnb, dproj)
    dproj, dy, g_snw, dsk_cols = _ynorm_bwd(dyn, y, xs, proj, dskip_full, snw, dproj)
    acc_f = _ssd_bwd(xs, bcm, dy, dt, la, hp_f, dskip_full, consts_f, False, nxc, "ssd_bwd_f")
    dxbc, a1, a2, r2, sv = _ssd_bwd(xs, bcm, dy, dt, la, hp_b, dskip_full, consts_b, True, nxc, "ssd_bwd_b", acc=acc_f)
    dproj, g_dtb, g_alog = _dt_bwd(a1, a2, r2, sv, dt, la, proj, bias_row, alog_row, dproj)
    dpre, g_scw, g_scb = _ssm_conv_dpre(dxbc, proj, scw8, scb, nlx)
    dproj = _ssm_conv_t(dpre, scw8, dproj, nlx)
    g_wp, *parts_b = _matmul(
        h_t, dproj, bf16, "g_w_in_scatter", tm=1024, tn=NP // 5,
        comm=([g_wos.reshape(N_DEV, DI // N_DEV, D), g_woc.reshape(N_DEV, D // N_DEV, D), g_wo.reshape(N_DEV, D // N_DEV, D),
               _cols_to_blocks(g_scw[:SK]), _cols_to_blocks(g_ccw[:CK])], (True,) * 5))
    dh, parts_a = _matmul(dproj, wp, f32, "d_h_scatter", tb=True, tk=NP // 5,
                          comm=([_cols_to_blocks(_unpermute_w_in(g_wp))], (True,)))
    parts = [parts_a] + parts_b
    gx, g_nw, macc = _prenorm_bwd(x2, ctx2, dh, dx1, norm_w1, mod)
    dmod_x = jnp.concatenate([macc[0:1], macc[1:2], dgate], axis=1)
    dmod_c = jnp.concatenate([macc[2:3], macc[3:4], jnp.zeros((1, D), f32)], axis=1)
    dmod8 = jnp.concatenate([dmod_x, dmod_c, jnp.zeros((6, 3 * D), f32)], axis=0)
    ct = jnp.concatenate([c.reshape(D, 1), c_ctx.reshape(D, 1), jnp.zeros((D, 126), f32)], axis=1)
    g_bmod, g_cctx = _mod_bwd(ct, dmod8, wmod_bf)
    g_dskip = _head_sums(dsk_cols.reshape(NH, HD))[:, 0]

    small_g = _pack_small({
        "c_ctx": g_cctx[:, 0], "b_mod": g_bmod, "norm_w": g_nw, "ssm_conv_b": g_scb, "dt_bias": g_dtb[0, :2 * NH],
        "a_log": g_alog[0, :2 * NH], "d_skip": g_dskip, "ssm_norm_w": g_snw, "conf_conv_b": g_ccb, "conf_ln_w": g_lnw,
        "conf_ln_b": g_lnb, "final_norm_w": dfw})
    fac = jnp.concatenate([silu_rows[0:1].reshape(D // 128, 128), dmod_x.reshape(3 * D // 128, 128),
                           dmod_c.reshape(3 * D // 128, 128)], axis=0)
    small_parts, fac_all = _exchange([small_g, fac], (False, False), name="exchange_tail")
    nr = D // 128
    sct = jnp.concatenate([fac_all[:, 0:nr].reshape(N_DEV, D).T, silu_rows[1:2].T, jnp.zeros((D, 128 - N_DEV - 1), f32)], axis=1)
    my_cols = (4 * lax.axis_index("x") + 2 * lax.axis_index("y") + lax.axis_index("c")) * (3 * D // N_DEV)
    dmx_all = lax.dynamic_slice(fac_all[:, nr:4 * nr].reshape(N_DEV, 3 * D), (0, my_cols), (N_DEV, 3 * D // N_DEV))
    dmc_all = lax.dynamic_slice(fac_all[:, 4 * nr:7 * nr].reshape(N_DEV, 3 * D), (0, my_cols), (N_DEV, 3 * D // N_DEV))
    g_wmod = _wmod_grad(sct, dmx_all, dmc_all)
    parts = [parts[0], g_wmod[None]] + parts[1:]

    given = dict(c_ctx=c_ctx, w_mod=w_mod, b_mod=b_mod, norm_w=norm_w, w_in=w_in, ssm_conv_w=ssm_conv_w, ssm_conv_b=ssm_conv_b,
                 dt_bias=dt_bias, a_log=a_log, d_skip=d_skip, ssm_norm_w=ssm_norm_w, w_out_ssm=w_out_ssm, conf_conv_w=conf_conv_w,
                 conf_conv_b=conf_conv_b, conf_ln_w=conf_ln_w, conf_ln_b=conf_ln_b, w_out_conf=w_out_conf, w_out=w_out,
                 final_norm_w=final_norm_w)
    ms = dict(c_ctx=m_c_ctx, w_mod=m_w_mod, b_mod=m_b_mod, norm_w=m_norm_w, w_in=m_w_in, ssm_conv_w=m_ssm_conv_w,
              ssm_conv_b=m_ssm_conv_b, dt_bias=m_dt_bias, a_log=m_a_log, d_skip=m_d_skip, ssm_norm_w=m_ssm_norm_w,
              w_out_ssm=m_w_out_ssm, conf_conv_w=m_conf_conv_w, conf_conv_b=m_conf_conv_b, conf_ln_w=m_conf_ln_w,
              conf_ln_b=m_conf_ln_b, w_out_conf=m_w_out_conf, w_out=m_w_out, final_norm_w=m_final_norm_w)
    vs = dict(c_ctx=v_c_ctx, w_mod=v_w_mod, b_mod=v_b_mod, norm_w=v_norm_w, w_in=v_w_in, ssm_conv_w=v_ssm_conv_w,
              ssm_conv_b=v_ssm_conv_b, dt_bias=v_dt_bias, a_log=v_a_log, d_skip=v_d_skip, ssm_norm_w=v_ssm_norm_w,
              w_out_ssm=v_w_out_ssm, conf_conv_w=v_conf_conv_w, conf_conv_b=v_conf_conv_b, conf_ln_w=v_conf_ln_w,
              conf_ln_b=v_conf_ln_b, w_out_conf=v_w_out_conf, w_out=v_w_out, final_norm_w=v_final_norm_w)
    grads, deltas, new_m, new_v = {}, {}, {}, {}
    sharded = ("w_in", "w_mod", "w_out_ssm", "w_out_conf", "w_out", "ssm_conv_w", "conf_conv_w")
    for i, nm in enumerate(sharded):
        shp = given[nm].shape
        w2 = given[nm].reshape(shp[1], shp[2])
        res = _adamw(parts[i], w2, ms[nm].reshape(w2.shape), vs[nm].reshape(w2.shape), "adamw_" + nm)
        grads[nm], deltas[nm], new_m[nm], new_v[nm] = [r.reshape(shp) for r in res]
    shapes = {nm: given[nm].shape for nm, _ in _SMALL}
    res = _adamw(small_parts, _pack_small(given), _pack_small(ms), _pack_small(vs), "adamw_small")
    for dst, packed in zip((grads, deltas, new_m, new_v), res):
        dst.update(_unpack_small(packed, shapes))

    loss = lax.psum(loss_acc[0, 0], ("x", "y", "c"))
    order = ("c_ctx", "w_mod", "b_mod", "norm_w", "w_in", "ssm_conv_w", "ssm_conv_b", "dt_bias", "a_log", "d_skip", "ssm_norm_w",
             "w_out_ssm", "conf_conv_w", "conf_conv_b", "conf_ln_w", "conf_ln_b", "w_out_conf", "w_out", "final_norm_w")
    return (loss, gx.reshape(1, L, D), *[grads[n] for n in order], *[deltas[n] for n in order],
            *[new_m[n] for n in order], *[new_v[n] for n in order])
```

```python
import jax
import jax.numpy as jnp
from jax import lax
from jax.experimental import pallas as pl
from jax.experimental.pallas import tpu as pltpu

f32 = jnp.float32
bf16 = jnp.bfloat16

D = 1024
DI = 2048
NG = 8
HPG = 4
HD = 64
GW = HPG * HD
NS = 128
NH = 32
Q = 128
GRID_W = 64
CK = 31
SK = 4
EPS = 1e-6
RT = 256
N_DEV = 8
IN_COLS = 11328
G10, G20, PBC0, PX0, GV0, GG0, CG0, Z0, DT0, NP = 0, 1024, 2048, 4096, 6144, 7168, 8192, 9216, 11264, 11520
CONV_CT = 2048
B0, C0 = 2048, 3072
VMEM_LIMIT = 50 * 1024 * 1024
NEG = -1e30

ADAM_LR, ADAM_B1, ADAM_B2, ADAM_EPS, ADAM_WD, ADAM_STEP = 0.001, 0.9, 0.999, 1e-08, 0.01, 10

MESH = pl.DeviceIdType.MESH
S = jax.ShapeDtypeStruct


def _params(*sem):
    return pltpu.CompilerParams(dimension_semantics=tuple(sem) if sem else None, vmem_limit_bytes=VMEM_LIMIT)


def _sig(x):
    return 1.0 / (1.0 + jnp.exp(-x))


def _silu(x):
    return x * _sig(x)


def _dsilu(x, s):
    return s * (1.0 + x * (1.0 - s))


def _dot(a, b):
    return jnp.dot(a, b, preferred_element_type=f32)


def _dot_nt(a, b):
    return lax.dot_general(a, b, (((1,), (1,)), ((), ())), preferred_element_type=f32)


def _dot_tn(a, b):
    return lax.dot_general(a, b, (((0,), (0,)), ((), ())), preferred_element_type=f32)


def _dot3(t_bf, v):
    v1 = v.astype(bf16)
    r1 = v - v1.astype(f32)
    v2 = r1.astype(bf16)
    v3 = (r1 - v2.astype(f32)).astype(bf16)
    return _dot(t_bf, v1) + _dot(t_bf, v2) + _dot(t_bf, v3)


def _pick(n, prefs):
    for p in prefs:
        if n % p == 0:
            return p
    return n


def _full(shape):
    nd = len(shape)
    return pl.BlockSpec(shape, lambda *_: (0,) * nd)


def _matmul(a, b, out_dtype, name, tm=None, tn=None, tk=None, tb=False, comm=None):
    m, k = a.shape
    n = b.shape[0] if tb else b.shape[1]
    tm = tm if tm and m % tm == 0 else _pick(m, (768, 512, 256, 128))
    tn = tn if tn and n % tn == 0 else _pick(n, (1024, 512, 256, 128))
    tk = tk if tk and k % tk == 0 else _pick(k, (1024, 768, 512, 256, 128))
    nk = k // tk
    gi, gj = m // tm, n // tn
    carrs, modes = comm if comm else ((), ())
    nc = len(carrs)

    def kern(*refs):
        a_ref, b_ref = refs[:2]
        cins = refs[2:2 + nc]
        o_ref = refs[2 + nc]
        couts = refs[3 + nc:3 + 2 * nc]
        acc_ref = refs[3 + 2 * nc]
        sems = refs[4 + 2 * nc:]
        i, j, kk = pl.program_id(0), pl.program_id(1), pl.program_id(2)
        if nc:
            @pl.when(jnp.logical_and(jnp.logical_and(i == 0, j == 0), kk == 0))
            def _():
                _xchg_start(cins, couts, *sems, modes)

        part = _dot_nt(a_ref[...], b_ref[...]) if tb else _dot(a_ref[...], b_ref[...])
        if nk == 1:
            o_ref[...] = part.astype(o_ref.dtype)
        else:
            @pl.when(kk == 0)
            def _():
                acc_ref[...] = part

            @pl.when(kk > 0)
            def _():
                acc_ref[...] += part

            @pl.when(kk == nk - 1)
            def _():
                o_ref[...] = acc_ref[...].astype(o_ref.dtype)

        if nc:
            @pl.when(jnp.logical_and(jnp.logical_and(i == gi - 1, j == gj - 1), kk == nk - 1))
            def _():
                _xchg_wait(cins, couts, *sems, modes)

    anyspec = pl.BlockSpec(memory_space=pl.ANY)
    bspec = pl.BlockSpec((tn, tk), lambda i, j, kk: (j, kk)) if tb else pl.BlockSpec((tk, tn), lambda i, j, kk: (kk, j))
    out_shape = (S((m, n), out_dtype),) + _xchg_out_shapes(carrs, modes)
    res = pl.pallas_call(
        kern, out_shape=out_shape, grid=(gi, gj, nk),
        in_specs=[pl.BlockSpec((tm, tk), lambda i, j, kk: (i, kk)), bspec] + [anyspec] * nc,
        out_specs=(pl.BlockSpec((tm, tn), lambda i, j, kk: (i, j)),) + (anyspec,) * nc,
        scratch_shapes=[pltpu.VMEM((tm, tn), f32)] + (_xchg_sems(nc) if nc else []),
        compiler_params=_params(*((("arbitrary",) * 3) if nc else ("parallel", "parallel", "arbitrary"))), name=name)(a, b, *carrs)
    return res if nc else res[0]


def _mod_fwd(cc8, w_mod_bf, b_mod):
    def kern(c_ref, w_ref, b_ref, o_ref, s_ref):
        s = _silu(c_ref[...])
        s_ref[...] = s
        o_ref[...] = _dot(s.astype(bf16), w_ref[...]) + b_ref[...]

    return pl.pallas_call(kern, out_shape=(S((8, 3 * D), f32), S((8, D), f32)), compiler_params=_params(),
                          name="mod_fwd")(cc8, w_mod_bf, b_mod)


def _mod_bwd(ct, dmod8, w_mod_bf):
    tc = 512
    nj = 3 * D // tc

    def kern(ct_ref, dm_ref, w_ref, db_ref, dc_ref):
        j = pl.program_id(0)
        cx = ct_ref[:, 1:2]
        sx = _sig(cx)
        dmc = dm_ref[1:2, :]
        db_ref[...] = dm_ref[0:1, :] + dmc
        t = jnp.sum(w_ref[...].astype(f32) * dmc.astype(bf16).astype(f32), axis=1, keepdims=True) * _dsilu(cx, sx)

        @pl.when(j == 0)
        def _():
            dc_ref[...] = jnp.zeros_like(dc_ref)

        dc_ref[...] += jnp.broadcast_to(t, (D, 128))

    return pl.pallas_call(
        kern, out_shape=(S((1, 3 * D), f32), S((D, 128), f32)), grid=(nj,),
        in_specs=[_full((D, 128)), pl.BlockSpec((8, tc), lambda j: (0, j)), pl.BlockSpec((D, tc), lambda j: (0, j))],
        out_specs=(pl.BlockSpec((1, tc), lambda j: (0, j)), _full((D, 128))),
        compiler_params=_params("arbitrary"), name="mod_bwd")(ct, dmod8, w_mod_bf)


def _wmod_grad(sct, dmx, dmc):
    cols = dmx.shape[1]

    def kern(s_ref, dmx_ref, dmc_ref, g_ref):
        dmc_sum = dmc_ref[0:1, :]
        for d in range(1, N_DEV):
            dmc_sum = dmc_sum + dmc_ref[d:d + 1, :]
        g = s_ref[:, N_DEV:N_DEV + 1] * dmc_sum
        for d in range(N_DEV):
            g = g + s_ref[:, d:d + 1] * dmx_ref[d:d + 1, :]
        g_ref[...] = g

    return pl.pallas_call(kern, out_shape=S((D, cols), f32), compiler_params=_params(), name="wmod_grad")(sct, dmx, dmc)


def _prenorm(x, ctx, norm_w, mod):
    L, Lc = x.shape[0], ctx.shape[0]
    nlx, nt = L // RT, (L + Lc) // RT

    def kern(x_ref, c_ref, nw_ref, mod_ref, h_ref, ht_ref):
        i = pl.program_id(0)
        is_c = i >= nlx
        xv = jnp.where(is_c, c_ref[...], x_ref[...])
        shift = jnp.where(is_c, mod_ref[1:2, 0:D], mod_ref[0:1, 0:D])
        scale = jnp.where(is_c, mod_ref[1:2, D:2 * D], mod_ref[0:1, D:2 * D])
        r = lax.rsqrt(jnp.mean(xv * xv, axis=1, keepdims=True) + EPS)
        hv = (xv * r) * nw_ref[...] * (1.0 + scale) + shift
        h_ref[...] = hv.astype(bf16)
        ht_ref[...] = jnp.transpose(hv).astype(bf16)

    return pl.pallas_call(
        kern, out_shape=(S((L + Lc, D), bf16), S((D, L + Lc), bf16)), grid=(nt,),
        in_specs=[pl.BlockSpec((RT, D), lambda i: (jnp.minimum(i, nlx - 1), 0)),
                  pl.BlockSpec((RT, D), lambda i: (jnp.maximum(i - nlx, 0), 0)),
                  _full((1, D)), _full((8, 3 * D))],
        out_specs=(pl.BlockSpec((RT, D), lambda i: (i, 0)), pl.BlockSpec((D, RT), lambda i: (0, i))),
        compiler_params=_params("parallel"), name="prenorm")(x, ctx, norm_w, mod)


def _prenorm_bwd(x, ctx, dh, dx1, norm_w, mod):
    L, Lc = x.shape[0], ctx.shape[0]
    nlx, nt = L // RT, (L + Lc) // RT

    def kern(x_ref, c_ref, dh_ref, dx1_ref, nw_ref, mod_ref, gx_ref, dnw_ref, acc_ref):
        i = pl.program_id(0)
        is_c = i >= nlx

        @pl.when(i == 0)
        def _():
            dnw_ref[...] = jnp.zeros_like(dnw_ref)
            acc_ref[...] = jnp.zeros_like(acc_ref)

        xv = jnp.where(is_c, c_ref[...], x_ref[...])
        scale = jnp.where(is_c, mod_ref[1:2, D:2 * D], mod_ref[0:1, D:2 * D])
        nw = nw_ref[...]
        r = lax.rsqrt(jnp.mean(xv * xv, axis=1, keepdims=True) + EPS)
        xn = xv * r
        dh = dh_ref[...]
        dsh = jnp.sum(dh, axis=0, keepdims=True)
        dsc = jnp.sum(dh * (xn * nw), axis=0, keepdims=True)
        dxnw = dh * (1.0 + scale)
        dnw_ref[...] += jnp.sum(dxnw * xn, axis=0, keepdims=True)
        dxn = dxnw * nw
        dx = r * (dxn - xn * jnp.mean(dxn * xn, axis=1, keepdims=True))

        @pl.when(jnp.logical_not(is_c))
        def _():
            gx_ref[...] = dx1_ref[...] + dx
            acc_ref[0:1, :] += dsh
            acc_ref[1:2, :] += dsc

        @pl.when(is_c)
        def _():
            acc_ref[2:3, :] += dsh
            acc_ref[3:4, :] += dsc

    xmap = lambda i: (jnp.minimum(i, nlx - 1), 0)
    return pl.pallas_call(
        kern, out_shape=(S((L, D), f32), S((1, D), f32), S((8, D), f32)), grid=(nt,),
        in_specs=[pl.BlockSpec((RT, D), xmap), pl.BlockSpec((RT, D), lambda i: (jnp.maximum(i - nlx, 0), 0)),
                  pl.BlockSpec((RT, D), lambda i: (i, 0)), pl.BlockSpec((RT, D), xmap), _full((1, D)), _full((8, 3 * D))],
        out_specs=(pl.BlockSpec((RT, D), xmap), _full((1, D)), _full((8, D))),
        compiler_params=_params("arbitrary"), name="prenorm_bwd")(x, ctx, dh, dx1, norm_w, mod)


def _xbc_col(j):
    return jnp.where(j == 0, PX0 // CONV_CT, PBC0 // CONV_CT)


def _halo_specs(nt_rows, ct, col=lambda j: j):
    cur = pl.BlockSpec((RT, ct), lambda i, j: (i, col(j)))
    prev = pl.BlockSpec((8, ct), lambda i, j: (jnp.maximum(i * (RT // 8) - 1, 0), col(j)))
    nxt = pl.BlockSpec((8, ct), lambda i, j: (jnp.minimum((i + 1) * (RT // 8), nt_rows // 8 - 1), col(j)))
    return cur, prev, nxt


def _fill_halo(scr, cur_ref, prev_ref, next_ref, i, nlx, nt):
    prev_ok = jnp.logical_and(i != 0, i != nlx)
    next_ok = jnp.logical_and(i != nlx - 1, i != nt - 1)
    scr[0:8, :] = jnp.where(prev_ok, prev_ref[...], 0.0)
    scr[8:8 + RT, :] = cur_ref[...]
    scr[8 + RT:16 + RT, :] = jnp.where(next_ok, next_ref[...], 0.0)


CONV_RB = 32


def _conv_blocks(ct):
    return [(slice(cb * 128, (cb + 1) * 128), r0) for cb in range(ct // 128) for r0 in range(0, RT, CONV_RB)]


def _taps(scr, cs, r0, shifts):
    blk = scr[r0:r0 + CONV_RB + 16, cs]
    n = CONV_RB + 16
    return [(blk if d == 0 else pltpu.roll(blk, (-d) % n, 0))[8:8 + CONV_RB, :] for d in shifts]


def _ssm_conv_fwd(proj, w8, b, nlx, half, out_dtype, name):
    T = proj.shape[0]
    nt = T // RT
    ct = CONV_CT
    cur, prev, nxt = _halo_specs(T, ct, lambda j: _xbc_col(j + half))

    def kern(cur_ref, prev_ref, next_ref, w_ref, b_ref, o_ref, scr):
        i = pl.program_id(0)
        _fill_halo(scr, cur_ref, prev_ref, next_ref, i, nlx, nt)
        for cs, r0 in _conv_blocks(ct):
            taps = _taps(scr, cs, r0, [k - 2 for k in range(SK)])
            acc = jnp.broadcast_to(b_ref[:, cs], (CONV_RB, 128))
            for k in range(SK):
                acc = acc + w_ref[k:k + 1, cs] * taps[k]
            o_ref[r0:r0 + CONV_RB, cs] = _silu(acc).astype(out_dtype)

    return pl.pallas_call(
        kern, out_shape=S((T, ct), out_dtype), grid=(nt, 1),
        in_specs=[cur, prev, nxt, pl.BlockSpec((8, ct), lambda i, j: (0, j + half)),
                  pl.BlockSpec((1, ct), lambda i, j: (0, j + half))],
        out_specs=pl.BlockSpec((RT, ct), lambda i, j: (i, j)),
        scratch_shapes=[pltpu.VMEM((RT + 16, ct), f32)],
        compiler_params=_params("parallel", "parallel"), name=name)(proj, proj, proj, w8, b)


def _ssm_conv_dpre(dxbc, proj, w8, b, nlx):
    T = proj.shape[0]
    nt = T // RT
    ct = CONV_CT
    cur = pl.BlockSpec((RT, ct), lambda j, i: (i, j))
    pcur = pl.BlockSpec((RT, ct), lambda j, i: (i, _xbc_col(j)))
    prev = pl.BlockSpec((8, ct), lambda j, i: (jnp.maximum(i * (RT // 8) - 1, 0), _xbc_col(j)))
    nxt = pl.BlockSpec((8, ct), lambda j, i: (jnp.minimum((i + 1) * (RT // 8), T // 8 - 1), _xbc_col(j)))

    def kern(d_ref, cur_ref, prev_ref, next_ref, w_ref, b_ref, dpre_ref, dw_ref, db_ref, scr):
        i = pl.program_id(1)
        _fill_halo(scr, cur_ref, prev_ref, next_ref, i, nlx, nt)

        @pl.when(i == 0)
        def _():
            dw_ref[...] = jnp.zeros_like(dw_ref)
            db_ref[...] = jnp.zeros_like(db_ref)

        for cb in range(ct // 128):
            cs = slice(cb * 128, (cb + 1) * 128)
            db_acc = jnp.zeros((CONV_RB, 128), f32)
            dw_acc = [jnp.zeros((CONV_RB, 128), f32) for _ in range(SK)]
            for r0 in range(0, RT, CONV_RB):
                taps = _taps(scr, cs, r0, [k - 2 for k in range(SK)])
                pre = jnp.broadcast_to(b_ref[:, cs], (CONV_RB, 128))
                for k in range(SK):
                    pre = pre + w_ref[k:k + 1, cs] * taps[k]
                dpre = d_ref[r0:r0 + CONV_RB, cs] * _dsilu(pre, _sig(pre))
                dpre_ref[r0:r0 + CONV_RB, cs] = dpre
                db_acc = db_acc + dpre
                dw_acc = [dw_acc[k] + dpre * taps[k] for k in range(SK)]
            db_ref[:, cs] += jnp.sum(db_acc, axis=0, keepdims=True)
            for k in range(SK):
                dw_ref[k:k + 1, cs] += jnp.sum(dw_acc[k], axis=0, keepdims=True)

    return pl.pallas_call(
        kern, out_shape=(S((T, 4096), f32), S((8, 4096), f32), S((1, 4096), f32)), grid=(4096 // ct, nt),
        in_specs=[cur, pcur, prev, nxt, pl.BlockSpec((8, ct), lambda j, i: (0, j)), pl.BlockSpec((1, ct), lambda j, i: (0, j))],
        out_specs=(cur, pl.BlockSpec((8, ct), lambda j, i: (0, j)), pl.BlockSpec((1, ct), lambda j, i: (0, j))),
        scratch_shapes=[pltpu.VMEM((RT + 16, ct), f32)],
        compiler_params=_params("parallel", "arbitrary"), name="ssm_conv_dpre")(dxbc, proj, proj, proj, w8, b)


def _ssm_conv_t(dpre, w8, dproj, nlx):
    T = dpre.shape[0]
    nt = T // RT
    ct = CONV_CT
    cur, prev, nxt = _halo_specs(T, ct)

    def kern(cur_ref, prev_ref, next_ref, w_ref, _alias, o_ref, scr):
        i = pl.program_id(0)
        _fill_halo(scr, cur_ref, prev_ref, next_ref, i, nlx, nt)
        for cs, r0 in _conv_blocks(ct):
            taps = _taps(scr, cs, r0, [2 - k for k in range(SK)])
            acc = jnp.zeros((CONV_RB, 128), f32)
            for k in range(SK):
                acc = acc + w_ref[k:k + 1, cs] * taps[k]
            o_ref[r0:r0 + CONV_RB, cs] = acc.astype(bf16)

    return pl.pallas_call(
        kern, out_shape=S(dproj.shape, bf16), grid=(nt, 4096 // ct),
        in_specs=[cur, prev, nxt, pl.BlockSpec((8, ct), lambda i, j: (0, j)), pl.BlockSpec(memory_space=pl.ANY)],
        out_specs=pl.BlockSpec((RT, ct), lambda i, j: (i, _xbc_col(j))),
        scratch_shapes=[pltpu.VMEM((RT + 16, ct), f32)], input_output_aliases={4: 0},
        compiler_params=_params("parallel", "parallel"), name="ssm_conv_t")(dpre, dpre, dpre, w8, dproj)


def _tri():
    li = lax.broadcasted_iota(jnp.int32, (Q, Q), 0)
    si = lax.broadcasted_iota(jnp.int32, (Q, Q), 1)
    return (si <= li).astype(bf16), (si >= li).astype(bf16)


def _dt_prep(proj, bias_row, alog_row):
    T = proj.shape[0]
    nch = T // Q

    def kern(raw_ref, b_ref, al_ref, dt_ref, la_ref):
        lane = lax.broadcasted_iota(jnp.int32, (Q, 128), 1)
        v = raw_ref[...] + b_ref[...]
        dt = jnp.maximum(v, 0.0) + jnp.log1p(jnp.exp(-jnp.abs(v)))
        a = jnp.where(lane[0:1, :] < 2 * NH, -jnp.exp(al_ref[...]), 0.0)
        da = dt * a
        tri, trit = _tri()
        dt_ref[...] = dt
        la_ref[...] = jnp.where(lane < NH, _dot3(tri, da), _dot3(trit, da))

    return pl.pallas_call(
        kern, out_shape=(S((T, 128), f32), S((T, 128), f32)), grid=(nch,),
        in_specs=[pl.BlockSpec((Q, 128), lambda c: (c, DT0 // 128)), _full((1, 128)), _full((1, 128))],
        out_specs=(pl.BlockSpec((Q, 128), lambda c: (c, 0)), pl.BlockSpec((Q, 128), lambda c: (c, 0))),
        compiler_params=_params("parallel"), name="dt_prep")(proj, bias_row, alog_row)


def _dt_bwd(a1, a2, r2, sv, dt, la, proj, bias_row, alog_row, dproj):
    T = proj.shape[0]
    nch = T // Q
    blk = pl.BlockSpec((Q, 128), lambda c: (c, 0))

    def kern(a1_ref, a2_ref, r2_ref, s_ref, dt_ref, la_ref, raw_ref, b_ref, al_ref, _alias, o_ref, db_ref, dal_ref):
        c = pl.program_id(0)

        @pl.when(c == 0)
        def _():
            db_ref[...] = jnp.zeros_like(db_ref)
            dal_ref[...] = jnp.zeros_like(dal_ref)

        lane = lax.broadcasted_iota(jnp.int32, (Q, 128), 1)
        row = lax.broadcasted_iota(jnp.int32, (Q, 128), 0)
        fwd = lane < NH
        dt = dt_ref[...]
        la = la_ref[...]
        a2v = a2_ref[...]
        r2v = r2_ref[...]
        a = jnp.where(lane[0:1, :] < 2 * NH, -jnp.exp(al_ref[...]), 0.0)
        la_e = jnp.where(fwd[0:1, :], la[Q - 1:Q, :], la[0:1, :])
        is_end = row == jnp.where(fwd, Q - 1, 0)
        e_end = jnp.exp(la_e - la)
        wend = e_end * dt
        extra = s_ref[0:1, :] * jnp.exp(la_e) + jnp.sum(wend * a2v, axis=0, keepdims=True)
        dla = a1_ref[...] - dt * r2v - wend * a2v + jnp.where(is_end, extra, 0.0)
        tri, trit = _tri()
        rcs = jnp.where(fwd, _dot3(trit, dla), _dot3(tri, dla))
        ddt = r2v + e_end * a2v + a * rcs
        dal_ref[...] += a * jnp.sum(dt * rcs, axis=0, keepdims=True)
        draw = jnp.where(lane < 2 * NH, ddt * _sig(raw_ref[...] + b_ref[...]), 0.0)
        db_ref[...] += jnp.sum(draw, axis=0, keepdims=True)
        o_ref[...] = jnp.zeros_like(o_ref)
        o_ref[:, 0:128] = draw.astype(bf16)

    return pl.pallas_call(
        kern, out_shape=(S(dproj.shape, bf16), S((1, 128), f32), S((1, 128), f32)), grid=(nch,),
        in_specs=[blk, blk, blk, blk, blk, blk, pl.BlockSpec((Q, 128), lambda c: (c, DT0 // 128)),
                  _full((1, 128)), _full((1, 128)), pl.BlockSpec(memory_space=pl.ANY)],
        out_specs=(pl.BlockSpec((Q, NP - DT0), lambda c: (c, DT0 // (NP - DT0))), _full((1, 128)), _full((1, 128))),
        input_output_aliases={9: 0},
        compiler_params=_params("arbitrary"), name="dt_bwd")(a1, a2, r2, sv, dt, la, proj, bias_row, alog_row, dproj)


def _split2(v):
    hi = v.astype(bf16)
    lo = (v - hi.astype(f32)).astype(bf16)
    return jnp.concatenate([hi, lo], axis=1)


def _scan_consts(rev):
    hoff = NH if rev else 0
    g = jnp.arange(NG, dtype=jnp.int32)[:, None, None]

    def rc(nr, ncol):
        return jnp.arange(nr, dtype=jnp.int32)[None, :, None], jnp.arange(ncol, dtype=jnp.int32)[None, None, :]

    r, c = rc(2 * 128, HPG * HD)
    sel_w = (lax.rem(r, 128) == hoff + HPG * g + c // HD).astype(bf16)
    r, c = rc(HPG * HD, 128)
    ind_h = (c == hoff + HPG * g + r // HD).astype(bf16)
    r, c = rc(2 * HPG * Q, 128)
    ind_e = (c == hoff + HPG * g + lax.rem(r, HPG * Q) // Q).astype(bf16)
    return sel_w, ind_h, ind_e


def _masks(rev):
    li = lax.broadcasted_iota(jnp.int32, (Q, Q), 0)
    si = lax.broadcasted_iota(jnp.int32, (Q, Q), 1)
    mask = (li <= si) if rev else (li >= si)
    mask_t = (li >= si) if rev else (li <= si)
    lane = lax.broadcasted_iota(jnp.int32, (Q, HPG * HD), 1)
    hms = [jnp.logical_and(lane >= r * HD, lane < (r + 1) * HD) for r in range(HPG)]
    return mask, mask_t, hms


def _mine(hoff):
    lane = lax.broadcasted_iota(jnp.int32, (Q, 128), 1)
    return jnp.logical_and(lane >= hoff, lane < hoff + NH)


def _head_row(vals, hc0):
    lane = lax.broadcasted_iota(jnp.int32, (1, HPG * HD), 1)
    out = jnp.zeros((1, HPG * HD), f32)
    for r in range(HPG):
        out = jnp.where(jnp.logical_and(lane >= r * HD, lane < (r + 1) * HD), vals[:, hc0 + r:hc0 + r + 1], out)
    return out


SCAN_CH = 2


def _chunk_of(j, rev, nxc, nch):
    return (nch - 1 - j) if rev else lax.rem(j + nxc, nch)


def _ssd_fwd(xs, bc, dt, la, consts, rev, nxc, name, y_acc=None):
    T = xs.shape[0]
    nch = T // Q
    hoff = NH if rev else 0
    e = 0 if rev else Q - 1
    cm = lambda j: _chunk_of(j, rev, nxc // SCAN_CH, nch // SCAN_CH)
    sel_w = consts[0]
    has_acc = y_acc is not None

    def kern(*refs):
        xs_ref, bc_ref, dt_ref, la_ref, sw_ref = refs[:5]
        yacc_ref = refs[5] if has_acc else None
        y_ref, hp_ref, h_ref = refs[5 + has_acc:]
        j = pl.program_id(0)

        @pl.when(j == 0)
        def _():
            h_ref[...] = jnp.zeros_like(h_ref)

        mask, _, hms = _masks(rev)
        for hh in range(SCAN_CH):
            h = SCAN_CH - 1 - hh if rev else hh
            chunk(refs, mask, hms, h, slice(h * Q, (h + 1) * Q))

    def chunk(refs, mask, hms, h, rows):
        xs_ref, bc_ref, dt_ref, la_ref, sw_ref = refs[:5]
        yacc_ref = refs[5] if has_acc else None
        y_ref, hp_ref, h_ref = refs[5 + has_acc:]
        hp_ref[h] = h_ref[...]
        la_all = la_ref[rows, :]
        dt_all = dt_ref[rows, :]
        la_t = jnp.transpose(la_all)
        dt_t = jnp.transpose(dt_all)
        la_e = la_all[e:e + 1, :]
        w2 = _split2(jnp.exp(jnp.where(_mine(hoff), la_e - la_all, 0.0)) * dt_all)
        e2 = _split2(jnp.exp(la_all))
        ela_e = jnp.exp(la_e)
        for g in range(NG):
            hc0 = hoff + g * HPG
            x = xs_ref[rows, g * GW:(g + 1) * GW]
            bb = bc_ref[rows, g * NS:(g + 1) * NS]
            cb = bc_ref[rows, NG * NS + g * NS:NG * NS + (g + 1) * NS]
            ht = h_ref[g * NS:(g + 1) * NS, :]
            scores = _dot_nt(cb, bb)
            yoff = _dot(cb, ht.astype(bf16))
            wend = _dot(w2, sw_ref[g])
            expla = _dot(e2, sw_ref[g])
            mixes, xstack = [], []
            for r in range(HPG):
                hc = hc0 + r
                la_rep = jnp.broadcast_to(la_all[:, hc:hc + 1], (Q, 128))
                decay = jnp.exp(jnp.where(mask, la_rep - la_t[hc:hc + 1, :], NEG))
                mixes.append((scores * decay * dt_t[hc:hc + 1, :]).astype(bf16))
                xstack.append(jnp.where(hms[r], x, 0.0).astype(bf16))
            y = _dot(jnp.concatenate(mixes, axis=1), jnp.concatenate(xstack, axis=0)) + yoff * expla
            if has_acc:
                y = y + yacc_ref[rows, g * GW:(g + 1) * GW]
            y_ref[rows, g * GW:(g + 1) * GW] = y
            h_ref[g * NS:(g + 1) * NS, :] = ht * _head_row(ela_e, hc0) + _dot_tn(bb, (x * wend).astype(bf16))

    row = lambda j: (cm(j), 0)
    rq = SCAN_CH * Q
    yblk = pl.BlockSpec((rq, DI), row)
    return pl.pallas_call(
        kern, out_shape=(S((T, DI), f32), S((nch, NG * NS, HPG * HD), f32)), grid=(nch // SCAN_CH,),
        in_specs=[yblk, pl.BlockSpec((rq, 2 * NG * NS), row), pl.BlockSpec((rq, 128), row), pl.BlockSpec((rq, 128), row),
                  _full(sel_w.shape)] + ([yblk] if has_acc else []),
        out_specs=(yblk, pl.BlockSpec((SCAN_CH, NG * NS, HPG * HD), lambda j: (cm(j), 0, 0))),
        scratch_shapes=[pltpu.VMEM((NG * NS, HPG * HD), f32)],
        input_output_aliases={5: 0} if has_acc else {},
        compiler_params=_params("arbitrary"), name=name)(xs, bc, dt, la, sel_w, *([y_acc] if has_acc else []))


def _ssd_bwd(xs, bc, dy, dt, la, hprev, dskip_full, consts, rev, nxc, name, acc=None):
    T = xs.shape[0]
    nch = T // Q
    hoff = NH if rev else 0
    e = 0 if rev else Q - 1
    npair = nch // SCAN_CH
    cm = lambda j: _chunk_of(npair - 1 - j, rev, nxc // SCAN_CH, npair)
    has_acc = acc is not None
    sel_w, ind_h, ind_e = consts

    def kern(*refs):
        g_ref = refs[-2]
        j = pl.program_id(0)

        @pl.when(j == 0)
        def _():
            g_ref[...] = jnp.zeros_like(g_ref)

        masks = _masks(rev)
        for hh in range(SCAN_CH):
            h = hh if rev else SCAN_CH - 1 - hh
            chunk(refs, masks, h, slice(h * Q, (h + 1) * Q))

    def chunk(refs, masks, h, rows):
        xs_ref, bc_ref, dy_ref, dt_ref, la_ref, hp_ref, dsk_ref, sw_ref, ih_ref, ie_ref = refs[:10]
        k = 10
        if has_acc:
            dxbc_in, a1_in, a2_in, r2_in, s_in = refs[k:k + 5]
            k += 5
        dxbc_ref, a1_ref, a2_ref, r2_ref, s_ref, g_ref, r2scr = refs[k:k + 7]
        mask, mask_t, hms = masks
        la_all = la_ref[rows, :]
        dt_all = dt_ref[rows, :]
        la_t = jnp.transpose(la_all)
        dt_t = jnp.transpose(dt_all)
        la_e = la_all[e:e + 1, :]
        w2 = _split2(jnp.exp(jnp.where(_mine(hoff), la_e - la_all, 0.0)) * dt_all)
        e2 = _split2(jnp.exp(la_all))
        wed2 = jnp.concatenate([w2, e2, _split2(dt_all)], axis=0)
        ela_e = jnp.exp(la_e)
        r2scr[...] = jnp.zeros_like(r2scr)
        a1acc = jnp.zeros((Q, 128), f32)
        a2acc = jnp.zeros((Q, 128), f32)
        sacc = jnp.zeros((1, 128), f32)
        for g in range(NG):
            hc0 = hoff + g * HPG
            x = xs_ref[rows, g * GW:(g + 1) * GW]
            bb = bc_ref[rows, g * NS:(g + 1) * NS]
            cb = bc_ref[rows, NG * NS + g * NS:NG * NS + (g + 1) * NS]
            dyv = dy_ref[rows, g * GW:(g + 1) * GW]
            gt = g_ref[g * NS:(g + 1) * NS, :]
            ht = hp_ref[h, g * NS:(g + 1) * NS, :]
            gtb = gt.astype(bf16)
            htb = ht.astype(bf16)
            xb = x.astype(bf16)
            scores = _dot_nt(cb, bb)
            scores_t = _dot_nt(bb, cb)
            bg = _dot(bb, gtb)
            yoff = _dot(cb, htb)
            sel3 = _dot(wed2, sw_ref[g])
            wend, expla, dtf = sel3[0:Q], sel3[Q:2 * Q], sel3[2 * Q:3 * Q]
            dym = jnp.concatenate([jnp.where(hms[r], dyv, 0.0).astype(bf16) for r in range(HPG)], axis=0)
            dyx_all = _dot_nt(dym, xb)
            sdts, ems = [], []
            wsum = jnp.zeros((Q, Q), f32)
            for r in range(HPG):
                hc = hc0 + r
                la_rep = jnp.broadcast_to(la_all[:, hc:hc + 1], (Q, 128))
                la_r = la_t[hc:hc + 1, :]
                dt_r = dt_t[hc:hc + 1, :]
                decay = jnp.exp(jnp.where(mask, la_rep - la_r, NEG))
                decay_t = jnp.exp(jnp.where(mask_t, la_r - la_rep, NEG))
                dyx = dyx_all[r * Q:(r + 1) * Q, :]
                fm = dyx * (scores * decay)
                r2scr[hc:hc + 1, :] = jnp.sum(fm, axis=0, keepdims=True)
                ems.append(fm * dt_r)
                wsum = wsum + dyx * decay * dt_r
                sdts.append((scores_t * decay_t).astype(bf16))
            dx = dtf * _dot(jnp.concatenate(sdts, axis=1), dym) + wend * bg
            if not has_acc:
                dx = dx + dsk_ref[:, g * GW:(g + 1) * GW] * dyv
            red3 = _dot(jnp.concatenate([(dyv * yoff * expla).astype(bf16), (x * bg).astype(bf16), (gt * ht).astype(bf16)],
                                        axis=0), ih_ref[g])
            a1acc = a1acc + _dot(_split2(jnp.concatenate(ems, axis=1)), ie_ref[g]) + red3[0:Q]
            a2acc = a2acc + red3[Q:2 * Q]
            sacc = sacc + jnp.sum(red3[2 * Q:3 * Q], axis=0, keepdims=True)
            wb = wsum.astype(bf16)
            dysb = (dyv * expla).astype(bf16)
            dc = _dot(wb, bb) + _dot_nt(dysb, htb)
            db = _dot_tn(wb, cb) + _dot_nt((x * wend).astype(bf16), gtb)
            g_ref[g * NS:(g + 1) * NS, :] = gt * _head_row(ela_e, hc0) + _dot_tn(cb, dysb)
            if has_acc:
                dx = dx + dxbc_in[rows, g * GW:(g + 1) * GW]
                db = db + dxbc_in[rows, B0 + g * NS:B0 + (g + 1) * NS]
                dc = dc + dxbc_in[rows, C0 + g * NS:C0 + (g + 1) * NS]
            dxbc_ref[rows, g * GW:(g + 1) * GW] = dx
            dxbc_ref[rows, B0 + g * NS:B0 + (g + 1) * NS] = db
            dxbc_ref[rows, C0 + g * NS:C0 + (g + 1) * NS] = dc
        r2c = jnp.transpose(r2scr[...])
        sc = jnp.broadcast_to(sacc, (Q, 128))
        if has_acc:
            a1acc = a1acc + a1_in[rows, :]
            a2acc = a2acc + a2_in[rows, :]
            r2c = r2c + r2_in[rows, :]
            sc = sc + s_in[rows, :]
        a1_ref[rows, :] = a1acc
        a2_ref[rows, :] = a2acc
        r2_ref[rows, :] = r2c
        s_ref[rows, :] = sc

    rq = SCAN_CH * Q
    blk = pl.BlockSpec((rq, 128), lambda j: (cm(j), 0))
    big = pl.BlockSpec((rq, 4096), lambda j: (cm(j), 0))
    wide = pl.BlockSpec((rq, DI), lambda j: (cm(j), 0))
    in_specs = [wide, pl.BlockSpec((rq, 2 * NG * NS), lambda j: (cm(j), 0)), wide, blk, blk,
                pl.BlockSpec((SCAN_CH, NG * NS, HPG * HD), lambda j: (cm(j), 0, 0)), _full((1, DI)),
                _full(sel_w.shape), _full(ind_h.shape), _full(ind_e.shape)]
    args = [xs, bc, dy, dt, la, hprev, dskip_full, sel_w, ind_h, ind_e]
    aliases = {}
    if has_acc:
        in_specs += [big, blk, blk, blk, blk]
        args += list(acc)
        aliases = {10: 0, 11: 1, 12: 2, 13: 3, 14: 4}
    return pl.pallas_call(
        kern, out_shape=(S((T, 4096), f32), S((T, 128), f32), S((T, 128), f32), S((T, 128), f32), S((T, 128), f32)),
        grid=(npair,), in_specs=in_specs, out_specs=(big, blk, blk, blk, blk),
        scratch_shapes=[pltpu.VMEM((NG * NS, HPG * HD), f32), pltpu.VMEM((128, Q), f32)],
        input_output_aliases=aliases,
        compiler_params=_params("arbitrary"), name=name)(*args)


def _ynorm_fwd(ysum, xs, proj, dskip_full, nw, L):
    nlx = L // RT

    def kern(ys_ref, xs_ref, za_ref, zb_ref, dsk_ref, nw_ref, y_ref, yn_ref, ynt_ref):
        y = ys_ref[...] + dsk_ref[...] * xs_ref[...]
        y_ref[...] = y
        hg = NG // 2
        for g in range(NG):
            z_ref = za_ref if g < hg else zb_ref
            sl = y[:, g * GW:(g + 1) * GW] * _silu(z_ref[:, (g % hg) * GW:(g % hg + 1) * GW])
            r = lax.rsqrt(jnp.mean(sl * sl, axis=1, keepdims=True) + EPS)
            yn = (sl * r) * nw_ref[:, g * GW:(g + 1) * GW]
            yn_ref[:, g * GW:(g + 1) * GW] = yn.astype(bf16)
            ynt_ref[g * GW:(g + 1) * GW, :] = jnp.transpose(yn).astype(bf16)

    blk = pl.BlockSpec((RT, DI), lambda i: (i, 0))
    return pl.pallas_call(
        kern, out_shape=(S((L, DI), f32), S((L, DI), bf16), S((DI, L), bf16)), grid=(nlx,),
        in_specs=[blk, blk, pl.BlockSpec((RT, DI // 2), lambda i: (i, Z0 // (DI // 2))),
                  pl.BlockSpec((RT, DI // 2), lambda i: (i, Z0 // (DI // 2) + 1)), _full((1, DI)), _full((1, DI))],
        out_specs=(blk, blk, pl.BlockSpec((DI, RT), lambda i: (0, i))),
        compiler_params=_params("parallel"), name="ynorm_fwd")(ysum, xs, proj, proj, dskip_full, nw)


def _ynorm_bwd(dyn, y, xs, proj, dskip_full, nw, dproj):
    L = y.shape[0]
    T = proj.shape[0]
    nlx, nt = L // RT, T // RT

    hw = DI // 2

    def kern(dyn_ref, y_ref, xs_ref, z_ref, dsk_ref, nw_ref, _alias, dz_ref, dy_ref, dnw_ref, dsk_acc):
        i = pl.program_id(1)

        @pl.when(i == 0)
        def _():
            dnw_ref[...] = jnp.zeros_like(dnw_ref)
            dsk_acc[...] = jnp.zeros_like(dsk_acc)

        @pl.when(i >= nlx)
        def _():
            dz_ref[...] = jnp.zeros_like(dz_ref)
            dy_ref[...] = jnp.zeros_like(dy_ref)

        @pl.when(i < nlx)
        def _():
            y = y_ref[...]
            z = z_ref[...]
            sz = _sig(z)
            gz = z * sz
            yz = y * gz
            dynv = dyn_ref[...]
            for g in range(hw // GW):
                cs = slice(g * GW, (g + 1) * GW)
                sl = yz[:, cs]
                r = lax.rsqrt(jnp.mean(sl * sl, axis=1, keepdims=True) + EPS)
                yhat = sl * r
                dn = dynv[:, cs]
                dnw_ref[:, cs] += jnp.sum(dn * yhat, axis=0, keepdims=True)
                dyh = dn * nw_ref[:, cs]
                dyz = r * (dyh - yhat * jnp.mean(dyh * yhat, axis=1, keepdims=True))
                dyv = dyz * gz[:, cs]
                dy_ref[:, cs] = dyv
                dz_ref[:, cs] = (dyz * y[:, cs] * _dsilu(z[:, cs], sz[:, cs])).astype(bf16)
                dsk_acc[:, cs] += jnp.sum(dyv * xs_ref[:, cs], axis=0, keepdims=True)

    xblk = pl.BlockSpec((RT, hw), lambda j, i: (jnp.minimum(i, nlx - 1), j))
    row = pl.BlockSpec((1, hw), lambda j, i: (0, j))
    return pl.pallas_call(
        kern, out_shape=(S(dproj.shape, bf16), S((T, DI), f32), S((1, DI), f32), S((1, DI), f32)), grid=(2, nt),
        in_specs=[xblk, xblk, xblk, pl.BlockSpec((RT, hw), lambda j, i: (jnp.minimum(i, nlx - 1), Z0 // hw + j)), row, row,
                  pl.BlockSpec(memory_space=pl.ANY)],
        out_specs=(pl.BlockSpec((RT, hw), lambda j, i: (i, Z0 // hw + j)), pl.BlockSpec((RT, hw), lambda j, i: (i, j)), row, row),
        input_output_aliases={6: 0},
        compiler_params=_params("arbitrary", "arbitrary"), name="ynorm_bwd")(dyn, y, xs, proj, dskip_full, nw, dproj)


def _head_sums(cols):
    def kern(c_ref, o_ref):
        o_ref[...] = jnp.broadcast_to(jnp.sum(c_ref[...], axis=1, keepdims=True), (NH, 128))

    return pl.pallas_call(kern, out_shape=S((NH, 128), f32), name="head_sums")(cols)


SEG_STRIDE = 96
SEG_PAD = 16
NSEG = RT // GRID_W
CONF_ROWS = SEG_PAD + NSEG * SEG_STRIDE


SHIFT_ROWS = CONF_ROWS - 8
CONF_CW = 256


CONF_RB = 32


def _seg_zero_pads(scr):
    scr[0:SEG_PAD, :] = jnp.zeros((SEG_PAD, scr.shape[1]), f32)
    for s in range(NSEG):
        lo = SEG_PAD + s * SEG_STRIDE + GRID_W
        scr[lo:lo + SEG_STRIDE - GRID_W, :] = jnp.zeros((SEG_STRIDE - GRID_W, scr.shape[1]), f32)


def _seg_row(r0):
    return SEG_PAD + (r0 // GRID_W) * SEG_STRIDE + r0 % GRID_W


def _shift_copies(cps, scr, cs):
    full = scr[:, cs]
    for s in range(1, 8):
        cps[s - 1, :, :] = pltpu.roll(full, CONF_ROWS - s, 0)[0:SHIFT_ROWS, :]


def _tap(cps, scr, cs, o):
    rs = o % 8
    return scr[pl.ds(o, GRID_W), cs] if rs == 0 else cps[rs - 1, pl.ds(o - rs, GRID_W), :]


def _conf_fwd(proj, w32, cb, lnw, lnb, L):
    nlx = L // RT

    def kern(v_ref, g_ref, cg_ref, w_ref, cb_ref, lnw_ref, lnb_ref, u1_ref, u3_ref, u3t_ref, scr, cps, u3_scr):
        _seg_zero_pads(scr)
        for r0 in range(0, RT, CONF_RB):
            rows = slice(r0, r0 + CONF_RB)
            scr[_seg_row(r0):_seg_row(r0) + CONF_RB, :] = v_ref[rows, :] * _sig(g_ref[rows, :])
        for cc in range(D // CONF_CW):
            cs = slice(cc * CONF_CW, (cc + 1) * CONF_CW)
            _shift_copies(cps, scr, cs)
            for s in range(NSEG):
                acc = jnp.broadcast_to(cb_ref[:, cs], (GRID_W, CONF_CW))
                for k in range(CK):
                    acc = acc + w_ref[k:k + 1, cs] * _tap(cps, scr, cs, SEG_PAD + s * SEG_STRIDE + k - CK // 2)
                u1_ref[s * GRID_W:(s + 1) * GRID_W, cs] = acc
        for r0 in range(0, RT, CONF_RB):
            rows = slice(r0, r0 + CONF_RB)
            u1 = u1_ref[rows, :]
            xc = u1 - jnp.mean(u1, axis=1, keepdims=True)
            r = lax.rsqrt(jnp.mean(xc * xc, axis=1, keepdims=True) + EPS)
            u2 = (xc * r) * lnw_ref[...] + lnb_ref[...]
            u3 = _silu(u2) * _silu(cg_ref[rows, :])
            u3_ref[rows, :] = u3.astype(bf16)
            u3_scr[rows, :] = u3
        u3t_ref[...] = jnp.transpose(u3_scr[...]).astype(bf16)

    blk = pl.BlockSpec((RT, D), lambda i: (i, 0))
    return pl.pallas_call(
        kern, out_shape=(S((L, D), f32), S((L, D), bf16), S((D, L), bf16)), grid=(nlx,),
        in_specs=[pl.BlockSpec((RT, D), lambda i: (i, GV0 // D)), pl.BlockSpec((RT, D), lambda i: (i, GG0 // D)),
                  pl.BlockSpec((RT, D), lambda i: (i, CG0 // D)), _full((32, D)), _full((1, D)), _full((1, D)), _full((1, D))],
        out_specs=(blk, blk, pl.BlockSpec((D, RT), lambda i: (0, i))),
        scratch_shapes=[pltpu.VMEM((CONF_ROWS, D), f32), pltpu.VMEM((7, SHIFT_ROWS, CONF_CW), f32), pltpu.VMEM((RT, D), f32)],
        compiler_params=_params("parallel"), name="conf_fwd")(proj, proj, proj, w32, cb, lnw, lnb)


def _conf_bwd(du3, u1, proj, w32, lnw, lnb, dproj):
    L = u1.shape[0]
    T = proj.shape[0]
    nlx, nt = L // RT, T // RT

    def kern(du3_ref, u1_ref, v_ref, g_ref, cg_ref, w_ref, lnw_ref, lnb_ref, _alias,
             o_ref, dw_ref, dcb_ref, dlw_ref, dlb_ref, scr_u, scr_d, du0_scr, cps_u, cps_d):
        i = pl.program_id(0)

        @pl.when(i == 0)
        def _():
            dw_ref[...] = jnp.zeros_like(dw_ref)
            dcb_ref[...] = jnp.zeros_like(dcb_ref)
            dlw_ref[...] = jnp.zeros_like(dlw_ref)
            dlb_ref[...] = jnp.zeros_like(dlb_ref)

        @pl.when(i >= nlx)
        def _():
            o_ref[...] = jnp.zeros_like(o_ref)

        @pl.when(i < nlx)
        def _():
            _seg_zero_pads(scr_u)
            _seg_zero_pads(scr_d)
            for r0 in range(0, RT, CONF_RB):
                rows = slice(r0, r0 + CONF_RB)
                cg = cg_ref[rows, :]
                scg = _sig(cg)
                u1 = u1_ref[rows, :]
                xc = u1 - jnp.mean(u1, axis=1, keepdims=True)
                r = lax.rsqrt(jnp.mean(xc * xc, axis=1, keepdims=True) + EPS)
                xhat = xc * r
                u2 = xhat * lnw_ref[...] + lnb_ref[...]
                s2 = _sig(u2)
                du3v = du3_ref[rows, :]
                du2 = du3v * (cg * scg) * _dsilu(u2, s2)
                o_ref[rows, 2 * D:3 * D] = (du3v * (u2 * s2) * _dsilu(cg, scg)).astype(bf16)
                dlw_ref[...] += jnp.sum(du2 * xhat, axis=0, keepdims=True)
                dlb_ref[...] += jnp.sum(du2, axis=0, keepdims=True)
                dxh = du2 * lnw_ref[...]
                du1 = r * (dxh - jnp.mean(dxh, axis=1, keepdims=True) - xhat * jnp.mean(dxh * xhat, axis=1, keepdims=True))
                dcb_ref[...] += jnp.sum(du1, axis=0, keepdims=True)
                scr_u[_seg_row(r0):_seg_row(r0) + CONF_RB, :] = v_ref[rows, :] * _sig(g_ref[rows, :])
                scr_d[_seg_row(r0):_seg_row(r0) + CONF_RB, :] = du1
            for cc in range(D // CONF_CW):
                cs = slice(cc * CONF_CW, (cc + 1) * CONF_CW)
                _shift_copies(cps_u, scr_u, cs)
                _shift_copies(cps_d, scr_d, cs)
                for k in range(CK):
                    t = jnp.zeros((GRID_W, CONF_CW), f32)
                    for s in range(NSEG):
                        base = SEG_PAD + s * SEG_STRIDE
                        t = t + scr_d[pl.ds(base, GRID_W), cs] * _tap(cps_u, scr_u, cs, base + k - CK // 2)
                    dw_ref[k:k + 1, cs] += jnp.sum(t, axis=0, keepdims=True)
                for s in range(NSEG):
                    base = SEG_PAD + s * SEG_STRIDE
                    acc = jnp.zeros((GRID_W, CONF_CW), f32)
                    for k in range(CK):
                        acc = acc + w_ref[k:k + 1, cs] * _tap(cps_d, scr_d, cs, base + CK // 2 - k)
                    du0_scr[s * GRID_W:(s + 1) * GRID_W, cs] = acc
            for r0 in range(0, RT, CONF_RB):
                rows = slice(r0, r0 + CONF_RB)
                du0 = du0_scr[rows, :]
                sg = _sig(g_ref[rows, :])
                o_ref[rows, 0:D] = (du0 * sg).astype(bf16)
                o_ref[rows, D:2 * D] = (du0 * v_ref[rows, :] * sg * (1.0 - sg)).astype(bf16)

    xmap = lambda i: (jnp.minimum(i, nlx - 1), 0)
    pmap = lambda cb: (lambda i: (jnp.minimum(i, nlx - 1), cb))
    return pl.pallas_call(
        kern, out_shape=(S(dproj.shape, bf16), S((32, D), f32), S((1, D), f32), S((1, D), f32), S((1, D), f32)), grid=(nt,),
        in_specs=[pl.BlockSpec((RT, D), xmap), pl.BlockSpec((RT, D), xmap),
                  pl.BlockSpec((RT, D), pmap(GV0 // D)), pl.BlockSpec((RT, D), pmap(GG0 // D)), pl.BlockSpec((RT, D), pmap(CG0 // D)),
                  _full((32, D)), _full((1, D)), _full((1, D)), pl.BlockSpec(memory_space=pl.ANY)],
        out_specs=(pl.BlockSpec((RT, 3 * D), lambda i: (i, GV0 // (3 * D))), _full((32, D)), _full((1, D)), _full((1, D)), _full((1, D))),
        scratch_shapes=[pltpu.VMEM((CONF_ROWS, D), f32), pltpu.VMEM((CONF_ROWS, D), f32), pltpu.VMEM((RT, D), f32),
                        pltpu.VMEM((7, SHIFT_ROWS, CONF_CW), f32), pltpu.VMEM((7, SHIFT_ROWS, CONF_CW), f32)],
        input_output_aliases={8: 0},
        compiler_params=_params("arbitrary"), name="conf_bwd")(du3, u1, proj, proj, proj, w32, lnw, lnb, dproj)


def _merge_fwd(bs, bc, proj):
    L = bs.shape[0]

    def kern(bs_ref, bc_ref, g1_ref, g2_ref, o_ref, ot_ref):
        mv = _sig(g1_ref[...]) * bs_ref[...] + _sig(g2_ref[...]) * bc_ref[...]
        o_ref[...] = mv.astype(bf16)
        ot_ref[...] = jnp.transpose(mv).astype(bf16)

    blk = pl.BlockSpec((RT, D), lambda i: (i, 0))
    return pl.pallas_call(
        kern, out_shape=(S((L, D), bf16), S((D, L), bf16)), grid=(L // RT,),
        in_specs=[blk, blk, pl.BlockSpec((RT, D), lambda i: (i, G10 // D)), pl.BlockSpec((RT, D), lambda i: (i, G20 // D))],
        out_specs=(blk, pl.BlockSpec((D, RT), lambda i: (0, i))),
        compiler_params=_params("parallel"), name="merge_fwd")(bs, bc, proj, proj)


def _merge_bwd(dmerged, bs, bc, proj):
    L = bs.shape[0]
    T = proj.shape[0]
    nlx, nt = L // RT, T // RT

    def kern(dm_ref, bs_ref, bc_ref, g1_ref, g2_ref, o_ref, dbs_ref, dbc_ref):
        i = pl.program_id(0)

        @pl.when(i >= nlx)
        def _():
            o_ref[...] = jnp.zeros_like(o_ref)

        @pl.when(i < nlx)
        def _():
            dm = dm_ref[...]
            s1 = _sig(g1_ref[...])
            s2 = _sig(g2_ref[...])
            dbs_ref[...] = (dm * s1).astype(bf16)
            dbc_ref[...] = (dm * s2).astype(bf16)
            o_ref[:, 0:D] = (dm * bs_ref[...] * s1 * (1.0 - s1)).astype(bf16)
            o_ref[:, D:2 * D] = (dm * bc_ref[...] * s2 * (1.0 - s2)).astype(bf16)

    xmap = lambda i: (jnp.minimum(i, nlx - 1), 0)
    pmap = lambda cb: (lambda i: (jnp.minimum(i, nlx - 1), cb))
    xblk = pl.BlockSpec((RT, D), xmap)
    return pl.pallas_call(
        kern, out_shape=(S((T, NP), bf16), S((L, D), bf16), S((L, D), bf16)), grid=(nt,),
        in_specs=[xblk, xblk, xblk, pl.BlockSpec((RT, D), pmap(G10 // D)), pl.BlockSpec((RT, D), pmap(G20 // D))],
        out_specs=(pl.BlockSpec((RT, 2 * D), lambda i: (i, G10 // (2 * D))), xblk, xblk),
        compiler_params=_params("arbitrary"), name="merge_bwd")(dmerged, bs, bc, proj, proj)


def _final(x, out, target, mod, fw):
    L = x.shape[0]

    def kern(x_ref, o_ref, t_ref, mod_ref, fw_ref, dx1_ref, dout_ref, loss_ref, dfw_ref, dg_ref):
        i = pl.program_id(0)

        @pl.when(i == 0)
        def _():
            loss_ref[...] = jnp.zeros_like(loss_ref)
            dfw_ref[...] = jnp.zeros_like(dfw_ref)
            dg_ref[...] = jnp.zeros_like(dg_ref)

        gate = mod_ref[0:1, 2 * D:3 * D]
        ov = o_ref[...]
        x1 = x_ref[...] + gate * ov
        r = lax.rsqrt(jnp.mean(x1 * x1, axis=1, keepdims=True) + EPS)
        xn = x1 * r
        fw = fw_ref[...]
        err = xn * fw - t_ref[...]
        part = 0.5 * jnp.sum(jnp.mean(err * err, axis=1, keepdims=True), axis=0, keepdims=True)
        loss_ref[...] += jnp.broadcast_to(part, (8, 128))
        dy = err * (1.0 / D)
        dfw_ref[...] += jnp.sum(dy * xn, axis=0, keepdims=True)
        dyw = dy * fw
        dx1 = r * (dyw - xn * jnp.mean(dyw * xn, axis=1, keepdims=True))
        dx1_ref[...] = dx1
        dout_ref[...] = (gate * dx1).astype(bf16)
        dg_ref[...] += jnp.sum(dx1 * ov, axis=0, keepdims=True)

    blk = pl.BlockSpec((RT, D), lambda i: (i, 0))
    return pl.pallas_call(
        kern, out_shape=(S((L, D), f32), S((L, D), bf16), S((8, 128), f32), S((1, D), f32), S((1, D), f32)), grid=(L // RT,),
        in_specs=[blk, blk, blk, _full((8, 3 * D)), _full((1, D))],
        out_specs=(blk, blk, _full((8, 128)), _full((1, D)), _full((1, D))),
        compiler_params=_params("arbitrary"), name="final")(x, out, target, mod, fw)


def _me():
    return 4 * lax.axis_index("x") + 2 * lax.axis_index("y") + lax.axis_index("c")


def _xchg_copy(ins, outs, send_sems, recv_sems, modes, a, k, me):
    peer = lax.rem(me + k, N_DEV)
    pid = (peer // 4, lax.rem(peer // 2, 2), lax.rem(peer, 2))
    src = ins[a].at[peer] if modes[a] else ins[a]
    return pltpu.make_async_remote_copy(src_ref=src, dst_ref=outs[a].at[me], send_sem=send_sems.at[a, k - 1],
                                        recv_sem=recv_sems.at[a, k - 1], device_id=pid, device_id_type=MESH)


def _xchg_local(ins, outs, loc_sems, modes, a, me):
    return pltpu.make_async_copy(ins[a].at[me] if modes[a] else ins[a], outs[a].at[me], loc_sems.at[a])


def _xchg_start(ins, outs, send_sems, recv_sems, loc_sems, modes):
    me = _me()
    for a in range(len(modes)):
        _xchg_local(ins, outs, loc_sems, modes, a, me).start()
        for k in range(1, N_DEV):
            _xchg_copy(ins, outs, send_sems, recv_sems, modes, a, k, me).start()


def _xchg_wait(ins, outs, send_sems, recv_sems, loc_sems, modes):
    me = _me()
    for a in range(len(modes)):
        for k in range(1, N_DEV):
            frm = lax.rem(me + N_DEV - k, N_DEV)
            src = ins[a].at[frm] if modes[a] else ins[a]
            pltpu.make_async_remote_copy(src_ref=src, dst_ref=outs[a].at[frm], send_sem=send_sems.at[a, k - 1],
                                         recv_sem=recv_sems.at[a, k - 1], device_id=(0, 0, 0), device_id_type=MESH).wait_recv()
    for a in range(len(modes)):
        for k in range(1, N_DEV):
            _xchg_copy(ins, outs, send_sems, recv_sems, modes, a, k, me).wait_send()
        _xchg_local(ins, outs, loc_sems, modes, a, me).wait()


def _xchg_out_shapes(arrs, modes):
    return tuple(S((N_DEV,) + (a.shape[1:] if sc else a.shape), a.dtype) for a, sc in zip(arrs, modes))


def _xchg_sems(n):
    return [pltpu.SemaphoreType.DMA((n, N_DEV - 1)), pltpu.SemaphoreType.DMA((n, N_DEV - 1)), pltpu.SemaphoreType.DMA((n,))]


def _exchange(arrs, modes, name):
    n = len(arrs)

    def kern(*refs):
        ins, outs, sems = refs[:n], refs[n:2 * n], refs[2 * n:]
        _xchg_start(ins, outs, *sems, modes)
        _xchg_wait(ins, outs, *sems, modes)

    anyspec = pl.BlockSpec(memory_space=pl.ANY)
    return pl.pallas_call(
        kern, out_shape=_xchg_out_shapes(arrs, modes), in_specs=[anyspec] * n, out_specs=tuple([anyspec] * n),
        scratch_shapes=_xchg_sems(n), name=name)(*arrs)


def _gather2(arrs, name):
    n = len(arrs)

    def kern(*refs):
        ins, outs = refs[:n], refs[n:2 * n]
        send_sems, recv_sems, loc_sems = refs[2 * n:]
        x, y, c = lax.axis_index("x"), lax.axis_index("y"), lax.axis_index("c")
        me, sib = (x, y, c), (x, y, 1 - c)
        chips = [(1 - x, y), (x, 1 - y), (1 - x, 1 - y)]

        def slot(a, p):
            return outs[a].at[4 * p[0] + 2 * p[1] + p[2]]

        def cp(a, k, block, to, own=False):
            return pltpu.make_async_remote_copy(src_ref=ins[a] if own else slot(a, block), dst_ref=slot(a, block),
                                                send_sem=send_sems.at[a, k], recv_sem=recv_sems.at[a, k],
                                                device_id=to, device_id_type=MESH)

        started = []
        for a in range(n):
            pltpu.make_async_copy(ins[a], slot(a, me), loc_sems.at[a]).start()
            started.append(cp(a, 0, me, sib, own=True))
            started += [cp(a, 1 + j, me, (*chips[j], c), own=True) for j in range(2)]
        for s in started:
            s.start()
        for j in range(2):
            for a in range(n):
                cp(a, 1 + j, (*chips[j], c), me).wait_recv()
                fwd = cp(a, 4 + j, (*chips[j], c), sib)
                fwd.start()
                started.append(fwd)

            @pl.when(c == j)
            def _():
                for a in range(n):
                    cp(a, 3, (*chips[j], c), (*chips[1 - j], c)).start()
        for a in range(n):
            cp(a, 3, (*chips[2], c), me).wait_recv()
            fwd = cp(a, 6, (*chips[2], c), sib)
            fwd.start()
            started.append(fwd)
        for a in range(n):
            cp(a, 0, sib, me).wait_recv()
            for j in range(3):
                cp(a, 4 + j, (*chips[j], 1 - c), me).wait_recv()
        for s in started:
            s.wait_send()
        for a in range(n):
            cp(a, 3, me, me).wait_send()
            pltpu.make_async_copy(ins[a], slot(a, me), loc_sems.at[a]).wait()

    anyspec = pl.BlockSpec(memory_space=pl.ANY)
    return pl.pallas_call(
        kern, out_shape=_xchg_out_shapes(arrs, (False,) * n), in_specs=[anyspec] * n, out_specs=tuple([anyspec] * n),
        scratch_shapes=[pltpu.SemaphoreType.DMA((n, 7)), pltpu.SemaphoreType.DMA((n, 7)), pltpu.SemaphoreType.DMA((n,))],
        name=name)(*arrs)


def _adamw(parts, w, m, v, name):
    r, c = w.shape
    n_parts = parts.shape[0]
    tr = r
    for cand in (128, 64, 32, 16, 8):
        if r % cand == 0 and r > cand:
            tr = cand
            break
    c1 = 1.0 / (1.0 - ADAM_B1 ** ADAM_STEP)
    c2 = 1.0 / (1.0 - ADAM_B2 ** ADAM_STEP)

    def kern(p_ref, w_ref, m_ref, v_ref, g_ref, d_ref, m2_ref, v2_ref):
        g = p_ref[0].astype(f32)
        for i in range(1, n_parts):
            g = g + p_ref[i].astype(f32)
        g_ref[...] = g
        m2 = ADAM_B1 * m_ref[...] + (1.0 - ADAM_B1) * g
        v2 = ADAM_B2 * v_ref[...] + (1.0 - ADAM_B2) * (g * g)
        m2_ref[...] = m2
        v2_ref[...] = v2
        d_ref[...] = -ADAM_LR * ((m2 * c1) / (jnp.sqrt(v2 * c2) + ADAM_EPS) + ADAM_WD * w_ref[...])

    blk = pl.BlockSpec((tr, c), lambda i: (i, 0))
    sh = S((r, c), f32)
    return pl.pallas_call(
        kern, out_shape=(sh, sh, sh, sh), grid=(r // tr,),
        in_specs=[pl.BlockSpec((n_parts, tr, c), lambda i: (0, i, 0)), blk, blk, blk], out_specs=(blk, blk, blk, blk),
        compiler_params=_params("parallel"), name=name)(parts, w, m, v)


_SMALL = (("c_ctx", 1024), ("b_mod", 3072), ("norm_w", 1024), ("ssm_conv_b", 4096), ("dt_bias", 64), ("a_log", 64),
          ("d_skip", 32), ("ssm_norm_w", 2048), ("conf_conv_b", 1024), ("conf_ln_w", 1024), ("conf_ln_b", 1024),
          ("final_norm_w", 1024))
SMALL_TILE = 8 * 128


def _pack_small(d):
    rows = []
    for name, n in _SMALL:
        v = d[name].reshape(-1).astype(f32)
        pad = (-n) % SMALL_TILE
        if pad:
            v = jnp.concatenate([v, jnp.zeros((pad,), f32)])
        rows.append(v.reshape(-1, 128))
    return jnp.concatenate(rows, axis=0)


def _unpack_small(p, shapes):
    out, r0 = {}, 0
    for name, n in _SMALL:
        nr = 8 * ((n + SMALL_TILE - 1) // SMALL_TILE)
        out[name] = p[r0:r0 + nr].reshape(-1)[:n].reshape(shapes[name])
        r0 += nr
    return out


def _permute_w_in(w):
    return jnp.concatenate([w[:, 9280:11328], w[:, 2048:4096], w[:, 0:2048], w[:, 6208:9280], w[:, 4160:6208],
                            w[:, 4096:4160], jnp.zeros((w.shape[0], NP - DT0 - 64), w.dtype)], axis=1)


def _unpermute_w_in(wp):
    return jnp.concatenate([wp[:, PX0:PX0 + 2048], wp[:, PBC0:PBC0 + 2048], wp[:, DT0:DT0 + 64], wp[:, Z0:Z0 + 2048],
                            wp[:, GV0:GV0 + 3072], wp[:, G10:G10 + 2048]], axis=1)


def _cols_gathered(g):
    return jnp.transpose(g, (1, 0, 2)).reshape(g.shape[1], N_DEV * g.shape[2])


def _cols_to_blocks(a):
    r, c8 = a.shape
    return jnp.transpose(a.reshape(r, N_DEV, c8 // N_DEV), (1, 0, 2))


def kernel(x, c, ctx, c_ctx, w_mod, b_mod, norm_w, w_in, ssm_conv_w, ssm_conv_b, dt_bias, a_log, d_skip, ssm_norm_w, w_out_ssm, conf_conv_w, conf_conv_b, conf_ln_w, conf_ln_b, w_out_conf, w_out, final_norm_w, loss_target, m_c_ctx, m_w_mod, m_b_mod, m_norm_w, m_w_in, m_ssm_conv_w, m_ssm_conv_b, m_dt_bias, m_a_log, m_d_skip, m_ssm_norm_w, m_w_out_ssm, m_conf_conv_w, m_conf_conv_b, m_conf_ln_w, m_conf_ln_b, m_w_out_conf, m_w_out, m_final_norm_w, v_c_ctx, v_w_mod, v_b_mod, v_norm_w, v_w_in, v_ssm_conv_w, v_ssm_conv_b, v_dt_bias, v_a_log, v_d_skip, v_ssm_norm_w, v_w_out_ssm, v_conf_conv_w, v_conf_conv_b, v_conf_ln_w, v_conf_ln_b, v_w_out_conf, v_w_out, v_final_norm_w):
    L = x.shape[1]
    Lc = ctx.shape[1]
    T = L + Lc
    nlx = L // RT
    nxc = L // Q
    x2 = x.reshape(L, D)
    ctx2 = ctx.reshape(Lc, D)
    tgt = loss_target.reshape(L, D)

    gathered = _gather2([w_in[0].astype(bf16), w_mod[0].astype(bf16), ssm_conv_w[0], conf_conv_w[0]], name="gather_weights")
    wp = _permute_w_in(_cols_gathered(gathered[0]))
    wmod_bf = _cols_gathered(gathered[1])
    scw8 = jnp.concatenate([_cols_gathered(gathered[2]), jnp.zeros((8 - SK, 4096), f32)], axis=0)
    ccw32 = jnp.concatenate([_cols_gathered(gathered[3]), jnp.zeros((32 - CK, D), f32)], axis=0)

    norm_w1 = norm_w.reshape(1, D)
    scb = ssm_conv_b.reshape(1, 4096)
    bias_row = jnp.concatenate([dt_bias.reshape(1, 2 * NH), jnp.zeros((1, 128 - 2 * NH), f32)], axis=1)
    alog_row = jnp.concatenate([a_log.reshape(1, 2 * NH), jnp.zeros((1, 128 - 2 * NH), f32)], axis=1)
    dskip_full = jnp.repeat(d_skip.reshape(NH), HD).reshape(1, DI)
    snw = ssm_norm_w.reshape(1, DI)
    ccb = conf_conv_b.reshape(1, D)
    lnw = conf_ln_w.reshape(1, D)
    lnb = conf_ln_b.reshape(1, D)
    fw = final_norm_w.reshape(1, D)

    cc8 = jnp.concatenate([c.reshape(1, D), c_ctx.reshape(1, D), jnp.zeros((6, D), f32)], axis=0)
    mod, silu_rows = _mod_fwd(cc8, wmod_bf, b_mod.reshape(1, 3 * D))
    h, h_t = _prenorm(x2, ctx2, norm_w1, mod)
    proj, wos_g, woc_g, wo_g = _matmul(
        h, wp, f32, "proj_gather", tn=NP // 5,
        comm=([w_out_ssm[0].astype(bf16), w_out_conf[0].astype(bf16), w_out[0].astype(bf16)], (False,) * 3))
    wos_bf = wos_g.reshape(DI, D)
    woc_bf = woc_g.reshape(D, D)
    wo_bf = wo_g.reshape(D, D)
    xs = _ssm_conv_fwd(proj, scw8, scb, nlx, 0, f32, "ssm_conv_fwd_x")
    bcm = _ssm_conv_fwd(proj, scw8, scb, nlx, 1, bf16, "ssm_conv_fwd_bc")
    dt, la = _dt_prep(proj, bias_row, alog_row)
    consts_f, consts_b = _scan_consts(False), _scan_consts(True)
    yf, hp_f = _ssd_fwd(xs, bcm, dt, la, consts_f, False, nxc, "ssd_fwd_f")
    ysum, hp_b = _ssd_fwd(xs, bcm, dt, la, consts_b, True, nxc, "ssd_fwd_b", y_acc=yf)
    y, yn, yn_t = _ynorm_fwd(ysum, xs, proj, dskip_full, snw, L)
    bs = _matmul(yn, wos_bf, f32, "branch_ssm", tm=1024, tk=2048)
    u1, u3, u3_t = _conf_fwd(proj, ccw32, ccb, lnw, lnb, L)
    bc = _matmul(u3, woc_bf, f32, "branch_conf", tm=2048)
    merged, merged_t = _merge_fwd(bs, bc, proj)
    out = _matmul(merged, wo_bf, f32, "out_proj", tm=2048)
    dx1, dout, loss_acc, dfw, dgate = _final(x2, out, tgt, mod, fw)

    dmerged = _matmul(dout, wo_bf, f32, "d_merged", tb=True, tm=2048)
    g_wo = _matmul(merged_t, dout, bf16, "g_w_out", tm=1024, tk=2048)
    dproj, dbs, dbc = _merge_bwd(dmerged, bs, bc, proj)
    dyn = _matmul(dbs, wos_bf, f32, "d_yn", tb=True, tm=1024, tn=2048)
    g_wos = _matmul(yn_t, dbs, bf16, "g_w_out_ssm", tm=1024, tk=2048)
    du3 = _matmul(dbc, woc_bf, f32, "d_u3", tb=True, tm=2048)
    g_woc = _matmul(u3_t, dbc, bf16, "g_w_out_conf", tm=1024, tk=2048)
    dproj, g_ccw, g_ccb, g_lnw, g_lnb = _conf_bwd(du3, u1, proj, ccw32, lnw, lnb, dproj)
    dproj, dy, g_snw, dsk_cols = _ynorm_bwd(dyn, y, xs, proj, dskip_full, snw, dproj)
    acc_f = _ssd_bwd(xs, bcm, dy, dt, la, hp_f, dskip_full, consts_f, False, nxc, "ssd_bwd_f")
    dxbc, a1, a2, r2, sv = _ssd_bwd(xs, bcm, dy, dt, la, hp_b, dskip_full, consts_b, True, nxc, "ssd_bwd_b", acc=acc_f)
    dproj, g_dtb, g_alog = _dt_bwd(a1, a2, r2, sv, dt, la, proj, bias_row, alog_row, dproj)
    dpre, g_scw, g_scb = _ssm_conv_dpre(dxbc, proj, scw8, scb, nlx)
    dproj = _ssm_conv_t(dpre, scw8, dproj, nlx)
    g_wp, *parts_b = _matmul(
        h_t, dproj, bf16, "g_w_in_scatter", tm=1024, tn=NP // 5,
        comm=([g_wos.reshape(N_DEV, DI // N_DEV, D), g_woc.reshape(N_DEV, D // N_DEV, D), g_wo.reshape(N_DEV, D // N_DEV, D),
               _cols_to_blocks(g_scw[:SK]), _cols_to_blocks(g_ccw[:CK])], (True,) * 5))
    dh, parts_a = _matmul(dproj, wp, f32, "d_h_scatter", tb=True, tk=NP // 5,
                          comm=([_cols_to_blocks(_unpermute_w_in(g_wp))], (True,)))
    parts = [parts_a] + parts_b
    gx, g_nw, macc = _prenorm_bwd(x2, ctx2, dh, dx1, norm_w1, mod)
    dmod_x = jnp.concatenate([macc[0:1], macc[1:2], dgate], axis=1)
    dmod_c = jnp.concatenate([macc[2:3], macc[3:4], jnp.zeros((1, D), f32)], axis=1)
    dmod8 = jnp.concatenate([dmod_x, dmod_c, jnp.zeros((6, 3 * D), f32)], axis=0)
    ct = jnp.concatenate([c.reshape(D, 1), c_ctx.reshape(D, 1), jnp.zeros((D, 126), f32)], axis=1)
    g_bmod, g_cctx = _mod_bwd(ct, dmod8, wmod_bf)
    g_dskip = _head_sums(dsk_cols.reshape(NH, HD))[:, 0]

    small_g = _pack_small({
        "c_ctx": g_cctx[:, 0], "b_mod": g_bmod, "norm_w": g_nw, "ssm_conv_b": g_scb, "dt_bias": g_dtb[0, :2 * NH],
        "a_log": g_alog[0, :2 * NH], "d_skip": g_dskip, "ssm_norm_w": g_snw, "conf_conv_b": g_ccb, "conf_ln_w": g_lnw,
        "conf_ln_b": g_lnb, "final_norm_w": dfw})
    fac = jnp.concatenate([silu_rows[0:1].reshape(D // 128, 128), dmod_x.reshape(3 * D // 128, 128),
                           dmod_c.reshape(3 * D // 128, 128)], axis=0)
    small_parts, fac_all = _exchange([small_g, fac], (False, False), name="exchange_tail")
    nr = D // 128
    sct = jnp.concatenate([fac_all[:, 0:nr].reshape(N_DEV, D).T, silu_rows[1:2].T, jnp.zeros((D, 128 - N_DEV - 1), f32)], axis=1)
    my_cols = (4 * lax.axis_index("x") + 2 * lax.axis_index("y") + lax.axis_index("c")) * (3 * D // N_DEV)
    dmx_all = lax.dynamic_slice(fac_all[:, nr:4 * nr].reshape(N_DEV, 3 * D), (0, my_cols), (N_DEV, 3 * D // N_DEV))
    dmc_all = lax.dynamic_slice(fac_all[:, 4 * nr:7 * nr].reshape(N_DEV, 3 * D), (0, my_cols), (N_DEV, 3 * D // N_DEV))
    g_wmod = _wmod_grad(sct, dmx_all, dmc_all)
    parts = [parts[0], g_wmod[None]] + parts[1:]

    given = dict(c_ctx=c_ctx, w_mod=w_mod, b_mod=b_mod, norm_w=norm_w, w_in=w_in, ssm_conv_w=ssm_conv_w, ssm_conv_b=ssm_conv_b,
                 dt_bias=dt_bias, a_log=a_log, d_skip=d_skip, ssm_norm_w=ssm_norm_w, w_out_ssm=w_out_ssm, conf_conv_w=conf_conv_w,
                 conf_conv_b=conf_conv_b, conf_ln_w=conf_ln_w, conf_ln_b=conf_ln_b, w_out_conf=w_out_conf, w_out=w_out,
                 final_norm_w=final_norm_w)
    ms = dict(c_ctx=m_c_ctx, w_mod=m_w_mod, b_mod=m_b_mod, norm_w=m_norm_w, w_in=m_w_in, ssm_conv_w=m_ssm_conv_w,
              ssm_conv_b=m_ssm_conv_b, dt_bias=m_dt_bias, a_log=m_a_log, d_skip=m_d_skip, ssm_norm_w=m_ssm_norm_w,
              w_out_ssm=m_w_out_ssm, conf_conv_w=m_conf_conv_w, conf_conv_b=m_conf_conv_b, conf_ln_w=m_conf_ln_w,
              conf_ln_b=m_conf_ln_b, w_out_conf=m_w_out_conf, w_out=m_w_out, final_norm_w=m_final_norm_w)
    vs = dict(c_ctx=v_c_ctx, w_mod=v_w_mod, b_mod=v_b_mod, norm_w=v_norm_w, w_in=v_w_in, ssm_conv_w=v_ssm_conv_w,
              ssm_conv_b=v_ssm_conv_b, dt_bias=v_dt_bias, a_log=v_a_log, d_skip=v_d_skip, ssm_norm_w=v_ssm_norm_w,
              w_out_ssm=v_w_out_ssm, conf_conv_w=v_conf_conv_w, conf_conv_b=v_conf_conv_b, conf_ln_w=v_conf_ln_w,
              conf_ln_b=v_conf_ln_b, w_out_conf=v_w_out_conf, w_out=v_w_out, final_norm_w=v_final_norm_w)
    grads, deltas, new_m, new_v = {}, {}, {}, {}
    sharded = ("w_in", "w_mod", "w_out_ssm", "w_out_conf", "w_out", "ssm_conv_w", "conf_conv_w")
    for i, nm in enumerate(sharded):
        shp = given[nm].shape
        w2 = given[nm].reshape(shp[1], shp[2])
        res = _adamw(parts[i], w2, ms[nm].reshape(w2.shape), vs[nm].reshape(w2.shape), "adamw_" + nm)
        grads[nm], deltas[nm], new_m[nm], new_v[nm] = [r.reshape(shp) for r in res]
    shapes = {nm: given[nm].shape for nm, _ in _SMALL}
    res = _adamw(small_parts, _pack_small(given), _pack_small(ms), _pack_small(vs), "adamw_small")
    for dst, packed in zip((grads, deltas, new_m, new_v), res):
        dst.update(_unpack_small(packed, shapes))

    loss = lax.psum(loss_acc[0, 0], ("x", "y", "c"))
    order = ("c_ctx", "w_mod", "b_mod", "norm_w", "w_in", "ssm_conv_w", "ssm_conv_b", "dt_bias", "a_log", "d_skip", "ssm_norm_w",
             "w_out_ssm", "conf_conv_w", "conf_conv_b", "conf_ln_w", "conf_ln_b", "w_out_conf", "w_out", "final_norm_w")
    return (loss, gx.reshape(1, L, D), *[grads[n] for n in order], *[deltas[n] for n in order],
            *[new_m[n] for n in order], *[new_v[n] for n in order])
```

```python
import jax
import jax.numpy as jnp
from jax import lax
from jax.experimental import pallas as pl
from jax.experimental.pallas import tpu as pltpu

f32 = jnp.float32
bf16 = jnp.bfloat16

D = 1024
DI = 2048
NG = 8
HPG = 4
HD = 64
GW = HPG * HD
NS = 128
NH = 32
Q = 128
GRID_W = 64
CK = 31
SK = 4
EPS = 1e-6
RT = 256
N_DEV = 8
IN_COLS = 11328
G10, G20, PBC0, PX0, GV0, GG0, CG0, Z0, DT0, NP = 0, 1024, 2048, 4096, 6144, 7168, 8192, 9216, 11264, 11520
CONV_CT = 2048
B0, C0 = 2048, 3072
VMEM_LIMIT = 50 * 1024 * 1024
NEG = -1e30

ADAM_LR, ADAM_B1, ADAM_B2, ADAM_EPS, ADAM_WD, ADAM_STEP = 0.001, 0.9, 0.999, 1e-08, 0.01, 10

MESH = pl.DeviceIdType.MESH
S = jax.ShapeDtypeStruct


def _params(*sem):
    return pltpu.CompilerParams(dimension_semantics=tuple(sem) if sem else None, vmem_limit_bytes=VMEM_LIMIT)


def _sig(x):
    return 1.0 / (1.0 + jnp.exp(-x))


def _silu(x):
    return x * _sig(x)


def _dsilu(x, s):
    return s * (1.0 + x * (1.0 - s))


def _dot(a, b):
    return jnp.dot(a, b, preferred_element_type=f32)


def _dot_nt(a, b):
    return lax.dot_general(a, b, (((1,), (1,)), ((), ())), preferred_element_type=f32)


def _dot_tn(a, b):
    return lax.dot_general(a, b, (((0,), (0,)), ((), ())), preferred_element_type=f32)


def _dot3(t_bf, v):
    v1 = v.astype(bf16)
    r1 = v - v1.astype(f32)
    v2 = r1.astype(bf16)
    v3 = (r1 - v2.astype(f32)).astype(bf16)
    return _dot(t_bf, v1) + _dot(t_bf, v2) + _dot(t_bf, v3)


def _pick(n, prefs):
    for p in prefs:
        if n % p == 0:
            return p
    return n


def _full(shape):
    nd = len(shape)
    return pl.BlockSpec(shape, lambda *_: (0,) * nd)


def _matmul(a, b, out_dtype, name, tm=None, tn=None, tk=None, tb=False, comm=None):
    m, k = a.shape
    n = b.shape[0] if tb else b.shape[1]
    tm = tm if tm and m % tm == 0 else _pick(m, (768, 512, 256, 128))
    tn = tn if tn and n % tn == 0 else _pick(n, (1024, 512, 256, 128))
    tk = tk if tk and k % tk == 0 else _pick(k, (1024, 768, 512, 256, 128))
    nk = k // tk
    gi, gj = m // tm, n // tn
    carrs, modes = comm if comm else ((), ())
    nc = len(carrs)

    def kern(*refs):
        a_ref, b_ref = refs[:2]
        cins = refs[2:2 + nc]
        o_ref = refs[2 + nc]
        couts = refs[3 + nc:3 + 2 * nc]
        acc_ref = refs[3 + 2 * nc]
        sems = refs[4 + 2 * nc:]
        i, j, kk = pl.program_id(0), pl.program_id(1), pl.program_id(2)
        if nc:
            @pl.when(jnp.logical_and(jnp.logical_and(i == 0, j == 0), kk == 0))
            def _():
                _xchg_start(cins, couts, *sems, modes)

        part = _dot_nt(a_ref[...], b_ref[...]) if tb else _dot(a_ref[...], b_ref[...])
        if nk == 1:
            o_ref[...] = part.astype(o_ref.dtype)
        else:
            @pl.when(kk == 0)
            def _():
                acc_ref[...] = part

            @pl.when(kk > 0)
            def _():
                acc_ref[...] += part

            @pl.when(kk == nk - 1)
            def _():
                o_ref[...] = acc_ref[...].astype(o_ref.dtype)

        if nc:
            @pl.when(jnp.logical_and(jnp.logical_and(i == gi - 1, j == gj - 1), kk == nk - 1))
            def _():
                _xchg_wait(cins, couts, *sems, modes)

    anyspec = pl.BlockSpec(memory_space=pl.ANY)
    bspec = pl.BlockSpec((tn, tk), lambda i, j, kk: (j, kk)) if tb else pl.BlockSpec((tk, tn), lambda i, j, kk: (kk, j))
    out_shape = (S((m, n), out_dtype),) + _xchg_out_shapes(carrs, modes)
    res = pl.pallas_call(
        kern, out_shape=out_shape, grid=(gi, gj, nk),
        in_specs=[pl.BlockSpec((tm, tk), lambda i, j, kk: (i, kk)), bspec] + [anyspec] * nc,
        out_specs=(pl.BlockSpec((tm, tn), lambda i, j, kk: (i, j)),) + (anyspec,) * nc,
        scratch_shapes=[pltpu.VMEM((tm, tn), f32)] + (_xchg_sems(nc) if nc else []),
        compiler_params=_params(*((("arbitrary",) * 3) if nc else ("parallel", "parallel", "arbitrary"))), name=name)(a, b, *carrs)
    return res if nc else res[0]


def _mod_fwd(cc8, w_mod_bf, b_mod):
    def kern(c_ref, w_ref, b_ref, o_ref, s_ref):
        s = _silu(c_ref[...])
        s_ref[...] = s
        o_ref[...] = _dot(s.astype(bf16), w_ref[...]) + b_ref[...]

    return pl.pallas_call(kern, out_shape=(S((8, 3 * D), f32), S((8, D), f32)), compiler_params=_params(),
                          name="mod_fwd")(cc8, w_mod_bf, b_mod)


def _mod_bwd(ct, dmod8, w_mod_bf):
    tc = 512
    nj = 3 * D // tc

    def kern(ct_ref, dm_ref, w_ref, db_ref, dc_ref):
        j = pl.program_id(0)
        cx = ct_ref[:, 1:2]
        sx = _sig(cx)
        dmc = dm_ref[1:2, :]
        db_ref[...] = dm_ref[0:1, :] + dmc
        t = jnp.sum(w_ref[...].astype(f32) * dmc.astype(bf16).astype(f32), axis=1, keepdims=True) * _dsilu(cx, sx)

        @pl.when(j == 0)
        def _():
            dc_ref[...] = jnp.zeros_like(dc_ref)

        dc_ref[...] += jnp.broadcast_to(t, (D, 128))

    return pl.pallas_call(
        kern, out_shape=(S((1, 3 * D), f32), S((D, 128), f32)), grid=(nj,),
        in_specs=[_full((D, 128)), pl.BlockSpec((8, tc), lambda j: (0, j)), pl.BlockSpec((D, tc), lambda j: (0, j))],
        out_specs=(pl.BlockSpec((1, tc), lambda j: (0, j)), _full((D, 128))),
        compiler_params=_params("arbitrary"), name="mod_bwd")(ct, dmod8, w_mod_bf)


def _wmod_grad(sct, dmx, dmc):
    cols = dmx.shape[1]

    def kern(s_ref, dmx_ref, dmc_ref, g_ref):
        dmc_sum = dmc_ref[0:1, :]
        for d in range(1, N_DEV):
            dmc_sum = dmc_sum + dmc_ref[d:d + 1, :]
        g = s_ref[:, N_DEV:N_DEV + 1] * dmc_sum
        for d in range(N_DEV):
            g = g + s_ref[:, d:d + 1] * dmx_ref[d:d + 1, :]
        g_ref[...] = g

    return pl.pallas_call(kern, out_shape=S((D, cols), f32), compiler_params=_params(), name="wmod_grad")(sct, dmx, dmc)


def _prenorm(x, ctx, norm_w, mod):
    L, Lc = x.shape[0], ctx.shape[0]
    nlx, nt = L // RT, (L + Lc) // RT

    def kern(x_ref, c_ref, nw_ref, mod_ref, h_ref, ht_ref):
        i = pl.program_id(0)
        is_c = i >= nlx
        xv = jnp.where(is_c, c_ref[...], x_ref[...])
        shift = jnp.where(is_c, mod_ref[1:2, 0:D], mod_ref[0:1, 0:D])
        scale = jnp.where(is_c, mod_ref[1:2, D:2 * D], mod_ref[0:1, D:2 * D])
        r = lax.rsqrt(jnp.mean(xv * xv, axis=1, keepdims=True) + EPS)
        hv = (xv * r) * nw_ref[...] * (1.0 + scale) + shift
        h_ref[...] = hv.astype(bf16)
        ht_ref[...] = jnp.transpose(hv).astype(bf16)

    return pl.pallas_call(
        kern, out_shape=(S((L + Lc, D), bf16), S((D, L + Lc), bf16)), grid=(nt,),
        in_specs=[pl.BlockSpec((RT, D), lambda i: (jnp.minimum(i, nlx - 1), 0)),
                  pl.BlockSpec((RT, D), lambda i: (jnp.maximum(i - nlx, 0), 0)),
                  _full((1, D)), _full((8, 3 * D))],
        out_specs=(pl.BlockSpec((RT, D), lambda i: (i, 0)), pl.BlockSpec((D, RT), lambda i: (0, i))),
        compiler_params=_params("parallel"), name="prenorm")(x, ctx, norm_w, mod)


def _prenorm_bwd(x, ctx, dh, dx1, norm_w, mod):
    L, Lc = x.shape[0], ctx.shape[0]
    nlx, nt = L // RT, (L + Lc) // RT

    def kern(x_ref, c_ref, dh_ref, dx1_ref, nw_ref, mod_ref, gx_ref, dnw_ref, acc_ref):
        i = pl.program_id(0)
        is_c = i >= nlx

        @pl.when(i == 0)
        def _():
            dnw_ref[...] = jnp.zeros_like(dnw_ref)
            acc_ref[...] = jnp.zeros_like(acc_ref)

        xv = jnp.where(is_c, c_ref[...], x_ref[...])
        scale = jnp.where(is_c, mod_ref[1:2, D:2 * D], mod_ref[0:1, D:2 * D])
        nw = nw_ref[...]
        r = lax.rsqrt(jnp.mean(xv * xv, axis=1, keepdims=True) + EPS)
        xn = xv * r
        dh = dh_ref[...]
        dsh = jnp.sum(dh, axis=0, keepdims=True)
        dsc = jnp.sum(dh * (xn * nw), axis=0, keepdims=True)
        dxnw = dh * (1.0 + scale)
        dnw_ref[...] += jnp.sum(dxnw * xn, axis=0, keepdims=True)
        dxn = dxnw * nw
        dx = r * (dxn - xn * jnp.mean(dxn * xn, axis=1, keepdims=True))

        @pl.when(jnp.logical_not(is_c))
        def _():
            gx_ref[...] = dx1_ref[...] + dx
            acc_ref[0:1, :] += dsh
            acc_ref[1:2, :] += dsc

        @pl.when(is_c)
        def _():
            acc_ref[2:3, :] += dsh
            acc_ref[3:4, :] += dsc

    xmap = lambda i: (jnp.minimum(i, nlx - 1), 0)
    return pl.pallas_call(
        kern, out_shape=(S((L, D), f32), S((1, D), f32), S((8, D), f32)), grid=(nt,),
        in_specs=[pl.BlockSpec((RT, D), xmap), pl.BlockSpec((RT, D), lambda i: (jnp.maximum(i - nlx, 0), 0)),
                  pl.BlockSpec((RT, D), lambda i: (i, 0)), pl.BlockSpec((RT, D), xmap), _full((1, D)), _full((8, 3 * D))],
        out_specs=(pl.BlockSpec((RT, D), xmap), _full((1, D)), _full((8, D))),
        compiler_params=_params("arbitrary"), name="prenorm_bwd")(x, ctx, dh, dx1, norm_w, mod)


def _xbc_col(j):
    return jnp.where(j == 0, PX0 // CONV_CT, PBC0 // CONV_CT)


def _halo_specs(nt_rows, ct, col=lambda j: j):
    cur = pl.BlockSpec((RT, ct), lambda i, j: (i, col(j)))
    prev = pl.BlockSpec((8, ct), lambda i, j: (jnp.maximum(i * (RT // 8) - 1, 0), col(j)))
    nxt = pl.BlockSpec((8, ct), lambda i, j: (jnp.minimum((i + 1) * (RT // 8), nt_rows // 8 - 1), col(j)))
    return cur, prev, nxt


def _fill_halo(scr, cur_ref, prev_ref, next_ref, i, nlx, nt):
    prev_ok = jnp.logical_and(i != 0, i != nlx)
    next_ok = jnp.logical_and(i != nlx - 1, i != nt - 1)
    scr[0:8, :] = jnp.where(prev_ok, prev_ref[...], 0.0)
    scr[8:8 + RT, :] = cur_ref[...]
    scr[8 + RT:16 + RT, :] = jnp.where(next_ok, next_ref[...], 0.0)


CONV_RB = 32


def _conv_blocks(ct):
    return [(slice(cb * 128, (cb + 1) * 128), r0) for cb in range(ct // 128) for r0 in range(0, RT, CONV_RB)]


def _taps(scr, cs, r0, shifts):
    blk = scr[r0:r0 + CONV_RB + 16, cs]
    n = CONV_RB + 16
    return [(blk if d == 0 else pltpu.roll(blk, (-d) % n, 0))[8:8 + CONV_RB, :] for d in shifts]


def _ssm_conv_fwd(proj, w8, b, nlx, half, out_dtype, name):
    T = proj.shape[0]
    nt = T // RT
    ct = CONV_CT
    cur, prev, nxt = _halo_specs(T, ct, lambda j: _xbc_col(j + half))

    def kern(cur_ref, prev_ref, next_ref, w_ref, b_ref, o_ref, scr):
        i = pl.program_id(0)
        _fill_halo(scr, cur_ref, prev_ref, next_ref, i, nlx, nt)
        for cs, r0 in _conv_blocks(ct):
            taps = _taps(scr, cs, r0, [k - 2 for k in range(SK)])
            acc = jnp.broadcast_to(b_ref[:, cs], (CONV_RB, 128))
            for k in range(SK):
                acc = acc + w_ref[k:k + 1, cs] * taps[k]
            o_ref[r0:r0 + CONV_RB, cs] = _silu(acc).astype(out_dtype)

    return pl.pallas_call(
        kern, out_shape=S((T, ct), out_dtype), grid=(nt, 1),
        in_specs=[cur, prev, nxt, pl.BlockSpec((8, ct), lambda i, j: (0, j + half)),
                  pl.BlockSpec((1, ct), lambda i, j: (0, j + half))],
        out_specs=pl.BlockSpec((RT, ct), lambda i, j: (i, j)),
        scratch_shapes=[pltpu.VMEM((RT + 16, ct), f32)],
        compiler_params=_params("parallel", "parallel"), name=name)(proj, proj, proj, w8, b)


def _ssm_conv_dpre(dxbc, proj, w8, b, nlx):
    T = proj.shape[0]
    nt = T // RT
    ct = CONV_CT
    cur = pl.BlockSpec((RT, ct), lambda j, i: (i, j))
    pcur = pl.BlockSpec((RT, ct), lambda j, i: (i, _xbc_col(j)))
    prev = pl.BlockSpec((8, ct), lambda j, i: (jnp.maximum(i * (RT // 8) - 1, 0), _xbc_col(j)))
    nxt = pl.BlockSpec((8, ct), lambda j, i: (jnp.minimum((i + 1) * (RT // 8), T // 8 - 1), _xbc_col(j)))

    def kern(d_ref, cur_ref, prev_ref, next_ref, w_ref, b_ref, dpre_ref, dw_ref, db_ref, scr):
        i = pl.program_id(1)
        _fill_halo(scr, cur_ref, prev_ref, next_ref, i, nlx, nt)

        @pl.when(i == 0)
        def _():
            dw_ref[...] = jnp.zeros_like(dw_ref)
            db_ref[...] = jnp.zeros_like(db_ref)

        for cb in range(ct // 128):
            cs = slice(cb * 128, (cb + 1) * 128)
            db_acc = jnp.zeros((CONV_RB, 128), f32)
            dw_acc = [jnp.zeros((CONV_RB, 128), f32) for _ in range(SK)]
            for r0 in range(0, RT, CONV_RB):
                taps = _taps(scr, cs, r0, [k - 2 for k in range(SK)])
                pre = jnp.broadcast_to(b_ref[:, cs], (CONV_RB, 128))
                for k in range(SK):
                    pre = pre + w_ref[k:k + 1, cs] * taps[k]
                dpre = d_ref[r0:r0 + CONV_RB, cs] * _dsilu(pre, _sig(pre))
                dpre_ref[r0:r0 + CONV_RB, cs] = dpre
                db_acc = db_acc + dpre
                dw_acc = [dw_acc[k] + dpre * taps[k] for k in range(SK)]
            db_ref[:, cs] += jnp.sum(db_acc, axis=0, keepdims=True)
            for k in range(SK):
                dw_ref[k:k + 1, cs] += jnp.sum(dw_acc[k], axis=0, keepdims=True)

    return pl.pallas_call(
        kern, out_shape=(S((T, 4096), f32), S((8, 4096), f32), S((1, 4096), f32)), grid=(4096 // ct, nt),
        in_specs=[cur, pcur, prev, nxt, pl.BlockSpec((8, ct), lambda j, i: (0, j)), pl.BlockSpec((1, ct), lambda j, i: (0, j))],
        out_specs=(cur, pl.BlockSpec((8, ct), lambda j, i: (0, j)), pl.BlockSpec((1, ct), lambda j, i: (0, j))),
        scratch_shapes=[pltpu.VMEM((RT + 16, ct), f32)],
        compiler_params=_params("parallel", "arbitrary"), name="ssm_conv_dpre")(dxbc, proj, proj, proj, w8, b)


def _ssm_conv_t(dpre, w8, dproj, nlx):
    T = dpre.shape[0]
    nt = T // RT
    ct = CONV_CT
    cur, prev, nxt = _halo_specs(T, ct)

    def kern(cur_ref, prev_ref, next_ref, w_ref, _alias, o_ref, scr):
        i = pl.program_id(0)
        _fill_halo(scr, cur_ref, prev_ref, next_ref, i, nlx, nt)
        for cs, r0 in _conv_blocks(ct):
            taps = _taps(scr, cs, r0, [2 - k for k in range(SK)])
            acc = jnp.zeros((CONV_RB, 128), f32)
            for k in range(SK):
                acc = acc + w_ref[k:k + 1, cs] * taps[k]
            o_ref[r0:r0 + CONV_RB, cs] = acc.astype(bf16)

    return pl.pallas_call(
        kern, out_shape=S(dproj.shape, bf16), grid=(nt, 4096 // ct),
        in_specs=[cur, prev, nxt, pl.BlockSpec((8, ct), lambda i, j: (0, j)), pl.BlockSpec(memory_space=pl.ANY)],
        out_specs=pl.BlockSpec((RT, ct), lambda i, j: (i, _xbc_col(j))),
        scratch_shapes=[pltpu.VMEM((RT + 16, ct), f32)], input_output_aliases={4: 0},
        compiler_params=_params("parallel", "parallel"), name="ssm_conv_t")(dpre, dpre, dpre, w8, dproj)


def _tri():
    li = lax.broadcasted_iota(jnp.int32, (Q, Q), 0)
    si = lax.broadcasted_iota(jnp.int32, (Q, Q), 1)
    return (si <= li).astype(bf16), (si >= li).astype(bf16)


def _dt_prep(proj, bias_row, alog_row):
    T = proj.shape[0]
    nch = T // Q

    def kern(raw_ref, b_ref, al_ref, dt_ref, la_ref):
        lane = lax.broadcasted_iota(jnp.int32, (Q, 128), 1)
        a = jnp.where(lane[0:1, :] < 2 * NH, -jnp.exp(al_ref[...]), 0.0)
        tri, trit = _tri()
        for h in range(SCAN_CH):
            rows = slice(h * Q, (h + 1) * Q)
            v = raw_ref[rows, :] + b_ref[...]
            dt = jnp.maximum(v, 0.0) + jnp.log1p(jnp.exp(-jnp.abs(v)))
            da = dt * a
            dt_ref[rows, :] = dt
            la_ref[rows, :] = jnp.where(lane < NH, _dot3(tri, da), _dot3(trit, da))

    rq = SCAN_CH * Q
    return pl.pallas_call(
        kern, out_shape=(S((T, 128), f32), S((T, 128), f32)), grid=(nch // SCAN_CH,),
        in_specs=[pl.BlockSpec((rq, 128), lambda c: (c, DT0 // 128)), _full((1, 128)), _full((1, 128))],
        out_specs=(pl.BlockSpec((rq, 128), lambda c: (c, 0)), pl.BlockSpec((rq, 128), lambda c: (c, 0))),
        compiler_params=_params("parallel"), name="dt_prep")(proj, bias_row, alog_row)


def _dt_bwd(a1, a2, r2, sv, dt, la, proj, bias_row, alog_row, dproj):
    T = proj.shape[0]
    nch = T // Q
    rq = SCAN_CH * Q
    blk = pl.BlockSpec((rq, 128), lambda c: (c, 0))

    def kern(a1_ref, a2_ref, r2_ref, s_ref, dt_ref, la_ref, raw_ref, b_ref, al_ref, _alias, o_ref, db_ref, dal_ref):
        c = pl.program_id(0)

        @pl.when(c == 0)
        def _():
            db_ref[...] = jnp.zeros_like(db_ref)
            dal_ref[...] = jnp.zeros_like(dal_ref)

        lane = lax.broadcasted_iota(jnp.int32, (Q, 128), 1)
        row = lax.broadcasted_iota(jnp.int32, (Q, 128), 0)
        fwd = lane < NH
        a = jnp.where(lane[0:1, :] < 2 * NH, -jnp.exp(al_ref[...]), 0.0)
        is_end = row == jnp.where(fwd, Q - 1, 0)
        tri, trit = _tri()
        o_ref[...] = jnp.zeros_like(o_ref)
        for h in range(SCAN_CH):
            rows = slice(h * Q, (h + 1) * Q)
            dt = dt_ref[rows, :]
            la = la_ref[rows, :]
            a2v = a2_ref[rows, :]
            r2v = r2_ref[rows, :]
            la_e = jnp.where(fwd[0:1, :], la[Q - 1:Q, :], la[0:1, :])
            e_end = jnp.exp(la_e - la)
            wend = e_end * dt
            extra = s_ref[h * Q:h * Q + 1, :] * jnp.exp(la_e) + jnp.sum(wend * a2v, axis=0, keepdims=True)
            dla = a1_ref[rows, :] - dt * r2v - wend * a2v + jnp.where(is_end, extra, 0.0)
            rcs = jnp.where(fwd, _dot3(trit, dla), _dot3(tri, dla))
            ddt = r2v + e_end * a2v + a * rcs
            dal_ref[...] += a * jnp.sum(dt * rcs, axis=0, keepdims=True)
            draw = jnp.where(lane < 2 * NH, ddt * _sig(raw_ref[rows, :] + b_ref[...]), 0.0)
            db_ref[...] += jnp.sum(draw, axis=0, keepdims=True)
            o_ref[rows, 0:128] = draw.astype(bf16)

    return pl.pallas_call(
        kern, out_shape=(S(dproj.shape, bf16), S((1, 128), f32), S((1, 128), f32)), grid=(nch // SCAN_CH,),
        in_specs=[blk, blk, blk, blk, blk, blk, pl.BlockSpec((rq, 128), lambda c: (c, DT0 // 128)),
                  _full((1, 128)), _full((1, 128)), pl.BlockSpec(memory_space=pl.ANY)],
        out_specs=(pl.BlockSpec((rq, NP - DT0), lambda c: (c, DT0 // (NP - DT0))), _full((1, 128)), _full((1, 128))),
        input_output_aliases={9: 0},
        compiler_params=_params("arbitrary"), name="dt_bwd")(a1, a2, r2, sv, dt, la, proj, bias_row, alog_row, dproj)


def _split2(v):
    hi = v.astype(bf16)
    lo = (v - hi.astype(f32)).astype(bf16)
    return jnp.concatenate([hi, lo], axis=1)


def _scan_consts(rev):
    hoff = NH if rev else 0
    g = jnp.arange(NG, dtype=jnp.int32)[:, None, None]

    def rc(nr, ncol):
        return jnp.arange(nr, dtype=jnp.int32)[None, :, None], jnp.arange(ncol, dtype=jnp.int32)[None, None, :]

    r, c = rc(2 * 128, HPG * HD)
    sel_w = (lax.rem(r, 128) == hoff + HPG * g + c // HD).astype(bf16)
    r, c = rc(HPG * HD, 128)
    ind_h = (c == hoff + HPG * g + r // HD).astype(bf16)
    r, c = rc(2 * HPG * Q, 128)
    ind_e = (c == hoff + HPG * g + lax.rem(r, HPG * Q) // Q).astype(bf16)
    return sel_w, ind_h, ind_e


def _masks(rev):
    li = lax.broadcasted_iota(jnp.int32, (Q, Q), 0)
    si = lax.broadcasted_iota(jnp.int32, (Q, Q), 1)
    mask = (li <= si) if rev else (li >= si)
    mask_t = (li >= si) if rev else (li <= si)
    lane = lax.broadcasted_iota(jnp.int32, (Q, HPG * HD), 1)
    hms = [jnp.logical_and(lane >= r * HD, lane < (r + 1) * HD) for r in range(HPG)]
    return mask, mask_t, hms


def _mine(hoff):
    lane = lax.broadcasted_iota(jnp.int32, (Q, 128), 1)
    return jnp.logical_and(lane >= hoff, lane < hoff + NH)


def _head_row(vals, hc0):
    lane = lax.broadcasted_iota(jnp.int32, (1, HPG * HD), 1)
    out = jnp.zeros((1, HPG * HD), f32)
    for r in range(HPG):
        out = jnp.where(jnp.logical_and(lane >= r * HD, lane < (r + 1) * HD), vals[:, hc0 + r:hc0 + r + 1], out)
    return out


SCAN_CH = 2


def _chunk_of(j, rev, nxc, nch):
    return (nch - 1 - j) if rev else lax.rem(j + nxc, nch)


def _ssd_fwd(xs, bc, dt, la, consts, rev, nxc, name, y_acc=None):
    T = xs.shape[0]
    nch = T // Q
    hoff = NH if rev else 0
    e = 0 if rev else Q - 1
    cm = lambda j: _chunk_of(j, rev, nxc // SCAN_CH, nch // SCAN_CH)
    sel_w = consts[0]
    has_acc = y_acc is not None

    def kern(*refs):
        xs_ref, bc_ref, dt_ref, la_ref, sw_ref = refs[:5]
        yacc_ref = refs[5] if has_acc else None
        y_ref, hp_ref, h_ref = refs[5 + has_acc:]
        j = pl.program_id(0)

        @pl.when(j == 0)
        def _():
            h_ref[...] = jnp.zeros_like(h_ref)

        mask, _, hms = _masks(rev)
        for hh in range(SCAN_CH):
            h = SCAN_CH - 1 - hh if rev else hh
            chunk(refs, mask, hms, h, slice(h * Q, (h + 1) * Q))

    def chunk(refs, mask, hms, h, rows):
        xs_ref, bc_ref, dt_ref, la_ref, sw_ref = refs[:5]
        yacc_ref = refs[5] if has_acc else None
        y_ref, hp_ref, h_ref = refs[5 + has_acc:]
        hp_ref[h] = h_ref[...]
        la_all = la_ref[rows, :]
        dt_all = dt_ref[rows, :]
        la_t = jnp.transpose(la_all)
        dt_t = jnp.transpose(dt_all)
        la_e = la_all[e:e + 1, :]
        w2 = _split2(jnp.exp(jnp.where(_mine(hoff), la_e - la_all, 0.0)) * dt_all)
        e2 = _split2(jnp.exp(la_all))
        ela_e = jnp.exp(la_e)
        for g in range(NG):
            hc0 = hoff + g * HPG
            x = xs_ref[rows, g * GW:(g + 1) * GW]
            bb = bc_ref[rows, g * NS:(g + 1) * NS]
            cb = bc_ref[rows, NG * NS + g * NS:NG * NS + (g + 1) * NS]
            ht = h_ref[g * NS:(g + 1) * NS, :]
            scores = _dot_nt(cb, bb)
            yoff = _dot(cb, ht.astype(bf16))
            wend = _dot(w2, sw_ref[g])
            expla = _dot(e2, sw_ref[g])
            mixes, xstack = [], []
            for r in range(HPG):
                hc = hc0 + r
                la_rep = jnp.broadcast_to(la_all[:, hc:hc + 1], (Q, 128))
                decay = jnp.exp(jnp.where(mask, la_rep - la_t[hc:hc + 1, :], NEG))
                mixes.append((scores * decay * dt_t[hc:hc + 1, :]).astype(bf16))
                xstack.append(jnp.where(hms[r], x, 0.0).astype(bf16))
            y = _dot(jnp.concatenate(mixes, axis=1), jnp.concatenate(xstack, axis=0)) + yoff * expla
            if has_acc:
                y = y + yacc_ref[rows, g * GW:(g + 1) * GW]
            y_ref[rows, g * GW:(g + 1) * GW] = y
            h_ref[g * NS:(g + 1) * NS, :] = ht * _head_row(ela_e, hc0) + _dot_tn(bb, (x * wend).astype(bf16))

    row = lambda j: (cm(j), 0)
    rq = SCAN_CH * Q
    yblk = pl.BlockSpec((rq, DI), row)
    return pl.pallas_call(
        kern, out_shape=(S((T, DI), f32), S((nch, NG * NS, HPG * HD), f32)), grid=(nch // SCAN_CH,),
        in_specs=[yblk, pl.BlockSpec((rq, 2 * NG * NS), row), pl.BlockSpec((rq, 128), row), pl.BlockSpec((rq, 128), row),
                  _full(sel_w.shape)] + ([yblk] if has_acc else []),
        out_specs=(yblk, pl.BlockSpec((SCAN_CH, NG * NS, HPG * HD), lambda j: (cm(j), 0, 0))),
        scratch_shapes=[pltpu.VMEM((NG * NS, HPG * HD), f32)],
        input_output_aliases={5: 0} if has_acc else {},
        compiler_params=_params("arbitrary"), name=name)(xs, bc, dt, la, sel_w, *([y_acc] if has_acc else []))


def _ssd_bwd(xs, bc, dy, dt, la, hprev, dskip_full, consts, rev, nxc, name, acc=None):
    T = xs.shape[0]
    nch = T // Q
    hoff = NH if rev else 0
    e = 0 if rev else Q - 1
    npair = nch // SCAN_CH
    cm = lambda j: _chunk_of(npair - 1 - j, rev, nxc // SCAN_CH, npair)
    has_acc = acc is not None
    sel_w, ind_h, ind_e = consts

    def kern(*refs):
        g_ref = refs[-2]
        j = pl.program_id(0)

        @pl.when(j == 0)
        def _():
            g_ref[...] = jnp.zeros_like(g_ref)

        masks = _masks(rev)
        for hh in range(SCAN_CH):
            h = hh if rev else SCAN_CH - 1 - hh
            chunk(refs, masks, h, slice(h * Q, (h + 1) * Q))

    def chunk(refs, masks, h, rows):
        xs_ref, bc_ref, dy_ref, dt_ref, la_ref, hp_ref, dsk_ref, sw_ref, ih_ref, ie_ref = refs[:10]
        k = 10
        if has_acc:
            dxbc_in, a1_in, a2_in, r2_in, s_in = refs[k:k + 5]
            k += 5
        dxbc_ref, a1_ref, a2_ref, r2_ref, s_ref, g_ref, r2scr = refs[k:k + 7]
        mask, mask_t, hms = masks
        la_all = la_ref[rows, :]
        dt_all = dt_ref[rows, :]
        la_t = jnp.transpose(la_all)
        dt_t = jnp.transpose(dt_all)
        la_e = la_all[e:e + 1, :]
        w2 = _split2(jnp.exp(jnp.where(_mine(hoff), la_e - la_all, 0.0)) * dt_all)
        e2 = _split2(jnp.exp(la_all))
        wed2 = jnp.concatenate([w2, e2, _split2(dt_all)], axis=0)
        ela_e = jnp.exp(la_e)
        r2scr[...] = jnp.zeros_like(r2scr)
        a1acc = jnp.zeros((Q, 128), f32)
        a2acc = jnp.zeros((Q, 128), f32)
        sacc = jnp.zeros((1, 128), f32)
        for g in range(NG):
            hc0 = hoff + g * HPG
            x = xs_ref[rows, g * GW:(g + 1) * GW]
            bb = bc_ref[rows, g * NS:(g + 1) * NS]
            cb = bc_ref[rows, NG * NS + g * NS:NG * NS + (g + 1) * NS]
            dyv = dy_ref[rows, g * GW:(g + 1) * GW]
            gt = g_ref[g * NS:(g + 1) * NS, :]
            ht = hp_ref[h, g * NS:(g + 1) * NS, :]
            gtb = gt.astype(bf16)
            htb = ht.astype(bf16)
            xb = x.astype(bf16)
            scores = _dot_nt(cb, bb)
            scores_t = _dot_nt(bb, cb)
            bg = _dot(bb, gtb)
            yoff = _dot(cb, htb)
            sel3 = _dot(wed2, sw_ref[g])
            wend, expla, dtf = sel3[0:Q], sel3[Q:2 * Q], sel3[2 * Q:3 * Q]
            dym = jnp.concatenate([jnp.where(hms[r], dyv, 0.0).astype(bf16) for r in range(HPG)], axis=0)
            dyx_all = _dot_nt(dym, xb)
            sdts, ems = [], []
            wsum = jnp.zeros((Q, Q), f32)
            for r in range(HPG):
                hc = hc0 + r
                la_rep = jnp.broadcast_to(la_all[:, hc:hc + 1], (Q, 128))
                la_r = la_t[hc:hc + 1, :]
                dt_r = dt_t[hc:hc + 1, :]
                decay = jnp.exp(jnp.where(mask, la_rep - la_r, NEG))
                decay_t = jnp.exp(jnp.where(mask_t, la_r - la_rep, NEG))
                dyx = dyx_all[r * Q:(r + 1) * Q, :]
                fm = dyx * (scores * decay)
                r2scr[hc:hc + 1, :] = jnp.sum(fm, axis=0, keepdims=True)
                ems.append(fm * dt_r)
                wsum = wsum + dyx * decay * dt_r
                sdts.append((scores_t * decay_t).astype(bf16))
            dx = dtf * _dot(jnp.concatenate(sdts, axis=1), dym) + wend * bg
            if not has_acc:
                dx = dx + dsk_ref[:, g * GW:(g + 1) * GW] * dyv
            red3 = _dot(jnp.concatenate([(dyv * yoff * expla).astype(bf16), (x * bg).astype(bf16), (gt * ht).astype(bf16)],
                                        axis=0), ih_ref[g])
            a1acc = a1acc + _dot(_split2(jnp.concatenate(ems, axis=1)), ie_ref[g]) + red3[0:Q]
            a2acc = a2acc + red3[Q:2 * Q]
            sacc = sacc + jnp.sum(red3[2 * Q:3 * Q], axis=0, keepdims=True)
            wb = wsum.astype(bf16)
            dysb = (dyv * expla).astype(bf16)
            dc = _dot(wb, bb) + _dot_nt(dysb, htb)
            db = _dot_tn(wb, cb) + _dot_nt((x * wend).astype(bf16), gtb)
            g_ref[g * NS:(g + 1) * NS, :] = gt * _head_row(ela_e, hc0) + _dot_tn(cb, dysb)
            if has_acc:
                dx = dx + dxbc_in[rows, g * GW:(g + 1) * GW]
                db = db + dxbc_in[rows, B0 + g * NS:B0 + (g + 1) * NS]
                dc = dc + dxbc_in[rows, C0 + g * NS:C0 + (g + 1) * NS]
            dxbc_ref[rows, g * GW:(g + 1) * GW] = dx
            dxbc_ref[rows, B0 + g * NS:B0 + (g + 1) * NS] = db
            dxbc_ref[rows, C0 + g * NS:C0 + (g + 1) * NS] = dc
        r2c = jnp.transpose(r2scr[...])
        sc = jnp.broadcast_to(sacc, (Q, 128))
        if has_acc:
            a1acc = a1acc + a1_in[rows, :]
            a2acc = a2acc + a2_in[rows, :]
            r2c = r2c + r2_in[rows, :]
            sc = sc + s_in[rows, :]
        a1_ref[rows, :] = a1acc
        a2_ref[rows, :] = a2acc
        r2_ref[rows, :] = r2c
        s_ref[rows, :] = sc

    rq = SCAN_CH * Q
    blk = pl.BlockSpec((rq, 128), lambda j: (cm(j), 0))
    big = pl.BlockSpec((rq, 4096), lambda j: (cm(j), 0))
    wide = pl.BlockSpec((rq, DI), lambda j: (cm(j), 0))
    in_specs = [wide, pl.BlockSpec((rq, 2 * NG * NS), lambda j: (cm(j), 0)), wide, blk, blk,
                pl.BlockSpec((SCAN_CH, NG * NS, HPG * HD), lambda j: (cm(j), 0, 0)), _full((1, DI)),
                _full(sel_w.shape), _full(ind_h.shape), _full(ind_e.shape)]
    args = [xs, bc, dy, dt, la, hprev, dskip_full, sel_w, ind_h, ind_e]
    aliases = {}
    if has_acc:
        in_specs += [big, blk, blk, blk, blk]
        args += list(acc)
        aliases = {10: 0, 11: 1, 12: 2, 13: 3, 14: 4}
    return pl.pallas_call(
        kern, out_shape=(S((T, 4096), f32), S((T, 128), f32), S((T, 128), f32), S((T, 128), f32), S((T, 128), f32)),
        grid=(npair,), in_specs=in_specs, out_specs=(big, blk, blk, blk, blk),
        scratch_shapes=[pltpu.VMEM((NG * NS, HPG * HD), f32), pltpu.VMEM((128, Q), f32)],
        input_output_aliases=aliases,
        compiler_params=_params("arbitrary"), name=name)(*args)


def _ynorm_fwd(ysum, xs, proj, dskip_full, nw, L):
    nlx = L // RT

    def kern(ys_ref, xs_ref, za_ref, zb_ref, dsk_ref, nw_ref, y_ref, yn_ref, ynt_ref):
        y = ys_ref[...] + dsk_ref[...] * xs_ref[...]
        y_ref[...] = y
        hg = NG // 2
        for g in range(NG):
            z_ref = za_ref if g < hg else zb_ref
            sl = y[:, g * GW:(g + 1) * GW] * _silu(z_ref[:, (g % hg) * GW:(g % hg + 1) * GW])
            r = lax.rsqrt(jnp.mean(sl * sl, axis=1, keepdims=True) + EPS)
            yn = (sl * r) * nw_ref[:, g * GW:(g + 1) * GW]
            yn_ref[:, g * GW:(g + 1) * GW] = yn.astype(bf16)
            ynt_ref[g * GW:(g + 1) * GW, :] = jnp.transpose(yn).astype(bf16)

    blk = pl.BlockSpec((RT, DI), lambda i: (i, 0))
    return pl.pallas_call(
        kern, out_shape=(S((L, DI), f32), S((L, DI), bf16), S((DI, L), bf16)), grid=(nlx,),
        in_specs=[blk, blk, pl.BlockSpec((RT, DI // 2), lambda i: (i, Z0 // (DI // 2))),
                  pl.BlockSpec((RT, DI // 2), lambda i: (i, Z0 // (DI // 2) + 1)), _full((1, DI)), _full((1, DI))],
        out_specs=(blk, blk, pl.BlockSpec((DI, RT), lambda i: (0, i))),
        compiler_params=_params("parallel"), name="ynorm_fwd")(ysum, xs, proj, proj, dskip_full, nw)


def _ynorm_bwd(dyn, y, xs, proj, dskip_full, nw, dproj):
    L = y.shape[0]
    T = proj.shape[0]
    nlx, nt = L // RT, T // RT

    hw = DI // 2

    def kern(dyn_ref, y_ref, xs_ref, z_ref, dsk_ref, nw_ref, _alias, dz_ref, dy_ref, dnw_ref, dsk_acc):
        i = pl.program_id(1)

        @pl.when(i == 0)
        def _():
            dnw_ref[...] = jnp.zeros_like(dnw_ref)
            dsk_acc[...] = jnp.zeros_like(dsk_acc)

        @pl.when(i >= nlx)
        def _():
            dz_ref[...] = jnp.zeros_like(dz_ref)
            dy_ref[...] = jnp.zeros_like(dy_ref)

        @pl.when(i < nlx)
        def _():
            y = y_ref[...]
            z = z_ref[...]
            sz = _sig(z)
            gz = z * sz
            yz = y * gz
            dynv = dyn_ref[...]
            for g in range(hw // GW):
                cs = slice(g * GW, (g + 1) * GW)
                sl = yz[:, cs]
                r = lax.rsqrt(jnp.mean(sl * sl, axis=1, keepdims=True) + EPS)
                yhat = sl * r
                dn = dynv[:, cs]
                dnw_ref[:, cs] += jnp.sum(dn * yhat, axis=0, keepdims=True)
                dyh = dn * nw_ref[:, cs]
                dyz = r * (dyh - yhat * jnp.mean(dyh * yhat, axis=1, keepdims=True))
                dyv = dyz * gz[:, cs]
                dy_ref[:, cs] = dyv
                dz_ref[:, cs] = (dyz * y[:, cs] * _dsilu(z[:, cs], sz[:, cs])).astype(bf16)
                dsk_acc[:, cs] += jnp.sum(dyv * xs_ref[:, cs], axis=0, keepdims=True)

    xblk = pl.BlockSpec((RT, hw), lambda j, i: (jnp.minimum(i, nlx - 1), j))
    row = pl.BlockSpec((1, hw), lambda j, i: (0, j))
    return pl.pallas_call(
        kern, out_shape=(S(dproj.shape, bf16), S((T, DI), f32), S((1, DI), f32), S((1, DI), f32)), grid=(2, nt),
        in_specs=[xblk, xblk, xblk, pl.BlockSpec((RT, hw), lambda j, i: (jnp.minimum(i, nlx - 1), Z0 // hw + j)), row, row,
                  pl.BlockSpec(memory_space=pl.ANY)],
        out_specs=(pl.BlockSpec((RT, hw), lambda j, i: (i, Z0 // hw + j)), pl.BlockSpec((RT, hw), lambda j, i: (i, j)), row, row),
        input_output_aliases={6: 0},
        compiler_params=_params("arbitrary", "arbitrary"), name="ynorm_bwd")(dyn, y, xs, proj, dskip_full, nw, dproj)


def _head_sums(cols):
    def kern(c_ref, o_ref):
        o_ref[...] = jnp.broadcast_to(jnp.sum(c_ref[...], axis=1, keepdims=True), (NH, 128))

    return pl.pallas_call(kern, out_shape=S((NH, 128), f32), name="head_sums")(cols)


SEG_STRIDE = 96
SEG_PAD = 16
NSEG = RT // GRID_W
CONF_ROWS = SEG_PAD + NSEG * SEG_STRIDE


SHIFT_ROWS = CONF_ROWS - 8
CONF_CW = 256


CONF_RB = 32


def _seg_zero_pads(scr):
    scr[0:SEG_PAD, :] = jnp.zeros((SEG_PAD, scr.shape[1]), f32)
    for s in range(NSEG):
        lo = SEG_PAD + s * SEG_STRIDE + GRID_W
        scr[lo:lo + SEG_STRIDE - GRID_W, :] = jnp.zeros((SEG_STRIDE - GRID_W, scr.shape[1]), f32)


def _seg_row(r0):
    return SEG_PAD + (r0 // GRID_W) * SEG_STRIDE + r0 % GRID_W


def _shift_copies(cps, scr, cs):
    full = scr[:, cs]
    for s in range(1, 8):
        cps[s - 1, :, :] = pltpu.roll(full, CONF_ROWS - s, 0)[0:SHIFT_ROWS, :]


def _tap(cps, scr, cs, o):
    rs = o % 8
    return scr[pl.ds(o, GRID_W), cs] if rs == 0 else cps[rs - 1, pl.ds(o - rs, GRID_W), :]


def _conf_fwd(proj, w32, cb, lnw, lnb, L):
    nlx = L // RT

    def kern(v_ref, g_ref, cg_ref, w_ref, cb_ref, lnw_ref, lnb_ref, u1_ref, u3_ref, u3t_ref, scr, cps, u3_scr):
        _seg_zero_pads(scr)
        for r0 in range(0, RT, CONF_RB):
            rows = slice(r0, r0 + CONF_RB)
            scr[_seg_row(r0):_seg_row(r0) + CONF_RB, :] = v_ref[rows, :] * _sig(g_ref[rows, :])
        for cc in range(D // CONF_CW):
            cs = slice(cc * CONF_CW, (cc + 1) * CONF_CW)
            _shift_copies(cps, scr, cs)
            for s in range(NSEG):
                acc = jnp.broadcast_to(cb_ref[:, cs], (GRID_W, CONF_CW))
                for k in range(CK):
                    acc = acc + w_ref[k:k + 1, cs] * _tap(cps, scr, cs, SEG_PAD + s * SEG_STRIDE + k - CK // 2)
                u1_ref[s * GRID_W:(s + 1) * GRID_W, cs] = acc
        for r0 in range(0, RT, CONF_RB):
            rows = slice(r0, r0 + CONF_RB)
            u1 = u1_ref[rows, :]
            xc = u1 - jnp.mean(u1, axis=1, keepdims=True)
            r = lax.rsqrt(jnp.mean(xc * xc, axis=1, keepdims=True) + EPS)
            u2 = (xc * r) * lnw_ref[...] + lnb_ref[...]
            u3 = _silu(u2) * _silu(cg_ref[rows, :])
            u3_ref[rows, :] = u3.astype(bf16)
            u3_scr[rows, :] = u3
        u3t_ref[...] = jnp.transpose(u3_scr[...]).astype(bf16)

    blk = pl.BlockSpec((RT, D), lambda i: (i, 0))
    return pl.pallas_call(
        kern, out_shape=(S((L, D), f32), S((L, D), bf16), S((D, L), bf16)), grid=(nlx,),
        in_specs=[pl.BlockSpec((RT, D), lambda i: (i, GV0 // D)), pl.BlockSpec((RT, D), lambda i: (i, GG0 // D)),
                  pl.BlockSpec((RT, D), lambda i: (i, CG0 // D)), _full((32, D)), _full((1, D)), _full((1, D)), _full((1, D))],
        out_specs=(blk, blk, pl.BlockSpec((D, RT), lambda i: (0, i))),
        scratch_shapes=[pltpu.VMEM((CONF_ROWS, D), f32), pltpu.VMEM((7, SHIFT_ROWS, CONF_CW), f32), pltpu.VMEM((RT, D), f32)],
        compiler_params=_params("parallel"), name="conf_fwd")(proj, proj, proj, w32, cb, lnw, lnb)


def _conf_bwd(du3, u1, proj, w32, lnw, lnb, dproj):
    L = u1.shape[0]
    T = proj.shape[0]
    nlx, nt = L // RT, T // RT

    def kern(du3_ref, u1_ref, v_ref, g_ref, cg_ref, w_ref, lnw_ref, lnb_ref, _alias,
             o_ref, dw_ref, dcb_ref, dlw_ref, dlb_ref, scr_u, scr_d, du0_scr, cps_u, cps_d):
        i = pl.program_id(0)

        @pl.when(i == 0)
        def _():
            dw_ref[...] = jnp.zeros_like(dw_ref)
            dcb_ref[...] = jnp.zeros_like(dcb_ref)
            dlw_ref[...] = jnp.zeros_like(dlw_ref)
            dlb_ref[...] = jnp.zeros_like(dlb_ref)

        @pl.when(i >= nlx)
        def _():
            o_ref[...] = jnp.zeros_like(o_ref)

        @pl.when(i < nlx)
        def _():
            _seg_zero_pads(scr_u)
            _seg_zero_pads(scr_d)
            for r0 in range(0, RT, CONF_RB):
                rows = slice(r0, r0 + CONF_RB)
                cg = cg_ref[rows, :]
                scg = _sig(cg)
                u1 = u1_ref[rows, :]
                xc = u1 - jnp.mean(u1, axis=1, keepdims=True)
                r = lax.rsqrt(jnp.mean(xc * xc, axis=1, keepdims=True) + EPS)
                xhat = xc * r
                u2 = xhat * lnw_ref[...] + lnb_ref[...]
                s2 = _sig(u2)
                du3v = du3_ref[rows, :]
                du2 = du3v * (cg * scg) * _dsilu(u2, s2)
                o_ref[rows, 2 * D:3 * D] = (du3v * (u2 * s2) * _dsilu(cg, scg)).astype(bf16)
                dlw_ref[...] += jnp.sum(du2 * xhat, axis=0, keepdims=True)
                dlb_ref[...] += jnp.sum(du2, axis=0, keepdims=True)
                dxh = du2 * lnw_ref[...]
                du1 = r * (dxh - jnp.mean(dxh, axis=1, keepdims=True) - xhat * jnp.mean(dxh * xhat, axis=1, keepdims=True))
                dcb_ref[...] += jnp.sum(du1, axis=0, keepdims=True)
                scr_u[_seg_row(r0):_seg_row(r0) + CONF_RB, :] = v_ref[rows, :] * _sig(g_ref[rows, :])
                scr_d[_seg_row(r0):_seg_row(r0) + CONF_RB, :] = du1
            for cc in range(D // CONF_CW):
                cs = slice(cc * CONF_CW, (cc + 1) * CONF_CW)
                _shift_copies(cps_u, scr_u, cs)
                _shift_copies(cps_d, scr_d, cs)
                for k in range(CK):
                    t = jnp.zeros((GRID_W, CONF_CW), f32)
                    for s in range(NSEG):
                        base = SEG_PAD + s * SEG_STRIDE
                        t = t + scr_d[pl.ds(base, GRID_W), cs] * _tap(cps_u, scr_u, cs, base + k - CK // 2)
                    dw_ref[k:k + 1, cs] += jnp.sum(t, axis=0, keepdims=True)
                for s in range(NSEG):
                    base = SEG_PAD + s * SEG_STRIDE
                    acc = jnp.zeros((GRID_W, CONF_CW), f32)
                    for k in range(CK):
                        acc = acc + w_ref[k:k + 1, cs] * _tap(cps_d, scr_d, cs, base + CK // 2 - k)
                    du0_scr[s * GRID_W:(s + 1) * GRID_W, cs] = acc
            for r0 in range(0, RT, CONF_RB):
                rows = slice(r0, r0 + CONF_RB)
                du0 = du0_scr[rows, :]
                sg = _sig(g_ref[rows, :])
                o_ref[rows, 0:D] = (du0 * sg).astype(bf16)
                o_ref[rows, D:2 * D] = (du0 * v_ref[rows, :] * sg * (1.0 - sg)).astype(bf16)

    xmap = lambda i: (jnp.minimum(i, nlx - 1), 0)
    pmap = lambda cb: (lambda i: (jnp.minimum(i, nlx - 1), cb))
    return pl.pallas_call(
        kern, out_shape=(S(dproj.shape, bf16), S((32, D), f32), S((1, D), f32), S((1, D), f32), S((1, D), f32)), grid=(nt,),
        in_specs=[pl.BlockSpec((RT, D), xmap), pl.BlockSpec((RT, D), xmap),
                  pl.BlockSpec((RT, D), pmap(GV0 // D)), pl.BlockSpec((RT, D), pmap(GG0 // D)), pl.BlockSpec((RT, D), pmap(CG0 // D)),
                  _full((32, D)), _full((1, D)), _full((1, D)), pl.BlockSpec(memory_space=pl.ANY)],
        out_specs=(pl.BlockSpec((RT, 3 * D), lambda i: (i, GV0 // (3 * D))), _full((32, D)), _full((1, D)), _full((1, D)), _full((1, D))),
        scratch_shapes=[pltpu.VMEM((CONF_ROWS, D), f32), pltpu.VMEM((CONF_ROWS, D), f32), pltpu.VMEM((RT, D), f32),
                        pltpu.VMEM((7, SHIFT_ROWS, CONF_CW), f32), pltpu.VMEM((7, SHIFT_ROWS, CONF_CW), f32)],
        input_output_aliases={8: 0},
        compiler_params=_params("arbitrary"), name="conf_bwd")(du3, u1, proj, proj, proj, w32, lnw, lnb, dproj)


def _merge_fwd(bs, bc, proj):
    L = bs.shape[0]

    def kern(bs_ref, bc_ref, g1_ref, g2_ref, o_ref, ot_ref):
        mv = _sig(g1_ref[...]) * bs_ref[...] + _sig(g2_ref[...]) * bc_ref[...]
        o_ref[...] = mv.astype(bf16)
        ot_ref[...] = jnp.transpose(mv).astype(bf16)

    blk = pl.BlockSpec((RT, D), lambda i: (i, 0))
    return pl.pallas_call(
        kern, out_shape=(S((L, D), bf16), S((D, L), bf16)), grid=(L // RT,),
        in_specs=[blk, blk, pl.BlockSpec((RT, D), lambda i: (i, G10 // D)), pl.BlockSpec((RT, D), lambda i: (i, G20 // D))],
        out_specs=(blk, pl.BlockSpec((D, RT), lambda i: (0, i))),
        compiler_params=_params("parallel"), name="merge_fwd")(bs, bc, proj, proj)


def _merge_bwd(dmerged, bs, bc, proj):
    L = bs.shape[0]
    T = proj.shape[0]
    nlx, nt = L // RT, T // RT

    def kern(dm_ref, bs_ref, bc_ref, g1_ref, g2_ref, o_ref, dbs_ref, dbc_ref):
        i = pl.program_id(0)

        @pl.when(i >= nlx)
        def _():
            o_ref[...] = jnp.zeros_like(o_ref)

        @pl.when(i < nlx)
        def _():
            dm = dm_ref[...]
            s1 = _sig(g1_ref[...])
            s2 = _sig(g2_ref[...])
            dbs_ref[...] = (dm * s1).astype(bf16)
            dbc_ref[...] = (dm * s2).astype(bf16)
            o_ref[:, 0:D] = (dm * bs_ref[...] * s1 * (1.0 - s1)).astype(bf16)
            o_ref[:, D:2 * D] = (dm * bc_ref[...] * s2 * (1.0 - s2)).astype(bf16)

    xmap = lambda i: (jnp.minimum(i, nlx - 1), 0)
    pmap = lambda cb: (lambda i: (jnp.minimum(i, nlx - 1), cb))
    xblk = pl.BlockSpec((RT, D), xmap)
    return pl.pallas_call(
        kern, out_shape=(S((T, NP), bf16), S((L, D), bf16), S((L, D), bf16)), grid=(nt,),
        in_specs=[xblk, xblk, xblk, pl.BlockSpec((RT, D), pmap(G10 // D)), pl.BlockSpec((RT, D), pmap(G20 // D))],
        out_specs=(pl.BlockSpec((RT, 2 * D), lambda i: (i, G10 // (2 * D))), xblk, xblk),
        compiler_params=_params("arbitrary"), name="merge_bwd")(dmerged, bs, bc, proj, proj)


def _final(x, out, target, mod, fw):
    L = x.shape[0]

    def kern(x_ref, o_ref, t_ref, mod_ref, fw_ref, dx1_ref, dout_ref, loss_ref, dfw_ref, dg_ref):
        i = pl.program_id(0)

        @pl.when(i == 0)
        def _():
            loss_ref[...] = jnp.zeros_like(loss_ref)
            dfw_ref[...] = jnp.zeros_like(dfw_ref)
            dg_ref[...] = jnp.zeros_like(dg_ref)

        gate = mod_ref[0:1, 2 * D:3 * D]
        ov = o_ref[...]
        x1 = x_ref[...] + gate * ov
        r = lax.rsqrt(jnp.mean(x1 * x1, axis=1, keepdims=True) + EPS)
        xn = x1 * r
        fw = fw_ref[...]
        err = xn * fw - t_ref[...]
        part = 0.5 * jnp.sum(jnp.mean(err * err, axis=1, keepdims=True), axis=0, keepdims=True)
        loss_ref[...] += jnp.broadcast_to(part, (8, 128))
        dy = err * (1.0 / D)
        dfw_ref[...] += jnp.sum(dy * xn, axis=0, keepdims=True)
        dyw = dy * fw
        dx1 = r * (dyw - xn * jnp.mean(dyw * xn, axis=1, keepdims=True))
        dx1_ref[...] = dx1
        dout_ref[...] = (gate * dx1).astype(bf16)
        dg_ref[...] += jnp.sum(dx1 * ov, axis=0, keepdims=True)

    blk = pl.BlockSpec((RT, D), lambda i: (i, 0))
    return pl.pallas_call(
        kern, out_shape=(S((L, D), f32), S((L, D), bf16), S((8, 128), f32), S((1, D), f32), S((1, D), f32)), grid=(L // RT,),
        in_specs=[blk, blk, blk, _full((8, 3 * D)), _full((1, D))],
        out_specs=(blk, blk, _full((8, 128)), _full((1, D)), _full((1, D))),
        compiler_params=_params("arbitrary"), name="final")(x, out, target, mod, fw)


def _me():
    return 4 * lax.axis_index("x") + 2 * lax.axis_index("y") + lax.axis_index("c")


def _xchg_copy(ins, outs, send_sems, recv_sems, modes, a, k, me):
    peer = lax.rem(me + k, N_DEV)
    pid = (peer // 4, lax.rem(peer // 2, 2), lax.rem(peer, 2))
    src = ins[a].at[peer] if modes[a] else ins[a]
    return pltpu.make_async_remote_copy(src_ref=src, dst_ref=outs[a].at[me], send_sem=send_sems.at[a, k - 1],
                                        recv_sem=recv_sems.at[a, k - 1], device_id=pid, device_id_type=MESH)


def _xchg_local(ins, outs, loc_sems, modes, a, me):
    return pltpu.make_async_copy(ins[a].at[me] if modes[a] else ins[a], outs[a].at[me], loc_sems.at[a])


def _xchg_start(ins, outs, send_sems, recv_sems, loc_sems, modes):
    me = _me()
    for a in range(len(modes)):
        _xchg_local(ins, outs, loc_sems, modes, a, me).start()
        for k in range(1, N_DEV):
            _xchg_copy(ins, outs, send_sems, recv_sems, modes, a, k, me).start()


def _xchg_wait(ins, outs, send_sems, recv_sems, loc_sems, modes):
    me = _me()
    for a in range(len(modes)):
        for k in range(1, N_DEV):
            frm = lax.rem(me + N_DEV - k, N_DEV)
            src = ins[a].at[frm] if modes[a] else ins[a]
            pltpu.make_async_remote_copy(src_ref=src, dst_ref=outs[a].at[frm], send_sem=send_sems.at[a, k - 1],
                                         recv_sem=recv_sems.at[a, k - 1], device_id=(0, 0, 0), device_id_type=MESH).wait_recv()
    for a in range(len(modes)):
        for k in range(1, N_DEV):
            _xchg_copy(ins, outs, send_sems, recv_sems, modes, a, k, me).wait_send()
        _xchg_local(ins, outs, loc_sems, modes, a, me).wait()


def _xchg_out_shapes(arrs, modes):
    return tuple(S((N_DEV,) + (a.shape[1:] if sc else a.shape), a.dtype) for a, sc in zip(arrs, modes))


def _xchg_sems(n):
    return [pltpu.SemaphoreType.DMA((n, N_DEV - 1)), pltpu.SemaphoreType.DMA((n, N_DEV - 1)), pltpu.SemaphoreType.DMA((n,))]


def _exchange(arrs, modes, name):
    n = len(arrs)

    def kern(*refs):
        ins, outs, sems = refs[:n], refs[n:2 * n], refs[2 * n:]
        _xchg_start(ins, outs, *sems, modes)
        _xchg_wait(ins, outs, *sems, modes)

    anyspec = pl.BlockSpec(memory_space=pl.ANY)
    return pl.pallas_call(
        kern, out_shape=_xchg_out_shapes(arrs, modes), in_specs=[anyspec] * n, out_specs=tuple([anyspec] * n),
        scratch_shapes=_xchg_sems(n), name=name)(*arrs)


def _gather2(arrs, name):
    n = len(arrs)

    def kern(*refs):
        ins, outs = refs[:n], refs[n:2 * n]
        send_sems, recv_sems, loc_sems = refs[2 * n:]
        x, y, c = lax.axis_index("x"), lax.axis_index("y"), lax.axis_index("c")
        me, sib = (x, y, c), (x, y, 1 - c)
        chips = [(1 - x, y), (x, 1 - y), (1 - x, 1 - y)]

        def slot(a, p):
            return outs[a].at[4 * p[0] + 2 * p[1] + p[2]]

        def cp(a, k, block, to, own=False):
            return pltpu.make_async_remote_copy(src_ref=ins[a] if own else slot(a, block), dst_ref=slot(a, block),
                                                send_sem=send_sems.at[a, k], recv_sem=recv_sems.at[a, k],
                                                device_id=to, device_id_type=MESH)

        started = []
        for a in range(n):
            pltpu.make_async_copy(ins[a], slot(a, me), loc_sems.at[a]).start()
            started.append(cp(a, 0, me, sib, own=True))
            started += [cp(a, 1 + j, me, (*chips[j], c), own=True) for j in range(2)]
        for s in started:
            s.start()
        for j in range(2):
            for a in range(n):
                cp(a, 1 + j, (*chips[j], c), me).wait_recv()
                fwd = cp(a, 4 + j, (*chips[j], c), sib)
                fwd.start()
                started.append(fwd)

            @pl.when(c == j)
            def _():
                for a in range(n):
                    cp(a, 3, (*chips[j], c), (*chips[1 - j], c)).start()
        for a in range(n):
            cp(a, 3, (*chips[2], c), me).wait_recv()
            fwd = cp(a, 6, (*chips[2], c), sib)
            fwd.start()
            started.append(fwd)
        for a in range(n):
            cp(a, 0, sib, me).wait_recv()
            for j in range(3):
                cp(a, 4 + j, (*chips[j], 1 - c), me).wait_recv()
        for s in started:
            s.wait_send()
        for a in range(n):
            cp(a, 3, me, me).wait_send()
            pltpu.make_async_copy(ins[a], slot(a, me), loc_sems.at[a]).wait()

    anyspec = pl.BlockSpec(memory_space=pl.ANY)
    return pl.pallas_call(
        kern, out_shape=_xchg_out_shapes(arrs, (False,) * n), in_specs=[anyspec] * n, out_specs=tuple([anyspec] * n),
        scratch_shapes=[pltpu.SemaphoreType.DMA((n, 7)), pltpu.SemaphoreType.DMA((n, 7)), pltpu.SemaphoreType.DMA((n,))],
        name=name)(*arrs)


def _adamw(parts, w, m, v, name):
    r, c = w.shape
    n_parts = parts.shape[0]
    tr = r
    for cand in (128, 64, 32, 16, 8):
        if r % cand == 0 and r > cand:
            tr = cand
            break
    c1 = 1.0 / (1.0 - ADAM_B1 ** ADAM_STEP)
    c2 = 1.0 / (1.0 - ADAM_B2 ** ADAM_STEP)

    def kern(p_ref, w_ref, m_ref, v_ref, g_ref, d_ref, m2_ref, v2_ref):
        g = p_ref[0].astype(f32)
        for i in range(1, n_parts):
            g = g + p_ref[i].astype(f32)
        g_ref[...] = g
        m2 = ADAM_B1 * m_ref[...] + (1.0 - ADAM_B1) * g
        v2 = ADAM_B2 * v_ref[...] + (1.0 - ADAM_B2) * (g * g)
        m2_ref[...] = m2
        v2_ref[...] = v2
        d_ref[...] = -ADAM_LR * ((m2 * c1) / (jnp.sqrt(v2 * c2) + ADAM_EPS) + ADAM_WD * w_ref[...])

    blk = pl.BlockSpec((tr, c), lambda i: (i, 0))
    sh = S((r, c), f32)
    return pl.pallas_call(
        kern, out_shape=(sh, sh, sh, sh), grid=(r // tr,),
        in_specs=[pl.BlockSpec((n_parts, tr, c), lambda i: (0, i, 0)), blk, blk, blk], out_specs=(blk, blk, blk, blk),
        compiler_params=_params("parallel"), name=name)(parts, w, m, v)


_SMALL = (("c_ctx", 1024), ("b_mod", 3072), ("norm_w", 1024), ("ssm_conv_b", 4096), ("dt_bias", 64), ("a_log", 64),
          ("d_skip", 32), ("ssm_norm_w", 2048), ("conf_conv_b", 1024), ("conf_ln_w", 1024), ("conf_ln_b", 1024),
          ("final_norm_w", 1024))
SMALL_TILE = 8 * 128


def _pack_small(d):
    rows = []
    for name, n in _SMALL:
        v = d[name].reshape(-1).astype(f32)
        pad = (-n) % SMALL_TILE
        if pad:
            v = jnp.concatenate([v, jnp.zeros((pad,), f32)])
        rows.append(v.reshape(-1, 128))
    return jnp.concatenate(rows, axis=0)


def _unpack_small(p, shapes):
    out, r0 = {}, 0
    for name, n in _SMALL:
        nr = 8 * ((n + SMALL_TILE - 1) // SMALL_TILE)
        out[name] = p[r0:r0 + nr].reshape(-1)[:n].reshape(shapes[name])
        r0 += nr
    return out


def _permute_w_in(w):
    return jnp.concatenate([w[:, 9280:11328], w[:, 2048:4096], w[:, 0:2048], w[:, 6208:9280], w[:, 4160:6208],
                            w[:, 4096:4160], jnp.zeros((w.shape[0], NP - DT0 - 64), w.dtype)], axis=1)


def _unpermute_w_in(wp):
    return jnp.concatenate([wp[:, PX0:PX0 + 2048], wp[:, PBC0:PBC0 + 2048], wp[:, DT0:DT0 + 64], wp[:, Z0:Z0 + 2048],
                            wp[:, GV0:GV0 + 3072], wp[:, G10:G10 + 2048]], axis=1)


def _cols_gathered(g):
    return jnp.transpose(g, (1, 0, 2)).reshape(g.shape[1], N_DEV * g.shape[2])


def _cols_to_blocks(a):
    r, c8 = a.shape
    return jnp.transpose(a.reshape(r, N_DEV, c8 // N_DEV), (1, 0, 2))


def kernel(x, c, ctx, c_ctx, w_mod, b_mod, norm_w, w_in, ssm_conv_w, ssm_conv_b, dt_bias, a_log, d_skip, ssm_norm_w, w_out_ssm, conf_conv_w, conf_conv_b, conf_ln_w, conf_ln_b, w_out_conf, w_out, final_norm_w, loss_target, m_c_ctx, m_w_mod, m_b_mod, m_norm_w, m_w_in, m_ssm_conv_w, m_ssm_conv_b, m_dt_bias, m_a_log, m_d_skip, m_ssm_norm_w, m_w_out_ssm, m_conf_conv_w, m_conf_conv_b, m_conf_ln_w, m_conf_ln_b, m_w_out_conf, m_w_out, m_final_norm_w, v_c_ctx, v_w_mod, v_b_mod, v_norm_w, v_w_in, v_ssm_conv_w, v_ssm_conv_b, v_dt_bias, v_a_log, v_d_skip, v_ssm_norm_w, v_w_out_ssm, v_conf_conv_w, v_conf_conv_b, v_conf_ln_w, v_conf_ln_b, v_w_out_conf, v_w_out, v_final_norm_w):
    L = x.shape[1]
    Lc = ctx.shape[1]
    T = L + Lc
    nlx = L // RT
    nxc = L // Q
    x2 = x.reshape(L, D)
    ctx2 = ctx.reshape(Lc, D)
    tgt = loss_target.reshape(L, D)

    gathered = _gather2([w_in[0].astype(bf16), w_mod[0].astype(bf16), ssm_conv_w[0], conf_conv_w[0]], name="gather_weights")
    wp = _permute_w_in(_cols_gathered(gathered[0]))
    wmod_bf = _cols_gathered(gathered[1])
    scw8 = jnp.concatenate([_cols_gathered(gathered[2]), jnp.zeros((8 - SK, 4096), f32)], axis=0)
    ccw32 = jnp.concatenate([_cols_gathered(gathered[3]), jnp.zeros((32 - CK, D), f32)], axis=0)

    norm_w1 = norm_w.reshape(1, D)
    scb = ssm_conv_b.reshape(1, 4096)
    bias_row = jnp.concatenate([dt_bias.reshape(1, 2 * NH), jnp.zeros((1, 128 - 2 * NH), f32)], axis=1)
    alog_row = jnp.concatenate([a_log.reshape(1, 2 * NH), jnp.zeros((1, 128 - 2 * NH), f32)], axis=1)
    dskip_full = jnp.repeat(d_skip.reshape(NH), HD).reshape(1, DI)
    snw = ssm_norm_w.reshape(1, DI)
    ccb = conf_conv_b.reshape(1, D)
    lnw = conf_ln_w.reshape(1, D)
    lnb = conf_ln_b.reshape(1, D)
    fw = final_norm_w.reshape(1, D)

    cc8 = jnp.concatenate([c.reshape(1, D), c_ctx.reshape(1, D), jnp.zeros((6, D), f32)], axis=0)
    mod, silu_rows = _mod_fwd(cc8, wmod_bf, b_mod.reshape(1, 3 * D))
    h, h_t = _prenorm(x2, ctx2, norm_w1, mod)
    proj, wos_g, woc_g, wo_g = _matmul(
        h, wp, f32, "proj_gather", tn=NP // 5,
        comm=([w_out_ssm[0].astype(bf16), w_out_conf[0].astype(bf16), w_out[0].astype(bf16)], (False,) * 3))
    wos_bf = wos_g.reshape(DI, D)
    woc_bf = woc_g.reshape(D, D)
    wo_bf = wo_g.reshape(D, D)
    xs = _ssm_conv_fwd(proj, scw8, scb, nlx, 0, f32, "ssm_conv_fwd_x")
    bcm = _ssm_conv_fwd(proj, scw8, scb, nlx, 1, bf16, "ssm_conv_fwd_bc")
    dt, la = _dt_prep(proj, bias_row, alog_row)
    consts_f, consts_b = _scan_consts(False), _scan_consts(True)
    yf, hp_f = _ssd_fwd(xs, bcm, dt, la, consts_f, False, nxc, "ssd_fwd_f")
    ysum, hp_b = _ssd_fwd(xs, bcm, dt, la, consts_b, True, nxc, "ssd_fwd_b", y_acc=yf)
    y, yn, yn_t = _ynorm_fwd(ysum, xs, proj, dskip_full, snw, L)
    bs = _matmul(yn, wos_bf, f32, "branch_ssm", tm=1024, tk=2048)
    u1, u3, u3_t = _conf_fwd(proj, ccw32, ccb, lnw, lnb, L)
    bc = _matmul(u3, woc_bf, f32, "branch_conf", tm=2048)
    merged, merged_t = _merge_fwd(bs, bc, proj)
    out = _matmul(merged, wo_bf, f32, "out_proj", tm=2048)
    dx1, dout, loss_acc, dfw, dgate = _final(x2, out, tgt, mod, fw)

    dmerged = _matmul(dout, wo_bf, f32, "d_merged", tb=True, tm=2048)
    g_wo = _matmul(merged_t, dout, bf16, "g_w_out", tm=1024, tk=2048)
    dproj, dbs, dbc = _merge_bwd(dmerged, bs, bc, proj)
    dyn = _matmul(dbs, wos_bf, f32, "d_yn", tb=True, tm=1024, tn=2048)
    g_wos = _matmul(yn_t, dbs, bf16, "g_w_out_ssm", tm=1024, tk=2048)
    du3 = _matmul(dbc, woc_bf, f32, "d_u3", tb=True, tm=2048)
    g_woc = _matmul(u3_t, dbc, bf16, "g_w_out_conf", tm=1024, tk=2048)
    dproj, g_ccw, g_ccb, g_lnw, g_lnb = _conf_bwd(du3, u1, proj, ccw32, lnw, lnb, dproj)
    dproj, dy, g_snw, dsk_cols = _ynorm_bwd(dyn, y, xs, proj, dskip_full, snw, dproj)
    acc_f = _ssd_bwd(xs, bcm, dy, dt, la, hp_f, dskip_full, consts_f, False, nxc, "ssd_bwd_f")
    dxbc, a1, a2, r2, sv = _ssd_bwd(xs, bcm, dy, dt, la, hp_b, dskip_full, consts_b, True, nxc, "ssd_bwd_b", acc=acc_f)
    dproj, g_dtb, g_alog = _dt_bwd(a1, a2, r2, sv, dt, la, proj, bias_row, alog_row, dproj)
    dpre, g_scw, g_scb = _ssm_conv_dpre(dxbc, proj, scw8, scb, nlx)
    dproj = _ssm_conv_t(dpre, scw8, dproj, nlx)
    g_wp, *parts_b = _matmul(
        h_t, dproj, bf16, "g_w_in_scatter", tm=1024, tn=NP // 5,
        comm=([g_wos.reshape(N_DEV, DI // N_DEV, D), g_woc.reshape(N_DEV, D // N_DEV, D), g_wo.reshape(N_DEV, D // N_DEV, D),
               _cols_to_blocks(g_scw[:SK]), _cols_to_blocks(g_ccw[:CK])], (True,) * 5))
    dh, parts_a = _matmul(dproj, wp, f32, "d_h_scatter", tb=True, tk=NP // 5,
                          comm=([_cols_to_blocks(_unpermute_w_in(g_wp))], (True,)))
    parts = [parts_a] + parts_b
    gx, g_nw, macc = _prenorm_bwd(x2, ctx2, dh, dx1, norm_w1, mod)
    dmod_x = jnp.concatenate([macc[0:1], macc[1:2], dgate], axis=1)
    dmod_c = jnp.concatenate([macc[2:3], macc[3:4], jnp.zeros((1, D), f32)], axis=1)
    dmod8 = jnp.concatenate([dmod_x, dmod_c, jnp.zeros((6, 3 * D), f32)], axis=0)
    ct = jnp.concatenate([c.reshape(D, 1), c_ctx.reshape(D, 1), jnp.zeros((D, 126), f32)], axis=1)
    g_bmod, g_cctx = _mod_bwd(ct, dmod8, wmod_bf)
    g_dskip = _head_sums(dsk_cols.reshape(NH, HD))[:, 0]

    small_g = _pack_small({
        "c_ctx": g_cctx[:, 0], "b_mod": g_bmod, "norm_w": g_nw, "ssm_conv_b": g_scb, "dt_bias": g_dtb[0, :2 * NH],
        "a_log": g_alog[0, :2 * NH], "d_skip": g_dskip, "ssm_norm_w": g_snw, "conf_conv_b": g_ccb, "conf_ln_w": g_lnw,
        "conf_ln_b": g_lnb, "final_norm_w": dfw})
    fac = jnp.concatenate([silu_rows[0:1].reshape(D // 128, 128), dmod_x.reshape(3 * D // 128, 128),
                           dmod_c.reshape(3 * D // 128, 128)], axis=0)
    small_parts, fac_all = _exchange([small_g, fac], (False, False), name="exchange_tail")
    nr = D // 128
    sct = jnp.concatenate([fac_all[:, 0:nr].reshape(N_DEV, D).T, silu_rows[1:2].T, jnp.zeros((D, 128 - N_DEV - 1), f32)], axis=1)
    my_cols = (4 * lax.axis_index("x") + 2 * lax.axis_index("y") + lax.axis_index("c")) * (3 * D // N_DEV)
    dmx_all = lax.dynamic_slice(fac_all[:, nr:4 * nr].reshape(N_DEV, 3 * D), (0, my_cols), (N_DEV, 3 * D // N_DEV))
    dmc_all = lax.dynamic_slice(fac_all[:, 4 * nr:7 * nr].reshape(N_DEV, 3 * D), (0, my_cols), (N_DEV, 3 * D // N_DEV))
    g_wmod = _wmod_grad(sct, dmx_all, dmc_all)
    parts = [parts[0], g_wmod[None]] + parts[1:]

    given = dict(c_ctx=c_ctx, w_mod=w_mod, b_mod=b_mod, norm_w=norm_w, w_in=w_in, ssm_conv_w=ssm_conv_w, ssm_conv_b=ssm_conv_b,
                 dt_bias=dt_bias, a_log=a_log, d_skip=d_skip, ssm_norm_w=ssm_norm_w, w_out_ssm=w_out_ssm, conf_conv_w=conf_conv_w,
                 conf_conv_b=conf_conv_b, conf_ln_w=conf_ln_w, conf_ln_b=conf_ln_b, w_out_conf=w_out_conf, w_out=w_out,
                 final_norm_w=final_norm_w)
    ms = dict(c_ctx=m_c_ctx, w_mod=m_w_mod, b_mod=m_b_mod, norm_w=m_norm_w, w_in=m_w_in, ssm_conv_w=m_ssm_conv_w,
              ssm_conv_b=m_ssm_conv_b, dt_bias=m_dt_bias, a_log=m_a_log, d_skip=m_d_skip, ssm_norm_w=m_ssm_norm_w,
              w_out_ssm=m_w_out_ssm, conf_conv_w=m_conf_conv_w, conf_conv_b=m_conf_conv_b, conf_ln_w=m_conf_ln_w,
              conf_ln_b=m_conf_ln_b, w_out_conf=m_w_out_conf, w_out=m_w_out, final_norm_w=m_final_norm_w)
    vs = dict(c_ctx=v_c_ctx, w_mod=v_w_mod, b_mod=v_b_mod, norm_w=v_norm_w, w_in=v_w_in, ssm_conv_w=v_ssm_conv_w,
              ssm_conv_b=v_ssm_conv_b, dt_bias=v_dt_bias, a_log=v_a_log, d_skip=v_d_skip, ssm_norm_w=v_ssm_norm_w,
              w_out_ssm=v_w_out_ssm, conf_conv_w=v_conf_conv_w, conf_conv_b=v_conf_conv_b, conf_ln_w=v_conf_ln_w,
              conf_ln_b=v_conf_ln_b, w_out_conf=v_w_out_conf, w_out=v_w_out, final_norm_w=v_final_norm_w)
    grads, deltas, new_m, new_v = {}, {}, {}, {}
    sharded = ("w_in", "w_mod", "w_out_ssm", "w_out_conf", "w_out", "ssm_conv_w", "conf_conv_w")
    for i, nm in enumerate(sharded):
        shp = given[nm].shape
        w2 = given[nm].reshape(shp[1], shp[2])
        res = _adamw(parts[i], w2, ms[nm].reshape(w2.shape), vs[nm].reshape(w2.shape), "adamw_" + nm)
        grads[nm], deltas[nm], new_m[nm], new_v[nm] = [r.reshape(shp) for r in res]
    shapes = {nm: given[nm].shape for nm, _ in _SMALL}
    res = _adamw(small_parts, _pack_small(given), _pack_small(ms), _pack_small(vs), "adamw_small")
    for dst, packed in zip((grads, deltas, new_m, new_v), res):
        dst.update(_unpack_small(packed, shapes))

    loss = lax.psum(loss_acc[0, 0], ("x", "y", "c"))
    order = ("c_ctx", "w_mod", "b_mod", "norm_w", "w_in", "ssm_conv_w", "ssm_conv_b", "dt_bias", "a_log", "d_skip", "ssm_norm_w",
             "w_out_ssm", "conf_conv_w", "conf_conv_b", "conf_ln_w", "conf_ln_b", "w_out_conf", "w_out", "final_norm_w")
    return (loss, gx.reshape(1, L, D), *[grads[n] for n in order], *[deltas[n] for n in order],
            *[new_m[n] for n in order], *[new_v[n] for n in order])
```

```python
import jax
import jax.numpy as jnp
from jax import lax
from jax.experimental import pallas as pl
from jax.experimental.pallas import tpu as pltpu

f32 = jnp.float32
bf16 = jnp.bfloat16

D = 1024
DI = 2048
NG = 8
HPG = 4
HD = 64
GW = HPG * HD
NS = 128
NH = 32
Q = 128
GRID_W = 64
CK = 31
SK = 4
EPS = 1e-6
RT = 256
N_DEV = 8
IN_COLS = 11328
G10, G20, PBC0, PX0, GV0, GG0, CG0, Z0, DT0, NP = 0, 1024, 2048, 4096, 6144, 7168, 8192, 9216, 11264, 11520
CONV_CT = 2048
B0, C0 = 2048, 3072
VMEM_LIMIT = 50 * 1024 * 1024
NEG = -1e30

ADAM_LR, ADAM_B1, ADAM_B2, ADAM_EPS, ADAM_WD, ADAM_STEP = 0.001, 0.9, 0.999, 1e-08, 0.01, 10

MESH = pl.DeviceIdType.MESH
S = jax.ShapeDtypeStruct


def _params(*sem):
    return pltpu.CompilerParams(dimension_semantics=tuple(sem) if sem else None, vmem_limit_bytes=VMEM_LIMIT)


def _sig(x):
    return 1.0 / (1.0 + jnp.exp(-x))


def _silu(x):
    return x * _sig(x)


def _dsilu(x, s):
    return s * (1.0 + x * (1.0 - s))


def _dot(a, b):
    return jnp.dot(a, b, preferred_element_type=f32)


def _dot_nt(a, b):
    return lax.dot_general(a, b, (((1,), (1,)), ((), ())), preferred_element_type=f32)


def _dot_tn(a, b):
    return lax.dot_general(a, b, (((0,), (0,)), ((), ())), preferred_element_type=f32)


def _dot3(t_bf, v):
    v1 = v.astype(bf16)
    r1 = v - v1.astype(f32)
    v2 = r1.astype(bf16)
    v3 = (r1 - v2.astype(f32)).astype(bf16)
    return _dot(t_bf, v1) + _dot(t_bf, v2) + _dot(t_bf, v3)


def _pick(n, prefs):
    for p in prefs:
        if n % p == 0:
            return p
    return n


def _full(shape):
    nd = len(shape)
    return pl.BlockSpec(shape, lambda *_: (0,) * nd)


def _matmul(a, b, out_dtype, name, tm=None, tn=None, tk=None, tb=False, comm=None):
    m, k = a.shape
    n = b.shape[0] if tb else b.shape[1]
    tm = tm if tm and m % tm == 0 else _pick(m, (768, 512, 256, 128))
    tn = tn if tn and n % tn == 0 else _pick(n, (1024, 512, 256, 128))
    tk = tk if tk and k % tk == 0 else _pick(k, (1024, 768, 512, 256, 128))
    nk = k // tk
    gi, gj = m // tm, n // tn
    carrs, modes = comm if comm else ((), ())
    nc = len(carrs)

    def kern(*refs):
        a_ref, b_ref = refs[:2]
        cins = refs[2:2 + nc]
        o_ref = refs[2 + nc]
        couts = refs[3 + nc:3 + 2 * nc]
        acc_ref = refs[3 + 2 * nc]
        sems = refs[4 + 2 * nc:]
        i, j, kk = pl.program_id(0), pl.program_id(1), pl.program_id(2)
        if nc:
            @pl.when(jnp.logical_and(jnp.logical_and(i == 0, j == 0), kk == 0))
            def _():
                _xchg_start(cins, couts, *sems, modes)

        part = _dot_nt(a_ref[...], b_ref[...]) if tb else _dot(a_ref[...], b_ref[...])
        if nk == 1:
            o_ref[...] = part.astype(o_ref.dtype)
        else:
            @pl.when(kk == 0)
            def _():
                acc_ref[...] = part

            @pl.when(kk > 0)
            def _():
                acc_ref[...] += part

            @pl.when(kk == nk - 1)
            def _():
                o_ref[...] = acc_ref[...].astype(o_ref.dtype)

        if nc:
            @pl.when(jnp.logical_and(jnp.logical_and(i == gi - 1, j == gj - 1), kk == nk - 1))
            def _():
                _xchg_wait(cins, couts, *sems, modes)

    anyspec = pl.BlockSpec(memory_space=pl.ANY)
    bspec = pl.BlockSpec((tn, tk), lambda i, j, kk: (j, kk)) if tb else pl.BlockSpec((tk, tn), lambda i, j, kk: (kk, j))
    out_shape = (S((m, n), out_dtype),) + _xchg_out_shapes(carrs, modes)
    res = pl.pallas_call(
        kern, out_shape=out_shape, grid=(gi, gj, nk),
        in_specs=[pl.BlockSpec((tm, tk), lambda i, j, kk: (i, kk)), bspec] + [anyspec] * nc,
        out_specs=(pl.BlockSpec((tm, tn), lambda i, j, kk: (i, j)),) + (anyspec,) * nc,
        scratch_shapes=[pltpu.VMEM((tm, tn), f32)] + (_xchg_sems(nc) if nc else []),
        compiler_params=_params(*((("arbitrary",) * 3) if nc else ("parallel", "parallel", "arbitrary"))), name=name)(a, b, *carrs)
    return res if nc else res[0]


def _mod_fwd(cc8, w_mod_bf, b_mod):
    def kern(c_ref, w_ref, b_ref, o_ref, s_ref):
        s = _silu(c_ref[...])
        s_ref[...] = s
        o_ref[...] = _dot(s.astype(bf16), w_ref[...]) + b_ref[...]

    return pl.pallas_call(kern, out_shape=(S((8, 3 * D), f32), S((8, D), f32)), compiler_params=_params(),
                          name="mod_fwd")(cc8, w_mod_bf, b_mod)


def _mod_bwd(ct, dmod8, w_mod_bf):
    tc = 512
    nj = 3 * D // tc

    def kern(ct_ref, dm_ref, w_ref, db_ref, dc_ref):
        j = pl.program_id(0)
        cx = ct_ref[:, 1:2]
        sx = _sig(cx)
        dmc = dm_ref[1:2, :]
        db_ref[...] = dm_ref[0:1, :] + dmc
        t = jnp.sum(w_ref[...].astype(f32) * dmc.astype(bf16).astype(f32), axis=1, keepdims=True) * _dsilu(cx, sx)

        @pl.when(j == 0)
        def _():
            dc_ref[...] = jnp.zeros_like(dc_ref)

        dc_ref[...] += jnp.broadcast_to(t, (D, 128))

    return pl.pallas_call(
        kern, out_shape=(S((1, 3 * D), f32), S((D, 128), f32)), grid=(nj,),
        in_specs=[_full((D, 128)), pl.BlockSpec((8, tc), lambda j: (0, j)), pl.BlockSpec((D, tc), lambda j: (0, j))],
        out_specs=(pl.BlockSpec((1, tc), lambda j: (0, j)), _full((D, 128))),
        compiler_params=_params("arbitrary"), name="mod_bwd")(ct, dmod8, w_mod_bf)


def _wmod_grad(sct, dmx, dmc):
    cols = dmx.shape[1]

    def kern(s_ref, dmx_ref, dmc_ref, g_ref):
        dmc_sum = dmc_ref[0:1, :]
        for d in range(1, N_DEV):
            dmc_sum = dmc_sum + dmc_ref[d:d + 1, :]
        g = s_ref[:, N_DEV:N_DEV + 1] * dmc_sum
        for d in range(N_DEV):
            g = g + s_ref[:, d:d + 1] * dmx_ref[d:d + 1, :]
        g_ref[...] = g

    return pl.pallas_call(kern, out_shape=S((D, cols), f32), compiler_params=_params(), name="wmod_grad")(sct, dmx, dmc)


def _prenorm(x, ctx, norm_w, mod):
    L, Lc = x.shape[0], ctx.shape[0]
    nlx, nt = L // RT, (L + Lc) // RT

    def kern(x_ref, c_ref, nw_ref, mod_ref, h_ref, ht_ref):
        i = pl.program_id(0)
        is_c = i >= nlx
        xv = jnp.where(is_c, c_ref[...], x_ref[...])
        shift = jnp.where(is_c, mod_ref[1:2, 0:D], mod_ref[0:1, 0:D])
        scale = jnp.where(is_c, mod_ref[1:2, D:2 * D], mod_ref[0:1, D:2 * D])
        r = lax.rsqrt(jnp.mean(xv * xv, axis=1, keepdims=True) + EPS)
        hv = (xv * r) * nw_ref[...] * (1.0 + scale) + shift
        h_ref[...] = hv.astype(bf16)
        ht_ref[...] = jnp.transpose(hv).astype(bf16)

    return pl.pallas_call(
        kern, out_shape=(S((L + Lc, D), bf16), S((D, L + Lc), bf16)), grid=(nt,),
        in_specs=[pl.BlockSpec((RT, D), lambda i: (jnp.minimum(i, nlx - 1), 0)),
                  pl.BlockSpec((RT, D), lambda i: (jnp.maximum(i - nlx, 0), 0)),
                  _full((1, D)), _full((8, 3 * D))],
        out_specs=(pl.BlockSpec((RT, D), lambda i: (i, 0)), pl.BlockSpec((D, RT), lambda i: (0, i))),
        compiler_params=_params("parallel"), name="prenorm")(x, ctx, norm_w, mod)


def _prenorm_bwd(x, ctx, dh, dx1, norm_w, mod):
    L, Lc = x.shape[0], ctx.shape[0]
    nlx, nt = L // RT, (L + Lc) // RT

    def kern(x_ref, c_ref, dh_ref, dx1_ref, nw_ref, mod_ref, gx_ref, dnw_ref, acc_ref):
        i = pl.program_id(0)
        is_c = i >= nlx

        @pl.when(i == 0)
        def _():
            dnw_ref[...] = jnp.zeros_like(dnw_ref)
            acc_ref[...] = jnp.zeros_like(acc_ref)

        xv = jnp.where(is_c, c_ref[...], x_ref[...])
        scale = jnp.where(is_c, mod_ref[1:2, D:2 * D], mod_ref[0:1, D:2 * D])
        nw = nw_ref[...]
        r = lax.rsqrt(jnp.mean(xv * xv, axis=1, keepdims=True) + EPS)
        xn = xv * r
        dh = dh_ref[...]
        dsh = jnp.sum(dh, axis=0, keepdims=True)
        dsc = jnp.sum(dh * (xn * nw), axis=0, keepdims=True)
        dxnw = dh * (1.0 + scale)
        dnw_ref[...] += jnp.sum(dxnw * xn, axis=0, keepdims=True)
        dxn = dxnw * nw
        dx = r * (dxn - xn * jnp.mean(dxn * xn, axis=1, keepdims=True))

        @pl.when(jnp.logical_not(is_c))
        def _():
            gx_ref[...] = dx1_ref[...] + dx
            acc_ref[0:1, :] += dsh
            acc_ref[1:2, :] += dsc

        @pl.when(is_c)
        def _():
            acc_ref[2:3, :] += dsh
            acc_ref[3:4, :] += dsc

    xmap = lambda i: (jnp.minimum(i, nlx - 1), 0)
    return pl.pallas_call(
        kern, out_shape=(S((L, D), f32), S((1, D), f32), S((8, D), f32)), grid=(nt,),
        in_specs=[pl.BlockSpec((RT, D), xmap), pl.BlockSpec((RT, D), lambda i: (jnp.maximum(i - nlx, 0), 0)),
                  pl.BlockSpec((RT, D), lambda i: (i, 0)), pl.BlockSpec((RT, D), xmap), _full((1, D)), _full((8, 3 * D))],
        out_specs=(pl.BlockSpec((RT, D), xmap), _full((1, D)), _full((8, D))),
        compiler_params=_params("arbitrary"), name="prenorm_bwd")(x, ctx, dh, dx1, norm_w, mod)


def _xbc_col(j):
    return jnp.where(j == 0, PX0 // CONV_CT, PBC0 // CONV_CT)


def _halo_specs(nt_rows, ct, col=lambda j: j):
    cur = pl.BlockSpec((RT, ct), lambda i, j: (i, col(j)))
    prev = pl.BlockSpec((8, ct), lambda i, j: (jnp.maximum(i * (RT // 8) - 1, 0), col(j)))
    nxt = pl.BlockSpec((8, ct), lambda i, j: (jnp.minimum((i + 1) * (RT // 8), nt_rows // 8 - 1), col(j)))
    return cur, prev, nxt


def _fill_halo(scr, cur_ref, prev_ref, next_ref, i, nlx, nt):
    prev_ok = jnp.logical_and(i != 0, i != nlx)
    next_ok = jnp.logical_and(i != nlx - 1, i != nt - 1)
    scr[0:8, :] = jnp.where(prev_ok, prev_ref[...], 0.0)
    scr[8:8 + RT, :] = cur_ref[...]
    scr[8 + RT:16 + RT, :] = jnp.where(next_ok, next_ref[...], 0.0)


CONV_RB = 32


def _conv_blocks(ct):
    return [(slice(cb * 128, (cb + 1) * 128), r0) for cb in range(ct // 128) for r0 in range(0, RT, CONV_RB)]


def _taps(scr, cs, r0, shifts):
    blk = scr[r0:r0 + CONV_RB + 16, cs]
    n = CONV_RB + 16
    return [(blk if d == 0 else pltpu.roll(blk, (-d) % n, 0))[8:8 + CONV_RB, :] for d in shifts]


def _ssm_conv_fwd(proj, w8, b, nlx, half, out_dtype, name):
    T = proj.shape[0]
    nt = T // RT
    ct = CONV_CT
    cur, prev, nxt = _halo_specs(T, ct, lambda j: _xbc_col(j + half))

    def kern(cur_ref, prev_ref, next_ref, w_ref, b_ref, o_ref, scr):
        i = pl.program_id(0)
        _fill_halo(scr, cur_ref, prev_ref, next_ref, i, nlx, nt)
        for cs, r0 in _conv_blocks(ct):
            taps = _taps(scr, cs, r0, [k - 2 for k in range(SK)])
            acc = jnp.broadcast_to(b_ref[:, cs], (CONV_RB, 128))
            for k in range(SK):
                acc = acc + w_ref[k:k + 1, cs] * taps[k]
            o_ref[r0:r0 + CONV_RB, cs] = _silu(acc).astype(out_dtype)

    return pl.pallas_call(
        kern, out_shape=S((T, ct), out_dtype), grid=(nt, 1),
        in_specs=[cur, prev, nxt, pl.BlockSpec((8, ct), lambda i, j: (0, j + half)),
                  pl.BlockSpec((1, ct), lambda i, j: (0, j + half))],
        out_specs=pl.BlockSpec((RT, ct), lambda i, j: (i, j)),
        scratch_shapes=[pltpu.VMEM((RT + 16, ct), f32)],
        compiler_params=_params("parallel", "parallel"), name=name)(proj, proj, proj, w8, b)


def _ssm_conv_dpre(dxbc, proj, w8, b, nlx):
    T = proj.shape[0]
    nt = T // RT
    ct = CONV_CT
    cur = pl.BlockSpec((RT, ct), lambda j, i: (i, j))
    pcur = pl.BlockSpec((RT, ct), lambda j, i: (i, _xbc_col(j)))
    prev = pl.BlockSpec((8, ct), lambda j, i: (jnp.maximum(i * (RT // 8) - 1, 0), _xbc_col(j)))
    nxt = pl.BlockSpec((8, ct), lambda j, i: (jnp.minimum((i + 1) * (RT // 8), T // 8 - 1), _xbc_col(j)))

    def kern(d_ref, cur_ref, prev_ref, next_ref, w_ref, b_ref, dpre_ref, dw_ref, db_ref, scr):
        i = pl.program_id(1)
        _fill_halo(scr, cur_ref, prev_ref, next_ref, i, nlx, nt)

        @pl.when(i == 0)
        def _():
            dw_ref[...] = jnp.zeros_like(dw_ref)
            db_ref[...] = jnp.zeros_like(db_ref)

        for cb in range(ct // 128):
            cs = slice(cb * 128, (cb + 1) * 128)
            db_acc = jnp.zeros((CONV_RB, 128), f32)
            dw_acc = [jnp.zeros((CONV_RB, 128), f32) for _ in range(SK)]
            for r0 in range(0, RT, CONV_RB):
                taps = _taps(scr, cs, r0, [k - 2 for k in range(SK)])
                pre = jnp.broadcast_to(b_ref[:, cs], (CONV_RB, 128))
                for k in range(SK):
                    pre = pre + w_ref[k:k + 1, cs] * taps[k]
                dpre = d_ref[r0:r0 + CONV_RB, cs] * _dsilu(pre, _sig(pre))
                dpre_ref[r0:r0 + CONV_RB, cs] = dpre
                db_acc = db_acc + dpre
                dw_acc = [dw_acc[k] + dpre * taps[k] for k in range(SK)]
            db_ref[:, cs] += jnp.sum(db_acc, axis=0, keepdims=True)
            for k in range(SK):
                dw_ref[k:k + 1, cs] += jnp.sum(dw_acc[k], axis=0, keepdims=True)

    return pl.pallas_call(
        kern, out_shape=(S((T, 4096), f32), S((8, 4096), f32), S((1, 4096), f32)), grid=(4096 // ct, nt),
        in_specs=[cur, pcur, prev, nxt, pl.BlockSpec((8, ct), lambda j, i: (0, j)), pl.BlockSpec((1, ct), lambda j, i: (0, j))],
        out_specs=(cur, pl.BlockSpec((8, ct), lambda j, i: (0, j)), pl.BlockSpec((1, ct), lambda j, i: (0, j))),
        scratch_shapes=[pltpu.VMEM((RT + 16, ct), f32)],
        compiler_params=_params("parallel", "arbitrary"), name="ssm_conv_dpre")(dxbc, proj, proj, proj, w8, b)


def _ssm_conv_t(dpre, w8, dproj, nlx):
    T = dpre.shape[0]
    nt = T // RT
    ct = CONV_CT
    cur, prev, nxt = _halo_specs(T, ct)

    def kern(cur_ref, prev_ref, next_ref, w_ref, _alias, o_ref, scr):
        i = pl.program_id(0)
        _fill_halo(scr, cur_ref, prev_ref, next_ref, i, nlx, nt)
        for cs, r0 in _conv_blocks(ct):
            taps = _taps(scr, cs, r0, [2 - k for k in range(SK)])
            acc = jnp.zeros((CONV_RB, 128), f32)
            for k in range(SK):
                acc = acc + w_ref[k:k + 1, cs] * taps[k]
            o_ref[r0:r0 + CONV_RB, cs] = acc.astype(bf16)

    return pl.pallas_call(
        kern, out_shape=S(dproj.shape, bf16), grid=(nt, 4096 // ct),
        in_specs=[cur, prev, nxt, pl.BlockSpec((8, ct), lambda i, j: (0, j)), pl.BlockSpec(memory_space=pl.ANY)],
        out_specs=pl.BlockSpec((RT, ct), lambda i, j: (i, _xbc_col(j))),
        scratch_shapes=[pltpu.VMEM((RT + 16, ct), f32)], input_output_aliases={4: 0},
        compiler_params=_params("parallel", "parallel"), name="ssm_conv_t")(dpre, dpre, dpre, w8, dproj)


DT_CH = 6


def _tri():
    li = lax.broadcasted_iota(jnp.int32, (Q, Q), 0)
    si = lax.broadcasted_iota(jnp.int32, (Q, Q), 1)
    return (si <= li).astype(bf16), (si >= li).astype(bf16)


def _dt_prep(proj, bias_row, alog_row):
    T = proj.shape[0]
    nch = T // Q
    assert nch % DT_CH == 0

    def kern(raw_ref, b_ref, al_ref, dt_ref, la_ref):
        lane = lax.broadcasted_iota(jnp.int32, (Q, 128), 1)
        a = jnp.where(lane[0:1, :] < 2 * NH, -jnp.exp(al_ref[...]), 0.0)
        tri, trit = _tri()
        for h in range(DT_CH):
            rows = slice(h * Q, (h + 1) * Q)
            v = raw_ref[rows, :] + b_ref[...]
            dt = jnp.maximum(v, 0.0) + jnp.log1p(jnp.exp(-jnp.abs(v)))
            da = dt * a
            dt_ref[rows, :] = dt
            la_ref[rows, :] = jnp.where(lane < NH, _dot3(tri, da), _dot3(trit, da))

    rq = DT_CH * Q
    return pl.pallas_call(
        kern, out_shape=(S((T, 128), f32), S((T, 128), f32)), grid=(nch // DT_CH,),
        in_specs=[pl.BlockSpec((rq, 128), lambda c: (c, DT0 // 128)), _full((1, 128)), _full((1, 128))],
        out_specs=(pl.BlockSpec((rq, 128), lambda c: (c, 0)), pl.BlockSpec((rq, 128), lambda c: (c, 0))),
        compiler_params=_params("parallel"), name="dt_prep")(proj, bias_row, alog_row)


def _dt_bwd(a1, a2, r2, sv, dt, la, proj, bias_row, alog_row, dproj):
    T = proj.shape[0]
    nch = T // Q
    assert nch % DT_CH == 0
    rq = DT_CH * Q
    blk = pl.BlockSpec((rq, 128), lambda c: (c, 0))

    def kern(a1_ref, a2_ref, r2_ref, s_ref, dt_ref, la_ref, raw_ref, b_ref, al_ref, _alias, o_ref, db_ref, dal_ref):
        c = pl.program_id(0)

        @pl.when(c == 0)
        def _():
            db_ref[...] = jnp.zeros_like(db_ref)
            dal_ref[...] = jnp.zeros_like(dal_ref)

        lane = lax.broadcasted_iota(jnp.int32, (Q, 128), 1)
        row = lax.broadcasted_iota(jnp.int32, (Q, 128), 0)
        fwd = lane < NH
        a = jnp.where(lane[0:1, :] < 2 * NH, -jnp.exp(al_ref[...]), 0.0)
        is_end = row == jnp.where(fwd, Q - 1, 0)
        tri, trit = _tri()
        o_ref[...] = jnp.zeros_like(o_ref)
        for h in range(DT_CH):
            rows = slice(h * Q, (h + 1) * Q)
            dt = dt_ref[rows, :]
            la = la_ref[rows, :]
            a2v = a2_ref[rows, :]
            r2v = r2_ref[rows, :]
            la_e = jnp.where(fwd[0:1, :], la[Q - 1:Q, :], la[0:1, :])
            e_end = jnp.exp(la_e - la)
            wend = e_end * dt
            extra = s_ref[h * Q:h * Q + 1, :] * jnp.exp(la_e) + jnp.sum(wend * a2v, axis=0, keepdims=True)
            dla = a1_ref[rows, :] - dt * r2v - wend * a2v + jnp.where(is_end, extra, 0.0)
            rcs = jnp.where(fwd, _dot3(trit, dla), _dot3(tri, dla))
            ddt = r2v + e_end * a2v + a * rcs
            dal_ref[...] += a * jnp.sum(dt * rcs, axis=0, keepdims=True)
            draw = jnp.where(lane < 2 * NH, ddt * _sig(raw_ref[rows, :] + b_ref[...]), 0.0)
            db_ref[...] += jnp.sum(draw, axis=0, keepdims=True)
            o_ref[rows, 0:128] = draw.astype(bf16)

    return pl.pallas_call(
        kern, out_shape=(S(dproj.shape, bf16), S((1, 128), f32), S((1, 128), f32)), grid=(nch // DT_CH,),
        in_specs=[blk, blk, blk, blk, blk, blk, pl.BlockSpec((rq, 128), lambda c: (c, DT0 // 128)),
                  _full((1, 128)), _full((1, 128)), pl.BlockSpec(memory_space=pl.ANY)],
        out_specs=(pl.BlockSpec((rq, NP - DT0), lambda c: (c, DT0 // (NP - DT0))), _full((1, 128)), _full((1, 128))),
        input_output_aliases={9: 0},
        compiler_params=_params("arbitrary"), name="dt_bwd")(a1, a2, r2, sv, dt, la, proj, bias_row, alog_row, dproj)


def _split2(v):
    hi = v.astype(bf16)
    lo = (v - hi.astype(f32)).astype(bf16)
    return jnp.concatenate([hi, lo], axis=1)


def _scan_consts(rev):
    hoff = NH if rev else 0
    g = jnp.arange(NG, dtype=jnp.int32)[:, None, None]

    def rc(nr, ncol):
        return jnp.arange(nr, dtype=jnp.int32)[None, :, None], jnp.arange(ncol, dtype=jnp.int32)[None, None, :]

    r, c = rc(2 * 128, HPG * HD)
    sel_w = (lax.rem(r, 128) == hoff + HPG * g + c // HD).astype(bf16)
    r, c = rc(HPG * HD, 128)
    ind_h = (c == hoff + HPG * g + r // HD).astype(bf16)
    r, c = rc(2 * HPG * Q, 128)
    ind_e = (c == hoff + HPG * g + lax.rem(r, HPG * Q) // Q).astype(bf16)
    return sel_w, ind_h, ind_e


def _masks(rev):
    li = lax.broadcasted_iota(jnp.int32, (Q, Q), 0)
    si = lax.broadcasted_iota(jnp.int32, (Q, Q), 1)
    mask = (li <= si) if rev else (li >= si)
    mask_t = (li >= si) if rev else (li <= si)
    lane = lax.broadcasted_iota(jnp.int32, (Q, HPG * HD), 1)
    hms = [jnp.logical_and(lane >= r * HD, lane < (r + 1) * HD) for r in range(HPG)]
    return mask, mask_t, hms


def _mine(hoff):
    lane = lax.broadcasted_iota(jnp.int32, (Q, 128), 1)
    return jnp.logical_and(lane >= hoff, lane < hoff + NH)


def _head_row(vals, hc0):
    lane = lax.broadcasted_iota(jnp.int32, (1, HPG * HD), 1)
    out = jnp.zeros((1, HPG * HD), f32)
    for r in range(HPG):
        out = jnp.where(jnp.logical_and(lane >= r * HD, lane < (r + 1) * HD), vals[:, hc0 + r:hc0 + r + 1], out)
    return out


SCAN_CH = 2


def _chunk_of(j, rev, nxc, nch):
    return (nch - 1 - j) if rev else lax.rem(j + nxc, nch)


def _ssd_fwd(xs, bc, dt, la, consts, rev, nxc, name, y_acc=None):
    T = xs.shape[0]
    nch = T // Q
    hoff = NH if rev else 0
    e = 0 if rev else Q - 1
    cm = lambda j: _chunk_of(j, rev, nxc // SCAN_CH, nch // SCAN_CH)
    sel_w = consts[0]
    has_acc = y_acc is not None

    def kern(*refs):
        xs_ref, bc_ref, dt_ref, la_ref, sw_ref = refs[:5]
        yacc_ref = refs[5] if has_acc else None
        y_ref, hp_ref, h_ref = refs[5 + has_acc:]
        j = pl.program_id(0)

        @pl.when(j == 0)
        def _():
            h_ref[...] = jnp.zeros_like(h_ref)

        mask, _, hms = _masks(rev)
        for hh in range(SCAN_CH):
            h = SCAN_CH - 1 - hh if rev else hh
            chunk(refs, mask, hms, h, slice(h * Q, (h + 1) * Q))

    def chunk(refs, mask, hms, h, rows):
        xs_ref, bc_ref, dt_ref, la_ref, sw_ref = refs[:5]
        yacc_ref = refs[5] if has_acc else None
        y_ref, hp_ref, h_ref = refs[5 + has_acc:]
        hp_ref[h] = h_ref[...]
        la_all = la_ref[rows, :]
        dt_all = dt_ref[rows, :]
        la_t = jnp.transpose(la_all)
        dt_t = jnp.transpose(dt_all)
        la_e = la_all[e:e + 1, :]
        w2 = _split2(jnp.exp(jnp.where(_mine(hoff), la_e - la_all, 0.0)) * dt_all)
        e2 = _split2(jnp.exp(la_all))
        ela_e = jnp.exp(la_e)
        for g in range(NG):
            hc0 = hoff + g * HPG
            x = xs_ref[rows, g * GW:(g + 1) * GW]
            bb = bc_ref[rows, g * NS:(g + 1) * NS]
            cb = bc_ref[rows, NG * NS + g * NS:NG * NS + (g + 1) * NS]
            ht = h_ref[g * NS:(g + 1) * NS, :]
            scores = _dot_nt(cb, bb)
            yoff = _dot(cb, ht.astype(bf16))
            wend = _dot(w2, sw_ref[g])
            expla = _dot(e2, sw_ref[g])
            mixes, xstack = [], []
            for r in range(HPG):
                hc = hc0 + r
                la_rep = jnp.broadcast_to(la_all[:, hc:hc + 1], (Q, 128))
                decay = jnp.exp(jnp.where(mask, la_rep - la_t[hc:hc + 1, :], NEG))
                mixes.append((scores * decay * dt_t[hc:hc + 1, :]).astype(bf16))
                xstack.append(jnp.where(hms[r], x, 0.0).astype(bf16))
            y = _dot(jnp.concatenate(mixes, axis=1), jnp.concatenate(xstack, axis=0)) + yoff * expla
            if has_acc:
                y = y + yacc_ref[rows, g * GW:(g + 1) * GW]
            y_ref[rows, g * GW:(g + 1) * GW] = y
            h_ref[g * NS:(g + 1) * NS, :] = ht * _head_row(ela_e, hc0) + _dot_tn(bb, (x * wend).astype(bf16))

    row = lambda j: (cm(j), 0)
    rq = SCAN_CH * Q
    yblk = pl.BlockSpec((rq, DI), row)
    return pl.pallas_call(
        kern, out_shape=(S((T, DI), f32), S((nch, NG * NS, HPG * HD), f32)), grid=(nch // SCAN_CH,),
        in_specs=[yblk, pl.BlockSpec((rq, 2 * NG * NS), row), pl.BlockSpec((rq, 128), row), pl.BlockSpec((rq, 128), row),
                  _full(sel_w.shape)] + ([yblk] if has_acc else []),
        out_specs=(yblk, pl.BlockSpec((SCAN_CH, NG * NS, HPG * HD), lambda j: (cm(j), 0, 0))),
        scratch_shapes=[pltpu.VMEM((NG * NS, HPG * HD), f32)],
        input_output_aliases={5: 0} if has_acc else {},
        compiler_params=_params("arbitrary"), name=name)(xs, bc, dt, la, sel_w, *([y_acc] if has_acc else []))


def _ssd_bwd(xs, bc, dy, dt, la, hprev, dskip_full, consts, rev, nxc, name, acc=None):
    T = xs.shape[0]
    nch = T // Q
    hoff = NH if rev else 0
    e = 0 if rev else Q - 1
    npair = nch // SCAN_CH
    cm = lambda j: _chunk_of(npair - 1 - j, rev, nxc // SCAN_CH, npair)
    has_acc = acc is not None
    sel_w, ind_h, ind_e = consts

    def kern(*refs):
        g_ref = refs[-2]
        j = pl.program_id(0)

        @pl.when(j == 0)
        def _():
            g_ref[...] = jnp.zeros_like(g_ref)

        masks = _masks(rev)
        for hh in range(SCAN_CH):
            h = hh if rev else SCAN_CH - 1 - hh
            chunk(refs, masks, h, slice(h * Q, (h + 1) * Q))

    def chunk(refs, masks, h, rows):
        xs_ref, bc_ref, dy_ref, dt_ref, la_ref, hp_ref, dsk_ref, sw_ref, ih_ref, ie_ref = refs[:10]
        k = 10
        if has_acc:
            dxbc_in, a1_in, a2_in, r2_in, s_in = refs[k:k + 5]
            k += 5
        dxbc_ref, a1_ref, a2_ref, r2_ref, s_ref, g_ref, r2scr = refs[k:k + 7]
        mask, mask_t, hms = masks
        la_all = la_ref[rows, :]
        dt_all = dt_ref[rows, :]
        la_t = jnp.transpose(la_all)
        dt_t = jnp.transpose(dt_all)
        la_e = la_all[e:e + 1, :]
        w2 = _split2(jnp.exp(jnp.where(_mine(hoff), la_e - la_all, 0.0)) * dt_all)
        e2 = _split2(jnp.exp(la_all))
        wed2 = jnp.concatenate([w2, e2, _split2(dt_all)], axis=0)
        ela_e = jnp.exp(la_e)
        r2scr[...] = jnp.zeros_like(r2scr)
        a1acc = jnp.zeros((Q, 128), f32)
        a2acc = jnp.zeros((Q, 128), f32)
        sacc = jnp.zeros((1, 128), f32)
        for g in range(NG):
            hc0 = hoff + g * HPG
            x = xs_ref[rows, g * GW:(g + 1) * GW]
            bb = bc_ref[rows, g * NS:(g + 1) * NS]
            cb = bc_ref[rows, NG * NS + g * NS:NG * NS + (g + 1) * NS]
            dyv = dy_ref[rows, g * GW:(g + 1) * GW]
            gt = g_ref[g * NS:(g + 1) * NS, :]
            ht = hp_ref[h, g * NS:(g + 1) * NS, :]
            gtb = gt.astype(bf16)
            htb = ht.astype(bf16)
            xb = x.astype(bf16)
            scores = _dot_nt(cb, bb)
            scores_t = _dot_nt(bb, cb)
            bg = _dot(bb, gtb)
            yoff = _dot(cb, htb)
            sel3 = _dot(wed2, sw_ref[g])
            wend, expla, dtf = sel3[0:Q], sel3[Q:2 * Q], sel3[2 * Q:3 * Q]
            dym = jnp.concatenate([jnp.where(hms[r], dyv, 0.0).astype(bf16) for r in range(HPG)], axis=0)
            dyx_all = _dot_nt(dym, xb)
            sdts, ems = [], []
            wsum = jnp.zeros((Q, Q), f32)
            for r in range(HPG):
                hc = hc0 + r
                la_rep = jnp.broadcast_to(la_all[:, hc:hc + 1], (Q, 128))
                la_r = la_t[hc:hc + 1, :]
                dt_r = dt_t[hc:hc + 1, :]
                decay = jnp.exp(jnp.where(mask, la_rep - la_r, NEG))
                decay_t = jnp.exp(jnp.where(mask_t, la_r - la_rep, NEG))
                dyx = dyx_all[r * Q:(r + 1) * Q, :]
                fm = dyx * (scores * decay)
                r2scr[hc:hc + 1, :] = jnp.sum(fm, axis=0, keepdims=True)
                ems.append(fm * dt_r)
                wsum = wsum + dyx * decay * dt_r
                sdts.append((scores_t * decay_t).astype(bf16))
            dx = dtf * _dot(jnp.concatenate(sdts, axis=1), dym) + wend * bg
            if not has_acc:
                dx = dx + dsk_ref[:, g * GW:(g + 1) * GW] * dyv
            red3 = _dot(jnp.concatenate([(dyv * yoff * expla).astype(bf16), (x * bg).astype(bf16), (gt * ht).astype(bf16)],
                                        axis=0), ih_ref[g])
            a1acc = a1acc + _dot(_split2(jnp.concatenate(ems, axis=1)), ie_ref[g]) + red3[0:Q]
            a2acc = a2acc + red3[Q:2 * Q]
            sacc = sacc + jnp.sum(red3[2 * Q:3 * Q], axis=0, keepdims=True)
            wb = wsum.astype(bf16)
            dysb = (dyv * expla).astype(bf16)
            dc = _dot(wb, bb) + _dot_nt(dysb, htb)
            db = _dot_tn(wb, cb) + _dot_nt((x * wend).astype(bf16), gtb)
            g_ref[g * NS:(g + 1) * NS, :] = gt * _head_row(ela_e, hc0) + _dot_tn(cb, dysb)
            if has_acc:
                dx = dx + dxbc_in[rows, g * GW:(g + 1) * GW]
                db = db + dxbc_in[rows, B0 + g * NS:B0 + (g + 1) * NS]
                dc = dc + dxbc_in[rows, C0 + g * NS:C0 + (g + 1) * NS]
            dxbc_ref[rows, g * GW:(g + 1) * GW] = dx
            dxbc_ref[rows, B0 + g * NS:B0 + (g + 1) * NS] = db
            dxbc_ref[rows, C0 + g * NS:C0 + (g + 1) * NS] = dc
        r2c = jnp.transpose(r2scr[...])
        sc = jnp.broadcast_to(sacc, (Q, 128))
        if has_acc:
            a1acc = a1acc + a1_in[rows, :]
            a2acc = a2acc + a2_in[rows, :]
            r2c = r2c + r2_in[rows, :]
            sc = sc + s_in[rows, :]
        a1_ref[rows, :] = a1acc
        a2_ref[rows, :] = a2acc
        r2_ref[rows, :] = r2c
        s_ref[rows, :] = sc

    rq = SCAN_CH * Q
    blk = pl.BlockSpec((rq, 128), lambda j: (cm(j), 0))
    big = pl.BlockSpec((rq, 4096), lambda j: (cm(j), 0))
    wide = pl.BlockSpec((rq, DI), lambda j: (cm(j), 0))
    in_specs = [wide, pl.BlockSpec((rq, 2 * NG * NS), lambda j: (cm(j), 0)), wide, blk, blk,
                pl.BlockSpec((SCAN_CH, NG * NS, HPG * HD), lambda j: (cm(j), 0, 0)), _full((1, DI)),
                _full(sel_w.shape), _full(ind_h.shape), _full(ind_e.shape)]
    args = [xs, bc, dy, dt, la, hprev, dskip_full, sel_w, ind_h, ind_e]
    aliases = {}
    if has_acc:
        in_specs += [big, blk, blk, blk, blk]
        args += list(acc)
        aliases = {10: 0, 11: 1, 12: 2, 13: 3, 14: 4}
    return pl.pallas_call(
        kern, out_shape=(S((T, 4096), f32), S((T, 128), f32), S((T, 128), f32), S((T, 128), f32), S((T, 128), f32)),
        grid=(npair,), in_specs=in_specs, out_specs=(big, blk, blk, blk, blk),
        scratch_shapes=[pltpu.VMEM((NG * NS, HPG * HD), f32), pltpu.VMEM((128, Q), f32)],
        input_output_aliases=aliases,
        compiler_params=_params("arbitrary"), name=name)(*args)


def _ynorm_fwd(ysum, xs, proj, dskip_full, nw, L):
    nlx = L // RT

    def kern(ys_ref, xs_ref, za_ref, zb_ref, dsk_ref, nw_ref, y_ref, yn_ref, ynt_ref):
        y = ys_ref[...] + dsk_ref[...] * xs_ref[...]
        y_ref[...] = y
        hg = NG // 2
        for g in range(NG):
            z_ref = za_ref if g < hg else zb_ref
            sl = y[:, g * GW:(g + 1) * GW] * _silu(z_ref[:, (g % hg) * GW:(g % hg + 1) * GW])
            r = lax.rsqrt(jnp.mean(sl * sl, axis=1, keepdims=True) + EPS)
            yn = (sl * r) * nw_ref[:, g * GW:(g + 1) * GW]
            yn_ref[:, g * GW:(g + 1) * GW] = yn.astype(bf16)
            ynt_ref[g * GW:(g + 1) * GW, :] = jnp.transpose(yn).astype(bf16)

    blk = pl.BlockSpec((RT, DI), lambda i: (i, 0))
    return pl.pallas_call(
        kern, out_shape=(S((L, DI), f32), S((L, DI), bf16), S((DI, L), bf16)), grid=(nlx,),
        in_specs=[blk, blk, pl.BlockSpec((RT, DI // 2), lambda i: (i, Z0 // (DI // 2))),
                  pl.BlockSpec((RT, DI // 2), lambda i: (i, Z0 // (DI // 2) + 1)), _full((1, DI)), _full((1, DI))],
        out_specs=(blk, blk, pl.BlockSpec((DI, RT), lambda i: (0, i))),
        compiler_params=_params("parallel"), name="ynorm_fwd")(ysum, xs, proj, proj, dskip_full, nw)


def _ynorm_bwd(dyn, y, xs, proj, dskip_full, nw, dproj):
    L = y.shape[0]
    T = proj.shape[0]
    nlx, nt = L // RT, T // RT

    hw = DI // 2

    def kern(dyn_ref, y_ref, xs_ref, z_ref, dsk_ref, nw_ref, _alias, dz_ref, dy_ref, dnw_ref, dsk_acc):
        i = pl.program_id(1)

        @pl.when(i == 0)
        def _():
            dnw_ref[...] = jnp.zeros_like(dnw_ref)
            dsk_acc[...] = jnp.zeros_like(dsk_acc)

        @pl.when(i >= nlx)
        def _():
            dz_ref[...] = jnp.zeros_like(dz_ref)
            dy_ref[...] = jnp.zeros_like(dy_ref)

        @pl.when(i < nlx)
        def _():
            y = y_ref[...]
            z = z_ref[...]
            sz = _sig(z)
            gz = z * sz
            yz = y * gz
            dynv = dyn_ref[...]
            for g in range(hw // GW):
                cs = slice(g * GW, (g + 1) * GW)
                sl = yz[:, cs]
                r = lax.rsqrt(jnp.mean(sl * sl, axis=1, keepdims=True) + EPS)
                yhat = sl * r
                dn = dynv[:, cs]
                dnw_ref[:, cs] += jnp.sum(dn * yhat, axis=0, keepdims=True)
                dyh = dn * nw_ref[:, cs]
                dyz = r * (dyh - yhat * jnp.mean(dyh * yhat, axis=1, keepdims=True))
                dyv = dyz * gz[:, cs]
                dy_ref[:, cs] = dyv
                dz_ref[:, cs] = (dyz * y[:, cs] * _dsilu(z[:, cs], sz[:, cs])).astype(bf16)
                dsk_acc[:, cs] += jnp.sum(dyv * xs_ref[:, cs], axis=0, keepdims=True)

    xblk = pl.BlockSpec((RT, hw), lambda j, i: (jnp.minimum(i, nlx - 1), j))
    row = pl.BlockSpec((1, hw), lambda j, i: (0, j))
    return pl.pallas_call(
        kern, out_shape=(S(dproj.shape, bf16), S((T, DI), f32), S((1, DI), f32), S((1, DI), f32)), grid=(2, nt),
        in_specs=[xblk, xblk, xblk, pl.BlockSpec((RT, hw), lambda j, i: (jnp.minimum(i, nlx - 1), Z0 // hw + j)), row, row,
                  pl.BlockSpec(memory_space=pl.ANY)],
        out_specs=(pl.BlockSpec((RT, hw), lambda j, i: (i, Z0 // hw + j)), pl.BlockSpec((RT, hw), lambda j, i: (i, j)), row, row),
        input_output_aliases={6: 0},
        compiler_params=_params("arbitrary", "arbitrary"), name="ynorm_bwd")(dyn, y, xs, proj, dskip_full, nw, dproj)


def _head_sums(cols):
    def kern(c_ref, o_ref):
        o_ref[...] = jnp.broadcast_to(jnp.sum(c_ref[...], axis=1, keepdims=True), (NH, 128))

    return pl.pallas_call(kern, out_shape=S((NH, 128), f32), name="head_sums")(cols)


SEG_STRIDE = 96
SEG_PAD = 16
NSEG = RT // GRID_W
CONF_ROWS = SEG_PAD + NSEG * SEG_STRIDE


SHIFT_ROWS = CONF_ROWS - 8
CONF_CW = 256


CONF_RB = 32


def _seg_zero_pads(scr):
    scr[0:SEG_PAD, :] = jnp.zeros((SEG_PAD, scr.shape[1]), f32)
    for s in range(NSEG):
        lo = SEG_PAD + s * SEG_STRIDE + GRID_W
        scr[lo:lo + SEG_STRIDE - GRID_W, :] = jnp.zeros((SEG_STRIDE - GRID_W, scr.shape[1]), f32)


def _seg_row(r0):
    return SEG_PAD + (r0 // GRID_W) * SEG_STRIDE + r0 % GRID_W


def _shift_copies(cps, scr, cs):
    full = scr[:, cs]
    for s in range(1, 8):
        cps[s - 1, :, :] = pltpu.roll(full, CONF_ROWS - s, 0)[0:SHIFT_ROWS, :]


def _tap(cps, scr, cs, o):
    rs = o % 8
    return scr[pl.ds(o, GRID_W), cs] if rs == 0 else cps[rs - 1, pl.ds(o - rs, GRID_W), :]


def _conf_fwd(proj, w32, cb, lnw, lnb, L):
    nlx = L // RT

    def kern(v_ref, g_ref, cg_ref, w_ref, cb_ref, lnw_ref, lnb_ref, u1_ref, u3_ref, u3t_ref, scr, cps, u3_scr):
        _seg_zero_pads(scr)
        for r0 in range(0, RT, CONF_RB):
            rows = slice(r0, r0 + CONF_RB)
            scr[_seg_row(r0):_seg_row(r0) + CONF_RB, :] = v_ref[rows, :] * _sig(g_ref[rows, :])
        for cc in range(D // CONF_CW):
            cs = slice(cc * CONF_CW, (cc + 1) * CONF_CW)
            _shift_copies(cps, scr, cs)
            for s in range(NSEG):
                acc = jnp.broadcast_to(cb_ref[:, cs], (GRID_W, CONF_CW))
                for k in range(CK):
                    acc = acc + w_ref[k:k + 1, cs] * _tap(cps, scr, cs, SEG_PAD + s * SEG_STRIDE + k - CK // 2)
                u1_ref[s * GRID_W:(s + 1) * GRID_W, cs] = acc
        for r0 in range(0, RT, CONF_RB):
            rows = slice(r0, r0 + CONF_RB)
            u1 = u1_ref[rows, :]
            xc = u1 - jnp.mean(u1, axis=1, keepdims=True)
            r = lax.rsqrt(jnp.mean(xc * xc, axis=1, keepdims=True) + EPS)
            u2 = (xc * r) * lnw_ref[...] + lnb_ref[...]
            u3 = _silu(u2) * _silu(cg_ref[rows, :])
            u3_ref[rows, :] = u3.astype(bf16)
            u3_scr[rows, :] = u3
        u3t_ref[...] = jnp.transpose(u3_scr[...]).astype(bf16)

    blk = pl.BlockSpec((RT, D), lambda i: (i, 0))
    return pl.pallas_call(
        kern, out_shape=(S((L, D), f32), S((L, D), bf16), S((D, L), bf16)), grid=(nlx,),
        in_specs=[pl.BlockSpec((RT, D), lambda i: (i, GV0 // D)), pl.BlockSpec((RT, D), lambda i: (i, GG0 // D)),
                  pl.BlockSpec((RT, D), lambda i: (i, CG0 // D)), _full((32, D)), _full((1, D)), _full((1, D)), _full((1, D))],
        out_specs=(blk, blk, pl.BlockSpec((D, RT), lambda i: (0, i))),
        scratch_shapes=[pltpu.VMEM((CONF_ROWS, D), f32), pltpu.VMEM((7, SHIFT_ROWS, CONF_CW), f32), pltpu.VMEM((RT, D), f32)],
        compiler_params=_params("parallel"), name="conf_fwd")(proj, proj, proj, w32, cb, lnw, lnb)


def _conf_bwd(du3, u1, proj, w32, lnw, lnb, dproj):
    L = u1.shape[0]
    T = proj.shape[0]
    nlx, nt = L // RT, T // RT

    def kern(du3_ref, u1_ref, v_ref, g_ref, cg_ref, w_ref, lnw_ref, lnb_ref, _alias,
             o_ref, dw_ref, dcb_ref, dlw_ref, dlb_ref, scr_u, scr_d, du0_scr, cps_u, cps_d):
        i = pl.program_id(0)

        @pl.when(i == 0)
        def _():
            dw_ref[...] = jnp.zeros_like(dw_ref)
            dcb_ref[...] = jnp.zeros_like(dcb_ref)
            dlw_ref[...] = jnp.zeros_like(dlw_ref)
            dlb_ref[...] = jnp.zeros_like(dlb_ref)

        @pl.when(i >= nlx)
        def _():
            o_ref[...] = jnp.zeros_like(o_ref)

        @pl.when(i < nlx)
        def _():
            _seg_zero_pads(scr_u)
            _seg_zero_pads(scr_d)
            for r0 in range(0, RT, CONF_RB):
                rows = slice(r0, r0 + CONF_RB)
                cg = cg_ref[rows, :]
                scg = _sig(cg)
                u1 = u1_ref[rows, :]
                xc = u1 - jnp.mean(u1, axis=1, keepdims=True)
                r = lax.rsqrt(jnp.mean(xc * xc, axis=1, keepdims=True) + EPS)
                xhat = xc * r
                u2 = xhat * lnw_ref[...] + lnb_ref[...]
                s2 = _sig(u2)
                du3v = du3_ref[rows, :]
                du2 = du3v * (cg * scg) * _dsilu(u2, s2)
                o_ref[rows, 2 * D:3 * D] = (du3v * (u2 * s2) * _dsilu(cg, scg)).astype(bf16)
                dlw_ref[...] += jnp.sum(du2 * xhat, axis=0, keepdims=True)
                dlb_ref[...] += jnp.sum(du2, axis=0, keepdims=True)
                dxh = du2 * lnw_ref[...]
                du1 = r * (dxh - jnp.mean(dxh, axis=1, keepdims=True) - xhat * jnp.mean(dxh * xhat, axis=1, keepdims=True))
                dcb_ref[...] += jnp.sum(du1, axis=0, keepdims=True)
                scr_u[_seg_row(r0):_seg_row(r0) + CONF_RB, :] = v_ref[rows, :] * _sig(g_ref[rows, :])
                scr_d[_seg_row(r0):_seg_row(r0) + CONF_RB, :] = du1
            for cc in range(D // CONF_CW):
                cs = slice(cc * CONF_CW, (cc + 1) * CONF_CW)
                _shift_copies(cps_u, scr_u, cs)
                _shift_copies(cps_d, scr_d, cs)
                for k in range(CK):
                    t = jnp.zeros((GRID_W, CONF_CW), f32)
                    for s in range(NSEG):
                        base = SEG_PAD + s * SEG_STRIDE
                        t = t + scr_d[pl.ds(base, GRID_W), cs] * _tap(cps_u, scr_u, cs, base + k - CK // 2)
                    dw_ref[k:k + 1, cs] += jnp.sum(t, axis=0, keepdims=True)
                for s in range(NSEG):
                    base = SEG_PAD + s * SEG_STRIDE
                    acc = jnp.zeros((GRID_W, CONF_CW), f32)
                    for k in range(CK):
                        acc = acc + w_ref[k:k + 1, cs] * _tap(cps_d, scr_d, cs, base + CK // 2 - k)
                    du0_scr[s * GRID_W:(s + 1) * GRID_W, cs] = acc
            for r0 in range(0, RT, CONF_RB):
                rows = slice(r0, r0 + CONF_RB)
                du0 = du0_scr[rows, :]
                sg = _sig(g_ref[rows, :])
                o_ref[rows, 0:D] = (du0 * sg).astype(bf16)
                o_ref[rows, D:2 * D] = (du0 * v_ref[rows, :] * sg * (1.0 - sg)).astype(bf16)

    xmap = lambda i: (jnp.minimum(i, nlx - 1), 0)
    pmap = lambda cb: (lambda i: (jnp.minimum(i, nlx - 1), cb))
    return pl.pallas_call(
        kern, out_shape=(S(dproj.shape, bf16), S((32, D), f32), S((1, D), f32), S((1, D), f32), S((1, D), f32)), grid=(nt,),
        in_specs=[pl.BlockSpec((RT, D), xmap), pl.BlockSpec((RT, D), xmap),
                  pl.BlockSpec((RT, D), pmap(GV0 // D)), pl.BlockSpec((RT, D), pmap(GG0 // D)), pl.BlockSpec((RT, D), pmap(CG0 // D)),
                  _full((32, D)), _full((1, D)), _full((1, D)), pl.BlockSpec(memory_space=pl.ANY)],
        out_specs=(pl.BlockSpec((RT, 3 * D), lambda i: (i, GV0 // (3 * D))), _full((32, D)), _full((1, D)), _full((1, D)), _full((1, D))),
        scratch_shapes=[pltpu.VMEM((CONF_ROWS, D), f32), pltpu.VMEM((CONF_ROWS, D), f32), pltpu.VMEM((RT, D), f32),
                        pltpu.VMEM((7, SHIFT_ROWS, CONF_CW), f32), pltpu.VMEM((7, SHIFT_ROWS, CONF_CW), f32)],
        input_output_aliases={8: 0},
        compiler_params=_params("arbitrary"), name="conf_bwd")(du3, u1, proj, proj, proj, w32, lnw, lnb, dproj)


def _merge_fwd(bs, bc, proj):
    L = bs.shape[0]

    def kern(bs_ref, bc_ref, g1_ref, g2_ref, o_ref, ot_ref):
        mv = _sig(g1_ref[...]) * bs_ref[...] + _sig(g2_ref[...]) * bc_ref[...]
        o_ref[...] = mv.astype(bf16)
        ot_ref[...] = jnp.transpose(mv).astype(bf16)

    rt = _pick(L, (2 * RT, RT))
    blk = pl.BlockSpec((rt, D), lambda i: (i, 0))
    return pl.pallas_call(
        kern, out_shape=(S((L, D), bf16), S((D, L), bf16)), grid=(L // rt,),
        in_specs=[blk, blk, pl.BlockSpec((rt, D), lambda i: (i, G10 // D)), pl.BlockSpec((rt, D), lambda i: (i, G20 // D))],
        out_specs=(blk, pl.BlockSpec((D, rt), lambda i: (0, i))),
        compiler_params=_params("parallel"), name="merge_fwd")(bs, bc, proj, proj)


def _merge_bwd(dmerged, bs, bc, proj):
    L = bs.shape[0]
    T = proj.shape[0]
    nlx, nt = L // RT, T // RT

    def kern(dm_ref, bs_ref, bc_ref, g1_ref, g2_ref, o_ref, dbs_ref, dbc_ref):
        i = pl.program_id(0)

        @pl.when(i >= nlx)
        def _():
            o_ref[...] = jnp.zeros_like(o_ref)

        @pl.when(i < nlx)
        def _():
            dm = dm_ref[...]
            s1 = _sig(g1_ref[...])
            s2 = _sig(g2_ref[...])
            dbs_ref[...] = (dm * s1).astype(bf16)
            dbc_ref[...] = (dm * s2).astype(bf16)
            o_ref[:, 0:D] = (dm * bs_ref[...] * s1 * (1.0 - s1)).astype(bf16)
            o_ref[:, D:2 * D] = (dm * bc_ref[...] * s2 * (1.0 - s2)).astype(bf16)

    xmap = lambda i: (jnp.minimum(i, nlx - 1), 0)
    pmap = lambda cb: (lambda i: (jnp.minimum(i, nlx - 1), cb))
    xblk = pl.BlockSpec((RT, D), xmap)
    return pl.pallas_call(
        kern, out_shape=(S((T, NP), bf16), S((L, D), bf16), S((L, D), bf16)), grid=(nt,),
        in_specs=[xblk, xblk, xblk, pl.BlockSpec((RT, D), pmap(G10 // D)), pl.BlockSpec((RT, D), pmap(G20 // D))],
        out_specs=(pl.BlockSpec((RT, 2 * D), lambda i: (i, G10 // (2 * D))), xblk, xblk),
        compiler_params=_params("arbitrary"), name="merge_bwd")(dmerged, bs, bc, proj, proj)


def _final(x, out, target, mod, fw):
    L = x.shape[0]

    def kern(x_ref, o_ref, t_ref, mod_ref, fw_ref, dx1_ref, dout_ref, loss_ref, dfw_ref, dg_ref):
        i = pl.program_id(0)

        @pl.when(i == 0)
        def _():
            loss_ref[...] = jnp.zeros_like(loss_ref)
            dfw_ref[...] = jnp.zeros_like(dfw_ref)
            dg_ref[...] = jnp.zeros_like(dg_ref)

        gate = mod_ref[0:1, 2 * D:3 * D]
        ov = o_ref[...]
        x1 = x_ref[...] + gate * ov
        r = lax.rsqrt(jnp.mean(x1 * x1, axis=1, keepdims=True) + EPS)
        xn = x1 * r
        fw = fw_ref[...]
        err = xn * fw - t_ref[...]
        part = 0.5 * jnp.sum(jnp.mean(err * err, axis=1, keepdims=True), axis=0, keepdims=True)
        loss_ref[...] += jnp.broadcast_to(part, (8, 128))
        dy = err * (1.0 / D)
        dfw_ref[...] += jnp.sum(dy * xn, axis=0, keepdims=True)
        dyw = dy * fw
        dx1 = r * (dyw - xn * jnp.mean(dyw * xn, axis=1, keepdims=True))
        dx1_ref[...] = dx1
        dout_ref[...] = (gate * dx1).astype(bf16)
        dg_ref[...] += jnp.sum(dx1 * ov, axis=0, keepdims=True)

    rt = _pick(L, (2 * RT, RT))
    blk = pl.BlockSpec((rt, D), lambda i: (i, 0))
    return pl.pallas_call(
        kern, out_shape=(S((L, D), f32), S((L, D), bf16), S((8, 128), f32), S((1, D), f32), S((1, D), f32)), grid=(L // rt,),
        in_specs=[blk, blk, blk, _full((8, 3 * D)), _full((1, D))],
        out_specs=(blk, blk, _full((8, 128)), _full((1, D)), _full((1, D))),
        compiler_params=_params("arbitrary"), name="final")(x, out, target, mod, fw)


def _me():
    return 4 * lax.axis_index("x") + 2 * lax.axis_index("y") + lax.axis_index("c")


def _xchg_copy(ins, outs, send_sems, recv_sems, modes, a, k, me):
    peer = lax.rem(me + k, N_DEV)
    pid = (peer // 4, lax.rem(peer // 2, 2), lax.rem(peer, 2))
    src = ins[a].at[peer] if modes[a] else ins[a]
    return pltpu.make_async_remote_copy(src_ref=src, dst_ref=outs[a].at[me], send_sem=send_sems.at[a, k - 1],
                                        recv_sem=recv_sems.at[a, k - 1], device_id=pid, device_id_type=MESH)


def _xchg_local(ins, outs, loc_sems, modes, a, me):
    return pltpu.make_async_copy(ins[a].at[me] if modes[a] else ins[a], outs[a].at[me], loc_sems.at[a])


def _xchg_start(ins, outs, send_sems, recv_sems, loc_sems, modes):
    me = _me()
    for a in range(len(modes)):
        _xchg_local(ins, outs, loc_sems, modes, a, me).start()
        for k in range(1, N_DEV):
            _xchg_copy(ins, outs, send_sems, recv_sems, modes, a, k, me).start()


def _xchg_wait(ins, outs, send_sems, recv_sems, loc_sems, modes):
    me = _me()
    for a in range(len(modes)):
        for k in range(1, N_DEV):
            frm = lax.rem(me + N_DEV - k, N_DEV)
            src = ins[a].at[frm] if modes[a] else ins[a]
            pltpu.make_async_remote_copy(src_ref=src, dst_ref=outs[a].at[frm], send_sem=send_sems.at[a, k - 1],
                                         recv_sem=recv_sems.at[a, k - 1], device_id=(0, 0, 0), device_id_type=MESH).wait_recv()
    for a in range(len(modes)):
        for k in range(1, N_DEV):
            _xchg_copy(ins, outs, send_sems, recv_sems, modes, a, k, me).wait_send()
        _xchg_local(ins, outs, loc_sems, modes, a, me).wait()


def _xchg_out_shapes(arrs, modes):
    return tuple(S((N_DEV,) + (a.shape[1:] if sc else a.shape), a.dtype) for a, sc in zip(arrs, modes))


def _xchg_sems(n):
    return [pltpu.SemaphoreType.DMA((n, N_DEV - 1)), pltpu.SemaphoreType.DMA((n, N_DEV - 1)), pltpu.SemaphoreType.DMA((n,))]


def _exchange(arrs, modes, name):
    n = len(arrs)

    def kern(*refs):
        ins, outs, sems = refs[:n], refs[n:2 * n], refs[2 * n:]
        _xchg_start(ins, outs, *sems, modes)
        _xchg_wait(ins, outs, *sems, modes)

    anyspec = pl.BlockSpec(memory_space=pl.ANY)
    return pl.pallas_call(
        kern, out_shape=_xchg_out_shapes(arrs, modes), in_specs=[anyspec] * n, out_specs=tuple([anyspec] * n),
        scratch_shapes=_xchg_sems(n), name=name)(*arrs)


def _gather2(arrs, name):
    n = len(arrs)

    def kern(*refs):
        ins, outs = refs[:n], refs[n:2 * n]
        send_sems, recv_sems, loc_sems = refs[2 * n:]
        x, y, c = lax.axis_index("x"), lax.axis_index("y"), lax.axis_index("c")
        me, sib = (x, y, c), (x, y, 1 - c)
        chips = [(1 - x, y), (x, 1 - y), (1 - x, 1 - y)]

        def slot(a, p):
            return outs[a].at[4 * p[0] + 2 * p[1] + p[2]]

        def cp(a, k, block, to, own=False):
            return pltpu.make_async_remote_copy(src_ref=ins[a] if own else slot(a, block), dst_ref=slot(a, block),
                                                send_sem=send_sems.at[a, k], recv_sem=recv_sems.at[a, k],
                                                device_id=to, device_id_type=MESH)

        started = []
        for a in range(n):
            pltpu.make_async_copy(ins[a], slot(a, me), loc_sems.at[a]).start()
            started.append(cp(a, 0, me, sib, own=True))
            started += [cp(a, 1 + j, me, (*chips[j], c), own=True) for j in range(2)]
        for s in started:
            s.start()
        for j in range(2):
            for a in range(n):
                cp(a, 1 + j, (*chips[j], c), me).wait_recv()
                fwd = cp(a, 4 + j, (*chips[j], c), sib)
                fwd.start()
                started.append(fwd)

            @pl.when(c == j)
            def _():
                for a in range(n):
                    cp(a, 3, (*chips[j], c), (*chips[1 - j], c)).start()
        for a in range(n):
            cp(a, 3, (*chips[2], c), me).wait_recv()
            fwd = cp(a, 6, (*chips[2], c), sib)
            fwd.start()
            started.append(fwd)
        for a in range(n):
            cp(a, 0, sib, me).wait_recv()
            for j in range(3):
                cp(a, 4 + j, (*chips[j], 1 - c), me).wait_recv()
        for s in started:
            s.wait_send()
        for a in range(n):
            cp(a, 3, me, me).wait_send()
            pltpu.make_async_copy(ins[a], slot(a, me), loc_sems.at[a]).wait()

    anyspec = pl.BlockSpec(memory_space=pl.ANY)
    return pl.pallas_call(
        kern, out_shape=_xchg_out_shapes(arrs, (False,) * n), in_specs=[anyspec] * n, out_specs=tuple([anyspec] * n),
        scratch_shapes=[pltpu.SemaphoreType.DMA((n, 7)), pltpu.SemaphoreType.DMA((n, 7)), pltpu.SemaphoreType.DMA((n,))],
        name=name)(*arrs)


def _adamw(parts, w, m, v, name):
    r, c = w.shape
    n_parts = parts.shape[0]
    tr = r
    for cand in (128, 64, 32, 16, 8):
        if r % cand == 0 and r > cand:
            tr = cand
            break
    c1 = 1.0 / (1.0 - ADAM_B1 ** ADAM_STEP)
    c2 = 1.0 / (1.0 - ADAM_B2 ** ADAM_STEP)

    def kern(p_ref, w_ref, m_ref, v_ref, g_ref, d_ref, m2_ref, v2_ref):
        g = p_ref[0].astype(f32)
        for i in range(1, n_parts):
            g = g + p_ref[i].astype(f32)
        g_ref[...] = g
        m2 = ADAM_B1 * m_ref[...] + (1.0 - ADAM_B1) * g
        v2 = ADAM_B2 * v_ref[...] + (1.0 - ADAM_B2) * (g * g)
        m2_ref[...] = m2
        v2_ref[...] = v2
        d_ref[...] = -ADAM_LR * ((m2 * c1) / (jnp.sqrt(v2 * c2) + ADAM_EPS) + ADAM_WD * w_ref[...])

    blk = pl.BlockSpec((tr, c), lambda i: (i, 0))
    sh = S((r, c), f32)
    return pl.pallas_call(
        kern, out_shape=(sh, sh, sh, sh), grid=(r // tr,),
        in_specs=[pl.BlockSpec((n_parts, tr, c), lambda i: (0, i, 0)), blk, blk, blk], out_specs=(blk, blk, blk, blk),
        compiler_params=_params("parallel"), name=name)(parts, w, m, v)


_SMALL = (("c_ctx", 1024), ("b_mod", 3072), ("norm_w", 1024), ("ssm_conv_b", 4096), ("dt_bias", 64), ("a_log", 64),
          ("d_skip", 32), ("ssm_norm_w", 2048), ("conf_conv_b", 1024), ("conf_ln_w", 1024), ("conf_ln_b", 1024),
          ("final_norm_w", 1024))
SMALL_TILE = 8 * 128


def _pack_small(d):
    rows = []
    for name, n in _SMALL:
        v = d[name].reshape(-1).astype(f32)
        pad = (-n) % SMALL_TILE
        if pad:
            v = jnp.concatenate([v, jnp.zeros((pad,), f32)])
        rows.append(v.reshape(-1, 128))
    return jnp.concatenate(rows, axis=0)


def _unpack_small(p, shapes):
    out, r0 = {}, 0
    for name, n in _SMALL:
        nr = 8 * ((n + SMALL_TILE - 1) // SMALL_TILE)
        out[name] = p[r0:r0 + nr].reshape(-1)[:n].reshape(shapes[name])
        r0 += nr
    return out


def _permute_w_in(w):
    return jnp.concatenate([w[:, 9280:11328], w[:, 2048:4096], w[:, 0:2048], w[:, 6208:9280], w[:, 4160:6208],
                            w[:, 4096:4160], jnp.zeros((w.shape[0], NP - DT0 - 64), w.dtype)], axis=1)


def _unpermute_w_in(wp):
    return jnp.concatenate([wp[:, PX0:PX0 + 2048], wp[:, PBC0:PBC0 + 2048], wp[:, DT0:DT0 + 64], wp[:, Z0:Z0 + 2048],
                            wp[:, GV0:GV0 + 3072], wp[:, G10:G10 + 2048]], axis=1)


def _cols_gathered(g):
    return jnp.transpose(g, (1, 0, 2)).reshape(g.shape[1], N_DEV * g.shape[2])


def _cols_to_blocks(a):
    r, c8 = a.shape
    return jnp.transpose(a.reshape(r, N_DEV, c8 // N_DEV), (1, 0, 2))


def kernel(x, c, ctx, c_ctx, w_mod, b_mod, norm_w, w_in, ssm_conv_w, ssm_conv_b, dt_bias, a_log, d_skip, ssm_norm_w, w_out_ssm, conf_conv_w, conf_conv_b, conf_ln_w, conf_ln_b, w_out_conf, w_out, final_norm_w, loss_target, m_c_ctx, m_w_mod, m_b_mod, m_norm_w, m_w_in, m_ssm_conv_w, m_ssm_conv_b, m_dt_bias, m_a_log, m_d_skip, m_ssm_norm_w, m_w_out_ssm, m_conf_conv_w, m_conf_conv_b, m_conf_ln_w, m_conf_ln_b, m_w_out_conf, m_w_out, m_final_norm_w, v_c_ctx, v_w_mod, v_b_mod, v_norm_w, v_w_in, v_ssm_conv_w, v_ssm_conv_b, v_dt_bias, v_a_log, v_d_skip, v_ssm_norm_w, v_w_out_ssm, v_conf_conv_w, v_conf_conv_b, v_conf_ln_w, v_conf_ln_b, v_w_out_conf, v_w_out, v_final_norm_w):
    L = x.shape[1]
    Lc = ctx.shape[1]
    T = L + Lc
    nlx = L // RT
    nxc = L // Q
    x2 = x.reshape(L, D)
    ctx2 = ctx.reshape(Lc, D)
    tgt = loss_target.reshape(L, D)

    gathered = _gather2([w_in[0].astype(bf16), w_mod[0].astype(bf16), ssm_conv_w[0], conf_conv_w[0]], name="gather_weights")
    wp = _permute_w_in(_cols_gathered(gathered[0]))
    wmod_bf = _cols_gathered(gathered[1])
    scw8 = jnp.concatenate([_cols_gathered(gathered[2]), jnp.zeros((8 - SK, 4096), f32)], axis=0)
    ccw32 = jnp.concatenate([_cols_gathered(gathered[3]), jnp.zeros((32 - CK, D), f32)], axis=0)

    norm_w1 = norm_w.reshape(1, D)
    scb = ssm_conv_b.reshape(1, 4096)
    bias_row = jnp.concatenate([dt_bias.reshape(1, 2 * NH), jnp.zeros((1, 128 - 2 * NH), f32)], axis=1)
    alog_row = jnp.concatenate([a_log.reshape(1, 2 * NH), jnp.zeros((1, 128 - 2 * NH), f32)], axis=1)
    dskip_full = jnp.repeat(d_skip.reshape(NH), HD).reshape(1, DI)
    snw = ssm_norm_w.reshape(1, DI)
    ccb = conf_conv_b.reshape(1, D)
    lnw = conf_ln_w.reshape(1, D)
    lnb = conf_ln_b.reshape(1, D)
    fw = final_norm_w.reshape(1, D)

    cc8 = jnp.concatenate([c.reshape(1, D), c_ctx.reshape(1, D), jnp.zeros((6, D), f32)], axis=0)
    mod, silu_rows = _mod_fwd(cc8, wmod_bf, b_mod.reshape(1, 3 * D))
    h, h_t = _prenorm(x2, ctx2, norm_w1, mod)
    proj, wos_g, woc_g, wo_g = _matmul(
        h, wp, f32, "proj_gather", tn=NP // 5,
        comm=([w_out_ssm[0].astype(bf16), w_out_conf[0].astype(bf16), w_out[0].astype(bf16)], (False,) * 3))
    wos_bf = wos_g.reshape(DI, D)
    woc_bf = woc_g.reshape(D, D)
    wo_bf = wo_g.reshape(D, D)
    xs = _ssm_conv_fwd(proj, scw8, scb, nlx, 0, f32, "ssm_conv_fwd_x")
    bcm = _ssm_conv_fwd(proj, scw8, scb, nlx, 1, bf16, "ssm_conv_fwd_bc")
    dt, la = _dt_prep(proj, bias_row, alog_row)
    consts_f, consts_b = _scan_consts(False), _scan_consts(True)
    yf, hp_f = _ssd_fwd(xs, bcm, dt, la, consts_f, False, nxc, "ssd_fwd_f")
    ysum, hp_b = _ssd_fwd(xs, bcm, dt, la, consts_b, True, nxc, "ssd_fwd_b", y_acc=yf)
    y, yn, yn_t = _ynorm_fwd(ysum, xs, proj, dskip_full, snw, L)
    bs = _matmul(yn, wos_bf, f32, "branch_ssm", tm=1024, tk=2048)
    u1, u3, u3_t = _conf_fwd(proj, ccw32, ccb, lnw, lnb, L)
    bc = _matmul(u3, woc_bf, f32, "branch_conf", tm=2048)
    merged, merged_t = _merge_fwd(bs, bc, proj)
    out = _matmul(merged, wo_bf, f32, "out_proj", tm=2048)
    dx1, dout, loss_acc, dfw, dgate = _final(x2, out, tgt, mod, fw)

    dmerged = _matmul(dout, wo_bf, f32, "d_merged", tb=True, tm=2048)
    g_wo = _matmul(merged_t, dout, bf16, "g_w_out", tm=1024, tk=2048)
    dproj, dbs, dbc = _merge_bwd(dmerged, bs, bc, proj)
    dyn = _matmul(dbs, wos_bf, f32, "d_yn", tb=True, tm=1024, tn=2048)
    g_wos = _matmul(yn_t, dbs, bf16, "g_w_out_ssm", tm=1024, tk=2048)
    du3 = _matmul(dbc, woc_bf, f32, "d_u3", tb=True, tm=2048)
    g_woc = _matmul(u3_t, dbc, bf16, "g_w_out_conf", tm=1024, tk=2048)
    dproj, g_ccw, g_ccb, g_lnw, g_lnb = _conf_bwd(du3, u1, proj, ccw32, lnw, lnb, dproj)
    dproj, dy, g_snw, dsk_cols = _ynorm_bwd(dyn, y, xs, proj, dskip_full, snw, dproj)
    acc_f = _ssd_bwd(xs, bcm, dy, dt, la, hp_f, dskip_full, consts_f, False, nxc, "ssd_bwd_f")
    dxbc, a1, a2, r2, sv = _ssd_bwd(xs, bcm, dy, dt, la, hp_b, dskip_full, consts_b, True, nxc, "ssd_bwd_b", acc=acc_f)
    dproj, g_dtb, g_alog = _dt_bwd(a1, a2, r2, sv, dt, la, proj, bias_row, alog_row, dproj)
    dpre, g_scw, g_scb = _ssm_conv_dpre(dxbc, proj, scw8, scb, nlx)
    dproj = _ssm_conv_t(dpre, scw8, dproj, nlx)
    g_wp, *parts_b = _matmul(
        h_t, dproj, bf16, "g_w_in_scatter", tm=1024, tn=NP // 5,
        comm=([g_wos.reshape(N_DEV, DI // N_DEV, D), g_woc.reshape(N_DEV, D // N_DEV, D), g_wo.reshape(N_DEV, D // N_DEV, D),
               _cols_to_blocks(g_scw[:SK]), _cols_to_blocks(g_ccw[:CK])], (True,) * 5))
    dh, parts_a = _matmul(dproj, wp, f32, "d_h_scatter", tb=True, tk=NP // 5,
                          comm=([_cols_to_blocks(_unpermute_w_in(g_wp))], (True,)))
    parts = [parts_a] + parts_b
    gx, g_nw, macc = _prenorm_bwd(x2, ctx2, dh, dx1, norm_w1, mod)
    dmod_x = jnp.concatenate([macc[0:1], macc[1:2], dgate], axis=1)
    dmod_c = jnp.concatenate([macc[2:3], macc[3:4], jnp.zeros((1, D), f32)], axis=1)
    dmod8 = jnp.concatenate([dmod_x, dmod_c, jnp.zeros((6, 3 * D), f32)], axis=0)
    ct = jnp.concatenate([c.reshape(D, 1), c_ctx.reshape(D, 1), jnp.zeros((D, 126), f32)], axis=1)
    g_bmod, g_cctx = _mod_bwd(ct, dmod8, wmod_bf)
    g_dskip = _head_sums(dsk_cols.reshape(NH, HD))[:, 0]

    small_g = _pack_small({
        "c_ctx": g_cctx[:, 0], "b_mod": g_bmod, "norm_w": g_nw, "ssm_conv_b": g_scb, "dt_bias": g_dtb[0, :2 * NH],
        "a_log": g_alog[0, :2 * NH], "d_skip": g_dskip, "ssm_norm_w": g_snw, "conf_conv_b": g_ccb, "conf_ln_w": g_lnw,
        "conf_ln_b": g_lnb, "final_norm_w": dfw})
    fac = jnp.concatenate([silu_rows[0:1].reshape(D // 128, 128), dmod_x.reshape(3 * D // 128, 128),
                           dmod_c.reshape(3 * D // 128, 128)], axis=0)
    small_parts, fac_all = _exchange([small_g, fac], (False, False), name="exchange_tail")
    nr = D // 128
    sct = jnp.concatenate([fac_all[:, 0:nr].reshape(N_DEV, D).T, silu_rows[1:2].T, jnp.zeros((D, 128 - N_DEV - 1), f32)], axis=1)
    my_cols = (4 * lax.axis_index("x") + 2 * lax.axis_index("y") + lax.axis_index("c")) * (3 * D // N_DEV)
    dmx_all = lax.dynamic_slice(fac_all[:, nr:4 * nr].reshape(N_DEV, 3 * D), (0, my_cols), (N_DEV, 3 * D // N_DEV))
    dmc_all = lax.dynamic_slice(fac_all[:, 4 * nr:7 * nr].reshape(N_DEV, 3 * D), (0, my_cols), (N_DEV, 3 * D // N_DEV))
    g_wmod = _wmod_grad(sct, dmx_all, dmc_all)
    parts = [parts[0], g_wmod[None]] + parts[1:]

    given = dict(c_ctx=c_ctx, w_mod=w_mod, b_mod=b_mod, norm_w=norm_w, w_in=w_in, ssm_conv_w=ssm_conv_w, ssm_conv_b=ssm_conv_b,
                 dt_bias=dt_bias, a_log=a_log, d_skip=d_skip, ssm_norm_w=ssm_norm_w, w_out_ssm=w_out_ssm, conf_conv_w=conf_conv_w,
                 conf_conv_b=conf_conv_b, conf_ln_w=conf_ln_w, conf_ln_b=conf_ln_b, w_out_conf=w_out_conf, w_out=w_out,
                 final_norm_w=final_norm_w)
    ms = dict(c_ctx=m_c_ctx, w_mod=m_w_mod, b_mod=m_b_mod, norm_w=m_norm_w, w_in=m_w_in, ssm_conv_w=m_ssm_conv_w,
              ssm_conv_b=m_ssm_conv_b, dt_bias=m_dt_bias, a_log=m_a_log, d_skip=m_d_skip, ssm_norm_w=m_ssm_norm_w,
              w_out_ssm=m_w_out_ssm, conf_conv_w=m_conf_conv_w, conf_conv_b=m_conf_conv_b, conf_ln_w=m_conf_ln_w,
              conf_ln_b=m_conf_ln_b, w_out_conf=m_w_out_conf, w_out=m_w_out, final_norm_w=m_final_norm_w)
    vs = dict(c_ctx=v_c_ctx, w_mod=v_w_mod, b_mod=v_b_mod, norm_w=v_norm_w, w_in=v_w_in, ssm_conv_w=v_ssm_conv_w,
              ssm_conv_b=v_ssm_conv_b, dt_bias=v_dt_bias, a_log=v_a_log, d_skip=v_d_skip, ssm_norm_w=v_ssm_norm_w,
              w_out_ssm=v_w_out_ssm, conf_conv_w=v_conf_conv_w, conf_conv_b=v_conf_conv_b, conf_ln_w=v_conf_ln_w,
              conf_ln_b=v_conf_ln_b, w_out_conf=v_w_out_conf, w_out=v_w_out, final_norm_w=v_final_norm_w)
    grads, deltas, new_m, new_v = {}, {}, {}, {}
    sharded = ("w_in", "w_mod", "w_out_ssm", "w_out_conf", "w_out", "ssm_conv_w", "conf_conv_w")
    for i, nm in enumerate(sharded):
        shp = given[nm].shape
        w2 = given[nm].reshape(shp[1], shp[2])
        res = _adamw(parts[i], w2, ms[nm].reshape(w2.shape), vs[nm].reshape(w2.shape), "adamw_" + nm)
        grads[nm], deltas[nm], new_m[nm], new_v[nm] = [r.reshape(shp) for r in res]
    shapes = {nm: given[nm].shape for nm, _ in _SMALL}
    res = _adamw(small_parts, _pack_small(given), _pack_small(ms), _pack_small(vs), "adamw_small")
    for dst, packed in zip((grads, deltas, new_m, new_v), res):
        dst.update(_unpack_small(packed, shapes))

    loss = lax.psum(loss_acc[0, 0], ("x", "y", "c"))
    order = ("c_ctx", "w_mod", "b_mod", "norm_w", "w_in", "ssm_conv_w", "ssm_conv_b", "dt_bias", "a_log", "d_skip", "ssm_norm_w",
             "w_out_ssm", "conf_conv_w", "conf_conv_b", "conf_ln_w", "conf_ln_b", "w_out_conf", "w_out", "final_norm_w")
    return (loss, gx.reshape(1, L, D), *[grads[n] for n in order], *[deltas[n] for n in order],
            *[new_m[n] for n in order], *[new_v[n] for n in order])
```

```python
import jax
import jax.numpy as jnp
from jax import lax
from jax.experimental import pallas as pl
from jax.experimental.pallas import tpu as pltpu

f32 = jnp.float32
bf16 = jnp.bfloat16

D = 1024
DI = 2048
NG = 8
HPG = 4
HD = 64
GW = HPG * HD
NS = 128
NH = 32
Q = 128
GRID_W = 64
CK = 31
SK = 4
EPS = 1e-6
RT = 256
N_DEV = 8
IN_COLS = 11328
G10, G20, PBC0, PX0, GV0, GG0, CG0, Z0, DT0, NP = 0, 1024, 2048, 4096, 6144, 7168, 8192, 9216, 11264, 11520
CONV_CT = 2048
B0, C0 = 2048, 3072
VMEM_LIMIT = 50 * 1024 * 1024
NEG = -1e30

ADAM_LR, ADAM_B1, ADAM_B2, ADAM_EPS, ADAM_WD, ADAM_STEP = 0.001, 0.9, 0.999, 1e-08, 0.01, 10

MESH = pl.DeviceIdType.MESH
S = jax.ShapeDtypeStruct


def _params(*sem):
    return pltpu.CompilerParams(dimension_semantics=tuple(sem) if sem else None, vmem_limit_bytes=VMEM_LIMIT)


def _sig(x):
    return 1.0 / (1.0 + jnp.exp(-x))


def _silu(x):
    return x * _sig(x)


def _dsilu(x, s):
    return s * (1.0 + x * (1.0 - s))


def _dot(a, b):
    return jnp.dot(a, b, preferred_element_type=f32)


def _dot_nt(a, b):
    return lax.dot_general(a, b, (((1,), (1,)), ((), ())), preferred_element_type=f32)


def _dot_tn(a, b):
    return lax.dot_general(a, b, (((0,), (0,)), ((), ())), preferred_element_type=f32)


def _dot3(t_bf, v):
    v1 = v.astype(bf16)
    r1 = v - v1.astype(f32)
    v2 = r1.astype(bf16)
    v3 = (r1 - v2.astype(f32)).astype(bf16)
    return _dot(t_bf, v1) + _dot(t_bf, v2) + _dot(t_bf, v3)


def _pick(n, prefs):
    for p in prefs:
        if n % p == 0:
            return p
    return n


def _full(shape):
    nd = len(shape)
    return pl.BlockSpec(shape, lambda *_: (0,) * nd)


def _matmul(a, b, out_dtype, name, tm=None, tn=None, tk=None, tb=False, comm=None):
    m, k = a.shape
    n = b.shape[0] if tb else b.shape[1]
    tm = tm if tm and m % tm == 0 else _pick(m, (768, 512, 256, 128))
    tn = tn if tn and n % tn == 0 else _pick(n, (1024, 512, 256, 128))
    tk = tk if tk and k % tk == 0 else _pick(k, (1024, 768, 512, 256, 128))
    nk = k // tk
    gi, gj = m // tm, n // tn
    carrs, modes = comm if comm else ((), ())
    nc = len(carrs)

    def kern(*refs):
        a_ref, b_ref = refs[:2]
        cins = refs[2:2 + nc]
        o_ref = refs[2 + nc]
        couts = refs[3 + nc:3 + 2 * nc]
        acc_ref = refs[3 + 2 * nc]
        sems = refs[4 + 2 * nc:]
        i, j, kk = pl.program_id(0), pl.program_id(1), pl.program_id(2)
        if nc:
            @pl.when(jnp.logical_and(jnp.logical_and(i == 0, j == 0), kk == 0))
            def _():
                _xchg_start(cins, couts, *sems, modes)

        part = _dot_nt(a_ref[...], b_ref[...]) if tb else _dot(a_ref[...], b_ref[...])
        if nk == 1:
            o_ref[...] = part.astype(o_ref.dtype)
        else:
            @pl.when(kk == 0)
            def _():
                acc_ref[...] = part

            @pl.when(kk > 0)
            def _():
                acc_ref[...] += part

            @pl.when(kk == nk - 1)
            def _():
                o_ref[...] = acc_ref[...].astype(o_ref.dtype)

        if nc:
            @pl.when(jnp.logical_and(jnp.logical_and(i == gi - 1, j == gj - 1), kk == nk - 1))
            def _():
                _xchg_wait(cins, couts, *sems, modes)

    anyspec = pl.BlockSpec(memory_space=pl.ANY)
    bspec = pl.BlockSpec((tn, tk), lambda i, j, kk: (j, kk)) if tb else pl.BlockSpec((tk, tn), lambda i, j, kk: (kk, j))
    out_shape = (S((m, n), out_dtype),) + _xchg_out_shapes(carrs, modes)
    res = pl.pallas_call(
        kern, out_shape=out_shape, grid=(gi, gj, nk),
        in_specs=[pl.BlockSpec((tm, tk), lambda i, j, kk: (i, kk)), bspec] + [anyspec] * nc,
        out_specs=(pl.BlockSpec((tm, tn), lambda i, j, kk: (i, j)),) + (anyspec,) * nc,
        scratch_shapes=[pltpu.VMEM((tm, tn), f32)] + (_xchg_sems(nc) if nc else []),
        compiler_params=_params(*((("arbitrary",) * 3) if nc else ("parallel", "parallel", "arbitrary"))), name=name)(a, b, *carrs)
    return res if nc else res[0]


def _mod_fwd(cc8, w_mod_bf, b_mod):
    def kern(c_ref, w_ref, b_ref, o_ref, s_ref):
        s = _silu(c_ref[...])
        s_ref[...] = s
        o_ref[...] = _dot(s.astype(bf16), w_ref[...]) + b_ref[...]

    return pl.pallas_call(kern, out_shape=(S((8, 3 * D), f32), S((8, D), f32)), compiler_params=_params(),
                          name="mod_fwd")(cc8, w_mod_bf, b_mod)


def _mod_bwd(ct, dmod8, w_mod_bf):
    tc = 512
    nj = 3 * D // tc

    def kern(ct_ref, dm_ref, w_ref, db_ref, dc_ref):
        j = pl.program_id(0)
        cx = ct_ref[:, 1:2]
        sx = _sig(cx)
        dmc = dm_ref[1:2, :]
        db_ref[...] = dm_ref[0:1, :] + dmc
        t = jnp.sum(w_ref[...].astype(f32) * dmc.astype(bf16).astype(f32), axis=1, keepdims=True) * _dsilu(cx, sx)

        @pl.when(j == 0)
        def _():
            dc_ref[...] = jnp.zeros_like(dc_ref)

        dc_ref[...] += jnp.broadcast_to(t, (D, 128))

    return pl.pallas_call(
        kern, out_shape=(S((1, 3 * D), f32), S((D, 128), f32)), grid=(nj,),
        in_specs=[_full((D, 128)), pl.BlockSpec((8, tc), lambda j: (0, j)), pl.BlockSpec((D, tc), lambda j: (0, j))],
        out_specs=(pl.BlockSpec((1, tc), lambda j: (0, j)), _full((D, 128))),
        compiler_params=_params("arbitrary"), name="mod_bwd")(ct, dmod8, w_mod_bf)


def _wmod_grad(sct, dmx, dmc, losses):
    cols = dmx.shape[1]

    def kern(s_ref, dmx_ref, dmc_ref, l_ref, g_ref, lo_ref):
        dmc_sum = dmc_ref[0:1, :]
        lsum = l_ref[0:1, :]
        for d in range(1, N_DEV):
            dmc_sum = dmc_sum + dmc_ref[d:d + 1, :]
            lsum = lsum + l_ref[d:d + 1, :]
        lo_ref[...] = lsum
        g = s_ref[:, N_DEV:N_DEV + 1] * dmc_sum
        for d in range(N_DEV):
            g = g + s_ref[:, d:d + 1] * dmx_ref[d:d + 1, :]
        g_ref[...] = g

    return pl.pallas_call(kern, out_shape=(S((D, cols), f32), S((1, 128), f32)), compiler_params=_params(),
                          name="wmod_grad")(sct, dmx, dmc, losses)


def _prenorm(x, ctx, norm_w, mod):
    L, Lc = x.shape[0], ctx.shape[0]
    nlx, nt = L // RT, (L + Lc) // RT

    def kern(x_ref, c_ref, nw_ref, mod_ref, h_ref, ht_ref):
        i = pl.program_id(0)
        is_c = i >= nlx
        xv = jnp.where(is_c, c_ref[...], x_ref[...])
        shift = jnp.where(is_c, mod_ref[1:2, 0:D], mod_ref[0:1, 0:D])
        scale = jnp.where(is_c, mod_ref[1:2, D:2 * D], mod_ref[0:1, D:2 * D])
        r = lax.rsqrt(jnp.mean(xv * xv, axis=1, keepdims=True) + EPS)
        hv = (xv * r) * nw_ref[...] * (1.0 + scale) + shift
        h_ref[...] = hv.astype(bf16)
        ht_ref[...] = jnp.transpose(hv).astype(bf16)

    return pl.pallas_call(
        kern, out_shape=(S((L + Lc, D), bf16), S((D, L + Lc), bf16)), grid=(nt,),
        in_specs=[pl.BlockSpec((RT, D), lambda i: (jnp.minimum(i, nlx - 1), 0)),
                  pl.BlockSpec((RT, D), lambda i: (jnp.maximum(i - nlx, 0), 0)),
                  _full((1, D)), _full((8, 3 * D))],
        out_specs=(pl.BlockSpec((RT, D), lambda i: (i, 0)), pl.BlockSpec((D, RT), lambda i: (0, i))),
        compiler_params=_params("parallel"), name="prenorm")(x, ctx, norm_w, mod)


def _prenorm_bwd(x, ctx, dh, dx1, norm_w, mod):
    L, Lc = x.shape[0], ctx.shape[0]
    nlx, nt = L // RT, (L + Lc) // RT

    def kern(x_ref, c_ref, dh_ref, dx1_ref, nw_ref, mod_ref, gx_ref, dnw_ref, acc_ref):
        i = pl.program_id(0)
        is_c = i >= nlx

        @pl.when(i == 0)
        def _():
            dnw_ref[...] = jnp.zeros_like(dnw_ref)
            acc_ref[...] = jnp.zeros_like(acc_ref)

        xv = jnp.where(is_c, c_ref[...], x_ref[...])
        scale = jnp.where(is_c, mod_ref[1:2, D:2 * D], mod_ref[0:1, D:2 * D])
        nw = nw_ref[...]
        r = lax.rsqrt(jnp.mean(xv * xv, axis=1, keepdims=True) + EPS)
        xn = xv * r
        dh = dh_ref[...]
        dsh = jnp.sum(dh, axis=0, keepdims=True)
        dsc = jnp.sum(dh * (xn * nw), axis=0, keepdims=True)
        dxnw = dh * (1.0 + scale)
        dnw_ref[...] += jnp.sum(dxnw * xn, axis=0, keepdims=True)
        dxn = dxnw * nw
        dx = r * (dxn - xn * jnp.mean(dxn * xn, axis=1, keepdims=True))

        @pl.when(jnp.logical_not(is_c))
        def _():
            gx_ref[...] = dx1_ref[...] + dx
            acc_ref[0:1, :] += dsh
            acc_ref[1:2, :] += dsc

        @pl.when(is_c)
        def _():
            acc_ref[2:3, :] += dsh
            acc_ref[3:4, :] += dsc

    xmap = lambda i: (jnp.minimum(i, nlx - 1), 0)
    return pl.pallas_call(
        kern, out_shape=(S((L, D), f32), S((1, D), f32), S((8, D), f32)), grid=(nt,),
        in_specs=[pl.BlockSpec((RT, D), xmap), pl.BlockSpec((RT, D), lambda i: (jnp.maximum(i - nlx, 0), 0)),
                  pl.BlockSpec((RT, D), lambda i: (i, 0)), pl.BlockSpec((RT, D), xmap), _full((1, D)), _full((8, 3 * D))],
        out_specs=(pl.BlockSpec((RT, D), xmap), _full((1, D)), _full((8, D))),
        compiler_params=_params("arbitrary"), name="prenorm_bwd")(x, ctx, dh, dx1, norm_w, mod)


def _xbc_col(j):
    return jnp.where(j == 0, PX0 // CONV_CT, PBC0 // CONV_CT)


def _halo_specs(nt_rows, ct, col=lambda j: j):
    cur = pl.BlockSpec((RT, ct), lambda i, j: (i, col(j)))
    prev = pl.BlockSpec((8, ct), lambda i, j: (jnp.maximum(i * (RT // 8) - 1, 0), col(j)))
    nxt = pl.BlockSpec((8, ct), lambda i, j: (jnp.minimum((i + 1) * (RT // 8), nt_rows // 8 - 1), col(j)))
    return cur, prev, nxt


def _fill_halo(scr, cur_ref, prev_ref, next_ref, i, nlx, nt):
    prev_ok = jnp.logical_and(i != 0, i != nlx)
    next_ok = jnp.logical_and(i != nlx - 1, i != nt - 1)
    scr[0:8, :] = jnp.where(prev_ok, prev_ref[...], 0.0)
    scr[8:8 + RT, :] = cur_ref[...]
    scr[8 + RT:16 + RT, :] = jnp.where(next_ok, next_ref[...], 0.0)


CONV_RB = 32


def _conv_blocks(ct):
    return [(slice(cb * 128, (cb + 1) * 128), r0) for cb in range(ct // 128) for r0 in range(0, RT, CONV_RB)]


def _taps(scr, cs, r0, shifts):
    blk = scr[r0:r0 + CONV_RB + 16, cs]
    n = CONV_RB + 16
    return [(blk if d == 0 else pltpu.roll(blk, (-d) % n, 0))[8:8 + CONV_RB, :] for d in shifts]


def _ssm_conv_fwd(proj, w8, b, nlx, half, out_dtype, name):
    T = proj.shape[0]
    nt = T // RT
    ct = CONV_CT
    cur, prev, nxt = _halo_specs(T, ct, lambda j: _xbc_col(j + half))

    def kern(cur_ref, prev_ref, next_ref, w_ref, b_ref, o_ref, scr):
        i = pl.program_id(0)
        _fill_halo(scr, cur_ref, prev_ref, next_ref, i, nlx, nt)
        for cs, r0 in _conv_blocks(ct):
            taps = _taps(scr, cs, r0, [k - 2 for k in range(SK)])
            acc = jnp.broadcast_to(b_ref[:, cs], (CONV_RB, 128))
            for k in range(SK):
                acc = acc + w_ref[k:k + 1, cs] * taps[k]
            o_ref[r0:r0 + CONV_RB, cs] = _silu(acc).astype(out_dtype)

    return pl.pallas_call(
        kern, out_shape=S((T, ct), out_dtype), grid=(nt, 1),
        in_specs=[cur, prev, nxt, pl.BlockSpec((8, ct), lambda i, j: (0, j + half)),
                  pl.BlockSpec((1, ct), lambda i, j: (0, j + half))],
        out_specs=pl.BlockSpec((RT, ct), lambda i, j: (i, j)),
        scratch_shapes=[pltpu.VMEM((RT + 16, ct), f32)],
        compiler_params=_params("parallel", "parallel"), name=name)(proj, proj, proj, w8, b)


def _ssm_conv_dpre(dxbc, proj, w8, b, nlx):
    T = proj.shape[0]
    nt = T // RT
    ct = CONV_CT
    cur = pl.BlockSpec((RT, ct), lambda j, i: (i, j))
    pcur = pl.BlockSpec((RT, ct), lambda j, i: (i, _xbc_col(j)))
    prev = pl.BlockSpec((8, ct), lambda j, i: (jnp.maximum(i * (RT // 8) - 1, 0), _xbc_col(j)))
    nxt = pl.BlockSpec((8, ct), lambda j, i: (jnp.minimum((i + 1) * (RT // 8), T // 8 - 1), _xbc_col(j)))

    def kern(d_ref, cur_ref, prev_ref, next_ref, w_ref, b_ref, dpre_ref, dw_ref, db_ref, scr):
        i = pl.program_id(1)
        _fill_halo(scr, cur_ref, prev_ref, next_ref, i, nlx, nt)

        @pl.when(i == 0)
        def _():
            dw_ref[...] = jnp.zeros_like(dw_ref)
            db_ref[...] = jnp.zeros_like(db_ref)

        for cb in range(ct // 128):
            cs = slice(cb * 128, (cb + 1) * 128)
            db_acc = jnp.zeros((CONV_RB, 128), f32)
            dw_acc = [jnp.zeros((CONV_RB, 128), f32) for _ in range(SK)]
            for r0 in range(0, RT, CONV_RB):
                taps = _taps(scr, cs, r0, [k - 2 for k in range(SK)])
                pre = jnp.broadcast_to(b_ref[:, cs], (CONV_RB, 128))
                for k in range(SK):
                    pre = pre + w_ref[k:k + 1, cs] * taps[k]
                dpre = d_ref[r0:r0 + CONV_RB, cs] * _dsilu(pre, _sig(pre))
                dpre_ref[r0:r0 + CONV_RB, cs] = dpre
                db_acc = db_acc + dpre
                dw_acc = [dw_acc[k] + dpre * taps[k] for k in range(SK)]
            db_ref[:, cs] += jnp.sum(db_acc, axis=0, keepdims=True)
            for k in range(SK):
                dw_ref[k:k + 1, cs] += jnp.sum(dw_acc[k], axis=0, keepdims=True)

    return pl.pallas_call(
        kern, out_shape=(S((T, 4096), f32), S((8, 4096), f32), S((1, 4096), f32)), grid=(4096 // ct, nt),
        in_specs=[cur, pcur, prev, nxt, pl.BlockSpec((8, ct), lambda j, i: (0, j)), pl.BlockSpec((1, ct), lambda j, i: (0, j))],
        out_specs=(cur, pl.BlockSpec((8, ct), lambda j, i: (0, j)), pl.BlockSpec((1, ct), lambda j, i: (0, j))),
        scratch_shapes=[pltpu.VMEM((RT + 16, ct), f32)],
        compiler_params=_params("parallel", "arbitrary"), name="ssm_conv_dpre")(dxbc, proj, proj, proj, w8, b)


def _ssm_conv_t(dpre, w8, dproj, nlx):
    T = dpre.shape[0]
    nt = T // RT
    ct = CONV_CT
    cur, prev, nxt = _halo_specs(T, ct)

    def kern(cur_ref, prev_ref, next_ref, w_ref, _alias, o_ref, scr):
        i = pl.program_id(0)
        _fill_halo(scr, cur_ref, prev_ref, next_ref, i, nlx, nt)
        for cs, r0 in _conv_blocks(ct):
            taps = _taps(scr, cs, r0, [2 - k for k in range(SK)])
            acc = jnp.zeros((CONV_RB, 128), f32)
            for k in range(SK):
                acc = acc + w_ref[k:k + 1, cs] * taps[k]
            o_ref[r0:r0 + CONV_RB, cs] = acc.astype(bf16)

    return pl.pallas_call(
        kern, out_shape=S(dproj.shape, bf16), grid=(nt, 4096 // ct),
        in_specs=[cur, prev, nxt, pl.BlockSpec((8, ct), lambda i, j: (0, j)), pl.BlockSpec(memory_space=pl.ANY)],
        out_specs=pl.BlockSpec((RT, ct), lambda i, j: (i, _xbc_col(j))),
        scratch_shapes=[pltpu.VMEM((RT + 16, ct), f32)], input_output_aliases={4: 0},
        compiler_params=_params("parallel", "parallel"), name="ssm_conv_t")(dpre, dpre, dpre, w8, dproj)


DT_CH = 6


def _tri():
    li = lax.broadcasted_iota(jnp.int32, (Q, Q), 0)
    si = lax.broadcasted_iota(jnp.int32, (Q, Q), 1)
    return (si <= li).astype(bf16), (si >= li).astype(bf16)


def _dt_prep(proj, bias_row, alog_row):
    T = proj.shape[0]
    nch = T // Q
    assert nch % DT_CH == 0

    def kern(raw_ref, b_ref, al_ref, dt_ref, la_ref):
        lane = lax.broadcasted_iota(jnp.int32, (Q, 128), 1)
        a = jnp.where(lane[0:1, :] < 2 * NH, -jnp.exp(al_ref[...]), 0.0)
        tri, trit = _tri()
        for h in range(DT_CH):
            rows = slice(h * Q, (h + 1) * Q)
            v = raw_ref[rows, :] + b_ref[...]
            dt = jnp.maximum(v, 0.0) + jnp.log1p(jnp.exp(-jnp.abs(v)))
            da = dt * a
            dt_ref[rows, :] = dt
            la_ref[rows, :] = jnp.where(lane < NH, _dot3(tri, da), _dot3(trit, da))

    rq = DT_CH * Q
    return pl.pallas_call(
        kern, out_shape=(S((T, 128), f32), S((T, 128), f32)), grid=(nch // DT_CH,),
        in_specs=[pl.BlockSpec((rq, 128), lambda c: (c, DT0 // 128)), _full((1, 128)), _full((1, 128))],
        out_specs=(pl.BlockSpec((rq, 128), lambda c: (c, 0)), pl.BlockSpec((rq, 128), lambda c: (c, 0))),
        compiler_params=_params("parallel"), name="dt_prep")(proj, bias_row, alog_row)


def _dt_bwd(a1, a2, r2, sv, dt, la, proj, bias_row, alog_row, dproj):
    T = proj.shape[0]
    nch = T // Q
    assert nch % DT_CH == 0
    rq = DT_CH * Q
    blk = pl.BlockSpec((rq, 128), lambda c: (c, 0))

    def kern(a1_ref, a2_ref, r2_ref, s_ref, dt_ref, la_ref, raw_ref, b_ref, al_ref, _alias, o_ref, db_ref, dal_ref):
        c = pl.program_id(0)

        @pl.when(c == 0)
        def _():
            db_ref[...] = jnp.zeros_like(db_ref)
            dal_ref[...] = jnp.zeros_like(dal_ref)

        lane = lax.broadcasted_iota(jnp.int32, (Q, 128), 1)
        row = lax.broadcasted_iota(jnp.int32, (Q, 128), 0)
        fwd = lane < NH
        a = jnp.where(lane[0:1, :] < 2 * NH, -jnp.exp(al_ref[...]), 0.0)
        is_end = row == jnp.where(fwd, Q - 1, 0)
        tri, trit = _tri()
        o_ref[...] = jnp.zeros_like(o_ref)
        for h in range(DT_CH):
            rows = slice(h * Q, (h + 1) * Q)
            dt = dt_ref[rows, :]
            la = la_ref[rows, :]
            a2v = a2_ref[rows, :]
            r2v = r2_ref[rows, :]
            la_e = jnp.where(fwd[0:1, :], la[Q - 1:Q, :], la[0:1, :])
            e_end = jnp.exp(la_e - la)
            wend = e_end * dt
            extra = s_ref[h * Q:h * Q + 1, :] * jnp.exp(la_e) + jnp.sum(wend * a2v, axis=0, keepdims=True)
            dla = a1_ref[rows, :] - dt * r2v - wend * a2v + jnp.where(is_end, extra, 0.0)
            rcs = jnp.where(fwd, _dot3(trit, dla), _dot3(tri, dla))
            ddt = r2v + e_end * a2v + a * rcs
            dal_ref[...] += a * jnp.sum(dt * rcs, axis=0, keepdims=True)
            draw = jnp.where(lane < 2 * NH, ddt * _sig(raw_ref[rows, :] + b_ref[...]), 0.0)
            db_ref[...] += jnp.sum(draw, axis=0, keepdims=True)
            o_ref[rows, 0:128] = draw.astype(bf16)

    return pl.pallas_call(
        kern, out_shape=(S(dproj.shape, bf16), S((1, 128), f32), S((1, 128), f32)), grid=(nch // DT_CH,),
        in_specs=[blk, blk, blk, blk, blk, blk, pl.BlockSpec((rq, 128), lambda c: (c, DT0 // 128)),
                  _full((1, 128)), _full((1, 128)), pl.BlockSpec(memory_space=pl.ANY)],
        out_specs=(pl.BlockSpec((rq, NP - DT0), lambda c: (c, DT0 // (NP - DT0))), _full((1, 128)), _full((1, 128))),
        input_output_aliases={9: 0},
        compiler_params=_params("arbitrary"), name="dt_bwd")(a1, a2, r2, sv, dt, la, proj, bias_row, alog_row, dproj)


def _split2(v):
    hi = v.astype(bf16)
    lo = (v - hi.astype(f32)).astype(bf16)
    return jnp.concatenate([hi, lo], axis=1)


def _scan_consts(rev):
    hoff = NH if rev else 0
    g = jnp.arange(NG, dtype=jnp.int32)[:, None, None]

    def rc(nr, ncol):
        return jnp.arange(nr, dtype=jnp.int32)[None, :, None], jnp.arange(ncol, dtype=jnp.int32)[None, None, :]

    r, c = rc(2 * 128, HPG * HD)
    sel_w = (lax.rem(r, 128) == hoff + HPG * g + c // HD).astype(bf16)
    r, c = rc(HPG * HD, 128)
    ind_h = (c == hoff + HPG * g + r // HD).astype(bf16)
    r, c = rc(2 * HPG * Q, 128)
    ind_e = (c == hoff + HPG * g + lax.rem(r, HPG * Q) // Q).astype(bf16)
    return sel_w, ind_h, ind_e


def _masks(rev):
    li = lax.broadcasted_iota(jnp.int32, (Q, Q), 0)
    si = lax.broadcasted_iota(jnp.int32, (Q, Q), 1)
    mask = (li <= si) if rev else (li >= si)
    mask_t = (li >= si) if rev else (li <= si)
    lane = lax.broadcasted_iota(jnp.int32, (Q, HPG * HD), 1)
    hms = [jnp.logical_and(lane >= r * HD, lane < (r + 1) * HD) for r in range(HPG)]
    return mask, mask_t, hms


def _mine(hoff):
    lane = lax.broadcasted_iota(jnp.int32, (Q, 128), 1)
    return jnp.logical_and(lane >= hoff, lane < hoff + NH)


def _head_row(vals, hc0):
    lane = lax.broadcasted_iota(jnp.int32, (1, HPG * HD), 1)
    out = jnp.zeros((1, HPG * HD), f32)
    for r in range(HPG):
        out = jnp.where(jnp.logical_and(lane >= r * HD, lane < (r + 1) * HD), vals[:, hc0 + r:hc0 + r + 1], out)
    return out


SCAN_CH = 2


def _chunk_of(j, rev, nxc, nch):
    return (nch - 1 - j) if rev else lax.rem(j + nxc, nch)


def _ssd_fwd(xs, bc, dt, la, consts, rev, nxc, name, y_acc=None):
    T = xs.shape[0]
    nch = T // Q
    hoff = NH if rev else 0
    e = 0 if rev else Q - 1
    cm = lambda j: _chunk_of(j, rev, nxc // SCAN_CH, nch // SCAN_CH)
    sel_w = consts[0]
    has_acc = y_acc is not None

    def kern(*refs):
        xs_ref, bc_ref, dt_ref, la_ref, sw_ref = refs[:5]
        yacc_ref = refs[5] if has_acc else None
        y_ref, hp_ref, h_ref = refs[5 + has_acc:]
        j = pl.program_id(0)

        @pl.when(j == 0)
        def _():
            h_ref[...] = jnp.zeros_like(h_ref)

        mask, _, hms = _masks(rev)
        for hh in range(SCAN_CH):
            h = SCAN_CH - 1 - hh if rev else hh
            chunk(refs, mask, hms, h, slice(h * Q, (h + 1) * Q))

    def chunk(refs, mask, hms, h, rows):
        xs_ref, bc_ref, dt_ref, la_ref, sw_ref = refs[:5]
        yacc_ref = refs[5] if has_acc else None
        y_ref, hp_ref, h_ref = refs[5 + has_acc:]
        hp_ref[h] = h_ref[...]
        la_all = la_ref[rows, :]
        dt_all = dt_ref[rows, :]
        la_t = jnp.transpose(la_all)
        dt_t = jnp.transpose(dt_all)
        la_e = la_all[e:e + 1, :]
        w2 = _split2(jnp.exp(jnp.where(_mine(hoff), la_e - la_all, 0.0)) * dt_all)
        e2 = _split2(jnp.exp(la_all))
        ela_e = jnp.exp(la_e)
        for g in range(NG):
            hc0 = hoff + g * HPG
            x = xs_ref[rows, g * GW:(g + 1) * GW]
            bb = bc_ref[rows, g * NS:(g + 1) * NS]
            cb = bc_ref[rows, NG * NS + g * NS:NG * NS + (g + 1) * NS]
            ht = h_ref[g * NS:(g + 1) * NS, :]
            scores = _dot_nt(cb, bb)
            yoff = _dot(cb, ht.astype(bf16))
            wend = _dot(w2, sw_ref[g])
            expla = _dot(e2, sw_ref[g])
            mixes, xstack = [], []
            for r in range(HPG):
                hc = hc0 + r
                la_rep = jnp.broadcast_to(la_all[:, hc:hc + 1], (Q, 128))
                decay = jnp.exp(jnp.where(mask, la_rep - la_t[hc:hc + 1, :], NEG))
                mixes.append((scores * decay * dt_t[hc:hc + 1, :]).astype(bf16))
                xstack.append(jnp.where(hms[r], x, 0.0).astype(bf16))
            y = _dot(jnp.concatenate(mixes, axis=1), jnp.concatenate(xstack, axis=0)) + yoff * expla
            if has_acc:
                y = y + yacc_ref[rows, g * GW:(g + 1) * GW]
            y_ref[rows, g * GW:(g + 1) * GW] = y
            h_ref[g * NS:(g + 1) * NS, :] = ht * _head_row(ela_e, hc0) + _dot_tn(bb, (x * wend).astype(bf16))

    row = lambda j: (cm(j), 0)
    rq = SCAN_CH * Q
    yblk = pl.BlockSpec((rq, DI), row)
    return pl.pallas_call(
        kern, out_shape=(S((T, DI), f32), S((nch, NG * NS, HPG * HD), f32)), grid=(nch // SCAN_CH,),
        in_specs=[yblk, pl.BlockSpec((rq, 2 * NG * NS), row), pl.BlockSpec((rq, 128), row), pl.BlockSpec((rq, 128), row),
                  _full(sel_w.shape)] + ([yblk] if has_acc else []),
        out_specs=(yblk, pl.BlockSpec((SCAN_CH, NG * NS, HPG * HD), lambda j: (cm(j), 0, 0))),
        scratch_shapes=[pltpu.VMEM((NG * NS, HPG * HD), f32)],
        input_output_aliases={5: 0} if has_acc else {},
        compiler_params=_params("arbitrary"), name=name)(xs, bc, dt, la, sel_w, *([y_acc] if has_acc else []))


def _ssd_bwd(xs, bc, dy, dt, la, hprev, dskip_full, consts, rev, nxc, name, acc=None):
    T = xs.shape[0]
    nch = T // Q
    hoff = NH if rev else 0
    e = 0 if rev else Q - 1
    npair = nch // SCAN_CH
    cm = lambda j: _chunk_of(npair - 1 - j, rev, nxc // SCAN_CH, npair)
    has_acc = acc is not None
    sel_w, ind_h, ind_e = consts

    def kern(*refs):
        g_ref = refs[-2]
        j = pl.program_id(0)

        @pl.when(j == 0)
        def _():
            g_ref[...] = jnp.zeros_like(g_ref)

        masks = _masks(rev)
        for hh in range(SCAN_CH):
            h = hh if rev else SCAN_CH - 1 - hh
            chunk(refs, masks, h, slice(h * Q, (h + 1) * Q))

    def chunk(refs, masks, h, rows):
        xs_ref, bc_ref, dy_ref, dt_ref, la_ref, hp_ref, dsk_ref, sw_ref, ih_ref, ie_ref = refs[:10]
        k = 10
        if has_acc:
            dxbc_in, a1_in, a2_in, r2_in, s_in = refs[k:k + 5]
            k += 5
        dxbc_ref, a1_ref, a2_ref, r2_ref, s_ref, g_ref, r2scr = refs[k:k + 7]
        mask, mask_t, hms = masks
        la_all = la_ref[rows, :]
        dt_all = dt_ref[rows, :]
        la_t = jnp.transpose(la_all)
        dt_t = jnp.transpose(dt_all)
        la_e = la_all[e:e + 1, :]
        w2 = _split2(jnp.exp(jnp.where(_mine(hoff), la_e - la_all, 0.0)) * dt_all)
        e2 = _split2(jnp.exp(la_all))
        wed2 = jnp.concatenate([w2, e2, _split2(dt_all)], axis=0)
        ela_e = jnp.exp(la_e)
        r2scr[...] = jnp.zeros_like(r2scr)
        a1acc = jnp.zeros((Q, 128), f32)
        a2acc = jnp.zeros((Q, 128), f32)
        sacc = jnp.zeros((1, 128), f32)
        for g in range(NG):
            hc0 = hoff + g * HPG
            x = xs_ref[rows, g * GW:(g + 1) * GW]
            bb = bc_ref[rows, g * NS:(g + 1) * NS]
            cb = bc_ref[rows, NG * NS + g * NS:NG * NS + (g + 1) * NS]
            dyv = dy_ref[rows, g * GW:(g + 1) * GW]
            gt = g_ref[g * NS:(g + 1) * NS, :]
            ht = hp_ref[h, g * NS:(g + 1) * NS, :]
            gtb = gt.astype(bf16)
            htb = ht.astype(bf16)
            xb = x.astype(bf16)
            scores = _dot_nt(cb, bb)
            scores_t = _dot_nt(bb, cb)
            bg = _dot(bb, gtb)
            yoff = _dot(cb, htb)
            sel3 = _dot(wed2, sw_ref[g])
            wend, expla, dtf = sel3[0:Q], sel3[Q:2 * Q], sel3[2 * Q:3 * Q]
            dym = jnp.concatenate([jnp.where(hms[r], dyv, 0.0).astype(bf16) for r in range(HPG)], axis=0)
            dyx_all = _dot_nt(dym, xb)
            sdts, ems = [], []
            wsum = jnp.zeros((Q, Q), f32)
            for r in range(HPG):
                hc = hc0 + r
                la_rep = jnp.broadcast_to(la_all[:, hc:hc + 1], (Q, 128))
                la_r = la_t[hc:hc + 1, :]
                dt_r = dt_t[hc:hc + 1, :]
                decay = jnp.exp(jnp.where(mask, la_rep - la_r, NEG))
                decay_t = jnp.exp(jnp.where(mask_t, la_r - la_rep, NEG))
                dyx = dyx_all[r * Q:(r + 1) * Q, :]
                fm = dyx * (scores * decay)
                r2scr[hc:hc + 1, :] = jnp.sum(fm, axis=0, keepdims=True)
                ems.append(fm * dt_r)
                wsum = wsum + dyx * decay * dt_r
                sdts.append((scores_t * decay_t).astype(bf16))
            dx = dtf * _dot(jnp.concatenate(sdts, axis=1), dym) + wend * bg
            if not has_acc:
                dx = dx + dsk_ref[:, g * GW:(g + 1) * GW] * dyv
            red3 = _dot(jnp.concatenate([(dyv * yoff * expla).astype(bf16), (x * bg).astype(bf16), (gt * ht).astype(bf16)],
                                        axis=0), ih_ref[g])
            a1acc = a1acc + _dot(_split2(jnp.concatenate(ems, axis=1)), ie_ref[g]) + red3[0:Q]
            a2acc = a2acc + red3[Q:2 * Q]
            sacc = sacc + jnp.sum(red3[2 * Q:3 * Q], axis=0, keepdims=True)
            wb = wsum.astype(bf16)
            dysb = (dyv * expla).astype(bf16)
            dc = _dot(wb, bb) + _dot_nt(dysb, htb)
            db = _dot_tn(wb, cb) + _dot_nt((x * wend).astype(bf16), gtb)
            g_ref[g * NS:(g + 1) * NS, :] = gt * _head_row(ela_e, hc0) + _dot_tn(cb, dysb)
            if has_acc:
                dx = dx + dxbc_in[rows, g * GW:(g + 1) * GW]
                db = db + dxbc_in[rows, B0 + g * NS:B0 + (g + 1) * NS]
                dc = dc + dxbc_in[rows, C0 + g * NS:C0 + (g + 1) * NS]
            dxbc_ref[rows, g * GW:(g + 1) * GW] = dx
            dxbc_ref[rows, B0 + g * NS:B0 + (g + 1) * NS] = db
            dxbc_ref[rows, C0 + g * NS:C0 + (g + 1) * NS] = dc
        r2c = jnp.transpose(r2scr[...])
        sc = jnp.broadcast_to(sacc, (Q, 128))
        if has_acc:
            a1acc = a1acc + a1_in[rows, :]
            a2acc = a2acc + a2_in[rows, :]
            r2c = r2c + r2_in[rows, :]
            sc = sc + s_in[rows, :]
        a1_ref[rows, :] = a1acc
        a2_ref[rows, :] = a2acc
        r2_ref[rows, :] = r2c
        s_ref[rows, :] = sc

    rq = SCAN_CH * Q
    blk = pl.BlockSpec((rq, 128), lambda j: (cm(j), 0))
    big = pl.BlockSpec((rq, 4096), lambda j: (cm(j), 0))
    wide = pl.BlockSpec((rq, DI), lambda j: (cm(j), 0))
    in_specs = [wide, pl.BlockSpec((rq, 2 * NG * NS), lambda j: (cm(j), 0)), wide, blk, blk,
                pl.BlockSpec((SCAN_CH, NG * NS, HPG * HD), lambda j: (cm(j), 0, 0)), _full((1, DI)),
                _full(sel_w.shape), _full(ind_h.shape), _full(ind_e.shape)]
    args = [xs, bc, dy, dt, la, hprev, dskip_full, sel_w, ind_h, ind_e]
    aliases = {}
    if has_acc:
        in_specs += [big, blk, blk, blk, blk]
        args += list(acc)
        aliases = {10: 0, 11: 1, 12: 2, 13: 3, 14: 4}
    return pl.pallas_call(
        kern, out_shape=(S((T, 4096), f32), S((T, 128), f32), S((T, 128), f32), S((T, 128), f32), S((T, 128), f32)),
        grid=(npair,), in_specs=in_specs, out_specs=(big, blk, blk, blk, blk),
        scratch_shapes=[pltpu.VMEM((NG * NS, HPG * HD), f32), pltpu.VMEM((128, Q), f32)],
        input_output_aliases=aliases,
        compiler_params=_params("arbitrary"), name=name)(*args)


def _ynorm_fwd(ysum, xs, proj, dskip_full, nw, L):
    nlx = L // RT

    def kern(ys_ref, xs_ref, za_ref, zb_ref, dsk_ref, nw_ref, y_ref, yn_ref, ynt_ref):
        y = ys_ref[...] + dsk_ref[...] * xs_ref[...]
        y_ref[...] = y
        hg = NG // 2
        for g in range(NG):
            z_ref = za_ref if g < hg else zb_ref
            sl = y[:, g * GW:(g + 1) * GW] * _silu(z_ref[:, (g % hg) * GW:(g % hg + 1) * GW])
            r = lax.rsqrt(jnp.mean(sl * sl, axis=1, keepdims=True) + EPS)
            yn = (sl * r) * nw_ref[:, g * GW:(g + 1) * GW]
            yn_ref[:, g * GW:(g + 1) * GW] = yn.astype(bf16)
            ynt_ref[g * GW:(g + 1) * GW, :] = jnp.transpose(yn).astype(bf16)

    blk = pl.BlockSpec((RT, DI), lambda i: (i, 0))
    return pl.pallas_call(
        kern, out_shape=(S((L, DI), f32), S((L, DI), bf16), S((DI, L), bf16)), grid=(nlx,),
        in_specs=[blk, blk, pl.BlockSpec((RT, DI // 2), lambda i: (i, Z0 // (DI // 2))),
                  pl.BlockSpec((RT, DI // 2), lambda i: (i, Z0 // (DI // 2) + 1)), _full((1, DI)), _full((1, DI))],
        out_specs=(blk, blk, pl.BlockSpec((DI, RT), lambda i: (0, i))),
        compiler_params=_params("parallel"), name="ynorm_fwd")(ysum, xs, proj, proj, dskip_full, nw)


def _ynorm_bwd(dyn, y, xs, proj, dskip_full, nw, dproj):
    L = y.shape[0]
    T = proj.shape[0]
    nlx, nt = L // RT, T // RT

    hw = DI // 2

    def kern(dyn_ref, y_ref, xs_ref, z_ref, dsk_ref, nw_ref, _alias, dz_ref, dy_ref, dnw_ref, dsk_acc):
        i = pl.program_id(1)

        @pl.when(i == 0)
        def _():
            dnw_ref[...] = jnp.zeros_like(dnw_ref)
            dsk_acc[...] = jnp.zeros_like(dsk_acc)

        @pl.when(i >= nlx)
        def _():
            dz_ref[...] = jnp.zeros_like(dz_ref)
            dy_ref[...] = jnp.zeros_like(dy_ref)

        @pl.when(i < nlx)
        def _():
            y = y_ref[...]
            z = z_ref[...]
            sz = _sig(z)
            gz = z * sz
            yz = y * gz
            dynv = dyn_ref[...]
            for g in range(hw // GW):
                cs = slice(g * GW, (g + 1) * GW)
                sl = yz[:, cs]
                r = lax.rsqrt(jnp.mean(sl * sl, axis=1, keepdims=True) + EPS)
                yhat = sl * r
                dn = dynv[:, cs]
                dnw_ref[:, cs] += jnp.sum(dn * yhat, axis=0, keepdims=True)
                dyh = dn * nw_ref[:, cs]
                dyz = r * (dyh - yhat * jnp.mean(dyh * yhat, axis=1, keepdims=True))
                dyv = dyz * gz[:, cs]
                dy_ref[:, cs] = dyv
                dz_ref[:, cs] = (dyz * y[:, cs] * _dsilu(z[:, cs], sz[:, cs])).astype(bf16)
                dsk_acc[:, cs] += jnp.sum(dyv * xs_ref[:, cs], axis=0, keepdims=True)

    xblk = pl.BlockSpec((RT, hw), lambda j, i: (jnp.minimum(i, nlx - 1), j))
    row = pl.BlockSpec((1, hw), lambda j, i: (0, j))
    return pl.pallas_call(
        kern, out_shape=(S(dproj.shape, bf16), S((T, DI), f32), S((1, DI), f32), S((1, DI), f32)), grid=(2, nt),
        in_specs=[xblk, xblk, xblk, pl.BlockSpec((RT, hw), lambda j, i: (jnp.minimum(i, nlx - 1), Z0 // hw + j)), row, row,
                  pl.BlockSpec(memory_space=pl.ANY)],
        out_specs=(pl.BlockSpec((RT, hw), lambda j, i: (i, Z0 // hw + j)), pl.BlockSpec((RT, hw), lambda j, i: (i, j)), row, row),
        input_output_aliases={6: 0},
        compiler_params=_params("arbitrary", "arbitrary"), name="ynorm_bwd")(dyn, y, xs, proj, dskip_full, nw, dproj)


def _head_sums(cols):
    def kern(c_ref, o_ref):
        o_ref[...] = jnp.broadcast_to(jnp.sum(c_ref[...], axis=1, keepdims=True), (NH, 128))

    return pl.pallas_call(kern, out_shape=S((NH, 128), f32), name="head_sums")(cols)


SEG_STRIDE = 96
SEG_PAD = 16
NSEG = RT // GRID_W
CONF_ROWS = SEG_PAD + NSEG * SEG_STRIDE


SHIFT_ROWS = CONF_ROWS - 8
CONF_CW = 256


CONF_RB = 32


def _seg_zero_pads(scr):
    scr[0:SEG_PAD, :] = jnp.zeros((SEG_PAD, scr.shape[1]), f32)
    for s in range(NSEG):
        lo = SEG_PAD + s * SEG_STRIDE + GRID_W
        scr[lo:lo + SEG_STRIDE - GRID_W, :] = jnp.zeros((SEG_STRIDE - GRID_W, scr.shape[1]), f32)


def _seg_row(r0):
    return SEG_PAD + (r0 // GRID_W) * SEG_STRIDE + r0 % GRID_W


def _shift_copies(cps, scr, cs):
    full = scr[:, cs]
    for s in range(1, 8):
        cps[s - 1, :, :] = pltpu.roll(full, CONF_ROWS - s, 0)[0:SHIFT_ROWS, :]


def _tap(cps, scr, cs, o):
    rs = o % 8
    return scr[pl.ds(o, GRID_W), cs] if rs == 0 else cps[rs - 1, pl.ds(o - rs, GRID_W), :]


def _conf_fwd(proj, w32, cb, lnw, lnb, L):
    nlx = L // RT

    def kern(v_ref, g_ref, cg_ref, w_ref, cb_ref, lnw_ref, lnb_ref, u1_ref, u3_ref, u3t_ref, scr, cps, u3_scr):
        _seg_zero_pads(scr)
        for r0 in range(0, RT, CONF_RB):
            rows = slice(r0, r0 + CONF_RB)
            scr[_seg_row(r0):_seg_row(r0) + CONF_RB, :] = v_ref[rows, :] * _sig(g_ref[rows, :])
        for cc in range(D // CONF_CW):
            cs = slice(cc * CONF_CW, (cc + 1) * CONF_CW)
            _shift_copies(cps, scr, cs)
            for s in range(NSEG):
                acc = jnp.broadcast_to(cb_ref[:, cs], (GRID_W, CONF_CW))
                for k in range(CK):
                    acc = acc + w_ref[k:k + 1, cs] * _tap(cps, scr, cs, SEG_PAD + s * SEG_STRIDE + k - CK // 2)
                u1_ref[s * GRID_W:(s + 1) * GRID_W, cs] = acc
        for r0 in range(0, RT, CONF_RB):
            rows = slice(r0, r0 + CONF_RB)
            u1 = u1_ref[rows, :]
            xc = u1 - jnp.mean(u1, axis=1, keepdims=True)
            r = lax.rsqrt(jnp.mean(xc * xc, axis=1, keepdims=True) + EPS)
            u2 = (xc * r) * lnw_ref[...] + lnb_ref[...]
            u3 = _silu(u2) * _silu(cg_ref[rows, :])
            u3_ref[rows, :] = u3.astype(bf16)
            u3_scr[rows, :] = u3
        u3t_ref[...] = jnp.transpose(u3_scr[...]).astype(bf16)

    blk = pl.BlockSpec((RT, D), lambda i: (i, 0))
    return pl.pallas_call(
        kern, out_shape=(S((L, D), f32), S((L, D), bf16), S((D, L), bf16)), grid=(nlx,),
        in_specs=[pl.BlockSpec((RT, D), lambda i: (i, GV0 // D)), pl.BlockSpec((RT, D), lambda i: (i, GG0 // D)),
                  pl.BlockSpec((RT, D), lambda i: (i, CG0 // D)), _full((32, D)), _full((1, D)), _full((1, D)), _full((1, D))],
        out_specs=(blk, blk, pl.BlockSpec((D, RT), lambda i: (0, i))),
        scratch_shapes=[pltpu.VMEM((CONF_ROWS, D), f32), pltpu.VMEM((7, SHIFT_ROWS, CONF_CW), f32), pltpu.VMEM((RT, D), f32)],
        compiler_params=_params("parallel"), name="conf_fwd")(proj, proj, proj, w32, cb, lnw, lnb)


def _conf_bwd(du3, u1, proj, w32, lnw, lnb, dproj):
    L = u1.shape[0]
    T = proj.shape[0]
    nlx, nt = L // RT, T // RT

    def kern(du3_ref, u1_ref, v_ref, g_ref, cg_ref, w_ref, lnw_ref, lnb_ref, _alias,
             o_ref, dw_ref, dcb_ref, dlw_ref, dlb_ref, scr_u, scr_d, du0_scr, cps_u, cps_d):
        i = pl.program_id(0)

        @pl.when(i == 0)
        def _():
            dw_ref[...] = jnp.zeros_like(dw_ref)
            dcb_ref[...] = jnp.zeros_like(dcb_ref)
            dlw_ref[...] = jnp.zeros_like(dlw_ref)
            dlb_ref[...] = jnp.zeros_like(dlb_ref)

        @pl.when(i >= nlx)
        def _():
            o_ref[...] = jnp.zeros_like(o_ref)

        @pl.when(i < nlx)
        def _():
            _seg_zero_pads(scr_u)
            _seg_zero_pads(scr_d)
            for r0 in range(0, RT, CONF_RB):
                rows = slice(r0, r0 + CONF_RB)
                cg = cg_ref[rows, :]
                scg = _sig(cg)
                u1 = u1_ref[rows, :]
                xc = u1 - jnp.mean(u1, axis=1, keepdims=True)
                r = lax.rsqrt(jnp.mean(xc * xc, axis=1, keepdims=True) + EPS)
                xhat = xc * r
                u2 = xhat * lnw_ref[...] + lnb_ref[...]
                s2 = _sig(u2)
                du3v = du3_ref[rows, :]
                du2 = du3v * (cg * scg) * _dsilu(u2, s2)
                o_ref[rows, 2 * D:3 * D] = (du3v * (u2 * s2) * _dsilu(cg, scg)).astype(bf16)
                dlw_ref[...] += jnp.sum(du2 * xhat, axis=0, keepdims=True)
                dlb_ref[...] += jnp.sum(du2, axis=0, keepdims=True)
                dxh = du2 * lnw_ref[...]
                du1 = r * (dxh - jnp.mean(dxh, axis=1, keepdims=True) - xhat * jnp.mean(dxh * xhat, axis=1, keepdims=True))
                dcb_ref[...] += jnp.sum(du1, axis=0, keepdims=True)
                scr_u[_seg_row(r0):_seg_row(r0) + CONF_RB, :] = v_ref[rows, :] * _sig(g_ref[rows, :])
                scr_d[_seg_row(r0):_seg_row(r0) + CONF_RB, :] = du1
            for cc in range(D // CONF_CW):
                cs = slice(cc * CONF_CW, (cc + 1) * CONF_CW)
                _shift_copies(cps_u, scr_u, cs)
                _shift_copies(cps_d, scr_d, cs)
                for k in range(CK):
                    t = jnp.zeros((GRID_W, CONF_CW), f32)
                    for s in range(NSEG):
                        base = SEG_PAD + s * SEG_STRIDE
                        t = t + scr_d[pl.ds(base, GRID_W), cs] * _tap(cps_u, scr_u, cs, base + k - CK // 2)
                    dw_ref[k:k + 1, cs] += jnp.sum(t, axis=0, keepdims=True)
                for s in range(NSEG):
                    base = SEG_PAD + s * SEG_STRIDE
                    acc = jnp.zeros((GRID_W, CONF_CW), f32)
                    for k in range(CK):
                        acc = acc + w_ref[k:k + 1, cs] * _tap(cps_d, scr_d, cs, base + CK // 2 - k)
                    du0_scr[s * GRID_W:(s + 1) * GRID_W, cs] = acc
            for r0 in range(0, RT, CONF_RB):
                rows = slice(r0, r0 + CONF_RB)
                du0 = du0_scr[rows, :]
                sg = _sig(g_ref[rows, :])
                o_ref[rows, 0:D] = (du0 * sg).astype(bf16)
                o_ref[rows, D:2 * D] = (du0 * v_ref[rows, :] * sg * (1.0 - sg)).astype(bf16)

    xmap = lambda i: (jnp.minimum(i, nlx - 1), 0)
    pmap = lambda cb: (lambda i: (jnp.minimum(i, nlx - 1), cb))
    return pl.pallas_call(
        kern, out_shape=(S(dproj.shape, bf16), S((32, D), f32), S((1, D), f32), S((1, D), f32), S((1, D), f32)), grid=(nt,),
        in_specs=[pl.BlockSpec((RT, D), xmap), pl.BlockSpec((RT, D), xmap),
                  pl.BlockSpec((RT, D), pmap(GV0 // D)), pl.BlockSpec((RT, D), pmap(GG0 // D)), pl.BlockSpec((RT, D), pmap(CG0 // D)),
                  _full((32, D)), _full((1, D)), _full((1, D)), pl.BlockSpec(memory_space=pl.ANY)],
        out_specs=(pl.BlockSpec((RT, 3 * D), lambda i: (i, GV0 // (3 * D))), _full((32, D)), _full((1, D)), _full((1, D)), _full((1, D))),
        scratch_shapes=[pltpu.VMEM((CONF_ROWS, D), f32), pltpu.VMEM((CONF_ROWS, D), f32), pltpu.VMEM((RT, D), f32),
                        pltpu.VMEM((7, SHIFT_ROWS, CONF_CW), f32), pltpu.VMEM((7, SHIFT_ROWS, CONF_CW), f32)],
        input_output_aliases={8: 0},
        compiler_params=_params("arbitrary"), name="conf_bwd")(du3, u1, proj, proj, proj, w32, lnw, lnb, dproj)


def _merge_fwd(bs, bc, proj):
    L = bs.shape[0]

    def kern(bs_ref, bc_ref, g1_ref, g2_ref, o_ref, ot_ref):
        mv = _sig(g1_ref[...]) * bs_ref[...] + _sig(g2_ref[...]) * bc_ref[...]
        o_ref[...] = mv.astype(bf16)
        ot_ref[...] = jnp.transpose(mv).astype(bf16)

    rt = _pick(L, (2 * RT, RT))
    blk = pl.BlockSpec((rt, D), lambda i: (i, 0))
    return pl.pallas_call(
        kern, out_shape=(S((L, D), bf16), S((D, L), bf16)), grid=(L // rt,),
        in_specs=[blk, blk, pl.BlockSpec((rt, D), lambda i: (i, G10 // D)), pl.BlockSpec((rt, D), lambda i: (i, G20 // D))],
        out_specs=(blk, pl.BlockSpec((D, rt), lambda i: (0, i))),
        compiler_params=_params("parallel"), name="merge_fwd")(bs, bc, proj, proj)


def _merge_bwd(dmerged, bs, bc, proj):
    L = bs.shape[0]
    T = proj.shape[0]
    nlx, nt = L // RT, T // RT

    def kern(dm_ref, bs_ref, bc_ref, g1_ref, g2_ref, o_ref, dbs_ref, dbc_ref):
        i = pl.program_id(0)

        @pl.when(i >= nlx)
        def _():
            o_ref[...] = jnp.zeros_like(o_ref)

        @pl.when(i < nlx)
        def _():
            dm = dm_ref[...]
            s1 = _sig(g1_ref[...])
            s2 = _sig(g2_ref[...])
            dbs_ref[...] = (dm * s1).astype(bf16)
            dbc_ref[...] = (dm * s2).astype(bf16)
            o_ref[:, 0:D] = (dm * bs_ref[...] * s1 * (1.0 - s1)).astype(bf16)
            o_ref[:, D:2 * D] = (dm * bc_ref[...] * s2 * (1.0 - s2)).astype(bf16)

    xmap = lambda i: (jnp.minimum(i, nlx - 1), 0)
    pmap = lambda cb: (lambda i: (jnp.minimum(i, nlx - 1), cb))
    xblk = pl.BlockSpec((RT, D), xmap)
    return pl.pallas_call(
        kern, out_shape=(S((T, NP), bf16), S((L, D), bf16), S((L, D), bf16)), grid=(nt,),
        in_specs=[xblk, xblk, xblk, pl.BlockSpec((RT, D), pmap(G10 // D)), pl.BlockSpec((RT, D), pmap(G20 // D))],
        out_specs=(pl.BlockSpec((RT, 2 * D), lambda i: (i, G10 // (2 * D))), xblk, xblk),
        compiler_params=_params("arbitrary"), name="merge_bwd")(dmerged, bs, bc, proj, proj)


def _final(x, out, target, mod, fw):
    L = x.shape[0]

    def kern(x_ref, o_ref, t_ref, mod_ref, fw_ref, dx1_ref, dout_ref, loss_ref, dfw_ref, dg_ref):
        i = pl.program_id(0)

        @pl.when(i == 0)
        def _():
            loss_ref[...] = jnp.zeros_like(loss_ref)
            dfw_ref[...] = jnp.zeros_like(dfw_ref)
            dg_ref[...] = jnp.zeros_like(dg_ref)

        gate = mod_ref[0:1, 2 * D:3 * D]
        ov = o_ref[...]
        x1 = x_ref[...] + gate * ov
        r = lax.rsqrt(jnp.mean(x1 * x1, axis=1, keepdims=True) + EPS)
        xn = x1 * r
        fw = fw_ref[...]
        err = xn * fw - t_ref[...]
        part = 0.5 * jnp.sum(jnp.mean(err * err, axis=1, keepdims=True), axis=0, keepdims=True)
        loss_ref[...] += jnp.broadcast_to(part, (8, 128))
        dy = err * (1.0 / D)
        dfw_ref[...] += jnp.sum(dy * xn, axis=0, keepdims=True)
        dyw = dy * fw
        dx1 = r * (dyw - xn * jnp.mean(dyw * xn, axis=1, keepdims=True))
        dx1_ref[...] = dx1
        dout_ref[...] = (gate * dx1).astype(bf16)
        dg_ref[...] += jnp.sum(dx1 * ov, axis=0, keepdims=True)

    rt = _pick(L, (2 * RT, RT))
    blk = pl.BlockSpec((rt, D), lambda i: (i, 0))
    return pl.pallas_call(
        kern, out_shape=(S((L, D), f32), S((L, D), bf16), S((8, 128), f32), S((1, D), f32), S((1, D), f32)), grid=(L // rt,),
        in_specs=[blk, blk, blk, _full((8, 3 * D)), _full((1, D))],
        out_specs=(blk, blk, _full((8, 128)), _full((1, D)), _full((1, D))),
        compiler_params=_params("arbitrary"), name="final")(x, out, target, mod, fw)


def _me():
    return 4 * lax.axis_index("x") + 2 * lax.axis_index("y") + lax.axis_index("c")


def _xchg_copy(ins, outs, send_sems, recv_sems, modes, a, k, me):
    peer = lax.rem(me + k, N_DEV)
    pid = (peer // 4, lax.rem(peer // 2, 2), lax.rem(peer, 2))
    src = ins[a].at[peer] if modes[a] else ins[a]
    return pltpu.make_async_remote_copy(src_ref=src, dst_ref=outs[a].at[me], send_sem=send_sems.at[a, k - 1],
                                        recv_sem=recv_sems.at[a, k - 1], device_id=pid, device_id_type=MESH)


def _xchg_local(ins, outs, loc_sems, modes, a, me):
    return pltpu.make_async_copy(ins[a].at[me] if modes[a] else ins[a], outs[a].at[me], loc_sems.at[a])


def _xchg_start(ins, outs, send_sems, recv_sems, loc_sems, modes):
    me = _me()
    for a in range(len(modes)):
        _xchg_local(ins, outs, loc_sems, modes, a, me).start()
        for k in range(1, N_DEV):
            _xchg_copy(ins, outs, send_sems, recv_sems, modes, a, k, me).start()


def _xchg_wait(ins, outs, send_sems, recv_sems, loc_sems, modes):
    me = _me()
    for a in range(len(modes)):
        for k in range(1, N_DEV):
            frm = lax.rem(me + N_DEV - k, N_DEV)
            src = ins[a].at[frm] if modes[a] else ins[a]
            pltpu.make_async_remote_copy(src_ref=src, dst_ref=outs[a].at[frm], send_sem=send_sems.at[a, k - 1],
                                         recv_sem=recv_sems.at[a, k - 1], device_id=(0, 0, 0), device_id_type=MESH).wait_recv()
    for a in range(len(modes)):
        for k in range(1, N_DEV):
            _xchg_copy(ins, outs, send_sems, recv_sems, modes, a, k, me).wait_send()
        _xchg_local(ins, outs, loc_sems, modes, a, me).wait()


def _xchg_out_shapes(arrs, modes):
    return tuple(S((N_DEV,) + (a.shape[1:] if sc else a.shape), a.dtype) for a, sc in zip(arrs, modes))


def _xchg_sems(n):
    return [pltpu.SemaphoreType.DMA((n, N_DEV - 1)), pltpu.SemaphoreType.DMA((n, N_DEV - 1)), pltpu.SemaphoreType.DMA((n,))]


def _exchange(arrs, modes, name):
    n = len(arrs)

    def kern(*refs):
        ins, outs, sems = refs[:n], refs[n:2 * n], refs[2 * n:]
        _xchg_start(ins, outs, *sems, modes)
        _xchg_wait(ins, outs, *sems, modes)

    anyspec = pl.BlockSpec(memory_space=pl.ANY)
    return pl.pallas_call(
        kern, out_shape=_xchg_out_shapes(arrs, modes), in_specs=[anyspec] * n, out_specs=tuple([anyspec] * n),
        scratch_shapes=_xchg_sems(n), name=name)(*arrs)


def _gather2(arrs, name):
    n = len(arrs)

    def kern(*refs):
        ins, outs = refs[:n], refs[n:2 * n]
        send_sems, recv_sems, loc_sems = refs[2 * n:]
        x, y, c = lax.axis_index("x"), lax.axis_index("y"), lax.axis_index("c")
        me, sib = (x, y, c), (x, y, 1 - c)
        chips = [(1 - x, y), (x, 1 - y), (1 - x, 1 - y)]

        def slot(a, p):
            return outs[a].at[4 * p[0] + 2 * p[1] + p[2]]

        def cp(a, k, block, to, own=False):
            return pltpu.make_async_remote_copy(src_ref=ins[a] if own else slot(a, block), dst_ref=slot(a, block),
                                                send_sem=send_sems.at[a, k], recv_sem=recv_sems.at[a, k],
                                                device_id=to, device_id_type=MESH)

        started = []
        for a in range(n):
            pltpu.make_async_copy(ins[a], slot(a, me), loc_sems.at[a]).start()
            started.append(cp(a, 0, me, sib, own=True))
            started += [cp(a, 1 + j, me, (*chips[j], c), own=True) for j in range(2)]
        for s in started:
            s.start()
        for j in range(2):
            for a in range(n):
                cp(a, 1 + j, (*chips[j], c), me).wait_recv()
                fwd = cp(a, 4 + j, (*chips[j], c), sib)
                fwd.start()
                started.append(fwd)

            @pl.when(c == j)
            def _():
                for a in range(n):
                    cp(a, 3, (*chips[j], c), (*chips[1 - j], c)).start()
        for a in range(n):
            cp(a, 3, (*chips[2], c), me).wait_recv()
            fwd = cp(a, 6, (*chips[2], c), sib)
            fwd.start()
            started.append(fwd)
        for a in range(n):
            cp(a, 0, sib, me).wait_recv()
            for j in range(3):
                cp(a, 4 + j, (*chips[j], 1 - c), me).wait_recv()
        for s in started:
            s.wait_send()
        for a in range(n):
            cp(a, 3, me, me).wait_send()
            pltpu.make_async_copy(ins[a], slot(a, me), loc_sems.at[a]).wait()

    anyspec = pl.BlockSpec(memory_space=pl.ANY)
    return pl.pallas_call(
        kern, out_shape=_xchg_out_shapes(arrs, (False,) * n), in_specs=[anyspec] * n, out_specs=tuple([anyspec] * n),
        scratch_shapes=[pltpu.SemaphoreType.DMA((n, 7)), pltpu.SemaphoreType.DMA((n, 7)), pltpu.SemaphoreType.DMA((n,))],
        name=name)(*arrs)


def _adamw(parts, w, m, v, name):
    r, c = w.shape
    n_parts = parts.shape[0]
    tr = r
    for cand in (128, 64, 32, 16, 8):
        if r % cand == 0 and r > cand:
            tr = cand
            break
    c1 = 1.0 / (1.0 - ADAM_B1 ** ADAM_STEP)
    c2 = 1.0 / (1.0 - ADAM_B2 ** ADAM_STEP)

    def kern(p_ref, w_ref, m_ref, v_ref, g_ref, d_ref, m2_ref, v2_ref):
        g = p_ref[0].astype(f32)
        for i in range(1, n_parts):
            g = g + p_ref[i].astype(f32)
        g_ref[...] = g
        m2 = ADAM_B1 * m_ref[...] + (1.0 - ADAM_B1) * g
        v2 = ADAM_B2 * v_ref[...] + (1.0 - ADAM_B2) * (g * g)
        m2_ref[...] = m2
        v2_ref[...] = v2
        d_ref[...] = -ADAM_LR * ((m2 * c1) / (jnp.sqrt(v2 * c2) + ADAM_EPS) + ADAM_WD * w_ref[...])

    blk = pl.BlockSpec((tr, c), lambda i: (i, 0))
    sh = S((r, c), f32)
    return pl.pallas_call(
        kern, out_shape=(sh, sh, sh, sh), grid=(r // tr,),
        in_specs=[pl.BlockSpec((n_parts, tr, c), lambda i: (0, i, 0)), blk, blk, blk], out_specs=(blk, blk, blk, blk),
        compiler_params=_params("parallel"), name=name)(parts, w, m, v)


_SMALL = (("c_ctx", 1024), ("b_mod", 3072), ("norm_w", 1024), ("ssm_conv_b", 4096), ("dt_bias", 64), ("a_log", 64),
          ("d_skip", 32), ("ssm_norm_w", 2048), ("conf_conv_b", 1024), ("conf_ln_w", 1024), ("conf_ln_b", 1024),
          ("final_norm_w", 1024))
SMALL_TILE = 8 * 128


def _pack_small(d):
    rows = []
    for name, n in _SMALL:
        v = d[name].reshape(-1).astype(f32)
        pad = (-n) % SMALL_TILE
        if pad:
            v = jnp.concatenate([v, jnp.zeros((pad,), f32)])
        rows.append(v.reshape(-1, 128))
    return jnp.concatenate(rows, axis=0)


def _unpack_small(p, shapes):
    out, r0 = {}, 0
    for name, n in _SMALL:
        nr = 8 * ((n + SMALL_TILE - 1) // SMALL_TILE)
        out[name] = p[r0:r0 + nr].reshape(-1)[:n].reshape(shapes[name])
        r0 += nr
    return out


def _permute_w_in(w):
    return jnp.concatenate([w[:, 9280:11328], w[:, 2048:4096], w[:, 0:2048], w[:, 6208:9280], w[:, 4160:6208],
                            w[:, 4096:4160], jnp.zeros((w.shape[0], NP - DT0 - 64), w.dtype)], axis=1)


def _unpermute_w_in(wp):
    return jnp.concatenate([wp[:, PX0:PX0 + 2048], wp[:, PBC0:PBC0 + 2048], wp[:, DT0:DT0 + 64], wp[:, Z0:Z0 + 2048],
                            wp[:, GV0:GV0 + 3072], wp[:, G10:G10 + 2048]], axis=1)


def _cols_gathered(g):
    return jnp.transpose(g, (1, 0, 2)).reshape(g.shape[1], N_DEV * g.shape[2])


def _cols_to_blocks(a):
    r, c8 = a.shape
    return jnp.transpose(a.reshape(r, N_DEV, c8 // N_DEV), (1, 0, 2))


def kernel(x, c, ctx, c_ctx, w_mod, b_mod, norm_w, w_in, ssm_conv_w, ssm_conv_b, dt_bias, a_log, d_skip, ssm_norm_w, w_out_ssm, conf_conv_w, conf_conv_b, conf_ln_w, conf_ln_b, w_out_conf, w_out, final_norm_w, loss_target, m_c_ctx, m_w_mod, m_b_mod, m_norm_w, m_w_in, m_ssm_conv_w, m_ssm_conv_b, m_dt_bias, m_a_log, m_d_skip, m_ssm_norm_w, m_w_out_ssm, m_conf_conv_w, m_conf_conv_b, m_conf_ln_w, m_conf_ln_b, m_w_out_conf, m_w_out, m_final_norm_w, v_c_ctx, v_w_mod, v_b_mod, v_norm_w, v_w_in, v_ssm_conv_w, v_ssm_conv_b, v_dt_bias, v_a_log, v_d_skip, v_ssm_norm_w, v_w_out_ssm, v_conf_conv_w, v_conf_conv_b, v_conf_ln_w, v_conf_ln_b, v_w_out_conf, v_w_out, v_final_norm_w):
    L = x.shape[1]
    Lc = ctx.shape[1]
    T = L + Lc
    nlx = L // RT
    nxc = L // Q
    x2 = x.reshape(L, D)
    ctx2 = ctx.reshape(Lc, D)
    tgt = loss_target.reshape(L, D)

    gathered = _gather2([w_in[0].astype(bf16), w_mod[0].astype(bf16), ssm_conv_w[0], conf_conv_w[0]], name="gather_weights")
    wp = _permute_w_in(_cols_gathered(gathered[0]))
    wmod_bf = _cols_gathered(gathered[1])
    scw8 = jnp.concatenate([_cols_gathered(gathered[2]), jnp.zeros((8 - SK, 4096), f32)], axis=0)
    ccw32 = jnp.concatenate([_cols_gathered(gathered[3]), jnp.zeros((32 - CK, D), f32)], axis=0)

    norm_w1 = norm_w.reshape(1, D)
    scb = ssm_conv_b.reshape(1, 4096)
    bias_row = jnp.concatenate([dt_bias.reshape(1, 2 * NH), jnp.zeros((1, 128 - 2 * NH), f32)], axis=1)
    alog_row = jnp.concatenate([a_log.reshape(1, 2 * NH), jnp.zeros((1, 128 - 2 * NH), f32)], axis=1)
    dskip_full = jnp.repeat(d_skip.reshape(NH), HD).reshape(1, DI)
    snw = ssm_norm_w.reshape(1, DI)
    ccb = conf_conv_b.reshape(1, D)
    lnw = conf_ln_w.reshape(1, D)
    lnb = conf_ln_b.reshape(1, D)
    fw = final_norm_w.reshape(1, D)

    cc8 = jnp.concatenate([c.reshape(1, D), c_ctx.reshape(1, D), jnp.zeros((6, D), f32)], axis=0)
    mod, silu_rows = _mod_fwd(cc8, wmod_bf, b_mod.reshape(1, 3 * D))
    h, h_t = _prenorm(x2, ctx2, norm_w1, mod)
    proj, wos_g, woc_g, wo_g = _matmul(
        h, wp, f32, "proj_gather", tn=NP // 5,
        comm=([w_out_ssm[0].astype(bf16), w_out_conf[0].astype(bf16), w_out[0].astype(bf16)], (False,) * 3))
    wos_bf = wos_g.reshape(DI, D)
    woc_bf = woc_g.reshape(D, D)
    wo_bf = wo_g.reshape(D, D)
    xs = _ssm_conv_fwd(proj, scw8, scb, nlx, 0, f32, "ssm_conv_fwd_x")
    bcm = _ssm_conv_fwd(proj, scw8, scb, nlx, 1, bf16, "ssm_conv_fwd_bc")
    dt, la = _dt_prep(proj, bias_row, alog_row)
    consts_f, consts_b = _scan_consts(False), _scan_consts(True)
    yf, hp_f = _ssd_fwd(xs, bcm, dt, la, consts_f, False, nxc, "ssd_fwd_f")
    ysum, hp_b = _ssd_fwd(xs, bcm, dt, la, consts_b, True, nxc, "ssd_fwd_b", y_acc=yf)
    y, yn, yn_t = _ynorm_fwd(ysum, xs, proj, dskip_full, snw, L)
    bs = _matmul(yn, wos_bf, f32, "branch_ssm", tm=1024, tk=2048)
    u1, u3, u3_t = _conf_fwd(proj, ccw32, ccb, lnw, lnb, L)
    bc = _matmul(u3, woc_bf, f32, "branch_conf", tm=2048)
    merged, merged_t = _merge_fwd(bs, bc, proj)
    out = _matmul(merged, wo_bf, f32, "out_proj", tm=2048)
    dx1, dout, loss_acc, dfw, dgate = _final(x2, out, tgt, mod, fw)

    dmerged = _matmul(dout, wo_bf, f32, "d_merged", tb=True, tm=2048)
    g_wo = _matmul(merged_t, dout, bf16, "g_w_out", tm=1024, tk=2048)
    dproj, dbs, dbc = _merge_bwd(dmerged, bs, bc, proj)
    dyn = _matmul(dbs, wos_bf, f32, "d_yn", tb=True, tm=1024, tn=2048)
    g_wos = _matmul(yn_t, dbs, bf16, "g_w_out_ssm", tm=1024, tk=2048)
    du3 = _matmul(dbc, woc_bf, f32, "d_u3", tb=True, tm=2048)
    g_woc = _matmul(u3_t, dbc, bf16, "g_w_out_conf", tm=1024, tk=2048)
    dproj, g_ccw, g_ccb, g_lnw, g_lnb = _conf_bwd(du3, u1, proj, ccw32, lnw, lnb, dproj)
    dproj, dy, g_snw, dsk_cols = _ynorm_bwd(dyn, y, xs, proj, dskip_full, snw, dproj)
    acc_f = _ssd_bwd(xs, bcm, dy, dt, la, hp_f, dskip_full, consts_f, False, nxc, "ssd_bwd_f")
    dxbc, a1, a2, r2, sv = _ssd_bwd(xs, bcm, dy, dt, la, hp_b, dskip_full, consts_b, True, nxc, "ssd_bwd_b", acc=acc_f)
    dproj, g_dtb, g_alog = _dt_bwd(a1, a2, r2, sv, dt, la, proj, bias_row, alog_row, dproj)
    dpre, g_scw, g_scb = _ssm_conv_dpre(dxbc, proj, scw8, scb, nlx)
    dproj = _ssm_conv_t(dpre, scw8, dproj, nlx)
    g_wp, *parts_b = _matmul(
        h_t, dproj, bf16, "g_w_in_scatter", tm=1024, tn=NP // 5,
        comm=([g_wos.reshape(N_DEV, DI // N_DEV, D), g_woc.reshape(N_DEV, D // N_DEV, D), g_wo.reshape(N_DEV, D // N_DEV, D),
               _cols_to_blocks(g_scw[:SK]), _cols_to_blocks(g_ccw[:CK])], (True,) * 5))
    dh, parts_a = _matmul(dproj, wp, f32, "d_h_scatter", tb=True, tk=NP // 5,
                          comm=([_cols_to_blocks(_unpermute_w_in(g_wp))], (True,)))
    parts = [parts_a] + parts_b
    gx, g_nw, macc = _prenorm_bwd(x2, ctx2, dh, dx1, norm_w1, mod)
    dmod_x = jnp.concatenate([macc[0:1], macc[1:2], dgate], axis=1)
    dmod_c = jnp.concatenate([macc[2:3], macc[3:4], jnp.zeros((1, D), f32)], axis=1)
    dmod8 = jnp.concatenate([dmod_x, dmod_c, jnp.zeros((6, 3 * D), f32)], axis=0)
    ct = jnp.concatenate([c.reshape(D, 1), c_ctx.reshape(D, 1), jnp.zeros((D, 126), f32)], axis=1)
    g_bmod, g_cctx = _mod_bwd(ct, dmod8, wmod_bf)
    g_dskip = _head_sums(dsk_cols.reshape(NH, HD))[:, 0]

    small_g = _pack_small({
        "c_ctx": g_cctx[:, 0], "b_mod": g_bmod, "norm_w": g_nw, "ssm_conv_b": g_scb, "dt_bias": g_dtb[0, :2 * NH],
        "a_log": g_alog[0, :2 * NH], "d_skip": g_dskip, "ssm_norm_w": g_snw, "conf_conv_b": g_ccb, "conf_ln_w": g_lnw,
        "conf_ln_b": g_lnb, "final_norm_w": dfw})
    fac = jnp.concatenate([silu_rows[0:1].reshape(D // 128, 128), dmod_x.reshape(3 * D // 128, 128),
                           dmod_c.reshape(3 * D // 128, 128), loss_acc], axis=0)
    small_parts, fac_all = _exchange([small_g, fac], (False, False), name="exchange_tail")
    nr = D // 128
    sct = jnp.concatenate([fac_all[:, 0:nr].reshape(N_DEV, D).T, silu_rows[1:2].T, jnp.zeros((D, 128 - N_DEV - 1), f32)], axis=1)
    my_cols = (4 * lax.axis_index("x") + 2 * lax.axis_index("y") + lax.axis_index("c")) * (3 * D // N_DEV)
    dmx_all = lax.dynamic_slice(fac_all[:, nr:4 * nr].reshape(N_DEV, 3 * D), (0, my_cols), (N_DEV, 3 * D // N_DEV))
    dmc_all = lax.dynamic_slice(fac_all[:, 4 * nr:7 * nr].reshape(N_DEV, 3 * D), (0, my_cols), (N_DEV, 3 * D // N_DEV))
    g_wmod, loss_row = _wmod_grad(sct, dmx_all, dmc_all, fac_all[:, 7 * nr])
    parts = [parts[0], g_wmod[None]] + parts[1:]

    given = dict(c_ctx=c_ctx, w_mod=w_mod, b_mod=b_mod, norm_w=norm_w, w_in=w_in, ssm_conv_w=ssm_conv_w, ssm_conv_b=ssm_conv_b,
                 dt_bias=dt_bias, a_log=a_log, d_skip=d_skip, ssm_norm_w=ssm_norm_w, w_out_ssm=w_out_ssm, conf_conv_w=conf_conv_w,
                 conf_conv_b=conf_conv_b, conf_ln_w=conf_ln_w, conf_ln_b=conf_ln_b, w_out_conf=w_out_conf, w_out=w_out,
                 final_norm_w=final_norm_w)
    ms = dict(c_ctx=m_c_ctx, w_mod=m_w_mod, b_mod=m_b_mod, norm_w=m_norm_w, w_in=m_w_in, ssm_conv_w=m_ssm_conv_w,
              ssm_conv_b=m_ssm_conv_b, dt_bias=m_dt_bias, a_log=m_a_log, d_skip=m_d_skip, ssm_norm_w=m_ssm_norm_w,
              w_out_ssm=m_w_out_ssm, conf_conv_w=m_conf_conv_w, conf_conv_b=m_conf_conv_b, conf_ln_w=m_conf_ln_w,
              conf_ln_b=m_conf_ln_b, w_out_conf=m_w_out_conf, w_out=m_w_out, final_norm_w=m_final_norm_w)
    vs = dict(c_ctx=v_c_ctx, w_mod=v_w_mod, b_mod=v_b_mod, norm_w=v_norm_w, w_in=v_w_in, ssm_conv_w=v_ssm_conv_w,
              ssm_conv_b=v_ssm_conv_b, dt_bias=v_dt_bias, a_log=v_a_log, d_skip=v_d_skip, ssm_norm_w=v_ssm_norm_w,
              w_out_ssm=v_w_out_ssm, conf_conv_w=v_conf_conv_w, conf_conv_b=v_conf_conv_b, conf_ln_w=v_conf_ln_w,
              conf_ln_b=v_conf_ln_b, w_out_conf=v_w_out_conf, w_out=v_w_out, final_norm_w=v_final_norm_w)
    grads, deltas, new_m, new_v = {}, {}, {}, {}
    sharded = ("w_in", "w_mod", "w_out_ssm", "w_out_conf", "w_out", "ssm_conv_w", "conf_conv_w")
    for i, nm in enumerate(sharded):
        shp = given[nm].shape
        w2 = given[nm].reshape(shp[1], shp[2])
        res = _adamw(parts[i], w2, ms[nm].reshape(w2.shape), vs[nm].reshape(w2.shape), "adamw_" + nm)
        grads[nm], deltas[nm], new_m[nm], new_v[nm] = [r.reshape(shp) for r in res]
    shapes = {nm: given[nm].shape for nm, _ in _SMALL}
    res = _adamw(small_parts, _pack_small(given), _pack_small(ms), _pack_small(vs), "adamw_small")
    for dst, packed in zip((grads, deltas, new_m, new_v), res):
        dst.update(_unpack_small(packed, shapes))

    loss = loss_row[0, 0]
    order = ("c_ctx", "w_mod", "b_mod", "norm_w", "w_in", "ssm_conv_w", "ssm_conv_b", "dt_bias", "a_log", "d_skip", "ssm_norm_w",
             "w_out_ssm", "conf_conv_w", "conf_conv_b", "conf_ln_w", "conf_ln_b", "w_out_conf", "w_out", "final_norm_w")
    return (loss, gx.reshape(1, L, D), *[grads[n] for n in order], *[deltas[n] for n in order],
            *[new_m[n] for n in order], *[new_v[n] for n in order])
```

```python
import jax
import jax.numpy as jnp
from jax import lax
from jax.experimental import pallas as pl
from jax.experimental.pallas import tpu as pltpu

f32 = jnp.float32
bf16 = jnp.bfloat16

D = 1024
DI = 2048
NG = 8
HPG = 4
HD = 64
GW = HPG * HD
NS = 128
NH = 32
Q = 128
GRID_W = 64
CK = 31
SK = 4
EPS = 1e-6
RT = 256
N_DEV = 8
IN_COLS = 11328
G10, G20, PBC0, PX0, GV0, GG0, CG0, Z0, DT0, NP = 0, 1024, 2048, 4096, 6144, 7168, 8192, 9216, 11264, 11520
CONV_CT = 2048
B0, C0 = 2048, 3072
VMEM_LIMIT = 50 * 1024 * 1024
NEG = -1e30

ADAM_LR, ADAM_B1, ADAM_B2, ADAM_EPS, ADAM_WD, ADAM_STEP = 0.001, 0.9, 0.999, 1e-08, 0.01, 10

MESH = pl.DeviceIdType.MESH
S = jax.ShapeDtypeStruct


def _params(*sem):
    return pltpu.CompilerParams(dimension_semantics=tuple(sem) if sem else None, vmem_limit_bytes=VMEM_LIMIT)


def _sig(x):
    return 1.0 / (1.0 + jnp.exp(-x))


def _silu(x):
    return x * _sig(x)


def _dsilu(x, s):
    return s * (1.0 + x * (1.0 - s))


def _dot(a, b):
    return jnp.dot(a, b, preferred_element_type=f32)


def _dot_nt(a, b):
    return lax.dot_general(a, b, (((1,), (1,)), ((), ())), preferred_element_type=f32)


def _dot_tn(a, b):
    return lax.dot_general(a, b, (((0,), (0,)), ((), ())), preferred_element_type=f32)


def _dot3(t_bf, v):
    v1 = v.astype(bf16)
    r1 = v - v1.astype(f32)
    v2 = r1.astype(bf16)
    v3 = (r1 - v2.astype(f32)).astype(bf16)
    return _dot(t_bf, v1) + _dot(t_bf, v2) + _dot(t_bf, v3)


def _pick(n, prefs):
    for p in prefs:
        if n % p == 0:
            return p
    return n


def _full(shape):
    nd = len(shape)
    return pl.BlockSpec(shape, lambda *_: (0,) * nd)


def _matmul(a, b, out_dtype, name, tm=None, tn=None, tk=None, tb=False, comm=None):
    m, k = a.shape
    n = b.shape[0] if tb else b.shape[1]
    tm = tm if tm and m % tm == 0 else _pick(m, (768, 512, 256, 128))
    tn = tn if tn and n % tn == 0 else _pick(n, (1024, 512, 256, 128))
    tk = tk if tk and k % tk == 0 else _pick(k, (1024, 768, 512, 256, 128))
    nk = k // tk
    gi, gj = m // tm, n // tn
    carrs, modes = comm if comm else ((), ())
    nc = len(carrs)

    def kern(*refs):
        a_ref, b_ref = refs[:2]
        cins = refs[2:2 + nc]
        o_ref = refs[2 + nc]
        couts = refs[3 + nc:3 + 2 * nc]
        acc_ref = refs[3 + 2 * nc]
        sems = refs[4 + 2 * nc:]
        i, j, kk = pl.program_id(0), pl.program_id(1), pl.program_id(2)
        if nc:
            @pl.when(jnp.logical_and(jnp.logical_and(i == 0, j == 0), kk == 0))
            def _():
                _xchg_start(cins, couts, *sems, modes)

        part = _dot_nt(a_ref[...], b_ref[...]) if tb else _dot(a_ref[...], b_ref[...])
        if nk == 1:
            o_ref[...] = part.astype(o_ref.dtype)
        else:
            @pl.when(kk == 0)
            def _():
                acc_ref[...] = part

            @pl.when(kk > 0)
            def _():
                acc_ref[...] += part

            @pl.when(kk == nk - 1)
            def _():
                o_ref[...] = acc_ref[...].astype(o_ref.dtype)

        if nc:
            @pl.when(jnp.logical_and(jnp.logical_and(i == gi - 1, j == gj - 1), kk == nk - 1))
            def _():
                _xchg_wait(cins, couts, *sems, modes)

    anyspec = pl.BlockSpec(memory_space=pl.ANY)
    bspec = pl.BlockSpec((tn, tk), lambda i, j, kk: (j, kk)) if tb else pl.BlockSpec((tk, tn), lambda i, j, kk: (kk, j))
    out_shape = (S((m, n), out_dtype),) + _xchg_out_shapes(carrs, modes)
    res = pl.pallas_call(
        kern, out_shape=out_shape, grid=(gi, gj, nk),
        in_specs=[pl.BlockSpec((tm, tk), lambda i, j, kk: (i, kk)), bspec] + [anyspec] * nc,
        out_specs=(pl.BlockSpec((tm, tn), lambda i, j, kk: (i, j)),) + (anyspec,) * nc,
        scratch_shapes=[pltpu.VMEM((tm, tn), f32)] + (_xchg_sems(nc) if nc else []),
        compiler_params=_params(*((("arbitrary",) * 3) if nc else ("parallel", "parallel", "arbitrary"))), name=name)(a, b, *carrs)
    return res if nc else res[0]


def _mod_fwd(cc8, w_mod_bf, b_mod):
    def kern(c_ref, w_ref, b_ref, o_ref, s_ref):
        s = _silu(c_ref[...])
        s_ref[...] = s
        o_ref[...] = _dot(s.astype(bf16), w_ref[...]) + b_ref[...]

    return pl.pallas_call(kern, out_shape=(S((8, 3 * D), f32), S((8, D), f32)), compiler_params=_params(),
                          name="mod_fwd")(cc8, w_mod_bf, b_mod)


def _mod_bwd(ct, dmod8, w_mod_bf):
    tc = 512
    nj = 3 * D // tc

    def kern(ct_ref, dm_ref, w_ref, db_ref, dc_ref):
        j = pl.program_id(0)
        cx = ct_ref[:, 1:2]
        sx = _sig(cx)
        dmc = dm_ref[1:2, :]
        db_ref[...] = dm_ref[0:1, :] + dmc
        t = jnp.sum(w_ref[...].astype(f32) * dmc.astype(bf16).astype(f32), axis=1, keepdims=True) * _dsilu(cx, sx)

        @pl.when(j == 0)
        def _():
            dc_ref[...] = jnp.zeros_like(dc_ref)

        dc_ref[...] += jnp.broadcast_to(t, (D, 128))

    return pl.pallas_call(
        kern, out_shape=(S((1, 3 * D), f32), S((D, 128), f32)), grid=(nj,),
        in_specs=[_full((D, 128)), pl.BlockSpec((8, tc), lambda j: (0, j)), pl.BlockSpec((D, tc), lambda j: (0, j))],
        out_specs=(pl.BlockSpec((1, tc), lambda j: (0, j)), _full((D, 128))),
        compiler_params=_params("arbitrary"), name="mod_bwd")(ct, dmod8, w_mod_bf)


def _wmod_grad(sct, dmx, dmc, losses):
    cols = dmx.shape[1]

    def kern(s_ref, dmx_ref, dmc_ref, l_ref, g_ref, lo_ref):
        dmc_sum = dmc_ref[0:1, :]
        lsum = l_ref[0:1, :]
        for d in range(1, N_DEV):
            dmc_sum = dmc_sum + dmc_ref[d:d + 1, :]
            lsum = lsum + l_ref[d:d + 1, :]
        lo_ref[...] = lsum
        g = s_ref[:, N_DEV:N_DEV + 1] * dmc_sum
        for d in range(N_DEV):
            g = g + s_ref[:, d:d + 1] * dmx_ref[d:d + 1, :]
        g_ref[...] = g

    return pl.pallas_call(kern, out_shape=(S((D, cols), f32), S((1, 128), f32)), compiler_params=_params(),
                          name="wmod_grad")(sct, dmx, dmc, losses)


def _prenorm(x, ctx, norm_w, mod):
    L, Lc = x.shape[0], ctx.shape[0]
    nlx, nt = L // RT, (L + Lc) // RT

    def kern(x_ref, c_ref, nw_ref, mod_ref, h_ref, ht_ref):
        i = pl.program_id(0)
        is_c = i >= nlx
        xv = jnp.where(is_c, c_ref[...], x_ref[...])
        shift = jnp.where(is_c, mod_ref[1:2, 0:D], mod_ref[0:1, 0:D])
        scale = jnp.where(is_c, mod_ref[1:2, D:2 * D], mod_ref[0:1, D:2 * D])
        r = lax.rsqrt(jnp.mean(xv * xv, axis=1, keepdims=True) + EPS)
        hv = (xv * r) * nw_ref[...] * (1.0 + scale) + shift
        h_ref[...] = hv.astype(bf16)
        ht_ref[...] = jnp.transpose(hv).astype(bf16)

    return pl.pallas_call(
        kern, out_shape=(S((L + Lc, D), bf16), S((D, L + Lc), bf16)), grid=(nt,),
        in_specs=[pl.BlockSpec((RT, D), lambda i: (jnp.minimum(i, nlx - 1), 0)),
                  pl.BlockSpec((RT, D), lambda i: (jnp.maximum(i - nlx, 0), 0)),
                  _full((1, D)), _full((8, 3 * D))],
        out_specs=(pl.BlockSpec((RT, D), lambda i: (i, 0)), pl.BlockSpec((D, RT), lambda i: (0, i))),
        compiler_params=_params("parallel"), name="prenorm")(x, ctx, norm_w, mod)


def _prenorm_bwd(x, ctx, dh, dx1, norm_w, mod):
    L, Lc = x.shape[0], ctx.shape[0]
    nlx, nt = L // RT, (L + Lc) // RT

    def kern(x_ref, c_ref, dh_ref, dx1_ref, nw_ref, mod_ref, gx_ref, dnw_ref, acc_ref):
        i = pl.program_id(0)
        is_c = i >= nlx

        @pl.when(i == 0)
        def _():
            dnw_ref[...] = jnp.zeros_like(dnw_ref)
            acc_ref[...] = jnp.zeros_like(acc_ref)

        xv = jnp.where(is_c, c_ref[...], x_ref[...])
        scale = jnp.where(is_c, mod_ref[1:2, D:2 * D], mod_ref[0:1, D:2 * D])
        nw = nw_ref[...]
        r = lax.rsqrt(jnp.mean(xv * xv, axis=1, keepdims=True) + EPS)
        xn = xv * r
        dh = dh_ref[...]
        dsh = jnp.sum(dh, axis=0, keepdims=True)
        dsc = jnp.sum(dh * (xn * nw), axis=0, keepdims=True)
        dxnw = dh * (1.0 + scale)
        dnw_ref[...] += jnp.sum(dxnw * xn, axis=0, keepdims=True)
        dxn = dxnw * nw
        dx = r * (dxn - xn * jnp.mean(dxn * xn, axis=1, keepdims=True))

        @pl.when(jnp.logical_not(is_c))
        def _():
            gx_ref[...] = dx1_ref[...] + dx
            acc_ref[0:1, :] += dsh
            acc_ref[1:2, :] += dsc

        @pl.when(is_c)
        def _():
            acc_ref[2:3, :] += dsh
            acc_ref[3:4, :] += dsc

    xmap = lambda i: (jnp.minimum(i, nlx - 1), 0)
    return pl.pallas_call(
        kern, out_shape=(S((L, D), f32), S((1, D), f32), S((8, D), f32)), grid=(nt,),
        in_specs=[pl.BlockSpec((RT, D), xmap), pl.BlockSpec((RT, D), lambda i: (jnp.maximum(i - nlx, 0), 0)),
                  pl.BlockSpec((RT, D), lambda i: (i, 0)), pl.BlockSpec((RT, D), xmap), _full((1, D)), _full((8, 3 * D))],
        out_specs=(pl.BlockSpec((RT, D), xmap), _full((1, D)), _full((8, D))),
        compiler_params=_params("arbitrary"), name="prenorm_bwd")(x, ctx, dh, dx1, norm_w, mod)


def _xbc_col(j):
    return jnp.where(j == 0, PX0 // CONV_CT, PBC0 // CONV_CT)


def _halo_specs(nt_rows, ct, col=lambda j: j):
    cur = pl.BlockSpec((RT, ct), lambda i, j: (i, col(j)))
    prev = pl.BlockSpec((8, ct), lambda i, j: (jnp.maximum(i * (RT // 8) - 1, 0), col(j)))
    nxt = pl.BlockSpec((8, ct), lambda i, j: (jnp.minimum((i + 1) * (RT // 8), nt_rows // 8 - 1), col(j)))
    return cur, prev, nxt


def _fill_halo(scr, cur_ref, prev_ref, next_ref, i, nlx, nt):
    prev_ok = jnp.logical_and(i != 0, i != nlx)
    next_ok = jnp.logical_and(i != nlx - 1, i != nt - 1)
    scr[0:8, :] = jnp.where(prev_ok, prev_ref[...], 0.0)
    scr[8:8 + RT, :] = cur_ref[...]
    scr[8 + RT:16 + RT, :] = jnp.where(next_ok, next_ref[...], 0.0)


CONV_RB = 32


def _conv_blocks(ct):
    return [(slice(cb * 128, (cb + 1) * 128), r0) for cb in range(ct // 128) for r0 in range(0, RT, CONV_RB)]


def _taps(scr, cs, r0, shifts):
    blk = scr[r0:r0 + CONV_RB + 16, cs]
    n = CONV_RB + 16
    return [(blk if d == 0 else pltpu.roll(blk, (-d) % n, 0))[8:8 + CONV_RB, :] for d in shifts]


def _ssm_conv_fwd(proj, w8, b, nlx, half, out_dtype, name):
    T = proj.shape[0]
    nt = T // RT
    ct = CONV_CT
    cur, prev, nxt = _halo_specs(T, ct, lambda j: _xbc_col(j + half))

    def kern(cur_ref, prev_ref, next_ref, w_ref, b_ref, o_ref, scr):
        i = pl.program_id(0)
        _fill_halo(scr, cur_ref, prev_ref, next_ref, i, nlx, nt)
        for cs, r0 in _conv_blocks(ct):
            taps = _taps(scr, cs, r0, [k - 2 for k in range(SK)])
            acc = jnp.broadcast_to(b_ref[:, cs], (CONV_RB, 128))
            for k in range(SK):
                acc = acc + w_ref[k:k + 1, cs] * taps[k]
            o_ref[r0:r0 + CONV_RB, cs] = _silu(acc).astype(out_dtype)

    return pl.pallas_call(
        kern, out_shape=S((T, ct), out_dtype), grid=(nt, 1),
        in_specs=[cur, prev, nxt, pl.BlockSpec((8, ct), lambda i, j: (0, j + half)),
                  pl.BlockSpec((1, ct), lambda i, j: (0, j + half))],
        out_specs=pl.BlockSpec((RT, ct), lambda i, j: (i, j)),
        scratch_shapes=[pltpu.VMEM((RT + 16, ct), f32)],
        compiler_params=_params("parallel", "parallel"), name=name)(proj, proj, proj, w8, b)


def _ssm_conv_dpre(dxbc, proj, w8, b, nlx):
    T = proj.shape[0]
    nt = T // RT
    ct = CONV_CT
    cur = pl.BlockSpec((RT, ct), lambda j, i: (i, j))
    pcur = pl.BlockSpec((RT, ct), lambda j, i: (i, _xbc_col(j)))
    prev = pl.BlockSpec((8, ct), lambda j, i: (jnp.maximum(i * (RT // 8) - 1, 0), _xbc_col(j)))
    nxt = pl.BlockSpec((8, ct), lambda j, i: (jnp.minimum((i + 1) * (RT // 8), T // 8 - 1), _xbc_col(j)))

    def kern(d_ref, cur_ref, prev_ref, next_ref, w_ref, b_ref, dpre_ref, dw_ref, db_ref, scr):
        i = pl.program_id(1)
        _fill_halo(scr, cur_ref, prev_ref, next_ref, i, nlx, nt)

        @pl.when(i == 0)
        def _():
            dw_ref[...] = jnp.zeros_like(dw_ref)
            db_ref[...] = jnp.zeros_like(db_ref)

        for cb in range(ct // 128):
            cs = slice(cb * 128, (cb + 1) * 128)
            db_acc = jnp.zeros((CONV_RB, 128), f32)
            dw_acc = [jnp.zeros((CONV_RB, 128), f32) for _ in range(SK)]
            for r0 in range(0, RT, CONV_RB):
                taps = _taps(scr, cs, r0, [k - 2 for k in range(SK)])
                pre = jnp.broadcast_to(b_ref[:, cs], (CONV_RB, 128))
                for k in range(SK):
                    pre = pre + w_ref[k:k + 1, cs] * taps[k]
                dpre = d_ref[r0:r0 + CONV_RB, cs] * _dsilu(pre, _sig(pre))
                dpre_ref[r0:r0 + CONV_RB, cs] = dpre
                db_acc = db_acc + dpre
                dw_acc = [dw_acc[k] + dpre * taps[k] for k in range(SK)]
            db_ref[:, cs] += jnp.sum(db_acc, axis=0, keepdims=True)
            for k in range(SK):
                dw_ref[k:k + 1, cs] += jnp.sum(dw_acc[k], axis=0, keepdims=True)

    return pl.pallas_call(
        kern, out_shape=(S((T, 4096), f32), S((8, 4096), f32), S((1, 4096), f32)), grid=(4096 // ct, nt),
        in_specs=[cur, pcur, prev, nxt, pl.BlockSpec((8, ct), lambda j, i: (0, j)), pl.BlockSpec((1, ct), lambda j, i: (0, j))],
        out_specs=(cur, pl.BlockSpec((8, ct), lambda j, i: (0, j)), pl.BlockSpec((1, ct), lambda j, i: (0, j))),
        scratch_shapes=[pltpu.VMEM((RT + 16, ct), f32)],
        compiler_params=_params("parallel", "arbitrary"), name="ssm_conv_dpre")(dxbc, proj, proj, proj, w8, b)


def _ssm_conv_t(dpre, w8, dproj, nlx):
    T = dpre.shape[0]
    nt = T // RT
    ct = CONV_CT
    cur, prev, nxt = _halo_specs(T, ct)

    def kern(cur_ref, prev_ref, next_ref, w_ref, _alias, o_ref, scr):
        i = pl.program_id(0)
        _fill_halo(scr, cur_ref, prev_ref, next_ref, i, nlx, nt)
        for cs, r0 in _conv_blocks(ct):
            taps = _taps(scr, cs, r0, [2 - k for k in range(SK)])
            acc = jnp.zeros((CONV_RB, 128), f32)
            for k in range(SK):
                acc = acc + w_ref[k:k + 1, cs] * taps[k]
            o_ref[r0:r0 + CONV_RB, cs] = acc.astype(bf16)

    return pl.pallas_call(
        kern, out_shape=S(dproj.shape, bf16), grid=(nt, 4096 // ct),
        in_specs=[cur, prev, nxt, pl.BlockSpec((8, ct), lambda i, j: (0, j)), pl.BlockSpec(memory_space=pl.ANY)],
        out_specs=pl.BlockSpec((RT, ct), lambda i, j: (i, _xbc_col(j))),
        scratch_shapes=[pltpu.VMEM((RT + 16, ct), f32)], input_output_aliases={4: 0},
        compiler_params=_params("parallel", "parallel"), name="ssm_conv_t")(dpre, dpre, dpre, w8, dproj)


DT_CH = 6


def _tri():
    li = lax.broadcasted_iota(jnp.int32, (Q, Q), 0)
    si = lax.broadcasted_iota(jnp.int32, (Q, Q), 1)
    return (si <= li).astype(bf16), (si >= li).astype(bf16)


def _dt_prep(proj, bias_row, alog_row):
    T = proj.shape[0]
    nch = T // Q
    assert nch % DT_CH == 0

    def kern(raw_ref, b_ref, al_ref, dt_ref, la_ref):
        lane = lax.broadcasted_iota(jnp.int32, (Q, 128), 1)
        a = jnp.where(lane[0:1, :] < 2 * NH, -jnp.exp(al_ref[...]), 0.0)
        tri, trit = _tri()
        for h in range(DT_CH):
            rows = slice(h * Q, (h + 1) * Q)
            v = raw_ref[rows, :] + b_ref[...]
            dt = jnp.maximum(v, 0.0) + jnp.log1p(jnp.exp(-jnp.abs(v)))
            da = dt * a
            dt_ref[rows, :] = dt
            la_ref[rows, :] = jnp.where(lane < NH, _dot3(tri, da), _dot3(trit, da))

    rq = DT_CH * Q
    return pl.pallas_call(
        kern, out_shape=(S((T, 128), f32), S((T, 128), f32)), grid=(nch // DT_CH,),
        in_specs=[pl.BlockSpec((rq, 128), lambda c: (c, DT0 // 128)), _full((1, 128)), _full((1, 128))],
        out_specs=(pl.BlockSpec((rq, 128), lambda c: (c, 0)), pl.BlockSpec((rq, 128), lambda c: (c, 0))),
        compiler_params=_params("parallel"), name="dt_prep")(proj, bias_row, alog_row)


def _dt_bwd(a1, a2, r2, sv, dt, la, proj, bias_row, alog_row, dproj):
    T = proj.shape[0]
    nch = T // Q
    assert nch % DT_CH == 0
    rq = DT_CH * Q
    blk = pl.BlockSpec((rq, 128), lambda c: (c, 0))

    def kern(a1_ref, a2_ref, r2_ref, s_ref, dt_ref, la_ref, raw_ref, b_ref, al_ref, _alias, o_ref, db_ref, dal_ref):
        c = pl.program_id(0)

        @pl.when(c == 0)
        def _():
            db_ref[...] = jnp.zeros_like(db_ref)
            dal_ref[...] = jnp.zeros_like(dal_ref)

        lane = lax.broadcasted_iota(jnp.int32, (Q, 128), 1)
        row = lax.broadcasted_iota(jnp.int32, (Q, 128), 0)
        fwd = lane < NH
        a = jnp.where(lane[0:1, :] < 2 * NH, -jnp.exp(al_ref[...]), 0.0)
        is_end = row == jnp.where(fwd, Q - 1, 0)
        tri, trit = _tri()
        o_ref[...] = jnp.zeros_like(o_ref)
        for h in range(DT_CH):
            rows = slice(h * Q, (h + 1) * Q)
            dt = dt_ref[rows, :]
            la = la_ref[rows, :]
            a2v = a2_ref[rows, :]
            r2v = r2_ref[rows, :]
            la_e = jnp.where(fwd[0:1, :], la[Q - 1:Q, :], la[0:1, :])
            e_end = jnp.exp(la_e - la)
            wend = e_end * dt
            extra = s_ref[h * Q:h * Q + 1, :] * jnp.exp(la_e) + jnp.sum(wend * a2v, axis=0, keepdims=True)
            dla = a1_ref[rows, :] - dt * r2v - wend * a2v + jnp.where(is_end, extra, 0.0)
            rcs = jnp.where(fwd, _dot3(trit, dla), _dot3(tri, dla))
            ddt = r2v + e_end * a2v + a * rcs
            dal_ref[...] += a * jnp.sum(dt * rcs, axis=0, keepdims=True)
            draw = jnp.where(lane < 2 * NH, ddt * _sig(raw_ref[rows, :] + b_ref[...]), 0.0)
            db_ref[...] += jnp.sum(draw, axis=0, keepdims=True)
            o_ref[rows, 0:128] = draw.astype(bf16)

    return pl.pallas_call(
        kern, out_shape=(S(dproj.shape, bf16), S((1, 128), f32), S((1, 128), f32)), grid=(nch // DT_CH,),
        in_specs=[blk, blk, blk, blk, blk, blk, pl.BlockSpec((rq, 128), lambda c: (c, DT0 // 128)),
                  _full((1, 128)), _full((1, 128)), pl.BlockSpec(memory_space=pl.ANY)],
        out_specs=(pl.BlockSpec((rq, NP - DT0), lambda c: (c, DT0 // (NP - DT0))), _full((1, 128)), _full((1, 128))),
        input_output_aliases={9: 0},
        compiler_params=_params("arbitrary"), name="dt_bwd")(a1, a2, r2, sv, dt, la, proj, bias_row, alog_row, dproj)


def _split2(v):
    hi = v.astype(bf16)
    lo = (v - hi.astype(f32)).astype(bf16)
    return jnp.concatenate([hi, lo], axis=1)


def _scan_consts(rev):
    hoff = NH if rev else 0
    g = jnp.arange(NG, dtype=jnp.int32)[:, None, None]

    def rc(nr, ncol):
        return jnp.arange(nr, dtype=jnp.int32)[None, :, None], jnp.arange(ncol, dtype=jnp.int32)[None, None, :]

    r, c = rc(2 * 128, HPG * HD)
    sel_w = (lax.rem(r, 128) == hoff + HPG * g + c // HD).astype(bf16)
    r, c = rc(HPG * HD, 128)
    ind_h = (c == hoff + HPG * g + r // HD).astype(bf16)
    return sel_w, ind_h


def _masks(rev):
    li = lax.broadcasted_iota(jnp.int32, (Q, Q), 0)
    si = lax.broadcasted_iota(jnp.int32, (Q, Q), 1)
    mask = (li <= si) if rev else (li >= si)
    mask_t = (li >= si) if rev else (li <= si)
    lane = lax.broadcasted_iota(jnp.int32, (Q, HPG * HD), 1)
    hms = [jnp.logical_and(lane >= r * HD, lane < (r + 1) * HD) for r in range(HPG)]
    return mask, mask_t, hms


def _mine(hoff):
    lane = lax.broadcasted_iota(jnp.int32, (Q, 128), 1)
    return jnp.logical_and(lane >= hoff, lane < hoff + NH)


def _head_row(vals, hc0):
    lane = lax.broadcasted_iota(jnp.int32, (1, HPG * HD), 1)
    out = jnp.zeros((1, HPG * HD), f32)
    for r in range(HPG):
        out = jnp.where(jnp.logical_and(lane >= r * HD, lane < (r + 1) * HD), vals[:, hc0 + r:hc0 + r + 1], out)
    return out


SCAN_CH = 2


def _chunk_of(j, rev, nxc, nch):
    return (nch - 1 - j) if rev else lax.rem(j + nxc, nch)


def _ssd_fwd(xs, bc, dt, la, consts, rev, nxc, name, y_acc=None):
    T = xs.shape[0]
    nch = T // Q
    hoff = NH if rev else 0
    e = 0 if rev else Q - 1
    cm = lambda j: _chunk_of(j, rev, nxc // SCAN_CH, nch // SCAN_CH)
    sel_w = consts[0]
    has_acc = y_acc is not None

    def kern(*refs):
        xs_ref, bc_ref, dt_ref, la_ref, sw_ref = refs[:5]
        yacc_ref = refs[5] if has_acc else None
        y_ref, hp_ref, h_ref = refs[5 + has_acc:]
        j = pl.program_id(0)

        @pl.when(j == 0)
        def _():
            h_ref[...] = jnp.zeros_like(h_ref)

        mask, _, hms = _masks(rev)
        for hh in range(SCAN_CH):
            h = SCAN_CH - 1 - hh if rev else hh
            chunk(refs, mask, hms, h, slice(h * Q, (h + 1) * Q))

    def chunk(refs, mask, hms, h, rows):
        xs_ref, bc_ref, dt_ref, la_ref, sw_ref = refs[:5]
        yacc_ref = refs[5] if has_acc else None
        y_ref, hp_ref, h_ref = refs[5 + has_acc:]
        hp_ref[h] = h_ref[...]
        la_all = la_ref[rows, :]
        dt_all = dt_ref[rows, :]
        la_t = jnp.transpose(la_all)
        dt_t = jnp.transpose(dt_all)
        la_e = la_all[e:e + 1, :]
        w2 = _split2(jnp.exp(jnp.where(_mine(hoff), la_e - la_all, 0.0)) * dt_all)
        e2 = _split2(jnp.exp(la_all))
        ela_e = jnp.exp(la_e)
        for g in range(NG):
            hc0 = hoff + g * HPG
            x = xs_ref[rows, g * GW:(g + 1) * GW]
            bb = bc_ref[rows, g * NS:(g + 1) * NS]
            cb = bc_ref[rows, NG * NS + g * NS:NG * NS + (g + 1) * NS]
            ht = h_ref[g * NS:(g + 1) * NS, :]
            scores = _dot_nt(cb, bb)
            yoff = _dot(cb, ht.astype(bf16))
            wend = _dot(w2, sw_ref[g])
            expla = _dot(e2, sw_ref[g])
            mixes, xstack = [], []
            for r in range(HPG):
                hc = hc0 + r
                la_rep = jnp.broadcast_to(la_all[:, hc:hc + 1], (Q, 128))
                decay = jnp.exp(jnp.where(mask, la_rep - la_t[hc:hc + 1, :], NEG))
                mixes.append((scores * decay * dt_t[hc:hc + 1, :]).astype(bf16))
                xstack.append(jnp.where(hms[r], x, 0.0).astype(bf16))
            y = _dot(jnp.concatenate(mixes, axis=1), jnp.concatenate(xstack, axis=0)) + yoff * expla
            if has_acc:
                y = y + yacc_ref[rows, g * GW:(g + 1) * GW]
            y_ref[rows, g * GW:(g + 1) * GW] = y
            h_ref[g * NS:(g + 1) * NS, :] = ht * _head_row(ela_e, hc0) + _dot_tn(bb, (x * wend).astype(bf16))

    row = lambda j: (cm(j), 0)
    rq = SCAN_CH * Q
    yblk = pl.BlockSpec((rq, DI), row)
    return pl.pallas_call(
        kern, out_shape=(S((T, DI), f32), S((nch, NG * NS, HPG * HD), f32)), grid=(nch // SCAN_CH,),
        in_specs=[yblk, pl.BlockSpec((rq, 2 * NG * NS), row), pl.BlockSpec((rq, 128), row), pl.BlockSpec((rq, 128), row),
                  _full(sel_w.shape)] + ([yblk] if has_acc else []),
        out_specs=(yblk, pl.BlockSpec((SCAN_CH, NG * NS, HPG * HD), lambda j: (cm(j), 0, 0))),
        scratch_shapes=[pltpu.VMEM((NG * NS, HPG * HD), f32)],
        input_output_aliases={5: 0} if has_acc else {},
        compiler_params=_params("arbitrary"), name=name)(xs, bc, dt, la, sel_w, *([y_acc] if has_acc else []))


def _ssd_bwd(xs, bc, dy, dt, la, hprev, dskip_full, consts, rev, nxc, name, acc=None):
    T = xs.shape[0]
    nch = T // Q
    hoff = NH if rev else 0
    e = 0 if rev else Q - 1
    npair = nch // SCAN_CH
    cm = lambda j: _chunk_of(npair - 1 - j, rev, nxc // SCAN_CH, npair)
    has_acc = acc is not None
    sel_w, ind_h = consts

    def kern(*refs):
        g_ref = refs[-2]
        j = pl.program_id(0)

        @pl.when(j == 0)
        def _():
            g_ref[...] = jnp.zeros_like(g_ref)

        masks = _masks(rev)
        for hh in range(SCAN_CH):
            h = hh if rev else SCAN_CH - 1 - hh
            chunk(refs, masks, h, slice(h * Q, (h + 1) * Q))

    def chunk(refs, masks, h, rows):
        xs_ref, bc_ref, dy_ref, dt_ref, la_ref, hp_ref, dsk_ref, sw_ref, ih_ref = refs[:9]
        k = 9
        if has_acc:
            dxbc_in, a1_in, a2_in, r2_in, s_in = refs[k:k + 5]
            k += 5
        dxbc_ref, a1_ref, a2_ref, r2_ref, s_ref, g_ref, r2scr = refs[k:k + 7]
        mask, mask_t, hms = masks
        lane128 = lax.broadcasted_iota(jnp.int32, (Q, 128), 1)
        la_all = la_ref[rows, :]
        dt_all = dt_ref[rows, :]
        la_t = jnp.transpose(la_all)
        dt_t = jnp.transpose(dt_all)
        la_e = la_all[e:e + 1, :]
        w2 = _split2(jnp.exp(jnp.where(_mine(hoff), la_e - la_all, 0.0)) * dt_all)
        e2 = _split2(jnp.exp(la_all))
        wed2 = jnp.concatenate([w2, e2, _split2(dt_all)], axis=0)
        ela_e = jnp.exp(la_e)
        r2scr[...] = jnp.zeros_like(r2scr)
        a1acc = jnp.zeros((Q, 128), f32)
        a2acc = jnp.zeros((Q, 128), f32)
        sacc = jnp.zeros((1, 128), f32)
        for g in range(NG):
            hc0 = hoff + g * HPG
            x = xs_ref[rows, g * GW:(g + 1) * GW]
            bb = bc_ref[rows, g * NS:(g + 1) * NS]
            cb = bc_ref[rows, NG * NS + g * NS:NG * NS + (g + 1) * NS]
            dyv = dy_ref[rows, g * GW:(g + 1) * GW]
            gt = g_ref[g * NS:(g + 1) * NS, :]
            ht = hp_ref[h, g * NS:(g + 1) * NS, :]
            gtb = gt.astype(bf16)
            htb = ht.astype(bf16)
            xb = x.astype(bf16)
            scores = _dot_nt(cb, bb)
            scores_t = _dot_nt(bb, cb)
            bg = _dot(bb, gtb)
            yoff = _dot(cb, htb)
            sel3 = _dot(wed2, sw_ref[g])
            wend, expla, dtf = sel3[0:Q], sel3[Q:2 * Q], sel3[2 * Q:3 * Q]
            dym = jnp.concatenate([jnp.where(hms[r], dyv, 0.0).astype(bf16) for r in range(HPG)], axis=0)
            dyx_all = _dot_nt(dym, xb)
            sdts, ems = [], []
            wsum = jnp.zeros((Q, Q), f32)
            for r in range(HPG):
                hc = hc0 + r
                la_rep = jnp.broadcast_to(la_all[:, hc:hc + 1], (Q, 128))
                la_r = la_t[hc:hc + 1, :]
                dt_r = dt_t[hc:hc + 1, :]
                decay = jnp.exp(jnp.where(mask, la_rep - la_r, NEG))
                decay_t = jnp.exp(jnp.where(mask_t, la_r - la_rep, NEG))
                dyx = dyx_all[r * Q:(r + 1) * Q, :]
                fm = dyx * (scores * decay)
                r2scr[hc:hc + 1, :] = jnp.sum(fm, axis=0, keepdims=True)
                ems.append(fm * dt_r)
                wsum = wsum + dyx * decay * dt_r
                sdts.append((scores_t * decay_t).astype(bf16))
            dx = dtf * _dot(jnp.concatenate(sdts, axis=1), dym) + wend * bg
            if not has_acc:
                dx = dx + dsk_ref[:, g * GW:(g + 1) * GW] * dyv
            red3 = _dot(jnp.concatenate([(dyv * yoff * expla).astype(bf16), (x * bg).astype(bf16), (gt * ht).astype(bf16)],
                                        axis=0), ih_ref[g])
            a1acc = a1acc + red3[0:Q]
            for r in range(HPG):
                a1acc = jnp.where(lane128 == hc0 + r, a1acc + jnp.sum(ems[r], axis=1, keepdims=True), a1acc)
            a2acc = a2acc + red3[Q:2 * Q]
            sacc = sacc + jnp.sum(red3[2 * Q:3 * Q], axis=0, keepdims=True)
            wb = wsum.astype(bf16)
            dysb = (dyv * expla).astype(bf16)
            dc = _dot(wb, bb) + _dot_nt(dysb, htb)
            db = _dot_tn(wb, cb) + _dot_nt((x * wend).astype(bf16), gtb)
            g_ref[g * NS:(g + 1) * NS, :] = gt * _head_row(ela_e, hc0) + _dot_tn(cb, dysb)
            if has_acc:
                dx = dx + dxbc_in[rows, g * GW:(g + 1) * GW]
                db = db + dxbc_in[rows, B0 + g * NS:B0 + (g + 1) * NS]
                dc = dc + dxbc_in[rows, C0 + g * NS:C0 + (g + 1) * NS]
            dxbc_ref[rows, g * GW:(g + 1) * GW] = dx
            dxbc_ref[rows, B0 + g * NS:B0 + (g + 1) * NS] = db
            dxbc_ref[rows, C0 + g * NS:C0 + (g + 1) * NS] = dc
        r2c = jnp.transpose(r2scr[...])
        sc = jnp.broadcast_to(sacc, (Q, 128))
        if has_acc:
            a1acc = a1acc + a1_in[rows, :]
            a2acc = a2acc + a2_in[rows, :]
            r2c = r2c + r2_in[rows, :]
            sc = sc + s_in[rows, :]
        a1_ref[rows, :] = a1acc
        a2_ref[rows, :] = a2acc
        r2_ref[rows, :] = r2c
        s_ref[rows, :] = sc

    rq = SCAN_CH * Q
    blk = pl.BlockSpec((rq, 128), lambda j: (cm(j), 0))
    big = pl.BlockSpec((rq, 4096), lambda j: (cm(j), 0))
    wide = pl.BlockSpec((rq, DI), lambda j: (cm(j), 0))
    in_specs = [wide, pl.BlockSpec((rq, 2 * NG * NS), lambda j: (cm(j), 0)), wide, blk, blk,
                pl.BlockSpec((SCAN_CH, NG * NS, HPG * HD), lambda j: (cm(j), 0, 0)), _full((1, DI)),
                _full(sel_w.shape), _full(ind_h.shape)]
    args = [xs, bc, dy, dt, la, hprev, dskip_full, sel_w, ind_h]
    aliases = {}
    if has_acc:
        in_specs += [big, blk, blk, blk, blk]
        args += list(acc)
        aliases = {9: 0, 10: 1, 11: 2, 12: 3, 13: 4}
    return pl.pallas_call(
        kern, out_shape=(S((T, 4096), f32), S((T, 128), f32), S((T, 128), f32), S((T, 128), f32), S((T, 128), f32)),
        grid=(npair,), in_specs=in_specs, out_specs=(big, blk, blk, blk, blk),
        scratch_shapes=[pltpu.VMEM((NG * NS, HPG * HD), f32), pltpu.VMEM((128, Q), f32)],
        input_output_aliases=aliases,
        compiler_params=_params("arbitrary"), name=name)(*args)


def _ynorm_fwd(ysum, xs, proj, dskip_full, nw, L):
    nlx = L // RT

    def kern(ys_ref, xs_ref, za_ref, zb_ref, dsk_ref, nw_ref, y_ref, yn_ref, ynt_ref):
        y = ys_ref[...] + dsk_ref[...] * xs_ref[...]
        y_ref[...] = y
        hg = NG // 2
        for g in range(NG):
            z_ref = za_ref if g < hg else zb_ref
            sl = y[:, g * GW:(g + 1) * GW] * _silu(z_ref[:, (g % hg) * GW:(g % hg + 1) * GW])
            r = lax.rsqrt(jnp.mean(sl * sl, axis=1, keepdims=True) + EPS)
            yn = (sl * r) * nw_ref[:, g * GW:(g + 1) * GW]
            yn_ref[:, g * GW:(g + 1) * GW] = yn.astype(bf16)
            ynt_ref[g * GW:(g + 1) * GW, :] = jnp.transpose(yn).astype(bf16)

    blk = pl.BlockSpec((RT, DI), lambda i: (i, 0))
    return pl.pallas_call(
        kern, out_shape=(S((L, DI), f32), S((L, DI), bf16), S((DI, L), bf16)), grid=(nlx,),
        in_specs=[blk, blk, pl.BlockSpec((RT, DI // 2), lambda i: (i, Z0 // (DI // 2))),
                  pl.BlockSpec((RT, DI // 2), lambda i: (i, Z0 // (DI // 2) + 1)), _full((1, DI)), _full((1, DI))],
        out_specs=(blk, blk, pl.BlockSpec((DI, RT), lambda i: (0, i))),
        compiler_params=_params("parallel"), name="ynorm_fwd")(ysum, xs, proj, proj, dskip_full, nw)


def _ynorm_bwd(dyn, y, xs, proj, dskip_full, nw, dproj):
    L = y.shape[0]
    T = proj.shape[0]
    nlx, nt = L // RT, T // RT

    hw = DI // 2

    def kern(dyn_ref, y_ref, xs_ref, z_ref, dsk_ref, nw_ref, _alias, dz_ref, dy_ref, dnw_ref, dsk_acc):
        i = pl.program_id(1)

        @pl.when(i == 0)
        def _():
            dnw_ref[...] = jnp.zeros_like(dnw_ref)
            dsk_acc[...] = jnp.zeros_like(dsk_acc)

        @pl.when(i >= nlx)
        def _():
            dz_ref[...] = jnp.zeros_like(dz_ref)
            dy_ref[...] = jnp.zeros_like(dy_ref)

        @pl.when(i < nlx)
        def _():
            y = y_ref[...]
            z = z_ref[...]
            sz = _sig(z)
            gz = z * sz
            yz = y * gz
            dynv = dyn_ref[...]
            for g in range(hw // GW):
                cs = slice(g * GW, (g + 1) * GW)
                sl = yz[:, cs]
                r = lax.rsqrt(jnp.mean(sl * sl, axis=1, keepdims=True) + EPS)
                yhat = sl * r
                dn = dynv[:, cs]
                dnw_ref[:, cs] += jnp.sum(dn * yhat, axis=0, keepdims=True)
                dyh = dn * nw_ref[:, cs]
                dyz = r * (dyh - yhat * jnp.mean(dyh * yhat, axis=1, keepdims=True))
                dyv = dyz * gz[:, cs]
                dy_ref[:, cs] = dyv
                dz_ref[:, cs] = (dyz * y[:, cs] * _dsilu(z[:, cs], sz[:, cs])).astype(bf16)
                dsk_acc[:, cs] += jnp.sum(dyv * xs_ref[:, cs], axis=0, keepdims=True)

    xblk = pl.BlockSpec((RT, hw), lambda j, i: (jnp.minimum(i, nlx - 1), j))
    row = pl.BlockSpec((1, hw), lambda j, i: (0, j))
    return pl.pallas_call(
        kern, out_shape=(S(dproj.shape, bf16), S((T, DI), f32), S((1, DI), f32), S((1, DI), f32)), grid=(2, nt),
        in_specs=[xblk, xblk, xblk, pl.BlockSpec((RT, hw), lambda j, i: (jnp.minimum(i, nlx - 1), Z0 // hw + j)), row, row,
                  pl.BlockSpec(memory_space=pl.ANY)],
        out_specs=(pl.BlockSpec((RT, hw), lambda j, i: (i, Z0 // hw + j)), pl.BlockSpec((RT, hw), lambda j, i: (i, j)), row, row),
        input_output_aliases={6: 0},
        compiler_params=_params("arbitrary", "arbitrary"), name="ynorm_bwd")(dyn, y, xs, proj, dskip_full, nw, dproj)


def _head_sums(cols):
    def kern(c_ref, o_ref):
        o_ref[...] = jnp.broadcast_to(jnp.sum(c_ref[...], axis=1, keepdims=True), (NH, 128))

    return pl.pallas_call(kern, out_shape=S((NH, 128), f32), name="head_sums")(cols)


SEG_STRIDE = 96
SEG_PAD = 16
NSEG = RT // GRID_W
CONF_ROWS = SEG_PAD + NSEG * SEG_STRIDE


SHIFT_ROWS = CONF_ROWS - 8
CONF_CW = 256


CONF_RB = 32


def _seg_zero_pads(scr):
    scr[0:SEG_PAD, :] = jnp.zeros((SEG_PAD, scr.shape[1]), f32)
    for s in range(NSEG):
        lo = SEG_PAD + s * SEG_STRIDE + GRID_W
        scr[lo:lo + SEG_STRIDE - GRID_W, :] = jnp.zeros((SEG_STRIDE - GRID_W, scr.shape[1]), f32)


def _seg_row(r0):
    return SEG_PAD + (r0 // GRID_W) * SEG_STRIDE + r0 % GRID_W


def _shift_copies(cps, scr, cs):
    full = scr[:, cs]
    for s in range(1, 8):
        cps[s - 1, :, :] = pltpu.roll(full, CONF_ROWS - s, 0)[0:SHIFT_ROWS, :]


def _tap(cps, scr, cs, o):
    rs = o % 8
    return scr[pl.ds(o, GRID_W), cs] if rs == 0 else cps[rs - 1, pl.ds(o - rs, GRID_W), :]


def _conf_fwd(proj, w32, cb, lnw, lnb, L):
    nlx = L // RT

    def kern(v_ref, g_ref, cg_ref, w_ref, cb_ref, lnw_ref, lnb_ref, u1_ref, u3_ref, u3t_ref, scr, cps, u3_scr):
        _seg_zero_pads(scr)
        for r0 in range(0, RT, CONF_RB):
            rows = slice(r0, r0 + CONF_RB)
            scr[_seg_row(r0):_seg_row(r0) + CONF_RB, :] = v_ref[rows, :] * _sig(g_ref[rows, :])
        for cc in range(D // CONF_CW):
            cs = slice(cc * CONF_CW, (cc + 1) * CONF_CW)
            _shift_copies(cps, scr, cs)
            for s in range(NSEG):
                acc = jnp.broadcast_to(cb_ref[:, cs], (GRID_W, CONF_CW))
                for k in range(CK):
                    acc = acc + w_ref[k:k + 1, cs] * _tap(cps, scr, cs, SEG_PAD + s * SEG_STRIDE + k - CK // 2)
                u1_ref[s * GRID_W:(s + 1) * GRID_W, cs] = acc
        for r0 in range(0, RT, CONF_RB):
            rows = slice(r0, r0 + CONF_RB)
            u1 = u1_ref[rows, :]
            xc = u1 - jnp.mean(u1, axis=1, keepdims=True)
            r = lax.rsqrt(jnp.mean(xc * xc, axis=1, keepdims=True) + EPS)
            u2 = (xc * r) * lnw_ref[...] + lnb_ref[...]
            u3 = _silu(u2) * _silu(cg_ref[rows, :])
            u3_ref[rows, :] = u3.astype(bf16)
            u3_scr[rows, :] = u3
        u3t_ref[...] = jnp.transpose(u3_scr[...]).astype(bf16)

    blk = pl.BlockSpec((RT, D), lambda i: (i, 0))
    return pl.pallas_call(
        kern, out_shape=(S((L, D), f32), S((L, D), bf16), S((D, L), bf16)), grid=(nlx,),
        in_specs=[pl.BlockSpec((RT, D), lambda i: (i, GV0 // D)), pl.BlockSpec((RT, D), lambda i: (i, GG0 // D)),
                  pl.BlockSpec((RT, D), lambda i: (i, CG0 // D)), _full((32, D)), _full((1, D)), _full((1, D)), _full((1, D))],
        out_specs=(blk, blk, pl.BlockSpec((D, RT), lambda i: (0, i))),
        scratch_shapes=[pltpu.VMEM((CONF_ROWS, D), f32), pltpu.VMEM((7, SHIFT_ROWS, CONF_CW), f32), pltpu.VMEM((RT, D), f32)],
        compiler_params=_params("parallel"), name="conf_fwd")(proj, proj, proj, w32, cb, lnw, lnb)


def _conf_bwd(du3, u1, proj, w32, lnw, lnb, dproj):
    L = u1.shape[0]
    T = proj.shape[0]
    nlx, nt = L // RT, T // RT

    def kern(du3_ref, u1_ref, v_ref, g_ref, cg_ref, w_ref, lnw_ref, lnb_ref, _alias,
             o_ref, dw_ref, dcb_ref, dlw_ref, dlb_ref, scr_u, scr_d, du0_scr, cps_u, cps_d):
        i = pl.program_id(0)

        @pl.when(i == 0)
        def _():
            dw_ref[...] = jnp.zeros_like(dw_ref)
            dcb_ref[...] = jnp.zeros_like(dcb_ref)
            dlw_ref[...] = jnp.zeros_like(dlw_ref)
            dlb_ref[...] = jnp.zeros_like(dlb_ref)

        @pl.when(i >= nlx)
        def _():
            o_ref[...] = jnp.zeros_like(o_ref)

        @pl.when(i < nlx)
        def _():
            _seg_zero_pads(scr_u)
            _seg_zero_pads(scr_d)
            for r0 in range(0, RT, CONF_RB):
                rows = slice(r0, r0 + CONF_RB)
                cg = cg_ref[rows, :]
                scg = _sig(cg)
                u1 = u1_ref[rows, :]
                xc = u1 - jnp.mean(u1, axis=1, keepdims=True)
                r = lax.rsqrt(jnp.mean(xc * xc, axis=1, keepdims=True) + EPS)
                xhat = xc * r
                u2 = xhat * lnw_ref[...] + lnb_ref[...]
                s2 = _sig(u2)
                du3v = du3_ref[rows, :]
                du2 = du3v * (cg * scg) * _dsilu(u2, s2)
                o_ref[rows, 2 * D:3 * D] = (du3v * (u2 * s2) * _dsilu(cg, scg)).astype(bf16)
                dlw_ref[...] += jnp.sum(du2 * xhat, axis=0, keepdims=True)
                dlb_ref[...] += jnp.sum(du2, axis=0, keepdims=True)
                dxh = du2 * lnw_ref[...]
                du1 = r * (dxh - jnp.mean(dxh, axis=1, keepdims=True) - xhat * jnp.mean(dxh * xhat, axis=1, keepdims=True))
                dcb_ref[...] += jnp.sum(du1, axis=0, keepdims=True)
                scr_u[_seg_row(r0):_seg_row(r0) + CONF_RB, :] = v_ref[rows, :] * _sig(g_ref[rows, :])
                scr_d[_seg_row(r0):_seg_row(r0) + CONF_RB, :] = du1
            for cc in range(D // CONF_CW):
                cs = slice(cc * CONF_CW, (cc + 1) * CONF_CW)
                _shift_copies(cps_u, scr_u, cs)
                _shift_copies(cps_d, scr_d, cs)
                for k in range(CK):
                    t = jnp.zeros((GRID_W, CONF_CW), f32)
                    for s in range(NSEG):
                        base = SEG_PAD + s * SEG_STRIDE
                        t = t + scr_d[pl.ds(base, GRID_W), cs] * _tap(cps_u, scr_u, cs, base + k - CK // 2)
                    dw_ref[k:k + 1, cs] += jnp.sum(t, axis=0, keepdims=True)
                for s in range(NSEG):
                    base = SEG_PAD + s * SEG_STRIDE
                    acc = jnp.zeros((GRID_W, CONF_CW), f32)
                    for k in range(CK):
                        acc = acc + w_ref[k:k + 1, cs] * _tap(cps_d, scr_d, cs, base + CK // 2 - k)
                    du0_scr[s * GRID_W:(s + 1) * GRID_W, cs] = acc
            for r0 in range(0, RT, CONF_RB):
                rows = slice(r0, r0 + CONF_RB)
                du0 = du0_scr[rows, :]
                sg = _sig(g_ref[rows, :])
                o_ref[rows, 0:D] = (du0 * sg).astype(bf16)
                o_ref[rows, D:2 * D] = (du0 * v_ref[rows, :] * sg * (1.0 - sg)).astype(bf16)

    xmap = lambda i: (jnp.minimum(i, nlx - 1), 0)
    pmap = lambda cb: (lambda i: (jnp.minimum(i, nlx - 1), cb))
    return pl.pallas_call(
        kern, out_shape=(S(dproj.shape, bf16), S((32, D), f32), S((1, D), f32), S((1, D), f32), S((1, D), f32)), grid=(nt,),
        in_specs=[pl.BlockSpec((RT, D), xmap), pl.BlockSpec((RT, D), xmap),
                  pl.BlockSpec((RT, D), pmap(GV0 // D)), pl.BlockSpec((RT, D), pmap(GG0 // D)), pl.BlockSpec((RT, D), pmap(CG0 // D)),
                  _full((32, D)), _full((1, D)), _full((1, D)), pl.BlockSpec(memory_space=pl.ANY)],
        out_specs=(pl.BlockSpec((RT, 3 * D), lambda i: (i, GV0 // (3 * D))), _full((32, D)), _full((1, D)), _full((1, D)), _full((1, D))),
        scratch_shapes=[pltpu.VMEM((CONF_ROWS, D), f32), pltpu.VMEM((CONF_ROWS, D), f32), pltpu.VMEM((RT, D), f32),
                        pltpu.VMEM((7, SHIFT_ROWS, CONF_CW), f32), pltpu.VMEM((7, SHIFT_ROWS, CONF_CW), f32)],
        input_output_aliases={8: 0},
        compiler_params=_params("arbitrary"), name="conf_bwd")(du3, u1, proj, proj, proj, w32, lnw, lnb, dproj)


def _merge_fwd(bs, bc, proj):
    L = bs.shape[0]

    def kern(bs_ref, bc_ref, g1_ref, g2_ref, o_ref, ot_ref):
        mv = _sig(g1_ref[...]) * bs_ref[...] + _sig(g2_ref[...]) * bc_ref[...]
        o_ref[...] = mv.astype(bf16)
        ot_ref[...] = jnp.transpose(mv).astype(bf16)

    rt = _pick(L, (2 * RT, RT))
    blk = pl.BlockSpec((rt, D), lambda i: (i, 0))
    return pl.pallas_call(
        kern, out_shape=(S((L, D), bf16), S((D, L), bf16)), grid=(L // rt,),
        in_specs=[blk, blk, pl.BlockSpec((rt, D), lambda i: (i, G10 // D)), pl.BlockSpec((rt, D), lambda i: (i, G20 // D))],
        out_specs=(blk, pl.BlockSpec((D, rt), lambda i: (0, i))),
        compiler_params=_params("parallel"), name="merge_fwd")(bs, bc, proj, proj)


def _merge_bwd(dmerged, bs, bc, proj):
    L = bs.shape[0]
    T = proj.shape[0]
    nlx, nt = L // RT, T // RT

    def kern(dm_ref, bs_ref, bc_ref, g1_ref, g2_ref, o_ref, dbs_ref, dbc_ref):
        i = pl.program_id(0)

        @pl.when(i >= nlx)
        def _():
            o_ref[...] = jnp.zeros_like(o_ref)

        @pl.when(i < nlx)
        def _():
            dm = dm_ref[...]
            s1 = _sig(g1_ref[...])
            s2 = _sig(g2_ref[...])
            dbs_ref[...] = (dm * s1).astype(bf16)
            dbc_ref[...] = (dm * s2).astype(bf16)
            o_ref[:, 0:D] = (dm * bs_ref[...] * s1 * (1.0 - s1)).astype(bf16)
            o_ref[:, D:2 * D] = (dm * bc_ref[...] * s2 * (1.0 - s2)).astype(bf16)

    xmap = lambda i: (jnp.minimum(i, nlx - 1), 0)
    pmap = lambda cb: (lambda i: (jnp.minimum(i, nlx - 1), cb))
    xblk = pl.BlockSpec((RT, D), xmap)
    return pl.pallas_call(
        kern, out_shape=(S((T, NP), bf16), S((L, D), bf16), S((L, D), bf16)), grid=(nt,),
        in_specs=[xblk, xblk, xblk, pl.BlockSpec((RT, D), pmap(G10 // D)), pl.BlockSpec((RT, D), pmap(G20 // D))],
        out_specs=(pl.BlockSpec((RT, 2 * D), lambda i: (i, G10 // (2 * D))), xblk, xblk),
        compiler_params=_params("arbitrary"), name="merge_bwd")(dmerged, bs, bc, proj, proj)


def _final(x, out, target, mod, fw):
    L = x.shape[0]

    def kern(x_ref, o_ref, t_ref, mod_ref, fw_ref, dx1_ref, dout_ref, loss_ref, dfw_ref, dg_ref):
        i = pl.program_id(0)

        @pl.when(i == 0)
        def _():
            loss_ref[...] = jnp.zeros_like(loss_ref)
            dfw_ref[...] = jnp.zeros_like(dfw_ref)
            dg_ref[...] = jnp.zeros_like(dg_ref)

        gate = mod_ref[0:1, 2 * D:3 * D]
        ov = o_ref[...]
        x1 = x_ref[...] + gate * ov
        r = lax.rsqrt(jnp.mean(x1 * x1, axis=1, keepdims=True) + EPS)
        xn = x1 * r
        fw = fw_ref[...]
        err = xn * fw - t_ref[...]
        part = 0.5 * jnp.sum(jnp.mean(err * err, axis=1, keepdims=True), axis=0, keepdims=True)
        loss_ref[...] += jnp.broadcast_to(part, (8, 128))
        dy = err * (1.0 / D)
        dfw_ref[...] += jnp.sum(dy * xn, axis=0, keepdims=True)
        dyw = dy * fw
        dx1 = r * (dyw - xn * jnp.mean(dyw * xn, axis=1, keepdims=True))
        dx1_ref[...] = dx1
        dout_ref[...] = (gate * dx1).astype(bf16)
        dg_ref[...] += jnp.sum(dx1 * ov, axis=0, keepdims=True)

    rt = _pick(L, (2 * RT, RT))
    blk = pl.BlockSpec((rt, D), lambda i: (i, 0))
    return pl.pallas_call(
        kern, out_shape=(S((L, D), f32), S((L, D), bf16), S((8, 128), f32), S((1, D), f32), S((1, D), f32)), grid=(L // rt,),
        in_specs=[blk, blk, blk, _full((8, 3 * D)), _full((1, D))],
        out_specs=(blk, blk, _full((8, 128)), _full((1, D)), _full((1, D))),
        compiler_params=_params("arbitrary"), name="final")(x, out, target, mod, fw)


def _me():
    return 4 * lax.axis_index("x") + 2 * lax.axis_index("y") + lax.axis_index("c")


def _xchg_copy(ins, outs, send_sems, recv_sems, modes, a, k, me):
    peer = lax.rem(me + k, N_DEV)
    pid = (peer // 4, lax.rem(peer // 2, 2), lax.rem(peer, 2))
    src = ins[a].at[peer] if modes[a] else ins[a]
    return pltpu.make_async_remote_copy(src_ref=src, dst_ref=outs[a].at[me], send_sem=send_sems.at[a, k - 1],
                                        recv_sem=recv_sems.at[a, k - 1], device_id=pid, device_id_type=MESH)


def _xchg_local(ins, outs, loc_sems, modes, a, me):
    return pltpu.make_async_copy(ins[a].at[me] if modes[a] else ins[a], outs[a].at[me], loc_sems.at[a])


def _xchg_start(ins, outs, send_sems, recv_sems, loc_sems, modes):
    me = _me()
    for a in range(len(modes)):
        _xchg_local(ins, outs, loc_sems, modes, a, me).start()
        for k in range(1, N_DEV):
            _xchg_copy(ins, outs, send_sems, recv_sems, modes, a, k, me).start()


def _xchg_wait(ins, outs, send_sems, recv_sems, loc_sems, modes):
    me = _me()
    for a in range(len(modes)):
        for k in range(1, N_DEV):
            frm = lax.rem(me + N_DEV - k, N_DEV)
            src = ins[a].at[frm] if modes[a] else ins[a]
            pltpu.make_async_remote_copy(src_ref=src, dst_ref=outs[a].at[frm], send_sem=send_sems.at[a, k - 1],
                                         recv_sem=recv_sems.at[a, k - 1], device_id=(0, 0, 0), device_id_type=MESH).wait_recv()
    for a in range(len(modes)):
        for k in range(1, N_DEV):
            _xchg_copy(ins, outs, send_sems, recv_sems, modes, a, k, me).wait_send()
        _xchg_local(ins, outs, loc_sems, modes, a, me).wait()


def _xchg_out_shapes(arrs, modes):
    return tuple(S((N_DEV,) + (a.shape[1:] if sc else a.shape), a.dtype) for a, sc in zip(arrs, modes))


def _xchg_sems(n):
    return [pltpu.SemaphoreType.DMA((n, N_DEV - 1)), pltpu.SemaphoreType.DMA((n, N_DEV - 1)), pltpu.SemaphoreType.DMA((n,))]


def _exchange(arrs, modes, name):
    n = len(arrs)

    def kern(*refs):
        ins, outs, sems = refs[:n], refs[n:2 * n], refs[2 * n:]
        _xchg_start(ins, outs, *sems, modes)
        _xchg_wait(ins, outs, *sems, modes)

    anyspec = pl.BlockSpec(memory_space=pl.ANY)
    return pl.pallas_call(
        kern, out_shape=_xchg_out_shapes(arrs, modes), in_specs=[anyspec] * n, out_specs=tuple([anyspec] * n),
        scratch_shapes=_xchg_sems(n), name=name)(*arrs)


def _gather2(arrs, name):
    n = len(arrs)

    def kern(*refs):
        ins, outs = refs[:n], refs[n:2 * n]
        send_sems, recv_sems, loc_sems = refs[2 * n:]
        x, y, c = lax.axis_index("x"), lax.axis_index("y"), lax.axis_index("c")
        me, sib = (x, y, c), (x, y, 1 - c)
        chips = [(1 - x, y), (x, 1 - y), (1 - x, 1 - y)]

        def slot(a, p):
            return outs[a].at[4 * p[0] + 2 * p[1] + p[2]]

        def cp(a, k, block, to, own=False):
            return pltpu.make_async_remote_copy(src_ref=ins[a] if own else slot(a, block), dst_ref=slot(a, block),
                                                send_sem=send_sems.at[a, k], recv_sem=recv_sems.at[a, k],
                                                device_id=to, device_id_type=MESH)

        started = []
        for a in range(n):
            pltpu.make_async_copy(ins[a], slot(a, me), loc_sems.at[a]).start()
            started.append(cp(a, 0, me, sib, own=True))
            started += [cp(a, 1 + j, me, (*chips[j], c), own=True) for j in range(2)]
        for s in started:
            s.start()
        for j in range(2):
            for a in range(n):
                cp(a, 1 + j, (*chips[j], c), me).wait_recv()
                fwd = cp(a, 4 + j, (*chips[j], c), sib)
                fwd.start()
                started.append(fwd)

            @pl.when(c == j)
            def _():
                for a in range(n):
                    cp(a, 3, (*chips[j], c), (*chips[1 - j], c)).start()
        for a in range(n):
            cp(a, 3, (*chips[2], c), me).wait_recv()
            fwd = cp(a, 6, (*chips[2], c), sib)
            fwd.start()
            started.append(fwd)
        for a in range(n):
            cp(a, 0, sib, me).wait_recv()
            for j in range(3):
                cp(a, 4 + j, (*chips[j], 1 - c), me).wait_recv()
        for s in started:
            s.wait_send()
        for a in range(n):
            cp(a, 3, me, me).wait_send()
            pltpu.make_async_copy(ins[a], slot(a, me), loc_sems.at[a]).wait()

    anyspec = pl.BlockSpec(memory_space=pl.ANY)
    return pl.pallas_call(
        kern, out_shape=_xchg_out_shapes(arrs, (False,) * n), in_specs=[anyspec] * n, out_specs=tuple([anyspec] * n),
        scratch_shapes=[pltpu.SemaphoreType.DMA((n, 7)), pltpu.SemaphoreType.DMA((n, 7)), pltpu.SemaphoreType.DMA((n,))],
        name=name)(*arrs)


def _adamw(parts, w, m, v, name):
    r, c = w.shape
    n_parts = parts.shape[0]
    tr = r
    for cand in (128, 64, 32, 16, 8):
        if r % cand == 0 and r > cand:
            tr = cand
            break
    c1 = 1.0 / (1.0 - ADAM_B1 ** ADAM_STEP)
    c2 = 1.0 / (1.0 - ADAM_B2 ** ADAM_STEP)

    def kern(p_ref, w_ref, m_ref, v_ref, g_ref, d_ref, m2_ref, v2_ref):
        g = p_ref[0].astype(f32)
        for i in range(1, n_parts):
            g = g + p_ref[i].astype(f32)
        g_ref[...] = g
        m2 = ADAM_B1 * m_ref[...] + (1.0 - ADAM_B1) * g
        v2 = ADAM_B2 * v_ref[...] + (1.0 - ADAM_B2) * (g * g)
        m2_ref[...] = m2
        v2_ref[...] = v2
        d_ref[...] = -ADAM_LR * ((m2 * c1) / (jnp.sqrt(v2 * c2) + ADAM_EPS) + ADAM_WD * w_ref[...])

    blk = pl.BlockSpec((tr, c), lambda i: (i, 0))
    sh = S((r, c), f32)
    return pl.pallas_call(
        kern, out_shape=(sh, sh, sh, sh), grid=(r // tr,),
        in_specs=[pl.BlockSpec((n_parts, tr, c), lambda i: (0, i, 0)), blk, blk, blk], out_specs=(blk, blk, blk, blk),
        compiler_params=_params("parallel"), name=name)(parts, w, m, v)


_SMALL = (("c_ctx", 1024), ("b_mod", 3072), ("norm_w", 1024), ("ssm_conv_b", 4096), ("dt_bias", 64), ("a_log", 64),
          ("d_skip", 32), ("ssm_norm_w", 2048), ("conf_conv_b", 1024), ("conf_ln_w", 1024), ("conf_ln_b", 1024),
          ("final_norm_w", 1024))
SMALL_TILE = 8 * 128


def _pack_small(d):
    rows = []
    for name, n in _SMALL:
        v = d[name].reshape(-1).astype(f32)
        pad = (-n) % SMALL_TILE
        if pad:
            v = jnp.concatenate([v, jnp.zeros((pad,), f32)])
        rows.append(v.reshape(-1, 128))
    return jnp.concatenate(rows, axis=0)


def _unpack_small(p, shapes):
    out, r0 = {}, 0
    for name, n in _SMALL:
        nr = 8 * ((n + SMALL_TILE - 1) // SMALL_TILE)
        out[name] = p[r0:r0 + nr].reshape(-1)[:n].reshape(shapes[name])
        r0 += nr
    return out


def _permute_w_in(w):
    return jnp.concatenate([w[:, 9280:11328], w[:, 2048:4096], w[:, 0:2048], w[:, 6208:9280], w[:, 4160:6208],
                            w[:, 4096:4160], jnp.zeros((w.shape[0], NP - DT0 - 64), w.dtype)], axis=1)


def _unpermute_w_in(wp):
    return jnp.concatenate([wp[:, PX0:PX0 + 2048], wp[:, PBC0:PBC0 + 2048], wp[:, DT0:DT0 + 64], wp[:, Z0:Z0 + 2048],
                            wp[:, GV0:GV0 + 3072], wp[:, G10:G10 + 2048]], axis=1)


def _cols_gathered(g):
    return jnp.transpose(g, (1, 0, 2)).reshape(g.shape[1], N_DEV * g.shape[2])


def _cols_to_blocks(a):
    r, c8 = a.shape
    return jnp.transpose(a.reshape(r, N_DEV, c8 // N_DEV), (1, 0, 2))


def kernel(x, c, ctx, c_ctx, w_mod, b_mod, norm_w, w_in, ssm_conv_w, ssm_conv_b, dt_bias, a_log, d_skip, ssm_norm_w, w_out_ssm, conf_conv_w, conf_conv_b, conf_ln_w, conf_ln_b, w_out_conf, w_out, final_norm_w, loss_target, m_c_ctx, m_w_mod, m_b_mod, m_norm_w, m_w_in, m_ssm_conv_w, m_ssm_conv_b, m_dt_bias, m_a_log, m_d_skip, m_ssm_norm_w, m_w_out_ssm, m_conf_conv_w, m_conf_conv_b, m_conf_ln_w, m_conf_ln_b, m_w_out_conf, m_w_out, m_final_norm_w, v_c_ctx, v_w_mod, v_b_mod, v_norm_w, v_w_in, v_ssm_conv_w, v_ssm_conv_b, v_dt_bias, v_a_log, v_d_skip, v_ssm_norm_w, v_w_out_ssm, v_conf_conv_w, v_conf_conv_b, v_conf_ln_w, v_conf_ln_b, v_w_out_conf, v_w_out, v_final_norm_w):
    L = x.shape[1]
    Lc = ctx.shape[1]
    T = L + Lc
    nlx = L // RT
    nxc = L // Q
    x2 = x.reshape(L, D)
    ctx2 = ctx.reshape(Lc, D)
    tgt = loss_target.reshape(L, D)

    gathered = _gather2([w_in[0].astype(bf16), w_mod[0].astype(bf16), ssm_conv_w[0], conf_conv_w[0]], name="gather_weights")
    wp = _permute_w_in(_cols_gathered(gathered[0]))
    wmod_bf = _cols_gathered(gathered[1])
    scw8 = jnp.concatenate([_cols_gathered(gathered[2]), jnp.zeros((8 - SK, 4096), f32)], axis=0)
    ccw32 = jnp.concatenate([_cols_gathered(gathered[3]), jnp.zeros((32 - CK, D), f32)], axis=0)

    norm_w1 = norm_w.reshape(1, D)
    scb = ssm_conv_b.reshape(1, 4096)
    bias_row = jnp.concatenate([dt_bias.reshape(1, 2 * NH), jnp.zeros((1, 128 - 2 * NH), f32)], axis=1)
    alog_row = jnp.concatenate([a_log.reshape(1, 2 * NH), jnp.zeros((1, 128 - 2 * NH), f32)], axis=1)
    dskip_full = jnp.repeat(d_skip.reshape(NH), HD).reshape(1, DI)
    snw = ssm_norm_w.reshape(1, DI)
    ccb = conf_conv_b.reshape(1, D)
    lnw = conf_ln_w.reshape(1, D)
    lnb = conf_ln_b.reshape(1, D)
    fw = final_norm_w.reshape(1, D)

    cc8 = jnp.concatenate([c.reshape(1, D), c_ctx.reshape(1, D), jnp.zeros((6, D), f32)], axis=0)
    mod, silu_rows = _mod_fwd(cc8, wmod_bf, b_mod.reshape(1, 3 * D))
    h, h_t = _prenorm(x2, ctx2, norm_w1, mod)
    proj, wos_g, woc_g, wo_g = _matmul(
        h, wp, f32, "proj_gather", tn=NP // 5,
        comm=([w_out_ssm[0].astype(bf16), w_out_conf[0].astype(bf16), w_out[0].astype(bf16)], (False,) * 3))
    wos_bf = wos_g.reshape(DI, D)
    woc_bf = woc_g.reshape(D, D)
    wo_bf = wo_g.reshape(D, D)
    xs = _ssm_conv_fwd(proj, scw8, scb, nlx, 0, f32, "ssm_conv_fwd_x")
    bcm = _ssm_conv_fwd(proj, scw8, scb, nlx, 1, bf16, "ssm_conv_fwd_bc")
    dt, la = _dt_prep(proj, bias_row, alog_row)
    consts_f, consts_b = _scan_consts(False), _scan_consts(True)
    yf, hp_f = _ssd_fwd(xs, bcm, dt, la, consts_f, False, nxc, "ssd_fwd_f")
    ysum, hp_b = _ssd_fwd(xs, bcm, dt, la, consts_b, True, nxc, "ssd_fwd_b", y_acc=yf)
    y, yn, yn_t = _ynorm_fwd(ysum, xs, proj, dskip_full, snw, L)
    bs = _matmul(yn, wos_bf, f32, "branch_ssm", tm=1024, tk=2048)
    u1, u3, u3_t = _conf_fwd(proj, ccw32, ccb, lnw, lnb, L)
    bc = _matmul(u3, woc_bf, f32, "branch_conf", tm=2048)
    merged, merged_t = _merge_fwd(bs, bc, proj)
    out = _matmul(merged, wo_bf, f32, "out_proj", tm=2048)
    dx1, dout, loss_acc, dfw, dgate = _final(x2, out, tgt, mod, fw)

    dmerged = _matmul(dout, wo_bf, f32, "d_merged", tb=True, tm=2048)
    g_wo = _matmul(merged_t, dout, bf16, "g_w_out", tm=1024, tk=2048)
    dproj, dbs, dbc = _merge_bwd(dmerged, bs, bc, proj)
    dyn = _matmul(dbs, wos_bf, f32, "d_yn", tb=True, tm=1024, tn=2048)
    g_wos = _matmul(yn_t, dbs, bf16, "g_w_out_ssm", tm=1024, tk=2048)
    du3 = _matmul(dbc, woc_bf, f32, "d_u3", tb=True, tm=2048)
    g_woc = _matmul(u3_t, dbc, bf16, "g_w_out_conf", tm=1024, tk=2048)
    dproj, g_ccw, g_ccb, g_lnw, g_lnb = _conf_bwd(du3, u1, proj, ccw32, lnw, lnb, dproj)
    dproj, dy, g_snw, dsk_cols = _ynorm_bwd(dyn, y, xs, proj, dskip_full, snw, dproj)
    acc_f = _ssd_bwd(xs, bcm, dy, dt, la, hp_f, dskip_full, consts_f, False, nxc, "ssd_bwd_f")
    dxbc, a1, a2, r2, sv = _ssd_bwd(xs, bcm, dy, dt, la, hp_b, dskip_full, consts_b, True, nxc, "ssd_bwd_b", acc=acc_f)
    dproj, g_dtb, g_alog = _dt_bwd(a1, a2, r2, sv, dt, la, proj, bias_row, alog_row, dproj)
    dpre, g_scw, g_scb = _ssm_conv_dpre(dxbc, proj, scw8, scb, nlx)
    dproj = _ssm_conv_t(dpre, scw8, dproj, nlx)
    g_wp, *parts_b = _matmul(
        h_t, dproj, bf16, "g_w_in_scatter", tm=1024, tn=NP // 5,
        comm=([g_wos.reshape(N_DEV, DI // N_DEV, D), g_woc.reshape(N_DEV, D // N_DEV, D), g_wo.reshape(N_DEV, D // N_DEV, D),
               _cols_to_blocks(g_scw[:SK]), _cols_to_blocks(g_ccw[:CK])], (True,) * 5))
    dh, parts_a = _matmul(dproj, wp, f32, "d_h_scatter", tb=True, tk=NP // 5,
                          comm=([_cols_to_blocks(_unpermute_w_in(g_wp))], (True,)))
    parts = [parts_a] + parts_b
    gx, g_nw, macc = _prenorm_bwd(x2, ctx2, dh, dx1, norm_w1, mod)
    dmod_x = jnp.concatenate([macc[0:1], macc[1:2], dgate], axis=1)
    dmod_c = jnp.concatenate([macc[2:3], macc[3:4], jnp.zeros((1, D), f32)], axis=1)
    dmod8 = jnp.concatenate([dmod_x, dmod_c, jnp.zeros((6, 3 * D), f32)], axis=0)
    ct = jnp.concatenate([c.reshape(D, 1), c_ctx.reshape(D, 1), jnp.zeros((D, 126), f32)], axis=1)
    g_bmod, g_cctx = _mod_bwd(ct, dmod8, wmod_bf)
    g_dskip = _head_sums(dsk_cols.reshape(NH, HD))[:, 0]

    small_g = _pack_small({
        "c_ctx": g_cctx[:, 0], "b_mod": g_bmod, "norm_w": g_nw, "ssm_conv_b": g_scb, "dt_bias": g_dtb[0, :2 * NH],
        "a_log": g_alog[0, :2 * NH], "d_skip": g_dskip, "ssm_norm_w": g_snw, "conf_conv_b": g_ccb, "conf_ln_w": g_lnw,
        "conf_ln_b": g_lnb, "final_norm_w": dfw})
    fac = jnp.concatenate([silu_rows[0:1].reshape(D // 128, 128), dmod_x.reshape(3 * D // 128, 128),
                           dmod_c.reshape(3 * D // 128, 128), loss_acc], axis=0)
    small_parts, fac_all = _exchange([small_g, fac], (False, False), name="exchange_tail")
    nr = D // 128
    sct = jnp.concatenate([fac_all[:, 0:nr].reshape(N_DEV, D).T, silu_rows[1:2].T, jnp.zeros((D, 128 - N_DEV - 1), f32)], axis=1)
    my_cols = (4 * lax.axis_index("x") + 2 * lax.axis_index("y") + lax.axis_index("c")) * (3 * D // N_DEV)
    dmx_all = lax.dynamic_slice(fac_all[:, nr:4 * nr].reshape(N_DEV, 3 * D), (0, my_cols), (N_DEV, 3 * D // N_DEV))
    dmc_all = lax.dynamic_slice(fac_all[:, 4 * nr:7 * nr].reshape(N_DEV, 3 * D), (0, my_cols), (N_DEV, 3 * D // N_DEV))
    g_wmod, loss_row = _wmod_grad(sct, dmx_all, dmc_all, fac_all[:, 7 * nr])
    parts = [parts[0], g_wmod[None]] + parts[1:]

    given = dict(c_ctx=c_ctx, w_mod=w_mod, b_mod=b_mod, norm_w=norm_w, w_in=w_in, ssm_conv_w=ssm_conv_w, ssm_conv_b=ssm_conv_b,
                 dt_bias=dt_bias, a_log=a_log, d_skip=d_skip, ssm_norm_w=ssm_norm_w, w_out_ssm=w_out_ssm, conf_conv_w=conf_conv_w,
                 conf_conv_b=conf_conv_b, conf_ln_w=conf_ln_w, conf_ln_b=conf_ln_b, w_out_conf=w_out_conf, w_out=w_out,
                 final_norm_w=final_norm_w)
    ms = dict(c_ctx=m_c_ctx, w_mod=m_w_mod, b_mod=m_b_mod, norm_w=m_norm_w, w_in=m_w_in, ssm_conv_w=m_ssm_conv_w,
              ssm_conv_b=m_ssm_conv_b, dt_bias=m_dt_bias, a_log=m_a_log, d_skip=m_d_skip, ssm_norm_w=m_ssm_norm_w,
              w_out_ssm=m_w_out_ssm, conf_conv_w=m_conf_conv_w, conf_conv_b=m_conf_conv_b, conf_ln_w=m_conf_ln_w,
              conf_ln_b=m_conf_ln_b, w_out_conf=m_w_out_conf, w_out=m_w_out, final_norm_w=m_final_norm_w)
    vs = dict(c_ctx=v_c_ctx, w_mod=v_w_mod, b_mod=v_b_mod, norm_w=v_norm_w, w_in=v_w_in, ssm_conv_w=v_ssm_conv_w,
              ssm_conv_b=v_ssm_conv_b, dt_bias=v_dt_bias, a_log=v_a_log, d_skip=v_d_skip, ssm_norm_w=v_ssm_norm_w,
              w_out_ssm=v_w_out_ssm, conf_conv_w=v_conf_conv_w, conf_conv_b=v_conf_conv_b, conf_ln_w=v_conf_ln_w,
              conf_ln_b=v_conf_ln_b, w_out_conf=v_w_out_conf, w_out=v_w_out, final_norm_w=v_final_norm_w)
    grads, deltas, new_m, new_v = {}, {}, {}, {}
    sharded = ("w_in", "w_mod", "w_out_ssm", "w_out_conf", "w_out", "ssm_conv_w", "conf_conv_w")
    for i, nm in enumerate(sharded):
        shp = given[nm].shape
        w2 = given[nm].reshape(shp[1], shp[2])
        res = _adamw(parts[i], w2, ms[nm].reshape(w2.shape), vs[nm].reshape(w2.shape), "adamw_" + nm)
        grads[nm], deltas[nm], new_m[nm], new_v[nm] = [r.reshape(shp) for r in res]
    shapes = {nm: given[nm].shape for nm, _ in _SMALL}
    res = _adamw(small_parts, _pack_small(given), _pack_small(ms), _pack_small(vs), "adamw_small")
    for dst, packed in zip((grads, deltas, new_m, new_v), res):
        dst.update(_unpack_small(packed, shapes))

    loss = loss_row[0, 0]
    order = ("c_ctx", "w_mod", "b_mod", "norm_w", "w_in", "ssm_conv_w", "ssm_conv_b", "dt_bias", "a_log", "d_skip", "ssm_norm_w",
             "w_out_ssm", "conf_conv_w", "conf_conv_b", "conf_ln_w", "conf_ln_b", "w_out_conf", "w_out", "final_norm_w")
    return (loss, gx.reshape(1, L, D), *[grads[n] for n in order], *[deltas[n] for n in order],
            *[new_m[n] for n in order], *[new_v[n] for n in order])
```

```python
import jax
import jax.numpy as jnp
from jax import lax
from jax.experimental import pallas as pl
from jax.experimental.pallas import tpu as pltpu

f32 = jnp.float32
bf16 = jnp.bfloat16

D = 1024
DI = 2048
NG = 8
HPG = 4
HD = 64
GW = HPG * HD
NS = 128
NH = 32
Q = 128
GRID_W = 64
CK = 31
SK = 4
EPS = 1e-6
RT = 256
N_DEV = 8
IN_COLS = 11328
G10, G20, PBC0, PX0, GV0, GG0, CG0, Z0, DT0, NP = 0, 1024, 2048, 4096, 6144, 7168, 8192, 9216, 11264, 11520
CONV_CT = 2048
B0, C0 = 2048, 3072
VMEM_LIMIT = 50 * 1024 * 1024
NEG = -1e30

ADAM_LR, ADAM_B1, ADAM_B2, ADAM_EPS, ADAM_WD, ADAM_STEP = 0.001, 0.9, 0.999, 1e-08, 0.01, 10

MESH = pl.DeviceIdType.MESH
S = jax.ShapeDtypeStruct


def _params(*sem):
    return pltpu.CompilerParams(dimension_semantics=tuple(sem) if sem else None, vmem_limit_bytes=VMEM_LIMIT)


def _sig(x):
    return 1.0 / (1.0 + jnp.exp(-x))


def _silu(x):
    return x * _sig(x)


def _dsilu(x, s):
    return s * (1.0 + x * (1.0 - s))


def _dot(a, b):
    return jnp.dot(a, b, preferred_element_type=f32)


def _dot_nt(a, b):
    return lax.dot_general(a, b, (((1,), (1,)), ((), ())), preferred_element_type=f32)


def _dot_tn(a, b):
    return lax.dot_general(a, b, (((0,), (0,)), ((), ())), preferred_element_type=f32)


def _dot3(t_bf, v):
    v1 = v.astype(bf16)
    r1 = v - v1.astype(f32)
    v2 = r1.astype(bf16)
    v3 = (r1 - v2.astype(f32)).astype(bf16)
    return _dot(t_bf, v1) + _dot(t_bf, v2) + _dot(t_bf, v3)


def _pick(n, prefs):
    for p in prefs:
        if n % p == 0:
            return p
    return n


def _full(shape):
    nd = len(shape)
    return pl.BlockSpec(shape, lambda *_: (0,) * nd)


def _matmul(a, b, out_dtype, name, tm=None, tn=None, tk=None, tb=False, comm=None):
    m, k = a.shape
    n = b.shape[0] if tb else b.shape[1]
    tm = tm if tm and m % tm == 0 else _pick(m, (768, 512, 256, 128))
    tn = tn if tn and n % tn == 0 else _pick(n, (1024, 512, 256, 128))
    tk = tk if tk and k % tk == 0 else _pick(k, (1024, 768, 512, 256, 128))
    nk = k // tk
    gi, gj = m // tm, n // tn
    carrs, modes = comm if comm else ((), ())
    nc = len(carrs)

    def kern(*refs):
        a_ref, b_ref = refs[:2]
        cins = refs[2:2 + nc]
        o_ref = refs[2 + nc]
        couts = refs[3 + nc:3 + 2 * nc]
        acc_ref = refs[3 + 2 * nc]
        sems = refs[4 + 2 * nc:]
        i, j, kk = pl.program_id(0), pl.program_id(1), pl.program_id(2)
        if nc:
            @pl.when(jnp.logical_and(jnp.logical_and(i == 0, j == 0), kk == 0))
            def _():
                _xchg_start(cins, couts, *sems, modes)

        part = _dot_nt(a_ref[...], b_ref[...]) if tb else _dot(a_ref[...], b_ref[...])
        if nk == 1:
            o_ref[...] = part.astype(o_ref.dtype)
        else:
            @pl.when(kk == 0)
            def _():
                acc_ref[...] = part

            @pl.when(kk > 0)
            def _():
                acc_ref[...] += part

            @pl.when(kk == nk - 1)
            def _():
                o_ref[...] = acc_ref[...].astype(o_ref.dtype)

        if nc:
            @pl.when(jnp.logical_and(jnp.logical_and(i == gi - 1, j == gj - 1), kk == nk - 1))
            def _():
                _xchg_wait(cins, couts, *sems, modes)

    anyspec = pl.BlockSpec(memory_space=pl.ANY)
    bspec = pl.BlockSpec((tn, tk), lambda i, j, kk: (j, kk)) if tb else pl.BlockSpec((tk, tn), lambda i, j, kk: (kk, j))
    out_shape = (S((m, n), out_dtype),) + _xchg_out_shapes(carrs, modes)
    res = pl.pallas_call(
        kern, out_shape=out_shape, grid=(gi, gj, nk),
        in_specs=[pl.BlockSpec((tm, tk), lambda i, j, kk: (i, kk)), bspec] + [anyspec] * nc,
        out_specs=(pl.BlockSpec((tm, tn), lambda i, j, kk: (i, j)),) + (anyspec,) * nc,
        scratch_shapes=[pltpu.VMEM((tm, tn), f32)] + (_xchg_sems(nc) if nc else []),
        compiler_params=_params(*((("arbitrary",) * 3) if nc else ("parallel", "parallel", "arbitrary"))), name=name)(a, b, *carrs)
    return res if nc else res[0]


def _mod_fwd(cc8, w_mod_bf, b_mod):
    def kern(c_ref, w_ref, b_ref, o_ref, s_ref):
        s = _silu(c_ref[...])
        s_ref[...] = s
        o_ref[...] = _dot(s.astype(bf16), w_ref[...]) + b_ref[...]

    return pl.pallas_call(kern, out_shape=(S((8, 3 * D), f32), S((8, D), f32)), compiler_params=_params(),
                          name="mod_fwd")(cc8, w_mod_bf, b_mod)


def _mod_bwd(ct, dmod8, w_mod_bf):
    tc = 512
    nj = 3 * D // tc

    def kern(ct_ref, dm_ref, w_ref, db_ref, dc_ref):
        j = pl.program_id(0)
        cx = ct_ref[:, 1:2]
        sx = _sig(cx)
        dmc = dm_ref[1:2, :]
        db_ref[...] = dm_ref[0:1, :] + dmc
        t = jnp.sum(w_ref[...].astype(f32) * dmc.astype(bf16).astype(f32), axis=1, keepdims=True) * _dsilu(cx, sx)

        @pl.when(j == 0)
        def _():
            dc_ref[...] = jnp.zeros_like(dc_ref)

        dc_ref[...] += jnp.broadcast_to(t, (D, 128))

    return pl.pallas_call(
        kern, out_shape=(S((1, 3 * D), f32), S((D, 128), f32)), grid=(nj,),
        in_specs=[_full((D, 128)), pl.BlockSpec((8, tc), lambda j: (0, j)), pl.BlockSpec((D, tc), lambda j: (0, j))],
        out_specs=(pl.BlockSpec((1, tc), lambda j: (0, j)), _full((D, 128))),
        compiler_params=_params("arbitrary"), name="mod_bwd")(ct, dmod8, w_mod_bf)


def _wmod_grad(sct, dmx, dmc, losses):
    cols = dmx.shape[1]

    def kern(s_ref, dmx_ref, dmc_ref, l_ref, g_ref, lo_ref):
        dmc_sum = dmc_ref[0:1, :]
        lsum = l_ref[0:1, :]
        for d in range(1, N_DEV):
            dmc_sum = dmc_sum + dmc_ref[d:d + 1, :]
            lsum = lsum + l_ref[d:d + 1, :]
        lo_ref[...] = lsum
        g = s_ref[:, N_DEV:N_DEV + 1] * dmc_sum
        for d in range(N_DEV):
            g = g + s_ref[:, d:d + 1] * dmx_ref[d:d + 1, :]
        g_ref[...] = g

    return pl.pallas_call(kern, out_shape=(S((D, cols), f32), S((1, 128), f32)), compiler_params=_params(),
                          name="wmod_grad")(sct, dmx, dmc, losses)


def _prenorm(x, ctx, norm_w, mod):
    L, Lc = x.shape[0], ctx.shape[0]
    nlx, nt = L // RT, (L + Lc) // RT

    def kern(x_ref, c_ref, nw_ref, mod_ref, h_ref, ht_ref):
        i = pl.program_id(0)
        is_c = i >= nlx
        xv = jnp.where(is_c, c_ref[...], x_ref[...])
        shift = jnp.where(is_c, mod_ref[1:2, 0:D], mod_ref[0:1, 0:D])
        scale = jnp.where(is_c, mod_ref[1:2, D:2 * D], mod_ref[0:1, D:2 * D])
        r = lax.rsqrt(jnp.mean(xv * xv, axis=1, keepdims=True) + EPS)
        hv = (xv * r) * nw_ref[...] * (1.0 + scale) + shift
        h_ref[...] = hv.astype(bf16)
        ht_ref[...] = jnp.transpose(hv).astype(bf16)

    return pl.pallas_call(
        kern, out_shape=(S((L + Lc, D), bf16), S((D, L + Lc), bf16)), grid=(nt,),
        in_specs=[pl.BlockSpec((RT, D), lambda i: (jnp.minimum(i, nlx - 1), 0)),
                  pl.BlockSpec((RT, D), lambda i: (jnp.maximum(i - nlx, 0), 0)),
                  _full((1, D)), _full((8, 3 * D))],
        out_specs=(pl.BlockSpec((RT, D), lambda i: (i, 0)), pl.BlockSpec((D, RT), lambda i: (0, i))),
        compiler_params=_params("parallel"), name="prenorm")(x, ctx, norm_w, mod)


def _prenorm_bwd(x, ctx, dh, dx1, norm_w, mod):
    L, Lc = x.shape[0], ctx.shape[0]
    nlx, nt = L // RT, (L + Lc) // RT

    def kern(x_ref, c_ref, dh_ref, dx1_ref, nw_ref, mod_ref, gx_ref, dnw_ref, acc_ref):
        i = pl.program_id(0)
        is_c = i >= nlx

        @pl.when(i == 0)
        def _():
            dnw_ref[...] = jnp.zeros_like(dnw_ref)
            acc_ref[...] = jnp.zeros_like(acc_ref)

        xv = jnp.where(is_c, c_ref[...], x_ref[...])
        scale = jnp.where(is_c, mod_ref[1:2, D:2 * D], mod_ref[0:1, D:2 * D])
        nw = nw_ref[...]
        r = lax.rsqrt(jnp.mean(xv * xv, axis=1, keepdims=True) + EPS)
        xn = xv * r
        dh = dh_ref[...]
        dsh = jnp.sum(dh, axis=0, keepdims=True)
        dsc = jnp.sum(dh * (xn * nw), axis=0, keepdims=True)
        dxnw = dh * (1.0 + scale)
        dnw_ref[...] += jnp.sum(dxnw * xn, axis=0, keepdims=True)
        dxn = dxnw * nw
        dx = r * (dxn - xn * jnp.mean(dxn * xn, axis=1, keepdims=True))

        @pl.when(jnp.logical_not(is_c))
        def _():
            gx_ref[...] = dx1_ref[...] + dx
            acc_ref[0:1, :] += dsh
            acc_ref[1:2, :] += dsc

        @pl.when(is_c)
        def _():
            acc_ref[2:3, :] += dsh
            acc_ref[3:4, :] += dsc

    xmap = lambda i: (jnp.minimum(i, nlx - 1), 0)
    return pl.pallas_call(
        kern, out_shape=(S((L, D), f32), S((1, D), f32), S((8, D), f32)), grid=(nt,),
        in_specs=[pl.BlockSpec((RT, D), xmap), pl.BlockSpec((RT, D), lambda i: (jnp.maximum(i - nlx, 0), 0)),
                  pl.BlockSpec((RT, D), lambda i: (i, 0)), pl.BlockSpec((RT, D), xmap), _full((1, D)), _full((8, 3 * D))],
        out_specs=(pl.BlockSpec((RT, D), xmap), _full((1, D)), _full((8, D))),
        compiler_params=_params("arbitrary"), name="prenorm_bwd")(x, ctx, dh, dx1, norm_w, mod)


def _xbc_col(j):
    return jnp.where(j == 0, PX0 // CONV_CT, PBC0 // CONV_CT)


def _halo_specs(nt_rows, ct, col=lambda j: j):
    cur = pl.BlockSpec((RT, ct), lambda i, j: (i, col(j)))
    prev = pl.BlockSpec((8, ct), lambda i, j: (jnp.maximum(i * (RT // 8) - 1, 0), col(j)))
    nxt = pl.BlockSpec((8, ct), lambda i, j: (jnp.minimum((i + 1) * (RT // 8), nt_rows // 8 - 1), col(j)))
    return cur, prev, nxt


def _fill_halo(scr, cur_ref, prev_ref, next_ref, i, nlx, nt):
    prev_ok = jnp.logical_and(i != 0, i != nlx)
    next_ok = jnp.logical_and(i != nlx - 1, i != nt - 1)
    scr[0:8, :] = jnp.where(prev_ok, prev_ref[...], 0.0)
    scr[8:8 + RT, :] = cur_ref[...]
    scr[8 + RT:16 + RT, :] = jnp.where(next_ok, next_ref[...], 0.0)


CONV_RB = 32


def _conv_blocks(ct):
    return [(slice(cb * 128, (cb + 1) * 128), r0) for cb in range(ct // 128) for r0 in range(0, RT, CONV_RB)]


def _taps(scr, cs, r0, shifts):
    blk = scr[r0:r0 + CONV_RB + 16, cs]
    n = CONV_RB + 16
    return [(blk if d == 0 else pltpu.roll(blk, (-d) % n, 0))[8:8 + CONV_RB, :] for d in shifts]


def _ssm_conv_fwd(proj, w8, b, nlx, half, out_dtype, name):
    T = proj.shape[0]
    nt = T // RT
    ct = CONV_CT
    cur, prev, nxt = _halo_specs(T, ct, lambda j: _xbc_col(j + half))

    def kern(cur_ref, prev_ref, next_ref, w_ref, b_ref, o_ref, scr):
        i = pl.program_id(0)
        _fill_halo(scr, cur_ref, prev_ref, next_ref, i, nlx, nt)
        for cs, r0 in _conv_blocks(ct):
            taps = _taps(scr, cs, r0, [k - 2 for k in range(SK)])
            acc = jnp.broadcast_to(b_ref[:, cs], (CONV_RB, 128))
            for k in range(SK):
                acc = acc + w_ref[k:k + 1, cs] * taps[k]
            o_ref[r0:r0 + CONV_RB, cs] = _silu(acc).astype(out_dtype)

    return pl.pallas_call(
        kern, out_shape=S((T, ct), out_dtype), grid=(nt, 1),
        in_specs=[cur, prev, nxt, pl.BlockSpec((8, ct), lambda i, j: (0, j + half)),
                  pl.BlockSpec((1, ct), lambda i, j: (0, j + half))],
        out_specs=pl.BlockSpec((RT, ct), lambda i, j: (i, j)),
        scratch_shapes=[pltpu.VMEM((RT + 16, ct), f32)],
        compiler_params=_params("parallel", "parallel"), name=name)(proj, proj, proj, w8, b)


def _ssm_conv_dpre(dxbc, proj, w8, b, nlx):
    T = proj.shape[0]
    nt = T // RT
    ct = CONV_CT
    cur = pl.BlockSpec((RT, ct), lambda j, i: (i, j))
    pcur = pl.BlockSpec((RT, ct), lambda j, i: (i, _xbc_col(j)))
    prev = pl.BlockSpec((8, ct), lambda j, i: (jnp.maximum(i * (RT // 8) - 1, 0), _xbc_col(j)))
    nxt = pl.BlockSpec((8, ct), lambda j, i: (jnp.minimum((i + 1) * (RT // 8), T // 8 - 1), _xbc_col(j)))

    def kern(d_ref, cur_ref, prev_ref, next_ref, w_ref, b_ref, dpre_ref, dw_ref, db_ref, scr):
        i = pl.program_id(1)
        _fill_halo(scr, cur_ref, prev_ref, next_ref, i, nlx, nt)

        @pl.when(i == 0)
        def _():
            dw_ref[...] = jnp.zeros_like(dw_ref)
            db_ref[...] = jnp.zeros_like(db_ref)

        for cb in range(ct // 128):
            cs = slice(cb * 128, (cb + 1) * 128)
            db_acc = jnp.zeros((CONV_RB, 128), f32)
            dw_acc = [jnp.zeros((CONV_RB, 128), f32) for _ in range(SK)]
            for r0 in range(0, RT, CONV_RB):
                taps = _taps(scr, cs, r0, [k - 2 for k in range(SK)])
                pre = jnp.broadcast_to(b_ref[:, cs], (CONV_RB, 128))
                for k in range(SK):
                    pre = pre + w_ref[k:k + 1, cs] * taps[k]
                dpre = d_ref[r0:r0 + CONV_RB, cs] * _dsilu(pre, _sig(pre))
                dpre_ref[r0:r0 + CONV_RB, cs] = dpre
                db_acc = db_acc + dpre
                dw_acc = [dw_acc[k] + dpre * taps[k] for k in range(SK)]
            db_ref[:, cs] += jnp.sum(db_acc, axis=0, keepdims=True)
            for k in range(SK):
                dw_ref[k:k + 1, cs] += jnp.sum(dw_acc[k], axis=0, keepdims=True)

    return pl.pallas_call(
        kern, out_shape=(S((T, 4096), f32), S((8, 4096), f32), S((1, 4096), f32)), grid=(4096 // ct, nt),
        in_specs=[cur, pcur, prev, nxt, pl.BlockSpec((8, ct), lambda j, i: (0, j)), pl.BlockSpec((1, ct), lambda j, i: (0, j))],
        out_specs=(cur, pl.BlockSpec((8, ct), lambda j, i: (0, j)), pl.BlockSpec((1, ct), lambda j, i: (0, j))),
        scratch_shapes=[pltpu.VMEM((RT + 16, ct), f32)],
        compiler_params=_params("parallel", "arbitrary"), name="ssm_conv_dpre")(dxbc, proj, proj, proj, w8, b)


def _ssm_conv_t(dpre, w8, dproj, nlx):
    T = dpre.shape[0]
    nt = T // RT
    ct = CONV_CT
    cur, prev, nxt = _halo_specs(T, ct)

    def kern(cur_ref, prev_ref, next_ref, w_ref, _alias, o_ref, scr):
        i = pl.program_id(0)
        _fill_halo(scr, cur_ref, prev_ref, next_ref, i, nlx, nt)
        for cs, r0 in _conv_blocks(ct):
            taps = _taps(scr, cs, r0, [2 - k for k in range(SK)])
            acc = jnp.zeros((CONV_RB, 128), f32)
            for k in range(SK):
                acc = acc + w_ref[k:k + 1, cs] * taps[k]
            o_ref[r0:r0 + CONV_RB, cs] = acc.astype(bf16)

    return pl.pallas_call(
        kern, out_shape=S(dproj.shape, bf16), grid=(nt, 4096 // ct),
        in_specs=[cur, prev, nxt, pl.BlockSpec((8, ct), lambda i, j: (0, j)), pl.BlockSpec(memory_space=pl.ANY)],
        out_specs=pl.BlockSpec((RT, ct), lambda i, j: (i, _xbc_col(j))),
        scratch_shapes=[pltpu.VMEM((RT + 16, ct), f32)], input_output_aliases={4: 0},
        compiler_params=_params("parallel", "parallel"), name="ssm_conv_t")(dpre, dpre, dpre, w8, dproj)


DT_CH = 6


def _tri():
    li = lax.broadcasted_iota(jnp.int32, (Q, Q), 0)
    si = lax.broadcasted_iota(jnp.int32, (Q, Q), 1)
    return (si <= li).astype(bf16), (si >= li).astype(bf16)


def _dt_prep(proj, bias_row, alog_row):
    T = proj.shape[0]
    nch = T // Q
    assert nch % DT_CH == 0

    def kern(raw_ref, b_ref, al_ref, dt_ref, la_ref):
        lane = lax.broadcasted_iota(jnp.int32, (Q, 128), 1)
        a = jnp.where(lane[0:1, :] < 2 * NH, -jnp.exp(al_ref[...]), 0.0)
        tri, trit = _tri()
        for h in range(DT_CH):
            rows = slice(h * Q, (h + 1) * Q)
            v = raw_ref[rows, :] + b_ref[...]
            dt = jnp.maximum(v, 0.0) + jnp.log1p(jnp.exp(-jnp.abs(v)))
            da = dt * a
            dt_ref[rows, :] = dt
            la_ref[rows, :] = jnp.where(lane < NH, _dot3(tri, da), _dot3(trit, da))

    rq = DT_CH * Q
    return pl.pallas_call(
        kern, out_shape=(S((T, 128), f32), S((T, 128), f32)), grid=(nch // DT_CH,),
        in_specs=[pl.BlockSpec((rq, 128), lambda c: (c, DT0 // 128)), _full((1, 128)), _full((1, 128))],
        out_specs=(pl.BlockSpec((rq, 128), lambda c: (c, 0)), pl.BlockSpec((rq, 128), lambda c: (c, 0))),
        compiler_params=_params("parallel"), name="dt_prep")(proj, bias_row, alog_row)


def _dt_bwd(a1, a2, r2, sv, dt, la, proj, bias_row, alog_row, dproj):
    T = proj.shape[0]
    nch = T // Q
    assert nch % DT_CH == 0
    rq = DT_CH * Q
    blk = pl.BlockSpec((rq, 128), lambda c: (c, 0))

    def kern(a1_ref, a2_ref, r2_ref, s_ref, dt_ref, la_ref, raw_ref, b_ref, al_ref, _alias, o_ref, db_ref, dal_ref):
        c = pl.program_id(0)

        @pl.when(c == 0)
        def _():
            db_ref[...] = jnp.zeros_like(db_ref)
            dal_ref[...] = jnp.zeros_like(dal_ref)

        lane = lax.broadcasted_iota(jnp.int32, (Q, 128), 1)
        row = lax.broadcasted_iota(jnp.int32, (Q, 128), 0)
        fwd = lane < NH
        a = jnp.where(lane[0:1, :] < 2 * NH, -jnp.exp(al_ref[...]), 0.0)
        is_end = row == jnp.where(fwd, Q - 1, 0)
        tri, trit = _tri()
        o_ref[...] = jnp.zeros_like(o_ref)
        for h in range(DT_CH):
            rows = slice(h * Q, (h + 1) * Q)
            dt = dt_ref[rows, :]
            la = la_ref[rows, :]
            a2v = a2_ref[rows, :]
            r2v = r2_ref[rows, :]
            la_e = jnp.where(fwd[0:1, :], la[Q - 1:Q, :], la[0:1, :])
            e_end = jnp.exp(la_e - la)
            wend = e_end * dt
            extra = s_ref[h * Q:h * Q + 1, :] * jnp.exp(la_e) + jnp.sum(wend * a2v, axis=0, keepdims=True)
            dla = a1_ref[rows, :] - dt * r2v - wend * a2v + jnp.where(is_end, extra, 0.0)
            rcs = jnp.where(fwd, _dot3(trit, dla), _dot3(tri, dla))
            ddt = r2v + e_end * a2v + a * rcs
            dal_ref[...] += a * jnp.sum(dt * rcs, axis=0, keepdims=True)
            draw = jnp.where(lane < 2 * NH, ddt * _sig(raw_ref[rows, :] + b_ref[...]), 0.0)
            db_ref[...] += jnp.sum(draw, axis=0, keepdims=True)
            o_ref[rows, 0:128] = draw.astype(bf16)

    return pl.pallas_call(
        kern, out_shape=(S(dproj.shape, bf16), S((1, 128), f32), S((1, 128), f32)), grid=(nch // DT_CH,),
        in_specs=[blk, blk, blk, blk, blk, blk, pl.BlockSpec((rq, 128), lambda c: (c, DT0 // 128)),
                  _full((1, 128)), _full((1, 128)), pl.BlockSpec(memory_space=pl.ANY)],
        out_specs=(pl.BlockSpec((rq, NP - DT0), lambda c: (c, DT0 // (NP - DT0))), _full((1, 128)), _full((1, 128))),
        input_output_aliases={9: 0},
        compiler_params=_params("arbitrary"), name="dt_bwd")(a1, a2, r2, sv, dt, la, proj, bias_row, alog_row, dproj)


def _split2(v):
    hi = v.astype(bf16)
    lo = (v - hi.astype(f32)).astype(bf16)
    return jnp.concatenate([hi, lo], axis=1)


def _scan_consts(rev):
    hoff = NH if rev else 0
    g = jnp.arange(NG, dtype=jnp.int32)[:, None, None]

    def rc(nr, ncol):
        return jnp.arange(nr, dtype=jnp.int32)[None, :, None], jnp.arange(ncol, dtype=jnp.int32)[None, None, :]

    r, c = rc(2 * 128, HPG * HD)
    sel_w = (lax.rem(r, 128) == hoff + HPG * g + c // HD).astype(bf16)
    r, c = rc(HPG * HD, 128)
    ind_h = (c == hoff + HPG * g + r // HD).astype(bf16)
    return sel_w, ind_h


def _masks(rev):
    li = lax.broadcasted_iota(jnp.int32, (Q, Q), 0)
    si = lax.broadcasted_iota(jnp.int32, (Q, Q), 1)
    mask = (li <= si) if rev else (li >= si)
    mask_t = (li >= si) if rev else (li <= si)
    lane = lax.broadcasted_iota(jnp.int32, (Q, HPG * HD), 1)
    hms = [jnp.logical_and(lane >= r * HD, lane < (r + 1) * HD) for r in range(HPG)]
    return mask, mask_t, hms


def _mine(hoff):
    lane = lax.broadcasted_iota(jnp.int32, (Q, 128), 1)
    return jnp.logical_and(lane >= hoff, lane < hoff + NH)


def _head_row(vals, hc0):
    lane = lax.broadcasted_iota(jnp.int32, (1, HPG * HD), 1)
    out = jnp.zeros((1, HPG * HD), f32)
    for r in range(HPG):
        out = jnp.where(jnp.logical_and(lane >= r * HD, lane < (r + 1) * HD), vals[:, hc0 + r:hc0 + r + 1], out)
    return out


SCAN_CH = 2


def _chunk_of(j, rev, nxc, nch):
    return (nch - 1 - j) if rev else lax.rem(j + nxc, nch)


def _ssd_fwd(xs, bc, dt, la, consts, rev, nxc, name, y_acc=None):
    T = xs.shape[0]
    nch = T // Q
    hoff = NH if rev else 0
    e = 0 if rev else Q - 1
    cm = lambda j: _chunk_of(j, rev, nxc // SCAN_CH, nch // SCAN_CH)
    sel_w = consts[0]
    has_acc = y_acc is not None

    def kern(*refs):
        xs_ref, bc_ref, dt_ref, la_ref, sw_ref = refs[:5]
        yacc_ref = refs[5] if has_acc else None
        y_ref, hp_ref, h_ref = refs[5 + has_acc:]
        j = pl.program_id(0)

        @pl.when(j == 0)
        def _():
            h_ref[...] = jnp.zeros_like(h_ref)

        mask, _, hms = _masks(rev)
        for hh in range(SCAN_CH):
            h = SCAN_CH - 1 - hh if rev else hh
            chunk(refs, mask, hms, h, slice(h * Q, (h + 1) * Q))

    def chunk(refs, mask, hms, h, rows):
        xs_ref, bc_ref, dt_ref, la_ref, sw_ref = refs[:5]
        yacc_ref = refs[5] if has_acc else None
        y_ref, hp_ref, h_ref = refs[5 + has_acc:]
        hp_ref[h] = h_ref[...]
        la_all = la_ref[rows, :]
        dt_all = dt_ref[rows, :]
        la_t = jnp.transpose(la_all)
        dt_t = jnp.transpose(dt_all)
        la_e = la_all[e:e + 1, :]
        w2 = _split2(jnp.exp(jnp.where(_mine(hoff), la_e - la_all, 0.0)) * dt_all)
        e2 = _split2(jnp.exp(la_all))
        ela_e = jnp.exp(la_e)
        for g in range(NG):
            hc0 = hoff + g * HPG
            x = xs_ref[rows, g * GW:(g + 1) * GW]
            bb = bc_ref[rows, g * NS:(g + 1) * NS]
            cb = bc_ref[rows, NG * NS + g * NS:NG * NS + (g + 1) * NS]
            ht = h_ref[g * NS:(g + 1) * NS, :]
            scores = _dot_nt(cb, bb)
            yoff = _dot(cb, ht.astype(bf16))
            wend = _dot(w2, sw_ref[g])
            expla = _dot(e2, sw_ref[g])
            mixes, xstack = [], []
            for r in range(HPG):
                hc = hc0 + r
                la_rep = jnp.broadcast_to(la_all[:, hc:hc + 1], (Q, 128))
                decay = jnp.exp(jnp.where(mask, la_rep - la_t[hc:hc + 1, :], NEG))
                mixes.append((scores * decay * dt_t[hc:hc + 1, :]).astype(bf16))
                xstack.append(jnp.where(hms[r], x, 0.0).astype(bf16))
            y = _dot(jnp.concatenate(mixes, axis=1), jnp.concatenate(xstack, axis=0)) + yoff * expla
            if has_acc:
                y = y + yacc_ref[rows, g * GW:(g + 1) * GW]
            y_ref[rows, g * GW:(g + 1) * GW] = y
            h_ref[g * NS:(g + 1) * NS, :] = ht * _head_row(ela_e, hc0) + _dot_tn(bb, (x * wend).astype(bf16))

    row = lambda j: (cm(j), 0)
    rq = SCAN_CH * Q
    yblk = pl.BlockSpec((rq, DI), row)
    return pl.pallas_call(
        kern, out_shape=(S((T, DI), f32), S((nch, NG * NS, HPG * HD), f32)), grid=(nch // SCAN_CH,),
        in_specs=[yblk, pl.BlockSpec((rq, 2 * NG * NS), row), pl.BlockSpec((rq, 128), row), pl.BlockSpec((rq, 128), row),
                  _full(sel_w.shape)] + ([yblk] if has_acc else []),
        out_specs=(yblk, pl.BlockSpec((SCAN_CH, NG * NS, HPG * HD), lambda j: (cm(j), 0, 0))),
        scratch_shapes=[pltpu.VMEM((NG * NS, HPG * HD), f32)],
        input_output_aliases={5: 0} if has_acc else {},
        compiler_params=_params("arbitrary"), name=name)(xs, bc, dt, la, sel_w, *([y_acc] if has_acc else []))


def _ssd_bwd(xs, bc, dy, dt, la, hprev, dskip_full, consts, rev, nxc, name, acc=None):
    T = xs.shape[0]
    nch = T // Q
    hoff = NH if rev else 0
    e = 0 if rev else Q - 1
    npair = nch // SCAN_CH
    cm = lambda j: _chunk_of(npair - 1 - j, rev, nxc // SCAN_CH, npair)
    has_acc = acc is not None
    sel_w, ind_h = consts

    def kern(*refs):
        g_ref = refs[-2]
        j = pl.program_id(0)

        @pl.when(j == 0)
        def _():
            g_ref[...] = jnp.zeros_like(g_ref)

        masks = _masks(rev)
        for hh in range(SCAN_CH):
            h = hh if rev else SCAN_CH - 1 - hh
            chunk(refs, masks, h, slice(h * Q, (h + 1) * Q))

    def chunk(refs, masks, h, rows):
        xs_ref, bc_ref, dy_ref, dt_ref, la_ref, hp_ref, dsk_ref, sw_ref, ih_ref = refs[:9]
        k = 9
        if has_acc:
            dxbc_in, a1_in, a2_in, r2_in, s_in = refs[k:k + 5]
            k += 5
        dxbc_ref, a1_ref, a2_ref, r2_ref, s_ref, g_ref, r2scr = refs[k:k + 7]
        mask, mask_t, hms = masks
        lane128 = lax.broadcasted_iota(jnp.int32, (Q, 128), 1)
        la_all = la_ref[rows, :]
        dt_all = dt_ref[rows, :]
        la_t = jnp.transpose(la_all)
        dt_t = jnp.transpose(dt_all)
        la_e = la_all[e:e + 1, :]
        w2 = _split2(jnp.exp(jnp.where(_mine(hoff), la_e - la_all, 0.0)) * dt_all)
        e2 = _split2(jnp.exp(la_all))
        wed2 = jnp.concatenate([w2, e2, _split2(dt_all)], axis=0)
        ela_e = jnp.exp(la_e)
        r2scr[...] = jnp.zeros_like(r2scr)
        a1acc = jnp.zeros((Q, 128), f32)
        a2acc = jnp.zeros((Q, 128), f32)
        sacc = jnp.zeros((1, 128), f32)
        for g in range(NG):
            hc0 = hoff + g * HPG
            x = xs_ref[rows, g * GW:(g + 1) * GW]
            bb = bc_ref[rows, g * NS:(g + 1) * NS]
            cb = bc_ref[rows, NG * NS + g * NS:NG * NS + (g + 1) * NS]
            dyv = dy_ref[rows, g * GW:(g + 1) * GW]
            gt = g_ref[g * NS:(g + 1) * NS, :]
            ht = hp_ref[h, g * NS:(g + 1) * NS, :]
            gtb = gt.astype(bf16)
            htb = ht.astype(bf16)
            xb = x.astype(bf16)
            scores = _dot_nt(cb, bb)
            scores_t = _dot_nt(bb, cb)
            bg = _dot(bb, gtb)
            yoff = _dot(cb, htb)
            sel3 = _dot(wed2, sw_ref[g])
            wend, expla, dtf = sel3[0:Q], sel3[Q:2 * Q], sel3[2 * Q:3 * Q]
            dym = jnp.concatenate([jnp.where(hms[r], dyv, 0.0).astype(bf16) for r in range(HPG)], axis=0)
            dyx_all = _dot_nt(dym, xb)
            sdts, ems = [], []
            wsum = jnp.zeros((Q, Q), f32)
            for r in range(HPG):
                hc = hc0 + r
                la_rep = jnp.broadcast_to(la_all[:, hc:hc + 1], (Q, 128))
                la_r = la_t[hc:hc + 1, :]
                dt_r = dt_t[hc:hc + 1, :]
                decay = jnp.exp(jnp.where(mask, la_rep - la_r, NEG))
                decay_t = jnp.exp(jnp.where(mask_t, la_r - la_rep, NEG))
                dyx = dyx_all[r * Q:(r + 1) * Q, :]
                fm = dyx * (scores * decay)
                r2scr[hc:hc + 1, :] = jnp.sum(fm, axis=0, keepdims=True)
                ems.append(fm * dt_r)
                wsum = wsum + dyx * decay * dt_r
                sdts.append((scores_t * decay_t).astype(bf16))
            dx = dtf * _dot(jnp.concatenate(sdts, axis=1), dym) + wend * bg
            if not has_acc:
                dx = dx + dsk_ref[:, g * GW:(g + 1) * GW] * dyv
            red3 = _dot(jnp.concatenate([(dyv * yoff * expla).astype(bf16), (x * bg).astype(bf16), (gt * ht).astype(bf16)],
                                        axis=0), ih_ref[g])
            a1acc = a1acc + red3[0:Q]
            for r in range(HPG):
                a1acc = jnp.where(lane128 == hc0 + r, a1acc + jnp.sum(ems[r], axis=1, keepdims=True), a1acc)
            a2acc = a2acc + red3[Q:2 * Q]
            sacc = sacc + jnp.sum(red3[2 * Q:3 * Q], axis=0, keepdims=True)
            wb = wsum.astype(bf16)
            dysb = (dyv * expla).astype(bf16)
            dc = _dot(wb, bb) + _dot_nt(dysb, htb)
            db = _dot_tn(wb, cb) + _dot_nt((x * wend).astype(bf16), gtb)
            g_ref[g * NS:(g + 1) * NS, :] = gt * _head_row(ela_e, hc0) + _dot_tn(cb, dysb)
            if has_acc:
                dx = dx + dxbc_in[rows, g * GW:(g + 1) * GW]
                db = db + dxbc_in[rows, B0 + g * NS:B0 + (g + 1) * NS]
                dc = dc + dxbc_in[rows, C0 + g * NS:C0 + (g + 1) * NS]
            dxbc_ref[rows, g * GW:(g + 1) * GW] = dx
            dxbc_ref[rows, B0 + g * NS:B0 + (g + 1) * NS] = db
            dxbc_ref[rows, C0 + g * NS:C0 + (g + 1) * NS] = dc
        r2c = jnp.transpose(r2scr[...])
        sc = jnp.broadcast_to(sacc, (Q, 128))
        if has_acc:
            a1acc = a1acc + a1_in[rows, :]
            a2acc = a2acc + a2_in[rows, :]
            r2c = r2c + r2_in[rows, :]
            sc = sc + s_in[rows, :]
        a1_ref[rows, :] = a1acc
        a2_ref[rows, :] = a2acc
        r2_ref[rows, :] = r2c
        s_ref[rows, :] = sc

    rq = SCAN_CH * Q
    blk = pl.BlockSpec((rq, 128), lambda j: (cm(j), 0))
    big = pl.BlockSpec((rq, 4096), lambda j: (cm(j), 0))
    wide = pl.BlockSpec((rq, DI), lambda j: (cm(j), 0))
    in_specs = [wide, pl.BlockSpec((rq, 2 * NG * NS), lambda j: (cm(j), 0)), wide, blk, blk,
                pl.BlockSpec((SCAN_CH, NG * NS, HPG * HD), lambda j: (cm(j), 0, 0)), _full((1, DI)),
                _full(sel_w.shape), _full(ind_h.shape)]
    args = [xs, bc, dy, dt, la, hprev, dskip_full, sel_w, ind_h]
    aliases = {}
    if has_acc:
        in_specs += [big, blk, blk, blk, blk]
        args += list(acc)
        aliases = {9: 0, 10: 1, 11: 2, 12: 3, 13: 4}
    return pl.pallas_call(
        kern, out_shape=(S((T, 4096), f32), S((T, 128), f32), S((T, 128), f32), S((T, 128), f32), S((T, 128), f32)),
        grid=(npair,), in_specs=in_specs, out_specs=(big, blk, blk, blk, blk),
        scratch_shapes=[pltpu.VMEM((NG * NS, HPG * HD), f32), pltpu.VMEM((128, Q), f32)],
        input_output_aliases=aliases,
        compiler_params=_params("arbitrary"), name=name)(*args)


def _ynorm_fwd(ysum, xs, proj, dskip_full, nw, L):
    nlx = L // RT

    def kern(ys_ref, xs_ref, za_ref, zb_ref, dsk_ref, nw_ref, y_ref, yn_ref, ynt_ref):
        y = ys_ref[...] + dsk_ref[...] * xs_ref[...]
        y_ref[...] = y
        hg = NG // 2
        for g in range(NG):
            z_ref = za_ref if g < hg else zb_ref
            sl = y[:, g * GW:(g + 1) * GW] * _silu(z_ref[:, (g % hg) * GW:(g % hg + 1) * GW])
            r = lax.rsqrt(jnp.mean(sl * sl, axis=1, keepdims=True) + EPS)
            yn = (sl * r) * nw_ref[:, g * GW:(g + 1) * GW]
            yn_ref[:, g * GW:(g + 1) * GW] = yn.astype(bf16)
            ynt_ref[g * GW:(g + 1) * GW, :] = jnp.transpose(yn).astype(bf16)

    blk = pl.BlockSpec((RT, DI), lambda i: (i, 0))
    return pl.pallas_call(
        kern, out_shape=(S((L, DI), f32), S((L, DI), bf16), S((DI, L), bf16)), grid=(nlx,),
        in_specs=[blk, blk, pl.BlockSpec((RT, DI // 2), lambda i: (i, Z0 // (DI // 2))),
                  pl.BlockSpec((RT, DI // 2), lambda i: (i, Z0 // (DI // 2) + 1)), _full((1, DI)), _full((1, DI))],
        out_specs=(blk, blk, pl.BlockSpec((DI, RT), lambda i: (0, i))),
        compiler_params=_params("parallel"), name="ynorm_fwd")(ysum, xs, proj, proj, dskip_full, nw)


def _ynorm_bwd(dyn, y, xs, proj, dskip_full, nw, dproj):
    L = y.shape[0]
    T = proj.shape[0]
    nlx, nt = L // RT, T // RT

    hw = DI // 2

    def kern(dyn_ref, y_ref, xs_ref, z_ref, dsk_ref, nw_ref, _alias, dz_ref, dy_ref, dnw_ref, dsk_acc):
        i = pl.program_id(1)

        @pl.when(i == 0)
        def _():
            dnw_ref[...] = jnp.zeros_like(dnw_ref)
            dsk_acc[...] = jnp.zeros_like(dsk_acc)

        @pl.when(i >= nlx)
        def _():
            dz_ref[...] = jnp.zeros_like(dz_ref)
            dy_ref[...] = jnp.zeros_like(dy_ref)

        @pl.when(i < nlx)
        def _():
            for g in range(hw // GW):
                cs = slice(g * GW, (g + 1) * GW)
                y = y_ref[:, cs]
                z = z_ref[:, cs]
                sz = _sig(z)
                gz = z * sz
                sl = y * gz
                r = lax.rsqrt(jnp.mean(sl * sl, axis=1, keepdims=True) + EPS)
                yhat = sl * r
                dn = dyn_ref[:, cs]
                dnw_ref[:, cs] += jnp.sum(dn * yhat, axis=0, keepdims=True)
                dyh = dn * nw_ref[:, cs]
                dyz = r * (dyh - yhat * jnp.mean(dyh * yhat, axis=1, keepdims=True))
                dyv = dyz * gz
                dy_ref[:, cs] = dyv
                dz_ref[:, cs] = (dyz * y * _dsilu(z, sz)).astype(bf16)
                dsk_acc[:, cs] += jnp.sum(dyv * xs_ref[:, cs], axis=0, keepdims=True)

    xblk = pl.BlockSpec((RT, hw), lambda j, i: (jnp.minimum(i, nlx - 1), j))
    row = pl.BlockSpec((1, hw), lambda j, i: (0, j))
    return pl.pallas_call(
        kern, out_shape=(S(dproj.shape, bf16), S((T, DI), f32), S((1, DI), f32), S((1, DI), f32)), grid=(2, nt),
        in_specs=[xblk, xblk, xblk, pl.BlockSpec((RT, hw), lambda j, i: (jnp.minimum(i, nlx - 1), Z0 // hw + j)), row, row,
                  pl.BlockSpec(memory_space=pl.ANY)],
        out_specs=(pl.BlockSpec((RT, hw), lambda j, i: (i, Z0 // hw + j)), pl.BlockSpec((RT, hw), lambda j, i: (i, j)), row, row),
        input_output_aliases={6: 0},
        compiler_params=_params("arbitrary", "arbitrary"), name="ynorm_bwd")(dyn, y, xs, proj, dskip_full, nw, dproj)


def _head_sums(cols):
    def kern(c_ref, o_ref):
        o_ref[...] = jnp.broadcast_to(jnp.sum(c_ref[...], axis=1, keepdims=True), (NH, 128))

    return pl.pallas_call(kern, out_shape=S((NH, 128), f32), name="head_sums")(cols)


SEG_STRIDE = 96
SEG_PAD = 16
NSEG = RT // GRID_W
CONF_ROWS = SEG_PAD + NSEG * SEG_STRIDE


SHIFT_ROWS = CONF_ROWS - 8
CONF_CW = 256


CONF_RB = 32


def _seg_zero_pads(scr):
    scr[0:SEG_PAD, :] = jnp.zeros((SEG_PAD, scr.shape[1]), f32)
    for s in range(NSEG):
        lo = SEG_PAD + s * SEG_STRIDE + GRID_W
        scr[lo:lo + SEG_STRIDE - GRID_W, :] = jnp.zeros((SEG_STRIDE - GRID_W, scr.shape[1]), f32)


def _seg_row(r0):
    return SEG_PAD + (r0 // GRID_W) * SEG_STRIDE + r0 % GRID_W


def _shift_copies(cps, scr, cs):
    full = scr[:, cs]
    for s in range(1, 8):
        cps[s - 1, :, :] = pltpu.roll(full, CONF_ROWS - s, 0)[0:SHIFT_ROWS, :]


def _tap(cps, scr, cs, o):
    rs = o % 8
    return scr[pl.ds(o, GRID_W), cs] if rs == 0 else cps[rs - 1, pl.ds(o - rs, GRID_W), :]


def _conf_fwd(proj, w32, cb, lnw, lnb, L):
    nlx = L // RT

    def kern(v_ref, g_ref, cg_ref, w_ref, cb_ref, lnw_ref, lnb_ref, u1_ref, u3_ref, u3t_ref, scr, cps, u3_scr):
        _seg_zero_pads(scr)
        for r0 in range(0, RT, CONF_RB):
            rows = slice(r0, r0 + CONF_RB)
            scr[_seg_row(r0):_seg_row(r0) + CONF_RB, :] = v_ref[rows, :] * _sig(g_ref[rows, :])
        for cc in range(D // CONF_CW):
            cs = slice(cc * CONF_CW, (cc + 1) * CONF_CW)
            _shift_copies(cps, scr, cs)
            for s in range(NSEG):
                acc = jnp.broadcast_to(cb_ref[:, cs], (GRID_W, CONF_CW))
                for k in range(CK):
                    acc = acc + w_ref[k:k + 1, cs] * _tap(cps, scr, cs, SEG_PAD + s * SEG_STRIDE + k - CK // 2)
                u1_ref[s * GRID_W:(s + 1) * GRID_W, cs] = acc
        for r0 in range(0, RT, CONF_RB):
            rows = slice(r0, r0 + CONF_RB)
            u1 = u1_ref[rows, :]
            xc = u1 - jnp.mean(u1, axis=1, keepdims=True)
            r = lax.rsqrt(jnp.mean(xc * xc, axis=1, keepdims=True) + EPS)
            u2 = (xc * r) * lnw_ref[...] + lnb_ref[...]
            u3 = _silu(u2) * _silu(cg_ref[rows, :])
            u3_ref[rows, :] = u3.astype(bf16)
            u3_scr[rows, :] = u3
        u3t_ref[...] = jnp.transpose(u3_scr[...]).astype(bf16)

    blk = pl.BlockSpec((RT, D), lambda i: (i, 0))
    return pl.pallas_call(
        kern, out_shape=(S((L, D), f32), S((L, D), bf16), S((D, L), bf16)), grid=(nlx,),
        in_specs=[pl.BlockSpec((RT, D), lambda i: (i, GV0 // D)), pl.BlockSpec((RT, D), lambda i: (i, GG0 // D)),
                  pl.BlockSpec((RT, D), lambda i: (i, CG0 // D)), _full((32, D)), _full((1, D)), _full((1, D)), _full((1, D))],
        out_specs=(blk, blk, pl.BlockSpec((D, RT), lambda i: (0, i))),
        scratch_shapes=[pltpu.VMEM((CONF_ROWS, D), f32), pltpu.VMEM((7, SHIFT_ROWS, CONF_CW), f32), pltpu.VMEM((RT, D), f32)],
        compiler_params=_params("parallel"), name="conf_fwd")(proj, proj, proj, w32, cb, lnw, lnb)


def _conf_bwd(du3, u1, proj, w32, lnw, lnb, dproj):
    L = u1.shape[0]
    T = proj.shape[0]
    nlx, nt = L // RT, T // RT

    def kern(du3_ref, u1_ref, v_ref, g_ref, cg_ref, w_ref, lnw_ref, lnb_ref, _alias,
             o_ref, dw_ref, dcb_ref, dlw_ref, dlb_ref, scr_u, scr_d, du0_scr, cps_u, cps_d):
        i = pl.program_id(0)

        @pl.when(i == 0)
        def _():
            dw_ref[...] = jnp.zeros_like(dw_ref)
            dcb_ref[...] = jnp.zeros_like(dcb_ref)
            dlw_ref[...] = jnp.zeros_like(dlw_ref)
            dlb_ref[...] = jnp.zeros_like(dlb_ref)

        @pl.when(i >= nlx)
        def _():
            o_ref[...] = jnp.zeros_like(o_ref)

        @pl.when(i < nlx)
        def _():
            _seg_zero_pads(scr_u)
            _seg_zero_pads(scr_d)
            for r0 in range(0, RT, CONF_RB):
                rows = slice(r0, r0 + CONF_RB)
                cg = cg_ref[rows, :]
                scg = _sig(cg)
                u1 = u1_ref[rows, :]
                xc = u1 - jnp.mean(u1, axis=1, keepdims=True)
                r = lax.rsqrt(jnp.mean(xc * xc, axis=1, keepdims=True) + EPS)
                xhat = xc * r
                u2 = xhat * lnw_ref[...] + lnb_ref[...]
                s2 = _sig(u2)
                du3v = du3_ref[rows, :]
                du2 = du3v * (cg * scg) * _dsilu(u2, s2)
                o_ref[rows, 2 * D:3 * D] = (du3v * (u2 * s2) * _dsilu(cg, scg)).astype(bf16)
                dlw_ref[...] += jnp.sum(du2 * xhat, axis=0, keepdims=True)
                dlb_ref[...] += jnp.sum(du2, axis=0, keepdims=True)
                dxh = du2 * lnw_ref[...]
                du1 = r * (dxh - jnp.mean(dxh, axis=1, keepdims=True) - xhat * jnp.mean(dxh * xhat, axis=1, keepdims=True))
                dcb_ref[...] += jnp.sum(du1, axis=0, keepdims=True)
                scr_u[_seg_row(r0):_seg_row(r0) + CONF_RB, :] = v_ref[rows, :] * _sig(g_ref[rows, :])
                scr_d[_seg_row(r0):_seg_row(r0) + CONF_RB, :] = du1
            for cc in range(D // CONF_CW):
                cs = slice(cc * CONF_CW, (cc + 1) * CONF_CW)
                _shift_copies(cps_u, scr_u, cs)
                _shift_copies(cps_d, scr_d, cs)
                for k in range(CK):
                    t = jnp.zeros((GRID_W, CONF_CW), f32)
                    for s in range(NSEG):
                        base = SEG_PAD + s * SEG_STRIDE
                        t = t + scr_d[pl.ds(base, GRID_W), cs] * _tap(cps_u, scr_u, cs, base + k - CK // 2)
                    dw_ref[k:k + 1, cs] += jnp.sum(t, axis=0, keepdims=True)
                for s in range(NSEG):
                    base = SEG_PAD + s * SEG_STRIDE
                    acc = jnp.zeros((GRID_W, CONF_CW), f32)
                    for k in range(CK):
                        acc = acc + w_ref[k:k + 1, cs] * _tap(cps_d, scr_d, cs, base + CK // 2 - k)
                    du0_scr[s * GRID_W:(s + 1) * GRID_W, cs] = acc
            for r0 in range(0, RT, CONF_RB):
                rows = slice(r0, r0 + CONF_RB)
                du0 = du0_scr[rows, :]
                sg = _sig(g_ref[rows, :])
                o_ref[rows, 0:D] = (du0 * sg).astype(bf16)
                o_ref[rows, D:2 * D] = (du0 * v_ref[rows, :] * sg * (1.0 - sg)).astype(bf16)

    xmap = lambda i: (jnp.minimum(i, nlx - 1), 0)
    pmap = lambda cb: (lambda i: (jnp.minimum(i, nlx - 1), cb))
    return pl.pallas_call(
        kern, out_shape=(S(dproj.shape, bf16), S((32, D), f32), S((1, D), f32), S((1, D), f32), S((1, D), f32)), grid=(nt,),
        in_specs=[pl.BlockSpec((RT, D), xmap), pl.BlockSpec((RT, D), xmap),
                  pl.BlockSpec((RT, D), pmap(GV0 // D)), pl.BlockSpec((RT, D), pmap(GG0 // D)), pl.BlockSpec((RT, D), pmap(CG0 // D)),
                  _full((32, D)), _full((1, D)), _full((1, D)), pl.BlockSpec(memory_space=pl.ANY)],
        out_specs=(pl.BlockSpec((RT, 3 * D), lambda i: (i, GV0 // (3 * D))), _full((32, D)), _full((1, D)), _full((1, D)), _full((1, D))),
        scratch_shapes=[pltpu.VMEM((CONF_ROWS, D), f32), pltpu.VMEM((CONF_ROWS, D), f32), pltpu.VMEM((RT, D), f32),
                        pltpu.VMEM((7, SHIFT_ROWS, CONF_CW), f32), pltpu.VMEM((7, SHIFT_ROWS, CONF_CW), f32)],
        input_output_aliases={8: 0},
        compiler_params=_params("arbitrary"), name="conf_bwd")(du3, u1, proj, proj, proj, w32, lnw, lnb, dproj)


def _merge_fwd(bs, bc, proj):
    L = bs.shape[0]

    def kern(bs_ref, bc_ref, g1_ref, g2_ref, o_ref, ot_ref):
        mv = _sig(g1_ref[...]) * bs_ref[...] + _sig(g2_ref[...]) * bc_ref[...]
        o_ref[...] = mv.astype(bf16)
        ot_ref[...] = jnp.transpose(mv).astype(bf16)

    rt = _pick(L, (2 * RT, RT))
    blk = pl.BlockSpec((rt, D), lambda i: (i, 0))
    return pl.pallas_call(
        kern, out_shape=(S((L, D), bf16), S((D, L), bf16)), grid=(L // rt,),
        in_specs=[blk, blk, pl.BlockSpec((rt, D), lambda i: (i, G10 // D)), pl.BlockSpec((rt, D), lambda i: (i, G20 // D))],
        out_specs=(blk, pl.BlockSpec((D, rt), lambda i: (0, i))),
        compiler_params=_params("parallel"), name="merge_fwd")(bs, bc, proj, proj)


def _merge_bwd(dmerged, bs, bc, proj):
    L = bs.shape[0]
    T = proj.shape[0]
    nlx, nt = L // RT, T // RT

    def kern(dm_ref, bs_ref, bc_ref, g1_ref, g2_ref, o_ref, dbs_ref, dbc_ref):
        i = pl.program_id(0)

        @pl.when(i >= nlx)
        def _():
            o_ref[...] = jnp.zeros_like(o_ref)

        @pl.when(i < nlx)
        def _():
            dm = dm_ref[...]
            s1 = _sig(g1_ref[...])
            s2 = _sig(g2_ref[...])
            dbs_ref[...] = (dm * s1).astype(bf16)
            dbc_ref[...] = (dm * s2).astype(bf16)
            o_ref[:, 0:D] = (dm * bs_ref[...] * s1 * (1.0 - s1)).astype(bf16)
            o_ref[:, D:2 * D] = (dm * bc_ref[...] * s2 * (1.0 - s2)).astype(bf16)

    xmap = lambda i: (jnp.minimum(i, nlx - 1), 0)
    pmap = lambda cb: (lambda i: (jnp.minimum(i, nlx - 1), cb))
    xblk = pl.BlockSpec((RT, D), xmap)
    return pl.pallas_call(
        kern, out_shape=(S((T, NP), bf16), S((L, D), bf16), S((L, D), bf16)), grid=(nt,),
        in_specs=[xblk, xblk, xblk, pl.BlockSpec((RT, D), pmap(G10 // D)), pl.BlockSpec((RT, D), pmap(G20 // D))],
        out_specs=(pl.BlockSpec((RT, 2 * D), lambda i: (i, G10 // (2 * D))), xblk, xblk),
        compiler_params=_params("arbitrary"), name="merge_bwd")(dmerged, bs, bc, proj, proj)


def _final(x, out, target, mod, fw):
    L = x.shape[0]

    def kern(x_ref, o_ref, t_ref, mod_ref, fw_ref, dx1_ref, dout_ref, loss_ref, dfw_ref, dg_ref):
        i = pl.program_id(0)

        @pl.when(i == 0)
        def _():
            loss_ref[...] = jnp.zeros_like(loss_ref)
            dfw_ref[...] = jnp.zeros_like(dfw_ref)
            dg_ref[...] = jnp.zeros_like(dg_ref)

        gate = mod_ref[0:1, 2 * D:3 * D]
        ov = o_ref[...]
        x1 = x_ref[...] + gate * ov
        r = lax.rsqrt(jnp.mean(x1 * x1, axis=1, keepdims=True) + EPS)
        xn = x1 * r
        fw = fw_ref[...]
        err = xn * fw - t_ref[...]
        part = 0.5 * jnp.sum(jnp.mean(err * err, axis=1, keepdims=True), axis=0, keepdims=True)
        loss_ref[...] += jnp.broadcast_to(part, (8, 128))
        dy = err * (1.0 / D)
        dfw_ref[...] += jnp.sum(dy * xn, axis=0, keepdims=True)
        dyw = dy * fw
        dx1 = r * (dyw - xn * jnp.mean(dyw * xn, axis=1, keepdims=True))
        dx1_ref[...] = dx1
        dout_ref[...] = (gate * dx1).astype(bf16)
        dg_ref[...] += jnp.sum(dx1 * ov, axis=0, keepdims=True)

    rt = _pick(L, (2 * RT, RT))
    blk = pl.BlockSpec((rt, D), lambda i: (i, 0))
    return pl.pallas_call(
        kern, out_shape=(S((L, D), f32), S((L, D), bf16), S((8, 128), f32), S((1, D), f32), S((1, D), f32)), grid=(L // rt,),
        in_specs=[blk, blk, blk, _full((8, 3 * D)), _full((1, D))],
        out_specs=(blk, blk, _full((8, 128)), _full((1, D)), _full((1, D))),
        compiler_params=_params("arbitrary"), name="final")(x, out, target, mod, fw)


def _me():
    return 4 * lax.axis_index("x") + 2 * lax.axis_index("y") + lax.axis_index("c")


def _xchg_copy(ins, outs, send_sems, recv_sems, modes, a, k, me):
    peer = lax.rem(me + k, N_DEV)
    pid = (peer // 4, lax.rem(peer // 2, 2), lax.rem(peer, 2))
    src = ins[a].at[peer] if modes[a] else ins[a]
    return pltpu.make_async_remote_copy(src_ref=src, dst_ref=outs[a].at[me], send_sem=send_sems.at[a, k - 1],
                                        recv_sem=recv_sems.at[a, k - 1], device_id=pid, device_id_type=MESH)


def _xchg_local(ins, outs, loc_sems, modes, a, me):
    return pltpu.make_async_copy(ins[a].at[me] if modes[a] else ins[a], outs[a].at[me], loc_sems.at[a])


def _xchg_start(ins, outs, send_sems, recv_sems, loc_sems, modes):
    me = _me()
    for a in range(len(modes)):
        _xchg_local(ins, outs, loc_sems, modes, a, me).start()
        for k in range(1, N_DEV):
            _xchg_copy(ins, outs, send_sems, recv_sems, modes, a, k, me).start()


def _xchg_wait(ins, outs, send_sems, recv_sems, loc_sems, modes):
    me = _me()
    for a in range(len(modes)):
        for k in range(1, N_DEV):
            frm = lax.rem(me + N_DEV - k, N_DEV)
            src = ins[a].at[frm] if modes[a] else ins[a]
            pltpu.make_async_remote_copy(src_ref=src, dst_ref=outs[a].at[frm], send_sem=send_sems.at[a, k - 1],
                                         recv_sem=recv_sems.at[a, k - 1], device_id=(0, 0, 0), device_id_type=MESH).wait_recv()
    for a in range(len(modes)):
        for k in range(1, N_DEV):
            _xchg_copy(ins, outs, send_sems, recv_sems, modes, a, k, me).wait_send()
        _xchg_local(ins, outs, loc_sems, modes, a, me).wait()


def _xchg_out_shapes(arrs, modes):
    return tuple(S((N_DEV,) + (a.shape[1:] if sc else a.shape), a.dtype) for a, sc in zip(arrs, modes))


def _xchg_sems(n):
    return [pltpu.SemaphoreType.DMA((n, N_DEV - 1)), pltpu.SemaphoreType.DMA((n, N_DEV - 1)), pltpu.SemaphoreType.DMA((n,))]


def _exchange(arrs, modes, name):
    n = len(arrs)

    def kern(*refs):
        ins, outs, sems = refs[:n], refs[n:2 * n], refs[2 * n:]
        _xchg_start(ins, outs, *sems, modes)
        _xchg_wait(ins, outs, *sems, modes)

    anyspec = pl.BlockSpec(memory_space=pl.ANY)
    return pl.pallas_call(
        kern, out_shape=_xchg_out_shapes(arrs, modes), in_specs=[anyspec] * n, out_specs=tuple([anyspec] * n),
        scratch_shapes=_xchg_sems(n), name=name)(*arrs)


def _gather2(arrs, name):
    n = len(arrs)

    def kern(*refs):
        ins, outs = refs[:n], refs[n:2 * n]
        send_sems, recv_sems, loc_sems = refs[2 * n:]
        x, y, c = lax.axis_index("x"), lax.axis_index("y"), lax.axis_index("c")
        me, sib = (x, y, c), (x, y, 1 - c)
        chips = [(1 - x, y), (x, 1 - y), (1 - x, 1 - y)]

        def slot(a, p):
            return outs[a].at[4 * p[0] + 2 * p[1] + p[2]]

        def cp(a, k, block, to, own=False):
            return pltpu.make_async_remote_copy(src_ref=ins[a] if own else slot(a, block), dst_ref=slot(a, block),
                                                send_sem=send_sems.at[a, k], recv_sem=recv_sems.at[a, k],
                                                device_id=to, device_id_type=MESH)

        started = []
        for a in range(n):
            pltpu.make_async_copy(ins[a], slot(a, me), loc_sems.at[a]).start()
            started.append(cp(a, 0, me, sib, own=True))
            started += [cp(a, 1 + j, me, (*chips[j], c), own=True) for j in range(2)]
        for s in started:
            s.start()
        for j in range(2):
            for a in range(n):
                cp(a, 1 + j, (*chips[j], c), me).wait_recv()
                fwd = cp(a, 4 + j, (*chips[j], c), sib)
                fwd.start()
                started.append(fwd)

            @pl.when(c == j)
            def _():
                for a in range(n):
                    cp(a, 3, (*chips[j], c), (*chips[1 - j], c)).start()
        for a in range(n):
            cp(a, 3, (*chips[2], c), me).wait_recv()
            fwd = cp(a, 6, (*chips[2], c), sib)
            fwd.start()
            started.append(fwd)
        for a in range(n):
            cp(a, 0, sib, me).wait_recv()
            for j in range(3):
                cp(a, 4 + j, (*chips[j], 1 - c), me).wait_recv()
        for s in started:
            s.wait_send()
        for a in range(n):
            cp(a, 3, me, me).wait_send()
            pltpu.make_async_copy(ins[a], slot(a, me), loc_sems.at[a]).wait()

    anyspec = pl.BlockSpec(memory_space=pl.ANY)
    return pl.pallas_call(
        kern, out_shape=_xchg_out_shapes(arrs, (False,) * n), in_specs=[anyspec] * n, out_specs=tuple([anyspec] * n),
        scratch_shapes=[pltpu.SemaphoreType.DMA((n, 7)), pltpu.SemaphoreType.DMA((n, 7)), pltpu.SemaphoreType.DMA((n,))],
        name=name)(*arrs)


def _adamw(parts, w, m, v, name):
    r, c = w.shape
    n_parts = parts.shape[0]
    tr = r
    for cand in (128, 64, 32, 16, 8):
        if r % cand == 0 and r > cand:
            tr = cand
            break
    c1 = 1.0 / (1.0 - ADAM_B1 ** ADAM_STEP)
    c2 = 1.0 / (1.0 - ADAM_B2 ** ADAM_STEP)

    def kern(p_ref, w_ref, m_ref, v_ref, g_ref, d_ref, m2_ref, v2_ref):
        g = p_ref[0].astype(f32)
        for i in range(1, n_parts):
            g = g + p_ref[i].astype(f32)
        g_ref[...] = g
        m2 = ADAM_B1 * m_ref[...] + (1.0 - ADAM_B1) * g
        v2 = ADAM_B2 * v_ref[...] + (1.0 - ADAM_B2) * (g * g)
        m2_ref[...] = m2
        v2_ref[...] = v2
        d_ref[...] = -ADAM_LR * ((m2 * c1) / (jnp.sqrt(v2 * c2) + ADAM_EPS) + ADAM_WD * w_ref[...])

    blk = pl.BlockSpec((tr, c), lambda i: (i, 0))
    sh = S((r, c), f32)
    return pl.pallas_call(
        kern, out_shape=(sh, sh, sh, sh), grid=(r // tr,),
        in_specs=[pl.BlockSpec((n_parts, tr, c), lambda i: (0, i, 0)), blk, blk, blk], out_specs=(blk, blk, blk, blk),
        compiler_params=_params("parallel"), name=name)(parts, w, m, v)


_SMALL = (("c_ctx", 1024), ("b_mod", 3072), ("norm_w", 1024), ("ssm_conv_b", 4096), ("dt_bias", 64), ("a_log", 64),
          ("d_skip", 32), ("ssm_norm_w", 2048), ("conf_conv_b", 1024), ("conf_ln_w", 1024), ("conf_ln_b", 1024),
          ("final_norm_w", 1024))
SMALL_TILE = 8 * 128


def _pack_small(d):
    rows = []
    for name, n in _SMALL:
        v = d[name].reshape(-1).astype(f32)
        pad = (-n) % SMALL_TILE
        if pad:
            v = jnp.concatenate([v, jnp.zeros((pad,), f32)])
        rows.append(v.reshape(-1, 128))
    return jnp.concatenate(rows, axis=0)


def _unpack_small(p, shapes):
    out, r0 = {}, 0
    for name, n in _SMALL:
        nr = 8 * ((n + SMALL_TILE - 1) // SMALL_TILE)
        out[name] = p[r0:r0 + nr].reshape(-1)[:n].reshape(shapes[name])
        r0 += nr
    return out


def _permute_w_in(w):
    return jnp.concatenate([w[:, 9280:11328], w[:, 2048:4096], w[:, 0:2048], w[:, 6208:9280], w[:, 4160:6208],
                            w[:, 4096:4160], jnp.zeros((w.shape[0], NP - DT0 - 64), w.dtype)], axis=1)


def _unpermute_w_in(wp):
    return jnp.concatenate([wp[:, PX0:PX0 + 2048], wp[:, PBC0:PBC0 + 2048], wp[:, DT0:DT0 + 64], wp[:, Z0:Z0 + 2048],
                            wp[:, GV0:GV0 + 3072], wp[:, G10:G10 + 2048]], axis=1)


def _cols_gathered(g):
    return jnp.transpose(g, (1, 0, 2)).reshape(g.shape[1], N_DEV * g.shape[2])


def _cols_to_blocks(a):
    r, c8 = a.shape
    return jnp.transpose(a.reshape(r, N_DEV, c8 // N_DEV), (1, 0, 2))


def kernel(x, c, ctx, c_ctx, w_mod, b_mod, norm_w, w_in, ssm_conv_w, ssm_conv_b, dt_bias, a_log, d_skip, ssm_norm_w, w_out_ssm, conf_conv_w, conf_conv_b, conf_ln_w, conf_ln_b, w_out_conf, w_out, final_norm_w, loss_target, m_c_ctx, m_w_mod, m_b_mod, m_norm_w, m_w_in, m_ssm_conv_w, m_ssm_conv_b, m_dt_bias, m_a_log, m_d_skip, m_ssm_norm_w, m_w_out_ssm, m_conf_conv_w, m_conf_conv_b, m_conf_ln_w, m_conf_ln_b, m_w_out_conf, m_w_out, m_final_norm_w, v_c_ctx, v_w_mod, v_b_mod, v_norm_w, v_w_in, v_ssm_conv_w, v_ssm_conv_b, v_dt_bias, v_a_log, v_d_skip, v_ssm_norm_w, v_w_out_ssm, v_conf_conv_w, v_conf_conv_b, v_conf_ln_w, v_conf_ln_b, v_w_out_conf, v_w_out, v_final_norm_w):
    L = x.shape[1]
    Lc = ctx.shape[1]
    T = L + Lc
    nlx = L // RT
    nxc = L // Q
    x2 = x.reshape(L, D)
    ctx2 = ctx.reshape(Lc, D)
    tgt = loss_target.reshape(L, D)

    gathered = _gather2([w_in[0].astype(bf16), w_mod[0].astype(bf16), ssm_conv_w[0], conf_conv_w[0]], name="gather_weights")
    wp = _permute_w_in(_cols_gathered(gathered[0]))
    wmod_bf = _cols_gathered(gathered[1])
    scw8 = jnp.concatenate([_cols_gathered(gathered[2]), jnp.zeros((8 - SK, 4096), f32)], axis=0)
    ccw32 = jnp.concatenate([_cols_gathered(gathered[3]), jnp.zeros((32 - CK, D), f32)], axis=0)

    norm_w1 = norm_w.reshape(1, D)
    scb = ssm_conv_b.reshape(1, 4096)
    bias_row = jnp.concatenate([dt_bias.reshape(1, 2 * NH), jnp.zeros((1, 128 - 2 * NH), f32)], axis=1)
    alog_row = jnp.concatenate([a_log.reshape(1, 2 * NH), jnp.zeros((1, 128 - 2 * NH), f32)], axis=1)
    dskip_full = jnp.repeat(d_skip.reshape(NH), HD).reshape(1, DI)
    snw = ssm_norm_w.reshape(1, DI)
    ccb = conf_conv_b.reshape(1, D)
    lnw = conf_ln_w.reshape(1, D)
    lnb = conf_ln_b.reshape(1, D)
    fw = final_norm_w.reshape(1, D)

    cc8 = jnp.concatenate([c.reshape(1, D), c_ctx.reshape(1, D), jnp.zeros((6, D), f32)], axis=0)
    mod, silu_rows = _mod_fwd(cc8, wmod_bf, b_mod.reshape(1, 3 * D))
    h, h_t = _prenorm(x2, ctx2, norm_w1, mod)
    proj, wos_g, woc_g, wo_g = _matmul(
        h, wp, f32, "proj_gather", tn=NP // 5,
        comm=([w_out_ssm[0].astype(bf16), w_out_conf[0].astype(bf16), w_out[0].astype(bf16)], (False,) * 3))
    wos_bf = wos_g.reshape(DI, D)
    woc_bf = woc_g.reshape(D, D)
    wo_bf = wo_g.reshape(D, D)
    xs = _ssm_conv_fwd(proj, scw8, scb, nlx, 0, f32, "ssm_conv_fwd_x")
    bcm = _ssm_conv_fwd(proj, scw8, scb, nlx, 1, bf16, "ssm_conv_fwd_bc")
    dt, la = _dt_prep(proj, bias_row, alog_row)
    consts_f, consts_b = _scan_consts(False), _scan_consts(True)
    yf, hp_f = _ssd_fwd(xs, bcm, dt, la, consts_f, False, nxc, "ssd_fwd_f")
    ysum, hp_b = _ssd_fwd(xs, bcm, dt, la, consts_b, True, nxc, "ssd_fwd_b", y_acc=yf)
    y, yn, yn_t = _ynorm_fwd(ysum, xs, proj, dskip_full, snw, L)
    bs = _matmul(yn, wos_bf, f32, "branch_ssm", tm=1024, tk=2048)
    u1, u3, u3_t = _conf_fwd(proj, ccw32, ccb, lnw, lnb, L)
    bc = _matmul(u3, woc_bf, f32, "branch_conf", tm=2048)
    merged, merged_t = _merge_fwd(bs, bc, proj)
    out = _matmul(merged, wo_bf, f32, "out_proj", tm=2048)
    dx1, dout, loss_acc, dfw, dgate = _final(x2, out, tgt, mod, fw)

    dmerged = _matmul(dout, wo_bf, f32, "d_merged", tb=True, tm=2048)
    g_wo = _matmul(merged_t, dout, bf16, "g_w_out", tm=1024, tk=2048)
    dproj, dbs, dbc = _merge_bwd(dmerged, bs, bc, proj)
    dyn = _matmul(dbs, wos_bf, f32, "d_yn", tb=True, tm=1024, tn=2048)
    g_wos = _matmul(yn_t, dbs, bf16, "g_w_out_ssm", tm=1024, tk=2048)
    du3 = _matmul(dbc, woc_bf, f32, "d_u3", tb=True, tm=2048)
    g_woc = _matmul(u3_t, dbc, bf16, "g_w_out_conf", tm=1024, tk=2048)
    dproj, g_ccw, g_ccb, g_lnw, g_lnb = _conf_bwd(du3, u1, proj, ccw32, lnw, lnb, dproj)
    dproj, dy, g_snw, dsk_cols = _ynorm_bwd(dyn, y, xs, proj, dskip_full, snw, dproj)
    acc_f = _ssd_bwd(xs, bcm, dy, dt, la, hp_f, dskip_full, consts_f, False, nxc, "ssd_bwd_f")
    dxbc, a1, a2, r2, sv = _ssd_bwd(xs, bcm, dy, dt, la, hp_b, dskip_full, consts_b, True, nxc, "ssd_bwd_b", acc=acc_f)
    dproj, g_dtb, g_alog = _dt_bwd(a1, a2, r2, sv, dt, la, proj, bias_row, alog_row, dproj)
    dpre, g_scw, g_scb = _ssm_conv_dpre(dxbc, proj, scw8, scb, nlx)
    dproj = _ssm_conv_t(dpre, scw8, dproj, nlx)
    g_wp, *parts_b = _matmul(
        h_t, dproj, bf16, "g_w_in_scatter", tm=1024, tn=NP // 5,
        comm=([g_wos.reshape(N_DEV, DI // N_DEV, D), g_woc.reshape(N_DEV, D // N_DEV, D), g_wo.reshape(N_DEV, D // N_DEV, D),
               _cols_to_blocks(g_scw[:SK]), _cols_to_blocks(g_ccw[:CK])], (True,) * 5))
    dh, parts_a = _matmul(dproj, wp, f32, "d_h_scatter", tb=True, tk=NP // 5,
                          comm=([_cols_to_blocks(_unpermute_w_in(g_wp))], (True,)))
    parts = [parts_a] + parts_b
    gx, g_nw, macc = _prenorm_bwd(x2, ctx2, dh, dx1, norm_w1, mod)
    dmod_x = jnp.concatenate([macc[0:1], macc[1:2], dgate], axis=1)
    dmod_c = jnp.concatenate([macc[2:3], macc[3:4], jnp.zeros((1, D), f32)], axis=1)
    dmod8 = jnp.concatenate([dmod_x, dmod_c, jnp.zeros((6, 3 * D), f32)], axis=0)
    ct = jnp.concatenate([c.reshape(D, 1), c_ctx.reshape(D, 1), jnp.zeros((D, 126), f32)], axis=1)
    g_bmod, g_cctx = _mod_bwd(ct, dmod8, wmod_bf)
    g_dskip = _head_sums(dsk_cols.reshape(NH, HD))[:, 0]

    small_g = _pack_small({
        "c_ctx": g_cctx[:, 0], "b_mod": g_bmod, "norm_w": g_nw, "ssm_conv_b": g_scb, "dt_bias": g_dtb[0, :2 * NH],
        "a_log": g_alog[0, :2 * NH], "d_skip": g_dskip, "ssm_norm_w": g_snw, "conf_conv_b": g_ccb, "conf_ln_w": g_lnw,
        "conf_ln_b": g_lnb, "final_norm_w": dfw})
    fac = jnp.concatenate([silu_rows[0:1].reshape(D // 128, 128), dmod_x.reshape(3 * D // 128, 128),
                           dmod_c.reshape(3 * D // 128, 128), loss_acc], axis=0)
    small_parts, fac_all = _exchange([small_g, fac], (False, False), name="exchange_tail")
    nr = D // 128
    sct = jnp.concatenate([fac_all[:, 0:nr].reshape(N_DEV, D).T, silu_rows[1:2].T, jnp.zeros((D, 128 - N_DEV - 1), f32)], axis=1)
    my_cols = (4 * lax.axis_index("x") + 2 * lax.axis_index("y") + lax.axis_index("c")) * (3 * D // N_DEV)
    dmx_all = lax.dynamic_slice(fac_all[:, nr:4 * nr].reshape(N_DEV, 3 * D), (0, my_cols), (N_DEV, 3 * D // N_DEV))
    dmc_all = lax.dynamic_slice(fac_all[:, 4 * nr:7 * nr].reshape(N_DEV, 3 * D), (0, my_cols), (N_DEV, 3 * D // N_DEV))
    g_wmod, loss_row = _wmod_grad(sct, dmx_all, dmc_all, fac_all[:, 7 * nr])
    parts = [parts[0], g_wmod[None]] + parts[1:]

    given = dict(c_ctx=c_ctx, w_mod=w_mod, b_mod=b_mod, norm_w=norm_w, w_in=w_in, ssm_conv_w=ssm_conv_w, ssm_conv_b=ssm_conv_b,
                 dt_bias=dt_bias, a_log=a_log, d_skip=d_skip, ssm_norm_w=ssm_norm_w, w_out_ssm=w_out_ssm, conf_conv_w=conf_conv_w,
                 conf_conv_b=conf_conv_b, conf_ln_w=conf_ln_w, conf_ln_b=conf_ln_b, w_out_conf=w_out_conf, w_out=w_out,
                 final_norm_w=final_norm_w)
    ms = dict(c_ctx=m_c_ctx, w_mod=m_w_mod, b_mod=m_b_mod, norm_w=m_norm_w, w_in=m_w_in, ssm_conv_w=m_ssm_conv_w,
              ssm_conv_b=m_ssm_conv_b, dt_bias=m_dt_bias, a_log=m_a_log, d_skip=m_d_skip, ssm_norm_w=m_ssm_norm_w,
              w_out_ssm=m_w_out_ssm, conf_conv_w=m_conf_conv_w, conf_conv_b=m_conf_conv_b, conf_ln_w=m_conf_ln_w,
              conf_ln_b=m_conf_ln_b, w_out_conf=m_w_out_conf, w_out=m_w_out, final_norm_w=m_final_norm_w)
    vs = dict(c_ctx=v_c_ctx, w_mod=v_w_mod, b_mod=v_b_mod, norm_w=v_norm_w, w_in=v_w_in, ssm_conv_w=v_ssm_conv_w,
              ssm_conv_b=v_ssm_conv_b, dt_bias=v_dt_bias, a_log=v_a_log, d_skip=v_d_skip, ssm_norm_w=v_ssm_norm_w,
              w_out_ssm=v_w_out_ssm, conf_conv_w=v_conf_conv_w, conf_conv_b=v_conf_conv_b, conf_ln_w=v_conf_ln_w,
              conf_ln_b=v_conf_ln_b, w_out_conf=v_w_out_conf, w_out=v_w_out, final_norm_w=v_final_norm_w)
    grads, deltas, new_m, new_v = {}, {}, {}, {}
    sharded = ("w_in", "w_mod", "w_out_ssm", "w_out_conf", "w_out", "ssm_conv_w", "conf_conv_w")
    for i, nm in enumerate(sharded):
        shp = given[nm].shape
        w2 = given[nm].reshape(shp[1], shp[2])
        res = _adamw(parts[i], w2, ms[nm].reshape(w2.shape), vs[nm].reshape(w2.shape), "adamw_" + nm)
        grads[nm], deltas[nm], new_m[nm], new_v[nm] = [r.reshape(shp) for r in res]
    shapes = {nm: given[nm].shape for nm, _ in _SMALL}
    res = _adamw(small_parts, _pack_small(given), _pack_small(ms), _pack_small(vs), "adamw_small")
    for dst, packed in zip((grads, deltas, new_m, new_v), res):
        dst.update(_unpack_small(packed, shapes))

    loss = loss_row[0, 0]
    order = ("c_ctx", "w_mod", "b_mod", "norm_w", "w_in", "ssm_conv_w", "ssm_conv_b", "dt_bias", "a_log", "d_skip", "ssm_norm_w",
             "w_out_ssm", "conf_conv_w", "conf_conv_b", "conf_ln_w", "conf_ln_b", "w_out_conf", "w_out", "final_norm_w")
    return (loss, gx.reshape(1, L, D), *[grads[n] for n in order], *[deltas[n] for n in order],
            *[new_m[n] for n in order], *[new_v[n] for n in order])
```
